```python
import math
import jax, jax.numpy as jnp
from jax import lax
import numpy as np

D_MODEL = 1024
BATCH = 8
SEQ = 8192
DEPTH = 1

HEAD_DIM = 64
SWA_Q_HEADS = 8
SWA_KV_HEADS = 2
SWA_GROUP = SWA_Q_HEADS // SWA_KV_HEADS
SWA_WINDOW = 128
SB_HEADS = 8
BLOCK = 128
REL_BUCKETS = 32
REL_MAX_DIST = 128
D_FF = 2816
N_BRANCH = 2
RMS_EPS = 1e-6
NEG_BIG = -1e30

SWA_Q_W = SWA_Q_HEADS * HEAD_DIM
SWA_KV_W = SWA_KV_HEADS * HEAD_DIM
SB_W = SB_HEADS * HEAD_DIM
IN_SIZES = (SWA_Q_W, SWA_KV_W, SWA_KV_W, SB_W, SB_W, SB_W, D_MODEL, D_MODEL)
IN_W = sum(IN_SIZES)
IN_SPLITS = tuple(int(v) for v in np.cumsum(IN_SIZES)[:-1])

kernel_name = 'hybrid_swa_sink_stickbreaking_macaron'


def rmsnorm(x, g):
    xf = x.astype(jnp.float32)
    y = xf * lax.rsqrt(jnp.mean(xf * xf, axis=-1, keepdims=True) + RMS_EPS) * g.astype(jnp.float32)
    return y.astype(x.dtype)


def swiglu(h, w1, w3, w2):
    return (jax.nn.silu(h @ w1) * (h @ w3)) @ w2


def rel_bucket(dist):
    max_exact = REL_BUCKETS // 2
    d = jnp.maximum(dist, 1).astype(jnp.float32)
    large = max_exact + (jnp.log(d / max_exact) / math.log(REL_MAX_DIST / max_exact)
                         * (REL_BUCKETS - max_exact)).astype(jnp.int32)
    large = jnp.minimum(large, REL_BUCKETS - 1)
    return jnp.where(dist < max_exact, dist, large)


def sliding_window_attention(q, k, v, sinks, rel_table):
    B, S = q.shape[0], q.shape[1]
    nb = S // BLOCK
    qb = q.astype(jnp.float32).reshape(B, nb, BLOCK, SWA_KV_HEADS, SWA_GROUP, HEAD_DIM)
    kb = k.astype(jnp.float32).reshape(B, nb, BLOCK, SWA_KV_HEADS, HEAD_DIM)
    vb = v.astype(jnp.float32).reshape(B, nb, BLOCK, SWA_KV_HEADS, HEAD_DIM)
    pad = ((0, 0), (1, 0), (0, 0), (0, 0), (0, 0))
    kw = jnp.concatenate([jnp.pad(kb, pad)[:, :-1], kb], axis=2)
    vw = jnp.concatenate([jnp.pad(vb, pad)[:, :-1], vb], axis=2)
    logits = jnp.einsum('bnqhgd,bnkhd->bnhgqk', qb, kw) * (HEAD_DIM ** -0.5)
    qi = jnp.arange(BLOCK)[:, None] + BLOCK
    kj = jnp.arange(2 * BLOCK)[None, :]
    dist = qi - kj
    band = (dist >= 0) & (dist < SWA_WINDOW)
    bias = rel_table.astype(jnp.float32)[rel_bucket(jnp.maximum(dist, 0))]
    bias = bias.transpose(2, 0, 1).reshape(SWA_KV_HEADS, SWA_GROUP, BLOCK, 2 * BLOCK)
    key_pos = jnp.arange(nb)[:, None] * BLOCK + jnp.arange(2 * BLOCK)[None, :] - BLOCK
    valid = band[None] & (key_pos >= 0)[:, None, :]
    logits = jnp.where(valid[None, :, None, None], logits + bias, NEG_BIG)
    sink = sinks.astype(jnp.float32).reshape(SWA_KV_HEADS, SWA_GROUP)[None, None, :, :, None, None]
    m = jnp.maximum(jnp.max(logits, axis=-1, keepdims=True), sink)
    p = jnp.exp(logits - m)
    p = p / (jnp.sum(p, axis=-1, keepdims=True) + jnp.exp(sink - m))
    o = jnp.einsum('bnhgqk,bnkhd->bnqhgd', p, vw)
    return o.reshape(B, S, SWA_Q_W).astype(q.dtype)


def stick_breaking_attention(q, k, v):
    B, S = q.shape[0], q.shape[1]
    nb = S // BLOCK
    qf = q.astype(jnp.float32).transpose(0, 2, 1, 3) * (HEAD_DIM ** -0.5)
    kf = k.astype(jnp.float32).transpose(0, 2, 1, 3)
    vf = v.astype(jnp.float32).transpose(0, 2, 1, 3)
    qblocks = qf.reshape(B, SB_HEADS, nb, BLOCK, HEAD_DIM).transpose(2, 0, 1, 3, 4)
    key_pos = jnp.arange(S)

    def one_block(args):
        q_blk, start = args
        z = jnp.einsum('bhqd,bhkd->bhqk', q_blk, kf)
        q_pos = start + jnp.arange(BLOCK)
        causal = key_pos[None, :] < q_pos[:, None]
        log_keep = jnp.where(causal, jax.nn.log_sigmoid(-z), 0.0)
        rev = lax.cumsum(log_keep, axis=3, reverse=True)
        between = jnp.concatenate([rev[..., 1:], jnp.zeros_like(rev[..., :1])], axis=-1)
        a = jnp.where(causal, jnp.exp(jax.nn.log_sigmoid(z) + between), 0.0)
        return jnp.einsum('bhqk,bhkd->bhqd', a, vf)

    o = lax.map(one_block, (qblocks, jnp.arange(nb) * BLOCK))
    return o.transpose(1, 0, 3, 2, 4).reshape(B, S, SB_W).astype(q.dtype)


def _fwd_setup_inputs(seed: int = 0) -> dict:
    key = jax.random.key(seed)
    ks = jax.random.split(key, 20)
    f32 = jnp.float32

    def w(k, shape, fan_in):
        return jax.random.normal(k, shape, f32) * (fan_in ** -0.5)

    def gain(k):
        return 1.0 + 0.02 * jax.random.normal(k, (DEPTH, D_MODEL), f32)

    return {
        'x': jax.random.normal(ks[0], (BATCH, SEQ, D_MODEL), f32),
        'norm_ffn1': gain(ks[1]),
        'ffn1_w1': w(ks[2], (DEPTH, D_MODEL, D_FF), D_MODEL),
        'ffn1_w3': w(ks[3], (DEPTH, D_MODEL, D_FF), D_MODEL),
        'ffn1_w2': w(ks[4], (DEPTH, D_FF, D_MODEL), D_FF),
        'norm_mix': gain(ks[5]),
        'w_in': w(ks[6], (DEPTH, D_MODEL, IN_W), D_MODEL),
        'swa_sinks': 0.5 * jax.random.normal(ks[7], (DEPTH, SWA_Q_HEADS), f32),
        'rel_bias': 0.5 * jax.random.normal(ks[8], (REL_BUCKETS, SWA_Q_HEADS), f32),
        'w_branch_swa': w(ks[9], (DEPTH, SWA_Q_W, D_MODEL), SWA_Q_W),
        'w_branch_sb': w(ks[10], (DEPTH, SB_W, D_MODEL), SB_W),
        'w_out': w(ks[11], (DEPTH, D_MODEL, D_MODEL), D_MODEL),
        'norm_ffn2': gain(ks[12]),
        'ffn2_w1': w(ks[13], (DEPTH, D_MODEL, D_FF), D_MODEL),
        'ffn2_w3': w(ks[14], (DEPTH, D_MODEL, D_FF), D_MODEL),
        'ffn2_w2': w(ks[15], (DEPTH, D_FF, D_MODEL), D_FF),
        'norm_final': 1.0 + 0.02 * jax.random.normal(ks[16], (D_MODEL,), f32),
    }


def _fwd_reference(x, norm_ffn1, ffn1_w1, ffn1_w3, ffn1_w2, norm_mix, w_in, swa_sinks, rel_bias,
              w_branch_swa, w_branch_sb, w_out, norm_ffn2, ffn2_w1, ffn2_w3, ffn2_w2, norm_final):
    B, S = x.shape[0], x.shape[1]
    for layer in range(DEPTH):
        h = rmsnorm(x, norm_ffn1[layer])
        x = x + 0.5 * swiglu(h, ffn1_w1[layer], ffn1_w3[layer], ffn1_w2[layer])
        h = rmsnorm(x, norm_mix[layer])
        proj = h @ w_in[layer]
        q_a, k_a, v_a, q_b, k_b, v_b, g_a, g_b = jnp.split(proj, IN_SPLITS, axis=-1)
        o_a = sliding_window_attention(
            q_a.reshape(B, S, SWA_Q_HEADS, HEAD_DIM),
            k_a.reshape(B, S, SWA_KV_HEADS, HEAD_DIM),
            v_a.reshape(B, S, SWA_KV_HEADS, HEAD_DIM),
            swa_sinks[layer], rel_bias)
        o_b = stick_breaking_attention(
            q_b.reshape(B, S, SB_HEADS, HEAD_DIM),
            k_b.reshape(B, S, SB_HEADS, HEAD_DIM),
            v_b.reshape(B, S, SB_HEADS, HEAD_DIM))
        merged = (jax.nn.sigmoid(g_a) * (o_a @ w_branch_swa[layer])
                  + jax.nn.sigmoid(g_b) * (o_b @ w_branch_sb[layer]))
        x = x + merged @ w_out[layer]
        h = rmsnorm(x, norm_ffn2[layer])
        x = x + 0.5 * swiglu(h, ffn2_w1[layer], ffn2_w3[layer], ffn2_w2[layer])
    return rmsnorm(x, norm_final)


import jax as _jax
import jax.numpy as _jnp

TWIN_FORMAT = 'train_step'
FWD_PARAMS = ['x', 'norm_ffn1', 'ffn1_w1', 'ffn1_w3', 'ffn1_w2', 'norm_mix', 'w_in', 'swa_sinks', 'rel_bias', 'w_branch_swa', 'w_branch_sb', 'w_out', 'norm_ffn2', 'ffn2_w1', 'ffn2_w3', 'ffn2_w2', 'norm_final']
TWIN_WEIGHTS = ['norm_ffn1', 'ffn1_w1', 'ffn1_w3', 'ffn1_w2', 'norm_mix', 'w_in', 'swa_sinks', 'rel_bias', 'w_branch_swa', 'w_branch_sb', 'w_out', 'norm_ffn2', 'ffn2_w1', 'ffn2_w3', 'ffn2_w2', 'norm_final']
TWIN_DIFF_INPUT = 'x'
TWIN_INPUTS = ['x', 'norm_ffn1', 'ffn1_w1', 'ffn1_w3', 'ffn1_w2', 'norm_mix', 'w_in', 'swa_sinks', 'rel_bias', 'w_branch_swa', 'w_branch_sb', 'w_out', 'norm_ffn2', 'ffn2_w1', 'ffn2_w3', 'ffn2_w2', 'norm_final', 'loss_target', 'm_norm_ffn1', 'm_ffn1_w1', 'm_ffn1_w3', 'm_ffn1_w2', 'm_norm_mix', 'm_w_in', 'm_swa_sinks', 'm_rel_bias', 'm_w_branch_swa', 'm_w_branch_sb', 'm_w_out', 'm_norm_ffn2', 'm_ffn2_w1', 'm_ffn2_w3', 'm_ffn2_w2', 'm_norm_final', 'v_norm_ffn1', 'v_ffn1_w1', 'v_ffn1_w3', 'v_ffn1_w2', 'v_norm_mix', 'v_w_in', 'v_swa_sinks', 'v_rel_bias', 'v_w_branch_swa', 'v_w_branch_sb', 'v_w_out', 'v_norm_ffn2', 'v_ffn2_w1', 'v_ffn2_w3', 'v_ffn2_w2', 'v_norm_final']
TWIN_OUTPUTS = ['loss', 'grad_x', 'grad_norm_ffn1', 'grad_ffn1_w1', 'grad_ffn1_w3', 'grad_ffn1_w2', 'grad_norm_mix', 'grad_w_in', 'grad_swa_sinks', 'grad_rel_bias', 'grad_w_branch_swa', 'grad_w_branch_sb', 'grad_w_out', 'grad_norm_ffn2', 'grad_ffn2_w1', 'grad_ffn2_w3', 'grad_ffn2_w2', 'grad_norm_final', 'delta_norm_ffn1', 'delta_ffn1_w1', 'delta_ffn1_w3', 'delta_ffn1_w2', 'delta_norm_mix', 'delta_w_in', 'delta_swa_sinks', 'delta_rel_bias', 'delta_w_branch_swa', 'delta_w_branch_sb', 'delta_w_out', 'delta_norm_ffn2', 'delta_ffn2_w1', 'delta_ffn2_w3', 'delta_ffn2_w2', 'delta_norm_final', 'new_m_norm_ffn1', 'new_m_ffn1_w1', 'new_m_ffn1_w3', 'new_m_ffn1_w2', 'new_m_norm_mix', 'new_m_w_in', 'new_m_swa_sinks', 'new_m_rel_bias', 'new_m_w_branch_swa', 'new_m_w_branch_sb', 'new_m_w_out', 'new_m_norm_ffn2', 'new_m_ffn2_w1', 'new_m_ffn2_w3', 'new_m_ffn2_w2', 'new_m_norm_final', 'new_v_norm_ffn1', 'new_v_ffn1_w1', 'new_v_ffn1_w3', 'new_v_ffn1_w2', 'new_v_norm_mix', 'new_v_w_in', 'new_v_swa_sinks', 'new_v_rel_bias', 'new_v_w_branch_swa', 'new_v_w_branch_sb', 'new_v_w_out', 'new_v_norm_ffn2', 'new_v_ffn2_w1', 'new_v_ffn2_w3', 'new_v_ffn2_w2', 'new_v_norm_final']
TWIN_LEAF_KINDS = {'loss': 'loss', 'grad_x': 'grad_x', 'grad_norm_ffn1': 'grad_w', 'grad_ffn1_w1': 'grad_w', 'grad_ffn1_w3': 'grad_w', 'grad_ffn1_w2': 'grad_w', 'grad_norm_mix': 'grad_w', 'grad_w_in': 'grad_w', 'grad_swa_sinks': 'grad_w', 'grad_rel_bias': 'grad_w', 'grad_w_branch_swa': 'grad_w', 'grad_w_branch_sb': 'grad_w', 'grad_w_out': 'grad_w', 'grad_norm_ffn2': 'grad_w', 'grad_ffn2_w1': 'grad_w', 'grad_ffn2_w3': 'grad_w', 'grad_ffn2_w2': 'grad_w', 'grad_norm_final': 'grad_w', 'delta_norm_ffn1': 'delta_w', 'delta_ffn1_w1': 'delta_w', 'delta_ffn1_w3': 'delta_w', 'delta_ffn1_w2': 'delta_w', 'delta_norm_mix': 'delta_w', 'delta_w_in': 'delta_w', 'delta_swa_sinks': 'delta_w', 'delta_rel_bias': 'delta_w', 'delta_w_branch_swa': 'delta_w', 'delta_w_branch_sb': 'delta_w', 'delta_w_out': 'delta_w', 'delta_norm_ffn2': 'delta_w', 'delta_ffn2_w1': 'delta_w', 'delta_ffn2_w3': 'delta_w', 'delta_ffn2_w2': 'delta_w', 'delta_norm_final': 'delta_w', 'new_m_norm_ffn1': 'new_m', 'new_m_ffn1_w1': 'new_m', 'new_m_ffn1_w3': 'new_m', 'new_m_ffn1_w2': 'new_m', 'new_m_norm_mix': 'new_m', 'new_m_w_in': 'new_m', 'new_m_swa_sinks': 'new_m', 'new_m_rel_bias': 'new_m', 'new_m_w_branch_swa': 'new_m', 'new_m_w_branch_sb': 'new_m', 'new_m_w_out': 'new_m', 'new_m_norm_ffn2': 'new_m', 'new_m_ffn2_w1': 'new_m', 'new_m_ffn2_w3': 'new_m', 'new_m_ffn2_w2': 'new_m', 'new_m_norm_final': 'new_m', 'new_v_norm_ffn1': 'new_v', 'new_v_ffn1_w1': 'new_v', 'new_v_ffn1_w3': 'new_v', 'new_v_ffn1_w2': 'new_v', 'new_v_norm_mix': 'new_v', 'new_v_w_in': 'new_v', 'new_v_swa_sinks': 'new_v', 'new_v_rel_bias': 'new_v', 'new_v_w_branch_swa': 'new_v', 'new_v_w_branch_sb': 'new_v', 'new_v_w_out': 'new_v', 'new_v_norm_ffn2': 'new_v', 'new_v_ffn2_w1': 'new_v', 'new_v_ffn2_w3': 'new_v', 'new_v_ffn2_w2': 'new_v', 'new_v_norm_final': 'new_v'}


def _forward(args):
    return _fwd_reference(*[args[k] for k in FWD_PARAMS])


def _output_shape():
    out = _jax.eval_shape(lambda: _forward(_fwd_setup_inputs(0)))
    return out.shape, out.dtype

N_MICROBATCH = 1
ADAM_LR = 0.001
ADAM_B1 = 0.9
ADAM_B2 = 0.999
ADAM_EPS = 1e-08
ADAM_WD = 0.01
ADAM_STEP = 10
PER_EXAMPLE_BATCH_AXIS = {'x': 0, 'loss_target': 0}
SHARED_INPUTS = []
_WEIGHT_DTYPES = {'norm_ffn1': _jnp.float32, 'ffn1_w1': _jnp.float32, 'ffn1_w3': _jnp.float32, 'ffn1_w2': _jnp.float32, 'norm_mix': _jnp.float32, 'w_in': _jnp.float32, 'swa_sinks': _jnp.float32, 'rel_bias': _jnp.float32, 'w_branch_swa': _jnp.float32, 'w_branch_sb': _jnp.float32, 'w_out': _jnp.float32, 'norm_ffn2': _jnp.float32, 'ffn2_w1': _jnp.float32, 'ffn2_w3': _jnp.float32, 'ffn2_w2': _jnp.float32, 'norm_final': _jnp.float32}
MOMENT_SCALE = {'norm_ffn1': 1.340594e-01, 'ffn1_w1': 4.788408e-02, 'ffn1_w3': 4.629052e-02, 'ffn1_w2': 7.683567e-02, 'norm_mix': 1.124161e-01, 'w_in': 5.446466e-02, 'swa_sinks': 2.378265e-02, 'rel_bias': 4.811496e-02, 'w_branch_swa': 2.840777e-02, 'w_branch_sb': 8.308592e-02, 'w_out': 8.621419e-02, 'norm_ffn2': 1.025533e-01, 'ffn2_w1': 4.096486e-02, 'ffn2_w3': 3.969761e-02, 'ffn2_w2': 6.564405e-02, 'norm_final': 6.396384e+01}


def _to_microbatches(a, axis):
    t = _jnp.moveaxis(a, axis, 0)
    t = t.reshape((N_MICROBATCH, t.shape[0] // N_MICROBATCH) + t.shape[1:])
    return _jnp.moveaxis(t, 1, axis + 1)


def setup_inputs(seed: int = 0) -> dict:
    inp = _fwd_setup_inputs(seed)
    key = _jax.random.fold_in(_jax.random.key(seed), 7919)
    shape, _ = _output_shape()
    out = dict(inp)
    out["loss_target"] = _jax.random.normal(_jax.random.fold_in(key, 0), shape, _jnp.float32)
    for i, name in enumerate(TWIN_WEIGHTS):
        w = inp[name].astype(_jnp.float32)
        if MOMENT_SCALE is None:
            s = _jnp.sqrt(_jnp.mean(_jnp.square(w)) + 1e-30)
        else:
            s = MOMENT_SCALE[name]
        km, kv = _jax.random.split(_jax.random.fold_in(key, i + 1))
        out[name] = w
        out["m_" + name] = s * _jax.random.normal(km, w.shape, _jnp.float32)
        out["v_" + name] = (s * s) * _jax.random.uniform(kv, w.shape, _jnp.float32, 0.5, 1.5)
    if N_MICROBATCH > 1:
        for name, axis in PER_EXAMPLE_BATCH_AXIS.items():
            out[name] = _to_microbatches(out[name], axis)
    return {'x': out['x'], 'norm_ffn1': out['norm_ffn1'], 'ffn1_w1': out['ffn1_w1'], 'ffn1_w3': out['ffn1_w3'], 'ffn1_w2': out['ffn1_w2'], 'norm_mix': out['norm_mix'], 'w_in': out['w_in'], 'swa_sinks': out['swa_sinks'], 'rel_bias': out['rel_bias'], 'w_branch_swa': out['w_branch_swa'], 'w_branch_sb': out['w_branch_sb'], 'w_out': out['w_out'], 'norm_ffn2': out['norm_ffn2'], 'ffn2_w1': out['ffn2_w1'], 'ffn2_w3': out['ffn2_w3'], 'ffn2_w2': out['ffn2_w2'], 'norm_final': out['norm_final'], 'loss_target': out['loss_target'], 'm_norm_ffn1': out['m_norm_ffn1'], 'm_ffn1_w1': out['m_ffn1_w1'], 'm_ffn1_w3': out['m_ffn1_w3'], 'm_ffn1_w2': out['m_ffn1_w2'], 'm_norm_mix': out['m_norm_mix'], 'm_w_in': out['m_w_in'], 'm_swa_sinks': out['m_swa_sinks'], 'm_rel_bias': out['m_rel_bias'], 'm_w_branch_swa': out['m_w_branch_swa'], 'm_w_branch_sb': out['m_w_branch_sb'], 'm_w_out': out['m_w_out'], 'm_norm_ffn2': out['m_norm_ffn2'], 'm_ffn2_w1': out['m_ffn2_w1'], 'm_ffn2_w3': out['m_ffn2_w3'], 'm_ffn2_w2': out['m_ffn2_w2'], 'm_norm_final': out['m_norm_final'], 'v_norm_ffn1': out['v_norm_ffn1'], 'v_ffn1_w1': out['v_ffn1_w1'], 'v_ffn1_w3': out['v_ffn1_w3'], 'v_ffn1_w2': out['v_ffn1_w2'], 'v_norm_mix': out['v_norm_mix'], 'v_w_in': out['v_w_in'], 'v_swa_sinks': out['v_swa_sinks'], 'v_rel_bias': out['v_rel_bias'], 'v_w_branch_swa': out['v_w_branch_swa'], 'v_w_branch_sb': out['v_w_branch_sb'], 'v_w_out': out['v_w_out'], 'v_norm_ffn2': out['v_norm_ffn2'], 'v_ffn2_w1': out['v_ffn2_w1'], 'v_ffn2_w3': out['v_ffn2_w3'], 'v_ffn2_w2': out['v_ffn2_w2'], 'v_norm_final': out['v_norm_final']}


def _loss(weights, diff, rest, loss_target):
    with _jax.named_scope("forward"):
        args = {**rest, TWIN_DIFF_INPUT: diff, **{k: w.astype(_WEIGHT_DTYPES[k]) for k, w in weights.items()}}
        y = _forward(args)
    with _jax.named_scope("loss_head"):
        err = _jnp.square(y.astype(_jnp.float32) - loss_target)
        return 0.5 * _jnp.sum(_jnp.mean(err, axis=-1)) if err.ndim else 0.5 * err


def _adamw(w, g, m, v):
    m = ADAM_B1 * m + (1.0 - ADAM_B1) * g
    v = ADAM_B2 * v + (1.0 - ADAM_B2) * _jnp.square(g)
    m_hat = m / (1.0 - ADAM_B1 ** ADAM_STEP)
    v_hat = v / (1.0 - ADAM_B2 ** ADAM_STEP)
    delta = -ADAM_LR * (m_hat / (_jnp.sqrt(v_hat) + ADAM_EPS) + ADAM_WD * w)
    return delta, m, v


def reference(x, norm_ffn1, ffn1_w1, ffn1_w3, ffn1_w2, norm_mix, w_in, swa_sinks, rel_bias, w_branch_swa, w_branch_sb, w_out, norm_ffn2, ffn2_w1, ffn2_w3, ffn2_w2, norm_final, loss_target, m_norm_ffn1, m_ffn1_w1, m_ffn1_w3, m_ffn1_w2, m_norm_mix, m_w_in, m_swa_sinks, m_rel_bias, m_w_branch_swa, m_w_branch_sb, m_w_out, m_norm_ffn2, m_ffn2_w1, m_ffn2_w3, m_ffn2_w2, m_norm_final, v_norm_ffn1, v_ffn1_w1, v_ffn1_w3, v_ffn1_w2, v_norm_mix, v_w_in, v_swa_sinks, v_rel_bias, v_w_branch_swa, v_w_branch_sb, v_w_out, v_norm_ffn2, v_ffn2_w1, v_ffn2_w3, v_ffn2_w2, v_norm_final):
    given = dict(x=x, norm_ffn1=norm_ffn1, ffn1_w1=ffn1_w1, ffn1_w3=ffn1_w3, ffn1_w2=ffn1_w2, norm_mix=norm_mix, w_in=w_in, swa_sinks=swa_sinks, rel_bias=rel_bias, w_branch_swa=w_branch_swa, w_branch_sb=w_branch_sb, w_out=w_out, norm_ffn2=norm_ffn2, ffn2_w1=ffn2_w1, ffn2_w3=ffn2_w3, ffn2_w2=ffn2_w2, norm_final=norm_final, loss_target=loss_target, m_norm_ffn1=m_norm_ffn1, m_ffn1_w1=m_ffn1_w1, m_ffn1_w3=m_ffn1_w3, m_ffn1_w2=m_ffn1_w2, m_norm_mix=m_norm_mix, m_w_in=m_w_in, m_swa_sinks=m_swa_sinks, m_rel_bias=m_rel_bias, m_w_branch_swa=m_w_branch_swa, m_w_branch_sb=m_w_branch_sb, m_w_out=m_w_out, m_norm_ffn2=m_norm_ffn2, m_ffn2_w1=m_ffn2_w1, m_ffn2_w3=m_ffn2_w3, m_ffn2_w2=m_ffn2_w2, m_norm_final=m_norm_final, v_norm_ffn1=v_norm_ffn1, v_ffn1_w1=v_ffn1_w1, v_ffn1_w3=v_ffn1_w3, v_ffn1_w2=v_ffn1_w2, v_norm_mix=v_norm_mix, v_w_in=v_w_in, v_swa_sinks=v_swa_sinks, v_rel_bias=v_rel_bias, v_w_branch_swa=v_w_branch_swa, v_w_branch_sb=v_w_branch_sb, v_w_out=v_w_out, v_norm_ffn2=v_norm_ffn2, v_ffn2_w1=v_ffn2_w1, v_ffn2_w3=v_ffn2_w3, v_ffn2_w2=v_ffn2_w2, v_norm_final=v_norm_final)
    weights = {n: given[n] for n in TWIN_WEIGHTS}
    shared = {n: given[n] for n in SHARED_INPUTS}
    per_example = {n: given[n] for n in ['x']}
    grad_fn = _jax.value_and_grad(_loss, argnums=(0, 1))

    def one_microbatch(ex, loss_target):
        ex = dict(ex)
        diff = ex.pop(TWIN_DIFF_INPUT)
        return grad_fn(weights, diff, {**shared, **ex}, loss_target)

    if N_MICROBATCH == 1:
        loss, (grad_w, grad_x) = one_microbatch(per_example, given["loss_target"])
    else:
        def body(carry, xs):
            loss_sum, grad_sum = carry
            l_k, (gw_k, gx_k) = one_microbatch(xs[0], xs[1])
            with _jax.named_scope("update"):
                return (loss_sum + l_k, _jax.tree.map(_jnp.add, grad_sum, gw_k)), gx_k

        init = (_jnp.zeros((), _jnp.float32), _jax.tree.map(_jnp.zeros_like, weights))
        (loss, grad_w), grad_x = _jax.lax.scan(body, init, (per_example, given["loss_target"]))
    with _jax.named_scope("update"):
        delta_w, new_m, new_v = {}, {}, {}
        for n in TWIN_WEIGHTS:
            delta_w[n], new_m[n], new_v[n] = _adamw(weights[n], grad_w[n], given["m_" + n], given["v_" + n])
    return (loss, grad_x, *[grad_w[n] for n in TWIN_WEIGHTS], *[delta_w[n] for n in TWIN_WEIGHTS],
            *[new_m[n] for n in TWIN_WEIGHTS], *[new_v[n] for n in TWIN_WEIGHTS])
```

```python
import functools
import math

import jax
import jax.numpy as jnp
from jax import lax
from jax.experimental import pallas as pl
from jax.experimental.pallas import tpu as pltpu

F32, BF16 = jnp.float32, jnp.bfloat16
MESH_ID = pl.DeviceIdType.MESH
ANY = pl.BlockSpec(memory_space=pl.ANY)

RMS_EPS = 1e-6
HEAD_DIM = 64
SWA_Q_HEADS, SWA_KV_HEADS, SWA_GROUP = 8, 2, 4
SWA_BLOCK = 128
SB_HEADS = 8
SB_BLOCK = 256
REL_BUCKETS, REL_MAX_DIST = 32, 128
NEG_BIG = -1e30
QK_SCALE = HEAD_DIM ** -0.5
ADAM_LR, ADAM_B1, ADAM_B2, ADAM_EPS, ADAM_WD, ADAM_STEP = 0.001, 0.9, 0.999, 1e-08, 0.01, 10

N_CHIPS = 4
TOKEN_TILE = 512
FF_TILE = 256
VMEM_LIMIT = 48 * 1024 * 1024


def _cp(*sem):
    return pltpu.CompilerParams(dimension_semantics=sem, vmem_limit_bytes=VMEM_LIMIT)


def _nn(a, b):
    return jnp.dot(a, b, preferred_element_type=F32)


def _nt(a, b):
    return lax.dot_general(a, b, (((1,), (1,)), ((), ())), preferred_element_type=F32)


def _tn(a, b):
    return lax.dot_general(a, b, (((0,), (0,)), ((), ())), preferred_element_type=F32)


def _norm_fwd(x, g):
    return x * lax.rsqrt(jnp.mean(x * x, axis=-1, keepdims=True) + RMS_EPS) * g


def _norm_bwd(x, g, dh):
    r = lax.rsqrt(jnp.mean(x * x, axis=-1, keepdims=True) + RMS_EPS)
    xh = x * r
    dxh = dh * g
    dx = r * (dxh - xh * jnp.mean(dxh * xh, axis=-1, keepdims=True))
    return dx, jnp.sum(dh * xh, axis=0, keepdims=True)


def _hilo_nn(v, tri):
    hi = v.astype(BF16)
    lo = (v - hi.astype(F32)).astype(BF16)
    return _nn(hi, tri) + _nn(lo, tri)


def _softplus(z):
    return jnp.maximum(z, 0.0) + jnp.log(1.0 + jnp.exp(-jnp.abs(z)))


def _ffn_fwd(x, g, w1t, w3t, w2, name):
    S, D = x.shape
    F = w2.shape[0]
    tm, tf = min(TOKEN_TILE, S), FF_TILE
    nj = F // tf

    def body(x_ref, g_ref, w1_ref, w3_ref, w2_ref, xo_ref, h_ref, a_ref, b_ref, hs, acc):
        j = pl.program_id(1)

        @pl.when(j == 0)
        def _():
            hb = _norm_fwd(x_ref[...], g_ref[...]).astype(BF16)
            hs[...] = hb
            h_ref[...] = hb
            acc[...] = jnp.zeros_like(acc)

        h = hs[...]
        a = _nt(h, w1_ref[...])
        b = _nt(h, w3_ref[...])
        a_ref[...] = a.astype(BF16)
        b_ref[...] = b.astype(BF16)
        u = a * jax.nn.sigmoid(a) * b
        acc[...] += _nn(u.astype(BF16), w2_ref[...])

        @pl.when(j == nj - 1)
        def _():
            xo_ref[...] = x_ref[...] + 0.5 * acc[...]

    return pl.pallas_call(
        body, name=name, grid=(S // tm, nj),
        in_specs=[pl.BlockSpec((tm, D), lambda i, j: (i, 0)),
                  pl.BlockSpec((1, D), lambda i, j: (0, 0)),
                  pl.BlockSpec((tf, D), lambda i, j: (j, 0)),
                  pl.BlockSpec((tf, D), lambda i, j: (j, 0)),
                  pl.BlockSpec((tf, D), lambda i, j: (j, 0))],
        out_specs=[pl.BlockSpec((tm, D), lambda i, j: (i, 0)),
                   pl.BlockSpec((tm, D), lambda i, j: (i, 0)),
                   pl.BlockSpec((tm, tf), lambda i, j: (i, j)),
                   pl.BlockSpec((tm, tf), lambda i, j: (i, j))],
        out_shape=[jax.ShapeDtypeStruct((S, D), F32), jax.ShapeDtypeStruct((S, D), BF16),
                   jax.ShapeDtypeStruct((S, F), BF16), jax.ShapeDtypeStruct((S, F), BF16)],
        scratch_shapes=[pltpu.VMEM((tm, D), BF16), pltpu.VMEM((tm, D), F32)],
        compiler_params=_cp("arbitrary", "arbitrary"),
    )(x, g, w1t, w3t, w2)


def _ffn_bwd(dxo, x, g, a, b, w1t, w3t, w2, name):
    S, D = x.shape
    F = w2.shape[0]
    tm, tf = min(TOKEN_TILE, S), FF_TILE
    ni, nj = S // tm, F // tf

    def body(dxo_ref, x_ref, g_ref, a_ref, b_ref, w1_ref, w3_ref, w2_ref,
             dx_ref, dg_ref, dz_ref, da_ref, db_ref, u_ref, dzs, acc):
        i, j = pl.program_id(0), pl.program_id(1)

        @pl.when(j == 0)
        def _():
            dzb = (0.5 * dxo_ref[...]).astype(BF16)
            dzs[...] = dzb
            dz_ref[...] = dzb
            acc[...] = jnp.zeros_like(acc)

        du = _nt(dzs[...], w2_ref[...])
        av = a_ref[...].astype(F32)
        bv = b_ref[...].astype(F32)
        s = jax.nn.sigmoid(av)
        silu = av * s
        db = (du * silu).astype(BF16)
        da = (du * bv * (s * (1.0 + av * (1.0 - s)))).astype(BF16)
        da_ref[...] = da
        db_ref[...] = db
        u_ref[...] = (silu * bv).astype(BF16)
        acc[...] += _nn(da, w1_ref[...]) + _nn(db, w3_ref[...])

        @pl.when(j == nj - 1)
        def _():
            dx, dg = _norm_bwd(x_ref[...], g_ref[...], acc[...])
            dx_ref[...] = dxo_ref[...] + dx

            @pl.when(i == 0)
            def _():
                dg_ref[...] = dg

            @pl.when(i > 0)
            def _():
                dg_ref[...] += dg

    row = pl.BlockSpec((tm, D), lambda i, j: (i, 0))
    wsp = pl.BlockSpec((tf, D), lambda i, j: (j, 0))
    col = pl.BlockSpec((tm, tf), lambda i, j: (i, j))
    vec = pl.BlockSpec((1, D), lambda i, j: (0, 0))
    return pl.pallas_call(
        body, name=name, grid=(ni, nj),
        in_specs=[row, row, vec, col, col, wsp, wsp, wsp],
        out_specs=[row, vec, row, col, col, col],
        out_shape=[jax.ShapeDtypeStruct((S, D), F32), jax.ShapeDtypeStruct((1, D), F32),
                   jax.ShapeDtypeStruct((S, D), BF16), jax.ShapeDtypeStruct((S, F), BF16),
                   jax.ShapeDtypeStruct((S, F), BF16), jax.ShapeDtypeStruct((S, F), BF16)],
        scratch_shapes=[pltpu.VMEM((tm, D), BF16), pltpu.VMEM((tm, D), F32)],
        compiler_params=_cp("arbitrary", "arbitrary"),
    )(dxo, x, g, a, b, w1t, w3t, w2)


def _tn_matmul(a, b, name):
    S, M = a.shape
    N = b.shape[1]
    ts = min(TOKEN_TILE, S)
    tmm = 256 if M % 256 == 0 else M
    ns = S // ts

    def body(a_ref, b_ref, o_ref):
        s = pl.program_id(1)
        part = _tn(a_ref[...], b_ref[...])

        @pl.when(s == 0)
        def _():
            o_ref[...] = part

        @pl.when(s > 0)
        def _():
            o_ref[...] += part

    return pl.pallas_call(
        body, name=name, grid=(M // tmm, ns),
        in_specs=[pl.BlockSpec((ts, tmm), lambda m, s: (s, m)),
                  pl.BlockSpec((ts, N), lambda m, s: (s, 0))],
        out_specs=pl.BlockSpec((tmm, N), lambda m, s: (m, 0)),
        out_shape=jax.ShapeDtypeStruct((M, N), F32),
        compiler_params=_cp("arbitrary", "arbitrary"),
    )(a, b)


def _norm_matmul_nt(x, g, wt, out_dtype, name):
    S, D = x.shape
    N = wt.shape[0]
    tm, tn = min(TOKEN_TILE, S), 256

    def body(x_ref, g_ref, w_ref, o_ref, h_ref, hs):
        @pl.when(pl.program_id(1) == 0)
        def _():
            hb = _norm_fwd(x_ref[...], g_ref[...]).astype(BF16)
            hs[...] = hb
            h_ref[...] = hb

        o_ref[...] = _nt(hs[...], w_ref[...]).astype(out_dtype)

    return pl.pallas_call(
        body, name=name, grid=(S // tm, N // tn),
        in_specs=[pl.BlockSpec((tm, D), lambda i, j: (i, 0)),
                  pl.BlockSpec((1, D), lambda i, j: (0, 0)),
                  pl.BlockSpec((tn, D), lambda i, j: (j, 0))],
        out_specs=[pl.BlockSpec((tm, tn), lambda i, j: (i, j)),
                   pl.BlockSpec((tm, D), lambda i, j: (i, 0))],
        out_shape=[jax.ShapeDtypeStruct((S, N), out_dtype), jax.ShapeDtypeStruct((S, D), BF16)],
        scratch_shapes=[pltpu.VMEM((tm, D), BF16)],
        compiler_params=_cp("arbitrary", "arbitrary"),
    )(x, g, wt)


def _matmul_norm_bwd(dy, w, x, g, dres, name):
    S, K = dy.shape
    D = w.shape[1]
    tm, tk = min(TOKEN_TILE, S), 256
    nk = K // tk

    def body(dy_ref, w_ref, x_ref, g_ref, dres_ref, dx_ref, dg_ref, acc):
        i, k = pl.program_id(0), pl.program_id(1)
        part = _nn(dy_ref[...], w_ref[...])

        @pl.when(k == 0)
        def _():
            acc[...] = part

        @pl.when(k > 0)
        def _():
            acc[...] += part

        @pl.when(k == nk - 1)
        def _():
            dx, dg = _norm_bwd(x_ref[...], g_ref[...], acc[...])
            dx_ref[...] = dres_ref[...] + dx

            @pl.when(i == 0)
            def _():
                dg_ref[...] = dg

            @pl.when(i > 0)
            def _():
                dg_ref[...] += dg

    row = pl.BlockSpec((tm, D), lambda i, k: (i, 0))
    vec = pl.BlockSpec((1, D), lambda i, k: (0, 0))
    return pl.pallas_call(
        body, name=name, grid=(S // tm, nk),
        in_specs=[pl.BlockSpec((tm, tk), lambda i, k: (i, k)),
                  pl.BlockSpec((tk, D), lambda i, k: (k, 0)), row, vec, row],
        out_specs=[row, vec],
        out_shape=[jax.ShapeDtypeStruct((S, D), F32), jax.ShapeDtypeStruct((1, D), F32)],
        scratch_shapes=[pltpu.VMEM((tm, D), F32)],
        compiler_params=_cp("arbitrary", "arbitrary"),
    )(dy, w, x, g, dres)


def _merge_fwd(x1, gates, o_a, o_b, wat, wbt, w_out):
    S, D = x1.shape
    W = o_a.shape[1]
    tm = min(TOKEN_TILE, S)

    def body(x_ref, ga_ref, gb_ref, oa_ref, ob_ref, wa_ref, wb_ref, wo_ref,
             x2_ref, mg_ref, ba_ref, bb_ref):
        ba = _nt(oa_ref[...], wa_ref[...])
        bb = _nt(ob_ref[...], wb_ref[...])
        merged = jax.nn.sigmoid(ga_ref[...]) * ba + jax.nn.sigmoid(gb_ref[...]) * bb
        mb = merged.astype(BF16)
        mg_ref[...] = mb
        ba_ref[...] = ba.astype(BF16)
        bb_ref[...] = bb.astype(BF16)
        x2_ref[...] = x_ref[...] + _nn(mb, wo_ref[...])

    row = pl.BlockSpec((tm, D), lambda i: (i, 0))
    full = lambda r, c: pl.BlockSpec((r, c), lambda i: (0, 0))
    return pl.pallas_call(
        body, name="merge_fwd", grid=(S // tm,),
        in_specs=[row, pl.BlockSpec((tm, D), lambda i: (i, 0)), pl.BlockSpec((tm, D), lambda i: (i, 1)),
                  pl.BlockSpec((tm, W), lambda i: (i, 0)), pl.BlockSpec((tm, W), lambda i: (i, 0)),
                  full(D, W), full(D, W), full(D, D)],
        out_specs=[row, row, row, row],
        out_shape=[jax.ShapeDtypeStruct((S, D), F32)] + [jax.ShapeDtypeStruct((S, D), BF16)] * 3,
        compiler_params=_cp("arbitrary"),
    )(x1, gates, gates, o_a, o_b, wat, wbt, w_out)


def _merge_bwd(dx2, gates, ba, bb, wat, wbt, w_out):
    S, D = dx2.shape
    W = wat.shape[1]
    tm = min(TOKEN_TILE, S)

    def body(dx_ref, ga_ref, gb_ref, ba_ref, bb_ref, wa_ref, wb_ref, wo_ref,
             dxb_ref, dba_ref, dbb_ref, dgt_ref, doa_ref, dob_ref):
        dxb = dx_ref[...].astype(BF16)
        dxb_ref[...] = dxb
        dm = _nt(dxb, wo_ref[...])
        sa = jax.nn.sigmoid(ga_ref[...])
        sb = jax.nn.sigmoid(gb_ref[...])
        dba = (dm * sa).astype(BF16)
        dbb = (dm * sb).astype(BF16)
        dba_ref[...] = dba
        dbb_ref[...] = dbb
        dgt_ref[:, :D] = (dm * ba_ref[...].astype(F32) * sa * (1.0 - sa)).astype(BF16)
        dgt_ref[:, D:] = (dm * bb_ref[...].astype(F32) * sb * (1.0 - sb)).astype(BF16)
        doa_ref[...] = _nn(dba, wa_ref[...])
        dob_ref[...] = _nn(dbb, wb_ref[...])

    row = pl.BlockSpec((tm, D), lambda i: (i, 0))
    full = lambda r, c: pl.BlockSpec((r, c), lambda i: (0, 0))
    return pl.pallas_call(
        body, name="merge_bwd", grid=(S // tm,),
        in_specs=[row, pl.BlockSpec((tm, D), lambda i: (i, 0)), pl.BlockSpec((tm, D), lambda i: (i, 1)),
                  row, row, full(D, W), full(D, W), full(D, D)],
        out_specs=[row, row, row, pl.BlockSpec((tm, 2 * D), lambda i: (i, 0)),
                   pl.BlockSpec((tm, W), lambda i: (i, 0)), pl.BlockSpec((tm, W), lambda i: (i, 0))],
        out_shape=[jax.ShapeDtypeStruct((S, D), BF16)] * 3 + [jax.ShapeDtypeStruct((S, 2 * D), BF16)]
                  + [jax.ShapeDtypeStruct((S, W), F32)] * 2,
        compiler_params=_cp("arbitrary"),
    )(dx2, gates, gates, ba, bb, wat, wbt, w_out)


def _final_loss(x3, gf, target):
    S, D = x3.shape
    tm = min(TOKEN_TILE, S)

    def body(x_ref, g_ref, t_ref, loss_ref, dx_ref, dg_ref):
        i = pl.program_id(0)
        x = x_ref[...]
        g = g_ref[...]
        e = _norm_fwd(x, g) - t_ref[...]
        part = 0.5 * jnp.sum(jnp.mean(e * e, axis=-1, keepdims=True), axis=0, keepdims=True)
        dx, dg = _norm_bwd(x, g, e * (1.0 / D))
        dx_ref[...] = dx

        @pl.when(i == 0)
        def _():
            loss_ref[...] = part
            dg_ref[...] = dg

        @pl.when(i > 0)
        def _():
            loss_ref[...] += part
            dg_ref[...] += dg

    row = pl.BlockSpec((tm, D), lambda i: (i, 0))
    vec = pl.BlockSpec((1, D), lambda i: (0, 0))
    return pl.pallas_call(
        body, name="final_loss", grid=(S // tm,),
        in_specs=[row, vec, row],
        out_specs=[pl.BlockSpec((1, 1), lambda i: (0, 0)), row, vec],
        out_shape=[jax.ShapeDtypeStruct((1, 1), F32), jax.ShapeDtypeStruct((S, D), F32),
                   jax.ShapeDtypeStruct((1, D), F32)],
        compiler_params=_cp("arbitrary"),
    )(x3, gf, target)


def _sb_fwd(qn, kt, vt):
    H, nb, T, dh = qn.shape

    def body(q_ref, k_ref, v_ref, o_ref, tl_ref):
        row = lax.broadcasted_iota(jnp.int32, (T, T), 0)
        col = lax.broadcasted_iota(jnp.int32, (T, T), 1)
        upper = (row > col).astype(BF16)
        tri = col < row

        def qblock(i, _):
            q = q_ref[i]

            def kstep(t, carry):
                c, oacc = carry
                kb = i - t
                z = _nn(q, k_ref[kb])
                sp = _softplus(z)
                mask = jnp.logical_or(kb < i, tri)
                lk = jnp.where(mask, -sp, 0.0)
                between = _hilo_nn(lk, upper) + c
                a = jnp.where(mask, jnp.exp(z - sp + between), 0.0)
                oacc = oacc + _nt(v_ref[kb], a.astype(BF16))
                return c + jnp.sum(lk, axis=1, keepdims=True), oacc

            c, oacc = lax.fori_loop(0, i + 1, kstep,
                                    (jnp.zeros((T, 1), F32), jnp.zeros((dh, T), F32)))
            o_ref[i] = oacc
            tl_ref[i] = c
            return 0

        lax.fori_loop(0, nb, qblock, 0)

    hn = pl.BlockSpec((None, nb, T, dh), lambda h: (h, 0, 0, 0))
    ht = pl.BlockSpec((None, nb, dh, T), lambda h: (h, 0, 0, 0))
    return pl.pallas_call(
        body, name="sb_fwd", grid=(H,),
        in_specs=[hn, ht, ht],
        out_specs=[ht, pl.BlockSpec((None, nb, T, 1), lambda h: (h, 0, 0, 0))],
        out_shape=[jax.ShapeDtypeStruct((H, nb, dh, T), F32), jax.ShapeDtypeStruct((H, nb, T, 1), F32)],
        compiler_params=_cp("arbitrary"),
    )(qn, kt, vt)


def _sb_bwd(qn, qt, kt, vt, don, dot, tl):
    H, nb, T, dh = qn.shape

    def body(q_ref, qt_ref, k_ref, v_ref, do_ref, dot_ref, tl_ref, dq_ref, dk_ref, dv_ref):
        row = lax.broadcasted_iota(jnp.int32, (T, T), 0)
        col = lax.broadcasted_iota(jnp.int32, (T, T), 1)
        incl = (row <= col).astype(BF16)
        excl = (row < col).astype(BF16)
        tri = col < row
        dk_ref[...] = jnp.zeros_like(dk_ref)
        dv_ref[...] = jnp.zeros_like(dv_ref)

        def qblock(i, _):
            q = q_ref[i]
            qT = qt_ref[i]
            do = do_ref[i]
            doT = dot_ref[i]
            total = tl_ref[i]

            def kstep(kb, carry):
                pre, pre_g, dq = carry
                kT = k_ref[kb]
                z = _nn(q, kT)
                sp = _softplus(z)
                mask = jnp.logical_or(kb < i, tri)
                lk = jnp.where(mask, -sp, 0.0)
                between = total - pre - _hilo_nn(lk, incl)
                ls = z - sp
                a = jnp.where(mask, jnp.exp(ls + between), 0.0)
                sig = jnp.exp(ls)
                g = a * _nn(do, v_ref[kb])
                p = pre_g + _hilo_nn(g, excl)
                dzb = jnp.where(mask, g - (g + p) * sig, 0.0).astype(BF16)
                dq = dq + _nt(kT, dzb)
                dk_ref[kb] += _nn(qT, dzb)
                dv_ref[kb] += _nn(doT, a.astype(BF16))
                return (pre + jnp.sum(lk, axis=1, keepdims=True),
                        pre_g + jnp.sum(g, axis=1, keepdims=True), dq)

            zero = jnp.zeros((T, 1), F32)
            _, _, dq = lax.fori_loop(0, i + 1, kstep, (zero, zero, jnp.zeros((dh, T), F32)))
            dq_ref[i] = dq
            return 0

        lax.fori_loop(0, nb, qblock, 0)

    hn = pl.BlockSpec((None, nb, T, dh), lambda h: (h, 0, 0, 0))
    ht = pl.BlockSpec((None, nb, dh, T), lambda h: (h, 0, 0, 0))
    return pl.pallas_call(
        body, name="sb_bwd", grid=(H,),
        in_specs=[hn, ht, ht, ht, hn, ht, pl.BlockSpec((None, nb, T, 1), lambda h: (h, 0, 0, 0))],
        out_specs=[ht, ht, ht],
        out_shape=[jax.ShapeDtypeStruct((H, nb, dh, T), F32)] * 3,
        compiler_params=_cp("arbitrary"),
    )(qn, qt, kt, vt, don, dot, tl)


def _swa_probs(q, kp, kc, bias, sink, first):
    T = q.shape[0]
    row = lax.broadcasted_iota(jnp.int32, (T, T), 0)
    col = lax.broadcasted_iota(jnp.int32, (T, T), 1)
    lp = jnp.where(jnp.logical_and(col > row, jnp.logical_not(first)), _nn(q, kp) + bias[:, :T], NEG_BIG)
    lc = jnp.where(col <= row, _nn(q, kc) + bias[:, T:], NEG_BIG)
    m = jnp.maximum(jnp.maximum(jnp.max(lp, axis=1, keepdims=True), jnp.max(lc, axis=1, keepdims=True)), sink)
    pp = jnp.exp(lp - m)
    pc = jnp.exp(lc - m)
    ps = jnp.exp(sink - m)
    inv = 1.0 / (jnp.sum(pp, axis=1, keepdims=True) + jnp.sum(pc, axis=1, keepdims=True) + ps)
    return pp * inv, pc * inv, ps * inv


def _swa_fwd(qn, kt, vt, bias, sinks):
    Hq, nb, T, dh = qn.shape
    grp = Hq // kt.shape[0]

    def body(sink_ref, q_ref, kp_ref, kc_ref, vp_ref, vc_ref, bias_ref, o_ref):
        h, n = pl.program_id(0), pl.program_id(1)
        pp, pc, _ = _swa_probs(q_ref[...], kp_ref[...], kc_ref[...], bias_ref[...], sink_ref[h], n == 0)
        o_ref[...] = _nt(vp_ref[...], pp.astype(BF16)) + _nt(vc_ref[...], pc.astype(BF16))

    prev = pl.BlockSpec((None, None, dh, T), lambda h, n: (h // grp, jnp.maximum(n - 1, 0), 0, 0))
    cur = pl.BlockSpec((None, None, dh, T), lambda h, n: (h // grp, n, 0, 0))
    return pl.pallas_call(
        body, name="swa_fwd", grid=(Hq, nb),
        in_specs=[pl.BlockSpec(memory_space=pltpu.SMEM),
                  pl.BlockSpec((None, None, T, dh), lambda h, n: (h, n, 0, 0)),
                  prev, cur, prev, cur,
                  pl.BlockSpec((None, T, 2 * T), lambda h, n: (h, 0, 0))],
        out_specs=pl.BlockSpec((None, None, dh, T), lambda h, n: (h, n, 0, 0)),
        out_shape=jax.ShapeDtypeStruct((Hq, nb, dh, T), F32),
        compiler_params=_cp("arbitrary", "arbitrary"),
    )(sinks, qn, kt, kt, vt, vt, bias)


def _swa_bwd(qn, qt, kt, vt, bias, sinks, don, dot, on):
    Hq, nb, T, dh = qn.shape
    Hkv = kt.shape[0]
    grp = Hq // Hkv

    def body(sink_ref, q_ref, qt_ref, kp_ref, kc_ref, vp_ref, vc_ref, bias_ref, do_ref, dot_ref, o_ref,
             dq_ref, dk_ref, dv_ref, dbias_ref, dsink_ref, ck, cv):
        hk, n = pl.program_id(0), pl.program_id(1)

        @pl.when(n == 0)
        def _():
            dbias_ref[...] = jnp.zeros_like(dbias_ref)
            dsink_ref[...] = jnp.zeros_like(dsink_ref)
            ck[...] = jnp.zeros_like(ck)
            cv[...] = jnp.zeros_like(cv)

        @pl.when(n < nb)
        def _():
            kp, kc, vp, vc = kp_ref[...], kc_ref[...], vp_ref[...], vc_ref[...]
            kprev = jnp.zeros((dh, T), F32)
            vprev = jnp.zeros((dh, T), F32)
            kcur = jnp.zeros((dh, T), F32)
            vcur = jnp.zeros((dh, T), F32)
            for g in range(grp):
                pp, pc, ps = _swa_probs(q_ref[g], kp, kc, bias_ref[g], sink_ref[hk * grp + g], n == 0)
                do = do_ref[g]
                dob = do.astype(BF16)
                delta = jnp.sum(do * o_ref[g], axis=1, keepdims=True)
                dlp = pp * (_nn(dob, vp) - delta)
                dlc = pc * (_nn(dob, vc) - delta)
                dbias_ref[g, :, :T] += dlp
                dbias_ref[g, :, T:] += dlc
                dsink_ref[g] += -ps * delta
                dlpb, dlcb = dlp.astype(BF16), dlc.astype(BF16)
                dq_ref[g] = _nt(kp, dlpb) + _nt(kc, dlcb)
                qT, doT = qt_ref[g], dot_ref[g]
                kprev += _nn(qT, dlpb)
                kcur += _nn(qT, dlcb)
                vprev += _nn(doT, pp.astype(BF16))
                vcur += _nn(doT, pc.astype(BF16))
            dk_ref[...] = ck[...] + kprev
            dv_ref[...] = cv[...] + vprev
            ck[...] = kcur
            cv[...] = vcur

        @pl.when(n == nb)
        def _():
            dk_ref[...] = ck[...]
            dv_ref[...] = cv[...]

    qn_spec = pl.BlockSpec((grp, None, T, dh), lambda h, n: (h, jnp.minimum(n, nb - 1), 0, 0))
    qt_spec = pl.BlockSpec((grp, None, dh, T), lambda h, n: (h, jnp.minimum(n, nb - 1), 0, 0))
    prev = pl.BlockSpec((None, None, dh, T), lambda h, n: (h, jnp.maximum(n - 1, 0), 0, 0))
    cur = pl.BlockSpec((None, None, dh, T), lambda h, n: (h, jnp.minimum(n, nb - 1), 0, 0))
    per_group = lambda a, b: pl.BlockSpec((grp, a, b), lambda h, n: (h, 0, 0))
    return pl.pallas_call(
        body, name="swa_bwd", grid=(Hkv, nb + 1),
        in_specs=[pl.BlockSpec(memory_space=pltpu.SMEM), qn_spec, qt_spec, prev, cur, prev, cur,
                  per_group(T, 2 * T), qn_spec, qt_spec, qn_spec],
        out_specs=[qt_spec, prev, prev, per_group(T, 2 * T), per_group(T, 1)],
        out_shape=[jax.ShapeDtypeStruct((Hq, nb, dh, T), F32), jax.ShapeDtypeStruct((Hkv, nb, dh, T), F32),
                   jax.ShapeDtypeStruct((Hkv, nb, dh, T), F32), jax.ShapeDtypeStruct((Hq, T, 2 * T), F32),
                   jax.ShapeDtypeStruct((Hq, T, 1), F32)],
        scratch_shapes=[pltpu.VMEM((dh, T), F32), pltpu.VMEM((dh, T), F32)],
        compiler_params=_cp("arbitrary", "arbitrary"),
    )(sinks, qn, qt, kt, kt, vt, vt, bias, don, dot, on)


def _split3(x):
    h1 = x.astype(BF16)
    r1 = x - h1.astype(F32)
    h2 = r1.astype(BF16)
    h3 = (r1 - h2.astype(F32)).astype(BF16)
    return h1, h2, h3


def _bias_expand(rel_t, onehot):
    Hq, NB = rel_t.shape
    L = onehot.shape[1]

    def body(r_ref, oh_ref, o_ref):
        h1, h2, h3 = _split3(r_ref[...])
        oh = oh_ref[...]
        o_ref[...] = _nn(h1, oh) + _nn(h2, oh) + _nn(h3, oh)

    return pl.pallas_call(
        body, name="bias_expand", grid=(1,),
        in_specs=[pl.BlockSpec((Hq, NB), lambda i: (0, 0)), pl.BlockSpec((NB, L), lambda i: (0, 0))],
        out_specs=pl.BlockSpec((Hq, L), lambda i: (0, 0)),
        out_shape=jax.ShapeDtypeStruct((Hq, L), F32),
        compiler_params=_cp("arbitrary"),
    )(rel_t, onehot)


def _bias_reduce(dbias, onehot):
    Hq, L = dbias.shape
    NB = onehot.shape[0]

    def body(d_ref, oh_ref, o_ref):
        h1, h2, h3 = _split3(d_ref[...])
        oh = oh_ref[...]
        o_ref[...] = _nt(h1, oh) + _nt(h2, oh) + _nt(h3, oh)

    return pl.pallas_call(
        body, name="bias_reduce", grid=(1,),
        in_specs=[pl.BlockSpec((Hq, L), lambda i: (0, 0)), pl.BlockSpec((NB, L), lambda i: (0, 0))],
        out_specs=pl.BlockSpec((Hq, NB), lambda i: (0, 0)),
        out_shape=jax.ShapeDtypeStruct((Hq, NB), F32),
        compiler_params=_cp("arbitrary"),
    )(dbias, onehot)


def _adamw(w, g, m, v, name):
    R, C = w.shape
    tr = 256 if R % 256 == 0 else R
    bc1 = 1.0 - ADAM_B1 ** ADAM_STEP
    bc2 = 1.0 - ADAM_B2 ** ADAM_STEP

    def body(w_ref, g_ref, m_ref, v_ref, d_ref, nm_ref, nv_ref):
        g = g_ref[...]
        m2 = ADAM_B1 * m_ref[...] + (1.0 - ADAM_B1) * g
        v2 = ADAM_B2 * v_ref[...] + (1.0 - ADAM_B2) * (g * g)
        nm_ref[...] = m2
        nv_ref[...] = v2
        d_ref[...] = -ADAM_LR * ((m2 / bc1) / (jnp.sqrt(v2 / bc2) + ADAM_EPS) + ADAM_WD * w_ref[...])

    spec = pl.BlockSpec((tr, C), lambda i: (i, 0))
    return pl.pallas_call(
        body, name=name, grid=(R // tr,),
        in_specs=[spec] * 4, out_specs=[spec] * 3,
        out_shape=[jax.ShapeDtypeStruct((R, C), F32)] * 3,
        compiler_params=_cp("arbitrary"),
    )(w, g, m, v)


def _add_halves(mine, recv, name):
    K, R, C = mine.shape
    tr = 416 if R % 416 == 0 else R

    def body(a_ref, b_ref, o_ref, ob_ref):
        s = a_ref[...] + b_ref[...]
        o_ref[...] = s
        ob_ref[...] = s.astype(BF16)

    spec = pl.BlockSpec((None, tr, C), lambda k, i: (k, i, 0))
    return pl.pallas_call(
        body, name=name, grid=(K, R // tr),
        in_specs=[spec, spec], out_specs=[spec, spec],
        out_shape=[jax.ShapeDtypeStruct((K, R, C), F32), jax.ShapeDtypeStruct((K, R, C), BF16)],
        compiler_params=_cp("arbitrary", "arbitrary"),
    )(mine, recv)


def _add_received(own, recv, name):
    R, C = own.shape
    tr = 416 if R % 416 == 0 else R

    def body(a_ref, r_ref, o_ref):
        o_ref[...] = ((a_ref[...] + r_ref[0].astype(F32)) + r_ref[1].astype(F32)) + r_ref[2].astype(F32)

    return pl.pallas_call(
        body, name=name, grid=(R // tr,),
        in_specs=[pl.BlockSpec((tr, C), lambda i: (i, 0)), pl.BlockSpec((3, tr, C), lambda i: (0, i, 0))],
        out_specs=pl.BlockSpec((tr, C), lambda i: (i, 0)),
        out_shape=jax.ShapeDtypeStruct((R, C), F32),
        compiler_params=_cp("arbitrary"),
    )(own, recv)


def _position():
    x, y, c = lax.axis_index("x"), lax.axis_index("y"), lax.axis_index("c")
    others = [(1 - x, y), (x, 1 - y), (1 - x, 1 - y)]
    return x, y, c, others


def _remote(src, dst, send_sems, recv_sems, k, dev):
    return pltpu.make_async_remote_copy(src_ref=src, dst_ref=dst, send_sem=send_sems.at[k],
                                        recv_sem=recv_sems.at[k], device_id=dev, device_id_type=MESH_ID)


def _gather_weights(shard):
    R, C = shard.shape
    half = R // 2

    def body(src, out, send_sems, recv_sems, local_sem):
        x, y, c, others = _position()
        mine = 2 * x + y
        rows = pl.ds(pl.multiple_of(c * half, 16), half)
        other_rows = pl.ds(pl.multiple_of((1 - c) * half, 16), half)
        local = pltpu.make_async_copy(src, out.at[mine], local_sem)
        local.start()
        sends = [_remote(src.at[rows], out.at[mine, rows], send_sems, recv_sems, j, (ox, oy, c))
                 for j, (ox, oy) in enumerate(others)]
        for cp in sends:
            cp.start()
        passed = []
        for j, (ox, oy) in enumerate(others):
            slot = out.at[2 * ox + oy, rows]
            _remote(slot, slot, send_sems, recv_sems, j, (ox, oy, c)).wait_recv()
            fwd = _remote(slot, slot, send_sems, recv_sems, 3 + j, (x, y, 1 - c))
            fwd.start()
            passed.append(fwd)
        for j, (ox, oy) in enumerate(others):
            slot = out.at[2 * ox + oy, other_rows]
            _remote(slot, slot, send_sems, recv_sems, 3 + j, (x, y, 1 - c)).wait_recv()
        for cp in sends + passed:
            cp.wait_send()
        local.wait()

    return pl.pallas_call(
        body, name="gather_weights",
        in_specs=[ANY], out_specs=ANY,
        out_shape=jax.ShapeDtypeStruct((N_CHIPS, R, C), shard.dtype),
        scratch_shapes=[pltpu.SemaphoreType.DMA((6,)), pltpu.SemaphoreType.DMA((6,)), pltpu.SemaphoreType.DMA],
    )(shard)


def _swap_halves(grads):
    K, R, C = grads.shape
    half = R // 2

    def body(src, out, send_sems, recv_sems):
        x, y, c, _ = _position()
        theirs = src.at[:, pl.ds(pl.multiple_of((1 - c) * half, 8), half), :]
        cp = _remote(theirs, out, send_sems, recv_sems, 0, (x, y, 1 - c))
        cp.start()
        cp.wait()

    return pl.pallas_call(
        body, name="swap_halves",
        in_specs=[ANY], out_specs=ANY,
        out_shape=jax.ShapeDtypeStruct((K, half, C), grads.dtype),
        scratch_shapes=[pltpu.SemaphoreType.DMA((1,)), pltpu.SemaphoreType.DMA((1,))],
    )(grads)


def _scatter_to_owners(parts):
    K, H, C = parts.shape

    def body(src, out, send_sems, recv_sems):
        x, y, c, others = _position()
        sends = [_remote(src.at[2 * ox + oy], out.at[j], send_sems, recv_sems, j, (ox, oy, c))
                 for j, (ox, oy) in enumerate(others)]
        for cp in sends:
            cp.start()
        for cp in sends:
            cp.wait()

    return pl.pallas_call(
        body, name="scatter_to_owners",
        in_specs=[ANY], out_specs=ANY,
        out_shape=jax.ShapeDtypeStruct((3, H, C), parts.dtype),
        scratch_shapes=[pltpu.SemaphoreType.DMA((3,)), pltpu.SemaphoreType.DMA((3,))],
    )(parts)


def _join_halves(half_rows):
    H, C = half_rows.shape

    def body(src, out, send_sems, recv_sems, local_sem):
        x, y, c, _ = _position()
        rows = pl.ds(pl.multiple_of(c * H, 8), H)
        local = pltpu.make_async_copy(src, out.at[rows], local_sem)
        local.start()
        cp = _remote(src, out.at[rows], send_sems, recv_sems, 0, (x, y, 1 - c))
        cp.start()
        theirs = out.at[pl.ds(pl.multiple_of((1 - c) * H, 8), H)]
        _remote(theirs, theirs, send_sems, recv_sems, 0, (x, y, 1 - c)).wait_recv()
        cp.wait_send()
        local.wait()

    return pl.pallas_call(
        body, name="join_halves",
        in_specs=[ANY], out_specs=ANY,
        out_shape=jax.ShapeDtypeStruct((2 * H, C), half_rows.dtype),
        scratch_shapes=[pltpu.SemaphoreType.DMA((1,)), pltpu.SemaphoreType.DMA((1,)), pltpu.SemaphoreType.DMA],
    )(half_rows)


def _allreduce_small(block):
    R, C = block.shape
    n_dev = 8

    def body(src, out, slots, send_sems, recv_sems):
        x, y, c, _ = _position()
        me = 4 * x + 2 * y + c
        slots[me] = src[...]
        sends = []
        for r in range(1, n_dev):
            peer = (x ^ (r >> 2), y ^ ((r >> 1) & 1), c ^ (r & 1))
            cp = _remote(src, slots.at[me], send_sems, recv_sems, r - 1, peer)
            cp.start()
            sends.append(cp)
        for r in range(1, n_dev):
            theirs = slots.at[me ^ r]
            _remote(theirs, theirs, send_sems, recv_sems, r - 1, (x, y, c)).wait_recv()
        for cp in sends:
            cp.wait_send()
        acc = slots[0]
        for d in range(1, n_dev):
            acc = acc + slots[d]
        out[...] = acc

    return pl.pallas_call(
        body, name="allreduce_small",
        in_specs=[pl.BlockSpec(memory_space=pltpu.VMEM)], out_specs=pl.BlockSpec(memory_space=pltpu.VMEM),
        out_shape=jax.ShapeDtypeStruct((R, C), F32),
        scratch_shapes=[pltpu.VMEM((n_dev, R, C), F32), pltpu.SemaphoreType.DMA((7,)), pltpu.SemaphoreType.DMA((7,))],
    )(block)


def _heads_n(a, T, scale=None):
    S, W = a.shape
    a = a.reshape(S // T, T, W // HEAD_DIM, HEAD_DIM).transpose(2, 0, 1, 3)
    if scale is not None:
        a = a * scale
    return a.astype(BF16)


def _heads_t(a, T, scale=None):
    S, W = a.shape
    a = a.reshape(S // T, T, W // HEAD_DIM, HEAD_DIM).transpose(2, 0, 3, 1)
    if scale is not None:
        a = a * scale
    return a.astype(BF16)


def _heads_n_f32(a, T):
    S, W = a.shape
    return a.reshape(S // T, T, W // HEAD_DIM, HEAD_DIM).transpose(2, 0, 1, 3)


def _from_heads_t(a):
    H, nb, dh, T = a.shape
    return a.transpose(1, 3, 0, 2).reshape(nb * T, H * dh)


def _rel_bucket(dist):
    max_exact = REL_BUCKETS // 2
    d = jnp.maximum(dist, 1).astype(F32)
    large = max_exact + (jnp.log(d / max_exact) / math.log(REL_MAX_DIST / max_exact)
                         * (REL_BUCKETS - max_exact)).astype(jnp.int32)
    large = jnp.minimum(large, REL_BUCKETS - 1)
    return jnp.where(dist < max_exact, dist, large)


def _bucket_onehot():
    T = SWA_BLOCK
    dist = (jnp.arange(T)[:, None] + T) - jnp.arange(2 * T)[None, :]
    bucket = _rel_bucket(jnp.maximum(dist, 0)).reshape(1, T * 2 * T)
    return (bucket == jnp.arange(REL_BUCKETS)[:, None]).astype(BF16)


_BUF = (("ffn1_w1", "t"), ("ffn1_w3", "t"), ("ffn1_w2", "n"), ("ffn2_w1", "t"), ("ffn2_w3", "t"),
        ("ffn2_w2", "n"), ("w_in", "t"), ("w_out", "n"), ("w_branch_swa", "tw"), ("w_branch_sb", "tw"))


def _to_rows(name_kind, w, D):
    kind = name_kind[1]
    if kind == "n":
        return w
    if kind == "t":
        return w.T
    return w.T.reshape(-1, D)


def _from_rows(name_kind, rows, width):
    kind = name_kind[1]
    if kind == "n":
        return rows
    if kind == "t":
        return rows.T
    return rows.reshape(-1, width).T


def kernel(x, norm_ffn1, ffn1_w1, ffn1_w3, ffn1_w2, norm_mix, w_in, swa_sinks, rel_bias, w_branch_swa, w_branch_sb, w_out, norm_ffn2, ffn2_w1, ffn2_w3, ffn2_w2, norm_final, loss_target, m_norm_ffn1, m_ffn1_w1, m_ffn1_w3, m_ffn1_w2, m_norm_mix, m_w_in, m_swa_sinks, m_rel_bias, m_w_branch_swa, m_w_branch_sb, m_w_out, m_norm_ffn2, m_ffn2_w1, m_ffn2_w3, m_ffn2_w2, m_norm_final, v_norm_ffn1, v_ffn1_w1, v_ffn1_w3, v_ffn1_w2, v_norm_mix, v_w_in, v_swa_sinks, v_rel_bias, v_w_branch_swa, v_w_branch_sb, v_w_out, v_norm_ffn2, v_ffn2_w1, v_ffn2_w3, v_ffn2_w2, v_norm_final):
    names = ["norm_ffn1", "ffn1_w1", "ffn1_w3", "ffn1_w2", "norm_mix", "w_in", "swa_sinks", "rel_bias",
             "w_branch_swa", "w_branch_sb", "w_out", "norm_ffn2", "ffn2_w1", "ffn2_w3", "ffn2_w2", "norm_final"]
    W = dict(zip(names, [norm_ffn1, ffn1_w1, ffn1_w3, ffn1_w2, norm_mix, w_in, swa_sinks, rel_bias,
                         w_branch_swa, w_branch_sb, w_out, norm_ffn2, ffn2_w1, ffn2_w3, ffn2_w2, norm_final]))
    M = dict(zip(names, [m_norm_ffn1, m_ffn1_w1, m_ffn1_w3, m_ffn1_w2, m_norm_mix, m_w_in, m_swa_sinks, m_rel_bias,
                         m_w_branch_swa, m_w_branch_sb, m_w_out, m_norm_ffn2, m_ffn2_w1, m_ffn2_w3, m_ffn2_w2,
                         m_norm_final]))
    V = dict(zip(names, [v_norm_ffn1, v_ffn1_w1, v_ffn1_w3, v_ffn1_w2, v_norm_mix, v_w_in, v_swa_sinks, v_rel_bias,
                         v_w_branch_swa, v_w_branch_sb, v_w_out, v_norm_ffn2, v_ffn2_w1, v_ffn2_w3, v_ffn2_w2,
                         v_norm_final]))
    xs = x[0]
    target = loss_target[0]
    S, D = xs.shape
    QW = SWA_Q_HEADS * HEAD_DIM
    KW = SWA_KV_HEADS * HEAD_DIM
    BW = SB_HEADS * HEAD_DIM
    QKV = QW + 2 * KW + 3 * BW

    pieces = [_to_rows(nk, W[nk[0]][0], D) for nk in _BUF]
    sizes = [p.shape[0] for p in pieces]
    offs = [0]
    for s in sizes:
        offs.append(offs[-1] + s)
    shard = jnp.concatenate(pieces, axis=0).astype(BF16)
    gathered = _gather_weights(shard)

    def full(i):
        return gathered[:, offs[i]:offs[i + 1], :].reshape(N_CHIPS * sizes[i], D)

    f1w1, f1w3, f1w2, f2w1, f2w3, f2w2, w_in_t, w_out_f = [full(i) for i in range(8)]
    wa_t = full(8).reshape(D, QW)
    wb_t = full(9).reshape(D, BW)

    g1, gmix, g3 = W["norm_ffn1"], W["norm_mix"], W["norm_ffn2"]
    gf = W["norm_final"].reshape(1, D)

    x1, h1, a1, b1 = _ffn_fwd(xs, g1, f1w1, f1w3, f1w2, "ffn1_fwd")
    qkv, h2 = _norm_matmul_nt(x1, gmix, w_in_t[:QKV], BF16, "proj_qkv")
    gates, _ = _norm_matmul_nt(x1, gmix, w_in_t[QKV:], F32, "proj_gates")

    q_a, k_a, v_a = qkv[:, :QW], qkv[:, QW:QW + KW], qkv[:, QW + KW:QW + 2 * KW]
    o0 = QW + 2 * KW
    q_b, k_b, v_b = qkv[:, o0:o0 + BW], qkv[:, o0 + BW:o0 + 2 * BW], qkv[:, o0 + 2 * BW:o0 + 3 * BW]

    onehot = _bucket_onehot()
    bias = _bias_expand(W["rel_bias"].T, onehot).reshape(SWA_Q_HEADS, SWA_BLOCK, 2 * SWA_BLOCK)
    sinks = W["swa_sinks"].reshape(SWA_Q_HEADS)
    qa_n, qa_t = _heads_n(q_a, SWA_BLOCK, QK_SCALE), _heads_t(q_a, SWA_BLOCK, QK_SCALE)
    ka_t, va_t = _heads_t(k_a, SWA_BLOCK), _heads_t(v_a, SWA_BLOCK)
    oa_t = _swa_fwd(qa_n, ka_t, va_t, bias, sinks)

    qb_n, qb_t = _heads_n(q_b, SB_BLOCK, QK_SCALE), _heads_t(q_b, SB_BLOCK, QK_SCALE)
    kb_t, vb_t = _heads_t(k_b, SB_BLOCK), _heads_t(v_b, SB_BLOCK)
    ob_t, totals = _sb_fwd(qb_n, kb_t, vb_t)

    o_a = _from_heads_t(oa_t)
    o_b = _from_heads_t(ob_t)
    o_a16, o_b16 = o_a.astype(BF16), o_b.astype(BF16)
    x2, merged, ba, bb = _merge_fwd(x1, gates, o_a16, o_b16, wa_t, wb_t, w_out_f)
    x3, h3, a2, b2 = _ffn_fwd(x2, g3, f2w1, f2w3, f2w2, "ffn2_fwd")
    loss_part, dx3, dgf = _final_loss(x3, gf, target)

    dx2, dg3, dz2, da2, db2, u2 = _ffn_bwd(dx3, x2, g3, a2, b2, f2w1, f2w3, f2w2, "ffn2_bwd")
    grads = {}
    grads["ffn2_w1"] = _tn_matmul(da2, h3, "ffn2_dw1")
    grads["ffn2_w3"] = _tn_matmul(db2, h3, "ffn2_dw3")
    grads["ffn2_w2"] = _tn_matmul(u2, dz2, "ffn2_dw2")

    dx2b, dba, dbb, dgates, do_a, do_b = _merge_bwd(dx2, gates, ba, bb, wa_t, wb_t, w_out_f)
    grads["w_out"] = _tn_matmul(merged, dx2b, "dw_out")
    grads["w_branch_swa"] = _tn_matmul(dba, o_a16, "dw_branch_swa")
    grads["w_branch_sb"] = _tn_matmul(dbb, o_b16, "dw_branch_sb")

    dqb_t, dkb_t, dvb_t = _sb_bwd(qb_n, qb_t, kb_t, vb_t, _heads_n(do_b, SB_BLOCK), _heads_t(do_b, SB_BLOCK), totals)
    dqa_t, dka_t, dva_t, dbias, dsink_rows = _swa_bwd(
        qa_n, qa_t, ka_t, va_t, bias, sinks, _heads_n_f32(do_a, SWA_BLOCK), _heads_t(do_a, SWA_BLOCK),
        _heads_n_f32(o_a, SWA_BLOCK))
    d_rel = _bias_reduce(dbias.reshape(SWA_Q_HEADS, -1), onehot).T
    d_sinks = jnp.sum(dsink_rows, axis=(1, 2))

    dqkv = jnp.concatenate([_from_heads_t(dqa_t) * QK_SCALE, _from_heads_t(dka_t), _from_heads_t(dva_t),
                            _from_heads_t(dqb_t) * QK_SCALE, _from_heads_t(dkb_t), _from_heads_t(dvb_t)],
                           axis=1).astype(BF16)
    dproj = jnp.concatenate([dqkv, dgates], axis=1)
    grads["w_in"] = _tn_matmul(dproj, h2, "dw_in")
    dx1, dgmix = _matmul_norm_bwd(dproj, w_in_t, x1, gmix, dx2, "proj_bwd")

    dx0, dg1, dz1, da1, db1, u1 = _ffn_bwd(dx1, xs, g1, a1, b1, f1w1, f1w3, f1w2, "ffn1_bwd")
    grads["ffn1_w1"] = _tn_matmul(da1, h1, "ffn1_dw1")
    grads["ffn1_w3"] = _tn_matmul(db1, h1, "ffn1_dw3")
    grads["ffn1_w2"] = _tn_matmul(u1, dz1, "ffn1_dw2")

    gparts = [grads[nk[0]].reshape(N_CHIPS, sizes[i], D) for i, nk in enumerate(_BUF)]
    gbuf = jnp.concatenate(gparts, axis=1)
    R = gbuf.shape[1]
    half = R // 2
    c = lax.axis_index("c")
    mine = 2 * lax.axis_index("x") + lax.axis_index("y")
    from_sibling = _swap_halves(gbuf)
    my_half = lax.dynamic_slice_in_dim(gbuf, c * half, half, axis=1)
    chip_sum, chip_sum16 = _add_halves(my_half, from_sibling, "add_sibling")
    received = _scatter_to_owners(chip_sum16)
    own = lax.dynamic_index_in_dim(chip_sum, mine, axis=0, keepdims=False)
    reduced = _join_halves(_add_received(own, received, "add_chips"))

    small_rows = [dg1, dgmix, dg3, dgf,
                  jnp.pad(d_sinks.reshape(1, -1), ((0, 0), (0, D - SWA_Q_HEADS))),
                  jnp.pad(d_rel.reshape(1, -1), ((0, 0), (0, D - REL_BUCKETS * SWA_Q_HEADS))),
                  jnp.pad(loss_part, ((0, 0), (0, D - 1))), jnp.zeros((1, D), F32)]
    small = _allreduce_small(jnp.concatenate(small_rows, axis=0))
    loss = small[6, 0]

    G = {}
    for i, nk in enumerate(_BUF):
        G[nk[0]] = _from_rows(nk, reduced[offs[i]:offs[i + 1]], W[nk[0]].shape[1])[None]
    G["norm_ffn1"], G["norm_mix"], G["norm_ffn2"] = small[0:1], small[1:2], small[2:3]
    G["norm_final"] = small[3]
    G["swa_sinks"] = small[4:5, :SWA_Q_HEADS]
    G["rel_bias"] = small[5, :REL_BUCKETS * SWA_Q_HEADS].reshape(REL_BUCKETS, SWA_Q_HEADS)

    delta, new_m, new_v = {}, {}, {}
    small_names = ["norm_ffn1", "norm_mix", "norm_ffn2", "norm_final", "swa_sinks", "rel_bias"]

    def pack(d):
        return jnp.concatenate([jnp.pad(d[n].reshape(1, -1), ((0, 0), (0, D - d[n].size))) for n in small_names]
                               + [jnp.zeros((2, D), F32)], axis=0)

    sd, sm, sv = _adamw(pack(W), pack(G), pack(M), pack(V), "adamw_small")
    for r, n in enumerate(small_names):
        for dst, src in ((delta, sd), (new_m, sm), (new_v, sv)):
            dst[n] = src[r, :W[n].size].reshape(W[n].shape)
    for nk in _BUF:
        n = nk[0]
        shp = W[n].shape
        two_d = (shp[1], shp[2])
        d_, m_, v_ = _adamw(W[n].reshape(two_d), G[n].reshape(two_d), M[n].reshape(two_d), V[n].reshape(two_d),
                            "adamw_" + n)
        delta[n], new_m[n], new_v[n] = d_.reshape(shp), m_.reshape(shp), v_.reshape(shp)

    return (loss, dx0[None], *[G[n] for n in names], *[delta[n] for n in names],
            *[new_m[n] for n in names], *[new_v[n] for n in names])
```

```python
import functools
import math

import jax
import jax.numpy as jnp
from jax import lax
from jax.experimental import pallas as pl
from jax.experimental.pallas import tpu as pltpu

F32, BF16 = jnp.float32, jnp.bfloat16
MESH_ID = pl.DeviceIdType.MESH
ANY = pl.BlockSpec(memory_space=pl.ANY)

RMS_EPS = 1e-6
HEAD_DIM = 64
SWA_Q_HEADS, SWA_KV_HEADS, SWA_GROUP = 8, 2, 4
SWA_BLOCK = 128
SB_HEADS = 8
SB_BLOCK = 256
REL_BUCKETS, REL_MAX_DIST = 32, 128
NEG_BIG = -1e30
QK_SCALE = HEAD_DIM ** -0.5
ADAM_LR, ADAM_B1, ADAM_B2, ADAM_EPS, ADAM_WD, ADAM_STEP = 0.001, 0.9, 0.999, 1e-08, 0.01, 10

N_CHIPS = 4
TOKEN_TILE = 512
FF_TILE = 256
VMEM_LIMIT = 48 * 1024 * 1024


def _cp(*sem):
    return pltpu.CompilerParams(dimension_semantics=sem, vmem_limit_bytes=VMEM_LIMIT)


def _nn(a, b):
    return jnp.dot(a, b, preferred_element_type=F32)


def _nt(a, b):
    return lax.dot_general(a, b, (((1,), (1,)), ((), ())), preferred_element_type=F32)


def _tn(a, b):
    return lax.dot_general(a, b, (((0,), (0,)), ((), ())), preferred_element_type=F32)


def _norm_fwd(x, g):
    return x * lax.rsqrt(jnp.mean(x * x, axis=-1, keepdims=True) + RMS_EPS) * g


def _norm_bwd(x, g, dh):
    r = lax.rsqrt(jnp.mean(x * x, axis=-1, keepdims=True) + RMS_EPS)
    xh = x * r
    dxh = dh * g
    dx = r * (dxh - xh * jnp.mean(dxh * xh, axis=-1, keepdims=True))
    return dx, jnp.sum(dh * xh, axis=0, keepdims=True)


def _hilo_nn(v, tri):
    hi = v.astype(BF16)
    lo = (v - hi.astype(F32)).astype(BF16)
    return _nn(hi, tri) + _nn(lo, tri)


def _softplus(z):
    return jnp.maximum(z, 0.0) + jnp.log(1.0 + jnp.exp(-jnp.abs(z)))


def _ffn_fwd(x, g, w1t, w3t, w2, name):
    S, D = x.shape
    F = w2.shape[0]
    tm, tf = min(TOKEN_TILE, S), FF_TILE
    nj = F // tf

    def body(x_ref, g_ref, w1_ref, w3_ref, w2_ref, xo_ref, h_ref, a_ref, b_ref, hs, acc):
        j = pl.program_id(1)

        @pl.when(j == 0)
        def _():
            hb = _norm_fwd(x_ref[...], g_ref[...]).astype(BF16)
            hs[...] = hb
            h_ref[...] = hb
            acc[...] = jnp.zeros_like(acc)

        h = hs[...]
        a = _nt(h, w1_ref[...])
        b = _nt(h, w3_ref[...])
        a_ref[...] = a.astype(BF16)
        b_ref[...] = b.astype(BF16)
        u = a * jax.nn.sigmoid(a) * b
        acc[...] += _nn(u.astype(BF16), w2_ref[...])

        @pl.when(j == nj - 1)
        def _():
            xo_ref[...] = x_ref[...] + 0.5 * acc[...]

    return pl.pallas_call(
        body, name=name, grid=(S // tm, nj),
        in_specs=[pl.BlockSpec((tm, D), lambda i, j: (i, 0)),
                  pl.BlockSpec((1, D), lambda i, j: (0, 0)),
                  pl.BlockSpec((tf, D), lambda i, j: (j, 0)),
                  pl.BlockSpec((tf, D), lambda i, j: (j, 0)),
                  pl.BlockSpec((tf, D), lambda i, j: (j, 0))],
        out_specs=[pl.BlockSpec((tm, D), lambda i, j: (i, 0)),
                   pl.BlockSpec((tm, D), lambda i, j: (i, 0)),
                   pl.BlockSpec((tm, tf), lambda i, j: (i, j)),
                   pl.BlockSpec((tm, tf), lambda i, j: (i, j))],
        out_shape=[jax.ShapeDtypeStruct((S, D), F32), jax.ShapeDtypeStruct((S, D), BF16),
                   jax.ShapeDtypeStruct((S, F), BF16), jax.ShapeDtypeStruct((S, F), BF16)],
        scratch_shapes=[pltpu.VMEM((tm, D), BF16), pltpu.VMEM((tm, D), F32)],
        compiler_params=_cp("arbitrary", "arbitrary"),
    )(x, g, w1t, w3t, w2)


def _ffn_bwd(dxo, x, g, a, b, w1t, w3t, w2, name):
    S, D = x.shape
    F = w2.shape[0]
    tm, tf = min(TOKEN_TILE, S), FF_TILE
    ni, nj = S // tm, F // tf

    def body(dxo_ref, x_ref, g_ref, a_ref, b_ref, w1_ref, w3_ref, w2_ref,
             dx_ref, dg_ref, dz_ref, da_ref, db_ref, u_ref, dzs, acc):
        i, j = pl.program_id(0), pl.program_id(1)

        @pl.when(j == 0)
        def _():
            dzb = (0.5 * dxo_ref[...]).astype(BF16)
            dzs[...] = dzb
            dz_ref[...] = dzb
            acc[...] = jnp.zeros_like(acc)

        du = _nt(dzs[...], w2_ref[...])
        av = a_ref[...].astype(F32)
        bv = b_ref[...].astype(F32)
        s = jax.nn.sigmoid(av)
        silu = av * s
        db = (du * silu).astype(BF16)
        da = (du * bv * (s * (1.0 + av * (1.0 - s)))).astype(BF16)
        da_ref[...] = da
        db_ref[...] = db
        u_ref[...] = (silu * bv).astype(BF16)
        acc[...] += _nn(da, w1_ref[...]) + _nn(db, w3_ref[...])

        @pl.when(j == nj - 1)
        def _():
            dx, dg = _norm_bwd(x_ref[...], g_ref[...], acc[...])
            dx_ref[...] = dxo_ref[...] + dx

            @pl.when(i == 0)
            def _():
                dg_ref[...] = dg

            @pl.when(i > 0)
            def _():
                dg_ref[...] += dg

    row = pl.BlockSpec((tm, D), lambda i, j: (i, 0))
    wsp = pl.BlockSpec((tf, D), lambda i, j: (j, 0))
    col = pl.BlockSpec((tm, tf), lambda i, j: (i, j))
    vec = pl.BlockSpec((1, D), lambda i, j: (0, 0))
    return pl.pallas_call(
        body, name=name, grid=(ni, nj),
        in_specs=[row, row, vec, col, col, wsp, wsp, wsp],
        out_specs=[row, vec, row, col, col, col],
        out_shape=[jax.ShapeDtypeStruct((S, D), F32), jax.ShapeDtypeStruct((1, D), F32),
                   jax.ShapeDtypeStruct((S, D), BF16), jax.ShapeDtypeStruct((S, F), BF16),
                   jax.ShapeDtypeStruct((S, F), BF16), jax.ShapeDtypeStruct((S, F), BF16)],
        scratch_shapes=[pltpu.VMEM((tm, D), BF16), pltpu.VMEM((tm, D), F32)],
        compiler_params=_cp("arbitrary", "arbitrary"),
    )(dxo, x, g, a, b, w1t, w3t, w2)


def _tn_matmul(a, b, name):
    S, M = a.shape
    N = b.shape[1]
    ts = min(TOKEN_TILE, S)
    tmm = 256 if M % 256 == 0 else M
    ns = S // ts

    def body(a_ref, b_ref, o_ref):
        s = pl.program_id(1)
        part = _tn(a_ref[...], b_ref[...])

        @pl.when(s == 0)
        def _():
            o_ref[...] = part

        @pl.when(s > 0)
        def _():
            o_ref[...] += part

    return pl.pallas_call(
        body, name=name, grid=(M // tmm, ns),
        in_specs=[pl.BlockSpec((ts, tmm), lambda m, s: (s, m)),
                  pl.BlockSpec((ts, N), lambda m, s: (s, 0))],
        out_specs=pl.BlockSpec((tmm, N), lambda m, s: (m, 0)),
        out_shape=jax.ShapeDtypeStruct((M, N), F32),
        compiler_params=_cp("arbitrary", "arbitrary"),
    )(a, b)


def _norm_matmul_nt(x, g, wt, out_dtype, name):
    S, D = x.shape
    N = wt.shape[0]
    tm, tn = min(TOKEN_TILE, S), 256

    def body(x_ref, g_ref, w_ref, o_ref, h_ref, hs):
        @pl.when(pl.program_id(1) == 0)
        def _():
            hb = _norm_fwd(x_ref[...], g_ref[...]).astype(BF16)
            hs[...] = hb
            h_ref[...] = hb

        o_ref[...] = _nt(hs[...], w_ref[...]).astype(out_dtype)

    return pl.pallas_call(
        body, name=name, grid=(S // tm, N // tn),
        in_specs=[pl.BlockSpec((tm, D), lambda i, j: (i, 0)),
                  pl.BlockSpec((1, D), lambda i, j: (0, 0)),
                  pl.BlockSpec((tn, D), lambda i, j: (j, 0))],
        out_specs=[pl.BlockSpec((tm, tn), lambda i, j: (i, j)),
                   pl.BlockSpec((tm, D), lambda i, j: (i, 0))],
        out_shape=[jax.ShapeDtypeStruct((S, N), out_dtype), jax.ShapeDtypeStruct((S, D), BF16)],
        scratch_shapes=[pltpu.VMEM((tm, D), BF16)],
        compiler_params=_cp("arbitrary", "arbitrary"),
    )(x, g, wt)


def _matmul_norm_bwd(dy, w, x, g, dres, name):
    S, K = dy.shape
    D = w.shape[1]
    tm, tk = min(TOKEN_TILE, S), 256
    nk = K // tk

    def body(dy_ref, w_ref, x_ref, g_ref, dres_ref, dx_ref, dg_ref, acc):
        i, k = pl.program_id(0), pl.program_id(1)
        part = _nn(dy_ref[...], w_ref[...])

        @pl.when(k == 0)
        def _():
            acc[...] = part

        @pl.when(k > 0)
        def _():
            acc[...] += part

        @pl.when(k == nk - 1)
        def _():
            dx, dg = _norm_bwd(x_ref[...], g_ref[...], acc[...])
            dx_ref[...] = dres_ref[...] + dx

            @pl.when(i == 0)
            def _():
                dg_ref[...] = dg

            @pl.when(i > 0)
            def _():
                dg_ref[...] += dg

    row = pl.BlockSpec((tm, D), lambda i, k: (i, 0))
    vec = pl.BlockSpec((1, D), lambda i, k: (0, 0))
    return pl.pallas_call(
        body, name=name, grid=(S // tm, nk),
        in_specs=[pl.BlockSpec((tm, tk), lambda i, k: (i, k)),
                  pl.BlockSpec((tk, D), lambda i, k: (k, 0)), row, vec, row],
        out_specs=[row, vec],
        out_shape=[jax.ShapeDtypeStruct((S, D), F32), jax.ShapeDtypeStruct((1, D), F32)],
        scratch_shapes=[pltpu.VMEM((tm, D), F32)],
        compiler_params=_cp("arbitrary", "arbitrary"),
    )(dy, w, x, g, dres)


def _merge_fwd(x1, gates, o_a, o_b, wat, wbt, w_out):
    S, D = x1.shape
    W = o_a.shape[1]
    tm = min(TOKEN_TILE, S)

    def body(x_ref, ga_ref, gb_ref, oa_ref, ob_ref, wa_ref, wb_ref, wo_ref,
             x2_ref, mg_ref, ba_ref, bb_ref):
        ba = _nt(oa_ref[...], wa_ref[...])
        bb = _nt(ob_ref[...], wb_ref[...])
        merged = jax.nn.sigmoid(ga_ref[...]) * ba + jax.nn.sigmoid(gb_ref[...]) * bb
        mb = merged.astype(BF16)
        mg_ref[...] = mb
        ba_ref[...] = ba.astype(BF16)
        bb_ref[...] = bb.astype(BF16)
        x2_ref[...] = x_ref[...] + _nn(mb, wo_ref[...])

    row = pl.BlockSpec((tm, D), lambda i: (i, 0))
    full = lambda r, c: pl.BlockSpec((r, c), lambda i: (0, 0))
    return pl.pallas_call(
        body, name="merge_fwd", grid=(S // tm,),
        in_specs=[row, pl.BlockSpec((tm, D), lambda i: (i, 0)), pl.BlockSpec((tm, D), lambda i: (i, 1)),
                  pl.BlockSpec((tm, W), lambda i: (i, 0)), pl.BlockSpec((tm, W), lambda i: (i, 0)),
                  full(D, W), full(D, W), full(D, D)],
        out_specs=[row, row, row, row],
        out_shape=[jax.ShapeDtypeStruct((S, D), F32)] + [jax.ShapeDtypeStruct((S, D), BF16)] * 3,
        compiler_params=_cp("arbitrary"),
    )(x1, gates, gates, o_a, o_b, wat, wbt, w_out)


def _merge_bwd(dx2, gates, ba, bb, wat, wbt, w_out):
    S, D = dx2.shape
    W = wat.shape[1]
    tm = min(TOKEN_TILE, S)

    def body(dx_ref, ga_ref, gb_ref, ba_ref, bb_ref, wa_ref, wb_ref, wo_ref,
             dxb_ref, dba_ref, dbb_ref, dgt_ref, doa_ref, dob_ref):
        dxb = dx_ref[...].astype(BF16)
        dxb_ref[...] = dxb
        dm = _nt(dxb, wo_ref[...])
        sa = jax.nn.sigmoid(ga_ref[...])
        sb = jax.nn.sigmoid(gb_ref[...])
        dba = (dm * sa).astype(BF16)
        dbb = (dm * sb).astype(BF16)
        dba_ref[...] = dba
        dbb_ref[...] = dbb
        dgt_ref[:, :D] = (dm * ba_ref[...].astype(F32) * sa * (1.0 - sa)).astype(BF16)
        dgt_ref[:, D:] = (dm * bb_ref[...].astype(F32) * sb * (1.0 - sb)).astype(BF16)
        doa_ref[...] = _nn(dba, wa_ref[...])
        dob_ref[...] = _nn(dbb, wb_ref[...])

    row = pl.BlockSpec((tm, D), lambda i: (i, 0))
    full = lambda r, c: pl.BlockSpec((r, c), lambda i: (0, 0))
    return pl.pallas_call(
        body, name="merge_bwd", grid=(S // tm,),
        in_specs=[row, pl.BlockSpec((tm, D), lambda i: (i, 0)), pl.BlockSpec((tm, D), lambda i: (i, 1)),
                  row, row, full(D, W), full(D, W), full(D, D)],
        out_specs=[row, row, row, pl.BlockSpec((tm, 2 * D), lambda i: (i, 0)),
                   pl.BlockSpec((tm, W), lambda i: (i, 0)), pl.BlockSpec((tm, W), lambda i: (i, 0))],
        out_shape=[jax.ShapeDtypeStruct((S, D), BF16)] * 3 + [jax.ShapeDtypeStruct((S, 2 * D), BF16)]
                  + [jax.ShapeDtypeStruct((S, W), F32)] * 2,
        compiler_params=_cp("arbitrary"),
    )(dx2, gates, gates, ba, bb, wat, wbt, w_out)


def _final_loss(x3, gf, target):
    S, D = x3.shape
    tm = min(TOKEN_TILE, S)

    def body(x_ref, g_ref, t_ref, loss_ref, dx_ref, dg_ref):
        i = pl.program_id(0)
        x = x_ref[...]
        g = g_ref[...]
        e = _norm_fwd(x, g) - t_ref[...]
        part = 0.5 * jnp.sum(jnp.mean(e * e, axis=-1, keepdims=True), axis=0, keepdims=True)
        dx, dg = _norm_bwd(x, g, e * (1.0 / D))
        dx_ref[...] = dx

        @pl.when(i == 0)
        def _():
            loss_ref[...] = part
            dg_ref[...] = dg

        @pl.when(i > 0)
        def _():
            loss_ref[...] += part
            dg_ref[...] += dg

    row = pl.BlockSpec((tm, D), lambda i: (i, 0))
    vec = pl.BlockSpec((1, D), lambda i: (0, 0))
    return pl.pallas_call(
        body, name="final_loss", grid=(S // tm,),
        in_specs=[row, vec, row],
        out_specs=[pl.BlockSpec((1, 1), lambda i: (0, 0)), row, vec],
        out_shape=[jax.ShapeDtypeStruct((1, 1), F32), jax.ShapeDtypeStruct((S, D), F32),
                   jax.ShapeDtypeStruct((1, D), F32)],
        compiler_params=_cp("arbitrary"),
    )(x3, gf, target)


SB_HEAD_GROUP = 2
LANES = 128


def _tri(T, kind):
    r = lax.broadcasted_iota(jnp.int32, (T, T), 0)
    c = lax.broadcasted_iota(jnp.int32, (T, T), 1)
    return {"after": r > c, "upto": r <= c, "before": r < c}[kind].astype(BF16)


def _lane(v, j):
    return jnp.broadcast_to(v[:, j:j + 1], (v.shape[0], LANES))


def _t_bf16(x):
    return x.astype(F32).T.astype(BF16)


def _wide(v, T):
    return jnp.tile(v, (1, T // LANES))


def _sb_fwd(qt, kt, vt):
    H, nb, dh, T = qt.shape
    HG = SB_HEAD_GROUP

    def body(q_ref, k_ref, v_ref, o_ref, tl_ref):
        row = lax.broadcasted_iota(jnp.int32, (T, T), 0)
        col = lax.broadcasted_iota(jnp.int32, (T, T), 1)
        tri = col < row
        after = _tri(T, "after")

        def blocks(qs, kb, carry, diag):
            hs = range(HG)
            z = [_nn(qs[hh], k_ref[hh, kb]) for hh in hs]
            res, ls, first = [None] * HG, [None] * HG, [None] * HG
            for hh in hs:
                sp = _softplus(z[hh])
                if diag:
                    sp = jnp.where(tri, sp, 0.0)
                ls[hh] = z[hh] - sp
                first[hh] = _lane(sp, 0)
                res[hh] = _hilo_nn(sp, after)
            out = []
            for hh in hs:
                c, oacc = carry[2 * hh], carry[2 * hh + 1]
                a = jnp.exp(ls[hh] - (res[hh] + _wide(c, T)))
                if diag:
                    a = jnp.where(tri, a, 0.0)
                out.extend([c + (first[hh] + _lane(res[hh], 0)), oacc + _nt(v_ref[hh, kb], a.astype(BF16))])
            return tuple(out)

        def qblock(i, _):
            qs = [_t_bf16(q_ref[hh, i]) for hh in range(HG)]
            carry = blocks(qs, i, (jnp.zeros((T, LANES), F32), jnp.zeros((dh, T), F32)) * HG, True)

            def kstep(t, carry):
                return blocks(qs, i - 1 - t, carry, False)

            carry = lax.fori_loop(0, i, kstep, carry)
            for hh in range(HG):
                o_ref[hh, i] = carry[2 * hh + 1]
                tl_ref[hh, i] = carry[2 * hh].T[:8]
            return 0

        lax.fori_loop(0, nb, qblock, 0)

    ht = pl.BlockSpec((HG, nb, dh, T), lambda h: (h, 0, 0, 0))
    return pl.pallas_call(
        body, name="sb_fwd", grid=(H // HG,),
        in_specs=[ht, ht, ht],
        out_specs=[ht, pl.BlockSpec((HG, nb, 8, T), lambda h: (h, 0, 0, 0))],
        out_shape=[jax.ShapeDtypeStruct((H, nb, dh, T), F32), jax.ShapeDtypeStruct((H, nb, 8, T), F32)],
        compiler_params=_cp("arbitrary"),
    )(qt, kt, vt)


def _sb_bwd(qt, kt, vt, dot, tl):
    H, nb, dh, T = qt.shape
    HG = SB_HEAD_GROUP

    def body(qt_ref, k_ref, v_ref, dot_ref, tl_ref, dq_ref, dk_ref, dv_ref):
        row = lax.broadcasted_iota(jnp.int32, (T, T), 0)
        col = lax.broadcasted_iota(jnp.int32, (T, T), 1)
        tri = col < row
        upto = _tri(T, "upto")
        before = _tri(T, "before")
        dk_ref[...] = jnp.zeros_like(dk_ref)
        dv_ref[...] = jnp.zeros_like(dv_ref)

        def blocks(qs, qTs, dos, doTs, kb, carry, diag):
            hs = range(HG)
            kT = [k_ref[hh, kb] for hh in hs]
            z = [_nn(qs[hh], kT[hh]) for hh in hs]
            da = [_nn(dos[hh], v_ref[hh, kb]) for hh in hs]
            res, ls = [None] * HG, [None] * HG
            for hh in hs:
                sp = _softplus(z[hh])
                if diag:
                    sp = jnp.where(tri, sp, 0.0)
                ls[hh] = z[hh] - sp
                res[hh] = _hilo_nn(sp, upto)
            g, ab, resg = [None] * HG, [None] * HG, [None] * HG
            for hh in hs:
                a = jnp.exp(ls[hh] + (res[hh] - _wide(carry[3 * hh], T)))
                if diag:
                    a = jnp.where(tri, a, 0.0)
                ab[hh] = a.astype(BF16)
                g[hh] = a * da[hh]
                resg[hh] = _hilo_nn(g[hh], before)
            out = []
            for hh in hs:
                rem, pre_g, dq = carry[3 * hh:3 * hh + 3]
                dz = g[hh] - (g[hh] + (resg[hh] + _wide(pre_g, T))) * jnp.exp(ls[hh])
                if diag:
                    dz = jnp.where(tri, dz, 0.0)
                dzb = dz.astype(BF16)
                dk_ref[hh, kb] += _nn(qTs[hh], dzb)
                dv_ref[hh, kb] += _nn(doTs[hh], ab[hh])
                out.extend([rem - _lane(res[hh], T - 1), pre_g + (_lane(resg[hh], T - 1) + _lane(g[hh], T - 1)),
                            dq + _nt(kT[hh], dzb)])
            return tuple(out)

        def qblock(i, _):
            qTs = [qt_ref[hh, i] for hh in range(HG)]
            doTs = [dot_ref[hh, i] for hh in range(HG)]
            qs = [_t_bf16(v) for v in qTs]
            dos = [_t_bf16(v) for v in doTs]
            carry = []
            for hh in range(HG):
                total = jnp.broadcast_to(tl_ref[hh, i][0:1], (LANES, T)).T
                carry.extend([total, jnp.zeros((T, LANES), F32), jnp.zeros((dh, T), F32)])

            def kstep(kb, carry):
                return blocks(qs, qTs, dos, doTs, kb, carry, False)

            carry = lax.fori_loop(0, i, kstep, tuple(carry))
            carry = blocks(qs, qTs, dos, doTs, i, carry, True)
            for hh in range(HG):
                dq_ref[hh, i] = carry[3 * hh + 2]
            return 0

        lax.fori_loop(0, nb, qblock, 0)

    ht = pl.BlockSpec((HG, nb, dh, T), lambda h: (h, 0, 0, 0))
    return pl.pallas_call(
        body, name="sb_bwd", grid=(H // HG,),
        in_specs=[ht, ht, ht, ht, pl.BlockSpec((HG, nb, 8, T), lambda h: (h, 0, 0, 0))],
        out_specs=[ht, ht, ht],
        out_shape=[jax.ShapeDtypeStruct((H, nb, dh, T), F32)] * 3,
        compiler_params=_cp("arbitrary"),
    )(qt, kt, vt, dot, tl)


def _swa_probs(q, kp, kc, bias, sink, first):
    T = q.shape[0]
    row = lax.broadcasted_iota(jnp.int32, (T, T), 0)
    col = lax.broadcasted_iota(jnp.int32, (T, T), 1)
    lp = jnp.where(jnp.logical_and(col > row, jnp.logical_not(first)), _nn(q, kp) + bias[:, :T], NEG_BIG)
    lc = jnp.where(col <= row, _nn(q, kc) + bias[:, T:], NEG_BIG)
    m = jnp.maximum(jnp.maximum(jnp.max(lp, axis=1, keepdims=True), jnp.max(lc, axis=1, keepdims=True)), sink)
    pp = jnp.exp(lp - m)
    pc = jnp.exp(lc - m)
    ps = jnp.exp(sink - m)
    inv = 1.0 / (jnp.sum(pp, axis=1, keepdims=True) + jnp.sum(pc, axis=1, keepdims=True) + ps)
    return pp * inv, pc * inv, ps * inv


def _swa_fwd(qn, kt, vt, bias, sinks):
    Hq, nb, T, dh = qn.shape
    grp = Hq // kt.shape[0]

    def body(sink_ref, q_ref, kp_ref, kc_ref, vp_ref, vc_ref, bias_ref, o_ref):
        h, n = pl.program_id(0), pl.program_id(1)
        pp, pc, _ = _swa_probs(q_ref[...], kp_ref[...], kc_ref[...], bias_ref[...], sink_ref[h], n == 0)
        o_ref[...] = _nt(vp_ref[...], pp.astype(BF16)) + _nt(vc_ref[...], pc.astype(BF16))

    prev = pl.BlockSpec((None, None, dh, T), lambda h, n: (h // grp, jnp.maximum(n - 1, 0), 0, 0))
    cur = pl.BlockSpec((None, None, dh, T), lambda h, n: (h // grp, n, 0, 0))
    return pl.pallas_call(
        body, name="swa_fwd", grid=(Hq, nb),
        in_specs=[pl.BlockSpec(memory_space=pltpu.SMEM),
                  pl.BlockSpec((None, None, T, dh), lambda h, n: (h, n, 0, 0)),
                  prev, cur, prev, cur,
                  pl.BlockSpec((None, T, 2 * T), lambda h, n: (h, 0, 0))],
        out_specs=pl.BlockSpec((None, None, dh, T), lambda h, n: (h, n, 0, 0)),
        out_shape=jax.ShapeDtypeStruct((Hq, nb, dh, T), F32),
        compiler_params=_cp("arbitrary", "arbitrary"),
    )(sinks, qn, kt, kt, vt, vt, bias)


def _swa_bwd(qn, qt, kt, vt, bias, sinks, don, dot, on):
    Hq, nb, T, dh = qn.shape
    Hkv = kt.shape[0]
    grp = Hq // Hkv

    def body(sink_ref, q_ref, qt_ref, kp_ref, kc_ref, vp_ref, vc_ref, bias_ref, do_ref, dot_ref, o_ref,
             dq_ref, dk_ref, dv_ref, dbias_ref, dsink_ref, ck, cv):
        hk, n = pl.program_id(0), pl.program_id(1)

        @pl.when(n == 0)
        def _():
            dbias_ref[...] = jnp.zeros_like(dbias_ref)
            dsink_ref[...] = jnp.zeros_like(dsink_ref)
            ck[...] = jnp.zeros_like(ck)
            cv[...] = jnp.zeros_like(cv)

        @pl.when(n < nb)
        def _():
            kp, kc, vp, vc = kp_ref[...], kc_ref[...], vp_ref[...], vc_ref[...]
            kprev = jnp.zeros((dh, T), F32)
            vprev = jnp.zeros((dh, T), F32)
            kcur = jnp.zeros((dh, T), F32)
            vcur = jnp.zeros((dh, T), F32)
            for g in range(grp):
                pp, pc, ps = _swa_probs(q_ref[g], kp, kc, bias_ref[g], sink_ref[hk * grp + g], n == 0)
                do = do_ref[g]
                dob = do.astype(BF16)
                delta = jnp.sum(do * o_ref[g], axis=1, keepdims=True)
                dlp = pp * (_nn(dob, vp) - delta)
                dlc = pc * (_nn(dob, vc) - delta)
                dbias_ref[g, :, :T] += dlp
                dbias_ref[g, :, T:] += dlc
                dsink_ref[g] += -ps * delta
                dlpb, dlcb = dlp.astype(BF16), dlc.astype(BF16)
                dq_ref[g] = _nt(kp, dlpb) + _nt(kc, dlcb)
                qT, doT = qt_ref[g], dot_ref[g]
                kprev += _nn(qT, dlpb)
                kcur += _nn(qT, dlcb)
                vprev += _nn(doT, pp.astype(BF16))
                vcur += _nn(doT, pc.astype(BF16))
            dk_ref[...] = ck[...] + kprev
            dv_ref[...] = cv[...] + vprev
            ck[...] = kcur
            cv[...] = vcur

        @pl.when(n == nb)
        def _():
            dk_ref[...] = ck[...]
            dv_ref[...] = cv[...]

    qn_spec = pl.BlockSpec((grp, None, T, dh), lambda h, n: (h, jnp.minimum(n, nb - 1), 0, 0))
    qt_spec = pl.BlockSpec((grp, None, dh, T), lambda h, n: (h, jnp.minimum(n, nb - 1), 0, 0))
    prev = pl.BlockSpec((None, None, dh, T), lambda h, n: (h, jnp.maximum(n - 1, 0), 0, 0))
    cur = pl.BlockSpec((None, None, dh, T), lambda h, n: (h, jnp.minimum(n, nb - 1), 0, 0))
    per_group = lambda a, b: pl.BlockSpec((grp, a, b), lambda h, n: (h, 0, 0))
    return pl.pallas_call(
        body, name="swa_bwd", grid=(Hkv, nb + 1),
        in_specs=[pl.BlockSpec(memory_space=pltpu.SMEM), qn_spec, qt_spec, prev, cur, prev, cur,
                  per_group(T, 2 * T), qn_spec, qt_spec, qn_spec],
        out_specs=[qt_spec, prev, prev, per_group(T, 2 * T), per_group(T, 1)],
        out_shape=[jax.ShapeDtypeStruct((Hq, nb, dh, T), F32), jax.ShapeDtypeStruct((Hkv, nb, dh, T), F32),
                   jax.ShapeDtypeStruct((Hkv, nb, dh, T), F32), jax.ShapeDtypeStruct((Hq, T, 2 * T), F32),
                   jax.ShapeDtypeStruct((Hq, T, 1), F32)],
        scratch_shapes=[pltpu.VMEM((dh, T), F32), pltpu.VMEM((dh, T), F32)],
        compiler_params=_cp("arbitrary", "arbitrary"),
    )(sinks, qn, qt, kt, kt, vt, vt, bias, don, dot, on)


def _split3(x):
    h1 = x.astype(BF16)
    r1 = x - h1.astype(F32)
    h2 = r1.astype(BF16)
    h3 = (r1 - h2.astype(F32)).astype(BF16)
    return h1, h2, h3


def _bias_expand(rel_t, onehot):
    Hq, NB = rel_t.shape
    L = onehot.shape[1]

    def body(r_ref, oh_ref, o_ref):
        h1, h2, h3 = _split3(r_ref[...])
        oh = oh_ref[...]
        o_ref[...] = _nn(h1, oh) + _nn(h2, oh) + _nn(h3, oh)

    return pl.pallas_call(
        body, name="bias_expand", grid=(1,),
        in_specs=[pl.BlockSpec((Hq, NB), lambda i: (0, 0)), pl.BlockSpec((NB, L), lambda i: (0, 0))],
        out_specs=pl.BlockSpec((Hq, L), lambda i: (0, 0)),
        out_shape=jax.ShapeDtypeStruct((Hq, L), F32),
        compiler_params=_cp("arbitrary"),
    )(rel_t, onehot)


def _bias_reduce(dbias, onehot):
    Hq, L = dbias.shape
    NB = onehot.shape[0]

    def body(d_ref, oh_ref, o_ref):
        h1, h2, h3 = _split3(d_ref[...])
        oh = oh_ref[...]
        o_ref[...] = _nt(h1, oh) + _nt(h2, oh) + _nt(h3, oh)

    return pl.pallas_call(
        body, name="bias_reduce", grid=(1,),
        in_specs=[pl.BlockSpec((Hq, L), lambda i: (0, 0)), pl.BlockSpec((NB, L), lambda i: (0, 0))],
        out_specs=pl.BlockSpec((Hq, NB), lambda i: (0, 0)),
        out_shape=jax.ShapeDtypeStruct((Hq, NB), F32),
        compiler_params=_cp("arbitrary"),
    )(dbias, onehot)


def _adamw(w, g, m, v, name):
    R, C = w.shape
    tr = 256 if R % 256 == 0 else R
    bc1 = 1.0 - ADAM_B1 ** ADAM_STEP
    bc2 = 1.0 - ADAM_B2 ** ADAM_STEP

    def body(w_ref, g_ref, m_ref, v_ref, d_ref, nm_ref, nv_ref):
        g = g_ref[...]
        m2 = ADAM_B1 * m_ref[...] + (1.0 - ADAM_B1) * g
        v2 = ADAM_B2 * v_ref[...] + (1.0 - ADAM_B2) * (g * g)
        nm_ref[...] = m2
        nv_ref[...] = v2
        d_ref[...] = -ADAM_LR * ((m2 / bc1) / (jnp.sqrt(v2 / bc2) + ADAM_EPS) + ADAM_WD * w_ref[...])

    spec = pl.BlockSpec((tr, C), lambda i: (i, 0))
    return pl.pallas_call(
        body, name=name, grid=(R // tr,),
        in_specs=[spec] * 4, out_specs=[spec] * 3,
        out_shape=[jax.ShapeDtypeStruct((R, C), F32)] * 3,
        compiler_params=_cp("arbitrary"),
    )(w, g, m, v)


def _add_halves(mine, recv, name):
    K, R, C = mine.shape
    tr = 416 if R % 416 == 0 else R

    def body(a_ref, b_ref, o_ref, ob_ref):
        s = a_ref[...] + b_ref[...]
        o_ref[...] = s
        ob_ref[...] = s.astype(BF16)

    spec = pl.BlockSpec((None, tr, C), lambda k, i: (k, i, 0))
    return pl.pallas_call(
        body, name=name, grid=(K, R // tr),
        in_specs=[spec, spec], out_specs=[spec, spec],
        out_shape=[jax.ShapeDtypeStruct((K, R, C), F32), jax.ShapeDtypeStruct((K, R, C), BF16)],
        compiler_params=_cp("arbitrary", "arbitrary"),
    )(mine, recv)


def _add_received(own, recv, name):
    R, C = own.shape
    tr = 416 if R % 416 == 0 else R

    def body(a_ref, r_ref, o_ref):
        o_ref[...] = ((a_ref[...] + r_ref[0].astype(F32)) + r_ref[1].astype(F32)) + r_ref[2].astype(F32)

    return pl.pallas_call(
        body, name=name, grid=(R // tr,),
        in_specs=[pl.BlockSpec((tr, C), lambda i: (i, 0)), pl.BlockSpec((3, tr, C), lambda i: (0, i, 0))],
        out_specs=pl.BlockSpec((tr, C), lambda i: (i, 0)),
        out_shape=jax.ShapeDtypeStruct((R, C), F32),
        compiler_params=_cp("arbitrary"),
    )(own, recv)


def _position():
    x, y, c = lax.axis_index("x"), lax.axis_index("y"), lax.axis_index("c")
    others = [(1 - x, y), (x, 1 - y), (1 - x, 1 - y)]
    return x, y, c, others


def _remote(src, dst, send_sems, recv_sems, k, dev):
    return pltpu.make_async_remote_copy(src_ref=src, dst_ref=dst, send_sem=send_sems.at[k],
                                        recv_sem=recv_sems.at[k], device_id=dev, device_id_type=MESH_ID)


def _gather_weights(shard):
    R, C = shard.shape
    half = R // 2

    def body(src, out, send_sems, recv_sems, local_sem):
        x, y, c, others = _position()
        mine = 2 * x + y
        rows = pl.ds(pl.multiple_of(c * half, 16), half)
        other_rows = pl.ds(pl.multiple_of((1 - c) * half, 16), half)
        local = pltpu.make_async_copy(src, out.at[mine], local_sem)
        local.start()
        sends = [_remote(src.at[rows], out.at[mine, rows], send_sems, recv_sems, j, (ox, oy, c))
                 for j, (ox, oy) in enumerate(others)]
        for cp in sends:
            cp.start()
        passed = []
        for j, (ox, oy) in enumerate(others):
            slot = out.at[2 * ox + oy, rows]
            _remote(slot, slot, send_sems, recv_sems, j, (ox, oy, c)).wait_recv()
            fwd = _remote(slot, slot, send_sems, recv_sems, 3 + j, (x, y, 1 - c))
            fwd.start()
            passed.append(fwd)
        for j, (ox, oy) in enumerate(others):
            slot = out.at[2 * ox + oy, other_rows]
            _remote(slot, slot, send_sems, recv_sems, 3 + j, (x, y, 1 - c)).wait_recv()
        for cp in sends + passed:
            cp.wait_send()
        local.wait()

    return pl.pallas_call(
        body, name="gather_weights",
        in_specs=[ANY], out_specs=ANY,
        out_shape=jax.ShapeDtypeStruct((N_CHIPS, R, C), shard.dtype),
        scratch_shapes=[pltpu.SemaphoreType.DMA((6,)), pltpu.SemaphoreType.DMA((6,)), pltpu.SemaphoreType.DMA],
    )(shard)


def _swap_halves(grads):
    K, R, C = grads.shape
    half = R // 2

    def body(src, out, send_sems, recv_sems):
        x, y, c, _ = _position()
        theirs = src.at[:, pl.ds(pl.multiple_of((1 - c) * half, 8), half), :]
        cp = _remote(theirs, out, send_sems, recv_sems, 0, (x, y, 1 - c))
        cp.start()
        cp.wait()

    return pl.pallas_call(
        body, name="swap_halves",
        in_specs=[ANY], out_specs=ANY,
        out_shape=jax.ShapeDtypeStruct((K, half, C), grads.dtype),
        scratch_shapes=[pltpu.SemaphoreType.DMA((1,)), pltpu.SemaphoreType.DMA((1,))],
    )(grads)


def _scatter_to_owners(parts):
    K, H, C = parts.shape

    def body(src, out, send_sems, recv_sems):
        x, y, c, others = _position()
        sends = [_remote(src.at[2 * ox + oy], out.at[j], send_sems, recv_sems, j, (ox, oy, c))
                 for j, (ox, oy) in enumerate(others)]
        for cp in sends:
            cp.start()
        for cp in sends:
            cp.wait()

    return pl.pallas_call(
        body, name="scatter_to_owners",
        in_specs=[ANY], out_specs=ANY,
        out_shape=jax.ShapeDtypeStruct((3, H, C), parts.dtype),
        scratch_shapes=[pltpu.SemaphoreType.DMA((3,)), pltpu.SemaphoreType.DMA((3,))],
    )(parts)


def _join_halves(half_rows):
    H, C = half_rows.shape

    def body(src, out, send_sems, recv_sems, local_sem):
        x, y, c, _ = _position()
        rows = pl.ds(pl.multiple_of(c * H, 8), H)
        local = pltpu.make_async_copy(src, out.at[rows], local_sem)
        local.start()
        cp = _remote(src, out.at[rows], send_sems, recv_sems, 0, (x, y, 1 - c))
        cp.start()
        theirs = out.at[pl.ds(pl.multiple_of((1 - c) * H, 8), H)]
        _remote(theirs, theirs, send_sems, recv_sems, 0, (x, y, 1 - c)).wait_recv()
        cp.wait_send()
        local.wait()

    return pl.pallas_call(
        body, name="join_halves",
        in_specs=[ANY], out_specs=ANY,
        out_shape=jax.ShapeDtypeStruct((2 * H, C), half_rows.dtype),
        scratch_shapes=[pltpu.SemaphoreType.DMA((1,)), pltpu.SemaphoreType.DMA((1,)), pltpu.SemaphoreType.DMA],
    )(half_rows)


def _allreduce_small(block):
    R, C = block.shape
    n_dev = 8

    def body(src, out, slots, send_sems, recv_sems):
        x, y, c, _ = _position()
        me = 4 * x + 2 * y + c
        slots[me] = src[...]
        sends = []
        for r in range(1, n_dev):
            peer = (x ^ (r >> 2), y ^ ((r >> 1) & 1), c ^ (r & 1))
            cp = _remote(src, slots.at[me], send_sems, recv_sems, r - 1, peer)
            cp.start()
            sends.append(cp)
        for r in range(1, n_dev):
            theirs = slots.at[me ^ r]
            _remote(theirs, theirs, send_sems, recv_sems, r - 1, (x, y, c)).wait_recv()
        for cp in sends:
            cp.wait_send()
        acc = slots[0]
        for d in range(1, n_dev):
            acc = acc + slots[d]
        out[...] = acc

    return pl.pallas_call(
        body, name="allreduce_small",
        in_specs=[pl.BlockSpec(memory_space=pltpu.VMEM)], out_specs=pl.BlockSpec(memory_space=pltpu.VMEM),
        out_shape=jax.ShapeDtypeStruct((R, C), F32),
        scratch_shapes=[pltpu.VMEM((n_dev, R, C), F32), pltpu.SemaphoreType.DMA((7,)), pltpu.SemaphoreType.DMA((7,))],
    )(block)


def _heads_n(a, T, scale=None):
    S, W = a.shape
    a = a.reshape(S // T, T, W // HEAD_DIM, HEAD_DIM).transpose(2, 0, 1, 3)
    if scale is not None:
        a = a * scale
    return a.astype(BF16)


def _heads_t(a, T, scale=None):
    S, W = a.shape
    a = a.reshape(S // T, T, W // HEAD_DIM, HEAD_DIM).transpose(2, 0, 3, 1)
    if scale is not None:
        a = a * scale
    return a.astype(BF16)


def _heads_n_f32(a, T):
    S, W = a.shape
    return a.reshape(S // T, T, W // HEAD_DIM, HEAD_DIM).transpose(2, 0, 1, 3)


def _from_heads_t(a):
    H, nb, dh, T = a.shape
    return a.transpose(1, 3, 0, 2).reshape(nb * T, H * dh)


def _rel_bucket(dist):
    max_exact = REL_BUCKETS // 2
    d = jnp.maximum(dist, 1).astype(F32)
    large = max_exact + (jnp.log(d / max_exact) / math.log(REL_MAX_DIST / max_exact)
                         * (REL_BUCKETS - max_exact)).astype(jnp.int32)
    large = jnp.minimum(large, REL_BUCKETS - 1)
    return jnp.where(dist < max_exact, dist, large)


def _bucket_onehot():
    T = SWA_BLOCK
    dist = (jnp.arange(T)[:, None] + T) - jnp.arange(2 * T)[None, :]
    bucket = _rel_bucket(jnp.maximum(dist, 0)).reshape(1, T * 2 * T)
    return (bucket == jnp.arange(REL_BUCKETS)[:, None]).astype(BF16)


_BUF = (("ffn1_w1", "t"), ("ffn1_w3", "t"), ("ffn1_w2", "n"), ("ffn2_w1", "t"), ("ffn2_w3", "t"),
        ("ffn2_w2", "n"), ("w_in", "t"), ("w_out", "n"), ("w_branch_swa", "tw"), ("w_branch_sb", "tw"))


def _to_rows(name_kind, w, D):
    kind = name_kind[1]
    if kind == "n":
        return w
    if kind == "t":
        return w.T
    return w.T.reshape(-1, D)


def _from_rows(name_kind, rows, width):
    kind = name_kind[1]
    if kind == "n":
        return rows
    if kind == "t":
        return rows.T
    return rows.reshape(-1, width).T


def kernel(x, norm_ffn1, ffn1_w1, ffn1_w3, ffn1_w2, norm_mix, w_in, swa_sinks, rel_bias, w_branch_swa, w_branch_sb, w_out, norm_ffn2, ffn2_w1, ffn2_w3, ffn2_w2, norm_final, loss_target, m_norm_ffn1, m_ffn1_w1, m_ffn1_w3, m_ffn1_w2, m_norm_mix, m_w_in, m_swa_sinks, m_rel_bias, m_w_branch_swa, m_w_branch_sb, m_w_out, m_norm_ffn2, m_ffn2_w1, m_ffn2_w3, m_ffn2_w2, m_norm_final, v_norm_ffn1, v_ffn1_w1, v_ffn1_w3, v_ffn1_w2, v_norm_mix, v_w_in, v_swa_sinks, v_rel_bias, v_w_branch_swa, v_w_branch_sb, v_w_out, v_norm_ffn2, v_ffn2_w1, v_ffn2_w3, v_ffn2_w2, v_norm_final):
    names = ["norm_ffn1", "ffn1_w1", "ffn1_w3", "ffn1_w2", "norm_mix", "w_in", "swa_sinks", "rel_bias",
             "w_branch_swa", "w_branch_sb", "w_out", "norm_ffn2", "ffn2_w1", "ffn2_w3", "ffn2_w2", "norm_final"]
    W = dict(zip(names, [norm_ffn1, ffn1_w1, ffn1_w3, ffn1_w2, norm_mix, w_in, swa_sinks, rel_bias,
                         w_branch_swa, w_branch_sb, w_out, norm_ffn2, ffn2_w1, ffn2_w3, ffn2_w2, norm_final]))
    M = dict(zip(names, [m_norm_ffn1, m_ffn1_w1, m_ffn1_w3, m_ffn1_w2, m_norm_mix, m_w_in, m_swa_sinks, m_rel_bias,
                         m_w_branch_swa, m_w_branch_sb, m_w_out, m_norm_ffn2, m_ffn2_w1, m_ffn2_w3, m_ffn2_w2,
                         m_norm_final]))
    V = dict(zip(names, [v_norm_ffn1, v_ffn1_w1, v_ffn1_w3, v_ffn1_w2, v_norm_mix, v_w_in, v_swa_sinks, v_rel_bias,
                         v_w_branch_swa, v_w_branch_sb, v_w_out, v_norm_ffn2, v_ffn2_w1, v_ffn2_w3, v_ffn2_w2,
                         v_norm_final]))
    xs = x[0]
    target = loss_target[0]
    S, D = xs.shape
    QW = SWA_Q_HEADS * HEAD_DIM
    KW = SWA_KV_HEADS * HEAD_DIM
    BW = SB_HEADS * HEAD_DIM
    QKV = QW + 2 * KW + 3 * BW

    pieces = [_to_rows(nk, W[nk[0]][0], D) for nk in _BUF]
    sizes = [p.shape[0] for p in pieces]
    offs = [0]
    for s in sizes:
        offs.append(offs[-1] + s)
    shard = jnp.concatenate(pieces, axis=0).astype(BF16)
    gathered = _gather_weights(shard)

    def full(i):
        return gathered[:, offs[i]:offs[i + 1], :].reshape(N_CHIPS * sizes[i], D)

    f1w1, f1w3, f1w2, f2w1, f2w3, f2w2, w_in_t, w_out_f = [full(i) for i in range(8)]
    wa_t = full(8).reshape(D, QW)
    wb_t = full(9).reshape(D, BW)

    g1, gmix, g3 = W["norm_ffn1"], W["norm_mix"], W["norm_ffn2"]
    gf = W["norm_final"].reshape(1, D)

    x1, h1, a1, b1 = _ffn_fwd(xs, g1, f1w1, f1w3, f1w2, "ffn1_fwd")
    qkv, h2 = _norm_matmul_nt(x1, gmix, w_in_t[:QKV], BF16, "proj_qkv")
    gates, _ = _norm_matmul_nt(x1, gmix, w_in_t[QKV:], F32, "proj_gates")

    q_a, k_a, v_a = qkv[:, :QW], qkv[:, QW:QW + KW], qkv[:, QW + KW:QW + 2 * KW]
    o0 = QW + 2 * KW
    q_b, k_b, v_b = qkv[:, o0:o0 + BW], qkv[:, o0 + BW:o0 + 2 * BW], qkv[:, o0 + 2 * BW:o0 + 3 * BW]

    onehot = _bucket_onehot()
    bias = _bias_expand(W["rel_bias"].T, onehot).reshape(SWA_Q_HEADS, SWA_BLOCK, 2 * SWA_BLOCK)
    sinks = W["swa_sinks"].reshape(SWA_Q_HEADS)
    qa_n, qa_t = _heads_n(q_a, SWA_BLOCK, QK_SCALE), _heads_t(q_a, SWA_BLOCK, QK_SCALE)
    ka_t, va_t = _heads_t(k_a, SWA_BLOCK), _heads_t(v_a, SWA_BLOCK)
    oa_t = _swa_fwd(qa_n, ka_t, va_t, bias, sinks)

    qb_t = _heads_t(q_b, SB_BLOCK, QK_SCALE)
    kb_t, vb_t = _heads_t(k_b, SB_BLOCK), _heads_t(v_b, SB_BLOCK)
    ob_t, totals = _sb_fwd(qb_t, kb_t, vb_t)

    o_a = _from_heads_t(oa_t)
    o_b = _from_heads_t(ob_t)
    o_a16, o_b16 = o_a.astype(BF16), o_b.astype(BF16)
    x2, merged, ba, bb = _merge_fwd(x1, gates, o_a16, o_b16, wa_t, wb_t, w_out_f)
    x3, h3, a2, b2 = _ffn_fwd(x2, g3, f2w1, f2w3, f2w2, "ffn2_fwd")
    loss_part, dx3, dgf = _final_loss(x3, gf, target)

    dx2, dg3, dz2, da2, db2, u2 = _ffn_bwd(dx3, x2, g3, a2, b2, f2w1, f2w3, f2w2, "ffn2_bwd")
    grads = {}
    grads["ffn2_w1"] = _tn_matmul(da2, h3, "ffn2_dw1")
    grads["ffn2_w3"] = _tn_matmul(db2, h3, "ffn2_dw3")
    grads["ffn2_w2"] = _tn_matmul(u2, dz2, "ffn2_dw2")

    dx2b, dba, dbb, dgates, do_a, do_b = _merge_bwd(dx2, gates, ba, bb, wa_t, wb_t, w_out_f)
    grads["w_out"] = _tn_matmul(merged, dx2b, "dw_out")
    grads["w_branch_swa"] = _tn_matmul(dba, o_a16, "dw_branch_swa")
    grads["w_branch_sb"] = _tn_matmul(dbb, o_b16, "dw_branch_sb")

    dqb_t, dkb_t, dvb_t = _sb_bwd(qb_t, kb_t, vb_t, _heads_t(do_b, SB_BLOCK), totals)
    dqa_t, dka_t, dva_t, dbias, dsink_rows = _swa_bwd(
        qa_n, qa_t, ka_t, va_t, bias, sinks, _heads_n_f32(do_a, SWA_BLOCK), _heads_t(do_a, SWA_BLOCK),
        _heads_n_f32(o_a, SWA_BLOCK))
    d_rel = _bias_reduce(dbias.reshape(SWA_Q_HEADS, -1), onehot).T
    d_sinks = jnp.sum(dsink_rows, axis=(1, 2))

    dqkv = jnp.concatenate([_from_heads_t(dqa_t) * QK_SCALE, _from_heads_t(dka_t), _from_heads_t(dva_t),
                            _from_heads_t(dqb_t) * QK_SCALE, _from_heads_t(dkb_t), _from_heads_t(dvb_t)],
                           axis=1).astype(BF16)
    dproj = jnp.concatenate([dqkv, dgates], axis=1)
    grads["w_in"] = _tn_matmul(dproj, h2, "dw_in")
    dx1, dgmix = _matmul_norm_bwd(dproj, w_in_t, x1, gmix, dx2, "proj_bwd")

    dx0, dg1, dz1, da1, db1, u1 = _ffn_bwd(dx1, xs, g1, a1, b1, f1w1, f1w3, f1w2, "ffn1_bwd")
    grads["ffn1_w1"] = _tn_matmul(da1, h1, "ffn1_dw1")
    grads["ffn1_w3"] = _tn_matmul(db1, h1, "ffn1_dw3")
    grads["ffn1_w2"] = _tn_matmul(u1, dz1, "ffn1_dw2")

    gparts = [grads[nk[0]].reshape(N_CHIPS, sizes[i], D) for i, nk in enumerate(_BUF)]
    gbuf = jnp.concatenate(gparts, axis=1)
    R = gbuf.shape[1]
    half = R // 2
    c = lax.axis_index("c")
    mine = 2 * lax.axis_index("x") + lax.axis_index("y")
    from_sibling = _swap_halves(gbuf)
    my_half = lax.dynamic_slice_in_dim(gbuf, c * half, half, axis=1)
    chip_sum, chip_sum16 = _add_halves(my_half, from_sibling, "add_sibling")
    received = _scatter_to_owners(chip_sum16)
    own = lax.dynamic_index_in_dim(chip_sum, mine, axis=0, keepdims=False)
    reduced = _join_halves(_add_received(own, received, "add_chips"))

    small_rows = [dg1, dgmix, dg3, dgf,
                  jnp.pad(d_sinks.reshape(1, -1), ((0, 0), (0, D - SWA_Q_HEADS))),
                  jnp.pad(d_rel.reshape(1, -1), ((0, 0), (0, D - REL_BUCKETS * SWA_Q_HEADS))),
                  jnp.pad(loss_part, ((0, 0), (0, D - 1))), jnp.zeros((1, D), F32)]
    small = _allreduce_small(jnp.concatenate(small_rows, axis=0))
    loss = small[6, 0]

    G = {}
    for i, nk in enumerate(_BUF):
        G[nk[0]] = _from_rows(nk, reduced[offs[i]:offs[i + 1]], W[nk[0]].shape[1])[None]
    G["norm_ffn1"], G["norm_mix"], G["norm_ffn2"] = small[0:1], small[1:2], small[2:3]
    G["norm_final"] = small[3]
    G["swa_sinks"] = small[4:5, :SWA_Q_HEADS]
    G["rel_bias"] = small[5, :REL_BUCKETS * SWA_Q_HEADS].reshape(REL_BUCKETS, SWA_Q_HEADS)

    delta, new_m, new_v = {}, {}, {}
    small_names = ["norm_ffn1", "norm_mix", "norm_ffn2", "norm_final", "swa_sinks", "rel_bias"]

    def pack(d):
        return jnp.concatenate([jnp.pad(d[n].reshape(1, -1), ((0, 0), (0, D - d[n].size))) for n in small_names]
                               + [jnp.zeros((2, D), F32)], axis=0)

    sd, sm, sv = _adamw(pack(W), pack(G), pack(M), pack(V), "adamw_small")
    for r, n in enumerate(small_names):
        for dst, src in ((delta, sd), (new_m, sm), (new_v, sv)):
            dst[n] = src[r, :W[n].size].reshape(W[n].shape)
    for nk in _BUF:
        n = nk[0]
        shp = W[n].shape
        two_d = (shp[1], shp[2])
        d_, m_, v_ = _adamw(W[n].reshape(two_d), G[n].reshape(two_d), M[n].reshape(two_d), V[n].reshape(two_d),
                            "adamw_" + n)
        delta[n], new_m[n], new_v[n] = d_.reshape(shp), m_.reshape(shp), v_.reshape(shp)

    return (loss, dx0[None], *[G[n] for n in names], *[delta[n] for n in names],
            *[new_m[n] for n in names], *[new_v[n] for n in names])
```

```python
import functools
import math

import jax
import jax.numpy as jnp
from jax import lax
from jax.experimental import pallas as pl
from jax.experimental.pallas import tpu as pltpu

F32, BF16 = jnp.float32, jnp.bfloat16
MESH_ID = pl.DeviceIdType.MESH
ANY = pl.BlockSpec(memory_space=pl.ANY)

RMS_EPS = 1e-6
HEAD_DIM = 64
SWA_Q_HEADS, SWA_KV_HEADS, SWA_GROUP = 8, 2, 4
SWA_BLOCK = 128
SB_HEADS = 8
SB_BLOCK = 256
REL_BUCKETS, REL_MAX_DIST = 32, 128
NEG_BIG = -1e30
QK_SCALE = HEAD_DIM ** -0.5
ADAM_LR, ADAM_B1, ADAM_B2, ADAM_EPS, ADAM_WD, ADAM_STEP = 0.001, 0.9, 0.999, 1e-08, 0.01, 10

N_CHIPS = 4
TOKEN_TILE = 512
MATMUL_TOKEN_TILE = 1024
WGRAD_ROW_TILES = (2176, 1408, 1024, 256)
FF_TILE = 256
VMEM_LIMIT = 56 * 1024 * 1024


def _cp(*sem):
    return pltpu.CompilerParams(dimension_semantics=sem, vmem_limit_bytes=VMEM_LIMIT)


def _nn(a, b):
    return jnp.dot(a, b, preferred_element_type=F32)


def _nt(a, b):
    return lax.dot_general(a, b, (((1,), (1,)), ((), ())), preferred_element_type=F32)


def _tn(a, b):
    return lax.dot_general(a, b, (((0,), (0,)), ((), ())), preferred_element_type=F32)


def _norm_fwd(x, g):
    return x * lax.rsqrt(jnp.mean(x * x, axis=-1, keepdims=True) + RMS_EPS) * g


def _norm_bwd(x, g, dh):
    r = lax.rsqrt(jnp.mean(x * x, axis=-1, keepdims=True) + RMS_EPS)
    xh = x * r
    dxh = dh * g
    dx = r * (dxh - xh * jnp.mean(dxh * xh, axis=-1, keepdims=True))
    return dx, jnp.sum(dh * xh, axis=0, keepdims=True)


def _hilo_nn(v, tri):
    hi = v.astype(BF16)
    lo = (v - hi.astype(F32)).astype(BF16)
    return _nn(hi, tri) + _nn(lo, tri)


def _softplus(z):
    return jnp.maximum(z, 0.0) + jnp.log(1.0 + jnp.exp(-jnp.abs(z)))


def _ffn_fwd(x, g, w1t, w3t, w2, name):
    S, D = x.shape
    F = w2.shape[0]
    tm, tf = min(MATMUL_TOKEN_TILE, S), FF_TILE
    nj = F // tf

    def body(x_ref, g_ref, w1_ref, w3_ref, w2_ref, xo_ref, h_ref, a_ref, b_ref, hs, acc):
        j = pl.program_id(1)

        @pl.when(j == 0)
        def _():
            hb = _norm_fwd(x_ref[...], g_ref[...]).astype(BF16)
            hs[...] = hb
            h_ref[...] = hb
            acc[...] = jnp.zeros_like(acc)

        h = hs[...]
        a = _nt(h, w1_ref[...])
        b = _nt(h, w3_ref[...])
        a_ref[...] = a.astype(BF16)
        b_ref[...] = b.astype(BF16)
        u = a * jax.nn.sigmoid(a) * b
        acc[...] += _nn(u.astype(BF16), w2_ref[...])

        @pl.when(j == nj - 1)
        def _():
            xo_ref[...] = x_ref[...] + 0.5 * acc[...]

    return pl.pallas_call(
        body, name=name, grid=(S // tm, nj),
        in_specs=[pl.BlockSpec((tm, D), lambda i, j: (i, 0)),
                  pl.BlockSpec((1, D), lambda i, j: (0, 0)),
                  pl.BlockSpec((tf, D), lambda i, j: (j, 0)),
                  pl.BlockSpec((tf, D), lambda i, j: (j, 0)),
                  pl.BlockSpec((tf, D), lambda i, j: (j, 0))],
        out_specs=[pl.BlockSpec((tm, D), lambda i, j: (i, 0)),
                   pl.BlockSpec((tm, D), lambda i, j: (i, 0)),
                   pl.BlockSpec((tm, tf), lambda i, j: (i, j)),
                   pl.BlockSpec((tm, tf), lambda i, j: (i, j))],
        out_shape=[jax.ShapeDtypeStruct((S, D), F32), jax.ShapeDtypeStruct((S, D), BF16),
                   jax.ShapeDtypeStruct((S, F), BF16), jax.ShapeDtypeStruct((S, F), BF16)],
        scratch_shapes=[pltpu.VMEM((tm, D), BF16), pltpu.VMEM((tm, D), F32)],
        compiler_params=_cp("arbitrary", "arbitrary"),
    )(x, g, w1t, w3t, w2)


def _ffn_bwd(dxo, x, g, a, b, w1t, w3t, w2, name):
    S, D = x.shape
    F = w2.shape[0]
    tm, tf = min(MATMUL_TOKEN_TILE, S), FF_TILE
    ni, nj = S // tm, F // tf

    def body(dxo_ref, x_ref, g_ref, a_ref, b_ref, w1_ref, w3_ref, w2_ref,
             dx_ref, dg_ref, dz_ref, da_ref, db_ref, u_ref, dzs, acc):
        i, j = pl.program_id(0), pl.program_id(1)

        @pl.when(j == 0)
        def _():
            dzb = (0.5 * dxo_ref[...]).astype(BF16)
            dzs[...] = dzb
            dz_ref[...] = dzb
            acc[...] = jnp.zeros_like(acc)

        du = _nt(dzs[...], w2_ref[...])
        av = a_ref[...].astype(F32)
        bv = b_ref[...].astype(F32)
        s = jax.nn.sigmoid(av)
        silu = av * s
        db = (du * silu).astype(BF16)
        da = (du * bv * (s * (1.0 + av * (1.0 - s)))).astype(BF16)
        da_ref[...] = da
        db_ref[...] = db
        u_ref[...] = (silu * bv).astype(BF16)
        acc[...] += _nn(da, w1_ref[...]) + _nn(db, w3_ref[...])

        @pl.when(j == nj - 1)
        def _():
            dx, dg = _norm_bwd(x_ref[...], g_ref[...], acc[...])
            dx_ref[...] = dxo_ref[...] + dx

            @pl.when(i == 0)
            def _():
                dg_ref[...] = dg

            @pl.when(i > 0)
            def _():
                dg_ref[...] += dg

    row = pl.BlockSpec((tm, D), lambda i, j: (i, 0))
    wsp = pl.BlockSpec((tf, D), lambda i, j: (j, 0))
    col = pl.BlockSpec((tm, tf), lambda i, j: (i, j))
    vec = pl.BlockSpec((1, D), lambda i, j: (0, 0))
    return pl.pallas_call(
        body, name=name, grid=(ni, nj),
        in_specs=[row, row, vec, col, col, wsp, wsp, wsp],
        out_specs=[row, vec, row, col, col, col],
        out_shape=[jax.ShapeDtypeStruct((S, D), F32), jax.ShapeDtypeStruct((1, D), F32),
                   jax.ShapeDtypeStruct((S, D), BF16), jax.ShapeDtypeStruct((S, F), BF16),
                   jax.ShapeDtypeStruct((S, F), BF16), jax.ShapeDtypeStruct((S, F), BF16)],
        scratch_shapes=[pltpu.VMEM((tm, D), BF16), pltpu.VMEM((tm, D), F32)],
        compiler_params=_cp("arbitrary", "arbitrary"),
    )(dxo, x, g, a, b, w1t, w3t, w2)


def _tn_matmul(a, b, name):
    S, M = a.shape
    N = b.shape[1]
    ts = min(MATMUL_TOKEN_TILE, S)
    tmm = next(t for t in WGRAD_ROW_TILES if M % t == 0)
    ns = S // ts

    def body(a_ref, b_ref, o_ref):
        s = pl.program_id(1)
        part = _tn(a_ref[...], b_ref[...])

        @pl.when(s == 0)
        def _():
            o_ref[...] = part

        @pl.when(s > 0)
        def _():
            o_ref[...] += part

    return pl.pallas_call(
        body, name=name, grid=(M // tmm, ns),
        in_specs=[pl.BlockSpec((ts, tmm), lambda m, s: (s, m)),
                  pl.BlockSpec((ts, N), lambda m, s: (s, 0))],
        out_specs=pl.BlockSpec((tmm, N), lambda m, s: (m, 0)),
        out_shape=jax.ShapeDtypeStruct((M, N), F32),
        compiler_params=_cp("arbitrary", "arbitrary"),
    )(a, b)


def _norm_matmul_nt(x, g, wt, out_dtype, name):
    S, D = x.shape
    N = wt.shape[0]
    tm, tn = min(MATMUL_TOKEN_TILE, S), 256

    def body(x_ref, g_ref, w_ref, o_ref, h_ref, hs):
        @pl.when(pl.program_id(1) == 0)
        def _():
            hb = _norm_fwd(x_ref[...], g_ref[...]).astype(BF16)
            hs[...] = hb
            h_ref[...] = hb

        o_ref[...] = _nt(hs[...], w_ref[...]).astype(out_dtype)

    return pl.pallas_call(
        body, name=name, grid=(S // tm, N // tn),
        in_specs=[pl.BlockSpec((tm, D), lambda i, j: (i, 0)),
                  pl.BlockSpec((1, D), lambda i, j: (0, 0)),
                  pl.BlockSpec((tn, D), lambda i, j: (j, 0))],
        out_specs=[pl.BlockSpec((tm, tn), lambda i, j: (i, j)),
                   pl.BlockSpec((tm, D), lambda i, j: (i, 0))],
        out_shape=[jax.ShapeDtypeStruct((S, N), out_dtype), jax.ShapeDtypeStruct((S, D), BF16)],
        scratch_shapes=[pltpu.VMEM((tm, D), BF16)],
        compiler_params=_cp("arbitrary", "arbitrary"),
    )(x, g, wt)


def _matmul_norm_bwd(dy, w, x, g, dres, name):
    S, K = dy.shape
    D = w.shape[1]
    tm, tk = min(MATMUL_TOKEN_TILE, S), 256
    nk = K // tk

    def body(dy_ref, w_ref, x_ref, g_ref, dres_ref, dx_ref, dg_ref, acc):
        i, k = pl.program_id(0), pl.program_id(1)
        part = _nn(dy_ref[...], w_ref[...])

        @pl.when(k == 0)
        def _():
            acc[...] = part

        @pl.when(k > 0)
        def _():
            acc[...] += part

        @pl.when(k == nk - 1)
        def _():
            dx, dg = _norm_bwd(x_ref[...], g_ref[...], acc[...])
            dx_ref[...] = dres_ref[...] + dx

            @pl.when(i == 0)
            def _():
                dg_ref[...] = dg

            @pl.when(i > 0)
            def _():
                dg_ref[...] += dg

    row = pl.BlockSpec((tm, D), lambda i, k: (i, 0))
    vec = pl.BlockSpec((1, D), lambda i, k: (0, 0))
    return pl.pallas_call(
        body, name=name, grid=(S // tm, nk),
        in_specs=[pl.BlockSpec((tm, tk), lambda i, k: (i, k)),
                  pl.BlockSpec((tk, D), lambda i, k: (k, 0)), row, vec, row],
        out_specs=[row, vec],
        out_shape=[jax.ShapeDtypeStruct((S, D), F32), jax.ShapeDtypeStruct((1, D), F32)],
        scratch_shapes=[pltpu.VMEM((tm, D), F32)],
        compiler_params=_cp("arbitrary", "arbitrary"),
    )(dy, w, x, g, dres)


def _merge_fwd(x1, gates, o_a, o_b, wat, wbt, w_out):
    S, D = x1.shape
    W = o_a.shape[1]
    tm = min(TOKEN_TILE, S)

    def body(x_ref, ga_ref, gb_ref, oa_ref, ob_ref, wa_ref, wb_ref, wo_ref,
             x2_ref, mg_ref, ba_ref, bb_ref):
        ba = _nt(oa_ref[...], wa_ref[...])
        bb = _nt(ob_ref[...], wb_ref[...])
        merged = jax.nn.sigmoid(ga_ref[...]) * ba + jax.nn.sigmoid(gb_ref[...]) * bb
        mb = merged.astype(BF16)
        mg_ref[...] = mb
        ba_ref[...] = ba.astype(BF16)
        bb_ref[...] = bb.astype(BF16)
        x2_ref[...] = x_ref[...] + _nn(mb, wo_ref[...])

    row = pl.BlockSpec((tm, D), lambda i: (i, 0))
    full = lambda r, c: pl.BlockSpec((r, c), lambda i: (0, 0))
    return pl.pallas_call(
        body, name="merge_fwd", grid=(S // tm,),
        in_specs=[row, pl.BlockSpec((tm, D), lambda i: (i, 0)), pl.BlockSpec((tm, D), lambda i: (i, 1)),
                  pl.BlockSpec((tm, W), lambda i: (i, 0)), pl.BlockSpec((tm, W), lambda i: (i, 0)),
                  full(D, W), full(D, W), full(D, D)],
        out_specs=[row, row, row, row],
        out_shape=[jax.ShapeDtypeStruct((S, D), F32)] + [jax.ShapeDtypeStruct((S, D), BF16)] * 3,
        compiler_params=_cp("arbitrary"),
    )(x1, gates, gates, o_a, o_b, wat, wbt, w_out)


def _merge_bwd(dx2, gates, ba, bb, wat, wbt, w_out):
    S, D = dx2.shape
    W = wat.shape[1]
    tm = min(TOKEN_TILE, S)

    def body(dx_ref, ga_ref, gb_ref, ba_ref, bb_ref, wa_ref, wb_ref, wo_ref,
             dxb_ref, dba_ref, dbb_ref, dgt_ref, doa_ref, dob_ref):
        dxb = dx_ref[...].astype(BF16)
        dxb_ref[...] = dxb
        dm = _nt(dxb, wo_ref[...])
        sa = jax.nn.sigmoid(ga_ref[...])
        sb = jax.nn.sigmoid(gb_ref[...])
        dba = (dm * sa).astype(BF16)
        dbb = (dm * sb).astype(BF16)
        dba_ref[...] = dba
        dbb_ref[...] = dbb
        dgt_ref[:, :D] = (dm * ba_ref[...].astype(F32) * sa * (1.0 - sa)).astype(BF16)
        dgt_ref[:, D:] = (dm * bb_ref[...].astype(F32) * sb * (1.0 - sb)).astype(BF16)
        doa_ref[...] = _nn(dba, wa_ref[...])
        dob_ref[...] = _nn(dbb, wb_ref[...])

    row = pl.BlockSpec((tm, D), lambda i: (i, 0))
    full = lambda r, c: pl.BlockSpec((r, c), lambda i: (0, 0))
    return pl.pallas_call(
        body, name="merge_bwd", grid=(S // tm,),
        in_specs=[row, pl.BlockSpec((tm, D), lambda i: (i, 0)), pl.BlockSpec((tm, D), lambda i: (i, 1)),
                  row, row, full(D, W), full(D, W), full(D, D)],
        out_specs=[row, row, row, pl.BlockSpec((tm, 2 * D), lambda i: (i, 0)),
                   pl.BlockSpec((tm, W), lambda i: (i, 0)), pl.BlockSpec((tm, W), lambda i: (i, 0))],
        out_shape=[jax.ShapeDtypeStruct((S, D), BF16)] * 3 + [jax.ShapeDtypeStruct((S, 2 * D), BF16)]
                  + [jax.ShapeDtypeStruct((S, W), F32)] * 2,
        compiler_params=_cp("arbitrary"),
    )(dx2, gates, gates, ba, bb, wat, wbt, w_out)


def _final_loss(x3, gf, target):
    S, D = x3.shape
    tm = min(TOKEN_TILE, S)

    def body(x_ref, g_ref, t_ref, loss_ref, dx_ref, dg_ref):
        i = pl.program_id(0)
        x = x_ref[...]
        g = g_ref[...]
        e = _norm_fwd(x, g) - t_ref[...]
        part = 0.5 * jnp.sum(jnp.mean(e * e, axis=-1, keepdims=True), axis=0, keepdims=True)
        dx, dg = _norm_bwd(x, g, e * (1.0 / D))
        dx_ref[...] = dx

        @pl.when(i == 0)
        def _():
            loss_ref[...] = part
            dg_ref[...] = dg

        @pl.when(i > 0)
        def _():
            loss_ref[...] += part
            dg_ref[...] += dg

    row = pl.BlockSpec((tm, D), lambda i: (i, 0))
    vec = pl.BlockSpec((1, D), lambda i: (0, 0))
    return pl.pallas_call(
        body, name="final_loss", grid=(S // tm,),
        in_specs=[row, vec, row],
        out_specs=[pl.BlockSpec((1, 1), lambda i: (0, 0)), row, vec],
        out_shape=[jax.ShapeDtypeStruct((1, 1), F32), jax.ShapeDtypeStruct((S, D), F32),
                   jax.ShapeDtypeStruct((1, D), F32)],
        compiler_params=_cp("arbitrary"),
    )(x3, gf, target)


SB_HEAD_GROUP = 4
LANES = 128


def _tri(T, kind):
    r = lax.broadcasted_iota(jnp.int32, (T, T), 0)
    c = lax.broadcasted_iota(jnp.int32, (T, T), 1)
    return {"after": r > c, "upto": r <= c, "before": r < c}[kind].astype(BF16)


def _lane(v, j):
    return jnp.broadcast_to(v[:, j:j + 1], (v.shape[0], LANES))


def _t_bf16(x):
    return x.astype(F32).T.astype(BF16)


def _wide(v, T):
    return jnp.tile(v, (1, T // LANES))


def _sb_fwd(qt, kt, vt):
    H, nb, dh, T = qt.shape
    HG = SB_HEAD_GROUP

    def body(q_ref, k_ref, v_ref, o_ref, tl_ref):
        row = lax.broadcasted_iota(jnp.int32, (T, T), 0)
        col = lax.broadcasted_iota(jnp.int32, (T, T), 1)
        tri = col < row
        after = _tri(T, "after")

        def blocks(qs, kb, carry, diag):
            hs = range(HG)
            z = [_nn(qs[hh], k_ref[hh, kb]) for hh in hs]
            res, ls, first = [None] * HG, [None] * HG, [None] * HG
            for hh in hs:
                sp = _softplus(z[hh])
                if diag:
                    sp = jnp.where(tri, sp, 0.0)
                ls[hh] = z[hh] - sp
                first[hh] = _lane(sp, 0)
                res[hh] = _hilo_nn(sp, after)
            out = []
            for hh in hs:
                c, oacc = carry[2 * hh], carry[2 * hh + 1]
                a = jnp.exp(ls[hh] - (res[hh] + _wide(c, T)))
                if diag:
                    a = jnp.where(tri, a, 0.0)
                out.extend([c + (first[hh] + _lane(res[hh], 0)), oacc + _nt(v_ref[hh, kb], a.astype(BF16))])
            return tuple(out)

        def qblock(i, _):
            qs = [_t_bf16(q_ref[hh, i]) for hh in range(HG)]
            carry = blocks(qs, i, (jnp.zeros((T, LANES), F32), jnp.zeros((dh, T), F32)) * HG, True)

            def kstep(t, carry):
                return blocks(qs, i - 1 - t, carry, False)

            carry = lax.fori_loop(0, i, kstep, carry)
            for hh in range(HG):
                o_ref[hh, i] = carry[2 * hh + 1]
                tl_ref[hh, i] = carry[2 * hh].T[:8]
            return 0

        lax.fori_loop(0, nb, qblock, 0)

    ht = pl.BlockSpec((HG, nb, dh, T), lambda h: (h, 0, 0, 0))
    return pl.pallas_call(
        body, name="sb_fwd", grid=(H // HG,),
        in_specs=[ht, ht, ht],
        out_specs=[ht, pl.BlockSpec((HG, nb, 8, T), lambda h: (h, 0, 0, 0))],
        out_shape=[jax.ShapeDtypeStruct((H, nb, dh, T), F32), jax.ShapeDtypeStruct((H, nb, 8, T), F32)],
        compiler_params=_cp("arbitrary"),
    )(qt, kt, vt)


def _sb_bwd(qt, kt, vt, dot, tl):
    H, nb, dh, T = qt.shape
    HG = SB_HEAD_GROUP

    def body(qt_ref, k_ref, v_ref, dot_ref, tl_ref, dq_ref, dk_ref, dv_ref):
        row = lax.broadcasted_iota(jnp.int32, (T, T), 0)
        col = lax.broadcasted_iota(jnp.int32, (T, T), 1)
        tri = col < row
        upto = _tri(T, "upto")
        before = _tri(T, "before")
        dk_ref[...] = jnp.zeros_like(dk_ref)
        dv_ref[...] = jnp.zeros_like(dv_ref)

        def blocks(qs, qTs, dos, doTs, kb, carry, diag):
            hs = range(HG)
            kT = [k_ref[hh, kb] for hh in hs]
            z = [_nn(qs[hh], kT[hh]) for hh in hs]
            da = [_nn(dos[hh], v_ref[hh, kb]) for hh in hs]
            res, ls = [None] * HG, [None] * HG
            for hh in hs:
                sp = _softplus(z[hh])
                if diag:
                    sp = jnp.where(tri, sp, 0.0)
                ls[hh] = z[hh] - sp
                res[hh] = _hilo_nn(sp, upto)
            g, ab, resg = [None] * HG, [None] * HG, [None] * HG
            for hh in hs:
                a = jnp.exp(ls[hh] + (res[hh] - _wide(carry[3 * hh], T)))
                if diag:
                    a = jnp.where(tri, a, 0.0)
                ab[hh] = a.astype(BF16)
                g[hh] = a * da[hh]
                resg[hh] = _hilo_nn(g[hh], before)
            out = []
            for hh in hs:
                rem, pre_g, dq = carry[3 * hh:3 * hh + 3]
                dz = g[hh] - (g[hh] + (resg[hh] + _wide(pre_g, T))) * jnp.exp(ls[hh])
                if diag:
                    dz = jnp.where(tri, dz, 0.0)
                dzb = dz.astype(BF16)
                dk_ref[hh, kb] += _nn(qTs[hh], dzb)
                dv_ref[hh, kb] += _nn(doTs[hh], ab[hh])
                out.extend([rem - _lane(res[hh], T - 1), pre_g + (_lane(resg[hh], T - 1) + _lane(g[hh], T - 1)),
                            dq + _nt(kT[hh], dzb)])
            return tuple(out)

        def qblock(i, _):
            qTs = [qt_ref[hh, i] for hh in range(HG)]
            doTs = [dot_ref[hh, i] for hh in range(HG)]
            qs = [_t_bf16(v) for v in qTs]
            dos = [_t_bf16(v) for v in doTs]
            carry = []
            for hh in range(HG):
                total = jnp.broadcast_to(tl_ref[hh, i][0:1], (LANES, T)).T
                carry.extend([total, jnp.zeros((T, LANES), F32), jnp.zeros((dh, T), F32)])

            def kstep(kb, carry):
                return blocks(qs, qTs, dos, doTs, kb, carry, False)

            carry = lax.fori_loop(0, i, kstep, tuple(carry))
            carry = blocks(qs, qTs, dos, doTs, i, carry, True)
            for hh in range(HG):
                dq_ref[hh, i] = carry[3 * hh + 2]
            return 0

        lax.fori_loop(0, nb, qblock, 0)

    ht = pl.BlockSpec((HG, nb, dh, T), lambda h: (h, 0, 0, 0), pipeline_mode=pl.Buffered(1))
    return pl.pallas_call(
        body, name="sb_bwd", grid=(H // HG,),
        in_specs=[ht, ht, ht, ht, pl.BlockSpec((HG, nb, 8, T), lambda h: (h, 0, 0, 0))],
        out_specs=[ht, ht, ht],
        out_shape=[jax.ShapeDtypeStruct((H, nb, dh, T), F32)] * 3,
        compiler_params=_cp("arbitrary"),
    )(qt, kt, vt, dot, tl)


def _swa_probs(q, kp, kc, bias, sink, first):
    T = q.shape[0]
    row = lax.broadcasted_iota(jnp.int32, (T, T), 0)
    col = lax.broadcasted_iota(jnp.int32, (T, T), 1)
    lp = jnp.where(jnp.logical_and(col > row, jnp.logical_not(first)), _nn(q, kp) + bias[:, :T], NEG_BIG)
    lc = jnp.where(col <= row, _nn(q, kc) + bias[:, T:], NEG_BIG)
    m = jnp.maximum(jnp.maximum(jnp.max(lp, axis=1, keepdims=True), jnp.max(lc, axis=1, keepdims=True)), sink)
    pp = jnp.exp(lp - m)
    pc = jnp.exp(lc - m)
    ps = jnp.exp(sink - m)
    inv = 1.0 / (jnp.sum(pp, axis=1, keepdims=True) + jnp.sum(pc, axis=1, keepdims=True) + ps)
    return pp * inv, pc * inv, ps * inv


def _swa_fwd(qn, kt, vt, bias, sinks):
    Hq, nb, T, dh = qn.shape
    grp = Hq // kt.shape[0]

    def body(sink_ref, q_ref, kp_ref, kc_ref, vp_ref, vc_ref, bias_ref, o_ref):
        hk, n = pl.program_id(0), pl.program_id(1)
        kp, kc, vp, vc = kp_ref[...], kc_ref[...], vp_ref[...], vc_ref[...]
        for g in range(grp):
            pp, pc, _ = _swa_probs(q_ref[g], kp, kc, bias_ref[g], sink_ref[hk * grp + g], n == 0)
            o_ref[g] = _nt(vp, pp.astype(BF16)) + _nt(vc, pc.astype(BF16))

    prev = pl.BlockSpec((None, None, dh, T), lambda h, n: (h, jnp.maximum(n - 1, 0), 0, 0))
    cur = pl.BlockSpec((None, None, dh, T), lambda h, n: (h, n, 0, 0))
    return pl.pallas_call(
        body, name="swa_fwd", grid=(Hq // grp, nb),
        in_specs=[pl.BlockSpec(memory_space=pltpu.SMEM),
                  pl.BlockSpec((grp, None, T, dh), lambda h, n: (h, n, 0, 0)),
                  prev, cur, prev, cur,
                  pl.BlockSpec((grp, T, 2 * T), lambda h, n: (h, 0, 0))],
        out_specs=pl.BlockSpec((grp, None, dh, T), lambda h, n: (h, n, 0, 0)),
        out_shape=jax.ShapeDtypeStruct((Hq, nb, dh, T), F32),
        compiler_params=_cp("arbitrary", "arbitrary"),
    )(sinks, qn, kt, kt, vt, vt, bias)


def _swa_bwd(qn, qt, kt, vt, bias, sinks, don, dot, on):
    Hq, nb, T, dh = qn.shape
    Hkv = kt.shape[0]
    grp = Hq // Hkv

    def body(sink_ref, q_ref, qt_ref, kp_ref, kc_ref, vp_ref, vc_ref, bias_ref, do_ref, dot_ref, o_ref,
             dq_ref, dk_ref, dv_ref, dbias_ref, dsink_ref, ck, cv):
        hk, n = pl.program_id(0), pl.program_id(1)

        @pl.when(n == 0)
        def _():
            dbias_ref[...] = jnp.zeros_like(dbias_ref)
            dsink_ref[...] = jnp.zeros_like(dsink_ref)
            ck[...] = jnp.zeros_like(ck)
            cv[...] = jnp.zeros_like(cv)

        @pl.when(n < nb)
        def _():
            kp, kc, vp, vc = kp_ref[...], kc_ref[...], vp_ref[...], vc_ref[...]
            kprev = jnp.zeros((dh, T), F32)
            vprev = jnp.zeros((dh, T), F32)
            kcur = jnp.zeros((dh, T), F32)
            vcur = jnp.zeros((dh, T), F32)
            for g in range(grp):
                pp, pc, ps = _swa_probs(q_ref[g], kp, kc, bias_ref[g], sink_ref[hk * grp + g], n == 0)
                do = do_ref[g]
                dob = do.astype(BF16)
                delta = jnp.sum(do * o_ref[g], axis=1, keepdims=True)
                dlp = pp * (_nn(dob, vp) - delta)
                dlc = pc * (_nn(dob, vc) - delta)
                dbias_ref[g, :, :T] += dlp
                dbias_ref[g, :, T:] += dlc
                dsink_ref[g] += -ps * delta
                dlpb, dlcb = dlp.astype(BF16), dlc.astype(BF16)
                dq_ref[g] = _nt(kp, dlpb) + _nt(kc, dlcb)
                qT, doT = qt_ref[g], dot_ref[g]
                kprev += _nn(qT, dlpb)
                kcur += _nn(qT, dlcb)
                vprev += _nn(doT, pp.astype(BF16))
                vcur += _nn(doT, pc.astype(BF16))
            dk_ref[...] = ck[...] + kprev
            dv_ref[...] = cv[...] + vprev
            ck[...] = kcur
            cv[...] = vcur

        @pl.when(n == nb)
        def _():
            dk_ref[...] = ck[...]
            dv_ref[...] = cv[...]

    qn_spec = pl.BlockSpec((grp, None, T, dh), lambda h, n: (h, jnp.minimum(n, nb - 1), 0, 0))
    qt_spec = pl.BlockSpec((grp, None, dh, T), lambda h, n: (h, jnp.minimum(n, nb - 1), 0, 0))
    prev = pl.BlockSpec((None, None, dh, T), lambda h, n: (h, jnp.maximum(n - 1, 0), 0, 0))
    cur = pl.BlockSpec((None, None, dh, T), lambda h, n: (h, jnp.minimum(n, nb - 1), 0, 0))
    per_group = lambda a, b: pl.BlockSpec((grp, a, b), lambda h, n: (h, 0, 0))
    return pl.pallas_call(
        body, name="swa_bwd", grid=(Hkv, nb + 1),
        in_specs=[pl.BlockSpec(memory_space=pltpu.SMEM), qn_spec, qt_spec, prev, cur, prev, cur,
                  per_group(T, 2 * T), qn_spec, qt_spec, qn_spec],
        out_specs=[qt_spec, prev, prev, per_group(T, 2 * T), per_group(T, 1)],
        out_shape=[jax.ShapeDtypeStruct((Hq, nb, dh, T), F32), jax.ShapeDtypeStruct((Hkv, nb, dh, T), F32),
                   jax.ShapeDtypeStruct((Hkv, nb, dh, T), F32), jax.ShapeDtypeStruct((Hq, T, 2 * T), F32),
                   jax.ShapeDtypeStruct((Hq, T, 1), F32)],
        scratch_shapes=[pltpu.VMEM((dh, T), F32), pltpu.VMEM((dh, T), F32)],
        compiler_params=_cp("arbitrary", "arbitrary"),
    )(sinks, qn, qt, kt, kt, vt, vt, bias, don, dot, on)


def _split3(x):
    h1 = x.astype(BF16)
    r1 = x - h1.astype(F32)
    h2 = r1.astype(BF16)
    h3 = (r1 - h2.astype(F32)).astype(BF16)
    return h1, h2, h3


def _bias_expand(rel_t, onehot):
    Hq, NB = rel_t.shape
    L = onehot.shape[1]

    def body(r_ref, oh_ref, o_ref):
        h1, h2, h3 = _split3(r_ref[...])
        oh = oh_ref[...]
        o_ref[...] = _nn(h1, oh) + _nn(h2, oh) + _nn(h3, oh)

    return pl.pallas_call(
        body, name="bias_expand", grid=(1,),
        in_specs=[pl.BlockSpec((Hq, NB), lambda i: (0, 0)), pl.BlockSpec((NB, L), lambda i: (0, 0))],
        out_specs=pl.BlockSpec((Hq, L), lambda i: (0, 0)),
        out_shape=jax.ShapeDtypeStruct((Hq, L), F32),
        compiler_params=_cp("arbitrary"),
    )(rel_t, onehot)


def _bias_reduce(dbias, onehot):
    Hq, L = dbias.shape
    NB = onehot.shape[0]

    def body(d_ref, oh_ref, o_ref):
        h1, h2, h3 = _split3(d_ref[...])
        oh = oh_ref[...]
        o_ref[...] = _nt(h1, oh) + _nt(h2, oh) + _nt(h3, oh)

    return pl.pallas_call(
        body, name="bias_reduce", grid=(1,),
        in_specs=[pl.BlockSpec((Hq, L), lambda i: (0, 0)), pl.BlockSpec((NB, L), lambda i: (0, 0))],
        out_specs=pl.BlockSpec((Hq, NB), lambda i: (0, 0)),
        out_shape=jax.ShapeDtypeStruct((Hq, NB), F32),
        compiler_params=_cp("arbitrary"),
    )(dbias, onehot)


def _adamw(w, g, m, v, name):
    R, C = w.shape
    tr = 256 if R % 256 == 0 else R
    bc1 = 1.0 - ADAM_B1 ** ADAM_STEP
    bc2 = 1.0 - ADAM_B2 ** ADAM_STEP

    def body(w_ref, g_ref, m_ref, v_ref, d_ref, nm_ref, nv_ref):
        g = g_ref[...]
        m2 = ADAM_B1 * m_ref[...] + (1.0 - ADAM_B1) * g
        v2 = ADAM_B2 * v_ref[...] + (1.0 - ADAM_B2) * (g * g)
        nm_ref[...] = m2
        nv_ref[...] = v2
        d_ref[...] = -ADAM_LR * ((m2 / bc1) / (jnp.sqrt(v2 / bc2) + ADAM_EPS) + ADAM_WD * w_ref[...])

    spec = pl.BlockSpec((tr, C), lambda i: (i, 0))
    return pl.pallas_call(
        body, name=name, grid=(R // tr,),
        in_specs=[spec] * 4, out_specs=[spec] * 3,
        out_shape=[jax.ShapeDtypeStruct((R, C), F32)] * 3,
        compiler_params=_cp("arbitrary"),
    )(w, g, m, v)


def _add_halves(mine, recv, name):
    K, R, C = mine.shape
    tr = 416 if R % 416 == 0 else R

    def body(a_ref, b_ref, o_ref, ob_ref):
        s = a_ref[...] + b_ref[...]
        o_ref[...] = s
        ob_ref[...] = s.astype(BF16)

    spec = pl.BlockSpec((None, tr, C), lambda k, i: (k, i, 0))
    return pl.pallas_call(
        body, name=name, grid=(K, R // tr),
        in_specs=[spec, spec], out_specs=[spec, spec],
        out_shape=[jax.ShapeDtypeStruct((K, R, C), F32), jax.ShapeDtypeStruct((K, R, C), BF16)],
        compiler_params=_cp("arbitrary", "arbitrary"),
    )(mine, recv)


def _add_received(own, recv, name):
    R, C = own.shape
    tr = 416 if R % 416 == 0 else R

    def body(a_ref, r_ref, o_ref):
        o_ref[...] = ((a_ref[...] + r_ref[0].astype(F32)) + r_ref[1].astype(F32)) + r_ref[2].astype(F32)

    return pl.pallas_call(
        body, name=name, grid=(R // tr,),
        in_specs=[pl.BlockSpec((tr, C), lambda i: (i, 0)), pl.BlockSpec((3, tr, C), lambda i: (0, i, 0))],
        out_specs=pl.BlockSpec((tr, C), lambda i: (i, 0)),
        out_shape=jax.ShapeDtypeStruct((R, C), F32),
        compiler_params=_cp("arbitrary"),
    )(own, recv)


def _position():
    x, y, c = lax.axis_index("x"), lax.axis_index("y"), lax.axis_index("c")
    others = [(1 - x, y), (x, 1 - y), (1 - x, 1 - y)]
    return x, y, c, others


def _remote(src, dst, send_sems, recv_sems, k, dev):
    return pltpu.make_async_remote_copy(src_ref=src, dst_ref=dst, send_sem=send_sems.at[k],
                                        recv_sem=recv_sems.at[k], device_id=dev, device_id_type=MESH_ID)


def _gather_weights(shard):
    R, C = shard.shape
    half = R // 2

    def body(src, out, send_sems, recv_sems, local_sem):
        x, y, c, others = _position()
        mine = 2 * x + y
        rows = pl.ds(pl.multiple_of(c * half, 16), half)
        other_rows = pl.ds(pl.multiple_of((1 - c) * half, 16), half)
        local = pltpu.make_async_copy(src, out.at[mine], local_sem)
        local.start()
        sends = [_remote(src.at[rows], out.at[mine, rows], send_sems, recv_sems, j, (ox, oy, c))
                 for j, (ox, oy) in enumerate(others)]
        for cp in sends:
            cp.start()
        passed = []
        for j, (ox, oy) in enumerate(others):
            slot = out.at[2 * ox + oy, rows]
            _remote(slot, slot, send_sems, recv_sems, j, (ox, oy, c)).wait_recv()
            fwd = _remote(slot, slot, send_sems, recv_sems, 3 + j, (x, y, 1 - c))
            fwd.start()
            passed.append(fwd)
        for j, (ox, oy) in enumerate(others):
            slot = out.at[2 * ox + oy, other_rows]
            _remote(slot, slot, send_sems, recv_sems, 3 + j, (x, y, 1 - c)).wait_recv()
        for cp in sends + passed:
            cp.wait_send()
        local.wait()

    return pl.pallas_call(
        body, name="gather_weights",
        in_specs=[ANY], out_specs=ANY,
        out_shape=jax.ShapeDtypeStruct((N_CHIPS, R, C), shard.dtype),
        scratch_shapes=[pltpu.SemaphoreType.DMA((6,)), pltpu.SemaphoreType.DMA((6,)), pltpu.SemaphoreType.DMA],
    )(shard)


def _swap_halves(grads):
    K, R, C = grads.shape
    half = R // 2

    def body(src, out, send_sems, recv_sems):
        x, y, c, _ = _position()
        theirs = src.at[:, pl.ds(pl.multiple_of((1 - c) * half, 8), half), :]
        cp = _remote(theirs, out, send_sems, recv_sems, 0, (x, y, 1 - c))
        cp.start()
        cp.wait()

    return pl.pallas_call(
        body, name="swap_halves",
        in_specs=[ANY], out_specs=ANY,
        out_shape=jax.ShapeDtypeStruct((K, half, C), grads.dtype),
        scratch_shapes=[pltpu.SemaphoreType.DMA((1,)), pltpu.SemaphoreType.DMA((1,))],
    )(grads)


def _scatter_to_owners(parts):
    K, H, C = parts.shape

    def body(src, out, send_sems, recv_sems):
        x, y, c, others = _position()
        sends = [_remote(src.at[2 * ox + oy], out.at[j], send_sems, recv_sems, j, (ox, oy, c))
                 for j, (ox, oy) in enumerate(others)]
        for cp in sends:
            cp.start()
        for cp in sends:
            cp.wait()

    return pl.pallas_call(
        body, name="scatter_to_owners",
        in_specs=[ANY], out_specs=ANY,
        out_shape=jax.ShapeDtypeStruct((3, H, C), parts.dtype),
        scratch_shapes=[pltpu.SemaphoreType.DMA((3,)), pltpu.SemaphoreType.DMA((3,))],
    )(parts)


def _join_halves(half_rows):
    H, C = half_rows.shape

    def body(src, out, send_sems, recv_sems, local_sem):
        x, y, c, _ = _position()
        rows = pl.ds(pl.multiple_of(c * H, 8), H)
        local = pltpu.make_async_copy(src, out.at[rows], local_sem)
        local.start()
        cp = _remote(src, out.at[rows], send_sems, recv_sems, 0, (x, y, 1 - c))
        cp.start()
        theirs = out.at[pl.ds(pl.multiple_of((1 - c) * H, 8), H)]
        _remote(theirs, theirs, send_sems, recv_sems, 0, (x, y, 1 - c)).wait_recv()
        cp.wait_send()
        local.wait()

    return pl.pallas_call(
        body, name="join_halves",
        in_specs=[ANY], out_specs=ANY,
        out_shape=jax.ShapeDtypeStruct((2 * H, C), half_rows.dtype),
        scratch_shapes=[pltpu.SemaphoreType.DMA((1,)), pltpu.SemaphoreType.DMA((1,)), pltpu.SemaphoreType.DMA],
    )(half_rows)


def _allreduce_small(block):
    R, C = block.shape
    n_dev = 8

    def body(src, out, slots, send_sems, recv_sems):
        x, y, c, _ = _position()
        me = 4 * x + 2 * y + c
        slots[me] = src[...]
        sends = []
        for r in range(1, n_dev):
            peer = (x ^ (r >> 2), y ^ ((r >> 1) & 1), c ^ (r & 1))
            cp = _remote(src, slots.at[me], send_sems, recv_sems, r - 1, peer)
            cp.start()
            sends.append(cp)
        for r in range(1, n_dev):
            theirs = slots.at[me ^ r]
            _remote(theirs, theirs, send_sems, recv_sems, r - 1, (x, y, c)).wait_recv()
        for cp in sends:
            cp.wait_send()
        acc = slots[0]
        for d in range(1, n_dev):
            acc = acc + slots[d]
        out[...] = acc

    return pl.pallas_call(
        body, name="allreduce_small",
        in_specs=[pl.BlockSpec(memory_space=pltpu.VMEM)], out_specs=pl.BlockSpec(memory_space=pltpu.VMEM),
        out_shape=jax.ShapeDtypeStruct((R, C), F32),
        scratch_shapes=[pltpu.VMEM((n_dev, R, C), F32), pltpu.SemaphoreType.DMA((7,)), pltpu.SemaphoreType.DMA((7,))],
    )(block)


def _heads_n(a, T, scale=None):
    S, W = a.shape
    a = a.reshape(S // T, T, W // HEAD_DIM, HEAD_DIM).transpose(2, 0, 1, 3)
    if scale is not None:
        a = a * scale
    return a.astype(BF16)


def _heads_t(a, T, scale=None):
    S, W = a.shape
    a = a.reshape(S // T, T, W // HEAD_DIM, HEAD_DIM).transpose(2, 0, 3, 1)
    if scale is not None:
        a = a * scale
    return a.astype(BF16)


def _heads_n_f32(a, T):
    S, W = a.shape
    return a.reshape(S // T, T, W // HEAD_DIM, HEAD_DIM).transpose(2, 0, 1, 3)


def _from_heads_t(a):
    H, nb, dh, T = a.shape
    return a.transpose(1, 3, 0, 2).reshape(nb * T, H * dh)


def _rel_bucket(dist):
    max_exact = REL_BUCKETS // 2
    d = jnp.maximum(dist, 1).astype(F32)
    large = max_exact + (jnp.log(d / max_exact) / math.log(REL_MAX_DIST / max_exact)
                         * (REL_BUCKETS - max_exact)).astype(jnp.int32)
    large = jnp.minimum(large, REL_BUCKETS - 1)
    return jnp.where(dist < max_exact, dist, large)


def _bucket_onehot():
    T = SWA_BLOCK
    dist = (jnp.arange(T)[:, None] + T) - jnp.arange(2 * T)[None, :]
    bucket = _rel_bucket(jnp.maximum(dist, 0)).reshape(1, T * 2 * T)
    return (bucket == jnp.arange(REL_BUCKETS)[:, None]).astype(BF16)


_BUF = (("ffn1_w1", "t"), ("ffn1_w3", "t"), ("ffn1_w2", "n"), ("ffn2_w1", "t"), ("ffn2_w3", "t"),
        ("ffn2_w2", "n"), ("w_in", "t"), ("w_out", "n"), ("w_branch_swa", "tw"), ("w_branch_sb", "tw"))


def _to_rows(name_kind, w, D):
    kind = name_kind[1]
    if kind == "n":
        return w
    if kind == "t":
        return w.T
    return w.T.reshape(-1, D)


def _from_rows(name_kind, rows, width):
    kind = name_kind[1]
    if kind == "n":
        return rows
    if kind == "t":
        return rows.T
    return rows.reshape(-1, width).T


def kernel(x, norm_ffn1, ffn1_w1, ffn1_w3, ffn1_w2, norm_mix, w_in, swa_sinks, rel_bias, w_branch_swa, w_branch_sb, w_out, norm_ffn2, ffn2_w1, ffn2_w3, ffn2_w2, norm_final, loss_target, m_norm_ffn1, m_ffn1_w1, m_ffn1_w3, m_ffn1_w2, m_norm_mix, m_w_in, m_swa_sinks, m_rel_bias, m_w_branch_swa, m_w_branch_sb, m_w_out, m_norm_ffn2, m_ffn2_w1, m_ffn2_w3, m_ffn2_w2, m_norm_final, v_norm_ffn1, v_ffn1_w1, v_ffn1_w3, v_ffn1_w2, v_norm_mix, v_w_in, v_swa_sinks, v_rel_bias, v_w_branch_swa, v_w_branch_sb, v_w_out, v_norm_ffn2, v_ffn2_w1, v_ffn2_w3, v_ffn2_w2, v_norm_final):
    names = ["norm_ffn1", "ffn1_w1", "ffn1_w3", "ffn1_w2", "norm_mix", "w_in", "swa_sinks", "rel_bias",
             "w_branch_swa", "w_branch_sb", "w_out", "norm_ffn2", "ffn2_w1", "ffn2_w3", "ffn2_w2", "norm_final"]
    W = dict(zip(names, [norm_ffn1, ffn1_w1, ffn1_w3, ffn1_w2, norm_mix, w_in, swa_sinks, rel_bias,
                         w_branch_swa, w_branch_sb, w_out, norm_ffn2, ffn2_w1, ffn2_w3, ffn2_w2, norm_final]))
    M = dict(zip(names, [m_norm_ffn1, m_ffn1_w1, m_ffn1_w3, m_ffn1_w2, m_norm_mix, m_w_in, m_swa_sinks, m_rel_bias,
                         m_w_branch_swa, m_w_branch_sb, m_w_out, m_norm_ffn2, m_ffn2_w1, m_ffn2_w3, m_ffn2_w2,
                         m_norm_final]))
    V = dict(zip(names, [v_norm_ffn1, v_ffn1_w1, v_ffn1_w3, v_ffn1_w2, v_norm_mix, v_w_in, v_swa_sinks, v_rel_bias,
                         v_w_branch_swa, v_w_branch_sb, v_w_out, v_norm_ffn2, v_ffn2_w1, v_ffn2_w3, v_ffn2_w2,
                         v_norm_final]))
    xs = x[0]
    target = loss_target[0]
    S, D = xs.shape
    QW = SWA_Q_HEADS * HEAD_DIM
    KW = SWA_KV_HEADS * HEAD_DIM
    BW = SB_HEADS * HEAD_DIM
    QKV = QW + 2 * KW + 3 * BW

    pieces = [_to_rows(nk, W[nk[0]][0], D) for nk in _BUF]
    sizes = [p.shape[0] for p in pieces]
    offs = [0]
    for s in sizes:
        offs.append(offs[-1] + s)
    shard = jnp.concatenate(pieces, axis=0).astype(BF16)
    gathered = _gather_weights(shard)

    def full(i):
        return gathered[:, offs[i]:offs[i + 1], :].reshape(N_CHIPS * sizes[i], D)

    f1w1, f1w3, f1w2, f2w1, f2w3, f2w2, w_in_t, w_out_f = [full(i) for i in range(8)]
    wa_t = full(8).reshape(D, QW)
    wb_t = full(9).reshape(D, BW)

    g1, gmix, g3 = W["norm_ffn1"], W["norm_mix"], W["norm_ffn2"]
    gf = W["norm_final"].reshape(1, D)

    x1, h1, a1, b1 = _ffn_fwd(xs, g1, f1w1, f1w3, f1w2, "ffn1_fwd")
    qkv, h2 = _norm_matmul_nt(x1, gmix, w_in_t[:QKV], BF16, "proj_qkv")
    gates, _ = _norm_matmul_nt(x1, gmix, w_in_t[QKV:], F32, "proj_gates")

    q_a, k_a, v_a = qkv[:, :QW], qkv[:, QW:QW + KW], qkv[:, QW + KW:QW + 2 * KW]
    o0 = QW + 2 * KW
    q_b, k_b, v_b = qkv[:, o0:o0 + BW], qkv[:, o0 + BW:o0 + 2 * BW], qkv[:, o0 + 2 * BW:o0 + 3 * BW]

    onehot = _bucket_onehot()
    bias = _bias_expand(W["rel_bias"].T, onehot).reshape(SWA_Q_HEADS, SWA_BLOCK, 2 * SWA_BLOCK)
    sinks = W["swa_sinks"].reshape(SWA_Q_HEADS)
    qa_n, qa_t = _heads_n(q_a, SWA_BLOCK, QK_SCALE), _heads_t(q_a, SWA_BLOCK, QK_SCALE)
    ka_t, va_t = _heads_t(k_a, SWA_BLOCK), _heads_t(v_a, SWA_BLOCK)
    oa_t = _swa_fwd(qa_n, ka_t, va_t, bias, sinks)

    qb_t = _heads_t(q_b, SB_BLOCK, QK_SCALE)
    kb_t, vb_t = _heads_t(k_b, SB_BLOCK), _heads_t(v_b, SB_BLOCK)
    ob_t, totals = _sb_fwd(qb_t, kb_t, vb_t)

    o_a = _from_heads_t(oa_t)
    o_b = _from_heads_t(ob_t)
    o_a16, o_b16 = o_a.astype(BF16), o_b.astype(BF16)
    x2, merged, ba, bb = _merge_fwd(x1, gates, o_a16, o_b16, wa_t, wb_t, w_out_f)
    x3, h3, a2, b2 = _ffn_fwd(x2, g3, f2w1, f2w3, f2w2, "ffn2_fwd")
    loss_part, dx3, dgf = _final_loss(x3, gf, target)

    dx2, dg3, dz2, da2, db2, u2 = _ffn_bwd(dx3, x2, g3, a2, b2, f2w1, f2w3, f2w2, "ffn2_bwd")
    grads = {}
    grads["ffn2_w1"] = _tn_matmul(da2, h3, "ffn2_dw1")
    grads["ffn2_w3"] = _tn_matmul(db2, h3, "ffn2_dw3")
    grads["ffn2_w2"] = _tn_matmul(u2, dz2, "ffn2_dw2")

    dx2b, dba, dbb, dgates, do_a, do_b = _merge_bwd(dx2, gates, ba, bb, wa_t, wb_t, w_out_f)
    grads["w_out"] = _tn_matmul(merged, dx2b, "dw_out")
    grads["w_branch_swa"] = _tn_matmul(dba, o_a16, "dw_branch_swa")
    grads["w_branch_sb"] = _tn_matmul(dbb, o_b16, "dw_branch_sb")

    dqb_t, dkb_t, dvb_t = _sb_bwd(qb_t, kb_t, vb_t, _heads_t(do_b, SB_BLOCK), totals)
    dqa_t, dka_t, dva_t, dbias, dsink_rows = _swa_bwd(
        qa_n, qa_t, ka_t, va_t, bias, sinks, _heads_n_f32(do_a, SWA_BLOCK), _heads_t(do_a, SWA_BLOCK),
        _heads_n_f32(o_a, SWA_BLOCK))
    d_rel = _bias_reduce(dbias.reshape(SWA_Q_HEADS, -1), onehot).T
    d_sinks = jnp.sum(dsink_rows, axis=(1, 2))

    dqkv = jnp.concatenate([_from_heads_t(dqa_t) * QK_SCALE, _from_heads_t(dka_t), _from_heads_t(dva_t),
                            _from_heads_t(dqb_t) * QK_SCALE, _from_heads_t(dkb_t), _from_heads_t(dvb_t)],
                           axis=1).astype(BF16)
    dproj = jnp.concatenate([dqkv, dgates], axis=1)
    grads["w_in"] = _tn_matmul(dproj, h2, "dw_in")
    dx1, dgmix = _matmul_norm_bwd(dproj, w_in_t, x1, gmix, dx2, "proj_bwd")

    dx0, dg1, dz1, da1, db1, u1 = _ffn_bwd(dx1, xs, g1, a1, b1, f1w1, f1w3, f1w2, "ffn1_bwd")
    grads["ffn1_w1"] = _tn_matmul(da1, h1, "ffn1_dw1")
    grads["ffn1_w3"] = _tn_matmul(db1, h1, "ffn1_dw3")
    grads["ffn1_w2"] = _tn_matmul(u1, dz1, "ffn1_dw2")

    gparts = [grads[nk[0]].reshape(N_CHIPS, sizes[i], D) for i, nk in enumerate(_BUF)]
    gbuf = jnp.concatenate(gparts, axis=1)
    R = gbuf.shape[1]
    half = R // 2
    c = lax.axis_index("c")
    mine = 2 * lax.axis_index("x") + lax.axis_index("y")
    from_sibling = _swap_halves(gbuf)
    my_half = lax.dynamic_slice_in_dim(gbuf, c * half, half, axis=1)
    chip_sum, chip_sum16 = _add_halves(my_half, from_sibling, "add_sibling")
    received = _scatter_to_owners(chip_sum16)
    own = lax.dynamic_index_in_dim(chip_sum, mine, axis=0, keepdims=False)
    reduced = _join_halves(_add_received(own, received, "add_chips"))

    small_rows = [dg1, dgmix, dg3, dgf,
                  jnp.pad(d_sinks.reshape(1, -1), ((0, 0), (0, D - SWA_Q_HEADS))),
                  jnp.pad(d_rel.reshape(1, -1), ((0, 0), (0, D - REL_BUCKETS * SWA_Q_HEADS))),
                  jnp.pad(loss_part, ((0, 0), (0, D - 1))), jnp.zeros((1, D), F32)]
    small = _allreduce_small(jnp.concatenate(small_rows, axis=0))
    loss = small[6, 0]

    G = {}
    for i, nk in enumerate(_BUF):
        G[nk[0]] = _from_rows(nk, reduced[offs[i]:offs[i + 1]], W[nk[0]].shape[1])[None]
    G["norm_ffn1"], G["norm_mix"], G["norm_ffn2"] = small[0:1], small[1:2], small[2:3]
    G["norm_final"] = small[3]
    G["swa_sinks"] = small[4:5, :SWA_Q_HEADS]
    G["rel_bias"] = small[5, :REL_BUCKETS * SWA_Q_HEADS].reshape(REL_BUCKETS, SWA_Q_HEADS)

    delta, new_m, new_v = {}, {}, {}
    small_names = ["norm_ffn1", "norm_mix", "norm_ffn2", "norm_final", "swa_sinks", "rel_bias"]

    def pack(d):
        return jnp.concatenate([jnp.pad(d[n].reshape(1, -1), ((0, 0), (0, D - d[n].size))) for n in small_names]
                               + [jnp.zeros((2, D), F32)], axis=0)

    sd, sm, sv = _adamw(pack(W), pack(G), pack(M), pack(V), "adamw_small")
    for r, n in enumerate(small_names):
        for dst, src in ((delta, sd), (new_m, sm), (new_v, sv)):
            dst[n] = src[r, :W[n].size].reshape(W[n].shape)
    for nk in _BUF:
        n = nk[0]
        shp = W[n].shape
        two_d = (shp[1], shp[2])
        d_, m_, v_ = _adamw(W[n].reshape(two_d), G[n].reshape(two_d), M[n].reshape(two_d), V[n].reshape(two_d),
                            "adamw_" + n)
        delta[n], new_m[n], new_v[n] = d_.reshape(shp), m_.reshape(shp), v_.reshape(shp)

    return (loss, dx0[None], *[G[n] for n in names], *[delta[n] for n in names],
            *[new_m[n] for n in names], *[new_v[n] for n in names])
```

```python
import functools
import math

import jax
import jax.numpy as jnp
from jax import lax
from jax.experimental import pallas as pl
from jax.experimental.pallas import tpu as pltpu

F32, BF16 = jnp.float32, jnp.bfloat16
MESH_ID = pl.DeviceIdType.MESH
ANY = pl.BlockSpec(memory_space=pl.ANY)

RMS_EPS = 1e-6
HEAD_DIM = 64
SWA_Q_HEADS, SWA_KV_HEADS, SWA_GROUP = 8, 2, 4
SWA_BLOCK = 128
SB_HEADS = 8
SB_BLOCK = 256
REL_BUCKETS, REL_MAX_DIST = 32, 128
NEG_BIG = -1e30
QK_SCALE = HEAD_DIM ** -0.5
ADAM_LR, ADAM_B1, ADAM_B2, ADAM_EPS, ADAM_WD, ADAM_STEP = 0.001, 0.9, 0.999, 1e-08, 0.01, 10

N_CHIPS = 4
TOKEN_TILE = 512
MATMUL_TOKEN_TILE = 1024
WGRAD_ROW_TILES = (2176, 1408, 1024, 256)
FF_TILE = 256
VMEM_LIMIT = 56 * 1024 * 1024


def _cp(*sem):
    return pltpu.CompilerParams(dimension_semantics=sem, vmem_limit_bytes=VMEM_LIMIT)


def _nn(a, b):
    return jnp.dot(a, b, preferred_element_type=F32)


def _nt(a, b):
    return lax.dot_general(a, b, (((1,), (1,)), ((), ())), preferred_element_type=F32)


def _tn(a, b):
    return lax.dot_general(a, b, (((0,), (0,)), ((), ())), preferred_element_type=F32)


def _norm_fwd(x, g):
    return x * lax.rsqrt(jnp.mean(x * x, axis=-1, keepdims=True) + RMS_EPS) * g


def _norm_bwd(x, g, dh):
    r = lax.rsqrt(jnp.mean(x * x, axis=-1, keepdims=True) + RMS_EPS)
    xh = x * r
    dxh = dh * g
    dx = r * (dxh - xh * jnp.mean(dxh * xh, axis=-1, keepdims=True))
    return dx, jnp.sum(dh * xh, axis=0, keepdims=True)


def _softplus(z):
    return jnp.maximum(z, 0.0) + jnp.log(1.0 + jnp.exp(-jnp.abs(z)))


def _ffn_fwd(x, g, w1t, w3t, w2, name):
    S, D = x.shape
    F = w2.shape[0]
    tm, tf = min(MATMUL_TOKEN_TILE, S), FF_TILE
    nj = F // tf

    def body(x_ref, g_ref, w1_ref, w3_ref, w2_ref, xo_ref, h_ref, a_ref, b_ref, hs, acc):
        j = pl.program_id(1)

        @pl.when(j == 0)
        def _():
            hb = _norm_fwd(x_ref[...], g_ref[...]).astype(BF16)
            hs[...] = hb
            h_ref[...] = hb
            acc[...] = jnp.zeros_like(acc)

        h = hs[...]
        a = _nt(h, w1_ref[...])
        b = _nt(h, w3_ref[...])
        a_ref[...] = a.astype(BF16)
        b_ref[...] = b.astype(BF16)
        u = a * jax.nn.sigmoid(a) * b
        acc[...] += _nn(u.astype(BF16), w2_ref[...])

        @pl.when(j == nj - 1)
        def _():
            xo_ref[...] = x_ref[...] + 0.5 * acc[...]

    return pl.pallas_call(
        body, name=name, grid=(S // tm, nj),
        in_specs=[pl.BlockSpec((tm, D), lambda i, j: (i, 0)),
                  pl.BlockSpec((1, D), lambda i, j: (0, 0)),
                  pl.BlockSpec((tf, D), lambda i, j: (j, 0)),
                  pl.BlockSpec((tf, D), lambda i, j: (j, 0)),
                  pl.BlockSpec((tf, D), lambda i, j: (j, 0))],
        out_specs=[pl.BlockSpec((tm, D), lambda i, j: (i, 0)),
                   pl.BlockSpec((tm, D), lambda i, j: (i, 0)),
                   pl.BlockSpec((tm, tf), lambda i, j: (i, j)),
                   pl.BlockSpec((tm, tf), lambda i, j: (i, j))],
        out_shape=[jax.ShapeDtypeStruct((S, D), F32), jax.ShapeDtypeStruct((S, D), BF16),
                   jax.ShapeDtypeStruct((S, F), BF16), jax.ShapeDtypeStruct((S, F), BF16)],
        scratch_shapes=[pltpu.VMEM((tm, D), BF16), pltpu.VMEM((tm, D), F32)],
        compiler_params=_cp("arbitrary", "arbitrary"),
    )(x, g, w1t, w3t, w2)


def _ffn_bwd(dxo, x, g, a, b, w1t, w3t, w2, name):
    S, D = x.shape
    F = w2.shape[0]
    tm, tf = min(MATMUL_TOKEN_TILE, S), FF_TILE
    ni, nj = S // tm, F // tf

    def body(dxo_ref, x_ref, g_ref, a_ref, b_ref, w1_ref, w3_ref, w2_ref,
             dx_ref, dg_ref, dz_ref, da_ref, db_ref, u_ref, dzs, acc):
        i, j = pl.program_id(0), pl.program_id(1)

        @pl.when(j == 0)
        def _():
            dzb = (0.5 * dxo_ref[...]).astype(BF16)
            dzs[...] = dzb
            dz_ref[...] = dzb
            acc[...] = jnp.zeros_like(acc)

        du = _nt(dzs[...], w2_ref[...])
        av = a_ref[...].astype(F32)
        bv = b_ref[...].astype(F32)
        s = jax.nn.sigmoid(av)
        silu = av * s
        db = (du * silu).astype(BF16)
        da = (du * bv * (s * (1.0 + av * (1.0 - s)))).astype(BF16)
        da_ref[...] = da
        db_ref[...] = db
        u_ref[...] = (silu * bv).astype(BF16)
        acc[...] += _nn(da, w1_ref[...]) + _nn(db, w3_ref[...])

        @pl.when(j == nj - 1)
        def _():
            dx, dg = _norm_bwd(x_ref[...], g_ref[...], acc[...])
            dx_ref[...] = dxo_ref[...] + dx

            @pl.when(i == 0)
            def _():
                dg_ref[...] = dg

            @pl.when(i > 0)
            def _():
                dg_ref[...] += dg

    row = pl.BlockSpec((tm, D), lambda i, j: (i, 0))
    wsp = pl.BlockSpec((tf, D), lambda i, j: (j, 0))
    col = pl.BlockSpec((tm, tf), lambda i, j: (i, j))
    vec = pl.BlockSpec((1, D), lambda i, j: (0, 0))
    return pl.pallas_call(
        body, name=name, grid=(ni, nj),
        in_specs=[row, row, vec, col, col, wsp, wsp, wsp],
        out_specs=[row, vec, row, col, col, col],
        out_shape=[jax.ShapeDtypeStruct((S, D), F32), jax.ShapeDtypeStruct((1, D), F32),
                   jax.ShapeDtypeStruct((S, D), BF16), jax.ShapeDtypeStruct((S, F), BF16),
                   jax.ShapeDtypeStruct((S, F), BF16), jax.ShapeDtypeStruct((S, F), BF16)],
        scratch_shapes=[pltpu.VMEM((tm, D), BF16), pltpu.VMEM((tm, D), F32)],
        compiler_params=_cp("arbitrary", "arbitrary"),
    )(dxo, x, g, a, b, w1t, w3t, w2)


def _tn_matmul(a, b, name):
    S, M = a.shape
    N = b.shape[1]
    ts = min(MATMUL_TOKEN_TILE, S)
    tmm = next(t for t in WGRAD_ROW_TILES if M % t == 0)
    ns = S // ts

    def body(a_ref, b_ref, o_ref):
        s = pl.program_id(1)
        part = _tn(a_ref[...], b_ref[...])

        @pl.when(s == 0)
        def _():
            o_ref[...] = part

        @pl.when(s > 0)
        def _():
            o_ref[...] += part

    return pl.pallas_call(
        body, name=name, grid=(M // tmm, ns),
        in_specs=[pl.BlockSpec((ts, tmm), lambda m, s: (s, m)),
                  pl.BlockSpec((ts, N), lambda m, s: (s, 0))],
        out_specs=pl.BlockSpec((tmm, N), lambda m, s: (m, 0)),
        out_shape=jax.ShapeDtypeStruct((M, N), F32),
        compiler_params=_cp("arbitrary", "arbitrary"),
    )(a, b)


def _norm_matmul_nt(x, g, wt, out_dtype, name):
    S, D = x.shape
    N = wt.shape[0]
    tm = min(MATMUL_TOKEN_TILE, S)
    tn = next(t for t in (1024, 768, 256) if N % t == 0)

    def body(x_ref, g_ref, w_ref, o_ref, h_ref, hs):
        @pl.when(pl.program_id(1) == 0)
        def _():
            hb = _norm_fwd(x_ref[...], g_ref[...]).astype(BF16)
            hs[...] = hb
            h_ref[...] = hb

        o_ref[...] = _nt(hs[...], w_ref[...]).astype(out_dtype)

    return pl.pallas_call(
        body, name=name, grid=(S // tm, N // tn),
        in_specs=[pl.BlockSpec((tm, D), lambda i, j: (i, 0)),
                  pl.BlockSpec((1, D), lambda i, j: (0, 0)),
                  pl.BlockSpec((tn, D), lambda i, j: (j, 0))],
        out_specs=[pl.BlockSpec((tm, tn), lambda i, j: (i, j)),
                   pl.BlockSpec((tm, D), lambda i, j: (i, 0))],
        out_shape=[jax.ShapeDtypeStruct((S, N), out_dtype), jax.ShapeDtypeStruct((S, D), BF16)],
        scratch_shapes=[pltpu.VMEM((tm, D), BF16)],
        compiler_params=_cp("arbitrary", "arbitrary"),
    )(x, g, wt)


def _matmul_norm_bwd(dy, w, x, g, dres, name):
    S, K = dy.shape
    D = w.shape[1]
    tm = min(MATMUL_TOKEN_TILE, S)
    tk = next(t for t in (2176, 1024, 256) if K % t == 0)
    nk = K // tk

    def body(dy_ref, w_ref, x_ref, g_ref, dres_ref, dx_ref, dg_ref, acc):
        i, k = pl.program_id(0), pl.program_id(1)
        part = _nn(dy_ref[...], w_ref[...])

        @pl.when(k == 0)
        def _():
            acc[...] = part

        @pl.when(k > 0)
        def _():
            acc[...] += part

        @pl.when(k == nk - 1)
        def _():
            dx, dg = _norm_bwd(x_ref[...], g_ref[...], acc[...])
            dx_ref[...] = dres_ref[...] + dx

            @pl.when(i == 0)
            def _():
                dg_ref[...] = dg

            @pl.when(i > 0)
            def _():
                dg_ref[...] += dg

    row = pl.BlockSpec((tm, D), lambda i, k: (i, 0))
    vec = pl.BlockSpec((1, D), lambda i, k: (0, 0))
    return pl.pallas_call(
        body, name=name, grid=(S // tm, nk),
        in_specs=[pl.BlockSpec((tm, tk), lambda i, k: (i, k)),
                  pl.BlockSpec((tk, D), lambda i, k: (k, 0)), row, vec, row],
        out_specs=[row, vec],
        out_shape=[jax.ShapeDtypeStruct((S, D), F32), jax.ShapeDtypeStruct((1, D), F32)],
        scratch_shapes=[pltpu.VMEM((tm, D), F32)],
        compiler_params=_cp("arbitrary", "arbitrary"),
    )(dy, w, x, g, dres)


def _merge_fwd(x1, gates, o_a, o_b, wat, wbt, w_out):
    S, D = x1.shape
    W = o_a.shape[1]
    tm = min(TOKEN_TILE, S)

    def body(x_ref, ga_ref, gb_ref, oa_ref, ob_ref, wa_ref, wb_ref, wo_ref,
             x2_ref, mg_ref, ba_ref, bb_ref):
        ba = _nt(oa_ref[...], wa_ref[...])
        bb = _nt(ob_ref[...], wb_ref[...])
        merged = jax.nn.sigmoid(ga_ref[...]) * ba + jax.nn.sigmoid(gb_ref[...]) * bb
        mb = merged.astype(BF16)
        mg_ref[...] = mb
        ba_ref[...] = ba.astype(BF16)
        bb_ref[...] = bb.astype(BF16)
        x2_ref[...] = x_ref[...] + _nn(mb, wo_ref[...])

    row = pl.BlockSpec((tm, D), lambda i: (i, 0))
    full = lambda r, c: pl.BlockSpec((r, c), lambda i: (0, 0))
    return pl.pallas_call(
        body, name="merge_fwd", grid=(S // tm,),
        in_specs=[row, pl.BlockSpec((tm, D), lambda i: (i, 0)), pl.BlockSpec((tm, D), lambda i: (i, 1)),
                  pl.BlockSpec((tm, W), lambda i: (i, 0)), pl.BlockSpec((tm, W), lambda i: (i, 0)),
                  full(D, W), full(D, W), full(D, D)],
        out_specs=[row, row, row, row],
        out_shape=[jax.ShapeDtypeStruct((S, D), F32)] + [jax.ShapeDtypeStruct((S, D), BF16)] * 3,
        compiler_params=_cp("arbitrary"),
    )(x1, gates, gates, o_a, o_b, wat, wbt, w_out)


def _merge_bwd(dx2, gates, ba, bb, wat, wbt, w_out):
    S, D = dx2.shape
    W = wat.shape[1]
    tm = min(TOKEN_TILE, S)

    def body(dx_ref, ga_ref, gb_ref, ba_ref, bb_ref, wa_ref, wb_ref, wo_ref,
             dxb_ref, dba_ref, dbb_ref, dgt_ref, doa_ref, dob_ref):
        dxb = dx_ref[...].astype(BF16)
        dxb_ref[...] = dxb
        dm = _nt(dxb, wo_ref[...])
        sa = jax.nn.sigmoid(ga_ref[...])
        sb = jax.nn.sigmoid(gb_ref[...])
        dba = (dm * sa).astype(BF16)
        dbb = (dm * sb).astype(BF16)
        dba_ref[...] = dba
        dbb_ref[...] = dbb
        dgt_ref[:, :D] = (dm * ba_ref[...].astype(F32) * sa * (1.0 - sa)).astype(BF16)
        dgt_ref[:, D:] = (dm * bb_ref[...].astype(F32) * sb * (1.0 - sb)).astype(BF16)
        doa_ref[...] = _nn(dba, wa_ref[...])
        dob_ref[...] = _nn(dbb, wb_ref[...])

    row = pl.BlockSpec((tm, D), lambda i: (i, 0))
    full = lambda r, c: pl.BlockSpec((r, c), lambda i: (0, 0))
    return pl.pallas_call(
        body, name="merge_bwd", grid=(S // tm,),
        in_specs=[row, pl.BlockSpec((tm, D), lambda i: (i, 0)), pl.BlockSpec((tm, D), lambda i: (i, 1)),
                  row, row, full(D, W), full(D, W), full(D, D)],
        out_specs=[row, row, row, pl.BlockSpec((tm, 2 * D), lambda i: (i, 0)),
                   pl.BlockSpec((tm, W), lambda i: (i, 0)), pl.BlockSpec((tm, W), lambda i: (i, 0))],
        out_shape=[jax.ShapeDtypeStruct((S, D), BF16)] * 3 + [jax.ShapeDtypeStruct((S, 2 * D), BF16)]
                  + [jax.ShapeDtypeStruct((S, W), F32)] * 2,
        compiler_params=_cp("arbitrary"),
    )(dx2, gates, gates, ba, bb, wat, wbt, w_out)


def _final_loss(x3, gf, target):
    S, D = x3.shape
    tm = min(TOKEN_TILE, S)

    def body(x_ref, g_ref, t_ref, loss_ref, dx_ref, dg_ref):
        i = pl.program_id(0)
        x = x_ref[...]
        g = g_ref[...]
        e = _norm_fwd(x, g) - t_ref[...]
        part = 0.5 * jnp.sum(jnp.mean(e * e, axis=-1, keepdims=True), axis=0, keepdims=True)
        dx, dg = _norm_bwd(x, g, e * (1.0 / D))
        dx_ref[...] = dx

        @pl.when(i == 0)
        def _():
            loss_ref[...] = part
            dg_ref[...] = dg

        @pl.when(i > 0)
        def _():
            loss_ref[...] += part
            dg_ref[...] += dg

    row = pl.BlockSpec((tm, D), lambda i: (i, 0))
    vec = pl.BlockSpec((1, D), lambda i: (0, 0))
    return pl.pallas_call(
        body, name="final_loss", grid=(S // tm,),
        in_specs=[row, vec, row],
        out_specs=[pl.BlockSpec((1, 1), lambda i: (0, 0)), row, vec],
        out_shape=[jax.ShapeDtypeStruct((1, 1), F32), jax.ShapeDtypeStruct((S, D), F32),
                   jax.ShapeDtypeStruct((1, D), F32)],
        compiler_params=_cp("arbitrary"),
    )(x3, gf, target)


SB_HEAD_GROUP = 4
LANES = 128


def _tri(T, kind):
    r = lax.broadcasted_iota(jnp.int32, (T, T), 0)
    c = lax.broadcasted_iota(jnp.int32, (T, T), 1)
    return {"after": r > c, "upto": r <= c, "before": r < c}[kind].astype(BF16)


def _lane(v, j):
    return jnp.broadcast_to(v[:, j:j + 1], (v.shape[0], LANES))


def _t_bf16(x):
    return x.astype(F32).T.astype(BF16)


def _wide(v, T):
    return jnp.tile(v, (1, T // LANES))


def _sb_fwd(qt, kt, vt):
    H, nb, dh, T = qt.shape
    HG = SB_HEAD_GROUP

    def body(q_ref, k_ref, v_ref, o_ref, tl_ref):
        row = lax.broadcasted_iota(jnp.int32, (T, T), 0)
        col = lax.broadcasted_iota(jnp.int32, (T, T), 1)
        tri = col < row
        after = _tri(T, "after")

        def blocks(qs, kb, carry, diag):
            hs = range(HG)
            z = [_nn(qs[hh], k_ref[hh, kb]) for hh in hs]
            res, ls, first = [None] * HG, [None] * HG, [None] * HG
            for hh in hs:
                sp = _softplus(z[hh])
                if diag:
                    sp = jnp.where(tri, sp, 0.0)
                ls[hh] = z[hh] - sp
                spb = sp.astype(BF16)
                first[hh] = _lane(spb.astype(F32), 0)
                res[hh] = _nn(spb, after)
            out = []
            for hh in hs:
                c, oacc = carry[2 * hh], carry[2 * hh + 1]
                a = jnp.exp(ls[hh] - (res[hh] + _wide(c, T)))
                if diag:
                    a = jnp.where(tri, a, 0.0)
                out.extend([c + (first[hh] + _lane(res[hh], 0)), oacc + _nt(v_ref[hh, kb], a.astype(BF16))])
            return tuple(out)

        def qblock(i, _):
            qs = [_t_bf16(q_ref[hh, i]) for hh in range(HG)]
            carry = blocks(qs, i, (jnp.zeros((T, LANES), F32), jnp.zeros((dh, T), F32)) * HG, True)

            def kstep(t, carry):
                return blocks(qs, i - 1 - t, carry, False)

            carry = lax.fori_loop(0, i, kstep, carry)
            for hh in range(HG):
                o_ref[hh, i] = carry[2 * hh + 1]
                tl_ref[hh, i] = carry[2 * hh].T[:8]
            return 0

        lax.fori_loop(0, nb, qblock, 0)

    ht = pl.BlockSpec((HG, nb, dh, T), lambda h: (h, 0, 0, 0))
    return pl.pallas_call(
        body, name="sb_fwd", grid=(H // HG,),
        in_specs=[ht, ht, ht],
        out_specs=[ht, pl.BlockSpec((HG, nb, 8, T), lambda h: (h, 0, 0, 0))],
        out_shape=[jax.ShapeDtypeStruct((H, nb, dh, T), F32), jax.ShapeDtypeStruct((H, nb, 8, T), F32)],
        compiler_params=_cp("arbitrary"),
    )(qt, kt, vt)


def _sb_bwd(qt, kt, vt, dot, tl):
    H, nb, dh, T = qt.shape
    HG = SB_HEAD_GROUP

    def body(qt_ref, k_ref, v_ref, dot_ref, tl_ref, dq_ref, dk_ref, dv_ref):
        row = lax.broadcasted_iota(jnp.int32, (T, T), 0)
        col = lax.broadcasted_iota(jnp.int32, (T, T), 1)
        tri = col < row
        upto = _tri(T, "upto")
        before = _tri(T, "before")
        dk_ref[...] = jnp.zeros_like(dk_ref)
        dv_ref[...] = jnp.zeros_like(dv_ref)

        def blocks(qs, qTs, dos, doTs, kb, carry, diag):
            hs = range(HG)
            kT = [k_ref[hh, kb] for hh in hs]
            z = [_nn(qs[hh], kT[hh]) for hh in hs]
            da = [_nn(dos[hh], v_ref[hh, kb]) for hh in hs]
            res, ls = [None] * HG, [None] * HG
            for hh in hs:
                sp = _softplus(z[hh])
                if diag:
                    sp = jnp.where(tri, sp, 0.0)
                ls[hh] = z[hh] - sp
                res[hh] = _nn(sp.astype(BF16), upto)
            g, gb, ab, resg = [None] * HG, [None] * HG, [None] * HG, [None] * HG
            for hh in hs:
                a = jnp.exp(ls[hh] + (res[hh] - _wide(carry[3 * hh], T)))
                if diag:
                    a = jnp.where(tri, a, 0.0)
                ab[hh] = a.astype(BF16)
                g[hh] = a * da[hh]
                gb[hh] = g[hh].astype(BF16)
                resg[hh] = _nn(gb[hh], before)
            out = []
            for hh in hs:
                rem, pre_g, dq = carry[3 * hh:3 * hh + 3]
                dz = g[hh] - (g[hh] + (resg[hh] + _wide(pre_g, T))) * jnp.exp(ls[hh])
                if diag:
                    dz = jnp.where(tri, dz, 0.0)
                dzb = dz.astype(BF16)
                dk_ref[hh, kb] += _nn(qTs[hh], dzb)
                dv_ref[hh, kb] += _nn(doTs[hh], ab[hh])
                out.extend([rem - _lane(res[hh], T - 1), pre_g + (_lane(resg[hh], T - 1) + _lane(gb[hh].astype(F32), T - 1)),
                            dq + _nt(kT[hh], dzb)])
            return tuple(out)

        def qblock(i, _):
            qTs = [qt_ref[hh, i] for hh in range(HG)]
            doTs = [dot_ref[hh, i] for hh in range(HG)]
            qs = [_t_bf16(v) for v in qTs]
            dos = [_t_bf16(v) for v in doTs]
            carry = []
            for hh in range(HG):
                total = jnp.broadcast_to(tl_ref[hh, i][0:1], (LANES, T)).T
                carry.extend([total, jnp.zeros((T, LANES), F32), jnp.zeros((dh, T), F32)])

            def kstep(kb, carry):
                return blocks(qs, qTs, dos, doTs, kb, carry, False)

            carry = lax.fori_loop(0, i, kstep, tuple(carry))
            carry = blocks(qs, qTs, dos, doTs, i, carry, True)
            for hh in range(HG):
                dq_ref[hh, i] = carry[3 * hh + 2]
            return 0

        lax.fori_loop(0, nb, qblock, 0)

    ht = pl.BlockSpec((HG, nb, dh, T), lambda h: (h, 0, 0, 0), pipeline_mode=pl.Buffered(1))
    return pl.pallas_call(
        body, name="sb_bwd", grid=(H // HG,),
        in_specs=[ht, ht, ht, ht, pl.BlockSpec((HG, nb, 8, T), lambda h: (h, 0, 0, 0))],
        out_specs=[ht, ht, ht],
        out_shape=[jax.ShapeDtypeStruct((H, nb, dh, T), F32)] * 3,
        compiler_params=_cp("arbitrary"),
    )(qt, kt, vt, dot, tl)


def _swa_probs(q, kp, kc, bias, sink, first):
    T = q.shape[0]
    row = lax.broadcasted_iota(jnp.int32, (T, T), 0)
    col = lax.broadcasted_iota(jnp.int32, (T, T), 1)
    lp = jnp.where(jnp.logical_and(col > row, jnp.logical_not(first)), _nn(q, kp) + bias[:, :T], NEG_BIG)
    lc = jnp.where(col <= row, _nn(q, kc) + bias[:, T:], NEG_BIG)
    m = jnp.maximum(jnp.maximum(jnp.max(lp, axis=1, keepdims=True), jnp.max(lc, axis=1, keepdims=True)), sink)
    pp = jnp.exp(lp - m)
    pc = jnp.exp(lc - m)
    ps = jnp.exp(sink - m)
    inv = 1.0 / (jnp.sum(pp, axis=1, keepdims=True) + jnp.sum(pc, axis=1, keepdims=True) + ps)
    return pp * inv, pc * inv, ps * inv


def _swa_fwd(qn, kt, vt, bias, sinks):
    Hq, nb, T, dh = qn.shape
    grp = Hq // kt.shape[0]

    def body(sink_ref, q_ref, kp_ref, kc_ref, vp_ref, vc_ref, bias_ref, o_ref):
        hk, n = pl.program_id(0), pl.program_id(1)
        kp, kc, vp, vc = kp_ref[...], kc_ref[...], vp_ref[...], vc_ref[...]
        for g in range(grp):
            pp, pc, _ = _swa_probs(q_ref[g], kp, kc, bias_ref[g], sink_ref[hk * grp + g], n == 0)
            o_ref[g] = _nt(vp, pp.astype(BF16)) + _nt(vc, pc.astype(BF16))

    prev = pl.BlockSpec((None, None, dh, T), lambda h, n: (h, jnp.maximum(n - 1, 0), 0, 0))
    cur = pl.BlockSpec((None, None, dh, T), lambda h, n: (h, n, 0, 0))
    return pl.pallas_call(
        body, name="swa_fwd", grid=(Hq // grp, nb),
        in_specs=[pl.BlockSpec(memory_space=pltpu.SMEM),
                  pl.BlockSpec((grp, None, T, dh), lambda h, n: (h, n, 0, 0)),
                  prev, cur, prev, cur,
                  pl.BlockSpec((grp, T, 2 * T), lambda h, n: (h, 0, 0))],
        out_specs=pl.BlockSpec((grp, None, dh, T), lambda h, n: (h, n, 0, 0)),
        out_shape=jax.ShapeDtypeStruct((Hq, nb, dh, T), F32),
        compiler_params=_cp("arbitrary", "arbitrary"),
    )(sinks, qn, kt, kt, vt, vt, bias)


def _swa_bwd(qn, qt, kt, vt, bias, sinks, don, dot, on):
    Hq, nb, T, dh = qn.shape
    Hkv = kt.shape[0]
    grp = Hq // Hkv

    def body(sink_ref, q_ref, qt_ref, kp_ref, kc_ref, vp_ref, vc_ref, bias_ref, do_ref, dot_ref, o_ref,
             dq_ref, dk_ref, dv_ref, dbias_ref, dsink_ref, ck, cv):
        hk, n = pl.program_id(0), pl.program_id(1)

        @pl.when(n == 0)
        def _():
            dbias_ref[...] = jnp.zeros_like(dbias_ref)
            dsink_ref[...] = jnp.zeros_like(dsink_ref)
            ck[...] = jnp.zeros_like(ck)
            cv[...] = jnp.zeros_like(cv)

        @pl.when(n < nb)
        def _():
            kp, kc, vp, vc = kp_ref[...], kc_ref[...], vp_ref[...], vc_ref[...]
            kprev = jnp.zeros((dh, T), F32)
            vprev = jnp.zeros((dh, T), F32)
            kcur = jnp.zeros((dh, T), F32)
            vcur = jnp.zeros((dh, T), F32)
            for g in range(grp):
                pp, pc, ps = _swa_probs(q_ref[g], kp, kc, bias_ref[g], sink_ref[hk * grp + g], n == 0)
                do = do_ref[g]
                dob = do.astype(BF16)
                delta = jnp.sum(do * o_ref[g], axis=1, keepdims=True)
                dlp = pp * (_nn(dob, vp) - delta)
                dlc = pc * (_nn(dob, vc) - delta)
                dbias_ref[g, :, :T] += dlp
                dbias_ref[g, :, T:] += dlc
                dsink_ref[g] += -ps * delta
                dlpb, dlcb = dlp.astype(BF16), dlc.astype(BF16)
                dq_ref[g] = _nt(kp, dlpb) + _nt(kc, dlcb)
                qT, doT = qt_ref[g], dot_ref[g]
                kprev += _nn(qT, dlpb)
                kcur += _nn(qT, dlcb)
                vprev += _nn(doT, pp.astype(BF16))
                vcur += _nn(doT, pc.astype(BF16))
            dk_ref[...] = ck[...] + kprev
            dv_ref[...] = cv[...] + vprev
            ck[...] = kcur
            cv[...] = vcur

        @pl.when(n == nb)
        def _():
            dk_ref[...] = ck[...]
            dv_ref[...] = cv[...]

    qn_spec = pl.BlockSpec((grp, None, T, dh), lambda h, n: (h, jnp.minimum(n, nb - 1), 0, 0))
    qt_spec = pl.BlockSpec((grp, None, dh, T), lambda h, n: (h, jnp.minimum(n, nb - 1), 0, 0))
    prev = pl.BlockSpec((None, None, dh, T), lambda h, n: (h, jnp.maximum(n - 1, 0), 0, 0))
    cur = pl.BlockSpec((None, None, dh, T), lambda h, n: (h, jnp.minimum(n, nb - 1), 0, 0))
    per_group = lambda a, b: pl.BlockSpec((grp, a, b), lambda h, n: (h, 0, 0))
    return pl.pallas_call(
        body, name="swa_bwd", grid=(Hkv, nb + 1),
        in_specs=[pl.BlockSpec(memory_space=pltpu.SMEM), qn_spec, qt_spec, prev, cur, prev, cur,
                  per_group(T, 2 * T), qn_spec, qt_spec, qn_spec],
        out_specs=[qt_spec, prev, prev, per_group(T, 2 * T), per_group(T, 1)],
        out_shape=[jax.ShapeDtypeStruct((Hq, nb, dh, T), F32), jax.ShapeDtypeStruct((Hkv, nb, dh, T), F32),
                   jax.ShapeDtypeStruct((Hkv, nb, dh, T), F32), jax.ShapeDtypeStruct((Hq, T, 2 * T), F32),
                   jax.ShapeDtypeStruct((Hq, T, 1), F32)],
        scratch_shapes=[pltpu.VMEM((dh, T), F32), pltpu.VMEM((dh, T), F32)],
        compiler_params=_cp("arbitrary", "arbitrary"),
    )(sinks, qn, qt, kt, kt, vt, vt, bias, don, dot, on)


def _split3(x):
    h1 = x.astype(BF16)
    r1 = x - h1.astype(F32)
    h2 = r1.astype(BF16)
    h3 = (r1 - h2.astype(F32)).astype(BF16)
    return h1, h2, h3


def _bias_expand(rel_t, onehot):
    Hq, NB = rel_t.shape
    L = onehot.shape[1]

    def body(r_ref, oh_ref, o_ref):
        h1, h2, h3 = _split3(r_ref[...])
        oh = oh_ref[...]
        o_ref[...] = _nn(h1, oh) + _nn(h2, oh) + _nn(h3, oh)

    return pl.pallas_call(
        body, name="bias_expand", grid=(1,),
        in_specs=[pl.BlockSpec((Hq, NB), lambda i: (0, 0)), pl.BlockSpec((NB, L), lambda i: (0, 0))],
        out_specs=pl.BlockSpec((Hq, L), lambda i: (0, 0)),
        out_shape=jax.ShapeDtypeStruct((Hq, L), F32),
        compiler_params=_cp("arbitrary"),
    )(rel_t, onehot)


def _bias_reduce(dbias, onehot):
    Hq, L = dbias.shape
    NB = onehot.shape[0]

    def body(d_ref, oh_ref, o_ref):
        h1, h2, h3 = _split3(d_ref[...])
        oh = oh_ref[...]
        o_ref[...] = _nt(h1, oh) + _nt(h2, oh) + _nt(h3, oh)

    return pl.pallas_call(
        body, name="bias_reduce", grid=(1,),
        in_specs=[pl.BlockSpec((Hq, L), lambda i: (0, 0)), pl.BlockSpec((NB, L), lambda i: (0, 0))],
        out_specs=pl.BlockSpec((Hq, NB), lambda i: (0, 0)),
        out_shape=jax.ShapeDtypeStruct((Hq, NB), F32),
        compiler_params=_cp("arbitrary"),
    )(dbias, onehot)


def _adamw(w, g, m, v, name):
    R, C = w.shape
    tr = 256 if R % 256 == 0 else R
    bc1 = 1.0 - ADAM_B1 ** ADAM_STEP
    bc2 = 1.0 - ADAM_B2 ** ADAM_STEP

    def body(w_ref, g_ref, m_ref, v_ref, d_ref, nm_ref, nv_ref):
        g = g_ref[...]
        m2 = ADAM_B1 * m_ref[...] + (1.0 - ADAM_B1) * g
        v2 = ADAM_B2 * v_ref[...] + (1.0 - ADAM_B2) * (g * g)
        nm_ref[...] = m2
        nv_ref[...] = v2
        d_ref[...] = -ADAM_LR * ((m2 / bc1) / (jnp.sqrt(v2 / bc2) + ADAM_EPS) + ADAM_WD * w_ref[...])

    spec = pl.BlockSpec((tr, C), lambda i: (i, 0))
    return pl.pallas_call(
        body, name=name, grid=(R // tr,),
        in_specs=[spec] * 4, out_specs=[spec] * 3,
        out_shape=[jax.ShapeDtypeStruct((R, C), F32)] * 3,
        compiler_params=_cp("arbitrary"),
    )(w, g, m, v)


def _add_halves(mine, recv, name):
    K, R, C = mine.shape
    tr = 416 if R % 416 == 0 else R

    def body(a_ref, b_ref, o_ref, ob_ref):
        s = a_ref[...] + b_ref[...]
        o_ref[...] = s
        ob_ref[...] = s.astype(BF16)

    spec = pl.BlockSpec((None, tr, C), lambda k, i: (k, i, 0))
    return pl.pallas_call(
        body, name=name, grid=(K, R // tr),
        in_specs=[spec, spec], out_specs=[spec, spec],
        out_shape=[jax.ShapeDtypeStruct((K, R, C), F32), jax.ShapeDtypeStruct((K, R, C), BF16)],
        compiler_params=_cp("arbitrary", "arbitrary"),
    )(mine, recv)


def _add_received(own, recv, name):
    R, C = own.shape
    tr = 416 if R % 416 == 0 else R

    def body(a_ref, r_ref, o_ref):
        o_ref[...] = ((a_ref[...] + r_ref[0].astype(F32)) + r_ref[1].astype(F32)) + r_ref[2].astype(F32)

    return pl.pallas_call(
        body, name=name, grid=(R // tr,),
        in_specs=[pl.BlockSpec((tr, C), lambda i: (i, 0)), pl.BlockSpec((3, tr, C), lambda i: (0, i, 0))],
        out_specs=pl.BlockSpec((tr, C), lambda i: (i, 0)),
        out_shape=jax.ShapeDtypeStruct((R, C), F32),
        compiler_params=_cp("arbitrary"),
    )(own, recv)


def _position():
    x, y, c = lax.axis_index("x"), lax.axis_index("y"), lax.axis_index("c")
    others = [(1 - x, y), (x, 1 - y), (1 - x, 1 - y)]
    return x, y, c, others


def _remote(src, dst, send_sems, recv_sems, k, dev):
    return pltpu.make_async_remote_copy(src_ref=src, dst_ref=dst, send_sem=send_sems.at[k],
                                        recv_sem=recv_sems.at[k], device_id=dev, device_id_type=MESH_ID)


def _gather_weights(shard):
    R, C = shard.shape
    half = R // 2

    def body(src, out, send_sems, recv_sems, local_sem):
        x, y, c, others = _position()
        mine = 2 * x + y
        local = pltpu.make_async_copy(src, out.at[mine], local_sem)
        local.start()
        sends = [_remote(src.at[c], out.at[mine, c], send_sems, recv_sems, j, (ox, oy, c))
                 for j, (ox, oy) in enumerate(others)]
        for cp in sends:
            cp.start()
        passed = []
        for j, (ox, oy) in enumerate(others):
            slot = out.at[2 * ox + oy, c]
            _remote(slot, slot, send_sems, recv_sems, j, (ox, oy, c)).wait_recv()
            fwd = _remote(slot, slot, send_sems, recv_sems, 3 + j, (x, y, 1 - c))
            fwd.start()
            passed.append(fwd)
        for j, (ox, oy) in enumerate(others):
            slot = out.at[2 * ox + oy, 1 - c]
            _remote(slot, slot, send_sems, recv_sems, 3 + j, (x, y, 1 - c)).wait_recv()
        for cp in sends + passed:
            cp.wait_send()
        local.wait()

    return pl.pallas_call(
        body, name="gather_weights",
        in_specs=[ANY], out_specs=ANY,
        out_shape=jax.ShapeDtypeStruct((N_CHIPS, 2, half, C), shard.dtype),
        scratch_shapes=[pltpu.SemaphoreType.DMA((6,)), pltpu.SemaphoreType.DMA((6,)), pltpu.SemaphoreType.DMA],
    )(shard.reshape(2, half, C)).reshape(N_CHIPS, R, C)


def _swap_halves(grads):
    K, R, C = grads.shape
    half = R // 2

    def body(src, out, send_sems, recv_sems):
        x, y, c, _ = _position()
        theirs = src.at[:, pl.ds(pl.multiple_of((1 - c) * half, 8), half), :]
        cp = _remote(theirs, out, send_sems, recv_sems, 0, (x, y, 1 - c))
        cp.start()
        cp.wait()

    return pl.pallas_call(
        body, name="swap_halves",
        in_specs=[ANY], out_specs=ANY,
        out_shape=jax.ShapeDtypeStruct((K, half, C), grads.dtype),
        scratch_shapes=[pltpu.SemaphoreType.DMA((1,)), pltpu.SemaphoreType.DMA((1,))],
    )(grads)


def _scatter_to_owners(parts):
    K, H, C = parts.shape

    def body(src, out, send_sems, recv_sems):
        x, y, c, others = _position()
        sends = [_remote(src.at[2 * ox + oy], out.at[j], send_sems, recv_sems, j, (ox, oy, c))
                 for j, (ox, oy) in enumerate(others)]
        for cp in sends:
            cp.start()
        for cp in sends:
            cp.wait()

    return pl.pallas_call(
        body, name="scatter_to_owners",
        in_specs=[ANY], out_specs=ANY,
        out_shape=jax.ShapeDtypeStruct((3, H, C), parts.dtype),
        scratch_shapes=[pltpu.SemaphoreType.DMA((3,)), pltpu.SemaphoreType.DMA((3,))],
    )(parts)


def _join_halves(half_rows):
    H, C = half_rows.shape

    def body(src, out, send_sems, recv_sems, local_sem):
        x, y, c, _ = _position()
        local = pltpu.make_async_copy(src, out.at[c], local_sem)
        local.start()
        cp = _remote(src, out.at[c], send_sems, recv_sems, 0, (x, y, 1 - c))
        cp.start()
        theirs = out.at[1 - c]
        _remote(theirs, theirs, send_sems, recv_sems, 0, (x, y, 1 - c)).wait_recv()
        cp.wait_send()
        local.wait()

    return pl.pallas_call(
        body, name="join_halves",
        in_specs=[ANY], out_specs=ANY,
        out_shape=jax.ShapeDtypeStruct((2, H, C), half_rows.dtype),
        scratch_shapes=[pltpu.SemaphoreType.DMA((1,)), pltpu.SemaphoreType.DMA((1,)), pltpu.SemaphoreType.DMA],
    )(half_rows).reshape(2 * H, C)


def _allreduce_small(block):
    R, C = block.shape
    n_dev = 8

    def body(src, out, slots, send_sems, recv_sems):
        x, y, c, _ = _position()
        me = 4 * x + 2 * y + c
        slots[me] = src[...]
        sends = []
        for r in range(1, n_dev):
            peer = (x ^ (r >> 2), y ^ ((r >> 1) & 1), c ^ (r & 1))
            cp = _remote(src, slots.at[me], send_sems, recv_sems, r - 1, peer)
            cp.start()
            sends.append(cp)
        for r in range(1, n_dev):
            theirs = slots.at[me ^ r]
            _remote(theirs, theirs, send_sems, recv_sems, r - 1, (x, y, c)).wait_recv()
        for cp in sends:
            cp.wait_send()
        acc = slots[0]
        for d in range(1, n_dev):
            acc = acc + slots[d]
        out[...] = acc

    return pl.pallas_call(
        body, name="allreduce_small",
        in_specs=[pl.BlockSpec(memory_space=pltpu.VMEM)], out_specs=pl.BlockSpec(memory_space=pltpu.VMEM),
        out_shape=jax.ShapeDtypeStruct((R, C), F32),
        scratch_shapes=[pltpu.VMEM((n_dev, R, C), F32), pltpu.SemaphoreType.DMA((7,)), pltpu.SemaphoreType.DMA((7,))],
    )(block)


def _heads_n(a, T, scale=None):
    S, W = a.shape
    a = a.reshape(S // T, T, W // HEAD_DIM, HEAD_DIM).transpose(2, 0, 1, 3)
    if scale is not None:
        a = a * scale
    return a.astype(BF16)


def _heads_t(a, T, scale=None):
    S, W = a.shape
    a = a.reshape(S // T, T, W // HEAD_DIM, HEAD_DIM).transpose(2, 0, 3, 1)
    if scale is not None:
        a = a * scale
    return a.astype(BF16)


def _heads_n_f32(a, T):
    S, W = a.shape
    return a.reshape(S // T, T, W // HEAD_DIM, HEAD_DIM).transpose(2, 0, 1, 3)


def _from_heads_t(a):
    H, nb, dh, T = a.shape
    return a.transpose(1, 3, 0, 2).reshape(nb * T, H * dh)


def _rel_bucket(dist):
    max_exact = REL_BUCKETS // 2
    d = jnp.maximum(dist, 1).astype(F32)
    large = max_exact + (jnp.log(d / max_exact) / math.log(REL_MAX_DIST / max_exact)
                         * (REL_BUCKETS - max_exact)).astype(jnp.int32)
    large = jnp.minimum(large, REL_BUCKETS - 1)
    return jnp.where(dist < max_exact, dist, large)


def _bucket_onehot():
    T = SWA_BLOCK
    dist = (jnp.arange(T)[:, None] + T) - jnp.arange(2 * T)[None, :]
    bucket = _rel_bucket(jnp.maximum(dist, 0)).reshape(1, T * 2 * T)
    return (bucket == jnp.arange(REL_BUCKETS)[:, None]).astype(BF16)


_BUF = (("ffn1_w1", "t"), ("ffn1_w3", "t"), ("ffn1_w2", "n"), ("ffn2_w1", "t"), ("ffn2_w3", "t"),
        ("ffn2_w2", "n"), ("w_in", "t"), ("w_out", "n"), ("w_branch_swa", "tw"), ("w_branch_sb", "tw"))


def _to_rows(name_kind, w, D):
    kind = name_kind[1]
    if kind == "n":
        return w
    if kind == "t":
        return w.T
    return w.T.reshape(-1, D)


def _from_rows(name_kind, rows, width):
    kind = name_kind[1]
    if kind == "n":
        return rows
    if kind == "t":
        return rows.T
    return rows.reshape(-1, width).T


def kernel(x, norm_ffn1, ffn1_w1, ffn1_w3, ffn1_w2, norm_mix, w_in, swa_sinks, rel_bias, w_branch_swa, w_branch_sb, w_out, norm_ffn2, ffn2_w1, ffn2_w3, ffn2_w2, norm_final, loss_target, m_norm_ffn1, m_ffn1_w1, m_ffn1_w3, m_ffn1_w2, m_norm_mix, m_w_in, m_swa_sinks, m_rel_bias, m_w_branch_swa, m_w_branch_sb, m_w_out, m_norm_ffn2, m_ffn2_w1, m_ffn2_w3, m_ffn2_w2, m_norm_final, v_norm_ffn1, v_ffn1_w1, v_ffn1_w3, v_ffn1_w2, v_norm_mix, v_w_in, v_swa_sinks, v_rel_bias, v_w_branch_swa, v_w_branch_sb, v_w_out, v_norm_ffn2, v_ffn2_w1, v_ffn2_w3, v_ffn2_w2, v_norm_final):
    names = ["norm_ffn1", "ffn1_w1", "ffn1_w3", "ffn1_w2", "norm_mix", "w_in", "swa_sinks", "rel_bias",
             "w_branch_swa", "w_branch_sb", "w_out", "norm_ffn2", "ffn2_w1", "ffn2_w3", "ffn2_w2", "norm_final"]
    W = dict(zip(names, [norm_ffn1, ffn1_w1, ffn1_w3, ffn1_w2, norm_mix, w_in, swa_sinks, rel_bias,
                         w_branch_swa, w_branch_sb, w_out, norm_ffn2, ffn2_w1, ffn2_w3, ffn2_w2, norm_final]))
    M = dict(zip(names, [m_norm_ffn1, m_ffn1_w1, m_ffn1_w3, m_ffn1_w2, m_norm_mix, m_w_in, m_swa_sinks, m_rel_bias,
                         m_w_branch_swa, m_w_branch_sb, m_w_out, m_norm_ffn2, m_ffn2_w1, m_ffn2_w3, m_ffn2_w2,
                         m_norm_final]))
    V = dict(zip(names, [v_norm_ffn1, v_ffn1_w1, v_ffn1_w3, v_ffn1_w2, v_norm_mix, v_w_in, v_swa_sinks, v_rel_bias,
                         v_w_branch_swa, v_w_branch_sb, v_w_out, v_norm_ffn2, v_ffn2_w1, v_ffn2_w3, v_ffn2_w2,
                         v_norm_final]))
    xs = x[0]
    target = loss_target[0]
    S, D = xs.shape
    QW = SWA_Q_HEADS * HEAD_DIM
    KW = SWA_KV_HEADS * HEAD_DIM
    BW = SB_HEADS * HEAD_DIM
    QKV = QW + 2 * KW + 3 * BW

    pieces = [_to_rows(nk, W[nk[0]][0], D) for nk in _BUF]
    sizes = [p.shape[0] for p in pieces]
    offs = [0]
    for s in sizes:
        offs.append(offs[-1] + s)
    shard = jnp.concatenate(pieces, axis=0).astype(BF16)
    gathered = _gather_weights(shard)

    def full(i):
        return gathered[:, offs[i]:offs[i + 1], :].reshape(N_CHIPS * sizes[i], D)

    f1w1, f1w3, f1w2, f2w1, f2w3, f2w2, w_in_t, w_out_f = [full(i) for i in range(8)]
    wa_t = full(8).reshape(D, QW)
    wb_t = full(9).reshape(D, BW)

    g1, gmix, g3 = W["norm_ffn1"], W["norm_mix"], W["norm_ffn2"]
    gf = W["norm_final"].reshape(1, D)

    x1, h1, a1, b1 = _ffn_fwd(xs, g1, f1w1, f1w3, f1w2, "ffn1_fwd")
    qkv, h2 = _norm_matmul_nt(x1, gmix, w_in_t[:QKV], BF16, "proj_qkv")
    gates, _ = _norm_matmul_nt(x1, gmix, w_in_t[QKV:], F32, "proj_gates")

    q_a, k_a, v_a = qkv[:, :QW], qkv[:, QW:QW + KW], qkv[:, QW + KW:QW + 2 * KW]
    o0 = QW + 2 * KW
    q_b, k_b, v_b = qkv[:, o0:o0 + BW], qkv[:, o0 + BW:o0 + 2 * BW], qkv[:, o0 + 2 * BW:o0 + 3 * BW]

    onehot = _bucket_onehot()
    bias = _bias_expand(W["rel_bias"].T, onehot).reshape(SWA_Q_HEADS, SWA_BLOCK, 2 * SWA_BLOCK)
    sinks = W["swa_sinks"].reshape(SWA_Q_HEADS)
    qa_n, qa_t = _heads_n(q_a, SWA_BLOCK, QK_SCALE), _heads_t(q_a, SWA_BLOCK, QK_SCALE)
    ka_t, va_t = _heads_t(k_a, SWA_BLOCK), _heads_t(v_a, SWA_BLOCK)
    oa_t = _swa_fwd(qa_n, ka_t, va_t, bias, sinks)

    qb_t = _heads_t(q_b, SB_BLOCK, QK_SCALE)
    kb_t, vb_t = _heads_t(k_b, SB_BLOCK), _heads_t(v_b, SB_BLOCK)
    ob_t, totals = _sb_fwd(qb_t, kb_t, vb_t)

    o_a = _from_heads_t(oa_t)
    o_b = _from_heads_t(ob_t)
    o_a16, o_b16 = o_a.astype(BF16), o_b.astype(BF16)
    x2, merged, ba, bb = _merge_fwd(x1, gates, o_a16, o_b16, wa_t, wb_t, w_out_f)
    x3, h3, a2, b2 = _ffn_fwd(x2, g3, f2w1, f2w3, f2w2, "ffn2_fwd")
    loss_part, dx3, dgf = _final_loss(x3, gf, target)

    dx2, dg3, dz2, da2, db2, u2 = _ffn_bwd(dx3, x2, g3, a2, b2, f2w1, f2w3, f2w2, "ffn2_bwd")
    grads = {}
    grads["ffn2_w1"] = _tn_matmul(da2, h3, "ffn2_dw1")
    grads["ffn2_w3"] = _tn_matmul(db2, h3, "ffn2_dw3")
    grads["ffn2_w2"] = _tn_matmul(u2, dz2, "ffn2_dw2")

    dx2b, dba, dbb, dgates, do_a, do_b = _merge_bwd(dx2, gates, ba, bb, wa_t, wb_t, w_out_f)
    grads["w_out"] = _tn_matmul(merged, dx2b, "dw_out")
    grads["w_branch_swa"] = _tn_matmul(dba, o_a16, "dw_branch_swa")
    grads["w_branch_sb"] = _tn_matmul(dbb, o_b16, "dw_branch_sb")

    dqb_t, dkb_t, dvb_t = _sb_bwd(qb_t, kb_t, vb_t, _heads_t(do_b, SB_BLOCK), totals)
    dqa_t, dka_t, dva_t, dbias, dsink_rows = _swa_bwd(
        qa_n, qa_t, ka_t, va_t, bias, sinks, _heads_n_f32(do_a, SWA_BLOCK), _heads_t(do_a, SWA_BLOCK),
        _heads_n_f32(o_a, SWA_BLOCK))
    d_rel = _bias_reduce(dbias.reshape(SWA_Q_HEADS, -1), onehot).T
    d_sinks = jnp.sum(dsink_rows, axis=(1, 2))

    dqkv = jnp.concatenate([_from_heads_t(dqa_t) * QK_SCALE, _from_heads_t(dka_t), _from_heads_t(dva_t),
                            _from_heads_t(dqb_t) * QK_SCALE, _from_heads_t(dkb_t), _from_heads_t(dvb_t)],
                           axis=1).astype(BF16)
    dproj = jnp.concatenate([dqkv, dgates], axis=1)
    grads["w_in"] = _tn_matmul(dproj, h2, "dw_in")
    dx1, dgmix = _matmul_norm_bwd(dproj, w_in_t, x1, gmix, dx2, "proj_bwd")

    dx0, dg1, dz1, da1, db1, u1 = _ffn_bwd(dx1, xs, g1, a1, b1, f1w1, f1w3, f1w2, "ffn1_bwd")
    grads["ffn1_w1"] = _tn_matmul(da1, h1, "ffn1_dw1")
    grads["ffn1_w3"] = _tn_matmul(db1, h1, "ffn1_dw3")
    grads["ffn1_w2"] = _tn_matmul(u1, dz1, "ffn1_dw2")

    gparts = [grads[nk[0]].reshape(N_CHIPS, sizes[i], D) for i, nk in enumerate(_BUF)]
    gbuf = jnp.concatenate(gparts, axis=1)
    R = gbuf.shape[1]
    half = R // 2
    c = lax.axis_index("c")
    mine = 2 * lax.axis_index("x") + lax.axis_index("y")
    from_sibling = _swap_halves(gbuf)
    my_half = lax.dynamic_slice_in_dim(gbuf, c * half, half, axis=1)
    chip_sum, chip_sum16 = _add_halves(my_half, from_sibling, "add_sibling")
    received = _scatter_to_owners(chip_sum16)
    own = lax.dynamic_index_in_dim(chip_sum, mine, axis=0, keepdims=False)
    reduced = _join_halves(_add_received(own, received, "add_chips"))

    small_rows = [dg1, dgmix, dg3, dgf,
                  jnp.pad(d_sinks.reshape(1, -1), ((0, 0), (0, D - SWA_Q_HEADS))),
                  jnp.pad(d_rel.reshape(1, -1), ((0, 0), (0, D - REL_BUCKETS * SWA_Q_HEADS))),
                  jnp.pad(loss_part, ((0, 0), (0, D - 1))), jnp.zeros((1, D), F32)]
    small = _allreduce_small(jnp.concatenate(small_rows, axis=0))
    loss = small[6, 0]

    G = {}
    for i, nk in enumerate(_BUF):
        G[nk[0]] = _from_rows(nk, reduced[offs[i]:offs[i + 1]], W[nk[0]].shape[1])[None]
    G["norm_ffn1"], G["norm_mix"], G["norm_ffn2"] = small[0:1], small[1:2], small[2:3]
    G["norm_final"] = small[3]
    G["swa_sinks"] = small[4:5, :SWA_Q_HEADS]
    G["rel_bias"] = small[5, :REL_BUCKETS * SWA_Q_HEADS].reshape(REL_BUCKETS, SWA_Q_HEADS)

    delta, new_m, new_v = {}, {}, {}
    small_names = ["norm_ffn1", "norm_mix", "norm_ffn2", "norm_final", "swa_sinks", "rel_bias"]

    def pack(d):
        return jnp.concatenate([jnp.pad(d[n].reshape(1, -1), ((0, 0), (0, D - d[n].size))) for n in small_names]
                               + [jnp.zeros((2, D), F32)], axis=0)

    sd, sm, sv = _adamw(pack(W), pack(G), pack(M), pack(V), "adamw_small")
    for r, n in enumerate(small_names):
        for dst, src in ((delta, sd), (new_m, sm), (new_v, sv)):
            dst[n] = src[r, :W[n].size].reshape(W[n].shape)
    for nk in _BUF:
        n = nk[0]
        shp = W[n].shape
        two_d = (shp[1], shp[2])
        d_, m_, v_ = _adamw(W[n].reshape(two_d), G[n].reshape(two_d), M[n].reshape(two_d), V[n].reshape(two_d),
                            "adamw_" + n)
        delta[n], new_m[n], new_v[n] = d_.reshape(shp), m_.reshape(shp), v_.reshape(shp)

    return (loss, dx0[None], *[G[n] for n in names], *[delta[n] for n in names],
            *[new_m[n] for n in names], *[new_v[n] for n in names])
```

```python
import functools
import math

import jax
import jax.numpy as jnp
from jax import lax
from jax.experimental import pallas as pl
from jax.experimental.pallas import tpu as pltpu

F32, BF16 = jnp.float32, jnp.bfloat16
MESH_ID = pl.DeviceIdType.MESH
ANY = pl.BlockSpec(memory_space=pl.ANY)

RMS_EPS = 1e-6
HEAD_DIM = 64
SWA_Q_HEADS, SWA_KV_HEADS, SWA_GROUP = 8, 2, 4
SWA_BLOCK = 128
SB_HEADS = 8
SB_BLOCK = 256
REL_BUCKETS, REL_MAX_DIST = 32, 128
NEG_BIG = -1e30
QK_SCALE = HEAD_DIM ** -0.5
ADAM_LR, ADAM_B1, ADAM_B2, ADAM_EPS, ADAM_WD, ADAM_STEP = 0.001, 0.9, 0.999, 1e-08, 0.01, 10

N_CHIPS = 4
TOKEN_TILE = 512
MATMUL_TOKEN_TILE = 1024
WGRAD_ROW_TILES = (2176, 1408, 1024, 256)
FF_TILE = 256
VMEM_LIMIT = 56 * 1024 * 1024


def _cp(*sem):
    return pltpu.CompilerParams(dimension_semantics=sem, vmem_limit_bytes=VMEM_LIMIT)


def _nn(a, b):
    return jnp.dot(a, b, preferred_element_type=F32)


def _nt(a, b):
    return lax.dot_general(a, b, (((1,), (1,)), ((), ())), preferred_element_type=F32)


def _tn(a, b):
    return lax.dot_general(a, b, (((0,), (0,)), ((), ())), preferred_element_type=F32)


def _norm_fwd(x, g):
    return x * lax.rsqrt(jnp.mean(x * x, axis=-1, keepdims=True) + RMS_EPS) * g


def _norm_bwd(x, g, dh):
    r = lax.rsqrt(jnp.mean(x * x, axis=-1, keepdims=True) + RMS_EPS)
    xh = x * r
    dxh = dh * g
    dx = r * (dxh - xh * jnp.mean(dxh * xh, axis=-1, keepdims=True))
    return dx, jnp.sum(dh * xh, axis=0, keepdims=True)


def _softplus(z):
    return jnp.maximum(z, 0.0) + jnp.log(1.0 + jnp.exp(-jnp.abs(z)))


def _ffn_fwd(x, g, w1t, w3t, w2, name):
    S, D = x.shape
    F = w2.shape[0]
    tm, tf = min(MATMUL_TOKEN_TILE, S), FF_TILE
    nj = F // tf

    def body(x_ref, g_ref, w1_ref, w3_ref, w2_ref, xo_ref, h_ref, a_ref, b_ref, hs, acc):
        j = pl.program_id(1)

        @pl.when(j == 0)
        def _():
            hb = _norm_fwd(x_ref[...], g_ref[...]).astype(BF16)
            hs[...] = hb
            h_ref[...] = hb
            acc[...] = jnp.zeros_like(acc)

        h = hs[...]
        a = _nt(h, w1_ref[...])
        b = _nt(h, w3_ref[...])
        a_ref[...] = a.astype(BF16)
        b_ref[...] = b.astype(BF16)
        u = a * jax.nn.sigmoid(a) * b
        acc[...] += _nn(u.astype(BF16), w2_ref[...])

        @pl.when(j == nj - 1)
        def _():
            xo_ref[...] = x_ref[...] + 0.5 * acc[...]

    return pl.pallas_call(
        body, name=name, grid=(S // tm, nj),
        in_specs=[pl.BlockSpec((tm, D), lambda i, j: (i, 0)),
                  pl.BlockSpec((1, D), lambda i, j: (0, 0)),
                  pl.BlockSpec((tf, D), lambda i, j: (j, 0)),
                  pl.BlockSpec((tf, D), lambda i, j: (j, 0)),
                  pl.BlockSpec((tf, D), lambda i, j: (j, 0))],
        out_specs=[pl.BlockSpec((tm, D), lambda i, j: (i, 0)),
                   pl.BlockSpec((tm, D), lambda i, j: (i, 0)),
                   pl.BlockSpec((tm, tf), lambda i, j: (i, j)),
                   pl.BlockSpec((tm, tf), lambda i, j: (i, j))],
        out_shape=[jax.ShapeDtypeStruct((S, D), F32), jax.ShapeDtypeStruct((S, D), BF16),
                   jax.ShapeDtypeStruct((S, F), BF16), jax.ShapeDtypeStruct((S, F), BF16)],
        scratch_shapes=[pltpu.VMEM((tm, D), BF16), pltpu.VMEM((tm, D), F32)],
        compiler_params=_cp("arbitrary", "arbitrary"),
    )(x, g, w1t, w3t, w2)


def _ffn_bwd(dxo, x, g, a, b, w1t, w3t, w2, name):
    S, D = x.shape
    F = w2.shape[0]
    tm, tf = min(MATMUL_TOKEN_TILE, S), FF_TILE
    ni, nj = S // tm, F // tf

    def body(dxo_ref, x_ref, g_ref, a_ref, b_ref, w1_ref, w3_ref, w2_ref,
             dx_ref, dg_ref, dz_ref, da_ref, db_ref, u_ref, dzs, acc):
        i, j = pl.program_id(0), pl.program_id(1)

        @pl.when(j == 0)
        def _():
            dzb = (0.5 * dxo_ref[...]).astype(BF16)
            dzs[...] = dzb
            dz_ref[...] = dzb
            acc[...] = jnp.zeros_like(acc)

        du = _nt(dzs[...], w2_ref[...])
        av = a_ref[...].astype(F32)
        bv = b_ref[...].astype(F32)
        s = jax.nn.sigmoid(av)
        silu = av * s
        db = (du * silu).astype(BF16)
        da = (du * bv * (s * (1.0 + av * (1.0 - s)))).astype(BF16)
        da_ref[...] = da
        db_ref[...] = db
        u_ref[...] = (silu * bv).astype(BF16)
        acc[...] += _nn(da, w1_ref[...]) + _nn(db, w3_ref[...])

        @pl.when(j == nj - 1)
        def _():
            dx, dg = _norm_bwd(x_ref[...], g_ref[...], acc[...])
            dx_ref[...] = dxo_ref[...] + dx

            @pl.when(i == 0)
            def _():
                dg_ref[...] = dg

            @pl.when(i > 0)
            def _():
                dg_ref[...] += dg

    row = pl.BlockSpec((tm, D), lambda i, j: (i, 0))
    wsp = pl.BlockSpec((tf, D), lambda i, j: (j, 0))
    col = pl.BlockSpec((tm, tf), lambda i, j: (i, j))
    vec = pl.BlockSpec((1, D), lambda i, j: (0, 0))
    return pl.pallas_call(
        body, name=name, grid=(ni, nj),
        in_specs=[row, row, vec, col, col, wsp, wsp, wsp],
        out_specs=[row, vec, row, col, col, col],
        out_shape=[jax.ShapeDtypeStruct((S, D), F32), jax.ShapeDtypeStruct((1, D), F32),
                   jax.ShapeDtypeStruct((S, D), BF16), jax.ShapeDtypeStruct((S, F), BF16),
                   jax.ShapeDtypeStruct((S, F), BF16), jax.ShapeDtypeStruct((S, F), BF16)],
        scratch_shapes=[pltpu.VMEM((tm, D), BF16), pltpu.VMEM((tm, D), F32)],
        compiler_params=_cp("arbitrary", "arbitrary"),
    )(dxo, x, g, a, b, w1t, w3t, w2)


def _tn_matmul(a, b, name):
    S, M = a.shape
    N = b.shape[1]
    ts = min(MATMUL_TOKEN_TILE, S)
    tmm = next(t for t in WGRAD_ROW_TILES if M % t == 0)
    ns = S // ts

    def body(a_ref, b_ref, o_ref):
        s = pl.program_id(1)
        part = _tn(a_ref[...], b_ref[...])

        @pl.when(s == 0)
        def _():
            o_ref[...] = part

        @pl.when(s > 0)
        def _():
            o_ref[...] += part

    return pl.pallas_call(
        body, name=name, grid=(M // tmm, ns),
        in_specs=[pl.BlockSpec((ts, tmm), lambda m, s: (s, m)),
                  pl.BlockSpec((ts, N), lambda m, s: (s, 0))],
        out_specs=pl.BlockSpec((tmm, N), lambda m, s: (m, 0)),
        out_shape=jax.ShapeDtypeStruct((M, N), F32),
        compiler_params=_cp("arbitrary", "arbitrary"),
    )(a, b)


def _norm_matmul_nt(x, g, wt, out_dtype, name):
    S, D = x.shape
    N = wt.shape[0]
    tm = min(MATMUL_TOKEN_TILE, S)
    tn = next(t for t in (1024, 768, 256) if N % t == 0)

    def body(x_ref, g_ref, w_ref, o_ref, h_ref, hs):
        @pl.when(pl.program_id(1) == 0)
        def _():
            hb = _norm_fwd(x_ref[...], g_ref[...]).astype(BF16)
            hs[...] = hb
            h_ref[...] = hb

        o_ref[...] = _nt(hs[...], w_ref[...]).astype(out_dtype)

    return pl.pallas_call(
        body, name=name, grid=(S // tm, N // tn),
        in_specs=[pl.BlockSpec((tm, D), lambda i, j: (i, 0)),
                  pl.BlockSpec((1, D), lambda i, j: (0, 0)),
                  pl.BlockSpec((tn, D), lambda i, j: (j, 0))],
        out_specs=[pl.BlockSpec((tm, tn), lambda i, j: (i, j)),
                   pl.BlockSpec((tm, D), lambda i, j: (i, 0))],
        out_shape=[jax.ShapeDtypeStruct((S, N), out_dtype), jax.ShapeDtypeStruct((S, D), BF16)],
        scratch_shapes=[pltpu.VMEM((tm, D), BF16)],
        compiler_params=_cp("arbitrary", "arbitrary"),
    )(x, g, wt)


def _matmul_norm_bwd(dy, w, x, g, dres, name):
    S, K = dy.shape
    D = w.shape[1]
    tm = min(MATMUL_TOKEN_TILE, S)
    tk = next(t for t in (2176, 1024, 256) if K % t == 0)
    nk = K // tk

    def body(dy_ref, w_ref, x_ref, g_ref, dres_ref, dx_ref, dg_ref, acc):
        i, k = pl.program_id(0), pl.program_id(1)
        part = _nn(dy_ref[...], w_ref[...])

        @pl.when(k == 0)
        def _():
            acc[...] = part

        @pl.when(k > 0)
        def _():
            acc[...] += part

        @pl.when(k == nk - 1)
        def _():
            dx, dg = _norm_bwd(x_ref[...], g_ref[...], acc[...])
            dx_ref[...] = dres_ref[...] + dx

            @pl.when(i == 0)
            def _():
                dg_ref[...] = dg

            @pl.when(i > 0)
            def _():
                dg_ref[...] += dg

    row = pl.BlockSpec((tm, D), lambda i, k: (i, 0))
    vec = pl.BlockSpec((1, D), lambda i, k: (0, 0))
    return pl.pallas_call(
        body, name=name, grid=(S // tm, nk),
        in_specs=[pl.BlockSpec((tm, tk), lambda i, k: (i, k)),
                  pl.BlockSpec((tk, D), lambda i, k: (k, 0)), row, vec, row],
        out_specs=[row, vec],
        out_shape=[jax.ShapeDtypeStruct((S, D), F32), jax.ShapeDtypeStruct((1, D), F32)],
        scratch_shapes=[pltpu.VMEM((tm, D), F32)],
        compiler_params=_cp("arbitrary", "arbitrary"),
    )(dy, w, x, g, dres)


def _merge_fwd(x1, gates, o_a, o_b, wat, wbt, w_out):
    S, D = x1.shape
    W = o_a.shape[1]
    tm = min(TOKEN_TILE, S)

    def body(x_ref, ga_ref, gb_ref, oa_ref, ob_ref, wa_ref, wb_ref, wo_ref,
             x2_ref, mg_ref, ba_ref, bb_ref):
        ba = _nt(oa_ref[...], wa_ref[...])
        bb = _nt(ob_ref[...], wb_ref[...])
        merged = jax.nn.sigmoid(ga_ref[...]) * ba + jax.nn.sigmoid(gb_ref[...]) * bb
        mb = merged.astype(BF16)
        mg_ref[...] = mb
        ba_ref[...] = ba.astype(BF16)
        bb_ref[...] = bb.astype(BF16)
        x2_ref[...] = x_ref[...] + _nn(mb, wo_ref[...])

    row = pl.BlockSpec((tm, D), lambda i: (i, 0))
    full = lambda r, c: pl.BlockSpec((r, c), lambda i: (0, 0))
    return pl.pallas_call(
        body, name="merge_fwd", grid=(S // tm,),
        in_specs=[row, pl.BlockSpec((tm, D), lambda i: (i, 0)), pl.BlockSpec((tm, D), lambda i: (i, 1)),
                  pl.BlockSpec((tm, W), lambda i: (i, 0)), pl.BlockSpec((tm, W), lambda i: (i, 0)),
                  full(D, W), full(D, W), full(D, D)],
        out_specs=[row, row, row, row],
        out_shape=[jax.ShapeDtypeStruct((S, D), F32)] + [jax.ShapeDtypeStruct((S, D), BF16)] * 3,
        compiler_params=_cp("arbitrary"),
    )(x1, gates, gates, o_a, o_b, wat, wbt, w_out)


def _merge_bwd(dx2, gates, ba, bb, wat, wbt, w_out):
    S, D = dx2.shape
    W = wat.shape[1]
    tm = min(TOKEN_TILE, S)

    def body(dx_ref, ga_ref, gb_ref, ba_ref, bb_ref, wa_ref, wb_ref, wo_ref,
             dxb_ref, dba_ref, dbb_ref, dgt_ref, doa_ref, dob_ref):
        dxb = dx_ref[...].astype(BF16)
        dxb_ref[...] = dxb
        dm = _nt(dxb, wo_ref[...])
        sa = jax.nn.sigmoid(ga_ref[...])
        sb = jax.nn.sigmoid(gb_ref[...])
        dba = (dm * sa).astype(BF16)
        dbb = (dm * sb).astype(BF16)
        dba_ref[...] = dba
        dbb_ref[...] = dbb
        dgt_ref[:, :D] = (dm * ba_ref[...].astype(F32) * sa * (1.0 - sa)).astype(BF16)
        dgt_ref[:, D:] = (dm * bb_ref[...].astype(F32) * sb * (1.0 - sb)).astype(BF16)
        doa_ref[...] = _nn(dba, wa_ref[...])
        dob_ref[...] = _nn(dbb, wb_ref[...])

    row = pl.BlockSpec((tm, D), lambda i: (i, 0))
    full = lambda r, c: pl.BlockSpec((r, c), lambda i: (0, 0))
    return pl.pallas_call(
        body, name="merge_bwd", grid=(S // tm,),
        in_specs=[row, pl.BlockSpec((tm, D), lambda i: (i, 0)), pl.BlockSpec((tm, D), lambda i: (i, 1)),
                  row, row, full(D, W), full(D, W), full(D, D)],
        out_specs=[row, row, row, pl.BlockSpec((tm, 2 * D), lambda i: (i, 0)),
                   pl.BlockSpec((tm, W), lambda i: (i, 0)), pl.BlockSpec((tm, W), lambda i: (i, 0))],
        out_shape=[jax.ShapeDtypeStruct((S, D), BF16)] * 3 + [jax.ShapeDtypeStruct((S, 2 * D), BF16)]
                  + [jax.ShapeDtypeStruct((S, W), F32)] * 2,
        compiler_params=_cp("arbitrary"),
    )(dx2, gates, gates, ba, bb, wat, wbt, w_out)


def _final_loss(x3, gf, target):
    S, D = x3.shape
    tm = min(TOKEN_TILE, S)

    def body(x_ref, g_ref, t_ref, loss_ref, dx_ref, dg_ref):
        i = pl.program_id(0)
        x = x_ref[...]
        g = g_ref[...]
        e = _norm_fwd(x, g) - t_ref[...]
        part = 0.5 * jnp.sum(jnp.mean(e * e, axis=-1, keepdims=True), axis=0, keepdims=True)
        dx, dg = _norm_bwd(x, g, e * (1.0 / D))
        dx_ref[...] = dx

        @pl.when(i == 0)
        def _():
            loss_ref[...] = part
            dg_ref[...] = dg

        @pl.when(i > 0)
        def _():
            loss_ref[...] += part
            dg_ref[...] += dg

    row = pl.BlockSpec((tm, D), lambda i: (i, 0))
    vec = pl.BlockSpec((1, D), lambda i: (0, 0))
    return pl.pallas_call(
        body, name="final_loss", grid=(S // tm,),
        in_specs=[row, vec, row],
        out_specs=[pl.BlockSpec((1, 1), lambda i: (0, 0)), row, vec],
        out_shape=[jax.ShapeDtypeStruct((1, 1), F32), jax.ShapeDtypeStruct((S, D), F32),
                   jax.ShapeDtypeStruct((1, D), F32)],
        compiler_params=_cp("arbitrary"),
    )(x3, gf, target)


SB_HEAD_GROUP = 4
LANES = 128


def _tri(T, kind):
    r = lax.broadcasted_iota(jnp.int32, (T, T), 0)
    c = lax.broadcasted_iota(jnp.int32, (T, T), 1)
    return {"after": r > c, "upto": r <= c, "before": r < c}[kind].astype(BF16)


def _lane(v, j):
    return jnp.broadcast_to(v[:, j:j + 1], (v.shape[0], LANES))


def _t_bf16(x):
    return x.astype(F32).T.astype(BF16)


def _wide(v, T):
    return jnp.tile(v, (1, T // LANES))


def _sb_fwd(qt, kt, vt):
    H, nb, dh, T = qt.shape
    HG = SB_HEAD_GROUP

    def body(q_ref, k_ref, v_ref, o_ref, tl_ref):
        row = lax.broadcasted_iota(jnp.int32, (T, T), 0)
        col = lax.broadcasted_iota(jnp.int32, (T, T), 1)
        tri = col < row
        after = _tri(T, "after")

        def blocks(qs, kb, carry, diag):
            hs = range(HG)
            z = [_nn(qs[hh], k_ref[hh, kb]) for hh in hs]
            res, ls, first = [None] * HG, [None] * HG, [None] * HG
            for hh in hs:
                sp = _softplus(z[hh])
                if diag:
                    sp = jnp.where(tri, sp, 0.0)
                ls[hh] = z[hh] - sp
                spb = sp.astype(BF16)
                first[hh] = _lane(spb.astype(F32), 0)
                res[hh] = _nn(spb, after)
            out = []
            for hh in hs:
                c, oacc = carry[2 * hh], carry[2 * hh + 1]
                a = jnp.exp(ls[hh] - (res[hh] + _wide(c, T)))
                if diag:
                    a = jnp.where(tri, a, 0.0)
                out.extend([c + (first[hh] + _lane(res[hh], 0)), oacc + _nt(v_ref[hh, kb], a.astype(BF16))])
            return tuple(out)

        def qblock(i, _):
            qs = [_t_bf16(q_ref[hh, i]) for hh in range(HG)]
            carry = blocks(qs, i, (jnp.zeros((T, LANES), F32), jnp.zeros((dh, T), F32)) * HG, True)

            def kstep(t, carry):
                return blocks(qs, i - 1 - t, carry, False)

            carry = lax.fori_loop(0, i, kstep, carry)
            for hh in range(HG):
                o_ref[hh, i] = carry[2 * hh + 1]
                tl_ref[hh, i] = carry[2 * hh].T[:8]
            return 0

        lax.fori_loop(0, nb, qblock, 0)

    ht = pl.BlockSpec((HG, nb, dh, T), lambda h: (h, 0, 0, 0))
    return pl.pallas_call(
        body, name="sb_fwd", grid=(H // HG,),
        in_specs=[ht, ht, ht],
        out_specs=[ht, pl.BlockSpec((HG, nb, 8, T), lambda h: (h, 0, 0, 0))],
        out_shape=[jax.ShapeDtypeStruct((H, nb, dh, T), F32), jax.ShapeDtypeStruct((H, nb, 8, T), F32)],
        compiler_params=_cp("arbitrary"),
    )(qt, kt, vt)


def _sb_bwd(qt, kt, vt, dot, tl):
    H, nb, dh, T = qt.shape
    HG = SB_HEAD_GROUP

    def body(qt_ref, k_ref, v_ref, dot_ref, tl_ref, dq_ref, dk_ref, dv_ref):
        row = lax.broadcasted_iota(jnp.int32, (T, T), 0)
        col = lax.broadcasted_iota(jnp.int32, (T, T), 1)
        tri = col < row
        upto = _tri(T, "upto")
        before = _tri(T, "before")
        dk_ref[...] = jnp.zeros_like(dk_ref)
        dv_ref[...] = jnp.zeros_like(dv_ref)

        def blocks(qs, qTs, dos, doTs, kb, carry, diag):
            hs = range(HG)
            kT = [k_ref[hh, kb] for hh in hs]
            z = [_nn(qs[hh], kT[hh]) for hh in hs]
            da = [_nn(dos[hh], v_ref[hh, kb]) for hh in hs]
            res, ls = [None] * HG, [None] * HG
            for hh in hs:
                sp = _softplus(z[hh])
                if diag:
                    sp = jnp.where(tri, sp, 0.0)
                ls[hh] = z[hh] - sp
                res[hh] = _nn(sp.astype(BF16), upto)
            g, gb, ab, resg = [None] * HG, [None] * HG, [None] * HG, [None] * HG
            for hh in hs:
                a = jnp.exp(ls[hh] + (res[hh] - _wide(carry[3 * hh], T)))
                if diag:
                    a = jnp.where(tri, a, 0.0)
                ab[hh] = a.astype(BF16)
                g[hh] = a * da[hh]
                gb[hh] = g[hh].astype(BF16)
                resg[hh] = _nn(gb[hh], before)
            out = []
            for hh in hs:
                rem, pre_g, dq = carry[3 * hh:3 * hh + 3]
                dz = g[hh] - (g[hh] + (resg[hh] + _wide(pre_g, T))) * jnp.exp(ls[hh])
                if diag:
                    dz = jnp.where(tri, dz, 0.0)
                dzb = dz.astype(BF16)
                dk_ref[hh, kb] += _nn(qTs[hh], dzb)
                dv_ref[hh, kb] += _nn(doTs[hh], ab[hh])
                out.extend([rem - _lane(res[hh], T - 1), pre_g + (_lane(resg[hh], T - 1) + _lane(gb[hh].astype(F32), T - 1)),
                            dq + _nt(kT[hh], dzb)])
            return tuple(out)

        def qblock(i, _):
            qTs = [qt_ref[hh, i] for hh in range(HG)]
            doTs = [dot_ref[hh, i] for hh in range(HG)]
            qs = [_t_bf16(v) for v in qTs]
            dos = [_t_bf16(v) for v in doTs]
            carry = []
            for hh in range(HG):
                total = jnp.broadcast_to(tl_ref[hh, i][0:1], (LANES, T)).T
                carry.extend([total, jnp.zeros((T, LANES), F32), jnp.zeros((dh, T), F32)])

            def kstep(kb, carry):
                return blocks(qs, qTs, dos, doTs, kb, carry, False)

            carry = lax.fori_loop(0, i, kstep, tuple(carry))
            carry = blocks(qs, qTs, dos, doTs, i, carry, True)
            for hh in range(HG):
                dq_ref[hh, i] = carry[3 * hh + 2]
            return 0

        lax.fori_loop(0, nb, qblock, 0)

    ht = pl.BlockSpec((HG, nb, dh, T), lambda h: (h, 0, 0, 0), pipeline_mode=pl.Buffered(1))
    return pl.pallas_call(
        body, name="sb_bwd", grid=(H // HG,),
        in_specs=[ht, ht, ht, ht, pl.BlockSpec((HG, nb, 8, T), lambda h: (h, 0, 0, 0))],
        out_specs=[ht, ht, ht],
        out_shape=[jax.ShapeDtypeStruct((H, nb, dh, T), F32)] * 3,
        compiler_params=_cp("arbitrary"),
    )(qt, kt, vt, dot, tl)


def _swa_probs(q, kp, kc, bias, sink, first):
    T = q.shape[0]
    row = lax.broadcasted_iota(jnp.int32, (T, T), 0)
    col = lax.broadcasted_iota(jnp.int32, (T, T), 1)
    lp = jnp.where(jnp.logical_and(col > row, jnp.logical_not(first)), _nn(q, kp) + bias[:, :T], NEG_BIG)
    lc = jnp.where(col <= row, _nn(q, kc) + bias[:, T:], NEG_BIG)
    m = jnp.maximum(jnp.maximum(jnp.max(lp, axis=1, keepdims=True), jnp.max(lc, axis=1, keepdims=True)), sink)
    pp = jnp.exp(lp - m)
    pc = jnp.exp(lc - m)
    ps = jnp.exp(sink - m)
    inv = 1.0 / (jnp.sum(pp, axis=1, keepdims=True) + jnp.sum(pc, axis=1, keepdims=True) + ps)
    return pp * inv, pc * inv, ps * inv


def _swa_fwd(qn, kt, vt, bias, sinks):
    Hq, nb, T, dh = qn.shape
    grp = Hq // kt.shape[0]

    def body(sink_ref, q_ref, kp_ref, kc_ref, vp_ref, vc_ref, bias_ref, o_ref):
        hk, n = pl.program_id(0), pl.program_id(1)
        kp, kc, vp, vc = kp_ref[...], kc_ref[...], vp_ref[...], vc_ref[...]
        for g in range(grp):
            pp, pc, _ = _swa_probs(q_ref[g], kp, kc, bias_ref[g], sink_ref[hk * grp + g], n == 0)
            o_ref[g] = _nt(vp, pp.astype(BF16)) + _nt(vc, pc.astype(BF16))

    prev = pl.BlockSpec((None, None, dh, T), lambda h, n: (h, jnp.maximum(n - 1, 0), 0, 0))
    cur = pl.BlockSpec((None, None, dh, T), lambda h, n: (h, n, 0, 0))
    return pl.pallas_call(
        body, name="swa_fwd", grid=(Hq // grp, nb),
        in_specs=[pl.BlockSpec(memory_space=pltpu.SMEM),
                  pl.BlockSpec((grp, None, T, dh), lambda h, n: (h, n, 0, 0)),
                  prev, cur, prev, cur,
                  pl.BlockSpec((grp, T, 2 * T), lambda h, n: (h, 0, 0))],
        out_specs=pl.BlockSpec((grp, None, dh, T), lambda h, n: (h, n, 0, 0)),
        out_shape=jax.ShapeDtypeStruct((Hq, nb, dh, T), F32),
        compiler_params=_cp("arbitrary", "arbitrary"),
    )(sinks, qn, kt, kt, vt, vt, bias)


def _swa_bwd(qn, qt, kt, vt, bias, sinks, don, dot, on):
    Hq, nb, T, dh = qn.shape
    Hkv = kt.shape[0]
    grp = Hq // Hkv

    def body(sink_ref, q_ref, qt_ref, kp_ref, kc_ref, vp_ref, vc_ref, bias_ref, do_ref, dot_ref, o_ref,
             dq_ref, dk_ref, dv_ref, dbias_ref, dsink_ref, ck, cv):
        hk, n = pl.program_id(0), pl.program_id(1)

        @pl.when(n == 0)
        def _():
            dbias_ref[...] = jnp.zeros_like(dbias_ref)
            dsink_ref[...] = jnp.zeros_like(dsink_ref)
            ck[...] = jnp.zeros_like(ck)
            cv[...] = jnp.zeros_like(cv)

        @pl.when(n < nb)
        def _():
            kp, kc, vp, vc = kp_ref[...], kc_ref[...], vp_ref[...], vc_ref[...]
            kprev = jnp.zeros((dh, T), F32)
            vprev = jnp.zeros((dh, T), F32)
            kcur = jnp.zeros((dh, T), F32)
            vcur = jnp.zeros((dh, T), F32)
            for g in range(grp):
                pp, pc, ps = _swa_probs(q_ref[g], kp, kc, bias_ref[g], sink_ref[hk * grp + g], n == 0)
                do = do_ref[g]
                dob = do.astype(BF16)
                delta = jnp.sum(do * o_ref[g], axis=1, keepdims=True)
                dlp = pp * (_nn(dob, vp) - delta)
                dlc = pc * (_nn(dob, vc) - delta)
                dbias_ref[g, :, :T] += dlp
                dbias_ref[g, :, T:] += dlc
                dsink_ref[g] += -ps * delta
                dlpb, dlcb = dlp.astype(BF16), dlc.astype(BF16)
                dq_ref[g] = _nt(kp, dlpb) + _nt(kc, dlcb)
                qT, doT = qt_ref[g], dot_ref[g]
                kprev += _nn(qT, dlpb)
                kcur += _nn(qT, dlcb)
                vprev += _nn(doT, pp.astype(BF16))
                vcur += _nn(doT, pc.astype(BF16))
            dk_ref[...] = ck[...] + kprev
            dv_ref[...] = cv[...] + vprev
            ck[...] = kcur
            cv[...] = vcur

        @pl.when(n == nb)
        def _():
            dk_ref[...] = ck[...]
            dv_ref[...] = cv[...]

    qn_spec = pl.BlockSpec((grp, None, T, dh), lambda h, n: (h, jnp.minimum(n, nb - 1), 0, 0))
    qt_spec = pl.BlockSpec((grp, None, dh, T), lambda h, n: (h, jnp.minimum(n, nb - 1), 0, 0))
    prev = pl.BlockSpec((None, None, dh, T), lambda h, n: (h, jnp.maximum(n - 1, 0), 0, 0))
    cur = pl.BlockSpec((None, None, dh, T), lambda h, n: (h, jnp.minimum(n, nb - 1), 0, 0))
    per_group = lambda a, b: pl.BlockSpec((grp, a, b), lambda h, n: (h, 0, 0))
    return pl.pallas_call(
        body, name="swa_bwd", grid=(Hkv, nb + 1),
        in_specs=[pl.BlockSpec(memory_space=pltpu.SMEM), qn_spec, qt_spec, prev, cur, prev, cur,
                  per_group(T, 2 * T), qn_spec, qt_spec, qn_spec],
        out_specs=[qt_spec, prev, prev, per_group(T, 2 * T), per_group(T, 1)],
        out_shape=[jax.ShapeDtypeStruct((Hq, nb, dh, T), F32), jax.ShapeDtypeStruct((Hkv, nb, dh, T), F32),
                   jax.ShapeDtypeStruct((Hkv, nb, dh, T), F32), jax.ShapeDtypeStruct((Hq, T, 2 * T), F32),
                   jax.ShapeDtypeStruct((Hq, T, 1), F32)],
        scratch_shapes=[pltpu.VMEM((dh, T), F32), pltpu.VMEM((dh, T), F32)],
        compiler_params=_cp("arbitrary", "arbitrary"),
    )(sinks, qn, qt, kt, kt, vt, vt, bias, don, dot, on)


def _split3(x):
    h1 = x.astype(BF16)
    r1 = x - h1.astype(F32)
    h2 = r1.astype(BF16)
    h3 = (r1 - h2.astype(F32)).astype(BF16)
    return h1, h2, h3


def _bias_expand(rel_t, onehot):
    Hq, NB = rel_t.shape
    L = onehot.shape[1]

    def body(r_ref, oh_ref, o_ref):
        h1, h2, h3 = _split3(r_ref[...])
        oh = oh_ref[...]
        o_ref[...] = _nn(h1, oh) + _nn(h2, oh) + _nn(h3, oh)

    return pl.pallas_call(
        body, name="bias_expand", grid=(1,),
        in_specs=[pl.BlockSpec((Hq, NB), lambda i: (0, 0)), pl.BlockSpec((NB, L), lambda i: (0, 0))],
        out_specs=pl.BlockSpec((Hq, L), lambda i: (0, 0)),
        out_shape=jax.ShapeDtypeStruct((Hq, L), F32),
        compiler_params=_cp("arbitrary"),
    )(rel_t, onehot)


def _bias_reduce(dbias, onehot):
    Hq, L = dbias.shape
    NB = onehot.shape[0]

    def body(d_ref, oh_ref, o_ref):
        h1, h2, h3 = _split3(d_ref[...])
        oh = oh_ref[...]
        o_ref[...] = _nt(h1, oh) + _nt(h2, oh) + _nt(h3, oh)

    return pl.pallas_call(
        body, name="bias_reduce", grid=(1,),
        in_specs=[pl.BlockSpec((Hq, L), lambda i: (0, 0)), pl.BlockSpec((NB, L), lambda i: (0, 0))],
        out_specs=pl.BlockSpec((Hq, NB), lambda i: (0, 0)),
        out_shape=jax.ShapeDtypeStruct((Hq, NB), F32),
        compiler_params=_cp("arbitrary"),
    )(dbias, onehot)


def _adamw(w, g, m, v, name):
    R, C = w.shape
    tr = 256 if R % 256 == 0 else R
    bc1 = 1.0 - ADAM_B1 ** ADAM_STEP
    bc2 = 1.0 - ADAM_B2 ** ADAM_STEP

    def body(w_ref, g_ref, m_ref, v_ref, d_ref, nm_ref, nv_ref):
        g = g_ref[...]
        m2 = ADAM_B1 * m_ref[...] + (1.0 - ADAM_B1) * g
        v2 = ADAM_B2 * v_ref[...] + (1.0 - ADAM_B2) * (g * g)
        nm_ref[...] = m2
        nv_ref[...] = v2
        d_ref[...] = -ADAM_LR * ((m2 / bc1) / (jnp.sqrt(v2 / bc2) + ADAM_EPS) + ADAM_WD * w_ref[...])

    spec = pl.BlockSpec((tr, C), lambda i: (i, 0))
    return pl.pallas_call(
        body, name=name, grid=(R // tr,),
        in_specs=[spec] * 4, out_specs=[spec] * 3,
        out_shape=[jax.ShapeDtypeStruct((R, C), F32)] * 3,
        compiler_params=_cp("arbitrary"),
    )(w, g, m, v)


def _add_halves(mine, recv, name):
    K, R, C = mine.shape
    tr = 416 if R % 416 == 0 else R

    def body(a_ref, b_ref, o_ref, ob_ref):
        s = a_ref[...] + b_ref[...]
        o_ref[...] = s
        ob_ref[...] = s.astype(BF16)

    spec = pl.BlockSpec((None, tr, C), lambda k, i: (k, i, 0))
    return pl.pallas_call(
        body, name=name, grid=(K, R // tr),
        in_specs=[spec, spec], out_specs=[spec, spec],
        out_shape=[jax.ShapeDtypeStruct((K, R, C), F32), jax.ShapeDtypeStruct((K, R, C), BF16)],
        compiler_params=_cp("arbitrary", "arbitrary"),
    )(mine, recv)


def _add_received(own, recv, name):
    R, C = own.shape
    tr = 416 if R % 416 == 0 else R

    def body(a_ref, r_ref, o_ref):
        o_ref[...] = ((a_ref[...] + r_ref[0].astype(F32)) + r_ref[1].astype(F32)) + r_ref[2].astype(F32)

    return pl.pallas_call(
        body, name=name, grid=(R // tr,),
        in_specs=[pl.BlockSpec((tr, C), lambda i: (i, 0)), pl.BlockSpec((3, tr, C), lambda i: (0, i, 0))],
        out_specs=pl.BlockSpec((tr, C), lambda i: (i, 0)),
        out_shape=jax.ShapeDtypeStruct((R, C), F32),
        compiler_params=_cp("arbitrary"),
    )(own, recv)


def _position():
    x, y, c = lax.axis_index("x"), lax.axis_index("y"), lax.axis_index("c")
    others = [(1 - x, y), (x, 1 - y), (1 - x, 1 - y)]
    return x, y, c, others


def _remote(src, dst, send_sems, recv_sems, k, dev):
    return pltpu.make_async_remote_copy(src_ref=src, dst_ref=dst, send_sem=send_sems.at[k],
                                        recv_sem=recv_sems.at[k], device_id=dev, device_id_type=MESH_ID)


def _gather_weights(shard):
    R, C = shard.shape
    half = R // 2

    def body(src, out, send_sems, recv_sems):
        x, y, c, others = _position()
        mine = 2 * x + y
        sends = [_remote(src.at[c], out.at[mine, c], send_sems, recv_sems, j, (ox, oy, c))
                 for j, (ox, oy) in enumerate(others)]
        for cp in sends:
            cp.start()
        passed = []
        for j, (ox, oy) in enumerate(others):
            slot = out.at[2 * ox + oy, c]
            _remote(slot, slot, send_sems, recv_sems, j, (ox, oy, c)).wait_recv()
            fwd = _remote(slot, slot, send_sems, recv_sems, 3 + j, (x, y, 1 - c))
            fwd.start()
            passed.append(fwd)
        for j, (ox, oy) in enumerate(others):
            slot = out.at[2 * ox + oy, 1 - c]
            _remote(slot, slot, send_sems, recv_sems, 3 + j, (x, y, 1 - c)).wait_recv()
        for cp in sends + passed:
            cp.wait_send()

    return pl.pallas_call(
        body, name="gather_weights",
        in_specs=[ANY], out_specs=ANY,
        out_shape=jax.ShapeDtypeStruct((N_CHIPS, 2, half, C), shard.dtype),
        scratch_shapes=[pltpu.SemaphoreType.DMA((6,)), pltpu.SemaphoreType.DMA((6,))],
    )(shard.reshape(2, half, C)).reshape(N_CHIPS, R, C)


def _swap_halves(grads):
    K, R, C = grads.shape
    half = R // 2

    def body(src, out, send_sems, recv_sems):
        x, y, c, _ = _position()
        theirs = src.at[:, pl.ds(pl.multiple_of((1 - c) * half, 8), half), :]
        cp = _remote(theirs, out, send_sems, recv_sems, 0, (x, y, 1 - c))
        cp.start()
        cp.wait()

    return pl.pallas_call(
        body, name="swap_halves",
        in_specs=[ANY], out_specs=ANY,
        out_shape=jax.ShapeDtypeStruct((K, half, C), grads.dtype),
        scratch_shapes=[pltpu.SemaphoreType.DMA((1,)), pltpu.SemaphoreType.DMA((1,))],
    )(grads)


def _scatter_to_owners(parts):
    K, H, C = parts.shape

    def body(src, out, send_sems, recv_sems):
        x, y, c, others = _position()
        sends = [_remote(src.at[2 * ox + oy], out.at[j], send_sems, recv_sems, j, (ox, oy, c))
                 for j, (ox, oy) in enumerate(others)]
        for cp in sends:
            cp.start()
        for cp in sends:
            cp.wait()

    return pl.pallas_call(
        body, name="scatter_to_owners",
        in_specs=[ANY], out_specs=ANY,
        out_shape=jax.ShapeDtypeStruct((3, H, C), parts.dtype),
        scratch_shapes=[pltpu.SemaphoreType.DMA((3,)), pltpu.SemaphoreType.DMA((3,))],
    )(parts)


def _swap_reduced(half_rows):
    H, C = half_rows.shape

    def body(src, out, send_sems, recv_sems):
        x, y, c, _ = _position()
        cp = _remote(src, out, send_sems, recv_sems, 0, (x, y, 1 - c))
        cp.start()
        cp.wait()

    return pl.pallas_call(
        body, name="swap_reduced",
        in_specs=[ANY], out_specs=ANY,
        out_shape=jax.ShapeDtypeStruct((H, C), half_rows.dtype),
        scratch_shapes=[pltpu.SemaphoreType.DMA((1,)), pltpu.SemaphoreType.DMA((1,))],
    )(half_rows)


def _allreduce_small(block):
    R, C = block.shape
    n_dev = 8

    def body(src, out, slots, send_sems, recv_sems):
        x, y, c, _ = _position()
        me = 4 * x + 2 * y + c
        slots[me] = src[...]
        sends = []
        for r in range(1, n_dev):
            peer = (x ^ (r >> 2), y ^ ((r >> 1) & 1), c ^ (r & 1))
            cp = _remote(src, slots.at[me], send_sems, recv_sems, r - 1, peer)
            cp.start()
            sends.append(cp)
        for r in range(1, n_dev):
            theirs = slots.at[me ^ r]
            _remote(theirs, theirs, send_sems, recv_sems, r - 1, (x, y, c)).wait_recv()
        for cp in sends:
            cp.wait_send()
        acc = slots[0]
        for d in range(1, n_dev):
            acc = acc + slots[d]
        out[...] = acc

    return pl.pallas_call(
        body, name="allreduce_small",
        in_specs=[pl.BlockSpec(memory_space=pltpu.VMEM)], out_specs=pl.BlockSpec(memory_space=pltpu.VMEM),
        out_shape=jax.ShapeDtypeStruct((R, C), F32),
        scratch_shapes=[pltpu.VMEM((n_dev, R, C), F32), pltpu.SemaphoreType.DMA((7,)), pltpu.SemaphoreType.DMA((7,))],
    )(block)


def _heads_n(a, T, scale=None):
    S, W = a.shape
    a = a.reshape(S // T, T, W // HEAD_DIM, HEAD_DIM).transpose(2, 0, 1, 3)
    if scale is not None:
        a = a * scale
    return a.astype(BF16)


def _heads_t(a, T, scale=None):
    S, W = a.shape
    a = a.reshape(S // T, T, W // HEAD_DIM, HEAD_DIM).transpose(2, 0, 3, 1)
    if scale is not None:
        a = a * scale
    return a.astype(BF16)


def _heads_n_f32(a, T):
    S, W = a.shape
    return a.reshape(S // T, T, W // HEAD_DIM, HEAD_DIM).transpose(2, 0, 1, 3)


def _from_heads_t(a):
    H, nb, dh, T = a.shape
    return a.transpose(1, 3, 0, 2).reshape(nb * T, H * dh)


def _rel_bucket(dist):
    max_exact = REL_BUCKETS // 2
    d = jnp.maximum(dist, 1).astype(F32)
    large = max_exact + (jnp.log(d / max_exact) / math.log(REL_MAX_DIST / max_exact)
                         * (REL_BUCKETS - max_exact)).astype(jnp.int32)
    large = jnp.minimum(large, REL_BUCKETS - 1)
    return jnp.where(dist < max_exact, dist, large)


def _bucket_onehot():
    T = SWA_BLOCK
    dist = (jnp.arange(T)[:, None] + T) - jnp.arange(2 * T)[None, :]
    bucket = _rel_bucket(jnp.maximum(dist, 0)).reshape(1, T * 2 * T)
    return (bucket == jnp.arange(REL_BUCKETS)[:, None]).astype(BF16)


_BUF = (("ffn1_w1", "t"), ("ffn1_w3", "t"), ("ffn1_w2", "n"), ("ffn2_w1", "t"), ("ffn2_w3", "t"),
        ("ffn2_w2", "n"), ("w_in", "t"), ("w_out", "n"), ("w_branch_swa", "tw"), ("w_branch_sb", "tw"))


def _to_rows(name_kind, w, D):
    kind = name_kind[1]
    if kind == "n":
        return w
    if kind == "t":
        return w.T
    return w.T.reshape(-1, D)


def _from_rows(name_kind, rows, width):
    kind = name_kind[1]
    if kind == "n":
        return rows
    if kind == "t":
        return rows.T
    return rows.reshape(-1, width).T


def kernel(x, norm_ffn1, ffn1_w1, ffn1_w3, ffn1_w2, norm_mix, w_in, swa_sinks, rel_bias, w_branch_swa, w_branch_sb, w_out, norm_ffn2, ffn2_w1, ffn2_w3, ffn2_w2, norm_final, loss_target, m_norm_ffn1, m_ffn1_w1, m_ffn1_w3, m_ffn1_w2, m_norm_mix, m_w_in, m_swa_sinks, m_rel_bias, m_w_branch_swa, m_w_branch_sb, m_w_out, m_norm_ffn2, m_ffn2_w1, m_ffn2_w3, m_ffn2_w2, m_norm_final, v_norm_ffn1, v_ffn1_w1, v_ffn1_w3, v_ffn1_w2, v_norm_mix, v_w_in, v_swa_sinks, v_rel_bias, v_w_branch_swa, v_w_branch_sb, v_w_out, v_norm_ffn2, v_ffn2_w1, v_ffn2_w3, v_ffn2_w2, v_norm_final):
    names = ["norm_ffn1", "ffn1_w1", "ffn1_w3", "ffn1_w2", "norm_mix", "w_in", "swa_sinks", "rel_bias",
             "w_branch_swa", "w_branch_sb", "w_out", "norm_ffn2", "ffn2_w1", "ffn2_w3", "ffn2_w2", "norm_final"]
    W = dict(zip(names, [norm_ffn1, ffn1_w1, ffn1_w3, ffn1_w2, norm_mix, w_in, swa_sinks, rel_bias,
                         w_branch_swa, w_branch_sb, w_out, norm_ffn2, ffn2_w1, ffn2_w3, ffn2_w2, norm_final]))
    M = dict(zip(names, [m_norm_ffn1, m_ffn1_w1, m_ffn1_w3, m_ffn1_w2, m_norm_mix, m_w_in, m_swa_sinks, m_rel_bias,
                         m_w_branch_swa, m_w_branch_sb, m_w_out, m_norm_ffn2, m_ffn2_w1, m_ffn2_w3, m_ffn2_w2,
                         m_norm_final]))
    V = dict(zip(names, [v_norm_ffn1, v_ffn1_w1, v_ffn1_w3, v_ffn1_w2, v_norm_mix, v_w_in, v_swa_sinks, v_rel_bias,
                         v_w_branch_swa, v_w_branch_sb, v_w_out, v_norm_ffn2, v_ffn2_w1, v_ffn2_w3, v_ffn2_w2,
                         v_norm_final]))
    xs = x[0]
    target = loss_target[0]
    S, D = xs.shape
    QW = SWA_Q_HEADS * HEAD_DIM
    KW = SWA_KV_HEADS * HEAD_DIM
    BW = SB_HEADS * HEAD_DIM
    QKV = QW + 2 * KW + 3 * BW

    pieces = [_to_rows(nk, W[nk[0]][0], D) for nk in _BUF]
    sizes = [p.shape[0] for p in pieces]
    offs = [0]
    for s in sizes:
        offs.append(offs[-1] + s)
    shard = jnp.concatenate(pieces, axis=0).astype(BF16)
    chip = 2 * lax.axis_index("x") + lax.axis_index("y")
    gathered = lax.dynamic_update_slice(_gather_weights(shard), shard[None], (chip, 0, 0))

    def full(i):
        return gathered[:, offs[i]:offs[i + 1], :].reshape(N_CHIPS * sizes[i], D)

    f1w1, f1w3, f1w2, f2w1, f2w3, f2w2, w_in_t, w_out_f = [full(i) for i in range(8)]
    wa_t = full(8).reshape(D, QW)
    wb_t = full(9).reshape(D, BW)

    g1, gmix, g3 = W["norm_ffn1"], W["norm_mix"], W["norm_ffn2"]
    gf = W["norm_final"].reshape(1, D)

    x1, h1, a1, b1 = _ffn_fwd(xs, g1, f1w1, f1w3, f1w2, "ffn1_fwd")
    qkv, h2 = _norm_matmul_nt(x1, gmix, w_in_t[:QKV], BF16, "proj_qkv")
    gates, _ = _norm_matmul_nt(x1, gmix, w_in_t[QKV:], F32, "proj_gates")

    q_a, k_a, v_a = qkv[:, :QW], qkv[:, QW:QW + KW], qkv[:, QW + KW:QW + 2 * KW]
    o0 = QW + 2 * KW
    q_b, k_b, v_b = qkv[:, o0:o0 + BW], qkv[:, o0 + BW:o0 + 2 * BW], qkv[:, o0 + 2 * BW:o0 + 3 * BW]

    onehot = _bucket_onehot()
    bias = _bias_expand(W["rel_bias"].T, onehot).reshape(SWA_Q_HEADS, SWA_BLOCK, 2 * SWA_BLOCK)
    sinks = W["swa_sinks"].reshape(SWA_Q_HEADS)
    qa_n, qa_t = _heads_n(q_a, SWA_BLOCK, QK_SCALE), _heads_t(q_a, SWA_BLOCK, QK_SCALE)
    ka_t, va_t = _heads_t(k_a, SWA_BLOCK), _heads_t(v_a, SWA_BLOCK)
    oa_t = _swa_fwd(qa_n, ka_t, va_t, bias, sinks)

    qb_t = _heads_t(q_b, SB_BLOCK, QK_SCALE)
    kb_t, vb_t = _heads_t(k_b, SB_BLOCK), _heads_t(v_b, SB_BLOCK)
    ob_t, totals = _sb_fwd(qb_t, kb_t, vb_t)

    o_a = _from_heads_t(oa_t)
    o_b = _from_heads_t(ob_t)
    o_a16, o_b16 = o_a.astype(BF16), o_b.astype(BF16)
    x2, merged, ba, bb = _merge_fwd(x1, gates, o_a16, o_b16, wa_t, wb_t, w_out_f)
    x3, h3, a2, b2 = _ffn_fwd(x2, g3, f2w1, f2w3, f2w2, "ffn2_fwd")
    loss_part, dx3, dgf = _final_loss(x3, gf, target)

    dx2, dg3, dz2, da2, db2, u2 = _ffn_bwd(dx3, x2, g3, a2, b2, f2w1, f2w3, f2w2, "ffn2_bwd")
    grads = {}
    grads["ffn2_w1"] = _tn_matmul(da2, h3, "ffn2_dw1")
    grads["ffn2_w3"] = _tn_matmul(db2, h3, "ffn2_dw3")
    grads["ffn2_w2"] = _tn_matmul(u2, dz2, "ffn2_dw2")

    dx2b, dba, dbb, dgates, do_a, do_b = _merge_bwd(dx2, gates, ba, bb, wa_t, wb_t, w_out_f)
    grads["w_out"] = _tn_matmul(merged, dx2b, "dw_out")
    grads["w_branch_swa"] = _tn_matmul(dba, o_a16, "dw_branch_swa")
    grads["w_branch_sb"] = _tn_matmul(dbb, o_b16, "dw_branch_sb")

    dqb_t, dkb_t, dvb_t = _sb_bwd(qb_t, kb_t, vb_t, _heads_t(do_b, SB_BLOCK), totals)
    dqa_t, dka_t, dva_t, dbias, dsink_rows = _swa_bwd(
        qa_n, qa_t, ka_t, va_t, bias, sinks, _heads_n_f32(do_a, SWA_BLOCK), _heads_t(do_a, SWA_BLOCK),
        _heads_n_f32(o_a, SWA_BLOCK))
    d_rel = _bias_reduce(dbias.reshape(SWA_Q_HEADS, -1), onehot).T
    d_sinks = jnp.sum(dsink_rows, axis=(1, 2))

    dqkv = jnp.concatenate([_from_heads_t(dqa_t) * QK_SCALE, _from_heads_t(dka_t), _from_heads_t(dva_t),
                            _from_heads_t(dqb_t) * QK_SCALE, _from_heads_t(dkb_t), _from_heads_t(dvb_t)],
                           axis=1).astype(BF16)
    dproj = jnp.concatenate([dqkv, dgates], axis=1)
    grads["w_in"] = _tn_matmul(dproj, h2, "dw_in")
    dx1, dgmix = _matmul_norm_bwd(dproj, w_in_t, x1, gmix, dx2, "proj_bwd")

    dx0, dg1, dz1, da1, db1, u1 = _ffn_bwd(dx1, xs, g1, a1, b1, f1w1, f1w3, f1w2, "ffn1_bwd")
    grads["ffn1_w1"] = _tn_matmul(da1, h1, "ffn1_dw1")
    grads["ffn1_w3"] = _tn_matmul(db1, h1, "ffn1_dw3")
    grads["ffn1_w2"] = _tn_matmul(u1, dz1, "ffn1_dw2")

    gparts = [grads[nk[0]].reshape(N_CHIPS, sizes[i], D) for i, nk in enumerate(_BUF)]
    gbuf = jnp.concatenate(gparts, axis=1)
    R = gbuf.shape[1]
    half = R // 2
    c = lax.axis_index("c")
    mine = 2 * lax.axis_index("x") + lax.axis_index("y")
    from_sibling = _swap_halves(gbuf)
    my_half = lax.dynamic_slice_in_dim(gbuf, c * half, half, axis=1)
    chip_sum, chip_sum16 = _add_halves(my_half, from_sibling, "add_sibling")
    received = _scatter_to_owners(chip_sum16)
    own = lax.dynamic_index_in_dim(chip_sum, mine, axis=0, keepdims=False)
    my_rows = _add_received(own, received, "add_chips")
    their_rows = _swap_reduced(my_rows)
    reduced = jnp.concatenate([jnp.where(c == 0, my_rows, their_rows), jnp.where(c == 0, their_rows, my_rows)], axis=0)

    small_rows = [dg1, dgmix, dg3, dgf,
                  jnp.pad(d_sinks.reshape(1, -1), ((0, 0), (0, D - SWA_Q_HEADS))),
                  jnp.pad(d_rel.reshape(1, -1), ((0, 0), (0, D - REL_BUCKETS * SWA_Q_HEADS))),
                  jnp.pad(loss_part, ((0, 0), (0, D - 1))), jnp.zeros((1, D), F32)]
    small = _allreduce_small(jnp.concatenate(small_rows, axis=0))
    loss = small[6, 0]

    G = {}
    for i, nk in enumerate(_BUF):
        G[nk[0]] = _from_rows(nk, reduced[offs[i]:offs[i + 1]], W[nk[0]].shape[1])[None]
    G["norm_ffn1"], G["norm_mix"], G["norm_ffn2"] = small[0:1], small[1:2], small[2:3]
    G["norm_final"] = small[3]
    G["swa_sinks"] = small[4:5, :SWA_Q_HEADS]
    G["rel_bias"] = small[5, :REL_BUCKETS * SWA_Q_HEADS].reshape(REL_BUCKETS, SWA_Q_HEADS)

    delta, new_m, new_v = {}, {}, {}
    small_names = ["norm_ffn1", "norm_mix", "norm_ffn2", "norm_final", "swa_sinks", "rel_bias"]

    def pack(d):
        return jnp.concatenate([jnp.pad(d[n].reshape(1, -1), ((0, 0), (0, D - d[n].size))) for n in small_names]
                               + [jnp.zeros((2, D), F32)], axis=0)

    sd, sm, sv = _adamw(pack(W), pack(G), pack(M), pack(V), "adamw_small")
    for r, n in enumerate(small_names):
        for dst, src in ((delta, sd), (new_m, sm), (new_v, sv)):
            dst[n] = src[r, :W[n].size].reshape(W[n].shape)
    for nk in _BUF:
        n = nk[0]
        shp = W[n].shape
        two_d = (shp[1], shp[2])
        d_, m_, v_ = _adamw(W[n].reshape(two_d), G[n].reshape(two_d), M[n].reshape(two_d), V[n].reshape(two_d),
                            "adamw_" + n)
        delta[n], new_m[n], new_v[n] = d_.reshape(shp), m_.reshape(shp), v_.reshape(shp)

    return (loss, dx0[None], *[G[n] for n in names], *[delta[n] for n in names],
            *[new_m[n] for n in names], *[new_v[n] for n in names])
```

```python
import functools
import math

import jax
import jax.numpy as jnp
from jax import lax
from jax.experimental import pallas as pl
from jax.experimental.pallas import tpu as pltpu

F32, BF16 = jnp.float32, jnp.bfloat16
MESH_ID = pl.DeviceIdType.MESH
ANY = pl.BlockSpec(memory_space=pl.ANY)

RMS_EPS = 1e-6
HEAD_DIM = 64
SWA_Q_HEADS, SWA_KV_HEADS, SWA_GROUP = 8, 2, 4
SWA_BLOCK = 128
SB_HEADS = 8
SB_BLOCK = 256
REL_BUCKETS, REL_MAX_DIST = 32, 128
NEG_BIG = -1e30
QK_SCALE = HEAD_DIM ** -0.5
ADAM_LR, ADAM_B1, ADAM_B2, ADAM_EPS, ADAM_WD, ADAM_STEP = 0.001, 0.9, 0.999, 1e-08, 0.01, 10

N_CHIPS = 4
TOKEN_TILE = 512
MATMUL_TOKEN_TILE = 1024
WGRAD_ROW_TILES = (2176, 1408, 1024, 256)
FF_TILE = 256
VMEM_LIMIT = 56 * 1024 * 1024


def _cp(*sem):
    return pltpu.CompilerParams(dimension_semantics=sem, vmem_limit_bytes=VMEM_LIMIT)


def _nn(a, b):
    return jnp.dot(a, b, preferred_element_type=F32)


def _nt(a, b):
    return lax.dot_general(a, b, (((1,), (1,)), ((), ())), preferred_element_type=F32)


def _tn(a, b):
    return lax.dot_general(a, b, (((0,), (0,)), ((), ())), preferred_element_type=F32)


def _norm_fwd(x, g):
    return x * lax.rsqrt(jnp.mean(x * x, axis=-1, keepdims=True) + RMS_EPS) * g


def _norm_bwd(x, g, dh):
    r = lax.rsqrt(jnp.mean(x * x, axis=-1, keepdims=True) + RMS_EPS)
    xh = x * r
    dxh = dh * g
    dx = r * (dxh - xh * jnp.mean(dxh * xh, axis=-1, keepdims=True))
    return dx, jnp.sum(dh * xh, axis=0, keepdims=True)


def _softplus(z):
    return jnp.maximum(z, 0.0) + jnp.log(1.0 + jnp.exp(-jnp.abs(z)))


def _ffn_fwd(x, g, w1t, w3t, w2, name):
    S, D = x.shape
    F = w2.shape[0]
    tm, tf = min(MATMUL_TOKEN_TILE, S), FF_TILE
    nj = F // tf

    def body(x_ref, g_ref, w1_ref, w3_ref, w2_ref, xo_ref, h_ref, a_ref, b_ref, hs, acc):
        j = pl.program_id(1)

        @pl.when(j == 0)
        def _():
            hb = _norm_fwd(x_ref[...], g_ref[...]).astype(BF16)
            hs[...] = hb
            h_ref[...] = hb
            acc[...] = jnp.zeros_like(acc)

        h = hs[...]
        a = _nt(h, w1_ref[...])
        b = _nt(h, w3_ref[...])
        a_ref[...] = a.astype(BF16)
        b_ref[...] = b.astype(BF16)
        u = a * jax.nn.sigmoid(a) * b
        acc[...] += _nn(u.astype(BF16), w2_ref[...])

        @pl.when(j == nj - 1)
        def _():
            xo_ref[...] = x_ref[...] + 0.5 * acc[...]

    return pl.pallas_call(
        body, name=name, grid=(S // tm, nj),
        in_specs=[pl.BlockSpec((tm, D), lambda i, j: (i, 0)),
                  pl.BlockSpec((1, D), lambda i, j: (0, 0)),
                  pl.BlockSpec((tf, D), lambda i, j: (j, 0)),
                  pl.BlockSpec((tf, D), lambda i, j: (j, 0)),
                  pl.BlockSpec((tf, D), lambda i, j: (j, 0))],
        out_specs=[pl.BlockSpec((tm, D), lambda i, j: (i, 0)),
                   pl.BlockSpec((tm, D), lambda i, j: (i, 0)),
                   pl.BlockSpec((tm, tf), lambda i, j: (i, j)),
                   pl.BlockSpec((tm, tf), lambda i, j: (i, j))],
        out_shape=[jax.ShapeDtypeStruct((S, D), F32), jax.ShapeDtypeStruct((S, D), BF16),
                   jax.ShapeDtypeStruct((S, F), BF16), jax.ShapeDtypeStruct((S, F), BF16)],
        scratch_shapes=[pltpu.VMEM((tm, D), BF16), pltpu.VMEM((tm, D), F32)],
        compiler_params=_cp("arbitrary", "arbitrary"),
    )(x, g, w1t, w3t, w2)


def _ffn_bwd(dxo, x, g, a, b, w1t, w3t, w2, name):
    S, D = x.shape
    F = w2.shape[0]
    tm, tf = min(MATMUL_TOKEN_TILE, S), FF_TILE
    ni, nj = S // tm, F // tf

    def body(dxo_ref, x_ref, g_ref, a_ref, b_ref, w1_ref, w3_ref, w2_ref,
             dx_ref, dg_ref, dz_ref, da_ref, db_ref, u_ref, dzs, acc):
        i, j = pl.program_id(0), pl.program_id(1)

        @pl.when(j == 0)
        def _():
            dzb = (0.5 * dxo_ref[...]).astype(BF16)
            dzs[...] = dzb
            dz_ref[...] = dzb
            acc[...] = jnp.zeros_like(acc)

        du = _nt(dzs[...], w2_ref[...])
        av = a_ref[...].astype(F32)
        bv = b_ref[...].astype(F32)
        s = jax.nn.sigmoid(av)
        silu = av * s
        db = (du * silu).astype(BF16)
        da = (du * bv * (s * (1.0 + av * (1.0 - s)))).astype(BF16)
        da_ref[...] = da
        db_ref[...] = db
        u_ref[...] = (silu * bv).astype(BF16)
        acc[...] += _nn(da, w1_ref[...]) + _nn(db, w3_ref[...])

        @pl.when(j == nj - 1)
        def _():
            dx, dg = _norm_bwd(x_ref[...], g_ref[...], acc[...])
            dx_ref[...] = dxo_ref[...] + dx

            @pl.when(i == 0)
            def _():
                dg_ref[...] = dg

            @pl.when(i > 0)
            def _():
                dg_ref[...] += dg

    row = pl.BlockSpec((tm, D), lambda i, j: (i, 0))
    wsp = pl.BlockSpec((tf, D), lambda i, j: (j, 0))
    col = pl.BlockSpec((tm, tf), lambda i, j: (i, j))
    vec = pl.BlockSpec((1, D), lambda i, j: (0, 0))
    return pl.pallas_call(
        body, name=name, grid=(ni, nj),
        in_specs=[row, row, vec, col, col, wsp, wsp, wsp],
        out_specs=[row, vec, row, col, col, col],
        out_shape=[jax.ShapeDtypeStruct((S, D), F32), jax.ShapeDtypeStruct((1, D), F32),
                   jax.ShapeDtypeStruct((S, D), BF16), jax.ShapeDtypeStruct((S, F), BF16),
                   jax.ShapeDtypeStruct((S, F), BF16), jax.ShapeDtypeStruct((S, F), BF16)],
        scratch_shapes=[pltpu.VMEM((tm, D), BF16), pltpu.VMEM((tm, D), F32)],
        compiler_params=_cp("arbitrary", "arbitrary"),
    )(dxo, x, g, a, b, w1t, w3t, w2)


def _tn_matmul(a, b, name):
    S, M = a.shape
    N = b.shape[1]
    ts = min(MATMUL_TOKEN_TILE, S)
    tmm = next(t for t in WGRAD_ROW_TILES if M % t == 0)
    ns = S // ts

    def body(a_ref, b_ref, o_ref):
        s = pl.program_id(1)
        part = _tn(a_ref[...], b_ref[...])

        @pl.when(s == 0)
        def _():
            o_ref[...] = part

        @pl.when(s > 0)
        def _():
            o_ref[...] += part

    return pl.pallas_call(
        body, name=name, grid=(M // tmm, ns),
        in_specs=[pl.BlockSpec((ts, tmm), lambda m, s: (s, m)),
                  pl.BlockSpec((ts, N), lambda m, s: (s, 0))],
        out_specs=pl.BlockSpec((tmm, N), lambda m, s: (m, 0)),
        out_shape=jax.ShapeDtypeStruct((M, N), F32),
        compiler_params=_cp("arbitrary", "arbitrary"),
    )(a, b)


def _norm_matmul_nt(x, g, wt, out_dtype, name):
    S, D = x.shape
    N = wt.shape[0]
    tm = min(MATMUL_TOKEN_TILE, S)
    tn = next(t for t in (1024, 768, 256) if N % t == 0)

    def body(x_ref, g_ref, w_ref, o_ref, h_ref, hs):
        @pl.when(pl.program_id(1) == 0)
        def _():
            hb = _norm_fwd(x_ref[...], g_ref[...]).astype(BF16)
            hs[...] = hb
            h_ref[...] = hb

        o_ref[...] = _nt(hs[...], w_ref[...]).astype(out_dtype)

    return pl.pallas_call(
        body, name=name, grid=(S // tm, N // tn),
        in_specs=[pl.BlockSpec((tm, D), lambda i, j: (i, 0)),
                  pl.BlockSpec((1, D), lambda i, j: (0, 0)),
                  pl.BlockSpec((tn, D), lambda i, j: (j, 0))],
        out_specs=[pl.BlockSpec((tm, tn), lambda i, j: (i, j)),
                   pl.BlockSpec((tm, D), lambda i, j: (i, 0))],
        out_shape=[jax.ShapeDtypeStruct((S, N), out_dtype), jax.ShapeDtypeStruct((S, D), BF16)],
        scratch_shapes=[pltpu.VMEM((tm, D), BF16)],
        compiler_params=_cp("arbitrary", "arbitrary"),
    )(x, g, wt)


def _heads_tile(ref):
    Hh, nbk = ref.shape[0], ref.shape[1]
    return jnp.concatenate([jnp.concatenate([ref[h, b] for b in range(nbk)], axis=1) for h in range(Hh)], axis=0)


def _store_heads(ref, val):
    Hh, nbk, dh, T = ref.shape
    for h in range(Hh):
        for b in range(nbk):
            ref[h, b] = val[h * dh:(h + 1) * dh, b * T:(b + 1) * T].astype(ref.dtype)


def _norm_proj_heads(x, g, w_rows, T, name):
    S, D = x.shape
    N = w_rows.shape[0]
    tm, tn = min(MATMUL_TOKEN_TILE, S), 768

    def body(x_ref, g_ref, w_ref, o_ref, hs):
        @pl.when(pl.program_id(1) == 0)
        def _():
            hs[...] = _norm_fwd(x_ref[...], g_ref[...]).astype(BF16)

        _store_heads(o_ref, _nt(w_ref[...], hs[...]))

    return pl.pallas_call(
        body, name=name, grid=(S // tm, N // tn),
        in_specs=[pl.BlockSpec((tm, D), lambda i, j: (i, 0)),
                  pl.BlockSpec((1, D), lambda i, j: (0, 0)),
                  pl.BlockSpec((tn, D), lambda i, j: (j, 0))],
        out_specs=pl.BlockSpec((tn // HEAD_DIM, tm // T, HEAD_DIM, T), lambda i, j: (j, i, 0, 0)),
        out_shape=jax.ShapeDtypeStruct((N // HEAD_DIM, S // T, HEAD_DIM, T), BF16),
        scratch_shapes=[pltpu.VMEM((tm, D), BF16)],
        compiler_params=_cp("arbitrary", "arbitrary"),
    )(x, g, w_rows)


def _heads_matmul(at, b, scale, name):
    Hh, nb, dh, T = at.shape
    S, N = b.shape
    ts = min(MATMUL_TOKEN_TILE, S)
    ns = S // ts

    def body(a_ref, b_ref, o_ref):
        s = pl.program_id(0)
        a = _heads_tile(a_ref)
        part = _nn((a if scale == 1.0 else a * scale).astype(BF16), b_ref[...])

        @pl.when(s == 0)
        def _():
            o_ref[...] = part

        @pl.when(s > 0)
        def _():
            o_ref[...] += part

    return pl.pallas_call(
        body, name=name, grid=(ns,),
        in_specs=[pl.BlockSpec((Hh, ts // T, dh, T), lambda s: (0, s, 0, 0)),
                  pl.BlockSpec((ts, N), lambda s: (s, 0))],
        out_specs=pl.BlockSpec((Hh * dh, N), lambda s: (0, 0)),
        out_shape=jax.ShapeDtypeStruct((Hh * dh, N), F32),
        compiler_params=_cp("arbitrary"),
    )(at, b)


def _proj_bwd(pieces, dgates, w_rows, x, g, dres):
    S, D = x.shape
    tm = min(TOKEN_TILE, S)
    n_p = len(pieces)
    gate_row = w_rows.shape[0] - dgates.shape[1]

    def body(*refs):
        p_refs = refs[:n_p]
        dgt_ref, w_ref, x_ref, g_ref, dres_ref, dx_ref, dg_ref = refs[n_p:]
        i = pl.program_id(0)
        dh = _nn(dgt_ref[...], w_ref[gate_row:, :])
        for p_ref, (arr, row0) in zip(p_refs, pieces):
            rows = arr.shape[0] * arr.shape[2]
            dh += _tn(_heads_tile(p_ref).astype(BF16), w_ref[row0:row0 + rows, :])
        dx, dg = _norm_bwd(x_ref[...], g_ref[...], dh)
        dx_ref[...] = dres_ref[...] + dx

        @pl.when(i == 0)
        def _():
            dg_ref[...] = dg

        @pl.when(i > 0)
        def _():
            dg_ref[...] += dg

    row = pl.BlockSpec((tm, D), lambda i: (i, 0))
    vec = pl.BlockSpec((1, D), lambda i: (0, 0))
    p_specs = [pl.BlockSpec((a.shape[0], tm // a.shape[3], a.shape[2], a.shape[3]), lambda i: (0, i, 0, 0))
               for a, _ in pieces]
    return pl.pallas_call(
        body, name="proj_bwd", grid=(S // tm,),
        in_specs=p_specs + [pl.BlockSpec((tm, dgates.shape[1]), lambda i: (i, 0)),
                            pl.BlockSpec(w_rows.shape, lambda i: (0, 0), pipeline_mode=pl.Buffered(1)),
                            row, vec, row],
        out_specs=[row, vec],
        out_shape=[jax.ShapeDtypeStruct((S, D), F32), jax.ShapeDtypeStruct((1, D), F32)],
        compiler_params=_cp("arbitrary"),
    )(*[a for a, _ in pieces], dgates, w_rows, x, g, dres)


def _merge_fwd(x1, gates, oa_t, ob_t, wat, wbt, w_out):
    S, D = x1.shape
    W = wat.shape[1]
    tm = min(TOKEN_TILE, S)

    def body(x_ref, ga_ref, gb_ref, oa_ref, ob_ref, wa_ref, wb_ref, wo_ref,
             x2_ref, mg_ref, ba_ref, bb_ref):
        ba = _nt(_heads_tile(oa_ref).T.astype(BF16), wa_ref[...])
        bb = _nt(_heads_tile(ob_ref).T.astype(BF16), wb_ref[...])
        merged = jax.nn.sigmoid(ga_ref[...]) * ba + jax.nn.sigmoid(gb_ref[...]) * bb
        mb = merged.astype(BF16)
        mg_ref[...] = mb
        ba_ref[...] = ba.astype(BF16)
        bb_ref[...] = bb.astype(BF16)
        x2_ref[...] = x_ref[...] + _nn(mb, wo_ref[...])

    row = pl.BlockSpec((tm, D), lambda i: (i, 0))
    full = lambda r, c: pl.BlockSpec((r, c), lambda i: (0, 0))
    heads = lambda a: pl.BlockSpec((a.shape[0], tm // a.shape[3], a.shape[2], a.shape[3]), lambda i: (0, i, 0, 0))
    return pl.pallas_call(
        body, name="merge_fwd", grid=(S // tm,),
        in_specs=[row, pl.BlockSpec((tm, D), lambda i: (i, 0)), pl.BlockSpec((tm, D), lambda i: (i, 1)),
                  heads(oa_t), heads(ob_t), full(D, W), full(D, W), full(D, D)],
        out_specs=[row, row, row, row],
        out_shape=[jax.ShapeDtypeStruct((S, D), F32)] + [jax.ShapeDtypeStruct((S, D), BF16)] * 3,
        compiler_params=_cp("arbitrary"),
    )(x1, gates, gates, oa_t, ob_t, wat, wbt, w_out)


def _merge_bwd(dx2, gates, ba, bb, wa, wb, w_out, t_a, t_b):
    S, D = dx2.shape
    W = wa.shape[0]
    tm = min(TOKEN_TILE, S)
    Hh = W // HEAD_DIM

    def body(dx_ref, ga_ref, gb_ref, ba_ref, bb_ref, wa_ref, wb_ref, wo_ref,
             dxb_ref, dba_ref, dbb_ref, dgt_ref, doa_ref, dob_ref):
        dxb = dx_ref[...].astype(BF16)
        dxb_ref[...] = dxb
        dm = _nt(dxb, wo_ref[...])
        sa = jax.nn.sigmoid(ga_ref[...])
        sb = jax.nn.sigmoid(gb_ref[...])
        dba = (dm * sa).astype(BF16)
        dbb = (dm * sb).astype(BF16)
        dba_ref[...] = dba
        dbb_ref[...] = dbb
        dgt_ref[:, :D] = (dm * ba_ref[...].astype(F32) * sa * (1.0 - sa)).astype(BF16)
        dgt_ref[:, D:] = (dm * bb_ref[...].astype(F32) * sb * (1.0 - sb)).astype(BF16)
        _store_heads(doa_ref, _nt(wa_ref[...], dba))
        _store_heads(dob_ref, _nt(wb_ref[...], dbb))

    row = pl.BlockSpec((tm, D), lambda i: (i, 0))
    full = lambda r, c: pl.BlockSpec((r, c), lambda i: (0, 0))
    heads = lambda T: pl.BlockSpec((Hh, tm // T, HEAD_DIM, T), lambda i: (0, i, 0, 0))
    return pl.pallas_call(
        body, name="merge_bwd", grid=(S // tm,),
        in_specs=[row, pl.BlockSpec((tm, D), lambda i: (i, 0)), pl.BlockSpec((tm, D), lambda i: (i, 1)),
                  row, row, full(W, D), full(W, D), full(D, D)],
        out_specs=[row, row, row, pl.BlockSpec((tm, 2 * D), lambda i: (i, 0)), heads(t_a), heads(t_b)],
        out_shape=[jax.ShapeDtypeStruct((S, D), BF16)] * 3 + [jax.ShapeDtypeStruct((S, 2 * D), BF16),
                   jax.ShapeDtypeStruct((Hh, S // t_a, HEAD_DIM, t_a), F32),
                   jax.ShapeDtypeStruct((Hh, S // t_b, HEAD_DIM, t_b), BF16)],
        compiler_params=_cp("arbitrary"),
    )(dx2, gates, gates, ba, bb, wa, wb, w_out)


def _final_loss(x3, gf, target):
    S, D = x3.shape
    tm = min(TOKEN_TILE, S)

    def body(x_ref, g_ref, t_ref, loss_ref, dx_ref, dg_ref):
        i = pl.program_id(0)
        x = x_ref[...]
        g = g_ref[...]
        e = _norm_fwd(x, g) - t_ref[...]
        part = 0.5 * jnp.sum(jnp.mean(e * e, axis=-1, keepdims=True), axis=0, keepdims=True)
        dx, dg = _norm_bwd(x, g, e * (1.0 / D))
        dx_ref[...] = dx

        @pl.when(i == 0)
        def _():
            loss_ref[...] = part
            dg_ref[...] = dg

        @pl.when(i > 0)
        def _():
            loss_ref[...] += part
            dg_ref[...] += dg

    row = pl.BlockSpec((tm, D), lambda i: (i, 0))
    vec = pl.BlockSpec((1, D), lambda i: (0, 0))
    return pl.pallas_call(
        body, name="final_loss", grid=(S // tm,),
        in_specs=[row, vec, row],
        out_specs=[pl.BlockSpec((1, 1), lambda i: (0, 0)), row, vec],
        out_shape=[jax.ShapeDtypeStruct((1, 1), F32), jax.ShapeDtypeStruct((S, D), F32),
                   jax.ShapeDtypeStruct((1, D), F32)],
        compiler_params=_cp("arbitrary"),
    )(x3, gf, target)


SB_HEAD_GROUP = 4
LANES = 128


def _tri(T, kind):
    r = lax.broadcasted_iota(jnp.int32, (T, T), 0)
    c = lax.broadcasted_iota(jnp.int32, (T, T), 1)
    return {"after": r > c, "upto": r <= c, "before": r < c}[kind].astype(BF16)


def _lane(v, j):
    return jnp.broadcast_to(v[:, j:j + 1], (v.shape[0], LANES))


def _t_bf16(x):
    return x.astype(F32).T.astype(BF16)


def _wide(v, T):
    return jnp.tile(v, (1, T // LANES))


def _sb_fwd(qkv):
    H3, nb, dh, T = qkv.shape
    H = H3 // 3
    HG = SB_HEAD_GROUP

    def body(q_ref, k_ref, v_ref, o_ref, tl_ref):
        row = lax.broadcasted_iota(jnp.int32, (T, T), 0)
        col = lax.broadcasted_iota(jnp.int32, (T, T), 1)
        tri = col < row
        after = _tri(T, "after")

        def blocks(qs, kb, carry, diag):
            hs = range(HG)
            z = [_nn(qs[hh], k_ref[hh, kb]) for hh in hs]
            res, ls, first = [None] * HG, [None] * HG, [None] * HG
            for hh in hs:
                sp = _softplus(z[hh])
                if diag:
                    sp = jnp.where(tri, sp, 0.0)
                ls[hh] = z[hh] - sp
                spb = sp.astype(BF16)
                first[hh] = _lane(spb.astype(F32), 0)
                res[hh] = _nn(spb, after)
            out = []
            for hh in hs:
                c, oacc = carry[2 * hh], carry[2 * hh + 1]
                a = jnp.exp(ls[hh] - (res[hh] + _wide(c, T)))
                if diag:
                    a = jnp.where(tri, a, 0.0)
                out.extend([c + (first[hh] + _lane(res[hh], 0)), oacc + _nt(v_ref[hh, kb], a.astype(BF16))])
            return tuple(out)

        def qblock(i, _):
            qs = [_t_bf16(q_ref[hh, i]) for hh in range(HG)]
            carry = blocks(qs, i, (jnp.zeros((T, LANES), F32), jnp.zeros((dh, T), F32)) * HG, True)

            def kstep(t, carry):
                return blocks(qs, i - 1 - t, carry, False)

            carry = lax.fori_loop(0, i, kstep, carry)
            for hh in range(HG):
                o_ref[hh, i] = carry[2 * hh + 1]
                tl_ref[hh, i] = carry[2 * hh].T[:8]
            return 0

        lax.fori_loop(0, nb, qblock, 0)

    G = H // HG
    ht = lambda part: pl.BlockSpec((HG, nb, dh, T), lambda h: (h + part * G, 0, 0, 0))
    return pl.pallas_call(
        body, name="sb_fwd", grid=(G,),
        in_specs=[ht(0), ht(1), ht(2)],
        out_specs=[ht(0), pl.BlockSpec((HG, nb, 8, T), lambda h: (h, 0, 0, 0))],
        out_shape=[jax.ShapeDtypeStruct((H, nb, dh, T), F32), jax.ShapeDtypeStruct((H, nb, 8, T), F32)],
        compiler_params=_cp("arbitrary"),
    )(qkv, qkv, qkv)


def _sb_bwd(qkv, dot, tl):
    H3, nb, dh, T = qkv.shape
    H = H3 // 3
    HG = SB_HEAD_GROUP

    def body(qt_ref, k_ref, v_ref, dot_ref, tl_ref, dq_ref, dk_ref, dv_ref):
        row = lax.broadcasted_iota(jnp.int32, (T, T), 0)
        col = lax.broadcasted_iota(jnp.int32, (T, T), 1)
        tri = col < row
        upto = _tri(T, "upto")
        before = _tri(T, "before")
        dk_ref[...] = jnp.zeros_like(dk_ref)
        dv_ref[...] = jnp.zeros_like(dv_ref)

        def blocks(qs, qTs, dos, doTs, kb, carry, diag):
            hs = range(HG)
            kT = [k_ref[hh, kb] for hh in hs]
            z = [_nn(qs[hh], kT[hh]) for hh in hs]
            da = [_nn(dos[hh], v_ref[hh, kb]) for hh in hs]
            res, ls = [None] * HG, [None] * HG
            for hh in hs:
                sp = _softplus(z[hh])
                if diag:
                    sp = jnp.where(tri, sp, 0.0)
                ls[hh] = z[hh] - sp
                res[hh] = _nn(sp.astype(BF16), upto)
            g, gb, ab, resg = [None] * HG, [None] * HG, [None] * HG, [None] * HG
            for hh in hs:
                a = jnp.exp(ls[hh] + (res[hh] - _wide(carry[3 * hh], T)))
                if diag:
                    a = jnp.where(tri, a, 0.0)
                ab[hh] = a.astype(BF16)
                g[hh] = a * da[hh]
                gb[hh] = g[hh].astype(BF16)
                resg[hh] = _nn(gb[hh], before)
            out = []
            for hh in hs:
                rem, pre_g, dq = carry[3 * hh:3 * hh + 3]
                dz = g[hh] - (g[hh] + (resg[hh] + _wide(pre_g, T))) * jnp.exp(ls[hh])
                if diag:
                    dz = jnp.where(tri, dz, 0.0)
                dzb = dz.astype(BF16)
                dk_ref[hh, kb] += _nn(qTs[hh], dzb)
                dv_ref[hh, kb] += _nn(doTs[hh], ab[hh])
                out.extend([rem - _lane(res[hh], T - 1), pre_g + (_lane(resg[hh], T - 1) + _lane(gb[hh].astype(F32), T - 1)),
                            dq + _nt(kT[hh], dzb)])
            return tuple(out)

        def qblock(i, _):
            qTs = [qt_ref[hh, i] for hh in range(HG)]
            doTs = [dot_ref[hh, i] for hh in range(HG)]
            qs = [_t_bf16(v) for v in qTs]
            dos = [_t_bf16(v) for v in doTs]
            carry = []
            for hh in range(HG):
                total = jnp.broadcast_to(tl_ref[hh, i][0:1], (LANES, T)).T
                carry.extend([total, jnp.zeros((T, LANES), F32), jnp.zeros((dh, T), F32)])

            def kstep(kb, carry):
                return blocks(qs, qTs, dos, doTs, kb, carry, False)

            carry = lax.fori_loop(0, i, kstep, tuple(carry))
            carry = blocks(qs, qTs, dos, doTs, i, carry, True)
            for hh in range(HG):
                dq_ref[hh, i] = carry[3 * hh + 2]
            return 0

        lax.fori_loop(0, nb, qblock, 0)

    G = H // HG
    ht = lambda part: pl.BlockSpec((HG, nb, dh, T), lambda h: (h + part * G, 0, 0, 0),
                                   pipeline_mode=pl.Buffered(1))
    return pl.pallas_call(
        body, name="sb_bwd", grid=(G,),
        in_specs=[ht(0), ht(1), ht(2), ht(0), pl.BlockSpec((HG, nb, 8, T), lambda h: (h, 0, 0, 0))],
        out_specs=[ht(0), ht(0), ht(0)],
        out_shape=[jax.ShapeDtypeStruct((H, nb, dh, T), F32)] * 3,
        compiler_params=_cp("arbitrary"),
    )(qkv, qkv, qkv, dot, tl)


def _swa_probs(q, kp, kc, bias, sink, first):
    T = q.shape[0]
    row = lax.broadcasted_iota(jnp.int32, (T, T), 0)
    col = lax.broadcasted_iota(jnp.int32, (T, T), 1)
    lp = jnp.where(jnp.logical_and(col > row, jnp.logical_not(first)), _nn(q, kp) + bias[:, :T], NEG_BIG)
    lc = jnp.where(col <= row, _nn(q, kc) + bias[:, T:], NEG_BIG)
    m = jnp.maximum(jnp.maximum(jnp.max(lp, axis=1, keepdims=True), jnp.max(lc, axis=1, keepdims=True)), sink)
    pp = jnp.exp(lp - m)
    pc = jnp.exp(lc - m)
    ps = jnp.exp(sink - m)
    inv = 1.0 / (jnp.sum(pp, axis=1, keepdims=True) + jnp.sum(pc, axis=1, keepdims=True) + ps)
    return pp * inv, pc * inv, ps * inv


def _swa_specs(nb, dh, T, grp, Hq, Hkv, clamp):
    blk = (lambda n: jnp.minimum(n, nb - 1)) if clamp else (lambda n: n)
    q = pl.BlockSpec((grp, None, dh, T), lambda h, n: (h, blk(n), 0, 0))
    one = lambda first, back: pl.BlockSpec(
        (None, None, dh, T), lambda h, n: (first + h, jnp.maximum(blk(n) - back, 0) if back else blk(n), 0, 0))
    return q, [one(Hq, 1), one(Hq, 0), one(Hq + Hkv, 1), one(Hq + Hkv, 0)]


def _swa_fwd(qkv, bias, sinks):
    Hq, Hkv, grp = SWA_Q_HEADS, SWA_KV_HEADS, SWA_GROUP
    _, nb, dh, T = qkv.shape

    def body(sink_ref, q_ref, kp_ref, kc_ref, vp_ref, vc_ref, bias_ref, o_ref):
        hk, n = pl.program_id(0), pl.program_id(1)
        kp, kc, vp, vc = kp_ref[...], kc_ref[...], vp_ref[...], vc_ref[...]
        for g in range(grp):
            pp, pc, _ = _swa_probs(_t_bf16(q_ref[g]), kp, kc, bias_ref[g], sink_ref[hk * grp + g], n == 0)
            o_ref[g] = _nt(vp, pp.astype(BF16)) + _nt(vc, pc.astype(BF16))

    q_spec, kv_specs = _swa_specs(nb, dh, T, grp, Hq, Hkv, False)
    return pl.pallas_call(
        body, name="swa_fwd", grid=(Hkv, nb),
        in_specs=[pl.BlockSpec(memory_space=pltpu.SMEM), q_spec] + kv_specs
                 + [pl.BlockSpec((grp, T, 2 * T), lambda h, n: (h, 0, 0))],
        out_specs=pl.BlockSpec((grp, None, dh, T), lambda h, n: (h, n, 0, 0)),
        out_shape=jax.ShapeDtypeStruct((Hq, nb, dh, T), F32),
        compiler_params=_cp("arbitrary", "arbitrary"),
    )(sinks, qkv, qkv, qkv, qkv, qkv, bias)


def _swa_bwd(qkv, bias, sinks, dot, ot):
    Hq, Hkv, grp = SWA_Q_HEADS, SWA_KV_HEADS, SWA_GROUP
    _, nb, dh, T = qkv.shape

    def body(sink_ref, qt_ref, kp_ref, kc_ref, vp_ref, vc_ref, bias_ref, dot_ref, ot_ref,
             dq_ref, dk_ref, dv_ref, dbias_ref, dsink_ref, ck, cv):
        hk, n = pl.program_id(0), pl.program_id(1)

        @pl.when(n == 0)
        def _():
            dbias_ref[...] = jnp.zeros_like(dbias_ref)
            dsink_ref[...] = jnp.zeros_like(dsink_ref)
            ck[...] = jnp.zeros_like(ck)
            cv[...] = jnp.zeros_like(cv)

        @pl.when(n < nb)
        def _():
            kp, kc, vp, vc = kp_ref[...], kc_ref[...], vp_ref[...], vc_ref[...]
            kprev = jnp.zeros((dh, T), F32)
            vprev = jnp.zeros((dh, T), F32)
            kcur = jnp.zeros((dh, T), F32)
            vcur = jnp.zeros((dh, T), F32)
            for g in range(grp):
                qT = qt_ref[g]
                pp, pc, ps = _swa_probs(_t_bf16(qT), kp, kc, bias_ref[g], sink_ref[hk * grp + g], n == 0)
                do = dot_ref[g].T
                dob = do.astype(BF16)
                doT = dot_ref[g].astype(BF16)
                delta = jnp.sum(do * ot_ref[g].T, axis=1, keepdims=True)
                dlp = pp * (_nn(dob, vp) - delta)
                dlc = pc * (_nn(dob, vc) - delta)
                dbias_ref[g, :, :T] += dlp
                dbias_ref[g, :, T:] += dlc
                dsink_ref[g] += -ps * delta
                dlpb, dlcb = dlp.astype(BF16), dlc.astype(BF16)
                dq_ref[g] = _nt(kp, dlpb) + _nt(kc, dlcb)
                kprev += _nn(qT, dlpb)
                kcur += _nn(qT, dlcb)
                vprev += _nn(doT, pp.astype(BF16))
                vcur += _nn(doT, pc.astype(BF16))
            dk_ref[...] = ck[...] + kprev
            dv_ref[...] = cv[...] + vprev
            ck[...] = kcur
            cv[...] = vcur

        @pl.when(n == nb)
        def _():
            dk_ref[...] = ck[...]
            dv_ref[...] = cv[...]

    qt_spec, kv_specs = _swa_specs(nb, dh, T, grp, Hq, Hkv, True)
    prev = pl.BlockSpec((None, None, dh, T), lambda h, n: (h, jnp.maximum(n - 1, 0), 0, 0))
    per_group = lambda a, b: pl.BlockSpec((grp, a, b), lambda h, n: (h, 0, 0))
    return pl.pallas_call(
        body, name="swa_bwd", grid=(Hkv, nb + 1),
        in_specs=[pl.BlockSpec(memory_space=pltpu.SMEM), qt_spec] + kv_specs
                 + [per_group(T, 2 * T), qt_spec, qt_spec],
        out_specs=[qt_spec, prev, prev, per_group(T, 2 * T), per_group(T, 1)],
        out_shape=[jax.ShapeDtypeStruct((Hq, nb, dh, T), F32), jax.ShapeDtypeStruct((Hkv, nb, dh, T), F32),
                   jax.ShapeDtypeStruct((Hkv, nb, dh, T), F32), jax.ShapeDtypeStruct((Hq, T, 2 * T), F32),
                   jax.ShapeDtypeStruct((Hq, T, 1), F32)],
        scratch_shapes=[pltpu.VMEM((dh, T), F32), pltpu.VMEM((dh, T), F32)],
        compiler_params=_cp("arbitrary", "arbitrary"),
    )(sinks, qkv, qkv, qkv, qkv, qkv, bias, dot, ot)


def _split3(x):
    h1 = x.astype(BF16)
    r1 = x - h1.astype(F32)
    h2 = r1.astype(BF16)
    h3 = (r1 - h2.astype(F32)).astype(BF16)
    return h1, h2, h3


def _bias_expand(rel_t, onehot):
    Hq, NB = rel_t.shape
    L = onehot.shape[1]

    def body(r_ref, oh_ref, o_ref):
        h1, h2, h3 = _split3(r_ref[...])
        oh = oh_ref[...]
        o_ref[...] = _nn(h1, oh) + _nn(h2, oh) + _nn(h3, oh)

    return pl.pallas_call(
        body, name="bias_expand", grid=(1,),
        in_specs=[pl.BlockSpec((Hq, NB), lambda i: (0, 0)), pl.BlockSpec((NB, L), lambda i: (0, 0))],
        out_specs=pl.BlockSpec((Hq, L), lambda i: (0, 0)),
        out_shape=jax.ShapeDtypeStruct((Hq, L), F32),
        compiler_params=_cp("arbitrary"),
    )(rel_t, onehot)


def _bias_reduce(dbias, onehot):
    Hq, L = dbias.shape
    NB = onehot.shape[0]

    def body(d_ref, oh_ref, o_ref):
        h1, h2, h3 = _split3(d_ref[...])
        oh = oh_ref[...]
        o_ref[...] = _nt(h1, oh) + _nt(h2, oh) + _nt(h3, oh)

    return pl.pallas_call(
        body, name="bias_reduce", grid=(1,),
        in_specs=[pl.BlockSpec((Hq, L), lambda i: (0, 0)), pl.BlockSpec((NB, L), lambda i: (0, 0))],
        out_specs=pl.BlockSpec((Hq, NB), lambda i: (0, 0)),
        out_shape=jax.ShapeDtypeStruct((Hq, NB), F32),
        compiler_params=_cp("arbitrary"),
    )(dbias, onehot)


def _adamw(w, g, m, v, name):
    R, C = w.shape
    tr = 256 if R % 256 == 0 else R
    bc1 = 1.0 - ADAM_B1 ** ADAM_STEP
    bc2 = 1.0 - ADAM_B2 ** ADAM_STEP

    def body(w_ref, g_ref, m_ref, v_ref, d_ref, nm_ref, nv_ref):
        g = g_ref[...]
        m2 = ADAM_B1 * m_ref[...] + (1.0 - ADAM_B1) * g
        v2 = ADAM_B2 * v_ref[...] + (1.0 - ADAM_B2) * (g * g)
        nm_ref[...] = m2
        nv_ref[...] = v2
        d_ref[...] = -ADAM_LR * ((m2 / bc1) / (jnp.sqrt(v2 / bc2) + ADAM_EPS) + ADAM_WD * w_ref[...])

    spec = pl.BlockSpec((tr, C), lambda i: (i, 0))
    return pl.pallas_call(
        body, name=name, grid=(R // tr,),
        in_specs=[spec] * 4, out_specs=[spec] * 3,
        out_shape=[jax.ShapeDtypeStruct((R, C), F32)] * 3,
        compiler_params=_cp("arbitrary"),
    )(w, g, m, v)


def _add_halves(mine, recv, name):
    K, R, C = mine.shape
    tr = 416 if R % 416 == 0 else R

    def body(a_ref, b_ref, o_ref, ob_ref):
        s = a_ref[...] + b_ref[...]
        o_ref[...] = s
        ob_ref[...] = s.astype(BF16)

    spec = pl.BlockSpec((None, tr, C), lambda k, i: (k, i, 0))
    return pl.pallas_call(
        body, name=name, grid=(K, R // tr),
        in_specs=[spec, spec], out_specs=[spec, spec],
        out_shape=[jax.ShapeDtypeStruct((K, R, C), F32), jax.ShapeDtypeStruct((K, R, C), BF16)],
        compiler_params=_cp("arbitrary", "arbitrary"),
    )(mine, recv)


def _add_received(own, recv, name):
    R, C = own.shape
    tr = 416 if R % 416 == 0 else R

    def body(a_ref, r_ref, o_ref):
        o_ref[...] = ((a_ref[...] + r_ref[0].astype(F32)) + r_ref[1].astype(F32)) + r_ref[2].astype(F32)

    return pl.pallas_call(
        body, name=name, grid=(R // tr,),
        in_specs=[pl.BlockSpec((tr, C), lambda i: (i, 0)), pl.BlockSpec((3, tr, C), lambda i: (0, i, 0))],
        out_specs=pl.BlockSpec((tr, C), lambda i: (i, 0)),
        out_shape=jax.ShapeDtypeStruct((R, C), F32),
        compiler_params=_cp("arbitrary"),
    )(own, recv)


def _position():
    x, y, c = lax.axis_index("x"), lax.axis_index("y"), lax.axis_index("c")
    others = [(1 - x, y), (x, 1 - y), (1 - x, 1 - y)]
    return x, y, c, others


def _remote(src, dst, send_sems, recv_sems, k, dev):
    return pltpu.make_async_remote_copy(src_ref=src, dst_ref=dst, send_sem=send_sems.at[k],
                                        recv_sem=recv_sems.at[k], device_id=dev, device_id_type=MESH_ID)


def _gather_weights(shard):
    R, C = shard.shape
    half = R // 2

    def body(src, out, send_sems, recv_sems):
        x, y, c, others = _position()
        mine = 2 * x + y
        sends = [_remote(src.at[c], out.at[mine, c], send_sems, recv_sems, j, (ox, oy, c))
                 for j, (ox, oy) in enumerate(others)]
        for cp in sends:
            cp.start()
        passed = []
        for j, (ox, oy) in enumerate(others):
            slot = out.at[2 * ox + oy, c]
            _remote(slot, slot, send_sems, recv_sems, j, (ox, oy, c)).wait_recv()
            fwd = _remote(slot, slot, send_sems, recv_sems, 3 + j, (x, y, 1 - c))
            fwd.start()
            passed.append(fwd)
        for j, (ox, oy) in enumerate(others):
            slot = out.at[2 * ox + oy, 1 - c]
            _remote(slot, slot, send_sems, recv_sems, 3 + j, (x, y, 1 - c)).wait_recv()
        for cp in sends + passed:
            cp.wait_send()

    return pl.pallas_call(
        body, name="gather_weights",
        in_specs=[ANY], out_specs=ANY,
        out_shape=jax.ShapeDtypeStruct((N_CHIPS, 2, half, C), shard.dtype),
        scratch_shapes=[pltpu.SemaphoreType.DMA((6,)), pltpu.SemaphoreType.DMA((6,))],
    )(shard.reshape(2, half, C)).reshape(N_CHIPS, R, C)


def _swap_halves(grads):
    K, R, C = grads.shape
    half = R // 2

    def body(src, out, send_sems, recv_sems):
        x, y, c, _ = _position()
        theirs = src.at[:, pl.ds(pl.multiple_of((1 - c) * half, 8), half), :]
        cp = _remote(theirs, out, send_sems, recv_sems, 0, (x, y, 1 - c))
        cp.start()
        cp.wait()

    return pl.pallas_call(
        body, name="swap_halves",
        in_specs=[ANY], out_specs=ANY,
        out_shape=jax.ShapeDtypeStruct((K, half, C), grads.dtype),
        scratch_shapes=[pltpu.SemaphoreType.DMA((1,)), pltpu.SemaphoreType.DMA((1,))],
    )(grads)


def _scatter_to_owners(parts):
    K, H, C = parts.shape

    def body(src, out, send_sems, recv_sems):
        x, y, c, others = _position()
        sends = [_remote(src.at[2 * ox + oy], out.at[j], send_sems, recv_sems, j, (ox, oy, c))
                 for j, (ox, oy) in enumerate(others)]
        for cp in sends:
            cp.start()
        for cp in sends:
            cp.wait()

    return pl.pallas_call(
        body, name="scatter_to_owners",
        in_specs=[ANY], out_specs=ANY,
        out_shape=jax.ShapeDtypeStruct((3, H, C), parts.dtype),
        scratch_shapes=[pltpu.SemaphoreType.DMA((3,)), pltpu.SemaphoreType.DMA((3,))],
    )(parts)


def _swap_reduced(half_rows):
    H, C = half_rows.shape

    def body(src, out, send_sems, recv_sems):
        x, y, c, _ = _position()
        cp = _remote(src, out, send_sems, recv_sems, 0, (x, y, 1 - c))
        cp.start()
        cp.wait()

    return pl.pallas_call(
        body, name="swap_reduced",
        in_specs=[ANY], out_specs=ANY,
        out_shape=jax.ShapeDtypeStruct((H, C), half_rows.dtype),
        scratch_shapes=[pltpu.SemaphoreType.DMA((1,)), pltpu.SemaphoreType.DMA((1,))],
    )(half_rows)


def _allreduce_small(block):
    R, C = block.shape
    n_dev = 8

    def body(src, out, slots, send_sems, recv_sems):
        x, y, c, _ = _position()
        me = 4 * x + 2 * y + c
        slots[me] = src[...]
        sends = []
        for r in range(1, n_dev):
            peer = (x ^ (r >> 2), y ^ ((r >> 1) & 1), c ^ (r & 1))
            cp = _remote(src, slots.at[me], send_sems, recv_sems, r - 1, peer)
            cp.start()
            sends.append(cp)
        for r in range(1, n_dev):
            theirs = slots.at[me ^ r]
            _remote(theirs, theirs, send_sems, recv_sems, r - 1, (x, y, c)).wait_recv()
        for cp in sends:
            cp.wait_send()
        acc = slots[0]
        for d in range(1, n_dev):
            acc = acc + slots[d]
        out[...] = acc

    return pl.pallas_call(
        body, name="allreduce_small",
        in_specs=[pl.BlockSpec(memory_space=pltpu.VMEM)], out_specs=pl.BlockSpec(memory_space=pltpu.VMEM),
        out_shape=jax.ShapeDtypeStruct((R, C), F32),
        scratch_shapes=[pltpu.VMEM((n_dev, R, C), F32), pltpu.SemaphoreType.DMA((7,)), pltpu.SemaphoreType.DMA((7,))],
    )(block)


def _rel_bucket(dist):
    max_exact = REL_BUCKETS // 2
    d = jnp.maximum(dist, 1).astype(F32)
    large = max_exact + (jnp.log(d / max_exact) / math.log(REL_MAX_DIST / max_exact)
                         * (REL_BUCKETS - max_exact)).astype(jnp.int32)
    large = jnp.minimum(large, REL_BUCKETS - 1)
    return jnp.where(dist < max_exact, dist, large)


def _bucket_onehot():
    T = SWA_BLOCK
    dist = (jnp.arange(T)[:, None] + T) - jnp.arange(2 * T)[None, :]
    bucket = _rel_bucket(jnp.maximum(dist, 0)).reshape(1, T * 2 * T)
    return (bucket == jnp.arange(REL_BUCKETS)[:, None]).astype(BF16)


_BUF = (("ffn1_w1", "t"), ("ffn1_w3", "t"), ("ffn1_w2", "n"), ("ffn2_w1", "t"), ("ffn2_w3", "t"),
        ("ffn2_w2", "n"), ("w_in", "t"), ("w_out", "n"), ("w_branch_swa", "tw"), ("w_branch_sb", "tw"))


def _to_rows(name_kind, w, D):
    kind = name_kind[1]
    if kind == "n":
        return w
    if kind == "t":
        return w.T
    return w.T.reshape(-1, D)


def _from_rows(name_kind, rows, width):
    kind = name_kind[1]
    if kind == "n":
        return rows
    if kind == "t":
        return rows.T
    return rows.reshape(-1, width).T


def kernel(x, norm_ffn1, ffn1_w1, ffn1_w3, ffn1_w2, norm_mix, w_in, swa_sinks, rel_bias, w_branch_swa, w_branch_sb, w_out, norm_ffn2, ffn2_w1, ffn2_w3, ffn2_w2, norm_final, loss_target, m_norm_ffn1, m_ffn1_w1, m_ffn1_w3, m_ffn1_w2, m_norm_mix, m_w_in, m_swa_sinks, m_rel_bias, m_w_branch_swa, m_w_branch_sb, m_w_out, m_norm_ffn2, m_ffn2_w1, m_ffn2_w3, m_ffn2_w2, m_norm_final, v_norm_ffn1, v_ffn1_w1, v_ffn1_w3, v_ffn1_w2, v_norm_mix, v_w_in, v_swa_sinks, v_rel_bias, v_w_branch_swa, v_w_branch_sb, v_w_out, v_norm_ffn2, v_ffn2_w1, v_ffn2_w3, v_ffn2_w2, v_norm_final):
    names = ["norm_ffn1", "ffn1_w1", "ffn1_w3", "ffn1_w2", "norm_mix", "w_in", "swa_sinks", "rel_bias",
             "w_branch_swa", "w_branch_sb", "w_out", "norm_ffn2", "ffn2_w1", "ffn2_w3", "ffn2_w2", "norm_final"]
    W = dict(zip(names, [norm_ffn1, ffn1_w1, ffn1_w3, ffn1_w2, norm_mix, w_in, swa_sinks, rel_bias,
                         w_branch_swa, w_branch_sb, w_out, norm_ffn2, ffn2_w1, ffn2_w3, ffn2_w2, norm_final]))
    M = dict(zip(names, [m_norm_ffn1, m_ffn1_w1, m_ffn1_w3, m_ffn1_w2, m_norm_mix, m_w_in, m_swa_sinks, m_rel_bias,
                         m_w_branch_swa, m_w_branch_sb, m_w_out, m_norm_ffn2, m_ffn2_w1, m_ffn2_w3, m_ffn2_w2,
                         m_norm_final]))
    V = dict(zip(names, [v_norm_ffn1, v_ffn1_w1, v_ffn1_w3, v_ffn1_w2, v_norm_mix, v_w_in, v_swa_sinks, v_rel_bias,
                         v_w_branch_swa, v_w_branch_sb, v_w_out, v_norm_ffn2, v_ffn2_w1, v_ffn2_w3, v_ffn2_w2,
                         v_norm_final]))
    xs = x[0]
    target = loss_target[0]
    S, D = xs.shape
    QW = SWA_Q_HEADS * HEAD_DIM
    KW = SWA_KV_HEADS * HEAD_DIM
    BW = SB_HEADS * HEAD_DIM
    QKV = QW + 2 * KW + 3 * BW

    pieces = [_to_rows(nk, W[nk[0]][0], D) for nk in _BUF]
    sizes = [p.shape[0] for p in pieces]
    offs = [0]
    for s in sizes:
        offs.append(offs[-1] + s)
    shard = jnp.concatenate(pieces, axis=0).astype(BF16)
    chip = 2 * lax.axis_index("x") + lax.axis_index("y")
    gathered = lax.dynamic_update_slice(_gather_weights(shard), shard[None], (chip, 0, 0))

    def full(i):
        return gathered[:, offs[i]:offs[i + 1], :].reshape(N_CHIPS * sizes[i], D)

    f1w1, f1w3, f1w2, f2w1, f2w3, f2w2, w_in_t, w_out_f = [full(i) for i in range(8)]
    wa_t = full(8).reshape(D, QW)
    wb_t = full(9).reshape(D, BW)

    g1, gmix, g3 = W["norm_ffn1"], W["norm_mix"], W["norm_ffn2"]
    gf = W["norm_final"].reshape(1, D)

    x1, h1, a1, b1 = _ffn_fwd(xs, g1, f1w1, f1w3, f1w2, "ffn1_fwd")
    o0 = QW + 2 * KW
    rows = jnp.arange(w_in_t.shape[0])
    is_q = (rows < QW) | ((rows >= o0) & (rows < o0 + BW))
    w_in_s = w_in_t * jnp.where(is_q, QK_SCALE, 1.0).astype(BF16)[:, None]
    qkv_a = _norm_proj_heads(x1, gmix, w_in_s[:o0], SWA_BLOCK, "proj_swa")
    qkv_b = _norm_proj_heads(x1, gmix, w_in_s[o0:QKV], SB_BLOCK, "proj_sb")
    gates, h2 = _norm_matmul_nt(x1, gmix, w_in_t[QKV:], F32, "proj_gates")

    onehot = _bucket_onehot()
    bias = _bias_expand(W["rel_bias"].T, onehot).reshape(SWA_Q_HEADS, SWA_BLOCK, 2 * SWA_BLOCK)
    sinks = W["swa_sinks"].reshape(SWA_Q_HEADS)
    oa_t = _swa_fwd(qkv_a, bias, sinks)
    ob_t, totals = _sb_fwd(qkv_b)

    x2, merged, ba, bb = _merge_fwd(x1, gates, oa_t, ob_t, wa_t, wb_t, w_out_f)
    x3, h3, a2, b2 = _ffn_fwd(x2, g3, f2w1, f2w3, f2w2, "ffn2_fwd")
    loss_part, dx3, dgf = _final_loss(x3, gf, target)

    dx2, dg3, dz2, da2, db2, u2 = _ffn_bwd(dx3, x2, g3, a2, b2, f2w1, f2w3, f2w2, "ffn2_bwd")
    grads = {}
    grads["ffn2_w1"] = _tn_matmul(da2, h3, "ffn2_dw1")
    grads["ffn2_w3"] = _tn_matmul(db2, h3, "ffn2_dw3")
    grads["ffn2_w2"] = _tn_matmul(u2, dz2, "ffn2_dw2")

    dx2b, dba, dbb, dgates, doa_t, dob_t = _merge_bwd(dx2, gates, ba, bb, wa_t.T, wb_t.T, w_out_f,
                                                      SWA_BLOCK, SB_BLOCK)
    grads["w_out"] = _tn_matmul(merged, dx2b, "dw_out")
    grads["w_branch_swa"] = _heads_matmul(oa_t, dba, 1.0, "dw_branch_swa").T
    grads["w_branch_sb"] = _heads_matmul(ob_t, dbb, 1.0, "dw_branch_sb").T

    dqb_t, dkb_t, dvb_t = _sb_bwd(qkv_b, dob_t, totals)
    dqa_t, dka_t, dva_t, dbias, dsink_rows = _swa_bwd(qkv_a, bias, sinks, doa_t, oa_t)
    d_rel = _bias_reduce(dbias.reshape(SWA_Q_HEADS, -1), onehot).T
    d_sinks = jnp.sum(dsink_rows, axis=(1, 2))

    dheads = [(dqa_t, QK_SCALE, "q_swa"), (dka_t, 1.0, "k_swa"), (dva_t, 1.0, "v_swa"),
              (dqb_t, QK_SCALE, "q_sb"), (dkb_t, 1.0, "k_sb"), (dvb_t, 1.0, "v_sb")]
    grads["w_in"] = jnp.concatenate([_heads_matmul(a, h2, sc, "dw_in_" + nm) for a, sc, nm in dheads]
                                    + [_tn_matmul(dgates, h2, "dw_in_gates")], axis=0)
    row0, pieces_in = 0, []
    for a, _, _ in dheads:
        pieces_in.append((a, row0))
        row0 += a.shape[0] * HEAD_DIM
    dx1, dgmix = _proj_bwd(pieces_in, dgates, w_in_s, x1, gmix, dx2)

    dx0, dg1, dz1, da1, db1, u1 = _ffn_bwd(dx1, xs, g1, a1, b1, f1w1, f1w3, f1w2, "ffn1_bwd")
    grads["ffn1_w1"] = _tn_matmul(da1, h1, "ffn1_dw1")
    grads["ffn1_w3"] = _tn_matmul(db1, h1, "ffn1_dw3")
    grads["ffn1_w2"] = _tn_matmul(u1, dz1, "ffn1_dw2")

    gparts = [grads[nk[0]].reshape(N_CHIPS, sizes[i], D) for i, nk in enumerate(_BUF)]
    gbuf = jnp.concatenate(gparts, axis=1)
    R = gbuf.shape[1]
    half = R // 2
    c = lax.axis_index("c")
    mine = 2 * lax.axis_index("x") + lax.axis_index("y")
    from_sibling = _swap_halves(gbuf)
    my_half = lax.dynamic_slice_in_dim(gbuf, c * half, half, axis=1)
    chip_sum, chip_sum16 = _add_halves(my_half, from_sibling, "add_sibling")
    received = _scatter_to_owners(chip_sum16)
    own = lax.dynamic_index_in_dim(chip_sum, mine, axis=0, keepdims=False)
    my_rows = _add_received(own, received, "add_chips")
    their_rows = _swap_reduced(my_rows)
    reduced = jnp.concatenate([jnp.where(c == 0, my_rows, their_rows), jnp.where(c == 0, their_rows, my_rows)], axis=0)

    small_rows = [dg1, dgmix, dg3, dgf,
                  jnp.pad(d_sinks.reshape(1, -1), ((0, 0), (0, D - SWA_Q_HEADS))),
                  jnp.pad(d_rel.reshape(1, -1), ((0, 0), (0, D - REL_BUCKETS * SWA_Q_HEADS))),
                  jnp.pad(loss_part, ((0, 0), (0, D - 1))), jnp.zeros((1, D), F32)]
    small = _allreduce_small(jnp.concatenate(small_rows, axis=0))
    loss = small[6, 0]

    G = {}
    for i, nk in enumerate(_BUF):
        G[nk[0]] = _from_rows(nk, reduced[offs[i]:offs[i + 1]], W[nk[0]].shape[1])[None]
    G["norm_ffn1"], G["norm_mix"], G["norm_ffn2"] = small[0:1], small[1:2], small[2:3]
    G["norm_final"] = small[3]
    G["swa_sinks"] = small[4:5, :SWA_Q_HEADS]
    G["rel_bias"] = small[5, :REL_BUCKETS * SWA_Q_HEADS].reshape(REL_BUCKETS, SWA_Q_HEADS)

    delta, new_m, new_v = {}, {}, {}
    small_names = ["norm_ffn1", "norm_mix", "norm_ffn2", "norm_final", "swa_sinks", "rel_bias"]

    def pack(d):
        return jnp.concatenate([jnp.pad(d[n].reshape(1, -1), ((0, 0), (0, D - d[n].size))) for n in small_names]
                               + [jnp.zeros((2, D), F32)], axis=0)

    sd, sm, sv = _adamw(pack(W), pack(G), pack(M), pack(V), "adamw_small")
    for r, n in enumerate(small_names):
        for dst, src in ((delta, sd), (new_m, sm), (new_v, sv)):
            dst[n] = src[r, :W[n].size].reshape(W[n].shape)
    for nk in _BUF:
        n = nk[0]
        shp = W[n].shape
        two_d = (shp[1], shp[2])
        d_, m_, v_ = _adamw(W[n].reshape(two_d), G[n].reshape(two_d), M[n].reshape(two_d), V[n].reshape(two_d),
                            "adamw_" + n)
        delta[n], new_m[n], new_v[n] = d_.reshape(shp), m_.reshape(shp), v_.reshape(shp)

    return (loss, dx0[None], *[G[n] for n in names], *[delta[n] for n in names],
            *[new_m[n] for n in names], *[new_v[n] for n in names])
```

```python
import functools
import math

import jax
import jax.numpy as jnp
from jax import lax
from jax.experimental import pallas as pl
from jax.experimental.pallas import tpu as pltpu

F32, BF16 = jnp.float32, jnp.bfloat16
MESH_ID = pl.DeviceIdType.MESH
ANY = pl.BlockSpec(memory_space=pl.ANY)

RMS_EPS = 1e-6
HEAD_DIM = 64
SWA_Q_HEADS, SWA_KV_HEADS, SWA_GROUP = 8, 2, 4
SWA_BLOCK = 128
SB_HEADS = 8
SB_BLOCK = 256
REL_BUCKETS, REL_MAX_DIST = 32, 128
NEG_BIG = -1e30
QK_SCALE = HEAD_DIM ** -0.5
ADAM_LR, ADAM_B1, ADAM_B2, ADAM_EPS, ADAM_WD, ADAM_STEP = 0.001, 0.9, 0.999, 1e-08, 0.01, 10

N_CHIPS = 4
TOKEN_TILE = 512
MATMUL_TOKEN_TILE = 1024
WGRAD_ROW_TILES = (2176, 1408, 1024, 256)
FF_TILE = 256
VMEM_LIMIT = 56 * 1024 * 1024


def _cp(*sem):
    return pltpu.CompilerParams(dimension_semantics=sem, vmem_limit_bytes=VMEM_LIMIT)


def _nn(a, b):
    return jnp.dot(a, b, preferred_element_type=F32)


def _nt(a, b):
    return lax.dot_general(a, b, (((1,), (1,)), ((), ())), preferred_element_type=F32)


def _tn(a, b):
    return lax.dot_general(a, b, (((0,), (0,)), ((), ())), preferred_element_type=F32)


def _norm_fwd(x, g):
    return x * lax.rsqrt(jnp.mean(x * x, axis=-1, keepdims=True) + RMS_EPS) * g


def _norm_bwd(x, g, dh):
    r = lax.rsqrt(jnp.mean(x * x, axis=-1, keepdims=True) + RMS_EPS)
    xh = x * r
    dxh = dh * g
    dx = r * (dxh - xh * jnp.mean(dxh * xh, axis=-1, keepdims=True))
    return dx, jnp.sum(dh * xh, axis=0, keepdims=True)


def _softplus(z):
    return jnp.maximum(z, 0.0) + jnp.log(1.0 + jnp.exp(-jnp.abs(z)))


def _ffn_fwd(x, g, w1t, w3t, w2, name):
    S, D = x.shape
    F = w2.shape[0]
    tm, tf = min(MATMUL_TOKEN_TILE, S), FF_TILE
    nj = F // tf

    def body(x_ref, g_ref, w1_ref, w3_ref, w2_ref, xo_ref, h_ref, a_ref, b_ref, hs, acc):
        j = pl.program_id(1)

        @pl.when(j == 0)
        def _():
            hb = _norm_fwd(x_ref[...], g_ref[...]).astype(BF16)
            hs[...] = hb
            h_ref[...] = hb
            acc[...] = jnp.zeros_like(acc)

        h = hs[...]
        a = _nt(h, w1_ref[...])
        b = _nt(h, w3_ref[...])
        a_ref[...] = a.astype(BF16)
        b_ref[...] = b.astype(BF16)
        u = a * jax.nn.sigmoid(a) * b
        acc[...] += _nn(u.astype(BF16), w2_ref[...])

        @pl.when(j == nj - 1)
        def _():
            xo_ref[...] = x_ref[...] + 0.5 * acc[...]

    return pl.pallas_call(
        body, name=name, grid=(S // tm, nj),
        in_specs=[pl.BlockSpec((tm, D), lambda i, j: (i, 0)),
                  pl.BlockSpec((1, D), lambda i, j: (0, 0)),
                  pl.BlockSpec((tf, D), lambda i, j: (j, 0)),
                  pl.BlockSpec((tf, D), lambda i, j: (j, 0)),
                  pl.BlockSpec((tf, D), lambda i, j: (j, 0))],
        out_specs=[pl.BlockSpec((tm, D), lambda i, j: (i, 0)),
                   pl.BlockSpec((tm, D), lambda i, j: (i, 0)),
                   pl.BlockSpec((tm, tf), lambda i, j: (i, j)),
                   pl.BlockSpec((tm, tf), lambda i, j: (i, j))],
        out_shape=[jax.ShapeDtypeStruct((S, D), F32), jax.ShapeDtypeStruct((S, D), BF16),
                   jax.ShapeDtypeStruct((S, F), BF16), jax.ShapeDtypeStruct((S, F), BF16)],
        scratch_shapes=[pltpu.VMEM((tm, D), BF16), pltpu.VMEM((tm, D), F32)],
        compiler_params=_cp("arbitrary", "arbitrary"),
    )(x, g, w1t, w3t, w2)


def _ffn_bwd(dxo, x, g, a, b, w1t, w3t, w2, name):
    S, D = x.shape
    F = w2.shape[0]
    tm, tf = min(MATMUL_TOKEN_TILE, S), FF_TILE
    ni, nj = S // tm, F // tf

    def body(dxo_ref, x_ref, g_ref, a_ref, b_ref, w1_ref, w3_ref, w2_ref,
             dx_ref, dg_ref, dz_ref, da_ref, db_ref, u_ref, dzs, acc):
        i, j = pl.program_id(0), pl.program_id(1)

        @pl.when(j == 0)
        def _():
            dzb = (0.5 * dxo_ref[...]).astype(BF16)
            dzs[...] = dzb
            dz_ref[...] = dzb
            acc[...] = jnp.zeros_like(acc)

        du = _nt(dzs[...], w2_ref[...])
        av = a_ref[...].astype(F32)
        bv = b_ref[...].astype(F32)
        s = jax.nn.sigmoid(av)
        silu = av * s
        db = (du * silu).astype(BF16)
        da = (du * bv * (s * (1.0 + av * (1.0 - s)))).astype(BF16)
        da_ref[...] = da
        db_ref[...] = db
        u_ref[...] = (silu * bv).astype(BF16)
        acc[...] += _nn(da, w1_ref[...]) + _nn(db, w3_ref[...])

        @pl.when(j == nj - 1)
        def _():
            dx, dg = _norm_bwd(x_ref[...], g_ref[...], acc[...])
            dx_ref[...] = dxo_ref[...] + dx

            @pl.when(i == 0)
            def _():
                dg_ref[...] = dg

            @pl.when(i > 0)
            def _():
                dg_ref[...] += dg

    row = pl.BlockSpec((tm, D), lambda i, j: (i, 0))
    wsp = pl.BlockSpec((tf, D), lambda i, j: (j, 0))
    col = pl.BlockSpec((tm, tf), lambda i, j: (i, j))
    vec = pl.BlockSpec((1, D), lambda i, j: (0, 0))
    return pl.pallas_call(
        body, name=name, grid=(ni, nj),
        in_specs=[row, row, vec, col, col, wsp, wsp, wsp],
        out_specs=[row, vec, row, col, col, col],
        out_shape=[jax.ShapeDtypeStruct((S, D), F32), jax.ShapeDtypeStruct((1, D), F32),
                   jax.ShapeDtypeStruct((S, D), BF16), jax.ShapeDtypeStruct((S, F), BF16),
                   jax.ShapeDtypeStruct((S, F), BF16), jax.ShapeDtypeStruct((S, F), BF16)],
        scratch_shapes=[pltpu.VMEM((tm, D), BF16), pltpu.VMEM((tm, D), F32)],
        compiler_params=_cp("arbitrary", "arbitrary"),
    )(dxo, x, g, a, b, w1t, w3t, w2)


def _tn_matmul(a, b, name):
    S, M = a.shape
    N = b.shape[1]
    ts = min(MATMUL_TOKEN_TILE, S)
    tmm = next(t for t in WGRAD_ROW_TILES if M % t == 0)
    ns = S // ts

    def body(a_ref, b_ref, o_ref, acc):
        s = pl.program_id(1)
        part = _tn(a_ref[...], b_ref[...])

        @pl.when(s == 0)
        def _():
            acc[...] = part

        @pl.when(s > 0)
        def _():
            acc[...] += part

        @pl.when(s == ns - 1)
        def _():
            o_ref[...] = acc[...].astype(BF16)

    return pl.pallas_call(
        body, name=name, grid=(M // tmm, ns),
        in_specs=[pl.BlockSpec((ts, tmm), lambda m, s: (s, m)),
                  pl.BlockSpec((ts, N), lambda m, s: (s, 0))],
        out_specs=pl.BlockSpec((tmm, N), lambda m, s: (m, 0)),
        out_shape=jax.ShapeDtypeStruct((M, N), BF16),
        scratch_shapes=[pltpu.VMEM((tmm, N), F32)],
        compiler_params=_cp("arbitrary", "arbitrary"),
    )(a, b)


def _norm_matmul_nt(x, g, wt, out_dtype, name):
    S, D = x.shape
    N = wt.shape[0]
    tm = min(MATMUL_TOKEN_TILE, S)
    tn = next(t for t in (1024, 768, 256) if N % t == 0)

    def body(x_ref, g_ref, w_ref, o_ref, h_ref, hs):
        @pl.when(pl.program_id(1) == 0)
        def _():
            hb = _norm_fwd(x_ref[...], g_ref[...]).astype(BF16)
            hs[...] = hb
            h_ref[...] = hb

        o_ref[...] = _nt(hs[...], w_ref[...]).astype(out_dtype)

    return pl.pallas_call(
        body, name=name, grid=(S // tm, N // tn),
        in_specs=[pl.BlockSpec((tm, D), lambda i, j: (i, 0)),
                  pl.BlockSpec((1, D), lambda i, j: (0, 0)),
                  pl.BlockSpec((tn, D), lambda i, j: (j, 0))],
        out_specs=[pl.BlockSpec((tm, tn), lambda i, j: (i, j)),
                   pl.BlockSpec((tm, D), lambda i, j: (i, 0))],
        out_shape=[jax.ShapeDtypeStruct((S, N), out_dtype), jax.ShapeDtypeStruct((S, D), BF16)],
        scratch_shapes=[pltpu.VMEM((tm, D), BF16)],
        compiler_params=_cp("arbitrary", "arbitrary"),
    )(x, g, wt)


def _heads_tile(ref):
    Hh, nbk = ref.shape[0], ref.shape[1]
    return jnp.concatenate([jnp.concatenate([ref[h, b] for b in range(nbk)], axis=1) for h in range(Hh)], axis=0)


def _store_heads(ref, val):
    Hh, nbk, dh, T = ref.shape
    for h in range(Hh):
        for b in range(nbk):
            ref[h, b] = val[h * dh:(h + 1) * dh, b * T:(b + 1) * T].astype(ref.dtype)


def _norm_proj_heads(x, g, w_rows, T, name):
    S, D = x.shape
    N = w_rows.shape[0]
    tm, tn = min(MATMUL_TOKEN_TILE, S), 768

    def body(x_ref, g_ref, w_ref, o_ref, hs):
        @pl.when(pl.program_id(1) == 0)
        def _():
            hs[...] = _norm_fwd(x_ref[...], g_ref[...]).astype(BF16)

        _store_heads(o_ref, _nt(w_ref[...], hs[...]))

    return pl.pallas_call(
        body, name=name, grid=(S // tm, N // tn),
        in_specs=[pl.BlockSpec((tm, D), lambda i, j: (i, 0)),
                  pl.BlockSpec((1, D), lambda i, j: (0, 0)),
                  pl.BlockSpec((tn, D), lambda i, j: (j, 0))],
        out_specs=pl.BlockSpec((tn // HEAD_DIM, tm // T, HEAD_DIM, T), lambda i, j: (j, i, 0, 0)),
        out_shape=jax.ShapeDtypeStruct((N // HEAD_DIM, S // T, HEAD_DIM, T), BF16),
        scratch_shapes=[pltpu.VMEM((tm, D), BF16)],
        compiler_params=_cp("arbitrary", "arbitrary"),
    )(x, g, w_rows)


def _heads_matmul(at, b, scale, name):
    Hh, nb, dh, T = at.shape
    S, N = b.shape
    ts = min(MATMUL_TOKEN_TILE, S)
    ns = S // ts

    def body(a_ref, b_ref, o_ref):
        s = pl.program_id(0)
        a = _heads_tile(a_ref)
        part = _nn((a if scale == 1.0 else a * scale).astype(BF16), b_ref[...])

        @pl.when(s == 0)
        def _():
            o_ref[...] = part

        @pl.when(s > 0)
        def _():
            o_ref[...] += part

    return pl.pallas_call(
        body, name=name, grid=(ns,),
        in_specs=[pl.BlockSpec((Hh, ts // T, dh, T), lambda s: (0, s, 0, 0)),
                  pl.BlockSpec((ts, N), lambda s: (s, 0))],
        out_specs=pl.BlockSpec((Hh * dh, N), lambda s: (0, 0)),
        out_shape=jax.ShapeDtypeStruct((Hh * dh, N), F32),
        compiler_params=_cp("arbitrary"),
    )(at, b)


def _proj_bwd(pieces, dgates, w_rows, x, g, dres):
    S, D = x.shape
    tm = min(TOKEN_TILE, S)
    n_p = len(pieces)
    gate_row = w_rows.shape[0] - dgates.shape[1]

    def body(*refs):
        p_refs = refs[:n_p]
        dgt_ref, w_ref, x_ref, g_ref, dres_ref, dx_ref, dg_ref = refs[n_p:]
        i = pl.program_id(0)
        dh = _nn(dgt_ref[...], w_ref[gate_row:, :])
        for p_ref, (arr, row0) in zip(p_refs, pieces):
            rows = arr.shape[0] * arr.shape[2]
            dh += _tn(_heads_tile(p_ref).astype(BF16), w_ref[row0:row0 + rows, :])
        dx, dg = _norm_bwd(x_ref[...], g_ref[...], dh)
        dx_ref[...] = dres_ref[...] + dx

        @pl.when(i == 0)
        def _():
            dg_ref[...] = dg

        @pl.when(i > 0)
        def _():
            dg_ref[...] += dg

    row = pl.BlockSpec((tm, D), lambda i: (i, 0))
    vec = pl.BlockSpec((1, D), lambda i: (0, 0))
    p_specs = [pl.BlockSpec((a.shape[0], tm // a.shape[3], a.shape[2], a.shape[3]), lambda i: (0, i, 0, 0))
               for a, _ in pieces]
    return pl.pallas_call(
        body, name="proj_bwd", grid=(S // tm,),
        in_specs=p_specs + [pl.BlockSpec((tm, dgates.shape[1]), lambda i: (i, 0)),
                            pl.BlockSpec(w_rows.shape, lambda i: (0, 0), pipeline_mode=pl.Buffered(1)),
                            row, vec, row],
        out_specs=[row, vec],
        out_shape=[jax.ShapeDtypeStruct((S, D), F32), jax.ShapeDtypeStruct((1, D), F32)],
        compiler_params=_cp("arbitrary"),
    )(*[a for a, _ in pieces], dgates, w_rows, x, g, dres)


def _merge_fwd(x1, gates, oa_t, ob_t, wat, wbt, w_out):
    S, D = x1.shape
    W = wat.shape[1]
    tm = min(TOKEN_TILE, S)

    def body(x_ref, ga_ref, gb_ref, oa_ref, ob_ref, wa_ref, wb_ref, wo_ref,
             x2_ref, mg_ref, ba_ref, bb_ref):
        ba = _nt(_heads_tile(oa_ref).T.astype(BF16), wa_ref[...])
        bb = _nt(_heads_tile(ob_ref).T.astype(BF16), wb_ref[...])
        merged = jax.nn.sigmoid(ga_ref[...]) * ba + jax.nn.sigmoid(gb_ref[...]) * bb
        mb = merged.astype(BF16)
        mg_ref[...] = mb
        ba_ref[...] = ba.astype(BF16)
        bb_ref[...] = bb.astype(BF16)
        x2_ref[...] = x_ref[...] + _nn(mb, wo_ref[...])

    row = pl.BlockSpec((tm, D), lambda i: (i, 0))
    full = lambda r, c: pl.BlockSpec((r, c), lambda i: (0, 0))
    heads = lambda a: pl.BlockSpec((a.shape[0], tm // a.shape[3], a.shape[2], a.shape[3]), lambda i: (0, i, 0, 0))
    return pl.pallas_call(
        body, name="merge_fwd", grid=(S // tm,),
        in_specs=[row, pl.BlockSpec((tm, D), lambda i: (i, 0)), pl.BlockSpec((tm, D), lambda i: (i, 1)),
                  heads(oa_t), heads(ob_t), full(D, W), full(D, W), full(D, D)],
        out_specs=[row, row, row, row],
        out_shape=[jax.ShapeDtypeStruct((S, D), F32)] + [jax.ShapeDtypeStruct((S, D), BF16)] * 3,
        compiler_params=_cp("arbitrary"),
    )(x1, gates, gates, oa_t, ob_t, wat, wbt, w_out)


def _merge_bwd(dx2, gates, ba, bb, wa, wb, w_out, t_a, t_b):
    S, D = dx2.shape
    W = wa.shape[0]
    tm = min(TOKEN_TILE, S)
    Hh = W // HEAD_DIM

    def body(dx_ref, ga_ref, gb_ref, ba_ref, bb_ref, wa_ref, wb_ref, wo_ref,
             dxb_ref, dba_ref, dbb_ref, dgt_ref, doa_ref, dob_ref):
        dxb = dx_ref[...].astype(BF16)
        dxb_ref[...] = dxb
        dm = _nt(dxb, wo_ref[...])
        sa = jax.nn.sigmoid(ga_ref[...])
        sb = jax.nn.sigmoid(gb_ref[...])
        dba = (dm * sa).astype(BF16)
        dbb = (dm * sb).astype(BF16)
        dba_ref[...] = dba
        dbb_ref[...] = dbb
        dgt_ref[:, :D] = (dm * ba_ref[...].astype(F32) * sa * (1.0 - sa)).astype(BF16)
        dgt_ref[:, D:] = (dm * bb_ref[...].astype(F32) * sb * (1.0 - sb)).astype(BF16)
        _store_heads(doa_ref, _nt(wa_ref[...], dba))
        _store_heads(dob_ref, _nt(wb_ref[...], dbb))

    row = pl.BlockSpec((tm, D), lambda i: (i, 0))
    full = lambda r, c: pl.BlockSpec((r, c), lambda i: (0, 0))
    heads = lambda T: pl.BlockSpec((Hh, tm // T, HEAD_DIM, T), lambda i: (0, i, 0, 0))
    return pl.pallas_call(
        body, name="merge_bwd", grid=(S // tm,),
        in_specs=[row, pl.BlockSpec((tm, D), lambda i: (i, 0)), pl.BlockSpec((tm, D), lambda i: (i, 1)),
                  row, row, full(W, D), full(W, D), full(D, D)],
        out_specs=[row, row, row, pl.BlockSpec((tm, 2 * D), lambda i: (i, 0)), heads(t_a), heads(t_b)],
        out_shape=[jax.ShapeDtypeStruct((S, D), BF16)] * 3 + [jax.ShapeDtypeStruct((S, 2 * D), BF16),
                   jax.ShapeDtypeStruct((Hh, S // t_a, HEAD_DIM, t_a), F32),
                   jax.ShapeDtypeStruct((Hh, S // t_b, HEAD_DIM, t_b), BF16)],
        compiler_params=_cp("arbitrary"),
    )(dx2, gates, gates, ba, bb, wa, wb, w_out)


def _final_loss(x3, gf, target):
    S, D = x3.shape
    tm = min(TOKEN_TILE, S)

    def body(x_ref, g_ref, t_ref, loss_ref, dx_ref, dg_ref):
        i = pl.program_id(0)
        x = x_ref[...]
        g = g_ref[...]
        e = _norm_fwd(x, g) - t_ref[...]
        part = 0.5 * jnp.sum(jnp.mean(e * e, axis=-1, keepdims=True), axis=0, keepdims=True)
        dx, dg = _norm_bwd(x, g, e * (1.0 / D))
        dx_ref[...] = dx

        @pl.when(i == 0)
        def _():
            loss_ref[...] = part
            dg_ref[...] = dg

        @pl.when(i > 0)
        def _():
            loss_ref[...] += part
            dg_ref[...] += dg

    row = pl.BlockSpec((tm, D), lambda i: (i, 0))
    vec = pl.BlockSpec((1, D), lambda i: (0, 0))
    return pl.pallas_call(
        body, name="final_loss", grid=(S // tm,),
        in_specs=[row, vec, row],
        out_specs=[pl.BlockSpec((1, 1), lambda i: (0, 0)), row, vec],
        out_shape=[jax.ShapeDtypeStruct((1, 1), F32), jax.ShapeDtypeStruct((S, D), F32),
                   jax.ShapeDtypeStruct((1, D), F32)],
        compiler_params=_cp("arbitrary"),
    )(x3, gf, target)


SB_FWD_HEAD_GROUP = 8
SB_HEAD_GROUP = 4
LANES = 128


def _tri(T, kind):
    r = lax.broadcasted_iota(jnp.int32, (T, T), 0)
    c = lax.broadcasted_iota(jnp.int32, (T, T), 1)
    return {"after": r > c, "upto": r <= c, "before": r < c}[kind].astype(BF16)


def _lane(v, j):
    return jnp.broadcast_to(v[:, j:j + 1], (v.shape[0], LANES))


def _t_bf16(x):
    return x.astype(F32).T.astype(BF16)


def _wide(v, T):
    return jnp.tile(v, (1, T // LANES))


def _sb_fwd(qkv):
    H3, nb, dh, T = qkv.shape
    H = H3 // 3
    HG = SB_FWD_HEAD_GROUP

    def body(q_ref, k_ref, v_ref, o_ref, tl_ref):
        row = lax.broadcasted_iota(jnp.int32, (T, T), 0)
        col = lax.broadcasted_iota(jnp.int32, (T, T), 1)
        tri = col < row
        after = _tri(T, "after")

        def blocks(qs, kb, carry, diag):
            hs = range(HG)
            z = [_nn(qs[hh], k_ref[hh, kb]) for hh in hs]
            res, ls, first = [None] * HG, [None] * HG, [None] * HG
            for hh in hs:
                sp = _softplus(z[hh])
                if diag:
                    sp = jnp.where(tri, sp, 0.0)
                ls[hh] = z[hh] - sp
                spb = sp.astype(BF16)
                first[hh] = _lane(spb.astype(F32), 0)
                res[hh] = _nn(spb, after)
            out = []
            for hh in hs:
                c, oacc = carry[2 * hh], carry[2 * hh + 1]
                a = jnp.exp(ls[hh] - (res[hh] + _wide(c, T)))
                if diag:
                    a = jnp.where(tri, a, 0.0)
                out.extend([c + (first[hh] + _lane(res[hh], 0)), oacc + _nt(v_ref[hh, kb], a.astype(BF16))])
            return tuple(out)

        def qblock(i, _):
            qs = [_t_bf16(q_ref[hh, i]) for hh in range(HG)]
            carry = blocks(qs, i, (jnp.zeros((T, LANES), F32), jnp.zeros((dh, T), F32)) * HG, True)

            def kstep(t, carry):
                return blocks(qs, i - 1 - t, carry, False)

            carry = lax.fori_loop(0, i, kstep, carry)
            for hh in range(HG):
                o_ref[hh, i] = carry[2 * hh + 1]
                tl_ref[hh, i] = carry[2 * hh].T[:8]
            return 0

        lax.fori_loop(0, nb, qblock, 0)

    G = H // HG
    ht = lambda part: pl.BlockSpec((HG, nb, dh, T), lambda h: (h + part * G, 0, 0, 0),
                                   pipeline_mode=pl.Buffered(1))
    return pl.pallas_call(
        body, name="sb_fwd", grid=(G,),
        in_specs=[ht(0), ht(1), ht(2)],
        out_specs=[ht(0), pl.BlockSpec((HG, nb, 8, T), lambda h: (h, 0, 0, 0))],
        out_shape=[jax.ShapeDtypeStruct((H, nb, dh, T), F32), jax.ShapeDtypeStruct((H, nb, 8, T), F32)],
        compiler_params=_cp("arbitrary"),
    )(qkv, qkv, qkv)


def _sb_bwd(qkv, dot, tl):
    H3, nb, dh, T = qkv.shape
    H = H3 // 3
    HG = SB_HEAD_GROUP

    def body(qt_ref, k_ref, v_ref, dot_ref, tl_ref, dq_ref, dk_ref, dv_ref):
        row = lax.broadcasted_iota(jnp.int32, (T, T), 0)
        col = lax.broadcasted_iota(jnp.int32, (T, T), 1)
        tri = col < row
        upto = _tri(T, "upto")
        before = _tri(T, "before")
        dk_ref[...] = jnp.zeros_like(dk_ref)
        dv_ref[...] = jnp.zeros_like(dv_ref)

        def blocks(qs, qTs, dos, doTs, kb, carry, diag):
            hs = range(HG)
            kT = [k_ref[hh, kb] for hh in hs]
            z = [_nn(qs[hh], kT[hh]) for hh in hs]
            da = [_nn(dos[hh], v_ref[hh, kb]) for hh in hs]
            res, ls = [None] * HG, [None] * HG
            for hh in hs:
                sp = _softplus(z[hh])
                if diag:
                    sp = jnp.where(tri, sp, 0.0)
                ls[hh] = z[hh] - sp
                res[hh] = _nn(sp.astype(BF16), upto)
            g, gb, ab, resg = [None] * HG, [None] * HG, [None] * HG, [None] * HG
            for hh in hs:
                a = jnp.exp(ls[hh] + (res[hh] - _wide(carry[3 * hh], T)))
                if diag:
                    a = jnp.where(tri, a, 0.0)
                ab[hh] = a.astype(BF16)
                g[hh] = a * da[hh]
                gb[hh] = g[hh].astype(BF16)
                resg[hh] = _nn(gb[hh], before)
            out = []
            for hh in hs:
                rem, pre_g, dq = carry[3 * hh:3 * hh + 3]
                dz = g[hh] - (g[hh] + (resg[hh] + _wide(pre_g, T))) * jnp.exp(ls[hh])
                if diag:
                    dz = jnp.where(tri, dz, 0.0)
                dzb = dz.astype(BF16)
                dk_ref[hh, kb] += _nn(qTs[hh], dzb)
                dv_ref[hh, kb] += _nn(doTs[hh], ab[hh])
                out.extend([rem - _lane(res[hh], T - 1), pre_g + (_lane(resg[hh], T - 1) + _lane(gb[hh].astype(F32), T - 1)),
                            dq + _nt(kT[hh], dzb)])
            return tuple(out)

        def qblock(i, _):
            qTs = [qt_ref[hh, i] for hh in range(HG)]
            doTs = [dot_ref[hh, i] for hh in range(HG)]
            qs = [_t_bf16(v) for v in qTs]
            dos = [_t_bf16(v) for v in doTs]
            carry = []
            for hh in range(HG):
                total = jnp.broadcast_to(tl_ref[hh, i][0:1], (LANES, T)).T
                carry.extend([total, jnp.zeros((T, LANES), F32), jnp.zeros((dh, T), F32)])

            def kstep(kb, carry):
                return blocks(qs, qTs, dos, doTs, kb, carry, False)

            carry = lax.fori_loop(0, i, kstep, tuple(carry))
            carry = blocks(qs, qTs, dos, doTs, i, carry, True)
            for hh in range(HG):
                dq_ref[hh, i] = carry[3 * hh + 2]
            return 0

        lax.fori_loop(0, nb, qblock, 0)

    G = H // HG
    ht = lambda part: pl.BlockSpec((HG, nb, dh, T), lambda h: (h + part * G, 0, 0, 0),
                                   pipeline_mode=pl.Buffered(1))
    return pl.pallas_call(
        body, name="sb_bwd", grid=(G,),
        in_specs=[ht(0), ht(1), ht(2), ht(0), pl.BlockSpec((HG, nb, 8, T), lambda h: (h, 0, 0, 0))],
        out_specs=[ht(0), ht(0), ht(0)],
        out_shape=[jax.ShapeDtypeStruct((H, nb, dh, T), F32)] * 3,
        compiler_params=_cp("arbitrary"),
    )(qkv, qkv, qkv, dot, tl)


def _swa_probs(zp, zc, bias, sink, first):
    T = zp.shape[0]
    row = lax.broadcasted_iota(jnp.int32, (T, T), 0)
    col = lax.broadcasted_iota(jnp.int32, (T, T), 1)
    lp = jnp.where(jnp.logical_and(col > row, jnp.logical_not(first)), zp + bias[:, :T], NEG_BIG)
    lc = jnp.where(col <= row, zc + bias[:, T:], NEG_BIG)
    m = jnp.maximum(jnp.maximum(jnp.max(lp, axis=1, keepdims=True), jnp.max(lc, axis=1, keepdims=True)), sink)
    pp = jnp.exp(lp - m)
    pc = jnp.exp(lc - m)
    ps = jnp.exp(sink - m)
    inv = 1.0 / (jnp.sum(pp, axis=1, keepdims=True) + jnp.sum(pc, axis=1, keepdims=True) + ps)
    return pp * inv, pc * inv, ps * inv


def _swa_specs(nb, dh, T, grp, Hq, Hkv, clamp):
    blk = (lambda n: jnp.minimum(n, nb - 1)) if clamp else (lambda n: n)
    q = pl.BlockSpec((grp, None, dh, T), lambda h, n: (h, blk(n), 0, 0))
    one = lambda first, back: pl.BlockSpec(
        (None, None, dh, T), lambda h, n: (first + h, jnp.maximum(blk(n) - back, 0) if back else blk(n), 0, 0))
    return q, [one(Hq, 1), one(Hq, 0), one(Hq + Hkv, 1), one(Hq + Hkv, 0)]


def _swa_fwd(qkv, bias, sinks):
    Hq, Hkv, grp = SWA_Q_HEADS, SWA_KV_HEADS, SWA_GROUP
    _, nb, dh, T = qkv.shape

    def body(sink_ref, q_ref, kp_ref, kc_ref, vp_ref, vc_ref, bias_ref, o_ref):
        hk, n = pl.program_id(0), pl.program_id(1)
        kp, kc, vp, vc = kp_ref[...], kc_ref[...], vp_ref[...], vc_ref[...]
        qs = [_t_bf16(q_ref[g]) for g in range(grp)]
        zs = [(_nn(q, kp), _nn(q, kc)) for q in qs]
        for g in range(grp):
            pp, pc, _ = _swa_probs(*zs[g], bias_ref[g], sink_ref[hk * grp + g], n == 0)
            o_ref[g] = _nt(vp, pp.astype(BF16)) + _nt(vc, pc.astype(BF16))

    q_spec, kv_specs = _swa_specs(nb, dh, T, grp, Hq, Hkv, False)
    return pl.pallas_call(
        body, name="swa_fwd", grid=(Hkv, nb),
        in_specs=[pl.BlockSpec(memory_space=pltpu.SMEM), q_spec] + kv_specs
                 + [pl.BlockSpec((grp, T, 2 * T), lambda h, n: (h, 0, 0))],
        out_specs=pl.BlockSpec((grp, None, dh, T), lambda h, n: (h, n, 0, 0)),
        out_shape=jax.ShapeDtypeStruct((Hq, nb, dh, T), F32),
        compiler_params=_cp("arbitrary", "arbitrary"),
    )(sinks, qkv, qkv, qkv, qkv, qkv, bias)


def _swa_bwd(qkv, bias, sinks, dot, ot):
    Hq, Hkv, grp = SWA_Q_HEADS, SWA_KV_HEADS, SWA_GROUP
    _, nb, dh, T = qkv.shape

    def body(sink_ref, qt_ref, kp_ref, kc_ref, vp_ref, vc_ref, bias_ref, dot_ref, ot_ref,
             dq_ref, dk_ref, dv_ref, dbias_ref, dsink_ref, ck, cv):
        hk, n = pl.program_id(0), pl.program_id(1)

        @pl.when(n == 0)
        def _():
            dbias_ref[...] = jnp.zeros_like(dbias_ref)
            dsink_ref[...] = jnp.zeros_like(dsink_ref)
            ck[...] = jnp.zeros_like(ck)
            cv[...] = jnp.zeros_like(cv)

        @pl.when(n < nb)
        def _():
            kp, kc, vp, vc = kp_ref[...], kc_ref[...], vp_ref[...], vc_ref[...]
            kprev = jnp.zeros((dh, T), F32)
            vprev = jnp.zeros((dh, T), F32)
            kcur = jnp.zeros((dh, T), F32)
            vcur = jnp.zeros((dh, T), F32)
            qTs = [qt_ref[g] for g in range(grp)]
            qs = [_t_bf16(v) for v in qTs]
            dos = [dot_ref[g].T for g in range(grp)]
            zs = [(_nn(q, kp), _nn(q, kc)) for q in qs]
            dps = [(_nn(do.astype(BF16), vp), _nn(do.astype(BF16), vc)) for do in dos]
            dls, pbs = [], []
            for g in range(grp):
                pp, pc, ps = _swa_probs(*zs[g], bias_ref[g], sink_ref[hk * grp + g], n == 0)
                delta = jnp.sum(dos[g] * ot_ref[g].T, axis=1, keepdims=True)
                dlp = pp * (dps[g][0] - delta)
                dlc = pc * (dps[g][1] - delta)
                dbias_ref[g, :, :T] += dlp
                dbias_ref[g, :, T:] += dlc
                dsink_ref[g] += -ps * delta
                dls.append((dlp.astype(BF16), dlc.astype(BF16)))
                pbs.append((pp.astype(BF16), pc.astype(BF16)))
            for g in range(grp):
                dlpb, dlcb = dls[g]
                doT = dot_ref[g].astype(BF16)
                dq_ref[g] = _nt(kp, dlpb) + _nt(kc, dlcb)
                kprev += _nn(qTs[g], dlpb)
                kcur += _nn(qTs[g], dlcb)
                vprev += _nn(doT, pbs[g][0])
                vcur += _nn(doT, pbs[g][1])
            dk_ref[...] = ck[...] + kprev
            dv_ref[...] = cv[...] + vprev
            ck[...] = kcur
            cv[...] = vcur

        @pl.when(n == nb)
        def _():
            dk_ref[...] = ck[...]
            dv_ref[...] = cv[...]

    qt_spec, kv_specs = _swa_specs(nb, dh, T, grp, Hq, Hkv, True)
    prev = pl.BlockSpec((None, None, dh, T), lambda h, n: (h, jnp.maximum(n - 1, 0), 0, 0))
    per_group = lambda a, b: pl.BlockSpec((grp, a, b), lambda h, n: (h, 0, 0))
    return pl.pallas_call(
        body, name="swa_bwd", grid=(Hkv, nb + 1),
        in_specs=[pl.BlockSpec(memory_space=pltpu.SMEM), qt_spec] + kv_specs
                 + [per_group(T, 2 * T), qt_spec, qt_spec],
        out_specs=[qt_spec, prev, prev, per_group(T, 2 * T), per_group(T, 1)],
        out_shape=[jax.ShapeDtypeStruct((Hq, nb, dh, T), F32), jax.ShapeDtypeStruct((Hkv, nb, dh, T), F32),
                   jax.ShapeDtypeStruct((Hkv, nb, dh, T), F32), jax.ShapeDtypeStruct((Hq, T, 2 * T), F32),
                   jax.ShapeDtypeStruct((Hq, T, 1), F32)],
        scratch_shapes=[pltpu.VMEM((dh, T), F32), pltpu.VMEM((dh, T), F32)],
        compiler_params=_cp("arbitrary", "arbitrary"),
    )(sinks, qkv, qkv, qkv, qkv, qkv, bias, dot, ot)


def _split3(x):
    h1 = x.astype(BF16)
    r1 = x - h1.astype(F32)
    h2 = r1.astype(BF16)
    h3 = (r1 - h2.astype(F32)).astype(BF16)
    return h1, h2, h3


def _bias_expand(rel_t, onehot):
    Hq, NB = rel_t.shape
    L = onehot.shape[1]

    def body(r_ref, oh_ref, o_ref):
        h1, h2, h3 = _split3(r_ref[...])
        oh = oh_ref[...]
        o_ref[...] = _nn(h1, oh) + _nn(h2, oh) + _nn(h3, oh)

    return pl.pallas_call(
        body, name="bias_expand", grid=(1,),
        in_specs=[pl.BlockSpec((Hq, NB), lambda i: (0, 0)), pl.BlockSpec((NB, L), lambda i: (0, 0))],
        out_specs=pl.BlockSpec((Hq, L), lambda i: (0, 0)),
        out_shape=jax.ShapeDtypeStruct((Hq, L), F32),
        compiler_params=_cp("arbitrary"),
    )(rel_t, onehot)


def _bias_reduce(dbias, onehot):
    Hq, L = dbias.shape
    NB = onehot.shape[0]

    def body(d_ref, oh_ref, o_ref):
        h1, h2, h3 = _split3(d_ref[...])
        oh = oh_ref[...]
        o_ref[...] = _nt(h1, oh) + _nt(h2, oh) + _nt(h3, oh)

    return pl.pallas_call(
        body, name="bias_reduce", grid=(1,),
        in_specs=[pl.BlockSpec((Hq, L), lambda i: (0, 0)), pl.BlockSpec((NB, L), lambda i: (0, 0))],
        out_specs=pl.BlockSpec((Hq, NB), lambda i: (0, 0)),
        out_shape=jax.ShapeDtypeStruct((Hq, NB), F32),
        compiler_params=_cp("arbitrary"),
    )(dbias, onehot)


def _adamw(w, g, m, v, name):
    R, C = w.shape
    tr = 256 if R % 256 == 0 else R
    bc1 = 1.0 - ADAM_B1 ** ADAM_STEP
    bc2 = 1.0 - ADAM_B2 ** ADAM_STEP

    def body(w_ref, g_ref, m_ref, v_ref, d_ref, nm_ref, nv_ref):
        g = g_ref[...]
        m2 = ADAM_B1 * m_ref[...] + (1.0 - ADAM_B1) * g
        v2 = ADAM_B2 * v_ref[...] + (1.0 - ADAM_B2) * (g * g)
        nm_ref[...] = m2
        nv_ref[...] = v2
        d_ref[...] = -ADAM_LR * ((m2 / bc1) / (jnp.sqrt(v2 / bc2) + ADAM_EPS) + ADAM_WD * w_ref[...])

    spec = pl.BlockSpec((tr, C), lambda i: (i, 0))
    return pl.pallas_call(
        body, name=name, grid=(R // tr,),
        in_specs=[spec] * 4, out_specs=[spec] * 3,
        out_shape=[jax.ShapeDtypeStruct((R, C), F32)] * 3,
        compiler_params=_cp("arbitrary"),
    )(w, g, m, v)


def _add_halves(mine, recv, name):
    K, R, C = mine.shape
    tr = 416 if R % 416 == 0 else R

    def body(a_ref, b_ref, o_ref, ob_ref):
        s = a_ref[...].astype(F32) + b_ref[...].astype(F32)
        o_ref[...] = s
        ob_ref[...] = s.astype(BF16)

    spec = pl.BlockSpec((None, tr, C), lambda k, i: (k, i, 0))
    return pl.pallas_call(
        body, name=name, grid=(K, R // tr),
        in_specs=[spec, spec], out_specs=[spec, spec],
        out_shape=[jax.ShapeDtypeStruct((K, R, C), F32), jax.ShapeDtypeStruct((K, R, C), BF16)],
        compiler_params=_cp("arbitrary", "arbitrary"),
    )(mine, recv)


def _add_received(own, recv, name):
    R, C = own.shape
    tr = 416 if R % 416 == 0 else R

    def body(a_ref, r_ref, o_ref):
        o_ref[...] = ((a_ref[...] + r_ref[0].astype(F32)) + r_ref[1].astype(F32)) + r_ref[2].astype(F32)

    return pl.pallas_call(
        body, name=name, grid=(R // tr,),
        in_specs=[pl.BlockSpec((tr, C), lambda i: (i, 0)), pl.BlockSpec((3, tr, C), lambda i: (0, i, 0))],
        out_specs=pl.BlockSpec((tr, C), lambda i: (i, 0)),
        out_shape=jax.ShapeDtypeStruct((R, C), F32),
        compiler_params=_cp("arbitrary"),
    )(own, recv)


def _position():
    x, y, c = lax.axis_index("x"), lax.axis_index("y"), lax.axis_index("c")
    others = [(1 - x, y), (x, 1 - y), (1 - x, 1 - y)]
    return x, y, c, others


def _remote(src, dst, send_sems, recv_sems, k, dev):
    return pltpu.make_async_remote_copy(src_ref=src, dst_ref=dst, send_sem=send_sems.at[k],
                                        recv_sem=recv_sems.at[k], device_id=dev, device_id_type=MESH_ID)


def _gather_weights(shard):
    R, C = shard.shape
    half = R // 2

    def body(src, out, send_sems, recv_sems):
        x, y, c, others = _position()
        mine = 2 * x + y
        sends = [_remote(src.at[c], out.at[mine, c], send_sems, recv_sems, j, (ox, oy, c))
                 for j, (ox, oy) in enumerate(others)]
        for cp in sends:
            cp.start()
        passed = []
        for j, (ox, oy) in enumerate(others):
            slot = out.at[2 * ox + oy, c]
            _remote(slot, slot, send_sems, recv_sems, j, (ox, oy, c)).wait_recv()
            fwd = _remote(slot, slot, send_sems, recv_sems, 3 + j, (x, y, 1 - c))
            fwd.start()
            passed.append(fwd)
        for j, (ox, oy) in enumerate(others):
            slot = out.at[2 * ox + oy, 1 - c]
            _remote(slot, slot, send_sems, recv_sems, 3 + j, (x, y, 1 - c)).wait_recv()
        for cp in sends + passed:
            cp.wait_send()

    return pl.pallas_call(
        body, name="gather_weights",
        in_specs=[ANY], out_specs=ANY,
        out_shape=jax.ShapeDtypeStruct((N_CHIPS, 2, half, C), shard.dtype),
        scratch_shapes=[pltpu.SemaphoreType.DMA((6,)), pltpu.SemaphoreType.DMA((6,))],
    )(shard.reshape(2, half, C)).reshape(N_CHIPS, R, C)


def _swap_halves(grads):
    K, R, C = grads.shape
    half = R // 2

    def body(src, out, send_sems, recv_sems):
        x, y, c, _ = _position()
        theirs = src.at[:, pl.ds(pl.multiple_of((1 - c) * half, 16), half), :]
        cp = _remote(theirs, out, send_sems, recv_sems, 0, (x, y, 1 - c))
        cp.start()
        cp.wait()

    return pl.pallas_call(
        body, name="swap_halves",
        in_specs=[ANY], out_specs=ANY,
        out_shape=jax.ShapeDtypeStruct((K, half, C), grads.dtype),
        scratch_shapes=[pltpu.SemaphoreType.DMA((1,)), pltpu.SemaphoreType.DMA((1,))],
    )(grads)


def _scatter_to_owners(parts):
    K, H, C = parts.shape

    def body(src, out, send_sems, recv_sems):
        x, y, c, others = _position()
        sends = [_remote(src.at[2 * ox + oy], out.at[j], send_sems, recv_sems, j, (ox, oy, c))
                 for j, (ox, oy) in enumerate(others)]
        for cp in sends:
            cp.start()
        for cp in sends:
            cp.wait()

    return pl.pallas_call(
        body, name="scatter_to_owners",
        in_specs=[ANY], out_specs=ANY,
        out_shape=jax.ShapeDtypeStruct((3, H, C), parts.dtype),
        scratch_shapes=[pltpu.SemaphoreType.DMA((3,)), pltpu.SemaphoreType.DMA((3,))],
    )(parts)


def _swap_reduced(half_rows):
    H, C = half_rows.shape

    def body(src, out, send_sems, recv_sems):
        x, y, c, _ = _position()
        cp = _remote(src, out, send_sems, recv_sems, 0, (x, y, 1 - c))
        cp.start()
        cp.wait()

    return pl.pallas_call(
        body, name="swap_reduced",
        in_specs=[ANY], out_specs=ANY,
        out_shape=jax.ShapeDtypeStruct((H, C), half_rows.dtype),
        scratch_shapes=[pltpu.SemaphoreType.DMA((1,)), pltpu.SemaphoreType.DMA((1,))],
    )(half_rows)


def _allreduce_small(block):
    R, C = block.shape
    n_dev = 8

    def body(src, out, slots, send_sems, recv_sems):
        x, y, c, _ = _position()
        me = 4 * x + 2 * y + c
        slots[me] = src[...]
        sends = []
        for r in range(1, n_dev):
            peer = (x ^ (r >> 2), y ^ ((r >> 1) & 1), c ^ (r & 1))
            cp = _remote(src, slots.at[me], send_sems, recv_sems, r - 1, peer)
            cp.start()
            sends.append(cp)
        for r in range(1, n_dev):
            theirs = slots.at[me ^ r]
            _remote(theirs, theirs, send_sems, recv_sems, r - 1, (x, y, c)).wait_recv()
        for cp in sends:
            cp.wait_send()
        acc = slots[0]
        for d in range(1, n_dev):
            acc = acc + slots[d]
        out[...] = acc

    return pl.pallas_call(
        body, name="allreduce_small",
        in_specs=[pl.BlockSpec(memory_space=pltpu.VMEM)], out_specs=pl.BlockSpec(memory_space=pltpu.VMEM),
        out_shape=jax.ShapeDtypeStruct((R, C), F32),
        scratch_shapes=[pltpu.VMEM((n_dev, R, C), F32), pltpu.SemaphoreType.DMA((7,)), pltpu.SemaphoreType.DMA((7,))],
    )(block)


def _rel_bucket(dist):
    max_exact = REL_BUCKETS // 2
    d = jnp.maximum(dist, 1).astype(F32)
    large = max_exact + (jnp.log(d / max_exact) / math.log(REL_MAX_DIST / max_exact)
                         * (REL_BUCKETS - max_exact)).astype(jnp.int32)
    large = jnp.minimum(large, REL_BUCKETS - 1)
    return jnp.where(dist < max_exact, dist, large)


def _bucket_onehot():
    T = SWA_BLOCK
    dist = (jnp.arange(T)[:, None] + T) - jnp.arange(2 * T)[None, :]
    bucket = _rel_bucket(jnp.maximum(dist, 0)).reshape(1, T * 2 * T)
    return (bucket == jnp.arange(REL_BUCKETS)[:, None]).astype(BF16)


_BUF = (("ffn1_w1", "t"), ("ffn1_w3", "t"), ("ffn1_w2", "n"), ("ffn2_w1", "t"), ("ffn2_w3", "t"),
        ("ffn2_w2", "n"), ("w_in", "t"), ("w_out", "n"), ("w_branch_swa", "tw"), ("w_branch_sb", "tw"))


def _to_rows(name_kind, w, D):
    kind = name_kind[1]
    if kind == "n":
        return w
    if kind == "t":
        return w.T
    return w.T.reshape(-1, D)


def _from_rows(name_kind, rows, width):
    kind = name_kind[1]
    if kind == "n":
        return rows
    if kind == "t":
        return rows.T
    return rows.reshape(-1, width).T


def kernel(x, norm_ffn1, ffn1_w1, ffn1_w3, ffn1_w2, norm_mix, w_in, swa_sinks, rel_bias, w_branch_swa, w_branch_sb, w_out, norm_ffn2, ffn2_w1, ffn2_w3, ffn2_w2, norm_final, loss_target, m_norm_ffn1, m_ffn1_w1, m_ffn1_w3, m_ffn1_w2, m_norm_mix, m_w_in, m_swa_sinks, m_rel_bias, m_w_branch_swa, m_w_branch_sb, m_w_out, m_norm_ffn2, m_ffn2_w1, m_ffn2_w3, m_ffn2_w2, m_norm_final, v_norm_ffn1, v_ffn1_w1, v_ffn1_w3, v_ffn1_w2, v_norm_mix, v_w_in, v_swa_sinks, v_rel_bias, v_w_branch_swa, v_w_branch_sb, v_w_out, v_norm_ffn2, v_ffn2_w1, v_ffn2_w3, v_ffn2_w2, v_norm_final):
    names = ["norm_ffn1", "ffn1_w1", "ffn1_w3", "ffn1_w2", "norm_mix", "w_in", "swa_sinks", "rel_bias",
             "w_branch_swa", "w_branch_sb", "w_out", "norm_ffn2", "ffn2_w1", "ffn2_w3", "ffn2_w2", "norm_final"]
    W = dict(zip(names, [norm_ffn1, ffn1_w1, ffn1_w3, ffn1_w2, norm_mix, w_in, swa_sinks, rel_bias,
                         w_branch_swa, w_branch_sb, w_out, norm_ffn2, ffn2_w1, ffn2_w3, ffn2_w2, norm_final]))
    M = dict(zip(names, [m_norm_ffn1, m_ffn1_w1, m_ffn1_w3, m_ffn1_w2, m_norm_mix, m_w_in, m_swa_sinks, m_rel_bias,
                         m_w_branch_swa, m_w_branch_sb, m_w_out, m_norm_ffn2, m_ffn2_w1, m_ffn2_w3, m_ffn2_w2,
                         m_norm_final]))
    V = dict(zip(names, [v_norm_ffn1, v_ffn1_w1, v_ffn1_w3, v_ffn1_w2, v_norm_mix, v_w_in, v_swa_sinks, v_rel_bias,
                         v_w_branch_swa, v_w_branch_sb, v_w_out, v_norm_ffn2, v_ffn2_w1, v_ffn2_w3, v_ffn2_w2,
                         v_norm_final]))
    xs = x[0]
    target = loss_target[0]
    S, D = xs.shape
    QW = SWA_Q_HEADS * HEAD_DIM
    KW = SWA_KV_HEADS * HEAD_DIM
    BW = SB_HEADS * HEAD_DIM
    QKV = QW + 2 * KW + 3 * BW

    pieces = [_to_rows(nk, W[nk[0]][0], D) for nk in _BUF]
    sizes = [p.shape[0] for p in pieces]
    offs = [0]
    for s in sizes:
        offs.append(offs[-1] + s)
    shard = jnp.concatenate(pieces, axis=0).astype(BF16)
    chip = 2 * lax.axis_index("x") + lax.axis_index("y")
    gathered = lax.dynamic_update_slice(_gather_weights(shard), shard[None], (chip, 0, 0))

    def full(i):
        return gathered[:, offs[i]:offs[i + 1], :].reshape(N_CHIPS * sizes[i], D)

    f1w1, f1w3, f1w2, f2w1, f2w3, f2w2, w_in_t, w_out_f = [full(i) for i in range(8)]
    wa_t = full(8).reshape(D, QW)
    wb_t = full(9).reshape(D, BW)

    g1, gmix, g3 = W["norm_ffn1"], W["norm_mix"], W["norm_ffn2"]
    gf = W["norm_final"].reshape(1, D)

    x1, h1, a1, b1 = _ffn_fwd(xs, g1, f1w1, f1w3, f1w2, "ffn1_fwd")
    o0 = QW + 2 * KW
    rows = jnp.arange(w_in_t.shape[0])
    is_q = (rows < QW) | ((rows >= o0) & (rows < o0 + BW))
    w_in_s = w_in_t * jnp.where(is_q, QK_SCALE, 1.0).astype(BF16)[:, None]
    qkv_a = _norm_proj_heads(x1, gmix, w_in_s[:o0], SWA_BLOCK, "proj_swa")
    qkv_b = _norm_proj_heads(x1, gmix, w_in_s[o0:QKV], SB_BLOCK, "proj_sb")
    gates, h2 = _norm_matmul_nt(x1, gmix, w_in_t[QKV:], F32, "proj_gates")

    onehot = _bucket_onehot()
    bias = _bias_expand(W["rel_bias"].T, onehot).reshape(SWA_Q_HEADS, SWA_BLOCK, 2 * SWA_BLOCK)
    sinks = W["swa_sinks"].reshape(SWA_Q_HEADS)
    oa_t = _swa_fwd(qkv_a, bias, sinks)
    ob_t, totals = _sb_fwd(qkv_b)

    x2, merged, ba, bb = _merge_fwd(x1, gates, oa_t, ob_t, wa_t, wb_t, w_out_f)
    x3, h3, a2, b2 = _ffn_fwd(x2, g3, f2w1, f2w3, f2w2, "ffn2_fwd")
    loss_part, dx3, dgf = _final_loss(x3, gf, target)

    dx2, dg3, dz2, da2, db2, u2 = _ffn_bwd(dx3, x2, g3, a2, b2, f2w1, f2w3, f2w2, "ffn2_bwd")
    grads = {}
    grads["ffn2_w1"] = _tn_matmul(da2, h3, "ffn2_dw1")
    grads["ffn2_w3"] = _tn_matmul(db2, h3, "ffn2_dw3")
    grads["ffn2_w2"] = _tn_matmul(u2, dz2, "ffn2_dw2")

    dx2b, dba, dbb, dgates, doa_t, dob_t = _merge_bwd(dx2, gates, ba, bb, wa_t.T, wb_t.T, w_out_f,
                                                      SWA_BLOCK, SB_BLOCK)
    grads["w_out"] = _tn_matmul(merged, dx2b, "dw_out")
    grads["w_branch_swa"] = _heads_matmul(oa_t, dba, 1.0, "dw_branch_swa").T
    grads["w_branch_sb"] = _heads_matmul(ob_t, dbb, 1.0, "dw_branch_sb").T

    dqb_t, dkb_t, dvb_t = _sb_bwd(qkv_b, dob_t, totals)
    dqa_t, dka_t, dva_t, dbias, dsink_rows = _swa_bwd(qkv_a, bias, sinks, doa_t, oa_t)
    d_rel = _bias_reduce(dbias.reshape(SWA_Q_HEADS, -1), onehot).T
    d_sinks = jnp.sum(dsink_rows, axis=(1, 2))

    dheads = [(dqa_t, QK_SCALE, "q_swa"), (dka_t, 1.0, "k_swa"), (dva_t, 1.0, "v_swa"),
              (dqb_t, QK_SCALE, "q_sb"), (dkb_t, 1.0, "k_sb"), (dvb_t, 1.0, "v_sb")]
    grads["w_in"] = jnp.concatenate([_heads_matmul(a, h2, sc, "dw_in_" + nm).astype(BF16) for a, sc, nm in dheads]
                                    + [_tn_matmul(dgates, h2, "dw_in_gates")], axis=0)
    row0, pieces_in = 0, []
    for a, _, _ in dheads:
        pieces_in.append((a, row0))
        row0 += a.shape[0] * HEAD_DIM
    dx1, dgmix = _proj_bwd(pieces_in, dgates, w_in_s, x1, gmix, dx2)

    dx0, dg1, dz1, da1, db1, u1 = _ffn_bwd(dx1, xs, g1, a1, b1, f1w1, f1w3, f1w2, "ffn1_bwd")
    grads["ffn1_w1"] = _tn_matmul(da1, h1, "ffn1_dw1")
    grads["ffn1_w3"] = _tn_matmul(db1, h1, "ffn1_dw3")
    grads["ffn1_w2"] = _tn_matmul(u1, dz1, "ffn1_dw2")

    gparts = [grads[nk[0]].astype(BF16).reshape(N_CHIPS, sizes[i], D) for i, nk in enumerate(_BUF)]
    gbuf = jnp.concatenate(gparts, axis=1)
    R = gbuf.shape[1]
    half = R // 2
    c = lax.axis_index("c")
    mine = 2 * lax.axis_index("x") + lax.axis_index("y")
    from_sibling = _swap_halves(gbuf)
    my_half = lax.dynamic_slice_in_dim(gbuf, c * half, half, axis=1)
    chip_sum, chip_sum16 = _add_halves(my_half, from_sibling, "add_sibling")
    received = _scatter_to_owners(chip_sum16)
    own = lax.dynamic_index_in_dim(chip_sum, mine, axis=0, keepdims=False)
    my_rows = _add_received(own, received, "add_chips")
    their_rows = _swap_reduced(my_rows)
    reduced = jnp.concatenate([jnp.where(c == 0, my_rows, their_rows), jnp.where(c == 0, their_rows, my_rows)], axis=0)

    small_rows = [dg1, dgmix, dg3, dgf,
                  jnp.pad(d_sinks.reshape(1, -1), ((0, 0), (0, D - SWA_Q_HEADS))),
                  jnp.pad(d_rel.reshape(1, -1), ((0, 0), (0, D - REL_BUCKETS * SWA_Q_HEADS))),
                  jnp.pad(loss_part, ((0, 0), (0, D - 1))), jnp.zeros((1, D), F32)]
    small = _allreduce_small(jnp.concatenate(small_rows, axis=0))
    loss = small[6, 0]

    G = {}
    for i, nk in enumerate(_BUF):
        G[nk[0]] = _from_rows(nk, reduced[offs[i]:offs[i + 1]], W[nk[0]].shape[1])[None]
    G["norm_ffn1"], G["norm_mix"], G["norm_ffn2"] = small[0:1], small[1:2], small[2:3]
    G["norm_final"] = small[3]
    G["swa_sinks"] = small[4:5, :SWA_Q_HEADS]
    G["rel_bias"] = small[5, :REL_BUCKETS * SWA_Q_HEADS].reshape(REL_BUCKETS, SWA_Q_HEADS)

    delta, new_m, new_v = {}, {}, {}
    small_names = ["norm_ffn1", "norm_mix", "norm_ffn2", "norm_final", "swa_sinks", "rel_bias"]

    def pack(d):
        return jnp.concatenate([jnp.pad(d[n].reshape(1, -1), ((0, 0), (0, D - d[n].size))) for n in small_names]
                               + [jnp.zeros((2, D), F32)], axis=0)

    sd, sm, sv = _adamw(pack(W), pack(G), pack(M), pack(V), "adamw_small")
    for r, n in enumerate(small_names):
        for dst, src in ((delta, sd), (new_m, sm), (new_v, sv)):
            dst[n] = src[r, :W[n].size].reshape(W[n].shape)
    for nk in _BUF:
        n = nk[0]
        shp = W[n].shape
        two_d = (shp[1], shp[2])
        d_, m_, v_ = _adamw(W[n].reshape(two_d), G[n].reshape(two_d), M[n].reshape(two_d), V[n].reshape(two_d),
                            "adamw_" + n)
        delta[n], new_m[n], new_v[n] = d_.reshape(shp), m_.reshape(shp), v_.reshape(shp)

    return (loss, dx0[None], *[G[n] for n in names], *[delta[n] for n in names],
            *[new_m[n] for n in names], *[new_v[n] for n in names])
```

```python
import functools
import math

import jax
import jax.numpy as jnp
from jax import lax
from jax.experimental import pallas as pl
from jax.experimental.pallas import tpu as pltpu

F32, BF16 = jnp.float32, jnp.bfloat16
MESH_ID = pl.DeviceIdType.MESH
ANY = pl.BlockSpec(memory_space=pl.ANY)

RMS_EPS = 1e-6
HEAD_DIM = 64
SWA_Q_HEADS, SWA_KV_HEADS, SWA_GROUP = 8, 2, 4
SWA_BLOCK = 128
SB_HEADS = 8
SB_BLOCK = 256
REL_BUCKETS, REL_MAX_DIST = 32, 128
NEG_BIG = -1e30
QK_SCALE = HEAD_DIM ** -0.5
ADAM_LR, ADAM_B1, ADAM_B2, ADAM_EPS, ADAM_WD, ADAM_STEP = 0.001, 0.9, 0.999, 1e-08, 0.01, 10

N_CHIPS = 4
TOKEN_TILE = 512
MATMUL_TOKEN_TILE = 1024
WGRAD_ROW_TILES = (2176, 1408, 1024, 256)
FF_TILE = 256
VMEM_LIMIT = 56 * 1024 * 1024


def _cp(*sem):
    return pltpu.CompilerParams(dimension_semantics=sem, vmem_limit_bytes=VMEM_LIMIT)


def _nn(a, b):
    return jnp.dot(a, b, preferred_element_type=F32)


def _nt(a, b):
    return lax.dot_general(a, b, (((1,), (1,)), ((), ())), preferred_element_type=F32)


def _tn(a, b):
    return lax.dot_general(a, b, (((0,), (0,)), ((), ())), preferred_element_type=F32)


def _norm_fwd(x, g):
    return x * lax.rsqrt(jnp.mean(x * x, axis=-1, keepdims=True) + RMS_EPS) * g


def _norm_bwd(x, g, dh):
    r = lax.rsqrt(jnp.mean(x * x, axis=-1, keepdims=True) + RMS_EPS)
    xh = x * r
    dxh = dh * g
    dx = r * (dxh - xh * jnp.mean(dxh * xh, axis=-1, keepdims=True))
    return dx, jnp.sum(dh * xh, axis=0, keepdims=True)


def _softplus(z):
    return jnp.maximum(z, 0.0) + jnp.log(1.0 + jnp.exp(-jnp.abs(z)))


def _ffn_fwd(x, g, w1t, w3t, w2, name):
    S, D = x.shape
    F = w2.shape[0]
    tm, tf = min(MATMUL_TOKEN_TILE, S), FF_TILE
    nj = F // tf

    def body(x_ref, g_ref, w1_ref, w3_ref, w2_ref, xo_ref, h_ref, a_ref, b_ref, hs, acc):
        j = pl.program_id(1)

        @pl.when(j == 0)
        def _():
            hb = _norm_fwd(x_ref[...], g_ref[...]).astype(BF16)
            hs[...] = hb
            h_ref[...] = hb
            acc[...] = jnp.zeros_like(acc)

        h = hs[...]
        a = _nt(h, w1_ref[...])
        b = _nt(h, w3_ref[...])
        a_ref[...] = a.astype(BF16)
        b_ref[...] = b.astype(BF16)
        u = a * jax.nn.sigmoid(a) * b
        acc[...] += _nn(u.astype(BF16), w2_ref[...])

        @pl.when(j == nj - 1)
        def _():
            xo_ref[...] = x_ref[...] + 0.5 * acc[...]

    return pl.pallas_call(
        body, name=name, grid=(S // tm, nj),
        in_specs=[pl.BlockSpec((tm, D), lambda i, j: (i, 0)),
                  pl.BlockSpec((1, D), lambda i, j: (0, 0)),
                  pl.BlockSpec((tf, D), lambda i, j: (j, 0)),
                  pl.BlockSpec((tf, D), lambda i, j: (j, 0)),
                  pl.BlockSpec((tf, D), lambda i, j: (j, 0))],
        out_specs=[pl.BlockSpec((tm, D), lambda i, j: (i, 0)),
                   pl.BlockSpec((tm, D), lambda i, j: (i, 0)),
                   pl.BlockSpec((tm, tf), lambda i, j: (i, j)),
                   pl.BlockSpec((tm, tf), lambda i, j: (i, j))],
        out_shape=[jax.ShapeDtypeStruct((S, D), F32), jax.ShapeDtypeStruct((S, D), BF16),
                   jax.ShapeDtypeStruct((S, F), BF16), jax.ShapeDtypeStruct((S, F), BF16)],
        scratch_shapes=[pltpu.VMEM((tm, D), BF16), pltpu.VMEM((tm, D), F32)],
        compiler_params=_cp("arbitrary", "arbitrary"),
    )(x, g, w1t, w3t, w2)


def _ffn_bwd(dxo, x, g, a, b, w1t, w3t, w2, name):
    S, D = x.shape
    F = w2.shape[0]
    tm, tf = min(MATMUL_TOKEN_TILE, S), FF_TILE
    ni, nj = S // tm, F // tf

    def body(dxo_ref, x_ref, g_ref, a_ref, b_ref, w1_ref, w3_ref, w2_ref,
             dx_ref, dg_ref, dz_ref, da_ref, db_ref, u_ref, dzs, acc):
        i, j = pl.program_id(0), pl.program_id(1)

        @pl.when(j == 0)
        def _():
            dzb = (0.5 * dxo_ref[...]).astype(BF16)
            dzs[...] = dzb
            dz_ref[...] = dzb
            acc[...] = jnp.zeros_like(acc)

        du = _nt(dzs[...], w2_ref[...])
        av = a_ref[...].astype(F32)
        bv = b_ref[...].astype(F32)
        s = jax.nn.sigmoid(av)
        silu = av * s
        db = (du * silu).astype(BF16)
        da = (du * bv * (s * (1.0 + av * (1.0 - s)))).astype(BF16)
        da_ref[...] = da
        db_ref[...] = db
        u_ref[...] = (silu * bv).astype(BF16)
        acc[...] += _nn(da, w1_ref[...]) + _nn(db, w3_ref[...])

        @pl.when(j == nj - 1)
        def _():
            dx, dg = _norm_bwd(x_ref[...], g_ref[...], acc[...])
            dx_ref[...] = dxo_ref[...] + dx

            @pl.when(i == 0)
            def _():
                dg_ref[...] = dg

            @pl.when(i > 0)
            def _():
                dg_ref[...] += dg

    row = pl.BlockSpec((tm, D), lambda i, j: (i, 0))
    wsp = pl.BlockSpec((tf, D), lambda i, j: (j, 0))
    col = pl.BlockSpec((tm, tf), lambda i, j: (i, j))
    vec = pl.BlockSpec((1, D), lambda i, j: (0, 0))
    return pl.pallas_call(
        body, name=name, grid=(ni, nj),
        in_specs=[row, row, vec, col, col, wsp, wsp, wsp],
        out_specs=[row, vec, row, col, col, col],
        out_shape=[jax.ShapeDtypeStruct((S, D), F32), jax.ShapeDtypeStruct((1, D), F32),
                   jax.ShapeDtypeStruct((S, D), BF16), jax.ShapeDtypeStruct((S, F), BF16),
                   jax.ShapeDtypeStruct((S, F), BF16), jax.ShapeDtypeStruct((S, F), BF16)],
        scratch_shapes=[pltpu.VMEM((tm, D), BF16), pltpu.VMEM((tm, D), F32)],
        compiler_params=_cp("arbitrary", "arbitrary"),
    )(dxo, x, g, a, b, w1t, w3t, w2)


def _tn_matmul(a, b, name):
    S, M = a.shape
    N = b.shape[1]
    ts = min(MATMUL_TOKEN_TILE, S)
    tmm = next(t for t in WGRAD_ROW_TILES if M % t == 0)
    ns = S // ts

    def body(a_ref, b_ref, o_ref, acc):
        s = pl.program_id(1)
        part = _tn(a_ref[...], b_ref[...])

        @pl.when(s == 0)
        def _():
            acc[...] = part

        @pl.when(s > 0)
        def _():
            acc[...] += part

        @pl.when(s == ns - 1)
        def _():
            o_ref[...] = acc[...].astype(BF16)

    return pl.pallas_call(
        body, name=name, grid=(M // tmm, ns),
        in_specs=[pl.BlockSpec((ts, tmm), lambda m, s: (s, m)),
                  pl.BlockSpec((ts, N), lambda m, s: (s, 0))],
        out_specs=pl.BlockSpec((tmm, N), lambda m, s: (m, 0)),
        out_shape=jax.ShapeDtypeStruct((M, N), BF16),
        scratch_shapes=[pltpu.VMEM((tmm, N), F32)],
        compiler_params=_cp("arbitrary", "arbitrary"),
    )(a, b)


def _norm_matmul_nt(x, g, wt, out_dtype, name):
    S, D = x.shape
    N = wt.shape[0]
    tm = min(MATMUL_TOKEN_TILE, S)
    tn = next(t for t in (1024, 768, 256) if N % t == 0)

    def body(x_ref, g_ref, w_ref, o_ref, h_ref, hs):
        @pl.when(pl.program_id(1) == 0)
        def _():
            hb = _norm_fwd(x_ref[...], g_ref[...]).astype(BF16)
            hs[...] = hb
            h_ref[...] = hb

        o_ref[...] = _nt(hs[...], w_ref[...]).astype(out_dtype)

    return pl.pallas_call(
        body, name=name, grid=(S // tm, N // tn),
        in_specs=[pl.BlockSpec((tm, D), lambda i, j: (i, 0)),
                  pl.BlockSpec((1, D), lambda i, j: (0, 0)),
                  pl.BlockSpec((tn, D), lambda i, j: (j, 0))],
        out_specs=[pl.BlockSpec((tm, tn), lambda i, j: (i, j)),
                   pl.BlockSpec((tm, D), lambda i, j: (i, 0))],
        out_shape=[jax.ShapeDtypeStruct((S, N), out_dtype), jax.ShapeDtypeStruct((S, D), BF16)],
        scratch_shapes=[pltpu.VMEM((tm, D), BF16)],
        compiler_params=_cp("arbitrary", "arbitrary"),
    )(x, g, wt)


def _heads_tile(ref):
    Hh, nbk = ref.shape[0], ref.shape[1]
    return jnp.concatenate([jnp.concatenate([ref[h, b] for b in range(nbk)], axis=1) for h in range(Hh)], axis=0)


def _store_heads(ref, val):
    Hh, nbk, dh, T = ref.shape
    for h in range(Hh):
        for b in range(nbk):
            ref[h, b] = val[h * dh:(h + 1) * dh, b * T:(b + 1) * T].astype(ref.dtype)


def _norm_proj_heads(x, g, w_rows, T, name):
    S, D = x.shape
    N = w_rows.shape[0]
    tm, tn = min(MATMUL_TOKEN_TILE, S), 768

    def body(x_ref, g_ref, w_ref, o_ref, hs):
        @pl.when(pl.program_id(1) == 0)
        def _():
            hs[...] = _norm_fwd(x_ref[...], g_ref[...]).astype(BF16)

        _store_heads(o_ref, _nt(w_ref[...], hs[...]))

    return pl.pallas_call(
        body, name=name, grid=(S // tm, N // tn),
        in_specs=[pl.BlockSpec((tm, D), lambda i, j: (i, 0)),
                  pl.BlockSpec((1, D), lambda i, j: (0, 0)),
                  pl.BlockSpec((tn, D), lambda i, j: (j, 0))],
        out_specs=pl.BlockSpec((tn // HEAD_DIM, tm // T, HEAD_DIM, T), lambda i, j: (j, i, 0, 0)),
        out_shape=jax.ShapeDtypeStruct((N // HEAD_DIM, S // T, HEAD_DIM, T), BF16),
        scratch_shapes=[pltpu.VMEM((tm, D), BF16)],
        compiler_params=_cp("arbitrary", "arbitrary"),
    )(x, g, w_rows)


def _heads_matmul(at, b, scale, name):
    Hh, nb, dh, T = at.shape
    S, N = b.shape
    ts = min(MATMUL_TOKEN_TILE, S)
    ns = S // ts

    def body(a_ref, b_ref, o_ref):
        s = pl.program_id(0)
        a = _heads_tile(a_ref)
        part = _nn((a if scale == 1.0 else a * scale).astype(BF16), b_ref[...])

        @pl.when(s == 0)
        def _():
            o_ref[...] = part

        @pl.when(s > 0)
        def _():
            o_ref[...] += part

    return pl.pallas_call(
        body, name=name, grid=(ns,),
        in_specs=[pl.BlockSpec((Hh, ts // T, dh, T), lambda s: (0, s, 0, 0)),
                  pl.BlockSpec((ts, N), lambda s: (s, 0))],
        out_specs=pl.BlockSpec((Hh * dh, N), lambda s: (0, 0)),
        out_shape=jax.ShapeDtypeStruct((Hh * dh, N), F32),
        compiler_params=_cp("arbitrary"),
    )(at, b)


def _proj_bwd(pieces, dgates, w_rows, x, g, dres):
    S, D = x.shape
    tm = min(TOKEN_TILE, S)
    n_p = len(pieces)
    gate_row = w_rows.shape[0] - dgates.shape[1]

    def body(*refs):
        p_refs = refs[:n_p]
        dgt_ref, w_ref, x_ref, g_ref, dres_ref, dx_ref, dg_ref = refs[n_p:]
        i = pl.program_id(0)
        dh = _nn(dgt_ref[...], w_ref[gate_row:, :])
        for p_ref, (arr, row0) in zip(p_refs, pieces):
            rows = arr.shape[0] * arr.shape[2]
            dh += _tn(_heads_tile(p_ref).astype(BF16), w_ref[row0:row0 + rows, :])
        dx, dg = _norm_bwd(x_ref[...], g_ref[...], dh)
        dx_ref[...] = dres_ref[...] + dx

        @pl.when(i == 0)
        def _():
            dg_ref[...] = dg

        @pl.when(i > 0)
        def _():
            dg_ref[...] += dg

    row = pl.BlockSpec((tm, D), lambda i: (i, 0))
    vec = pl.BlockSpec((1, D), lambda i: (0, 0))
    p_specs = [pl.BlockSpec((a.shape[0], tm // a.shape[3], a.shape[2], a.shape[3]), lambda i: (0, i, 0, 0))
               for a, _ in pieces]
    return pl.pallas_call(
        body, name="proj_bwd", grid=(S // tm,),
        in_specs=p_specs + [pl.BlockSpec((tm, dgates.shape[1]), lambda i: (i, 0)),
                            pl.BlockSpec(w_rows.shape, lambda i: (0, 0), pipeline_mode=pl.Buffered(1)),
                            row, vec, row],
        out_specs=[row, vec],
        out_shape=[jax.ShapeDtypeStruct((S, D), F32), jax.ShapeDtypeStruct((1, D), F32)],
        compiler_params=_cp("arbitrary"),
    )(*[a for a, _ in pieces], dgates, w_rows, x, g, dres)


def _merge_fwd(x1, gates, oa_t, ob_t, wat, wbt, w_out):
    S, D = x1.shape
    W = wat.shape[1]
    tm = min(TOKEN_TILE, S)

    def body(x_ref, ga_ref, gb_ref, oa_ref, ob_ref, wa_ref, wb_ref, wo_ref,
             x2_ref, mg_ref, ba_ref, bb_ref):
        ba = _nt(_heads_tile(oa_ref).T.astype(BF16), wa_ref[...])
        bb = _nt(_heads_tile(ob_ref).T.astype(BF16), wb_ref[...])
        merged = jax.nn.sigmoid(ga_ref[...]) * ba + jax.nn.sigmoid(gb_ref[...]) * bb
        mb = merged.astype(BF16)
        mg_ref[...] = mb
        ba_ref[...] = ba.astype(BF16)
        bb_ref[...] = bb.astype(BF16)
        x2_ref[...] = x_ref[...] + _nn(mb, wo_ref[...])

    row = pl.BlockSpec((tm, D), lambda i: (i, 0))
    full = lambda r, c: pl.BlockSpec((r, c), lambda i: (0, 0))
    heads = lambda a: pl.BlockSpec((a.shape[0], tm // a.shape[3], a.shape[2], a.shape[3]), lambda i: (0, i, 0, 0))
    return pl.pallas_call(
        body, name="merge_fwd", grid=(S // tm,),
        in_specs=[row, pl.BlockSpec((tm, D), lambda i: (i, 0)), pl.BlockSpec((tm, D), lambda i: (i, 1)),
                  heads(oa_t), heads(ob_t), full(D, W), full(D, W), full(D, D)],
        out_specs=[row, row, row, row],
        out_shape=[jax.ShapeDtypeStruct((S, D), F32)] + [jax.ShapeDtypeStruct((S, D), BF16)] * 3,
        compiler_params=_cp("arbitrary"),
    )(x1, gates, gates, oa_t, ob_t, wat, wbt, w_out)


def _merge_bwd(dx2, gates, ba, bb, wa, wb, w_out, t_a, t_b):
    S, D = dx2.shape
    W = wa.shape[0]
    tm = min(TOKEN_TILE, S)
    Hh = W // HEAD_DIM

    def body(dx_ref, ga_ref, gb_ref, ba_ref, bb_ref, wa_ref, wb_ref, wo_ref,
             dxb_ref, dba_ref, dbb_ref, dgt_ref, doa_ref, dob_ref):
        dxb = dx_ref[...].astype(BF16)
        dxb_ref[...] = dxb
        dm = _nt(dxb, wo_ref[...])
        sa = jax.nn.sigmoid(ga_ref[...])
        sb = jax.nn.sigmoid(gb_ref[...])
        dba = (dm * sa).astype(BF16)
        dbb = (dm * sb).astype(BF16)
        dba_ref[...] = dba
        dbb_ref[...] = dbb
        dgt_ref[:, :D] = (dm * ba_ref[...].astype(F32) * sa * (1.0 - sa)).astype(BF16)
        dgt_ref[:, D:] = (dm * bb_ref[...].astype(F32) * sb * (1.0 - sb)).astype(BF16)
        _store_heads(doa_ref, _nt(wa_ref[...], dba))
        _store_heads(dob_ref, _nt(wb_ref[...], dbb))

    row = pl.BlockSpec((tm, D), lambda i: (i, 0))
    full = lambda r, c: pl.BlockSpec((r, c), lambda i: (0, 0))
    heads = lambda T: pl.BlockSpec((Hh, tm // T, HEAD_DIM, T), lambda i: (0, i, 0, 0))
    return pl.pallas_call(
        body, name="merge_bwd", grid=(S // tm,),
        in_specs=[row, pl.BlockSpec((tm, D), lambda i: (i, 0)), pl.BlockSpec((tm, D), lambda i: (i, 1)),
                  row, row, full(W, D), full(W, D), full(D, D)],
        out_specs=[row, row, row, pl.BlockSpec((tm, 2 * D), lambda i: (i, 0)), heads(t_a), heads(t_b)],
        out_shape=[jax.ShapeDtypeStruct((S, D), BF16)] * 3 + [jax.ShapeDtypeStruct((S, 2 * D), BF16),
                   jax.ShapeDtypeStruct((Hh, S // t_a, HEAD_DIM, t_a), F32),
                   jax.ShapeDtypeStruct((Hh, S // t_b, HEAD_DIM, t_b), BF16)],
        compiler_params=_cp("arbitrary"),
    )(dx2, gates, gates, ba, bb, wa, wb, w_out)


def _final_loss(x3, gf, target):
    S, D = x3.shape
    tm = min(TOKEN_TILE, S)

    def body(x_ref, g_ref, t_ref, loss_ref, dx_ref, dg_ref):
        i = pl.program_id(0)
        x = x_ref[...]
        g = g_ref[...]
        e = _norm_fwd(x, g) - t_ref[...]
        part = 0.5 * jnp.sum(jnp.mean(e * e, axis=-1, keepdims=True), axis=0, keepdims=True)
        dx, dg = _norm_bwd(x, g, e * (1.0 / D))
        dx_ref[...] = dx

        @pl.when(i == 0)
        def _():
            loss_ref[...] = part
            dg_ref[...] = dg

        @pl.when(i > 0)
        def _():
            loss_ref[...] += part
            dg_ref[...] += dg

    row = pl.BlockSpec((tm, D), lambda i: (i, 0))
    vec = pl.BlockSpec((1, D), lambda i: (0, 0))
    return pl.pallas_call(
        body, name="final_loss", grid=(S // tm,),
        in_specs=[row, vec, row],
        out_specs=[pl.BlockSpec((1, 1), lambda i: (0, 0)), row, vec],
        out_shape=[jax.ShapeDtypeStruct((1, 1), F32), jax.ShapeDtypeStruct((S, D), F32),
                   jax.ShapeDtypeStruct((1, D), F32)],
        compiler_params=_cp("arbitrary"),
    )(x3, gf, target)


SB_FWD_HEAD_GROUP = 8
SB_HEAD_GROUP = 4
LANES = 128


def _tri(T, kind):
    r = lax.broadcasted_iota(jnp.int32, (T, T), 0)
    c = lax.broadcasted_iota(jnp.int32, (T, T), 1)
    return {"after": r > c, "upto": r <= c, "before": r < c}[kind].astype(BF16)


def _lane(v, j):
    return jnp.broadcast_to(v[:, j:j + 1], (v.shape[0], LANES))


def _t_bf16(x):
    return x.astype(F32).T.astype(BF16)


def _wide(v, T):
    return jnp.tile(v, (1, T // LANES))


def _sb_pair(i, kb):
    return (i * (i + 1)) // 2 + kb


def _sb_fwd(qkv):
    H3, nb, dh, T = qkv.shape
    H = H3 // 3
    HG = SB_FWD_HEAD_GROUP
    assert HG == H, "one head group: a saved tile holds all the heads"
    n_pairs = (nb * (nb + 1)) // 2

    def body(q_ref, k_ref, v_ref, o_ref, saved_ref, stage, sems):
        row = lax.broadcasted_iota(jnp.int32, (T, T), 0)
        col = lax.broadcasted_iota(jnp.int32, (T, T), 1)
        tri = col < row
        after = _tri(T, "after")

        def save(slot, pair):
            return pltpu.make_async_copy(stage.at[slot], saved_ref.at[pair], sems.at[slot])

        def blocks(qs, i, kb, step, carry, diag):
            hs = range(HG)
            slot = step % 2

            @pl.when(step >= 2)
            def _():
                save(slot, 0).wait()

            z = [_nn(qs[hh], k_ref[hh, kb]) for hh in hs]
            res, ls, first = [None] * HG, [None] * HG, [None] * HG
            for hh in hs:
                sp = _softplus(z[hh])
                if diag:
                    sp = jnp.where(tri, sp, 0.0)
                ls[hh] = z[hh] - sp
                spb = sp.astype(BF16)
                first[hh] = _lane(spb.astype(F32), 0)
                res[hh] = _nn(spb, after)
            out = []
            for hh in hs:
                c, oacc = carry[2 * hh], carry[2 * hh + 1]
                a = jnp.exp(ls[hh] - (res[hh] + _wide(c, T)))
                if diag:
                    a = jnp.where(tri, a, 0.0)
                ab = a.astype(BF16)
                stage[slot, hh, 0] = ab
                stage[slot, hh, 1] = jnp.exp(ls[hh]).astype(BF16)
                out.extend([c + (first[hh] + _lane(res[hh], 0)), oacc + _nt(v_ref[hh, kb], ab)])
            save(slot, _sb_pair(i, kb)).start()
            return tuple(out)

        def qblock(i, step):
            qs = [_t_bf16(q_ref[hh, i]) for hh in range(HG)]
            carry = blocks(qs, i, i, step, (jnp.zeros((T, LANES), F32), jnp.zeros((dh, T), F32)) * HG, True)

            def kstep(t, carry):
                return blocks(qs, i, i - 1 - t, step + 1 + t, carry, False)

            carry = lax.fori_loop(0, i, kstep, carry)
            for hh in range(HG):
                o_ref[hh, i] = carry[2 * hh + 1]
            return step + 1 + i

        lax.fori_loop(0, nb, qblock, 0)
        for slot in range(min(2, n_pairs)):
            save(slot, 0).wait()

    ht = lambda part: pl.BlockSpec((HG, nb, dh, T), lambda h: (part, 0, 0, 0), pipeline_mode=pl.Buffered(1))
    return pl.pallas_call(
        body, name="sb_fwd", grid=(1,),
        in_specs=[ht(0), ht(1), ht(2)],
        out_specs=[ht(0), ANY],
        out_shape=[jax.ShapeDtypeStruct((H, nb, dh, T), F32),
                   jax.ShapeDtypeStruct((n_pairs, H, 2, T, T), BF16)],
        scratch_shapes=[pltpu.VMEM((2, HG, 2, T, T), BF16), pltpu.SemaphoreType.DMA((2,))],
        compiler_params=_cp("arbitrary"),
    )(qkv, qkv, qkv)


def _sb_bwd(qkv, dot, saved):
    H3, nb, dh, T = qkv.shape
    H = H3 // 3
    HG = SB_HEAD_GROUP
    n_pairs = (nb * (nb + 1)) // 2

    def body(qt_ref, k_ref, v_ref, dot_ref, saved_ref, dq_ref, dk_ref, dv_ref, stage, sems):
        head0 = pl.program_id(0) * HG
        row = lax.broadcasted_iota(jnp.int32, (T, T), 0)
        col = lax.broadcasted_iota(jnp.int32, (T, T), 1)
        tri = col < row
        before = _tri(T, "before")
        dk_ref[...] = jnp.zeros_like(dk_ref)
        dv_ref[...] = jnp.zeros_like(dv_ref)

        def fetch(slot, pair):
            return pltpu.make_async_copy(saved_ref.at[pair, pl.ds(head0, HG)], stage.at[slot], sems.at[slot])

        fetch(0, 0).start()

        def blocks(qTs, dos, doTs, i, kb, carry, diag):
            hs = range(HG)
            pair = _sb_pair(i, kb)
            slot = pair % 2
            fetch(slot, pair).wait()

            @pl.when(pair + 1 < n_pairs)
            def _():
                fetch(1 - slot, pair + 1).start()

            kT = [k_ref[hh, kb] for hh in hs]
            da = [_nn(dos[hh], v_ref[hh, kb]) for hh in hs]
            g, gb, resg = [None] * HG, [None] * HG, [None] * HG
            for hh in hs:
                g[hh] = stage[slot, hh, 0].astype(F32) * da[hh]
                gb[hh] = g[hh].astype(BF16)
                resg[hh] = _nn(gb[hh], before)
            out = []
            for hh in hs:
                pre_g, dq = carry[2 * hh], carry[2 * hh + 1]
                dz = g[hh] - (g[hh] + (resg[hh] + _wide(pre_g, T))) * stage[slot, hh, 1].astype(F32)
                if diag:
                    dz = jnp.where(tri, dz, 0.0)
                dzb = dz.astype(BF16)
                dk_ref[hh, kb] += _nn(qTs[hh], dzb)
                dv_ref[hh, kb] += _nn(doTs[hh], stage[slot, hh, 0])
                out.extend([pre_g + (_lane(resg[hh], T - 1) + _lane(gb[hh].astype(F32), T - 1)),
                            dq + _nt(kT[hh], dzb)])
            return tuple(out)

        def qblock(i, _):
            qTs = [qt_ref[hh, i] for hh in range(HG)]
            doTs = [dot_ref[hh, i] for hh in range(HG)]
            dos = [_t_bf16(v) for v in doTs]
            carry = (jnp.zeros((T, LANES), F32), jnp.zeros((dh, T), F32)) * HG

            def kstep(kb, carry):
                return blocks(qTs, dos, doTs, i, kb, carry, False)

            carry = lax.fori_loop(0, i, kstep, carry)
            carry = blocks(qTs, dos, doTs, i, i, carry, True)
            for hh in range(HG):
                dq_ref[hh, i] = carry[2 * hh + 1]
            return 0

        lax.fori_loop(0, nb, qblock, 0)

    G = H // HG
    ht = lambda part: pl.BlockSpec((HG, nb, dh, T), lambda h: (h + part * G, 0, 0, 0),
                                   pipeline_mode=pl.Buffered(1))
    return pl.pallas_call(
        body, name="sb_bwd", grid=(G,),
        in_specs=[ht(0), ht(1), ht(2), ht(0), ANY],
        out_specs=[ht(0), ht(0), ht(0)],
        out_shape=[jax.ShapeDtypeStruct((H, nb, dh, T), F32)] * 3,
        scratch_shapes=[pltpu.VMEM((2, HG, 2, T, T), BF16), pltpu.SemaphoreType.DMA((2,))],
        compiler_params=_cp("arbitrary"),
    )(qkv, qkv, qkv, dot, saved)


def _swa_probs(zp, zc, bias, sink, first):
    T = zp.shape[0]
    row = lax.broadcasted_iota(jnp.int32, (T, T), 0)
    col = lax.broadcasted_iota(jnp.int32, (T, T), 1)
    lp = jnp.where(jnp.logical_and(col > row, jnp.logical_not(first)), zp + bias[:, :T], NEG_BIG)
    lc = jnp.where(col <= row, zc + bias[:, T:], NEG_BIG)
    m = jnp.maximum(jnp.maximum(jnp.max(lp, axis=1, keepdims=True), jnp.max(lc, axis=1, keepdims=True)), sink)
    pp = jnp.exp(lp - m)
    pc = jnp.exp(lc - m)
    ps = jnp.exp(sink - m)
    inv = 1.0 / (jnp.sum(pp, axis=1, keepdims=True) + jnp.sum(pc, axis=1, keepdims=True) + ps)
    return pp * inv, pc * inv, ps * inv


def _swa_specs(nb, dh, T, grp, Hq, Hkv, clamp):
    blk = (lambda n: jnp.minimum(n, nb - 1)) if clamp else (lambda n: n)
    q = pl.BlockSpec((grp, None, dh, T), lambda h, n: (h, blk(n), 0, 0))
    one = lambda first, back: pl.BlockSpec(
        (None, None, dh, T), lambda h, n: (first + h, jnp.maximum(blk(n) - back, 0) if back else blk(n), 0, 0))
    return q, [one(Hq, 1), one(Hq, 0), one(Hq + Hkv, 1), one(Hq + Hkv, 0)]


def _swa_fwd(qkv, bias, sinks):
    Hq, Hkv, grp = SWA_Q_HEADS, SWA_KV_HEADS, SWA_GROUP
    _, nb, dh, T = qkv.shape

    def body(sink_ref, q_ref, kp_ref, kc_ref, vp_ref, vc_ref, bias_ref, o_ref):
        hk, n = pl.program_id(0), pl.program_id(1)
        kp, kc, vp, vc = kp_ref[...], kc_ref[...], vp_ref[...], vc_ref[...]
        qs = [_t_bf16(q_ref[g]) for g in range(grp)]
        zs = [(_nn(q, kp), _nn(q, kc)) for q in qs]
        for g in range(grp):
            pp, pc, _ = _swa_probs(*zs[g], bias_ref[g], sink_ref[hk * grp + g], n == 0)
            o_ref[g] = _nt(vp, pp.astype(BF16)) + _nt(vc, pc.astype(BF16))

    q_spec, kv_specs = _swa_specs(nb, dh, T, grp, Hq, Hkv, False)
    return pl.pallas_call(
        body, name="swa_fwd", grid=(Hkv, nb),
        in_specs=[pl.BlockSpec(memory_space=pltpu.SMEM), q_spec] + kv_specs
                 + [pl.BlockSpec((grp, T, 2 * T), lambda h, n: (h, 0, 0))],
        out_specs=pl.BlockSpec((grp, None, dh, T), lambda h, n: (h, n, 0, 0)),
        out_shape=jax.ShapeDtypeStruct((Hq, nb, dh, T), F32),
        compiler_params=_cp("arbitrary", "arbitrary"),
    )(sinks, qkv, qkv, qkv, qkv, qkv, bias)


def _swa_bwd(qkv, bias, sinks, dot, ot):
    Hq, Hkv, grp = SWA_Q_HEADS, SWA_KV_HEADS, SWA_GROUP
    _, nb, dh, T = qkv.shape

    def body(sink_ref, qt_ref, kp_ref, kc_ref, vp_ref, vc_ref, bias_ref, dot_ref, ot_ref,
             dq_ref, dk_ref, dv_ref, dbias_ref, dsink_ref, ck, cv):
        hk, n = pl.program_id(0), pl.program_id(1)

        @pl.when(n == 0)
        def _():
            dbias_ref[...] = jnp.zeros_like(dbias_ref)
            dsink_ref[...] = jnp.zeros_like(dsink_ref)
            ck[...] = jnp.zeros_like(ck)
            cv[...] = jnp.zeros_like(cv)

        @pl.when(n < nb)
        def _():
            kp, kc, vp, vc = kp_ref[...], kc_ref[...], vp_ref[...], vc_ref[...]
            kprev = jnp.zeros((dh, T), F32)
            vprev = jnp.zeros((dh, T), F32)
            kcur = jnp.zeros((dh, T), F32)
            vcur = jnp.zeros((dh, T), F32)
            qTs = [qt_ref[g] for g in range(grp)]
            qs = [_t_bf16(v) for v in qTs]
            dos = [dot_ref[g].T for g in range(grp)]
            zs = [(_nn(q, kp), _nn(q, kc)) for q in qs]
            dps = [(_nn(do.astype(BF16), vp), _nn(do.astype(BF16), vc)) for do in dos]
            dls, pbs = [], []
            for g in range(grp):
                pp, pc, ps = _swa_probs(*zs[g], bias_ref[g], sink_ref[hk * grp + g], n == 0)
                delta = jnp.sum(dos[g] * ot_ref[g].T, axis=1, keepdims=True)
                dlp = pp * (dps[g][0] - delta)
                dlc = pc * (dps[g][1] - delta)
                dbias_ref[g, :, :T] += dlp
                dbias_ref[g, :, T:] += dlc
                dsink_ref[g] += -ps * delta
                dls.append((dlp.astype(BF16), dlc.astype(BF16)))
                pbs.append((pp.astype(BF16), pc.astype(BF16)))
            for g in range(grp):
                dlpb, dlcb = dls[g]
                doT = dot_ref[g].astype(BF16)
                dq_ref[g] = _nt(kp, dlpb) + _nt(kc, dlcb)
                kprev += _nn(qTs[g], dlpb)
                kcur += _nn(qTs[g], dlcb)
                vprev += _nn(doT, pbs[g][0])
                vcur += _nn(doT, pbs[g][1])
            dk_ref[...] = ck[...] + kprev
            dv_ref[...] = cv[...] + vprev
            ck[...] = kcur
            cv[...] = vcur

        @pl.when(n == nb)
        def _():
            dk_ref[...] = ck[...]
            dv_ref[...] = cv[...]

    qt_spec, kv_specs = _swa_specs(nb, dh, T, grp, Hq, Hkv, True)
    prev = pl.BlockSpec((None, None, dh, T), lambda h, n: (h, jnp.maximum(n - 1, 0), 0, 0))
    per_group = lambda a, b: pl.BlockSpec((grp, a, b), lambda h, n: (h, 0, 0))
    return pl.pallas_call(
        body, name="swa_bwd", grid=(Hkv, nb + 1),
        in_specs=[pl.BlockSpec(memory_space=pltpu.SMEM), qt_spec] + kv_specs
                 + [per_group(T, 2 * T), qt_spec, qt_spec],
        out_specs=[qt_spec, prev, prev, per_group(T, 2 * T), per_group(T, 1)],
        out_shape=[jax.ShapeDtypeStruct((Hq, nb, dh, T), F32), jax.ShapeDtypeStruct((Hkv, nb, dh, T), F32),
                   jax.ShapeDtypeStruct((Hkv, nb, dh, T), F32), jax.ShapeDtypeStruct((Hq, T, 2 * T), F32),
                   jax.ShapeDtypeStruct((Hq, T, 1), F32)],
        scratch_shapes=[pltpu.VMEM((dh, T), F32), pltpu.VMEM((dh, T), F32)],
        compiler_params=_cp("arbitrary", "arbitrary"),
    )(sinks, qkv, qkv, qkv, qkv, qkv, bias, dot, ot)


def _split3(x):
    h1 = x.astype(BF16)
    r1 = x - h1.astype(F32)
    h2 = r1.astype(BF16)
    h3 = (r1 - h2.astype(F32)).astype(BF16)
    return h1, h2, h3


def _bias_expand(rel_t, onehot):
    Hq, NB = rel_t.shape
    L = onehot.shape[1]

    def body(r_ref, oh_ref, o_ref):
        h1, h2, h3 = _split3(r_ref[...])
        oh = oh_ref[...]
        o_ref[...] = _nn(h1, oh) + _nn(h2, oh) + _nn(h3, oh)

    return pl.pallas_call(
        body, name="bias_expand", grid=(1,),
        in_specs=[pl.BlockSpec((Hq, NB), lambda i: (0, 0)), pl.BlockSpec((NB, L), lambda i: (0, 0))],
        out_specs=pl.BlockSpec((Hq, L), lambda i: (0, 0)),
        out_shape=jax.ShapeDtypeStruct((Hq, L), F32),
        compiler_params=_cp("arbitrary"),
    )(rel_t, onehot)


def _bias_reduce(dbias, onehot):
    Hq, L = dbias.shape
    NB = onehot.shape[0]

    def body(d_ref, oh_ref, o_ref):
        h1, h2, h3 = _split3(d_ref[...])
        oh = oh_ref[...]
        o_ref[...] = _nt(h1, oh) + _nt(h2, oh) + _nt(h3, oh)

    return pl.pallas_call(
        body, name="bias_reduce", grid=(1,),
        in_specs=[pl.BlockSpec((Hq, L), lambda i: (0, 0)), pl.BlockSpec((NB, L), lambda i: (0, 0))],
        out_specs=pl.BlockSpec((Hq, NB), lambda i: (0, 0)),
        out_shape=jax.ShapeDtypeStruct((Hq, NB), F32),
        compiler_params=_cp("arbitrary"),
    )(dbias, onehot)


def _adamw(w, g, m, v, name):
    R, C = w.shape
    tr = 256 if R % 256 == 0 else R
    bc1 = 1.0 - ADAM_B1 ** ADAM_STEP
    bc2 = 1.0 - ADAM_B2 ** ADAM_STEP

    def body(w_ref, g_ref, m_ref, v_ref, d_ref, nm_ref, nv_ref):
        g = g_ref[...]
        m2 = ADAM_B1 * m_ref[...] + (1.0 - ADAM_B1) * g
        v2 = ADAM_B2 * v_ref[...] + (1.0 - ADAM_B2) * (g * g)
        nm_ref[...] = m2
        nv_ref[...] = v2
        d_ref[...] = -ADAM_LR * ((m2 / bc1) / (jnp.sqrt(v2 / bc2) + ADAM_EPS) + ADAM_WD * w_ref[...])

    spec = pl.BlockSpec((tr, C), lambda i: (i, 0))
    return pl.pallas_call(
        body, name=name, grid=(R // tr,),
        in_specs=[spec] * 4, out_specs=[spec] * 3,
        out_shape=[jax.ShapeDtypeStruct((R, C), F32)] * 3,
        compiler_params=_cp("arbitrary"),
    )(w, g, m, v)


def _add_halves(mine, recv, name):
    K, R, C = mine.shape
    tr = 416 if R % 416 == 0 else R

    def body(a_ref, b_ref, o_ref, ob_ref):
        s = a_ref[...].astype(F32) + b_ref[...].astype(F32)
        o_ref[...] = s
        ob_ref[...] = s.astype(BF16)

    spec = pl.BlockSpec((None, tr, C), lambda k, i: (k, i, 0))
    return pl.pallas_call(
        body, name=name, grid=(K, R // tr),
        in_specs=[spec, spec], out_specs=[spec, spec],
        out_shape=[jax.ShapeDtypeStruct((K, R, C), F32), jax.ShapeDtypeStruct((K, R, C), BF16)],
        compiler_params=_cp("arbitrary", "arbitrary"),
    )(mine, recv)


def _add_received(own, recv, name):
    R, C = own.shape
    tr = 416 if R % 416 == 0 else R

    def body(a_ref, r_ref, o_ref):
        o_ref[...] = ((a_ref[...] + r_ref[0].astype(F32)) + r_ref[1].astype(F32)) + r_ref[2].astype(F32)

    return pl.pallas_call(
        body, name=name, grid=(R // tr,),
        in_specs=[pl.BlockSpec((tr, C), lambda i: (i, 0)), pl.BlockSpec((3, tr, C), lambda i: (0, i, 0))],
        out_specs=pl.BlockSpec((tr, C), lambda i: (i, 0)),
        out_shape=jax.ShapeDtypeStruct((R, C), F32),
        compiler_params=_cp("arbitrary"),
    )(own, recv)


def _position():
    x, y, c = lax.axis_index("x"), lax.axis_index("y"), lax.axis_index("c")
    others = [(1 - x, y), (x, 1 - y), (1 - x, 1 - y)]
    return x, y, c, others


def _remote(src, dst, send_sems, recv_sems, k, dev):
    return pltpu.make_async_remote_copy(src_ref=src, dst_ref=dst, send_sem=send_sems.at[k],
                                        recv_sem=recv_sems.at[k], device_id=dev, device_id_type=MESH_ID)


def _gather_weights(shard):
    R, C = shard.shape
    half = R // 2

    def body(src, out, send_sems, recv_sems):
        x, y, c, others = _position()
        mine = 2 * x + y
        sends = [_remote(src.at[c], out.at[mine, c], send_sems, recv_sems, j, (ox, oy, c))
                 for j, (ox, oy) in enumerate(others)]
        for cp in sends:
            cp.start()
        passed = []
        for j, (ox, oy) in enumerate(others):
            slot = out.at[2 * ox + oy, c]
            _remote(slot, slot, send_sems, recv_sems, j, (ox, oy, c)).wait_recv()
            fwd = _remote(slot, slot, send_sems, recv_sems, 3 + j, (x, y, 1 - c))
            fwd.start()
            passed.append(fwd)
        for j, (ox, oy) in enumerate(others):
            slot = out.at[2 * ox + oy, 1 - c]
            _remote(slot, slot, send_sems, recv_sems, 3 + j, (x, y, 1 - c)).wait_recv()
        for cp in sends + passed:
            cp.wait_send()

    return pl.pallas_call(
        body, name="gather_weights",
        in_specs=[ANY], out_specs=ANY,
        out_shape=jax.ShapeDtypeStruct((N_CHIPS, 2, half, C), shard.dtype),
        scratch_shapes=[pltpu.SemaphoreType.DMA((6,)), pltpu.SemaphoreType.DMA((6,))],
    )(shard.reshape(2, half, C)).reshape(N_CHIPS, R, C)


def _swap_halves(grads):
    K, R, C = grads.shape
    half = R // 2

    def body(src, out, send_sems, recv_sems):
        x, y, c, _ = _position()
        theirs = src.at[:, pl.ds(pl.multiple_of((1 - c) * half, 16), half), :]
        cp = _remote(theirs, out, send_sems, recv_sems, 0, (x, y, 1 - c))
        cp.start()
        cp.wait()

    return pl.pallas_call(
        body, name="swap_halves",
        in_specs=[ANY], out_specs=ANY,
        out_shape=jax.ShapeDtypeStruct((K, half, C), grads.dtype),
        scratch_shapes=[pltpu.SemaphoreType.DMA((1,)), pltpu.SemaphoreType.DMA((1,))],
    )(grads)


def _scatter_to_owners(parts):
    K, H, C = parts.shape

    def body(src, out, send_sems, recv_sems):
        x, y, c, others = _position()
        sends = [_remote(src.at[2 * ox + oy], out.at[j], send_sems, recv_sems, j, (ox, oy, c))
                 for j, (ox, oy) in enumerate(others)]
        for cp in sends:
            cp.start()
        for cp in sends:
            cp.wait()

    return pl.pallas_call(
        body, name="scatter_to_owners",
        in_specs=[ANY], out_specs=ANY,
        out_shape=jax.ShapeDtypeStruct((3, H, C), parts.dtype),
        scratch_shapes=[pltpu.SemaphoreType.DMA((3,)), pltpu.SemaphoreType.DMA((3,))],
    )(parts)


def _swap_reduced(half_rows):
    H, C = half_rows.shape

    def body(src, out, send_sems, recv_sems):
        x, y, c, _ = _position()
        cp = _remote(src, out, send_sems, recv_sems, 0, (x, y, 1 - c))
        cp.start()
        cp.wait()

    return pl.pallas_call(
        body, name="swap_reduced",
        in_specs=[ANY], out_specs=ANY,
        out_shape=jax.ShapeDtypeStruct((H, C), half_rows.dtype),
        scratch_shapes=[pltpu.SemaphoreType.DMA((1,)), pltpu.SemaphoreType.DMA((1,))],
    )(half_rows)


def _allreduce_small(block):
    R, C = block.shape
    n_dev = 8

    def body(src, out, slots, send_sems, recv_sems):
        x, y, c, _ = _position()
        me = 4 * x + 2 * y + c
        slots[me] = src[...]
        sends = []
        for r in range(1, n_dev):
            peer = (x ^ (r >> 2), y ^ ((r >> 1) & 1), c ^ (r & 1))
            cp = _remote(src, slots.at[me], send_sems, recv_sems, r - 1, peer)
            cp.start()
            sends.append(cp)
        for r in range(1, n_dev):
            theirs = slots.at[me ^ r]
            _remote(theirs, theirs, send_sems, recv_sems, r - 1, (x, y, c)).wait_recv()
        for cp in sends:
            cp.wait_send()
        acc = slots[0]
        for d in range(1, n_dev):
            acc = acc + slots[d]
        out[...] = acc

    return pl.pallas_call(
        body, name="allreduce_small",
        in_specs=[pl.BlockSpec(memory_space=pltpu.VMEM)], out_specs=pl.BlockSpec(memory_space=pltpu.VMEM),
        out_shape=jax.ShapeDtypeStruct((R, C), F32),
        scratch_shapes=[pltpu.VMEM((n_dev, R, C), F32), pltpu.SemaphoreType.DMA((7,)), pltpu.SemaphoreType.DMA((7,))],
    )(block)


def _rel_bucket(dist):
    max_exact = REL_BUCKETS // 2
    d = jnp.maximum(dist, 1).astype(F32)
    large = max_exact + (jnp.log(d / max_exact) / math.log(REL_MAX_DIST / max_exact)
                         * (REL_BUCKETS - max_exact)).astype(jnp.int32)
    large = jnp.minimum(large, REL_BUCKETS - 1)
    return jnp.where(dist < max_exact, dist, large)


def _bucket_onehot():
    T = SWA_BLOCK
    dist = (jnp.arange(T)[:, None] + T) - jnp.arange(2 * T)[None, :]
    bucket = _rel_bucket(jnp.maximum(dist, 0)).reshape(1, T * 2 * T)
    return (bucket == jnp.arange(REL_BUCKETS)[:, None]).astype(BF16)


_BUF = (("ffn1_w1", "t"), ("ffn1_w3", "t"), ("ffn1_w2", "n"), ("ffn2_w1", "t"), ("ffn2_w3", "t"),
        ("ffn2_w2", "n"), ("w_in", "t"), ("w_out", "n"), ("w_branch_swa", "tw"), ("w_branch_sb", "tw"))


def _to_rows(name_kind, w, D):
    kind = name_kind[1]
    if kind == "n":
        return w
    if kind == "t":
        return w.T
    return w.T.reshape(-1, D)


def _from_rows(name_kind, rows, width):
    kind = name_kind[1]
    if kind == "n":
        return rows
    if kind == "t":
        return rows.T
    return rows.reshape(-1, width).T


def kernel(x, norm_ffn1, ffn1_w1, ffn1_w3, ffn1_w2, norm_mix, w_in, swa_sinks, rel_bias, w_branch_swa, w_branch_sb, w_out, norm_ffn2, ffn2_w1, ffn2_w3, ffn2_w2, norm_final, loss_target, m_norm_ffn1, m_ffn1_w1, m_ffn1_w3, m_ffn1_w2, m_norm_mix, m_w_in, m_swa_sinks, m_rel_bias, m_w_branch_swa, m_w_branch_sb, m_w_out, m_norm_ffn2, m_ffn2_w1, m_ffn2_w3, m_ffn2_w2, m_norm_final, v_norm_ffn1, v_ffn1_w1, v_ffn1_w3, v_ffn1_w2, v_norm_mix, v_w_in, v_swa_sinks, v_rel_bias, v_w_branch_swa, v_w_branch_sb, v_w_out, v_norm_ffn2, v_ffn2_w1, v_ffn2_w3, v_ffn2_w2, v_norm_final):
    names = ["norm_ffn1", "ffn1_w1", "ffn1_w3", "ffn1_w2", "norm_mix", "w_in", "swa_sinks", "rel_bias",
             "w_branch_swa", "w_branch_sb", "w_out", "norm_ffn2", "ffn2_w1", "ffn2_w3", "ffn2_w2", "norm_final"]
    W = dict(zip(names, [norm_ffn1, ffn1_w1, ffn1_w3, ffn1_w2, norm_mix, w_in, swa_sinks, rel_bias,
                         w_branch_swa, w_branch_sb, w_out, norm_ffn2, ffn2_w1, ffn2_w3, ffn2_w2, norm_final]))
    M = dict(zip(names, [m_norm_ffn1, m_ffn1_w1, m_ffn1_w3, m_ffn1_w2, m_norm_mix, m_w_in, m_swa_sinks, m_rel_bias,
                         m_w_branch_swa, m_w_branch_sb, m_w_out, m_norm_ffn2, m_ffn2_w1, m_ffn2_w3, m_ffn2_w2,
                         m_norm_final]))
    V = dict(zip(names, [v_norm_ffn1, v_ffn1_w1, v_ffn1_w3, v_ffn1_w2, v_norm_mix, v_w_in, v_swa_sinks, v_rel_bias,
                         v_w_branch_swa, v_w_branch_sb, v_w_out, v_norm_ffn2, v_ffn2_w1, v_ffn2_w3, v_ffn2_w2,
                         v_norm_final]))
    xs = x[0]
    target = loss_target[0]
    S, D = xs.shape
    QW = SWA_Q_HEADS * HEAD_DIM
    KW = SWA_KV_HEADS * HEAD_DIM
    BW = SB_HEADS * HEAD_DIM
    QKV = QW + 2 * KW + 3 * BW

    pieces = [_to_rows(nk, W[nk[0]][0], D) for nk in _BUF]
    sizes = [p.shape[0] for p in pieces]
    offs = [0]
    for s in sizes:
        offs.append(offs[-1] + s)
    shard = jnp.concatenate(pieces, axis=0).astype(BF16)
    chip = 2 * lax.axis_index("x") + lax.axis_index("y")
    gathered = lax.dynamic_update_slice(_gather_weights(shard), shard[None], (chip, 0, 0))

    def full(i):
        return gathered[:, offs[i]:offs[i + 1], :].reshape(N_CHIPS * sizes[i], D)

    f1w1, f1w3, f1w2, f2w1, f2w3, f2w2, w_in_t, w_out_f = [full(i) for i in range(8)]
    wa_t = full(8).reshape(D, QW)
    wb_t = full(9).reshape(D, BW)

    g1, gmix, g3 = W["norm_ffn1"], W["norm_mix"], W["norm_ffn2"]
    gf = W["norm_final"].reshape(1, D)

    x1, h1, a1, b1 = _ffn_fwd(xs, g1, f1w1, f1w3, f1w2, "ffn1_fwd")
    o0 = QW + 2 * KW
    rows = jnp.arange(w_in_t.shape[0])
    is_q = (rows < QW) | ((rows >= o0) & (rows < o0 + BW))
    w_in_s = w_in_t * jnp.where(is_q, QK_SCALE, 1.0).astype(BF16)[:, None]
    qkv_a = _norm_proj_heads(x1, gmix, w_in_s[:o0], SWA_BLOCK, "proj_swa")
    qkv_b = _norm_proj_heads(x1, gmix, w_in_s[o0:QKV], SB_BLOCK, "proj_sb")
    gates, h2 = _norm_matmul_nt(x1, gmix, w_in_t[QKV:], F32, "proj_gates")

    onehot = _bucket_onehot()
    bias = _bias_expand(W["rel_bias"].T, onehot).reshape(SWA_Q_HEADS, SWA_BLOCK, 2 * SWA_BLOCK)
    sinks = W["swa_sinks"].reshape(SWA_Q_HEADS)
    oa_t = _swa_fwd(qkv_a, bias, sinks)
    ob_t, saved_sb = _sb_fwd(qkv_b)

    x2, merged, ba, bb = _merge_fwd(x1, gates, oa_t, ob_t, wa_t, wb_t, w_out_f)
    x3, h3, a2, b2 = _ffn_fwd(x2, g3, f2w1, f2w3, f2w2, "ffn2_fwd")
    loss_part, dx3, dgf = _final_loss(x3, gf, target)

    dx2, dg3, dz2, da2, db2, u2 = _ffn_bwd(dx3, x2, g3, a2, b2, f2w1, f2w3, f2w2, "ffn2_bwd")
    grads = {}
    grads["ffn2_w1"] = _tn_matmul(da2, h3, "ffn2_dw1")
    grads["ffn2_w3"] = _tn_matmul(db2, h3, "ffn2_dw3")
    grads["ffn2_w2"] = _tn_matmul(u2, dz2, "ffn2_dw2")

    dx2b, dba, dbb, dgates, doa_t, dob_t = _merge_bwd(dx2, gates, ba, bb, wa_t.T, wb_t.T, w_out_f,
                                                      SWA_BLOCK, SB_BLOCK)
    grads["w_out"] = _tn_matmul(merged, dx2b, "dw_out")
    grads["w_branch_swa"] = _heads_matmul(oa_t, dba, 1.0, "dw_branch_swa").T
    grads["w_branch_sb"] = _heads_matmul(ob_t, dbb, 1.0, "dw_branch_sb").T

    dqb_t, dkb_t, dvb_t = _sb_bwd(qkv_b, dob_t, saved_sb)
    dqa_t, dka_t, dva_t, dbias, dsink_rows = _swa_bwd(qkv_a, bias, sinks, doa_t, oa_t)
    d_rel = _bias_reduce(dbias.reshape(SWA_Q_HEADS, -1), onehot).T
    d_sinks = jnp.sum(dsink_rows, axis=(1, 2))

    dheads = [(dqa_t, QK_SCALE, "q_swa"), (dka_t, 1.0, "k_swa"), (dva_t, 1.0, "v_swa"),
              (dqb_t, QK_SCALE, "q_sb"), (dkb_t, 1.0, "k_sb"), (dvb_t, 1.0, "v_sb")]
    grads["w_in"] = jnp.concatenate([_heads_matmul(a, h2, sc, "dw_in_" + nm).astype(BF16) for a, sc, nm in dheads]
                                    + [_tn_matmul(dgates, h2, "dw_in_gates")], axis=0)
    row0, pieces_in = 0, []
    for a, _, _ in dheads:
        pieces_in.append((a, row0))
        row0 += a.shape[0] * HEAD_DIM
    dx1, dgmix = _proj_bwd(pieces_in, dgates, w_in_s, x1, gmix, dx2)

    dx0, dg1, dz1, da1, db1, u1 = _ffn_bwd(dx1, xs, g1, a1, b1, f1w1, f1w3, f1w2, "ffn1_bwd")
    grads["ffn1_w1"] = _tn_matmul(da1, h1, "ffn1_dw1")
    grads["ffn1_w3"] = _tn_matmul(db1, h1, "ffn1_dw3")
    grads["ffn1_w2"] = _tn_matmul(u1, dz1, "ffn1_dw2")

    gparts = [grads[nk[0]].astype(BF16).reshape(N_CHIPS, sizes[i], D) for i, nk in enumerate(_BUF)]
    gbuf = jnp.concatenate(gparts, axis=1)
    R = gbuf.shape[1]
    half = R // 2
    c = lax.axis_index("c")
    mine = 2 * lax.axis_index("x") + lax.axis_index("y")
    from_sibling = _swap_halves(gbuf)
    my_half = lax.dynamic_slice_in_dim(gbuf, c * half, half, axis=1)
    chip_sum, chip_sum16 = _add_halves(my_half, from_sibling, "add_sibling")
    received = _scatter_to_owners(chip_sum16)
    own = lax.dynamic_index_in_dim(chip_sum, mine, axis=0, keepdims=False)
    my_rows = _add_received(own, received, "add_chips")
    their_rows = _swap_reduced(my_rows)
    reduced = jnp.concatenate([jnp.where(c == 0, my_rows, their_rows), jnp.where(c == 0, their_rows, my_rows)], axis=0)

    small_rows = [dg1, dgmix, dg3, dgf,
                  jnp.pad(d_sinks.reshape(1, -1), ((0, 0), (0, D - SWA_Q_HEADS))),
                  jnp.pad(d_rel.reshape(1, -1), ((0, 0), (0, D - REL_BUCKETS * SWA_Q_HEADS))),
                  jnp.pad(loss_part, ((0, 0), (0, D - 1))), jnp.zeros((1, D), F32)]
    small = _allreduce_small(jnp.concatenate(small_rows, axis=0))
    loss = small[6, 0]

    G = {}
    for i, nk in enumerate(_BUF):
        G[nk[0]] = _from_rows(nk, reduced[offs[i]:offs[i + 1]], W[nk[0]].shape[1])[None]
    G["norm_ffn1"], G["norm_mix"], G["norm_ffn2"] = small[0:1], small[1:2], small[2:3]
    G["norm_final"] = small[3]
    G["swa_sinks"] = small[4:5, :SWA_Q_HEADS]
    G["rel_bias"] = small[5, :REL_BUCKETS * SWA_Q_HEADS].reshape(REL_BUCKETS, SWA_Q_HEADS)

    delta, new_m, new_v = {}, {}, {}
    small_names = ["norm_ffn1", "norm_mix", "norm_ffn2", "norm_final", "swa_sinks", "rel_bias"]

    def pack(d):
        return jnp.concatenate([jnp.pad(d[n].reshape(1, -1), ((0, 0), (0, D - d[n].size))) for n in small_names]
                               + [jnp.zeros((2, D), F32)], axis=0)

    sd, sm, sv = _adamw(pack(W), pack(G), pack(M), pack(V), "adamw_small")
    for r, n in enumerate(small_names):
        for dst, src in ((delta, sd), (new_m, sm), (new_v, sv)):
            dst[n] = src[r, :W[n].size].reshape(W[n].shape)
    for nk in _BUF:
        n = nk[0]
        shp = W[n].shape
        two_d = (shp[1], shp[2])
        d_, m_, v_ = _adamw(W[n].reshape(two_d), G[n].reshape(two_d), M[n].reshape(two_d), V[n].reshape(two_d),
                            "adamw_" + n)
        delta[n], new_m[n], new_v[n] = d_.reshape(shp), m_.reshape(shp), v_.reshape(shp)

    return (loss, dx0[None], *[G[n] for n in names], *[delta[n] for n in names],
            *[new_m[n] for n in names], *[new_v[n] for n in names])
```

```python
import functools
import math

import jax
import jax.numpy as jnp
from jax import lax
from jax.experimental import pallas as pl
from jax.experimental.pallas import tpu as pltpu

F32, BF16 = jnp.float32, jnp.bfloat16
MESH_ID = pl.DeviceIdType.MESH
ANY = pl.BlockSpec(memory_space=pl.ANY)

RMS_EPS = 1e-6
HEAD_DIM = 64
SWA_Q_HEADS, SWA_KV_HEADS, SWA_GROUP = 8, 2, 4
SWA_BLOCK = 128
SB_HEADS = 8
SB_BLOCK = 256
REL_BUCKETS, REL_MAX_DIST = 32, 128
NEG_BIG = -1e30
QK_SCALE = HEAD_DIM ** -0.5
ADAM_LR, ADAM_B1, ADAM_B2, ADAM_EPS, ADAM_WD, ADAM_STEP = 0.001, 0.9, 0.999, 1e-08, 0.01, 10

N_CHIPS = 4
TOKEN_TILE = 512
MATMUL_TOKEN_TILE = 1024
WGRAD_ROW_TILES = (2176, 1408, 1024, 256)
FF_TILE = 256
VMEM_LIMIT = 56 * 1024 * 1024


def _cp(*sem):
    return pltpu.CompilerParams(dimension_semantics=sem, vmem_limit_bytes=VMEM_LIMIT)


def _nn(a, b):
    return jnp.dot(a, b, preferred_element_type=F32)


def _nt(a, b):
    return lax.dot_general(a, b, (((1,), (1,)), ((), ())), preferred_element_type=F32)


def _tn(a, b):
    return lax.dot_general(a, b, (((0,), (0,)), ((), ())), preferred_element_type=F32)


def _norm_fwd(x, g):
    return x * lax.rsqrt(jnp.mean(x * x, axis=-1, keepdims=True) + RMS_EPS) * g


def _norm_bwd(x, g, dh):
    r = lax.rsqrt(jnp.mean(x * x, axis=-1, keepdims=True) + RMS_EPS)
    xh = x * r
    dxh = dh * g
    dx = r * (dxh - xh * jnp.mean(dxh * xh, axis=-1, keepdims=True))
    return dx, jnp.sum(dh * xh, axis=0, keepdims=True)


def _softplus(z):
    return jnp.maximum(z, 0.0) + jnp.log(1.0 + jnp.exp(-jnp.abs(z)))


def _ffn_fwd(x, g, w1t, w3t, w2, name):
    S, D = x.shape
    F = w2.shape[0]
    tm, tf = min(MATMUL_TOKEN_TILE, S), FF_TILE
    nj = F // tf

    def body(x_ref, g_ref, w1_ref, w3_ref, w2_ref, xo_ref, h_ref, a_ref, b_ref, hs, acc):
        j = pl.program_id(1)

        @pl.when(j == 0)
        def _():
            hb = _norm_fwd(x_ref[...], g_ref[...]).astype(BF16)
            hs[...] = hb
            h_ref[...] = hb
            acc[...] = jnp.zeros_like(acc)

        h = hs[...]
        a = _nt(h, w1_ref[...])
        b = _nt(h, w3_ref[...])
        a_ref[...] = a.astype(BF16)
        b_ref[...] = b.astype(BF16)
        u = a * jax.nn.sigmoid(a) * b
        acc[...] += _nn(u.astype(BF16), w2_ref[...])

        @pl.when(j == nj - 1)
        def _():
            xo_ref[...] = x_ref[...] + 0.5 * acc[...]

    return pl.pallas_call(
        body, name=name, grid=(S // tm, nj),
        in_specs=[pl.BlockSpec((tm, D), lambda i, j: (i, 0)),
                  pl.BlockSpec((1, D), lambda i, j: (0, 0)),
                  pl.BlockSpec((tf, D), lambda i, j: (j, 0)),
                  pl.BlockSpec((tf, D), lambda i, j: (j, 0)),
                  pl.BlockSpec((tf, D), lambda i, j: (j, 0))],
        out_specs=[pl.BlockSpec((tm, D), lambda i, j: (i, 0)),
                   pl.BlockSpec((tm, D), lambda i, j: (i, 0)),
                   pl.BlockSpec((tm, tf), lambda i, j: (i, j)),
                   pl.BlockSpec((tm, tf), lambda i, j: (i, j))],
        out_shape=[jax.ShapeDtypeStruct((S, D), F32), jax.ShapeDtypeStruct((S, D), BF16),
                   jax.ShapeDtypeStruct((S, F), BF16), jax.ShapeDtypeStruct((S, F), BF16)],
        scratch_shapes=[pltpu.VMEM((tm, D), BF16), pltpu.VMEM((tm, D), F32)],
        compiler_params=_cp("arbitrary", "arbitrary"),
    )(x, g, w1t, w3t, w2)


def _ffn_bwd(dxo, x, g, a, b, w1t, w3t, w2, name):
    S, D = x.shape
    F = w2.shape[0]
    tm, tf = min(MATMUL_TOKEN_TILE, S), FF_TILE
    ni, nj = S // tm, F // tf

    def body(dxo_ref, x_ref, g_ref, a_ref, b_ref, w1_ref, w3_ref, w2_ref,
             dx_ref, dg_ref, dz_ref, da_ref, db_ref, u_ref, dzs, acc):
        i, j = pl.program_id(0), pl.program_id(1)

        @pl.when(j == 0)
        def _():
            dzb = (0.5 * dxo_ref[...]).astype(BF16)
            dzs[...] = dzb
            dz_ref[...] = dzb
            acc[...] = jnp.zeros_like(acc)

        du = _nt(dzs[...], w2_ref[...])
        av = a_ref[...].astype(F32)
        bv = b_ref[...].astype(F32)
        s = jax.nn.sigmoid(av)
        silu = av * s
        db = (du * silu).astype(BF16)
        da = (du * bv * (s * (1.0 + av * (1.0 - s)))).astype(BF16)
        da_ref[...] = da
        db_ref[...] = db
        u_ref[...] = (silu * bv).astype(BF16)
        acc[...] += _nn(da, w1_ref[...]) + _nn(db, w3_ref[...])

        @pl.when(j == nj - 1)
        def _():
            dx, dg = _norm_bwd(x_ref[...], g_ref[...], acc[...])
            dx_ref[...] = dxo_ref[...] + dx

            @pl.when(i == 0)
            def _():
                dg_ref[...] = dg

            @pl.when(i > 0)
            def _():
                dg_ref[...] += dg

    row = pl.BlockSpec((tm, D), lambda i, j: (i, 0))
    wsp = pl.BlockSpec((tf, D), lambda i, j: (j, 0))
    col = pl.BlockSpec((tm, tf), lambda i, j: (i, j))
    vec = pl.BlockSpec((1, D), lambda i, j: (0, 0))
    return pl.pallas_call(
        body, name=name, grid=(ni, nj),
        in_specs=[row, row, vec, col, col, wsp, wsp, wsp],
        out_specs=[row, vec, row, col, col, col],
        out_shape=[jax.ShapeDtypeStruct((S, D), F32), jax.ShapeDtypeStruct((1, D), F32),
                   jax.ShapeDtypeStruct((S, D), BF16), jax.ShapeDtypeStruct((S, F), BF16),
                   jax.ShapeDtypeStruct((S, F), BF16), jax.ShapeDtypeStruct((S, F), BF16)],
        scratch_shapes=[pltpu.VMEM((tm, D), BF16), pltpu.VMEM((tm, D), F32)],
        compiler_params=_cp("arbitrary", "arbitrary"),
    )(dxo, x, g, a, b, w1t, w3t, w2)


def _tn_matmul(a, b, name):
    S, M = a.shape
    N = b.shape[1]
    ts = min(MATMUL_TOKEN_TILE, S)
    tmm = next(t for t in WGRAD_ROW_TILES if M % t == 0)
    ns = S // ts

    def body(a_ref, b_ref, o_ref, acc):
        s = pl.program_id(1)
        part = _tn(a_ref[...], b_ref[...])

        @pl.when(s == 0)
        def _():
            acc[...] = part

        @pl.when(s > 0)
        def _():
            acc[...] += part

        @pl.when(s == ns - 1)
        def _():
            o_ref[...] = acc[...].astype(BF16)

    return pl.pallas_call(
        body, name=name, grid=(M // tmm, ns),
        in_specs=[pl.BlockSpec((ts, tmm), lambda m, s: (s, m)),
                  pl.BlockSpec((ts, N), lambda m, s: (s, 0))],
        out_specs=pl.BlockSpec((tmm, N), lambda m, s: (m, 0)),
        out_shape=jax.ShapeDtypeStruct((M, N), BF16),
        scratch_shapes=[pltpu.VMEM((tmm, N), F32)],
        compiler_params=_cp("arbitrary", "arbitrary"),
    )(a, b)


def _norm_matmul_nt(x, g, wt, out_dtype, name):
    S, D = x.shape
    N = wt.shape[0]
    tm = min(MATMUL_TOKEN_TILE, S)
    tn = next(t for t in (1024, 768, 256) if N % t == 0)

    def body(x_ref, g_ref, w_ref, o_ref, h_ref, hs):
        @pl.when(pl.program_id(1) == 0)
        def _():
            hb = _norm_fwd(x_ref[...], g_ref[...]).astype(BF16)
            hs[...] = hb
            h_ref[...] = hb

        o_ref[...] = _nt(hs[...], w_ref[...]).astype(out_dtype)

    return pl.pallas_call(
        body, name=name, grid=(S // tm, N // tn),
        in_specs=[pl.BlockSpec((tm, D), lambda i, j: (i, 0)),
                  pl.BlockSpec((1, D), lambda i, j: (0, 0)),
                  pl.BlockSpec((tn, D), lambda i, j: (j, 0))],
        out_specs=[pl.BlockSpec((tm, tn), lambda i, j: (i, j)),
                   pl.BlockSpec((tm, D), lambda i, j: (i, 0))],
        out_shape=[jax.ShapeDtypeStruct((S, N), out_dtype), jax.ShapeDtypeStruct((S, D), BF16)],
        scratch_shapes=[pltpu.VMEM((tm, D), BF16)],
        compiler_params=_cp("arbitrary", "arbitrary"),
    )(x, g, wt)


def _heads_tile(ref):
    Hh, nbk = ref.shape[0], ref.shape[1]
    return jnp.concatenate([jnp.concatenate([ref[h, b] for b in range(nbk)], axis=1) for h in range(Hh)], axis=0)


def _store_heads(ref, val):
    Hh, nbk, dh, T = ref.shape
    for h in range(Hh):
        for b in range(nbk):
            ref[h, b] = val[h * dh:(h + 1) * dh, b * T:(b + 1) * T].astype(ref.dtype)


def _norm_proj_heads(x, g, w_rows, T, name):
    S, D = x.shape
    N = w_rows.shape[0]
    tm, tn = min(MATMUL_TOKEN_TILE, S), 768

    def body(x_ref, g_ref, w_ref, o_ref, hs):
        @pl.when(pl.program_id(1) == 0)
        def _():
            hs[...] = _norm_fwd(x_ref[...], g_ref[...]).astype(BF16)

        _store_heads(o_ref, _nt(w_ref[...], hs[...]))

    return pl.pallas_call(
        body, name=name, grid=(S // tm, N // tn),
        in_specs=[pl.BlockSpec((tm, D), lambda i, j: (i, 0)),
                  pl.BlockSpec((1, D), lambda i, j: (0, 0)),
                  pl.BlockSpec((tn, D), lambda i, j: (j, 0))],
        out_specs=pl.BlockSpec((tn // HEAD_DIM, tm // T, HEAD_DIM, T), lambda i, j: (j, i, 0, 0)),
        out_shape=jax.ShapeDtypeStruct((N // HEAD_DIM, S // T, HEAD_DIM, T), BF16),
        scratch_shapes=[pltpu.VMEM((tm, D), BF16)],
        compiler_params=_cp("arbitrary", "arbitrary"),
    )(x, g, w_rows)


def _heads_matmul(at, b, scale, name):
    Hh, nb, dh, T = at.shape
    S, N = b.shape
    ts = min(MATMUL_TOKEN_TILE, S)
    ns = S // ts

    def body(a_ref, b_ref, o_ref):
        s = pl.program_id(0)
        a = _heads_tile(a_ref)
        part = _nn((a if scale == 1.0 else a * scale).astype(BF16), b_ref[...])

        @pl.when(s == 0)
        def _():
            o_ref[...] = part

        @pl.when(s > 0)
        def _():
            o_ref[...] += part

    return pl.pallas_call(
        body, name=name, grid=(ns,),
        in_specs=[pl.BlockSpec((Hh, ts // T, dh, T), lambda s: (0, s, 0, 0)),
                  pl.BlockSpec((ts, N), lambda s: (s, 0))],
        out_specs=pl.BlockSpec((Hh * dh, N), lambda s: (0, 0)),
        out_shape=jax.ShapeDtypeStruct((Hh * dh, N), F32),
        compiler_params=_cp("arbitrary"),
    )(at, b)


def _proj_bwd(pieces, dgates, w_rows, x, g, dres):
    S, D = x.shape
    tm = min(TOKEN_TILE, S)
    n_p = len(pieces)
    gate_row = w_rows.shape[0] - dgates.shape[1]

    def body(*refs):
        p_refs = refs[:n_p]
        dgt_ref, w_ref, x_ref, g_ref, dres_ref, dx_ref, dg_ref = refs[n_p:]
        i = pl.program_id(0)
        dh = _nn(dgt_ref[...], w_ref[gate_row:, :])
        for p_ref, (arr, row0) in zip(p_refs, pieces):
            rows = arr.shape[0] * arr.shape[2]
            dh += _tn(_heads_tile(p_ref).astype(BF16), w_ref[row0:row0 + rows, :])
        dx, dg = _norm_bwd(x_ref[...], g_ref[...], dh)
        dx_ref[...] = dres_ref[...] + dx

        @pl.when(i == 0)
        def _():
            dg_ref[...] = dg

        @pl.when(i > 0)
        def _():
            dg_ref[...] += dg

    row = pl.BlockSpec((tm, D), lambda i: (i, 0))
    vec = pl.BlockSpec((1, D), lambda i: (0, 0))
    p_specs = [pl.BlockSpec((a.shape[0], tm // a.shape[3], a.shape[2], a.shape[3]), lambda i: (0, i, 0, 0))
               for a, _ in pieces]
    return pl.pallas_call(
        body, name="proj_bwd", grid=(S // tm,),
        in_specs=p_specs + [pl.BlockSpec((tm, dgates.shape[1]), lambda i: (i, 0)),
                            pl.BlockSpec(w_rows.shape, lambda i: (0, 0), pipeline_mode=pl.Buffered(1)),
                            row, vec, row],
        out_specs=[row, vec],
        out_shape=[jax.ShapeDtypeStruct((S, D), F32), jax.ShapeDtypeStruct((1, D), F32)],
        compiler_params=_cp("arbitrary"),
    )(*[a for a, _ in pieces], dgates, w_rows, x, g, dres)


def _merge_fwd(x1, gates, oa_t, ob_t, wat, wbt, w_out):
    S, D = x1.shape
    W = wat.shape[1]
    tm = min(TOKEN_TILE, S)

    def body(x_ref, ga_ref, gb_ref, oa_ref, ob_ref, wa_ref, wb_ref, wo_ref,
             x2_ref, mg_ref, ba_ref, bb_ref):
        ba = _nt(_heads_tile(oa_ref).T.astype(BF16), wa_ref[...])
        bb = _nt(_heads_tile(ob_ref).T.astype(BF16), wb_ref[...])
        merged = jax.nn.sigmoid(ga_ref[...]) * ba + jax.nn.sigmoid(gb_ref[...]) * bb
        mb = merged.astype(BF16)
        mg_ref[...] = mb
        ba_ref[...] = ba.astype(BF16)
        bb_ref[...] = bb.astype(BF16)
        x2_ref[...] = x_ref[...] + _nn(mb, wo_ref[...])

    row = pl.BlockSpec((tm, D), lambda i: (i, 0))
    full = lambda r, c: pl.BlockSpec((r, c), lambda i: (0, 0))
    heads = lambda a: pl.BlockSpec((a.shape[0], tm // a.shape[3], a.shape[2], a.shape[3]), lambda i: (0, i, 0, 0))
    return pl.pallas_call(
        body, name="merge_fwd", grid=(S // tm,),
        in_specs=[row, pl.BlockSpec((tm, D), lambda i: (i, 0)), pl.BlockSpec((tm, D), lambda i: (i, 1)),
                  heads(oa_t), heads(ob_t), full(D, W), full(D, W), full(D, D)],
        out_specs=[row, row, row, row],
        out_shape=[jax.ShapeDtypeStruct((S, D), F32)] + [jax.ShapeDtypeStruct((S, D), BF16)] * 3,
        compiler_params=_cp("arbitrary"),
    )(x1, gates, gates, oa_t, ob_t, wat, wbt, w_out)


def _merge_bwd(dx2, gates, ba, bb, wa, wb, w_out, t_a, t_b):
    S, D = dx2.shape
    W = wa.shape[0]
    tm = min(TOKEN_TILE, S)
    Hh = W // HEAD_DIM

    def body(dx_ref, ga_ref, gb_ref, ba_ref, bb_ref, wa_ref, wb_ref, wo_ref,
             dxb_ref, dba_ref, dbb_ref, dgt_ref, doa_ref, dob_ref):
        dxb = dx_ref[...].astype(BF16)
        dxb_ref[...] = dxb
        dm = _nt(dxb, wo_ref[...])
        sa = jax.nn.sigmoid(ga_ref[...])
        sb = jax.nn.sigmoid(gb_ref[...])
        dba = (dm * sa).astype(BF16)
        dbb = (dm * sb).astype(BF16)
        dba_ref[...] = dba
        dbb_ref[...] = dbb
        dgt_ref[:, :D] = (dm * ba_ref[...].astype(F32) * sa * (1.0 - sa)).astype(BF16)
        dgt_ref[:, D:] = (dm * bb_ref[...].astype(F32) * sb * (1.0 - sb)).astype(BF16)
        _store_heads(doa_ref, _nt(wa_ref[...], dba))
        _store_heads(dob_ref, _nt(wb_ref[...], dbb))

    row = pl.BlockSpec((tm, D), lambda i: (i, 0))
    full = lambda r, c: pl.BlockSpec((r, c), lambda i: (0, 0))
    heads = lambda T: pl.BlockSpec((Hh, tm // T, HEAD_DIM, T), lambda i: (0, i, 0, 0))
    return pl.pallas_call(
        body, name="merge_bwd", grid=(S // tm,),
        in_specs=[row, pl.BlockSpec((tm, D), lambda i: (i, 0)), pl.BlockSpec((tm, D), lambda i: (i, 1)),
                  row, row, full(W, D), full(W, D), full(D, D)],
        out_specs=[row, row, row, pl.BlockSpec((tm, 2 * D), lambda i: (i, 0)), heads(t_a), heads(t_b)],
        out_shape=[jax.ShapeDtypeStruct((S, D), BF16)] * 3 + [jax.ShapeDtypeStruct((S, 2 * D), BF16),
                   jax.ShapeDtypeStruct((Hh, S // t_a, HEAD_DIM, t_a), F32),
                   jax.ShapeDtypeStruct((Hh, S // t_b, HEAD_DIM, t_b), BF16)],
        compiler_params=_cp("arbitrary"),
    )(dx2, gates, gates, ba, bb, wa, wb, w_out)


def _final_loss(x3, gf, target):
    S, D = x3.shape
    tm = min(TOKEN_TILE, S)

    def body(x_ref, g_ref, t_ref, loss_ref, dx_ref, dg_ref):
        i = pl.program_id(0)
        x = x_ref[...]
        g = g_ref[...]
        e = _norm_fwd(x, g) - t_ref[...]
        part = 0.5 * jnp.sum(jnp.mean(e * e, axis=-1, keepdims=True), axis=0, keepdims=True)
        dx, dg = _norm_bwd(x, g, e * (1.0 / D))
        dx_ref[...] = dx

        @pl.when(i == 0)
        def _():
            loss_ref[...] = part
            dg_ref[...] = dg

        @pl.when(i > 0)
        def _():
            loss_ref[...] += part
            dg_ref[...] += dg

    row = pl.BlockSpec((tm, D), lambda i: (i, 0))
    vec = pl.BlockSpec((1, D), lambda i: (0, 0))
    return pl.pallas_call(
        body, name="final_loss", grid=(S // tm,),
        in_specs=[row, vec, row],
        out_specs=[pl.BlockSpec((1, 1), lambda i: (0, 0)), row, vec],
        out_shape=[jax.ShapeDtypeStruct((1, 1), F32), jax.ShapeDtypeStruct((S, D), F32),
                   jax.ShapeDtypeStruct((1, D), F32)],
        compiler_params=_cp("arbitrary"),
    )(x3, gf, target)


SB_FWD_HEAD_GROUP = 8
SB_HEAD_GROUP = 4
LANES = 128


def _tri(T, kind):
    r = lax.broadcasted_iota(jnp.int32, (T, T), 0)
    c = lax.broadcasted_iota(jnp.int32, (T, T), 1)
    return {"after": r > c, "upto": r <= c, "before": r < c}[kind].astype(BF16)


def _lane(v, j):
    return jnp.broadcast_to(v[:, j:j + 1], (v.shape[0], LANES))


def _t_bf16(x):
    return x.astype(F32).T.astype(BF16)


def _wide(v, T):
    return jnp.tile(v, (1, T // LANES))


SB_SLOTS = 3
SB_FWD_SLOTS = 2
COPY_PARTS = 4


class _split_copy:
    def __init__(self, src, dst, sems):
        n = src.shape[0] // COPY_PARTS
        self.parts = [pltpu.make_async_copy(src.at[pl.ds(r * n, n)], dst.at[pl.ds(r * n, n)], sems.at[r])
                      for r in range(COPY_PARTS)]

    def start(self):
        for cp in self.parts:
            cp.start()

    def wait(self):
        for cp in self.parts:
            cp.wait()


def _sb_pair(i, kb):
    return (i * (i + 1)) // 2 + kb


def _sb_fwd(qkv):
    H3, nb, dh, T = qkv.shape
    H = H3 // 3
    HG = SB_FWD_HEAD_GROUP
    assert HG == H, "one head group: a saved tile holds all the heads"
    n_pairs = (nb * (nb + 1)) // 2

    def body(q_ref, k_ref, v_ref, o_ref, saved_ref, stage, sems):
        row = lax.broadcasted_iota(jnp.int32, (T, T), 0)
        col = lax.broadcasted_iota(jnp.int32, (T, T), 1)
        tri = col < row
        after = _tri(T, "after")

        def save(slot, pair):
            return _split_copy(stage.at[slot], saved_ref.at[pair], sems.at[slot])

        def blocks(qs, i, kb, step, carry, diag):
            hs = range(HG)
            slot = step % SB_FWD_SLOTS

            @pl.when(step >= SB_FWD_SLOTS)
            def _():
                save(slot, 0).wait()

            z = [_nn(qs[hh], k_ref[hh, kb]) for hh in hs]
            res, ls, first = [None] * HG, [None] * HG, [None] * HG
            for hh in hs:
                sp = _softplus(z[hh])
                if diag:
                    sp = jnp.where(tri, sp, 0.0)
                ls[hh] = z[hh] - sp
                spb = sp.astype(BF16)
                first[hh] = _lane(spb.astype(F32), 0)
                res[hh] = _nn(spb, after)
            out = []
            for hh in hs:
                c, oacc = carry[2 * hh], carry[2 * hh + 1]
                a = jnp.exp(ls[hh] - (res[hh] + _wide(c, T)))
                if diag:
                    a = jnp.where(tri, a, 0.0)
                ab = a.astype(BF16)
                stage[slot, hh, 0] = ab
                stage[slot, hh, 1] = jnp.exp(ls[hh]).astype(BF16)
                out.extend([c + (first[hh] + _lane(res[hh], 0)), oacc + _nt(v_ref[hh, kb], ab)])
            save(slot, _sb_pair(i, kb)).start()
            return tuple(out)

        def qblock(i, step):
            qs = [_t_bf16(q_ref[hh, i]) for hh in range(HG)]
            carry = blocks(qs, i, i, step, (jnp.zeros((T, LANES), F32), jnp.zeros((dh, T), F32)) * HG, True)

            def kstep(t, carry):
                return blocks(qs, i, i - 1 - t, step + 1 + t, carry, False)

            carry = lax.fori_loop(0, i, kstep, carry)
            for hh in range(HG):
                o_ref[hh, i] = carry[2 * hh + 1]
            return step + 1 + i

        lax.fori_loop(0, nb, qblock, 0)
        for slot in range(min(SB_FWD_SLOTS, n_pairs)):
            save(slot, 0).wait()

    ht = lambda part: pl.BlockSpec((HG, nb, dh, T), lambda h: (part, 0, 0, 0), pipeline_mode=pl.Buffered(1))
    return pl.pallas_call(
        body, name="sb_fwd", grid=(1,),
        in_specs=[ht(0), ht(1), ht(2)],
        out_specs=[ht(0), ANY],
        out_shape=[jax.ShapeDtypeStruct((H, nb, dh, T), F32),
                   jax.ShapeDtypeStruct((n_pairs, H, 2, T, T), BF16)],
        scratch_shapes=[pltpu.VMEM((SB_FWD_SLOTS, HG, 2, T, T), BF16),
                        pltpu.SemaphoreType.DMA((SB_FWD_SLOTS, COPY_PARTS))],
        compiler_params=_cp("arbitrary"),
    )(qkv, qkv, qkv)


def _sb_bwd(qkv, dot, saved):
    H3, nb, dh, T = qkv.shape
    H = H3 // 3
    HG = SB_HEAD_GROUP
    n_pairs = (nb * (nb + 1)) // 2

    def body(qt_ref, k_ref, v_ref, dot_ref, saved_ref, dq_ref, dk_ref, dv_ref, stage, sems):
        head0 = pl.program_id(0) * HG
        row = lax.broadcasted_iota(jnp.int32, (T, T), 0)
        col = lax.broadcasted_iota(jnp.int32, (T, T), 1)
        tri = col < row
        before = _tri(T, "before")
        dk_ref[...] = jnp.zeros_like(dk_ref)
        dv_ref[...] = jnp.zeros_like(dv_ref)

        def fetch(slot, pair):
            return _split_copy(saved_ref.at[pair, pl.ds(head0, HG)], stage.at[slot], sems.at[slot])

        for ahead in range(min(SB_SLOTS - 1, n_pairs)):
            fetch(ahead, ahead).start()

        def blocks(qTs, dos, doTs, i, kb, carry, diag):
            hs = range(HG)
            pair = _sb_pair(i, kb)
            slot = pair % SB_SLOTS
            fetch(slot, pair).wait()
            nxt = pair + (SB_SLOTS - 1)

            @pl.when(nxt < n_pairs)
            def _():
                fetch(nxt % SB_SLOTS, nxt).start()

            kT = [k_ref[hh, kb] for hh in hs]
            da = [_nn(dos[hh], v_ref[hh, kb]) for hh in hs]
            g, gb, resg = [None] * HG, [None] * HG, [None] * HG
            for hh in hs:
                g[hh] = stage[slot, hh, 0].astype(F32) * da[hh]
                gb[hh] = g[hh].astype(BF16)
                resg[hh] = _nn(gb[hh], before)
            out = []
            for hh in hs:
                pre_g, dq = carry[2 * hh], carry[2 * hh + 1]
                dz = g[hh] - (g[hh] + (resg[hh] + _wide(pre_g, T))) * stage[slot, hh, 1].astype(F32)
                if diag:
                    dz = jnp.where(tri, dz, 0.0)
                dzb = dz.astype(BF16)
                dk_ref[hh, kb] += _nn(qTs[hh], dzb)
                dv_ref[hh, kb] += _nn(doTs[hh], stage[slot, hh, 0])
                out.extend([pre_g + (_lane(resg[hh], T - 1) + _lane(gb[hh].astype(F32), T - 1)),
                            dq + _nt(kT[hh], dzb)])
            return tuple(out)

        def qblock(i, _):
            qTs = [qt_ref[hh, i] for hh in range(HG)]
            doTs = [dot_ref[hh, i] for hh in range(HG)]
            dos = [_t_bf16(v) for v in doTs]
            carry = (jnp.zeros((T, LANES), F32), jnp.zeros((dh, T), F32)) * HG

            def kstep(kb, carry):
                return blocks(qTs, dos, doTs, i, kb, carry, False)

            carry = lax.fori_loop(0, i, kstep, carry)
            carry = blocks(qTs, dos, doTs, i, i, carry, True)
            for hh in range(HG):
                dq_ref[hh, i] = carry[2 * hh + 1]
            return 0

        lax.fori_loop(0, nb, qblock, 0)

    G = H // HG
    ht = lambda part: pl.BlockSpec((HG, nb, dh, T), lambda h: (h + part * G, 0, 0, 0),
                                   pipeline_mode=pl.Buffered(1))
    return pl.pallas_call(
        body, name="sb_bwd", grid=(G,),
        in_specs=[ht(0), ht(1), ht(2), ht(0), ANY],
        out_specs=[ht(0), ht(0), ht(0)],
        out_shape=[jax.ShapeDtypeStruct((H, nb, dh, T), F32)] * 3,
        scratch_shapes=[pltpu.VMEM((SB_SLOTS, HG, 2, T, T), BF16), pltpu.SemaphoreType.DMA((SB_SLOTS, COPY_PARTS))],
        compiler_params=_cp("arbitrary"),
    )(qkv, qkv, qkv, dot, saved)


def _swa_probs(zp, zc, bias, sink, first):
    T = zp.shape[0]
    row = lax.broadcasted_iota(jnp.int32, (T, T), 0)
    col = lax.broadcasted_iota(jnp.int32, (T, T), 1)
    lp = jnp.where(jnp.logical_and(col > row, jnp.logical_not(first)), zp + bias[:, :T], NEG_BIG)
    lc = jnp.where(col <= row, zc + bias[:, T:], NEG_BIG)
    m = jnp.maximum(jnp.maximum(jnp.max(lp, axis=1, keepdims=True), jnp.max(lc, axis=1, keepdims=True)), sink)
    pp = jnp.exp(lp - m)
    pc = jnp.exp(lc - m)
    ps = jnp.exp(sink - m)
    inv = 1.0 / (jnp.sum(pp, axis=1, keepdims=True) + jnp.sum(pc, axis=1, keepdims=True) + ps)
    return pp * inv, pc * inv, ps * inv


def _swa_specs(nb, dh, T, grp, Hq, Hkv, clamp):
    blk = (lambda n: jnp.minimum(n, nb - 1)) if clamp else (lambda n: n)
    q = pl.BlockSpec((grp, None, dh, T), lambda h, n: (h, blk(n), 0, 0))
    one = lambda first, back: pl.BlockSpec(
        (None, None, dh, T), lambda h, n: (first + h, jnp.maximum(blk(n) - back, 0) if back else blk(n), 0, 0))
    return q, [one(Hq, 1), one(Hq, 0), one(Hq + Hkv, 1), one(Hq + Hkv, 0)]


def _swa_fwd(qkv, bias, sinks):
    Hq, Hkv, grp = SWA_Q_HEADS, SWA_KV_HEADS, SWA_GROUP
    _, nb, dh, T = qkv.shape

    def body(sink_ref, q_ref, kp_ref, kc_ref, vp_ref, vc_ref, bias_ref, o_ref):
        hk, n = pl.program_id(0), pl.program_id(1)
        kp, kc, vp, vc = kp_ref[...], kc_ref[...], vp_ref[...], vc_ref[...]
        qs = [_t_bf16(q_ref[g]) for g in range(grp)]
        zs = [(_nn(q, kp), _nn(q, kc)) for q in qs]
        for g in range(grp):
            pp, pc, _ = _swa_probs(*zs[g], bias_ref[g], sink_ref[hk * grp + g], n == 0)
            o_ref[g] = _nt(vp, pp.astype(BF16)) + _nt(vc, pc.astype(BF16))

    q_spec, kv_specs = _swa_specs(nb, dh, T, grp, Hq, Hkv, False)
    return pl.pallas_call(
        body, name="swa_fwd", grid=(Hkv, nb),
        in_specs=[pl.BlockSpec(memory_space=pltpu.SMEM), q_spec] + kv_specs
                 + [pl.BlockSpec((grp, T, 2 * T), lambda h, n: (h, 0, 0))],
        out_specs=pl.BlockSpec((grp, None, dh, T), lambda h, n: (h, n, 0, 0)),
        out_shape=jax.ShapeDtypeStruct((Hq, nb, dh, T), F32),
        compiler_params=_cp("arbitrary", "arbitrary"),
    )(sinks, qkv, qkv, qkv, qkv, qkv, bias)


def _swa_bwd(qkv, bias, sinks, dot, ot):
    Hq, Hkv, grp = SWA_Q_HEADS, SWA_KV_HEADS, SWA_GROUP
    _, nb, dh, T = qkv.shape

    def body(sink_ref, qt_ref, kp_ref, kc_ref, vp_ref, vc_ref, bias_ref, dot_ref, ot_ref,
             dq_ref, dk_ref, dv_ref, dbias_ref, dsink_ref, ck, cv):
        hk, n = pl.program_id(0), pl.program_id(1)

        @pl.when(n == 0)
        def _():
            dbias_ref[...] = jnp.zeros_like(dbias_ref)
            dsink_ref[...] = jnp.zeros_like(dsink_ref)
            ck[...] = jnp.zeros_like(ck)
            cv[...] = jnp.zeros_like(cv)

        @pl.when(n < nb)
        def _():
            kp, kc, vp, vc = kp_ref[...], kc_ref[...], vp_ref[...], vc_ref[...]
            kprev = jnp.zeros((dh, T), F32)
            vprev = jnp.zeros((dh, T), F32)
            kcur = jnp.zeros((dh, T), F32)
            vcur = jnp.zeros((dh, T), F32)
            qTs = [qt_ref[g] for g in range(grp)]
            qs = [_t_bf16(v) for v in qTs]
            dos = [dot_ref[g].T for g in range(grp)]
            zs = [(_nn(q, kp), _nn(q, kc)) for q in qs]
            dps = [(_nn(do.astype(BF16), vp), _nn(do.astype(BF16), vc)) for do in dos]
            dls, pbs = [], []
            for g in range(grp):
                pp, pc, ps = _swa_probs(*zs[g], bias_ref[g], sink_ref[hk * grp + g], n == 0)
                delta = jnp.sum(dos[g] * ot_ref[g].T, axis=1, keepdims=True)
                dlp = pp * (dps[g][0] - delta)
                dlc = pc * (dps[g][1] - delta)
                dbias_ref[g, :, :T] += dlp
                dbias_ref[g, :, T:] += dlc
                dsink_ref[g] += -ps * delta
                dls.append((dlp.astype(BF16), dlc.astype(BF16)))
                pbs.append((pp.astype(BF16), pc.astype(BF16)))
            for g in range(grp):
                dlpb, dlcb = dls[g]
                doT = dot_ref[g].astype(BF16)
                dq_ref[g] = _nt(kp, dlpb) + _nt(kc, dlcb)
                kprev += _nn(qTs[g], dlpb)
                kcur += _nn(qTs[g], dlcb)
                vprev += _nn(doT, pbs[g][0])
                vcur += _nn(doT, pbs[g][1])
            dk_ref[...] = ck[...] + kprev
            dv_ref[...] = cv[...] + vprev
            ck[...] = kcur
            cv[...] = vcur

        @pl.when(n == nb)
        def _():
            dk_ref[...] = ck[...]
            dv_ref[...] = cv[...]

    qt_spec, kv_specs = _swa_specs(nb, dh, T, grp, Hq, Hkv, True)
    prev = pl.BlockSpec((None, None, dh, T), lambda h, n: (h, jnp.maximum(n - 1, 0), 0, 0))
    per_group = lambda a, b: pl.BlockSpec((grp, a, b), lambda h, n: (h, 0, 0))
    return pl.pallas_call(
        body, name="swa_bwd", grid=(Hkv, nb + 1),
        in_specs=[pl.BlockSpec(memory_space=pltpu.SMEM), qt_spec] + kv_specs
                 + [per_group(T, 2 * T), qt_spec, qt_spec],
        out_specs=[qt_spec, prev, prev, per_group(T, 2 * T), per_group(T, 1)],
        out_shape=[jax.ShapeDtypeStruct((Hq, nb, dh, T), F32), jax.ShapeDtypeStruct((Hkv, nb, dh, T), F32),
                   jax.ShapeDtypeStruct((Hkv, nb, dh, T), F32), jax.ShapeDtypeStruct((Hq, T, 2 * T), F32),
                   jax.ShapeDtypeStruct((Hq, T, 1), F32)],
        scratch_shapes=[pltpu.VMEM((dh, T), F32), pltpu.VMEM((dh, T), F32)],
        compiler_params=_cp("arbitrary", "arbitrary"),
    )(sinks, qkv, qkv, qkv, qkv, qkv, bias, dot, ot)


def _split3(x):
    h1 = x.astype(BF16)
    r1 = x - h1.astype(F32)
    h2 = r1.astype(BF16)
    h3 = (r1 - h2.astype(F32)).astype(BF16)
    return h1, h2, h3


def _bias_expand(rel_t, onehot):
    Hq, NB = rel_t.shape
    L = onehot.shape[1]

    def body(r_ref, oh_ref, o_ref):
        h1, h2, h3 = _split3(r_ref[...])
        oh = oh_ref[...]
        o_ref[...] = _nn(h1, oh) + _nn(h2, oh) + _nn(h3, oh)

    return pl.pallas_call(
        body, name="bias_expand", grid=(1,),
        in_specs=[pl.BlockSpec((Hq, NB), lambda i: (0, 0)), pl.BlockSpec((NB, L), lambda i: (0, 0))],
        out_specs=pl.BlockSpec((Hq, L), lambda i: (0, 0)),
        out_shape=jax.ShapeDtypeStruct((Hq, L), F32),
        compiler_params=_cp("arbitrary"),
    )(rel_t, onehot)


def _bias_reduce(dbias, onehot):
    Hq, L = dbias.shape
    NB = onehot.shape[0]

    def body(d_ref, oh_ref, o_ref):
        h1, h2, h3 = _split3(d_ref[...])
        oh = oh_ref[...]
        o_ref[...] = _nt(h1, oh) + _nt(h2, oh) + _nt(h3, oh)

    return pl.pallas_call(
        body, name="bias_reduce", grid=(1,),
        in_specs=[pl.BlockSpec((Hq, L), lambda i: (0, 0)), pl.BlockSpec((NB, L), lambda i: (0, 0))],
        out_specs=pl.BlockSpec((Hq, NB), lambda i: (0, 0)),
        out_shape=jax.ShapeDtypeStruct((Hq, NB), F32),
        compiler_params=_cp("arbitrary"),
    )(dbias, onehot)


def _adamw(w, g, m, v, name):
    R, C = w.shape
    tr = 256 if R % 256 == 0 else R
    bc1 = 1.0 - ADAM_B1 ** ADAM_STEP
    bc2 = 1.0 - ADAM_B2 ** ADAM_STEP

    def body(w_ref, g_ref, m_ref, v_ref, d_ref, nm_ref, nv_ref):
        g = g_ref[...]
        m2 = ADAM_B1 * m_ref[...] + (1.0 - ADAM_B1) * g
        v2 = ADAM_B2 * v_ref[...] + (1.0 - ADAM_B2) * (g * g)
        nm_ref[...] = m2
        nv_ref[...] = v2
        d_ref[...] = -ADAM_LR * ((m2 / bc1) / (jnp.sqrt(v2 / bc2) + ADAM_EPS) + ADAM_WD * w_ref[...])

    spec = pl.BlockSpec((tr, C), lambda i: (i, 0))
    return pl.pallas_call(
        body, name=name, grid=(R // tr,),
        in_specs=[spec] * 4, out_specs=[spec] * 3,
        out_shape=[jax.ShapeDtypeStruct((R, C), F32)] * 3,
        compiler_params=_cp("arbitrary"),
    )(w, g, m, v)


def _add_halves(mine, recv, name):
    K, R, C = mine.shape
    tr = 416 if R % 416 == 0 else R

    def body(a_ref, b_ref, o_ref, ob_ref):
        s = a_ref[...].astype(F32) + b_ref[...].astype(F32)
        o_ref[...] = s
        ob_ref[...] = s.astype(BF16)

    spec = pl.BlockSpec((None, tr, C), lambda k, i: (k, i, 0))
    return pl.pallas_call(
        body, name=name, grid=(K, R // tr),
        in_specs=[spec, spec], out_specs=[spec, spec],
        out_shape=[jax.ShapeDtypeStruct((K, R, C), F32), jax.ShapeDtypeStruct((K, R, C), BF16)],
        compiler_params=_cp("arbitrary", "arbitrary"),
    )(mine, recv)


def _add_received(own, recv, name):
    R, C = own.shape
    tr = 416 if R % 416 == 0 else R

    def body(a_ref, r_ref, o_ref):
        o_ref[...] = ((a_ref[...] + r_ref[0].astype(F32)) + r_ref[1].astype(F32)) + r_ref[2].astype(F32)

    return pl.pallas_call(
        body, name=name, grid=(R // tr,),
        in_specs=[pl.BlockSpec((tr, C), lambda i: (i, 0)), pl.BlockSpec((3, tr, C), lambda i: (0, i, 0))],
        out_specs=pl.BlockSpec((tr, C), lambda i: (i, 0)),
        out_shape=jax.ShapeDtypeStruct((R, C), F32),
        compiler_params=_cp("arbitrary"),
    )(own, recv)


def _position():
    x, y, c = lax.axis_index("x"), lax.axis_index("y"), lax.axis_index("c")
    others = [(1 - x, y), (x, 1 - y), (1 - x, 1 - y)]
    return x, y, c, others


def _remote(src, dst, send_sems, recv_sems, k, dev):
    return pltpu.make_async_remote_copy(src_ref=src, dst_ref=dst, send_sem=send_sems.at[k],
                                        recv_sem=recv_sems.at[k], device_id=dev, device_id_type=MESH_ID)


def _gather_weights(shard):
    R, C = shard.shape
    half = R // 2

    def body(src, out, send_sems, recv_sems):
        x, y, c, others = _position()
        mine = 2 * x + y
        sends = [_remote(src.at[c], out.at[mine, c], send_sems, recv_sems, j, (ox, oy, c))
                 for j, (ox, oy) in enumerate(others)]
        for cp in sends:
            cp.start()
        passed = []
        for j, (ox, oy) in enumerate(others):
            slot = out.at[2 * ox + oy, c]
            _remote(slot, slot, send_sems, recv_sems, j, (ox, oy, c)).wait_recv()
            fwd = _remote(slot, slot, send_sems, recv_sems, 3 + j, (x, y, 1 - c))
            fwd.start()
            passed.append(fwd)
        for j, (ox, oy) in enumerate(others):
            slot = out.at[2 * ox + oy, 1 - c]
            _remote(slot, slot, send_sems, recv_sems, 3 + j, (x, y, 1 - c)).wait_recv()
        for cp in sends + passed:
            cp.wait_send()

    return pl.pallas_call(
        body, name="gather_weights",
        in_specs=[ANY], out_specs=ANY,
        out_shape=jax.ShapeDtypeStruct((N_CHIPS, 2, half, C), shard.dtype),
        scratch_shapes=[pltpu.SemaphoreType.DMA((6,)), pltpu.SemaphoreType.DMA((6,))],
    )(shard.reshape(2, half, C)).reshape(N_CHIPS, R, C)


def _swap_halves(grads):
    K, R, C = grads.shape
    half = R // 2

    def body(src, out, send_sems, recv_sems):
        x, y, c, _ = _position()
        theirs = src.at[:, pl.ds(pl.multiple_of((1 - c) * half, 16), half), :]
        cp = _remote(theirs, out, send_sems, recv_sems, 0, (x, y, 1 - c))
        cp.start()
        cp.wait()

    return pl.pallas_call(
        body, name="swap_halves",
        in_specs=[ANY], out_specs=ANY,
        out_shape=jax.ShapeDtypeStruct((K, half, C), grads.dtype),
        scratch_shapes=[pltpu.SemaphoreType.DMA((1,)), pltpu.SemaphoreType.DMA((1,))],
    )(grads)


def _scatter_to_owners(parts):
    K, H, C = parts.shape

    def body(src, out, send_sems, recv_sems):
        x, y, c, others = _position()
        sends = [_remote(src.at[2 * ox + oy], out.at[j], send_sems, recv_sems, j, (ox, oy, c))
                 for j, (ox, oy) in enumerate(others)]
        for cp in sends:
            cp.start()
        for cp in sends:
            cp.wait()

    return pl.pallas_call(
        body, name="scatter_to_owners",
        in_specs=[ANY], out_specs=ANY,
        out_shape=jax.ShapeDtypeStruct((3, H, C), parts.dtype),
        scratch_shapes=[pltpu.SemaphoreType.DMA((3,)), pltpu.SemaphoreType.DMA((3,))],
    )(parts)


def _swap_reduced(half_rows):
    H, C = half_rows.shape

    def body(src, out, send_sems, recv_sems):
        x, y, c, _ = _position()
        cp = _remote(src, out, send_sems, recv_sems, 0, (x, y, 1 - c))
        cp.start()
        cp.wait()

    return pl.pallas_call(
        body, name="swap_reduced",
        in_specs=[ANY], out_specs=ANY,
        out_shape=jax.ShapeDtypeStruct((H, C), half_rows.dtype),
        scratch_shapes=[pltpu.SemaphoreType.DMA((1,)), pltpu.SemaphoreType.DMA((1,))],
    )(half_rows)


def _allreduce_small(block):
    R, C = block.shape
    n_dev = 8

    def body(src, out, slots, send_sems, recv_sems):
        x, y, c, _ = _position()
        me = 4 * x + 2 * y + c
        slots[me] = src[...]
        sends = []
        for r in range(1, n_dev):
            peer = (x ^ (r >> 2), y ^ ((r >> 1) & 1), c ^ (r & 1))
            cp = _remote(src, slots.at[me], send_sems, recv_sems, r - 1, peer)
            cp.start()
            sends.append(cp)
        for r in range(1, n_dev):
            theirs = slots.at[me ^ r]
            _remote(theirs, theirs, send_sems, recv_sems, r - 1, (x, y, c)).wait_recv()
        for cp in sends:
            cp.wait_send()
        acc = slots[0]
        for d in range(1, n_dev):
            acc = acc + slots[d]
        out[...] = acc

    return pl.pallas_call(
        body, name="allreduce_small",
        in_specs=[pl.BlockSpec(memory_space=pltpu.VMEM)], out_specs=pl.BlockSpec(memory_space=pltpu.VMEM),
        out_shape=jax.ShapeDtypeStruct((R, C), F32),
        scratch_shapes=[pltpu.VMEM((n_dev, R, C), F32), pltpu.SemaphoreType.DMA((7,)), pltpu.SemaphoreType.DMA((7,))],
    )(block)


def _rel_bucket(dist):
    max_exact = REL_BUCKETS // 2
    d = jnp.maximum(dist, 1).astype(F32)
    large = max_exact + (jnp.log(d / max_exact) / math.log(REL_MAX_DIST / max_exact)
                         * (REL_BUCKETS - max_exact)).astype(jnp.int32)
    large = jnp.minimum(large, REL_BUCKETS - 1)
    return jnp.where(dist < max_exact, dist, large)


def _bucket_onehot():
    T = SWA_BLOCK
    dist = (jnp.arange(T)[:, None] + T) - jnp.arange(2 * T)[None, :]
    bucket = _rel_bucket(jnp.maximum(dist, 0)).reshape(1, T * 2 * T)
    return (bucket == jnp.arange(REL_BUCKETS)[:, None]).astype(BF16)


_BUF = (("ffn1_w1", "t"), ("ffn1_w3", "t"), ("ffn1_w2", "n"), ("ffn2_w1", "t"), ("ffn2_w3", "t"),
        ("ffn2_w2", "n"), ("w_in", "t"), ("w_out", "n"), ("w_branch_swa", "tw"), ("w_branch_sb", "tw"))


def _to_rows(name_kind, w, D):
    kind = name_kind[1]
    if kind == "n":
        return w
    if kind == "t":
        return w.T
    return w.T.reshape(-1, D)


def _from_rows(name_kind, rows, width):
    kind = name_kind[1]
    if kind == "n":
        return rows
    if kind == "t":
        return rows.T
    return rows.reshape(-1, width).T


def kernel(x, norm_ffn1, ffn1_w1, ffn1_w3, ffn1_w2, norm_mix, w_in, swa_sinks, rel_bias, w_branch_swa, w_branch_sb, w_out, norm_ffn2, ffn2_w1, ffn2_w3, ffn2_w2, norm_final, loss_target, m_norm_ffn1, m_ffn1_w1, m_ffn1_w3, m_ffn1_w2, m_norm_mix, m_w_in, m_swa_sinks, m_rel_bias, m_w_branch_swa, m_w_branch_sb, m_w_out, m_norm_ffn2, m_ffn2_w1, m_ffn2_w3, m_ffn2_w2, m_norm_final, v_norm_ffn1, v_ffn1_w1, v_ffn1_w3, v_ffn1_w2, v_norm_mix, v_w_in, v_swa_sinks, v_rel_bias, v_w_branch_swa, v_w_branch_sb, v_w_out, v_norm_ffn2, v_ffn2_w1, v_ffn2_w3, v_ffn2_w2, v_norm_final):
    names = ["norm_ffn1", "ffn1_w1", "ffn1_w3", "ffn1_w2", "norm_mix", "w_in", "swa_sinks", "rel_bias",
             "w_branch_swa", "w_branch_sb", "w_out", "norm_ffn2", "ffn2_w1", "ffn2_w3", "ffn2_w2", "norm_final"]
    W = dict(zip(names, [norm_ffn1, ffn1_w1, ffn1_w3, ffn1_w2, norm_mix, w_in, swa_sinks, rel_bias,
                         w_branch_swa, w_branch_sb, w_out, norm_ffn2, ffn2_w1, ffn2_w3, ffn2_w2, norm_final]))
    M = dict(zip(names, [m_norm_ffn1, m_ffn1_w1, m_ffn1_w3, m_ffn1_w2, m_norm_mix, m_w_in, m_swa_sinks, m_rel_bias,
                         m_w_branch_swa, m_w_branch_sb, m_w_out, m_norm_ffn2, m_ffn2_w1, m_ffn2_w3, m_ffn2_w2,
                         m_norm_final]))
    V = dict(zip(names, [v_norm_ffn1, v_ffn1_w1, v_ffn1_w3, v_ffn1_w2, v_norm_mix, v_w_in, v_swa_sinks, v_rel_bias,
                         v_w_branch_swa, v_w_branch_sb, v_w_out, v_norm_ffn2, v_ffn2_w1, v_ffn2_w3, v_ffn2_w2,
                         v_norm_final]))
    xs = x[0]
    target = loss_target[0]
    S, D = xs.shape
    QW = SWA_Q_HEADS * HEAD_DIM
    KW = SWA_KV_HEADS * HEAD_DIM
    BW = SB_HEADS * HEAD_DIM
    QKV = QW + 2 * KW + 3 * BW

    pieces = [_to_rows(nk, W[nk[0]][0], D) for nk in _BUF]
    sizes = [p.shape[0] for p in pieces]
    offs = [0]
    for s in sizes:
        offs.append(offs[-1] + s)
    shard = jnp.concatenate(pieces, axis=0).astype(BF16)
    chip = 2 * lax.axis_index("x") + lax.axis_index("y")
    gathered = lax.dynamic_update_slice(_gather_weights(shard), shard[None], (chip, 0, 0))

    def full(i):
        return gathered[:, offs[i]:offs[i + 1], :].reshape(N_CHIPS * sizes[i], D)

    f1w1, f1w3, f1w2, f2w1, f2w3, f2w2, w_in_t, w_out_f = [full(i) for i in range(8)]
    wa_t = full(8).reshape(D, QW)
    wb_t = full(9).reshape(D, BW)

    g1, gmix, g3 = W["norm_ffn1"], W["norm_mix"], W["norm_ffn2"]
    gf = W["norm_final"].reshape(1, D)

    x1, h1, a1, b1 = _ffn_fwd(xs, g1, f1w1, f1w3, f1w2, "ffn1_fwd")
    o0 = QW + 2 * KW
    rows = jnp.arange(w_in_t.shape[0])
    is_q = (rows < QW) | ((rows >= o0) & (rows < o0 + BW))
    w_in_s = w_in_t * jnp.where(is_q, QK_SCALE, 1.0).astype(BF16)[:, None]
    qkv_a = _norm_proj_heads(x1, gmix, w_in_s[:o0], SWA_BLOCK, "proj_swa")
    qkv_b = _norm_proj_heads(x1, gmix, w_in_s[o0:QKV], SB_BLOCK, "proj_sb")
    gates, h2 = _norm_matmul_nt(x1, gmix, w_in_t[QKV:], F32, "proj_gates")

    onehot = _bucket_onehot()
    bias = _bias_expand(W["rel_bias"].T, onehot).reshape(SWA_Q_HEADS, SWA_BLOCK, 2 * SWA_BLOCK)
    sinks = W["swa_sinks"].reshape(SWA_Q_HEADS)
    oa_t = _swa_fwd(qkv_a, bias, sinks)
    ob_t, saved_sb = _sb_fwd(qkv_b)

    x2, merged, ba, bb = _merge_fwd(x1, gates, oa_t, ob_t, wa_t, wb_t, w_out_f)
    x3, h3, a2, b2 = _ffn_fwd(x2, g3, f2w1, f2w3, f2w2, "ffn2_fwd")
    loss_part, dx3, dgf = _final_loss(x3, gf, target)

    dx2, dg3, dz2, da2, db2, u2 = _ffn_bwd(dx3, x2, g3, a2, b2, f2w1, f2w3, f2w2, "ffn2_bwd")
    grads = {}
    grads["ffn2_w1"] = _tn_matmul(da2, h3, "ffn2_dw1")
    grads["ffn2_w3"] = _tn_matmul(db2, h3, "ffn2_dw3")
    grads["ffn2_w2"] = _tn_matmul(u2, dz2, "ffn2_dw2")

    dx2b, dba, dbb, dgates, doa_t, dob_t = _merge_bwd(dx2, gates, ba, bb, wa_t.T, wb_t.T, w_out_f,
                                                      SWA_BLOCK, SB_BLOCK)
    grads["w_out"] = _tn_matmul(merged, dx2b, "dw_out")
    grads["w_branch_swa"] = _heads_matmul(oa_t, dba, 1.0, "dw_branch_swa").T
    grads["w_branch_sb"] = _heads_matmul(ob_t, dbb, 1.0, "dw_branch_sb").T

    dqb_t, dkb_t, dvb_t = _sb_bwd(qkv_b, dob_t, saved_sb)
    dqa_t, dka_t, dva_t, dbias, dsink_rows = _swa_bwd(qkv_a, bias, sinks, doa_t, oa_t)
    d_rel = _bias_reduce(dbias.reshape(SWA_Q_HEADS, -1), onehot).T
    d_sinks = jnp.sum(dsink_rows, axis=(1, 2))

    dheads = [(dqa_t, QK_SCALE, "q_swa"), (dka_t, 1.0, "k_swa"), (dva_t, 1.0, "v_swa"),
              (dqb_t, QK_SCALE, "q_sb"), (dkb_t, 1.0, "k_sb"), (dvb_t, 1.0, "v_sb")]
    grads["w_in"] = jnp.concatenate([_heads_matmul(a, h2, sc, "dw_in_" + nm).astype(BF16) for a, sc, nm in dheads]
                                    + [_tn_matmul(dgates, h2, "dw_in_gates")], axis=0)
    row0, pieces_in = 0, []
    for a, _, _ in dheads:
        pieces_in.append((a, row0))
        row0 += a.shape[0] * HEAD_DIM
    dx1, dgmix = _proj_bwd(pieces_in, dgates, w_in_s, x1, gmix, dx2)

    dx0, dg1, dz1, da1, db1, u1 = _ffn_bwd(dx1, xs, g1, a1, b1, f1w1, f1w3, f1w2, "ffn1_bwd")
    grads["ffn1_w1"] = _tn_matmul(da1, h1, "ffn1_dw1")
    grads["ffn1_w3"] = _tn_matmul(db1, h1, "ffn1_dw3")
    grads["ffn1_w2"] = _tn_matmul(u1, dz1, "ffn1_dw2")

    gparts = [grads[nk[0]].astype(BF16).reshape(N_CHIPS, sizes[i], D) for i, nk in enumerate(_BUF)]
    gbuf = jnp.concatenate(gparts, axis=1)
    R = gbuf.shape[1]
    half = R // 2
    c = lax.axis_index("c")
    mine = 2 * lax.axis_index("x") + lax.axis_index("y")
    from_sibling = _swap_halves(gbuf)
    my_half = lax.dynamic_slice_in_dim(gbuf, c * half, half, axis=1)
    chip_sum, chip_sum16 = _add_halves(my_half, from_sibling, "add_sibling")
    received = _scatter_to_owners(chip_sum16)
    own = lax.dynamic_index_in_dim(chip_sum, mine, axis=0, keepdims=False)
    my_rows = _add_received(own, received, "add_chips")
    their_rows = _swap_reduced(my_rows)
    reduced = jnp.concatenate([jnp.where(c == 0, my_rows, their_rows), jnp.where(c == 0, their_rows, my_rows)], axis=0)

    small_rows = [dg1, dgmix, dg3, dgf,
                  jnp.pad(d_sinks.reshape(1, -1), ((0, 0), (0, D - SWA_Q_HEADS))),
                  jnp.pad(d_rel.reshape(1, -1), ((0, 0), (0, D - REL_BUCKETS * SWA_Q_HEADS))),
                  jnp.pad(loss_part, ((0, 0), (0, D - 1))), jnp.zeros((1, D), F32)]
    small = _allreduce_small(jnp.concatenate(small_rows, axis=0))
    loss = small[6, 0]

    G = {}
    for i, nk in enumerate(_BUF):
        G[nk[0]] = _from_rows(nk, reduced[offs[i]:offs[i + 1]], W[nk[0]].shape[1])[None]
    G["norm_ffn1"], G["norm_mix"], G["norm_ffn2"] = small[0:1], small[1:2], small[2:3]
    G["norm_final"] = small[3]
    G["swa_sinks"] = small[4:5, :SWA_Q_HEADS]
    G["rel_bias"] = small[5, :REL_BUCKETS * SWA_Q_HEADS].reshape(REL_BUCKETS, SWA_Q_HEADS)

    delta, new_m, new_v = {}, {}, {}
    small_names = ["norm_ffn1", "norm_mix", "norm_ffn2", "norm_final", "swa_sinks", "rel_bias"]

    def pack(d):
        return jnp.concatenate([jnp.pad(d[n].reshape(1, -1), ((0, 0), (0, D - d[n].size))) for n in small_names]
                               + [jnp.zeros((2, D), F32)], axis=0)

    sd, sm, sv = _adamw(pack(W), pack(G), pack(M), pack(V), "adamw_small")
    for r, n in enumerate(small_names):
        for dst, src in ((delta, sd), (new_m, sm), (new_v, sv)):
            dst[n] = src[r, :W[n].size].reshape(W[n].shape)
    for nk in _BUF:
        n = nk[0]
        shp = W[n].shape
        two_d = (shp[1], shp[2])
        d_, m_, v_ = _adamw(W[n].reshape(two_d), G[n].reshape(two_d), M[n].reshape(two_d), V[n].reshape(two_d),
                            "adamw_" + n)
        delta[n], new_m[n], new_v[n] = d_.reshape(shp), m_.reshape(shp), v_.reshape(shp)

    return (loss, dx0[None], *[G[n] for n in names], *[delta[n] for n in names],
            *[new_m[n] for n in names], *[new_v[n] for n in names])
```

```python
import functools
import math

import jax
import jax.numpy as jnp
from jax import lax
from jax.experimental import pallas as pl
from jax.experimental.pallas import tpu as pltpu

F32, BF16 = jnp.float32, jnp.bfloat16
MESH_ID = pl.DeviceIdType.MESH
ANY = pl.BlockSpec(memory_space=pl.ANY)

RMS_EPS = 1e-6
HEAD_DIM = 64
SWA_Q_HEADS, SWA_KV_HEADS, SWA_GROUP = 8, 2, 4
SWA_BLOCK = 128
SB_HEADS = 8
SB_BLOCK = 256
REL_BUCKETS, REL_MAX_DIST = 32, 128
NEG_BIG = -1e30
QK_SCALE = HEAD_DIM ** -0.5
ADAM_LR, ADAM_B1, ADAM_B2, ADAM_EPS, ADAM_WD, ADAM_STEP = 0.001, 0.9, 0.999, 1e-08, 0.01, 10

N_CHIPS = 4
TOKEN_TILE = 512
MATMUL_TOKEN_TILE = 1024
WGRAD_ROW_TILES = (2176, 1408, 1024, 256)
FF_TILE = 1408
FFN_TOKEN_TILE = 512
FF_BWD_TILE = 256
VMEM_LIMIT = 56 * 1024 * 1024


def _cp(*sem):
    return pltpu.CompilerParams(dimension_semantics=sem, vmem_limit_bytes=VMEM_LIMIT)


def _nn(a, b):
    return jnp.dot(a, b, preferred_element_type=F32)


def _nt(a, b):
    return lax.dot_general(a, b, (((1,), (1,)), ((), ())), preferred_element_type=F32)


def _tn(a, b):
    return lax.dot_general(a, b, (((0,), (0,)), ((), ())), preferred_element_type=F32)


def _norm_fwd(x, g):
    return x * lax.rsqrt(jnp.mean(x * x, axis=-1, keepdims=True) + RMS_EPS) * g


def _norm_bwd(x, g, dh):
    r = lax.rsqrt(jnp.mean(x * x, axis=-1, keepdims=True) + RMS_EPS)
    xh = x * r
    dxh = dh * g
    dx = r * (dxh - xh * jnp.mean(dxh * xh, axis=-1, keepdims=True))
    return dx, jnp.sum(dh * xh, axis=0, keepdims=True)


def _softplus(z):
    return jnp.maximum(z, 0.0) + jnp.log(1.0 + jnp.exp(-jnp.abs(z)))


def _ffn_fwd(x, g, w1t, w3t, w2, name):
    S, D = x.shape
    F = w2.shape[0]
    tm, tf = min(FFN_TOKEN_TILE, S), FF_TILE
    nj = F // tf

    def body(x_ref, g_ref, w1_ref, w3_ref, w2_ref, xo_ref, h_ref, a_ref, b_ref, hs, acc):
        j = pl.program_id(1)

        @pl.when(j == 0)
        def _():
            hb = _norm_fwd(x_ref[...], g_ref[...]).astype(BF16)
            hs[...] = hb
            h_ref[...] = hb
            acc[...] = jnp.zeros_like(acc)

        h = hs[...]
        a = _nt(h, w1_ref[...])
        b = _nt(h, w3_ref[...])
        a_ref[...] = a.astype(BF16)
        b_ref[...] = b.astype(BF16)
        u = a * jax.nn.sigmoid(a) * b
        acc[...] += _nn(u.astype(BF16), w2_ref[...])

        @pl.when(j == nj - 1)
        def _():
            xo_ref[...] = x_ref[...] + 0.5 * acc[...]

    return pl.pallas_call(
        body, name=name, grid=(S // tm, nj),
        in_specs=[pl.BlockSpec((tm, D), lambda i, j: (i, 0)),
                  pl.BlockSpec((1, D), lambda i, j: (0, 0)),
                  pl.BlockSpec((tf, D), lambda i, j: (j, 0)),
                  pl.BlockSpec((tf, D), lambda i, j: (j, 0)),
                  pl.BlockSpec((tf, D), lambda i, j: (j, 0))],
        out_specs=[pl.BlockSpec((tm, D), lambda i, j: (i, 0)),
                   pl.BlockSpec((tm, D), lambda i, j: (i, 0)),
                   pl.BlockSpec((tm, tf), lambda i, j: (i, j)),
                   pl.BlockSpec((tm, tf), lambda i, j: (i, j))],
        out_shape=[jax.ShapeDtypeStruct((S, D), F32), jax.ShapeDtypeStruct((S, D), BF16),
                   jax.ShapeDtypeStruct((S, F), BF16), jax.ShapeDtypeStruct((S, F), BF16)],
        scratch_shapes=[pltpu.VMEM((tm, D), BF16), pltpu.VMEM((tm, D), F32)],
        compiler_params=_cp("arbitrary", "arbitrary"),
    )(x, g, w1t, w3t, w2)


def _ffn_bwd(dxo, x, g, a, b, w1t, w3t, w2, name):
    S, D = x.shape
    F = w2.shape[0]
    tm, tf = min(MATMUL_TOKEN_TILE, S), FF_BWD_TILE
    ni, nj = S // tm, F // tf

    def body(dxo_ref, x_ref, g_ref, a_ref, b_ref, w1_ref, w3_ref, w2_ref,
             dx_ref, dg_ref, dz_ref, da_ref, db_ref, u_ref, dzs, acc):
        i, j = pl.program_id(0), pl.program_id(1)

        @pl.when(j == 0)
        def _():
            dzb = (0.5 * dxo_ref[...]).astype(BF16)
            dzs[...] = dzb
            dz_ref[...] = dzb
            acc[...] = jnp.zeros_like(acc)

        du = _nt(dzs[...], w2_ref[...])
        av = a_ref[...].astype(F32)
        bv = b_ref[...].astype(F32)
        s = jax.nn.sigmoid(av)
        silu = av * s
        db = (du * silu).astype(BF16)
        da = (du * bv * (s * (1.0 + av * (1.0 - s)))).astype(BF16)
        da_ref[...] = da
        db_ref[...] = db
        u_ref[...] = (silu * bv).astype(BF16)
        acc[...] += _nn(da, w1_ref[...]) + _nn(db, w3_ref[...])

        @pl.when(j == nj - 1)
        def _():
            dx, dg = _norm_bwd(x_ref[...], g_ref[...], acc[...])
            dx_ref[...] = dxo_ref[...] + dx

            @pl.when(i == 0)
            def _():
                dg_ref[...] = dg

            @pl.when(i > 0)
            def _():
                dg_ref[...] += dg

    row = pl.BlockSpec((tm, D), lambda i, j: (i, 0))
    wsp = pl.BlockSpec((tf, D), lambda i, j: (j, 0))
    col = pl.BlockSpec((tm, tf), lambda i, j: (i, j))
    vec = pl.BlockSpec((1, D), lambda i, j: (0, 0))
    return pl.pallas_call(
        body, name=name, grid=(ni, nj),
        in_specs=[row, row, vec, col, col, wsp, wsp, wsp],
        out_specs=[row, vec, row, col, col, col],
        out_shape=[jax.ShapeDtypeStruct((S, D), F32), jax.ShapeDtypeStruct((1, D), F32),
                   jax.ShapeDtypeStruct((S, D), BF16), jax.ShapeDtypeStruct((S, F), BF16),
                   jax.ShapeDtypeStruct((S, F), BF16), jax.ShapeDtypeStruct((S, F), BF16)],
        scratch_shapes=[pltpu.VMEM((tm, D), BF16), pltpu.VMEM((tm, D), F32)],
        compiler_params=_cp("arbitrary", "arbitrary"),
    )(dxo, x, g, a, b, w1t, w3t, w2)


def _tn_matmul(a, b, name):
    S, M = a.shape
    N = b.shape[1]
    ts = min(MATMUL_TOKEN_TILE, S)
    tmm = next(t for t in WGRAD_ROW_TILES if M % t == 0)
    ns = S // ts

    def body(a_ref, b_ref, o_ref, acc):
        s = pl.program_id(1)
        part = _tn(a_ref[...], b_ref[...])

        @pl.when(s == 0)
        def _():
            acc[...] = part

        @pl.when(s > 0)
        def _():
            acc[...] += part

        @pl.when(s == ns - 1)
        def _():
            o_ref[...] = acc[...].astype(BF16)

    return pl.pallas_call(
        body, name=name, grid=(M // tmm, ns),
        in_specs=[pl.BlockSpec((ts, tmm), lambda m, s: (s, m)),
                  pl.BlockSpec((ts, N), lambda m, s: (s, 0))],
        out_specs=pl.BlockSpec((tmm, N), lambda m, s: (m, 0)),
        out_shape=jax.ShapeDtypeStruct((M, N), BF16),
        scratch_shapes=[pltpu.VMEM((tmm, N), F32)],
        compiler_params=_cp("arbitrary", "arbitrary"),
    )(a, b)


def _norm_matmul_nt(x, g, wt, out_dtype, name):
    S, D = x.shape
    N = wt.shape[0]
    tm = min(MATMUL_TOKEN_TILE, S)
    tn = next(t for t in (1024, 768, 256) if N % t == 0)

    def body(x_ref, g_ref, w_ref, o_ref, h_ref, hs):
        @pl.when(pl.program_id(1) == 0)
        def _():
            hb = _norm_fwd(x_ref[...], g_ref[...]).astype(BF16)
            hs[...] = hb
            h_ref[...] = hb

        o_ref[...] = _nt(hs[...], w_ref[...]).astype(out_dtype)

    return pl.pallas_call(
        body, name=name, grid=(S // tm, N // tn),
        in_specs=[pl.BlockSpec((tm, D), lambda i, j: (i, 0)),
                  pl.BlockSpec((1, D), lambda i, j: (0, 0)),
                  pl.BlockSpec((tn, D), lambda i, j: (j, 0))],
        out_specs=[pl.BlockSpec((tm, tn), lambda i, j: (i, j)),
                   pl.BlockSpec((tm, D), lambda i, j: (i, 0))],
        out_shape=[jax.ShapeDtypeStruct((S, N), out_dtype), jax.ShapeDtypeStruct((S, D), BF16)],
        scratch_shapes=[pltpu.VMEM((tm, D), BF16)],
        compiler_params=_cp("arbitrary", "arbitrary"),
    )(x, g, wt)


def _heads_tile(ref):
    Hh, nbk = ref.shape[0], ref.shape[1]
    return jnp.concatenate([jnp.concatenate([ref[h, b] for b in range(nbk)], axis=1) for h in range(Hh)], axis=0)


def _store_heads(ref, val):
    Hh, nbk, dh, T = ref.shape
    for h in range(Hh):
        for b in range(nbk):
            ref[h, b] = val[h * dh:(h + 1) * dh, b * T:(b + 1) * T].astype(ref.dtype)


def _norm_proj_heads(x, g, w_rows, T, name):
    S, D = x.shape
    N = w_rows.shape[0]
    tm, tn = min(MATMUL_TOKEN_TILE, S), 768

    def body(x_ref, g_ref, w_ref, o_ref, hs):
        @pl.when(pl.program_id(1) == 0)
        def _():
            hs[...] = _norm_fwd(x_ref[...], g_ref[...]).astype(BF16)

        _store_heads(o_ref, _nt(w_ref[...], hs[...]))

    return pl.pallas_call(
        body, name=name, grid=(S // tm, N // tn),
        in_specs=[pl.BlockSpec((tm, D), lambda i, j: (i, 0)),
                  pl.BlockSpec((1, D), lambda i, j: (0, 0)),
                  pl.BlockSpec((tn, D), lambda i, j: (j, 0))],
        out_specs=pl.BlockSpec((tn // HEAD_DIM, tm // T, HEAD_DIM, T), lambda i, j: (j, i, 0, 0)),
        out_shape=jax.ShapeDtypeStruct((N // HEAD_DIM, S // T, HEAD_DIM, T), BF16),
        scratch_shapes=[pltpu.VMEM((tm, D), BF16)],
        compiler_params=_cp("arbitrary", "arbitrary"),
    )(x, g, w_rows)


def _heads_matmul(at, b, scale, name):
    Hh, nb, dh, T = at.shape
    S, N = b.shape
    ts = min(MATMUL_TOKEN_TILE, S)
    ns = S // ts

    def body(a_ref, b_ref, o_ref):
        s = pl.program_id(0)
        a = _heads_tile(a_ref)
        part = _nn((a if scale == 1.0 else a * scale).astype(BF16), b_ref[...])

        @pl.when(s == 0)
        def _():
            o_ref[...] = part

        @pl.when(s > 0)
        def _():
            o_ref[...] += part

    return pl.pallas_call(
        body, name=name, grid=(ns,),
        in_specs=[pl.BlockSpec((Hh, ts // T, dh, T), lambda s: (0, s, 0, 0)),
                  pl.BlockSpec((ts, N), lambda s: (s, 0))],
        out_specs=pl.BlockSpec((Hh * dh, N), lambda s: (0, 0)),
        out_shape=jax.ShapeDtypeStruct((Hh * dh, N), F32),
        compiler_params=_cp("arbitrary"),
    )(at, b)


def _proj_bwd(pieces, dgates, w_rows, x, g, dres):
    S, D = x.shape
    tm = min(TOKEN_TILE, S)
    n_p = len(pieces)
    gate_row = w_rows.shape[0] - dgates.shape[1]

    def body(*refs):
        p_refs = refs[:n_p]
        dgt_ref, w_ref, x_ref, g_ref, dres_ref, dx_ref, dg_ref = refs[n_p:]
        i = pl.program_id(0)
        dh = _nn(dgt_ref[...], w_ref[gate_row:, :])
        for p_ref, (arr, row0) in zip(p_refs, pieces):
            rows = arr.shape[0] * arr.shape[2]
            dh += _tn(_heads_tile(p_ref).astype(BF16), w_ref[row0:row0 + rows, :])
        dx, dg = _norm_bwd(x_ref[...], g_ref[...], dh)
        dx_ref[...] = dres_ref[...] + dx

        @pl.when(i == 0)
        def _():
            dg_ref[...] = dg

        @pl.when(i > 0)
        def _():
            dg_ref[...] += dg

    row = pl.BlockSpec((tm, D), lambda i: (i, 0))
    vec = pl.BlockSpec((1, D), lambda i: (0, 0))
    p_specs = [pl.BlockSpec((a.shape[0], tm // a.shape[3], a.shape[2], a.shape[3]), lambda i: (0, i, 0, 0))
               for a, _ in pieces]
    return pl.pallas_call(
        body, name="proj_bwd", grid=(S // tm,),
        in_specs=p_specs + [pl.BlockSpec((tm, dgates.shape[1]), lambda i: (i, 0)),
                            pl.BlockSpec(w_rows.shape, lambda i: (0, 0), pipeline_mode=pl.Buffered(1)),
                            row, vec, row],
        out_specs=[row, vec],
        out_shape=[jax.ShapeDtypeStruct((S, D), F32), jax.ShapeDtypeStruct((1, D), F32)],
        compiler_params=_cp("arbitrary"),
    )(*[a for a, _ in pieces], dgates, w_rows, x, g, dres)


def _merge_fwd(x1, gates, oa_t, ob_t, wat, wbt, w_out):
    S, D = x1.shape
    W = wat.shape[1]
    tm = min(TOKEN_TILE, S)

    def body(x_ref, ga_ref, gb_ref, oa_ref, ob_ref, wa_ref, wb_ref, wo_ref,
             x2_ref, mg_ref, ba_ref, bb_ref):
        ba = _nt(_heads_tile(oa_ref).T.astype(BF16), wa_ref[...])
        bb = _nt(_heads_tile(ob_ref).T.astype(BF16), wb_ref[...])
        merged = jax.nn.sigmoid(ga_ref[...]) * ba + jax.nn.sigmoid(gb_ref[...]) * bb
        mb = merged.astype(BF16)
        mg_ref[...] = mb
        ba_ref[...] = ba.astype(BF16)
        bb_ref[...] = bb.astype(BF16)
        x2_ref[...] = x_ref[...] + _nn(mb, wo_ref[...])

    row = pl.BlockSpec((tm, D), lambda i: (i, 0))
    full = lambda r, c: pl.BlockSpec((r, c), lambda i: (0, 0))
    heads = lambda a: pl.BlockSpec((a.shape[0], tm // a.shape[3], a.shape[2], a.shape[3]), lambda i: (0, i, 0, 0))
    return pl.pallas_call(
        body, name="merge_fwd", grid=(S // tm,),
        in_specs=[row, pl.BlockSpec((tm, D), lambda i: (i, 0)), pl.BlockSpec((tm, D), lambda i: (i, 1)),
                  heads(oa_t), heads(ob_t), full(D, W), full(D, W), full(D, D)],
        out_specs=[row, row, row, row],
        out_shape=[jax.ShapeDtypeStruct((S, D), F32)] + [jax.ShapeDtypeStruct((S, D), BF16)] * 3,
        compiler_params=_cp("arbitrary"),
    )(x1, gates, gates, oa_t, ob_t, wat, wbt, w_out)


def _merge_bwd(dx2, gates, ba, bb, wa, wb, w_out, t_a, t_b):
    S, D = dx2.shape
    W = wa.shape[0]
    tm = min(TOKEN_TILE, S)
    Hh = W // HEAD_DIM

    def body(dx_ref, ga_ref, gb_ref, ba_ref, bb_ref, wa_ref, wb_ref, wo_ref,
             dxb_ref, dba_ref, dbb_ref, dgt_ref, doa_ref, dob_ref):
        dxb = dx_ref[...].astype(BF16)
        dxb_ref[...] = dxb
        dm = _nt(dxb, wo_ref[...])
        sa = jax.nn.sigmoid(ga_ref[...])
        sb = jax.nn.sigmoid(gb_ref[...])
        dba = (dm * sa).astype(BF16)
        dbb = (dm * sb).astype(BF16)
        dba_ref[...] = dba
        dbb_ref[...] = dbb
        dgt_ref[:, :D] = (dm * ba_ref[...].astype(F32) * sa * (1.0 - sa)).astype(BF16)
        dgt_ref[:, D:] = (dm * bb_ref[...].astype(F32) * sb * (1.0 - sb)).astype(BF16)
        _store_heads(doa_ref, _nt(wa_ref[...], dba))
        _store_heads(dob_ref, _nt(wb_ref[...], dbb))

    row = pl.BlockSpec((tm, D), lambda i: (i, 0))
    full = lambda r, c: pl.BlockSpec((r, c), lambda i: (0, 0))
    heads = lambda T: pl.BlockSpec((Hh, tm // T, HEAD_DIM, T), lambda i: (0, i, 0, 0))
    return pl.pallas_call(
        body, name="merge_bwd", grid=(S // tm,),
        in_specs=[row, pl.BlockSpec((tm, D), lambda i: (i, 0)), pl.BlockSpec((tm, D), lambda i: (i, 1)),
                  row, row, full(W, D), full(W, D), full(D, D)],
        out_specs=[row, row, row, pl.BlockSpec((tm, 2 * D), lambda i: (i, 0)), heads(t_a), heads(t_b)],
        out_shape=[jax.ShapeDtypeStruct((S, D), BF16)] * 3 + [jax.ShapeDtypeStruct((S, 2 * D), BF16),
                   jax.ShapeDtypeStruct((Hh, S // t_a, HEAD_DIM, t_a), F32),
                   jax.ShapeDtypeStruct((Hh, S // t_b, HEAD_DIM, t_b), BF16)],
        compiler_params=_cp("arbitrary"),
    )(dx2, gates, gates, ba, bb, wa, wb, w_out)


def _final_loss(x3, gf, target):
    S, D = x3.shape
    tm = min(TOKEN_TILE, S)

    def body(x_ref, g_ref, t_ref, loss_ref, dx_ref, dg_ref):
        i = pl.program_id(0)
        x = x_ref[...]
        g = g_ref[...]
        e = _norm_fwd(x, g) - t_ref[...]
        part = 0.5 * jnp.sum(jnp.mean(e * e, axis=-1, keepdims=True), axis=0, keepdims=True)
        dx, dg = _norm_bwd(x, g, e * (1.0 / D))
        dx_ref[...] = dx

        @pl.when(i == 0)
        def _():
            loss_ref[...] = part
            dg_ref[...] = dg

        @pl.when(i > 0)
        def _():
            loss_ref[...] += part
            dg_ref[...] += dg

    row = pl.BlockSpec((tm, D), lambda i: (i, 0))
    vec = pl.BlockSpec((1, D), lambda i: (0, 0))
    return pl.pallas_call(
        body, name="final_loss", grid=(S // tm,),
        in_specs=[row, vec, row],
        out_specs=[pl.BlockSpec((1, 1), lambda i: (0, 0)), row, vec],
        out_shape=[jax.ShapeDtypeStruct((1, 1), F32), jax.ShapeDtypeStruct((S, D), F32),
                   jax.ShapeDtypeStruct((1, D), F32)],
        compiler_params=_cp("arbitrary"),
    )(x3, gf, target)


SB_FWD_HEAD_GROUP = 8
SB_HEAD_GROUP = 4
LANES = 128


def _tri(T, kind):
    r = lax.broadcasted_iota(jnp.int32, (T, T), 0)
    c = lax.broadcasted_iota(jnp.int32, (T, T), 1)
    return {"after": r > c, "upto": r <= c, "before": r < c}[kind].astype(BF16)


def _lane(v, j):
    return jnp.broadcast_to(v[:, j:j + 1], (v.shape[0], LANES))


def _t_bf16(x):
    return x.astype(F32).T.astype(BF16)


def _wide(v, T):
    return jnp.tile(v, (1, T // LANES))


SB_SLOTS = 3
SB_FWD_SLOTS = 2
COPY_PARTS = 4


class _split_copy:
    def __init__(self, src, dst, sems):
        n = src.shape[0] // COPY_PARTS
        self.parts = [pltpu.make_async_copy(src.at[pl.ds(r * n, n)], dst.at[pl.ds(r * n, n)], sems.at[r])
                      for r in range(COPY_PARTS)]

    def start(self):
        for cp in self.parts:
            cp.start()

    def wait(self):
        for cp in self.parts:
            cp.wait()


def _sb_pair(i, kb):
    return (i * (i + 1)) // 2 + kb


def _sb_fwd(qkv):
    H3, nb, dh, T = qkv.shape
    H = H3 // 3
    HG = SB_FWD_HEAD_GROUP
    assert HG == H, "one head group: a saved tile holds all the heads"
    n_pairs = (nb * (nb + 1)) // 2

    def body(q_ref, k_ref, v_ref, o_ref, saved_ref, stage, sems):
        row = lax.broadcasted_iota(jnp.int32, (T, T), 0)
        col = lax.broadcasted_iota(jnp.int32, (T, T), 1)
        tri = col < row
        after = _tri(T, "after")

        def save(slot, pair):
            return _split_copy(stage.at[slot], saved_ref.at[pair], sems.at[slot])

        def blocks(qs, i, kb, step, carry, diag):
            hs = range(HG)
            slot = step % SB_FWD_SLOTS

            @pl.when(step >= SB_FWD_SLOTS)
            def _():
                save(slot, 0).wait()

            z = [_nn(qs[hh], k_ref[hh, kb]) for hh in hs]
            res, ls, first = [None] * HG, [None] * HG, [None] * HG
            for hh in hs:
                sp = _softplus(z[hh])
                if diag:
                    sp = jnp.where(tri, sp, 0.0)
                ls[hh] = z[hh] - sp
                spb = sp.astype(BF16)
                first[hh] = _lane(spb.astype(F32), 0)
                res[hh] = _nn(spb, after)
            out = []
            for hh in hs:
                c, oacc = carry[2 * hh], carry[2 * hh + 1]
                a = jnp.exp(ls[hh] - (res[hh] + _wide(c, T)))
                if diag:
                    a = jnp.where(tri, a, 0.0)
                ab = a.astype(BF16)
                stage[slot, hh, 0] = ab
                stage[slot, hh, 1] = jnp.exp(ls[hh]).astype(BF16)
                out.extend([c + (first[hh] + _lane(res[hh], 0)), oacc + _nt(v_ref[hh, kb], ab)])
            save(slot, _sb_pair(i, kb)).start()
            return tuple(out)

        def qblock(i, step):
            qs = [_t_bf16(q_ref[hh, i]) for hh in range(HG)]
            carry = blocks(qs, i, i, step, (jnp.zeros((T, LANES), F32), jnp.zeros((dh, T), F32)) * HG, True)

            def kstep(t, carry):
                return blocks(qs, i, i - 1 - t, step + 1 + t, carry, False)

            carry = lax.fori_loop(0, i, kstep, carry)
            for hh in range(HG):
                o_ref[hh, i] = carry[2 * hh + 1]
            return step + 1 + i

        lax.fori_loop(0, nb, qblock, 0)
        for slot in range(min(SB_FWD_SLOTS, n_pairs)):
            save(slot, 0).wait()

    ht = lambda part: pl.BlockSpec((HG, nb, dh, T), lambda h: (part, 0, 0, 0), pipeline_mode=pl.Buffered(1))
    return pl.pallas_call(
        body, name="sb_fwd", grid=(1,),
        in_specs=[ht(0), ht(1), ht(2)],
        out_specs=[ht(0), ANY],
        out_shape=[jax.ShapeDtypeStruct((H, nb, dh, T), F32),
                   jax.ShapeDtypeStruct((n_pairs, H, 2, T, T), BF16)],
        scratch_shapes=[pltpu.VMEM((SB_FWD_SLOTS, HG, 2, T, T), BF16),
                        pltpu.SemaphoreType.DMA((SB_FWD_SLOTS, COPY_PARTS))],
        compiler_params=_cp("arbitrary"),
    )(qkv, qkv, qkv)


def _sb_bwd(qkv, dot, saved):
    H3, nb, dh, T = qkv.shape
    H = H3 // 3
    HG = SB_HEAD_GROUP
    n_pairs = (nb * (nb + 1)) // 2

    def body(qt_ref, k_ref, v_ref, dot_ref, saved_ref, dq_ref, dk_ref, dv_ref, stage, sems):
        head0 = pl.program_id(0) * HG
        row = lax.broadcasted_iota(jnp.int32, (T, T), 0)
        col = lax.broadcasted_iota(jnp.int32, (T, T), 1)
        tri = col < row
        before = _tri(T, "before")
        dk_ref[...] = jnp.zeros_like(dk_ref)
        dv_ref[...] = jnp.zeros_like(dv_ref)

        def fetch(slot, pair):
            return _split_copy(saved_ref.at[pair, pl.ds(head0, HG)], stage.at[slot], sems.at[slot])

        for ahead in range(min(SB_SLOTS - 1, n_pairs)):
            fetch(ahead, ahead).start()

        def blocks(qTs, dos, doTs, i, kb, carry, diag):
            hs = range(HG)
            pair = _sb_pair(i, kb)
            slot = pair % SB_SLOTS
            fetch(slot, pair).wait()
            nxt = pair + (SB_SLOTS - 1)

            @pl.when(nxt < n_pairs)
            def _():
                fetch(nxt % SB_SLOTS, nxt).start()

            kT = [k_ref[hh, kb] for hh in hs]
            da = [_nn(dos[hh], v_ref[hh, kb]) for hh in hs]
            g, gb, resg = [None] * HG, [None] * HG, [None] * HG
            for hh in hs:
                g[hh] = stage[slot, hh, 0].astype(F32) * da[hh]
                gb[hh] = g[hh].astype(BF16)
                resg[hh] = _nn(gb[hh], before)
            out = []
            for hh in hs:
                pre_g, dq = carry[2 * hh], carry[2 * hh + 1]
                dz = g[hh] - (g[hh] + (resg[hh] + _wide(pre_g, T))) * stage[slot, hh, 1].astype(F32)
                if diag:
                    dz = jnp.where(tri, dz, 0.0)
                dzb = dz.astype(BF16)
                dk_ref[hh, kb] += _nn(qTs[hh], dzb)
                dv_ref[hh, kb] += _nn(doTs[hh], stage[slot, hh, 0])
                out.extend([pre_g + (_lane(resg[hh], T - 1) + _lane(gb[hh].astype(F32), T - 1)),
                            dq + _nt(kT[hh], dzb)])
            return tuple(out)

        def qblock(i, _):
            qTs = [qt_ref[hh, i] for hh in range(HG)]
            doTs = [dot_ref[hh, i] for hh in range(HG)]
            dos = [_t_bf16(v) for v in doTs]
            carry = (jnp.zeros((T, LANES), F32), jnp.zeros((dh, T), F32)) * HG

            def kstep(kb, carry):
                return blocks(qTs, dos, doTs, i, kb, carry, False)

            carry = lax.fori_loop(0, i, kstep, carry)
            carry = blocks(qTs, dos, doTs, i, i, carry, True)
            for hh in range(HG):
                dq_ref[hh, i] = carry[2 * hh + 1]
            return 0

        lax.fori_loop(0, nb, qblock, 0)

    G = H // HG
    ht = lambda part: pl.BlockSpec((HG, nb, dh, T), lambda h: (h + part * G, 0, 0, 0),
                                   pipeline_mode=pl.Buffered(1))
    return pl.pallas_call(
        body, name="sb_bwd", grid=(G,),
        in_specs=[ht(0), ht(1), ht(2), ht(0), ANY],
        out_specs=[ht(0), ht(0), ht(0)],
        out_shape=[jax.ShapeDtypeStruct((H, nb, dh, T), F32)] * 3,
        scratch_shapes=[pltpu.VMEM((SB_SLOTS, HG, 2, T, T), BF16), pltpu.SemaphoreType.DMA((SB_SLOTS, COPY_PARTS))],
        compiler_params=_cp("arbitrary"),
    )(qkv, qkv, qkv, dot, saved)


def _swa_probs(zp, zc, bias, sink, first):
    T = zp.shape[0]
    row = lax.broadcasted_iota(jnp.int32, (T, T), 0)
    col = lax.broadcasted_iota(jnp.int32, (T, T), 1)
    lp = jnp.where(jnp.logical_and(col > row, jnp.logical_not(first)), zp + bias[:, :T], NEG_BIG)
    lc = jnp.where(col <= row, zc + bias[:, T:], NEG_BIG)
    m = jnp.maximum(jnp.maximum(jnp.max(lp, axis=1, keepdims=True), jnp.max(lc, axis=1, keepdims=True)), sink)
    pp = jnp.exp(lp - m)
    pc = jnp.exp(lc - m)
    ps = jnp.exp(sink - m)
    inv = 1.0 / (jnp.sum(pp, axis=1, keepdims=True) + jnp.sum(pc, axis=1, keepdims=True) + ps)
    return pp * inv, pc * inv, ps * inv


def _swa_specs(nb, dh, T, grp, Hq, Hkv, clamp):
    blk = (lambda n: jnp.minimum(n, nb - 1)) if clamp else (lambda n: n)
    q = pl.BlockSpec((grp, None, dh, T), lambda h, n: (h, blk(n), 0, 0))
    one = lambda first, back: pl.BlockSpec(
        (None, None, dh, T), lambda h, n: (first + h, jnp.maximum(blk(n) - back, 0) if back else blk(n), 0, 0))
    return q, [one(Hq, 1), one(Hq, 0), one(Hq + Hkv, 1), one(Hq + Hkv, 0)]


def _swa_fwd(qkv, bias, sinks):
    Hq, Hkv, grp = SWA_Q_HEADS, SWA_KV_HEADS, SWA_GROUP
    _, nb, dh, T = qkv.shape

    def body(sink_ref, q_ref, kp_ref, kc_ref, vp_ref, vc_ref, bias_ref, o_ref):
        hk, n = pl.program_id(0), pl.program_id(1)
        kp, kc, vp, vc = kp_ref[...], kc_ref[...], vp_ref[...], vc_ref[...]
        qs = [_t_bf16(q_ref[g]) for g in range(grp)]
        zs = [(_nn(q, kp), _nn(q, kc)) for q in qs]
        for g in range(grp):
            pp, pc, _ = _swa_probs(*zs[g], bias_ref[g], sink_ref[hk * grp + g], n == 0)
            o_ref[g] = _nt(vp, pp.astype(BF16)) + _nt(vc, pc.astype(BF16))

    q_spec, kv_specs = _swa_specs(nb, dh, T, grp, Hq, Hkv, False)
    return pl.pallas_call(
        body, name="swa_fwd", grid=(Hkv, nb),
        in_specs=[pl.BlockSpec(memory_space=pltpu.SMEM), q_spec] + kv_specs
                 + [pl.BlockSpec((grp, T, 2 * T), lambda h, n: (h, 0, 0))],
        out_specs=pl.BlockSpec((grp, None, dh, T), lambda h, n: (h, n, 0, 0)),
        out_shape=jax.ShapeDtypeStruct((Hq, nb, dh, T), F32),
        compiler_params=_cp("arbitrary", "arbitrary"),
    )(sinks, qkv, qkv, qkv, qkv, qkv, bias)


def _swa_bwd(qkv, bias, sinks, dot, ot):
    Hq, Hkv, grp = SWA_Q_HEADS, SWA_KV_HEADS, SWA_GROUP
    _, nb, dh, T = qkv.shape

    def body(sink_ref, qt_ref, kp_ref, kc_ref, vp_ref, vc_ref, bias_ref, dot_ref, ot_ref,
             dq_ref, dk_ref, dv_ref, dbias_ref, dsink_ref, ck, cv):
        hk, n = pl.program_id(0), pl.program_id(1)

        @pl.when(n == 0)
        def _():
            dbias_ref[...] = jnp.zeros_like(dbias_ref)
            dsink_ref[...] = jnp.zeros_like(dsink_ref)
            ck[...] = jnp.zeros_like(ck)
            cv[...] = jnp.zeros_like(cv)

        @pl.when(n < nb)
        def _():
            kp, kc, vp, vc = kp_ref[...], kc_ref[...], vp_ref[...], vc_ref[...]
            kprev = jnp.zeros((dh, T), F32)
            vprev = jnp.zeros((dh, T), F32)
            kcur = jnp.zeros((dh, T), F32)
            vcur = jnp.zeros((dh, T), F32)
            qTs = [qt_ref[g] for g in range(grp)]
            qs = [_t_bf16(v) for v in qTs]
            dos = [dot_ref[g].T for g in range(grp)]
            zs = [(_nn(q, kp), _nn(q, kc)) for q in qs]
            dps = [(_nn(do.astype(BF16), vp), _nn(do.astype(BF16), vc)) for do in dos]
            dls, pbs = [], []
            for g in range(grp):
                pp, pc, ps = _swa_probs(*zs[g], bias_ref[g], sink_ref[hk * grp + g], n == 0)
                delta = jnp.sum(dos[g] * ot_ref[g].T, axis=1, keepdims=True)
                dlp = pp * (dps[g][0] - delta)
                dlc = pc * (dps[g][1] - delta)
                dbias_ref[g, :, :T] += dlp
                dbias_ref[g, :, T:] += dlc
                dsink_ref[g] += -ps * delta
                dls.append((dlp.astype(BF16), dlc.astype(BF16)))
                pbs.append((pp.astype(BF16), pc.astype(BF16)))
            for g in range(grp):
                dlpb, dlcb = dls[g]
                doT = dot_ref[g].astype(BF16)
                dq_ref[g] = _nt(kp, dlpb) + _nt(kc, dlcb)
                kprev += _nn(qTs[g], dlpb)
                kcur += _nn(qTs[g], dlcb)
                vprev += _nn(doT, pbs[g][0])
                vcur += _nn(doT, pbs[g][1])
            dk_ref[...] = ck[...] + kprev
            dv_ref[...] = cv[...] + vprev
            ck[...] = kcur
            cv[...] = vcur

        @pl.when(n == nb)
        def _():
            dk_ref[...] = ck[...]
            dv_ref[...] = cv[...]

    qt_spec, kv_specs = _swa_specs(nb, dh, T, grp, Hq, Hkv, True)
    prev = pl.BlockSpec((None, None, dh, T), lambda h, n: (h, jnp.maximum(n - 1, 0), 0, 0))
    per_group = lambda a, b: pl.BlockSpec((grp, a, b), lambda h, n: (h, 0, 0))
    return pl.pallas_call(
        body, name="swa_bwd", grid=(Hkv, nb + 1),
        in_specs=[pl.BlockSpec(memory_space=pltpu.SMEM), qt_spec] + kv_specs
                 + [per_group(T, 2 * T), qt_spec, qt_spec],
        out_specs=[qt_spec, prev, prev, per_group(T, 2 * T), per_group(T, 1)],
        out_shape=[jax.ShapeDtypeStruct((Hq, nb, dh, T), F32), jax.ShapeDtypeStruct((Hkv, nb, dh, T), F32),
                   jax.ShapeDtypeStruct((Hkv, nb, dh, T), F32), jax.ShapeDtypeStruct((Hq, T, 2 * T), F32),
                   jax.ShapeDtypeStruct((Hq, T, 1), F32)],
        scratch_shapes=[pltpu.VMEM((dh, T), F32), pltpu.VMEM((dh, T), F32)],
        compiler_params=_cp("arbitrary", "arbitrary"),
    )(sinks, qkv, qkv, qkv, qkv, qkv, bias, dot, ot)


def _split3(x):
    h1 = x.astype(BF16)
    r1 = x - h1.astype(F32)
    h2 = r1.astype(BF16)
    h3 = (r1 - h2.astype(F32)).astype(BF16)
    return h1, h2, h3


def _bias_expand(rel_t, onehot):
    Hq, NB = rel_t.shape
    L = onehot.shape[1]

    def body(r_ref, oh_ref, o_ref):
        h1, h2, h3 = _split3(r_ref[...])
        oh = oh_ref[...]
        o_ref[...] = _nn(h1, oh) + _nn(h2, oh) + _nn(h3, oh)

    return pl.pallas_call(
        body, name="bias_expand", grid=(1,),
        in_specs=[pl.BlockSpec((Hq, NB), lambda i: (0, 0)), pl.BlockSpec((NB, L), lambda i: (0, 0))],
        out_specs=pl.BlockSpec((Hq, L), lambda i: (0, 0)),
        out_shape=jax.ShapeDtypeStruct((Hq, L), F32),
        compiler_params=_cp("arbitrary"),
    )(rel_t, onehot)


def _bias_reduce(dbias, onehot):
    Hq, L = dbias.shape
    NB = onehot.shape[0]

    def body(d_ref, oh_ref, o_ref):
        h1, h2, h3 = _split3(d_ref[...])
        oh = oh_ref[...]
        o_ref[...] = _nt(h1, oh) + _nt(h2, oh) + _nt(h3, oh)

    return pl.pallas_call(
        body, name="bias_reduce", grid=(1,),
        in_specs=[pl.BlockSpec((Hq, L), lambda i: (0, 0)), pl.BlockSpec((NB, L), lambda i: (0, 0))],
        out_specs=pl.BlockSpec((Hq, NB), lambda i: (0, 0)),
        out_shape=jax.ShapeDtypeStruct((Hq, NB), F32),
        compiler_params=_cp("arbitrary"),
    )(dbias, onehot)


def _adamw(w, g, m, v, name):
    R, C = w.shape
    tr = 256 if R % 256 == 0 else R
    bc1 = 1.0 - ADAM_B1 ** ADAM_STEP
    bc2 = 1.0 - ADAM_B2 ** ADAM_STEP

    def body(w_ref, g_ref, m_ref, v_ref, d_ref, nm_ref, nv_ref):
        g = g_ref[...]
        m2 = ADAM_B1 * m_ref[...] + (1.0 - ADAM_B1) * g
        v2 = ADAM_B2 * v_ref[...] + (1.0 - ADAM_B2) * (g * g)
        nm_ref[...] = m2
        nv_ref[...] = v2
        d_ref[...] = -ADAM_LR * ((m2 / bc1) / (jnp.sqrt(v2 / bc2) + ADAM_EPS) + ADAM_WD * w_ref[...])

    spec = pl.BlockSpec((tr, C), lambda i: (i, 0))
    return pl.pallas_call(
        body, name=name, grid=(R // tr,),
        in_specs=[spec] * 4, out_specs=[spec] * 3,
        out_shape=[jax.ShapeDtypeStruct((R, C), F32)] * 3,
        compiler_params=_cp("arbitrary"),
    )(w, g, m, v)


def _add_halves(mine, recv, name):
    K, R, C = mine.shape
    tr = 416 if R % 416 == 0 else R

    def body(a_ref, b_ref, o_ref, ob_ref):
        s = a_ref[...].astype(F32) + b_ref[...].astype(F32)
        o_ref[...] = s
        ob_ref[...] = s.astype(BF16)

    spec = pl.BlockSpec((None, tr, C), lambda k, i: (k, i, 0))
    return pl.pallas_call(
        body, name=name, grid=(K, R // tr),
        in_specs=[spec, spec], out_specs=[spec, spec],
        out_shape=[jax.ShapeDtypeStruct((K, R, C), F32), jax.ShapeDtypeStruct((K, R, C), BF16)],
        compiler_params=_cp("arbitrary", "arbitrary"),
    )(mine, recv)


def _add_received(own, recv, name):
    R, C = own.shape
    tr = 416 if R % 416 == 0 else R

    def body(a_ref, r_ref, o_ref):
        o_ref[...] = ((a_ref[...] + r_ref[0].astype(F32)) + r_ref[1].astype(F32)) + r_ref[2].astype(F32)

    return pl.pallas_call(
        body, name=name, grid=(R // tr,),
        in_specs=[pl.BlockSpec((tr, C), lambda i: (i, 0)), pl.BlockSpec((3, tr, C), lambda i: (0, i, 0))],
        out_specs=pl.BlockSpec((tr, C), lambda i: (i, 0)),
        out_shape=jax.ShapeDtypeStruct((R, C), F32),
        compiler_params=_cp("arbitrary"),
    )(own, recv)


def _position():
    x, y, c = lax.axis_index("x"), lax.axis_index("y"), lax.axis_index("c")
    others = [(1 - x, y), (x, 1 - y), (1 - x, 1 - y)]
    return x, y, c, others


def _remote(src, dst, send_sems, recv_sems, k, dev):
    return pltpu.make_async_remote_copy(src_ref=src, dst_ref=dst, send_sem=send_sems.at[k],
                                        recv_sem=recv_sems.at[k], device_id=dev, device_id_type=MESH_ID)


def _gather_weights(shard):
    R, C = shard.shape
    half = R // 2

    def body(src, out, send_sems, recv_sems):
        x, y, c, others = _position()
        mine = 2 * x + y
        sends = [_remote(src.at[c], out.at[mine, c], send_sems, recv_sems, j, (ox, oy, c))
                 for j, (ox, oy) in enumerate(others)]
        for cp in sends:
            cp.start()
        passed = []
        for j, (ox, oy) in enumerate(others):
            slot = out.at[2 * ox + oy, c]
            _remote(slot, slot, send_sems, recv_sems, j, (ox, oy, c)).wait_recv()
            fwd = _remote(slot, slot, send_sems, recv_sems, 3 + j, (x, y, 1 - c))
            fwd.start()
            passed.append(fwd)
        for j, (ox, oy) in enumerate(others):
            slot = out.at[2 * ox + oy, 1 - c]
            _remote(slot, slot, send_sems, recv_sems, 3 + j, (x, y, 1 - c)).wait_recv()
        for cp in sends + passed:
            cp.wait_send()

    return pl.pallas_call(
        body, name="gather_weights",
        in_specs=[ANY], out_specs=ANY,
        out_shape=jax.ShapeDtypeStruct((N_CHIPS, 2, half, C), shard.dtype),
        scratch_shapes=[pltpu.SemaphoreType.DMA((6,)), pltpu.SemaphoreType.DMA((6,))],
    )(shard.reshape(2, half, C)).reshape(N_CHIPS, R, C)


def _swap_halves(grads):
    K, R, C = grads.shape
    half = R // 2

    def body(src, out, send_sems, recv_sems):
        x, y, c, _ = _position()
        theirs = src.at[:, pl.ds(pl.multiple_of((1 - c) * half, 16), half), :]
        cp = _remote(theirs, out, send_sems, recv_sems, 0, (x, y, 1 - c))
        cp.start()
        cp.wait()

    return pl.pallas_call(
        body, name="swap_halves",
        in_specs=[ANY], out_specs=ANY,
        out_shape=jax.ShapeDtypeStruct((K, half, C), grads.dtype),
        scratch_shapes=[pltpu.SemaphoreType.DMA((1,)), pltpu.SemaphoreType.DMA((1,))],
    )(grads)


def _scatter_to_owners(parts):
    K, H, C = parts.shape

    def body(src, out, send_sems, recv_sems):
        x, y, c, others = _position()
        sends = [_remote(src.at[2 * ox + oy], out.at[j], send_sems, recv_sems, j, (ox, oy, c))
                 for j, (ox, oy) in enumerate(others)]
        for cp in sends:
            cp.start()
        for cp in sends:
            cp.wait()

    return pl.pallas_call(
        body, name="scatter_to_owners",
        in_specs=[ANY], out_specs=ANY,
        out_shape=jax.ShapeDtypeStruct((3, H, C), parts.dtype),
        scratch_shapes=[pltpu.SemaphoreType.DMA((3,)), pltpu.SemaphoreType.DMA((3,))],
    )(parts)


def _swap_reduced(half_rows):
    H, C = half_rows.shape

    def body(src, out, send_sems, recv_sems):
        x, y, c, _ = _position()
        cp = _remote(src, out, send_sems, recv_sems, 0, (x, y, 1 - c))
        cp.start()
        cp.wait()

    return pl.pallas_call(
        body, name="swap_reduced",
        in_specs=[ANY], out_specs=ANY,
        out_shape=jax.ShapeDtypeStruct((H, C), half_rows.dtype),
        scratch_shapes=[pltpu.SemaphoreType.DMA((1,)), pltpu.SemaphoreType.DMA((1,))],
    )(half_rows)


def _allreduce_small(block):
    R, C = block.shape
    n_dev = 8

    def body(src, out, slots, send_sems, recv_sems):
        x, y, c, _ = _position()
        me = 4 * x + 2 * y + c
        slots[me] = src[...]
        sends = []
        for r in range(1, n_dev):
            peer = (x ^ (r >> 2), y ^ ((r >> 1) & 1), c ^ (r & 1))
            cp = _remote(src, slots.at[me], send_sems, recv_sems, r - 1, peer)
            cp.start()
            sends.append(cp)
        for r in range(1, n_dev):
            theirs = slots.at[me ^ r]
            _remote(theirs, theirs, send_sems, recv_sems, r - 1, (x, y, c)).wait_recv()
        for cp in sends:
            cp.wait_send()
        acc = slots[0]
        for d in range(1, n_dev):
            acc = acc + slots[d]
        out[...] = acc

    return pl.pallas_call(
        body, name="allreduce_small",
        in_specs=[pl.BlockSpec(memory_space=pltpu.VMEM)], out_specs=pl.BlockSpec(memory_space=pltpu.VMEM),
        out_shape=jax.ShapeDtypeStruct((R, C), F32),
        scratch_shapes=[pltpu.VMEM((n_dev, R, C), F32), pltpu.SemaphoreType.DMA((7,)), pltpu.SemaphoreType.DMA((7,))],
    )(block)


def _rel_bucket(dist):
    max_exact = REL_BUCKETS // 2
    d = jnp.maximum(dist, 1).astype(F32)
    large = max_exact + (jnp.log(d / max_exact) / math.log(REL_MAX_DIST / max_exact)
                         * (REL_BUCKETS - max_exact)).astype(jnp.int32)
    large = jnp.minimum(large, REL_BUCKETS - 1)
    return jnp.where(dist < max_exact, dist, large)


def _bucket_onehot():
    T = SWA_BLOCK
    dist = (jnp.arange(T)[:, None] + T) - jnp.arange(2 * T)[None, :]
    bucket = _rel_bucket(jnp.maximum(dist, 0)).reshape(1, T * 2 * T)
    return (bucket == jnp.arange(REL_BUCKETS)[:, None]).astype(BF16)


_BUF = (("ffn1_w1", "t"), ("ffn1_w3", "t"), ("ffn1_w2", "n"), ("ffn2_w1", "t"), ("ffn2_w3", "t"),
        ("ffn2_w2", "n"), ("w_in", "t"), ("w_out", "n"), ("w_branch_swa", "tw"), ("w_branch_sb", "tw"))


def _to_rows(name_kind, w, D):
    kind = name_kind[1]
    if kind == "n":
        return w
    if kind == "t":
        return w.T
    return w.T.reshape(-1, D)


def _from_rows(name_kind, rows, width):
    kind = name_kind[1]
    if kind == "n":
        return rows
    if kind == "t":
        return rows.T
    return rows.reshape(-1, width).T


def kernel(x, norm_ffn1, ffn1_w1, ffn1_w3, ffn1_w2, norm_mix, w_in, swa_sinks, rel_bias, w_branch_swa, w_branch_sb, w_out, norm_ffn2, ffn2_w1, ffn2_w3, ffn2_w2, norm_final, loss_target, m_norm_ffn1, m_ffn1_w1, m_ffn1_w3, m_ffn1_w2, m_norm_mix, m_w_in, m_swa_sinks, m_rel_bias, m_w_branch_swa, m_w_branch_sb, m_w_out, m_norm_ffn2, m_ffn2_w1, m_ffn2_w3, m_ffn2_w2, m_norm_final, v_norm_ffn1, v_ffn1_w1, v_ffn1_w3, v_ffn1_w2, v_norm_mix, v_w_in, v_swa_sinks, v_rel_bias, v_w_branch_swa, v_w_branch_sb, v_w_out, v_norm_ffn2, v_ffn2_w1, v_ffn2_w3, v_ffn2_w2, v_norm_final):
    names = ["norm_ffn1", "ffn1_w1", "ffn1_w3", "ffn1_w2", "norm_mix", "w_in", "swa_sinks", "rel_bias",
             "w_branch_swa", "w_branch_sb", "w_out", "norm_ffn2", "ffn2_w1", "ffn2_w3", "ffn2_w2", "norm_final"]
    W = dict(zip(names, [norm_ffn1, ffn1_w1, ffn1_w3, ffn1_w2, norm_mix, w_in, swa_sinks, rel_bias,
                         w_branch_swa, w_branch_sb, w_out, norm_ffn2, ffn2_w1, ffn2_w3, ffn2_w2, norm_final]))
    M = dict(zip(names, [m_norm_ffn1, m_ffn1_w1, m_ffn1_w3, m_ffn1_w2, m_norm_mix, m_w_in, m_swa_sinks, m_rel_bias,
                         m_w_branch_swa, m_w_branch_sb, m_w_out, m_norm_ffn2, m_ffn2_w1, m_ffn2_w3, m_ffn2_w2,
                         m_norm_final]))
    V = dict(zip(names, [v_norm_ffn1, v_ffn1_w1, v_ffn1_w3, v_ffn1_w2, v_norm_mix, v_w_in, v_swa_sinks, v_rel_bias,
                         v_w_branch_swa, v_w_branch_sb, v_w_out, v_norm_ffn2, v_ffn2_w1, v_ffn2_w3, v_ffn2_w2,
                         v_norm_final]))
    xs = x[0]
    target = loss_target[0]
    S, D = xs.shape
    QW = SWA_Q_HEADS * HEAD_DIM
    KW = SWA_KV_HEADS * HEAD_DIM
    BW = SB_HEADS * HEAD_DIM
    QKV = QW + 2 * KW + 3 * BW

    pieces = [_to_rows(nk, W[nk[0]][0], D) for nk in _BUF]
    sizes = [p.shape[0] for p in pieces]
    offs = [0]
    for s in sizes:
        offs.append(offs[-1] + s)
    shard = jnp.concatenate(pieces, axis=0).astype(BF16)
    chip = 2 * lax.axis_index("x") + lax.axis_index("y")
    gathered = lax.dynamic_update_slice(_gather_weights(shard), shard[None], (chip, 0, 0))

    def full(i):
        return gathered[:, offs[i]:offs[i + 1], :].reshape(N_CHIPS * sizes[i], D)

    f1w1, f1w3, f1w2, f2w1, f2w3, f2w2, w_in_t, w_out_f = [full(i) for i in range(8)]
    wa_t = full(8).reshape(D, QW)
    wb_t = full(9).reshape(D, BW)

    g1, gmix, g3 = W["norm_ffn1"], W["norm_mix"], W["norm_ffn2"]
    gf = W["norm_final"].reshape(1, D)

    x1, h1, a1, b1 = _ffn_fwd(xs, g1, f1w1, f1w3, f1w2, "ffn1_fwd")
    o0 = QW + 2 * KW
    rows = jnp.arange(w_in_t.shape[0])
    is_q = (rows < QW) | ((rows >= o0) & (rows < o0 + BW))
    w_in_s = w_in_t * jnp.where(is_q, QK_SCALE, 1.0).astype(BF16)[:, None]
    qkv_a = _norm_proj_heads(x1, gmix, w_in_s[:o0], SWA_BLOCK, "proj_swa")
    qkv_b = _norm_proj_heads(x1, gmix, w_in_s[o0:QKV], SB_BLOCK, "proj_sb")
    gates, h2 = _norm_matmul_nt(x1, gmix, w_in_t[QKV:], F32, "proj_gates")

    onehot = _bucket_onehot()
    bias = _bias_expand(W["rel_bias"].T, onehot).reshape(SWA_Q_HEADS, SWA_BLOCK, 2 * SWA_BLOCK)
    sinks = W["swa_sinks"].reshape(SWA_Q_HEADS)
    oa_t = _swa_fwd(qkv_a, bias, sinks)
    ob_t, saved_sb = _sb_fwd(qkv_b)

    x2, merged, ba, bb = _merge_fwd(x1, gates, oa_t, ob_t, wa_t, wb_t, w_out_f)
    x3, h3, a2, b2 = _ffn_fwd(x2, g3, f2w1, f2w3, f2w2, "ffn2_fwd")
    loss_part, dx3, dgf = _final_loss(x3, gf, target)

    dx2, dg3, dz2, da2, db2, u2 = _ffn_bwd(dx3, x2, g3, a2, b2, f2w1, f2w3, f2w2, "ffn2_bwd")
    grads = {}
    grads["ffn2_w1"] = _tn_matmul(da2, h3, "ffn2_dw1")
    grads["ffn2_w3"] = _tn_matmul(db2, h3, "ffn2_dw3")
    grads["ffn2_w2"] = _tn_matmul(u2, dz2, "ffn2_dw2")

    dx2b, dba, dbb, dgates, doa_t, dob_t = _merge_bwd(dx2, gates, ba, bb, wa_t.T, wb_t.T, w_out_f,
                                                      SWA_BLOCK, SB_BLOCK)
    grads["w_out"] = _tn_matmul(merged, dx2b, "dw_out")
    grads["w_branch_swa"] = _heads_matmul(oa_t, dba, 1.0, "dw_branch_swa").T
    grads["w_branch_sb"] = _heads_matmul(ob_t, dbb, 1.0, "dw_branch_sb").T

    dqb_t, dkb_t, dvb_t = _sb_bwd(qkv_b, dob_t, saved_sb)
    dqa_t, dka_t, dva_t, dbias, dsink_rows = _swa_bwd(qkv_a, bias, sinks, doa_t, oa_t)
    d_rel = _bias_reduce(dbias.reshape(SWA_Q_HEADS, -1), onehot).T
    d_sinks = jnp.sum(dsink_rows, axis=(1, 2))

    dheads = [(dqa_t, QK_SCALE, "q_swa"), (dka_t, 1.0, "k_swa"), (dva_t, 1.0, "v_swa"),
              (dqb_t, QK_SCALE, "q_sb"), (dkb_t, 1.0, "k_sb"), (dvb_t, 1.0, "v_sb")]
    grads["w_in"] = jnp.concatenate([_heads_matmul(a, h2, sc, "dw_in_" + nm).astype(BF16) for a, sc, nm in dheads]
                                    + [_tn_matmul(dgates, h2, "dw_in_gates")], axis=0)
    row0, pieces_in = 0, []
    for a, _, _ in dheads:
        pieces_in.append((a, row0))
        row0 += a.shape[0] * HEAD_DIM
    dx1, dgmix = _proj_bwd(pieces_in, dgates, w_in_s, x1, gmix, dx2)

    dx0, dg1, dz1, da1, db1, u1 = _ffn_bwd(dx1, xs, g1, a1, b1, f1w1, f1w3, f1w2, "ffn1_bwd")
    grads["ffn1_w1"] = _tn_matmul(da1, h1, "ffn1_dw1")
    grads["ffn1_w3"] = _tn_matmul(db1, h1, "ffn1_dw3")
    grads["ffn1_w2"] = _tn_matmul(u1, dz1, "ffn1_dw2")

    gparts = [grads[nk[0]].astype(BF16).reshape(N_CHIPS, sizes[i], D) for i, nk in enumerate(_BUF)]
    gbuf = jnp.concatenate(gparts, axis=1)
    R = gbuf.shape[1]
    half = R // 2
    c = lax.axis_index("c")
    mine = 2 * lax.axis_index("x") + lax.axis_index("y")
    from_sibling = _swap_halves(gbuf)
    my_half = lax.dynamic_slice_in_dim(gbuf, c * half, half, axis=1)
    chip_sum, chip_sum16 = _add_halves(my_half, from_sibling, "add_sibling")
    received = _scatter_to_owners(chip_sum16)
    own = lax.dynamic_index_in_dim(chip_sum, mine, axis=0, keepdims=False)
    my_rows = _add_received(own, received, "add_chips")
    their_rows = _swap_reduced(my_rows)
    reduced = jnp.concatenate([jnp.where(c == 0, my_rows, their_rows), jnp.where(c == 0, their_rows, my_rows)], axis=0)

    small_rows = [dg1, dgmix, dg3, dgf,
                  jnp.pad(d_sinks.reshape(1, -1), ((0, 0), (0, D - SWA_Q_HEADS))),
                  jnp.pad(d_rel.reshape(1, -1), ((0, 0), (0, D - REL_BUCKETS * SWA_Q_HEADS))),
                  jnp.pad(loss_part, ((0, 0), (0, D - 1))), jnp.zeros((1, D), F32)]
    small = _allreduce_small(jnp.concatenate(small_rows, axis=0))
    loss = small[6, 0]

    G = {}
    for i, nk in enumerate(_BUF):
        G[nk[0]] = _from_rows(nk, reduced[offs[i]:offs[i + 1]], W[nk[0]].shape[1])[None]
    G["norm_ffn1"], G["norm_mix"], G["norm_ffn2"] = small[0:1], small[1:2], small[2:3]
    G["norm_final"] = small[3]
    G["swa_sinks"] = small[4:5, :SWA_Q_HEADS]
    G["rel_bias"] = small[5, :REL_BUCKETS * SWA_Q_HEADS].reshape(REL_BUCKETS, SWA_Q_HEADS)

    delta, new_m, new_v = {}, {}, {}
    small_names = ["norm_ffn1", "norm_mix", "norm_ffn2", "norm_final", "swa_sinks", "rel_bias"]

    def pack(d):
        return jnp.concatenate([jnp.pad(d[n].reshape(1, -1), ((0, 0), (0, D - d[n].size))) for n in small_names]
                               + [jnp.zeros((2, D), F32)], axis=0)

    sd, sm, sv = _adamw(pack(W), pack(G), pack(M), pack(V), "adamw_small")
    for r, n in enumerate(small_names):
        for dst, src in ((delta, sd), (new_m, sm), (new_v, sv)):
            dst[n] = src[r, :W[n].size].reshape(W[n].shape)
    for nk in _BUF:
        n = nk[0]
        shp = W[n].shape
        two_d = (shp[1], shp[2])
        d_, m_, v_ = _adamw(W[n].reshape(two_d), G[n].reshape(two_d), M[n].reshape(two_d), V[n].reshape(two_d),
                            "adamw_" + n)
        delta[n], new_m[n], new_v[n] = d_.reshape(shp), m_.reshape(shp), v_.reshape(shp)

    return (loss, dx0[None], *[G[n] for n in names], *[delta[n] for n in names],
            *[new_m[n] for n in names], *[new_v[n] for n in names])
```

```python
import functools
import math

import jax
import jax.numpy as jnp
from jax import lax
from jax.experimental import pallas as pl
from jax.experimental.pallas import tpu as pltpu

F32, BF16 = jnp.float32, jnp.bfloat16
MESH_ID = pl.DeviceIdType.MESH
ANY = pl.BlockSpec(memory_space=pl.ANY)

RMS_EPS = 1e-6
HEAD_DIM = 64
SWA_Q_HEADS, SWA_KV_HEADS, SWA_GROUP = 8, 2, 4
SWA_BLOCK = 128
SB_HEADS = 8
SB_BLOCK = 256
REL_BUCKETS, REL_MAX_DIST = 32, 128
NEG_BIG = -1e30
QK_SCALE = HEAD_DIM ** -0.5
ADAM_LR, ADAM_B1, ADAM_B2, ADAM_EPS, ADAM_WD, ADAM_STEP = 0.001, 0.9, 0.999, 1e-08, 0.01, 10

N_CHIPS = 4
TOKEN_TILE = 512
MATMUL_TOKEN_TILE = 1024
WGRAD_ROW_TILES = (2176, 1408, 1024, 256)
FF_TILE = 1408
FFN_TOKEN_TILE = 512
FF_BWD_TILE = 256
VMEM_LIMIT = 56 * 1024 * 1024


def _cp(*sem):
    return pltpu.CompilerParams(dimension_semantics=sem, vmem_limit_bytes=VMEM_LIMIT)


def _nn(a, b):
    return jnp.dot(a, b, preferred_element_type=F32)


def _nt(a, b):
    return lax.dot_general(a, b, (((1,), (1,)), ((), ())), preferred_element_type=F32)


def _tn(a, b):
    return lax.dot_general(a, b, (((0,), (0,)), ((), ())), preferred_element_type=F32)


def _norm_fwd(x, g):
    return x * lax.rsqrt(jnp.mean(x * x, axis=-1, keepdims=True) + RMS_EPS) * g


def _norm_bwd(x, g, dh):
    r = lax.rsqrt(jnp.mean(x * x, axis=-1, keepdims=True) + RMS_EPS)
    xh = x * r
    dxh = dh * g
    dx = r * (dxh - xh * jnp.mean(dxh * xh, axis=-1, keepdims=True))
    return dx, jnp.sum(dh * xh, axis=0, keepdims=True)


def _softplus(z):
    return jnp.maximum(z, 0.0) + jnp.log(1.0 + jnp.exp(-jnp.abs(z)))


def _ffn_fwd(x, g, w1t, w3t, w2, name, gather=None):
    S, D = x.shape
    F = w2.shape[0]
    tm, tf = min(FFN_TOKEN_TILE, S), FF_TILE
    ni, nj = S // tm, F // tf

    def body(x_ref, g_ref, w1_ref, w3_ref, w2_ref, *rest):
        if gather is None:
            xo_ref, h_ref, a_ref, b_ref, hs, acc = rest
        else:
            shard_ref, xo_ref, h_ref, a_ref, b_ref, gathered_ref, hs, acc, send_sems, recv_sems = rest
        i, j = pl.program_id(0), pl.program_id(1)
        if gather is not None:
            for when, phase in ((jnp.logical_and(i == 0, j == 0), "start"),
                                (jnp.logical_and(i == ni // 2, j == 0), "forward"),
                                (jnp.logical_and(i == ni - 1, j == nj - 1), "finish")):
                @pl.when(when)
                def _():
                    getattr(_gather_exchange(shard_ref, gathered_ref, send_sems, recv_sems), phase)()

        @pl.when(j == 0)
        def _():
            hb = _norm_fwd(x_ref[...], g_ref[...]).astype(BF16)
            hs[...] = hb
            h_ref[...] = hb
            acc[...] = jnp.zeros_like(acc)

        h = hs[...]
        a = _nt(h, w1_ref[...])
        b = _nt(h, w3_ref[...])
        a_ref[...] = a.astype(BF16)
        b_ref[...] = b.astype(BF16)
        u = a * jax.nn.sigmoid(a) * b
        acc[...] += _nn(u.astype(BF16), w2_ref[...])

        @pl.when(j == nj - 1)
        def _():
            xo_ref[...] = x_ref[...] + 0.5 * acc[...]

    in_specs = [pl.BlockSpec((tm, D), lambda i, j: (i, 0)),
                pl.BlockSpec((1, D), lambda i, j: (0, 0)),
                pl.BlockSpec((tf, D), lambda i, j: (j, 0)),
                pl.BlockSpec((tf, D), lambda i, j: (j, 0)),
                pl.BlockSpec((tf, D), lambda i, j: (j, 0))]
    out_specs = [pl.BlockSpec((tm, D), lambda i, j: (i, 0)),
                 pl.BlockSpec((tm, D), lambda i, j: (i, 0)),
                 pl.BlockSpec((tm, tf), lambda i, j: (i, j)),
                 pl.BlockSpec((tm, tf), lambda i, j: (i, j))]
    out_shape = [jax.ShapeDtypeStruct((S, D), F32), jax.ShapeDtypeStruct((S, D), BF16),
                 jax.ShapeDtypeStruct((S, F), BF16), jax.ShapeDtypeStruct((S, F), BF16)]
    scratch = [pltpu.VMEM((tm, D), BF16), pltpu.VMEM((tm, D), F32)]
    operands = [x, g, w1t, w3t, w2]
    if gather is not None:
        R, C = gather.shape
        in_specs.append(ANY)
        out_specs.append(ANY)
        out_shape.append(jax.ShapeDtypeStruct((N_CHIPS, 2, R // 2, C), gather.dtype))
        scratch += [pltpu.SemaphoreType.DMA((6,)), pltpu.SemaphoreType.DMA((6,))]
        operands.append(gather.reshape(2, R // 2, C))
    outs = list(pl.pallas_call(
        body, name=name, grid=(ni, nj), in_specs=in_specs, out_specs=out_specs, out_shape=out_shape,
        scratch_shapes=scratch, compiler_params=_cp("arbitrary", "arbitrary"),
    )(*operands))
    if gather is not None:
        outs[4] = outs[4].reshape(N_CHIPS, R, C)
    return outs


def _ffn_bwd(dxo, x, g, a, b, w1t, w3t, w2, name):
    S, D = x.shape
    F = w2.shape[0]
    tm, tf = min(MATMUL_TOKEN_TILE, S), FF_BWD_TILE
    ni, nj = S // tm, F // tf

    def body(dxo_ref, x_ref, g_ref, a_ref, b_ref, w1_ref, w3_ref, w2_ref,
             dx_ref, dg_ref, dz_ref, da_ref, db_ref, u_ref, dzs, acc):
        i, j = pl.program_id(0), pl.program_id(1)

        @pl.when(j == 0)
        def _():
            dzb = (0.5 * dxo_ref[...]).astype(BF16)
            dzs[...] = dzb
            dz_ref[...] = dzb
            acc[...] = jnp.zeros_like(acc)

        du = _nt(dzs[...], w2_ref[...])
        av = a_ref[...].astype(F32)
        bv = b_ref[...].astype(F32)
        s = jax.nn.sigmoid(av)
        silu = av * s
        db = (du * silu).astype(BF16)
        da = (du * bv * (s * (1.0 + av * (1.0 - s)))).astype(BF16)
        da_ref[...] = da
        db_ref[...] = db
        u_ref[...] = (silu * bv).astype(BF16)
        acc[...] += _nn(da, w1_ref[...]) + _nn(db, w3_ref[...])

        @pl.when(j == nj - 1)
        def _():
            dx, dg = _norm_bwd(x_ref[...], g_ref[...], acc[...])
            dx_ref[...] = dxo_ref[...] + dx

            @pl.when(i == 0)
            def _():
                dg_ref[...] = dg

            @pl.when(i > 0)
            def _():
                dg_ref[...] += dg

    row = pl.BlockSpec((tm, D), lambda i, j: (i, 0))
    wsp = pl.BlockSpec((tf, D), lambda i, j: (j, 0))
    col = pl.BlockSpec((tm, tf), lambda i, j: (i, j))
    vec = pl.BlockSpec((1, D), lambda i, j: (0, 0))
    return pl.pallas_call(
        body, name=name, grid=(ni, nj),
        in_specs=[row, row, vec, col, col, wsp, wsp, wsp],
        out_specs=[row, vec, row, col, col, col],
        out_shape=[jax.ShapeDtypeStruct((S, D), F32), jax.ShapeDtypeStruct((1, D), F32),
                   jax.ShapeDtypeStruct((S, D), BF16), jax.ShapeDtypeStruct((S, F), BF16),
                   jax.ShapeDtypeStruct((S, F), BF16), jax.ShapeDtypeStruct((S, F), BF16)],
        scratch_shapes=[pltpu.VMEM((tm, D), BF16), pltpu.VMEM((tm, D), F32)],
        compiler_params=_cp("arbitrary", "arbitrary"),
    )(dxo, x, g, a, b, w1t, w3t, w2)


def _tn_matmul(a, b, name):
    S, M = a.shape
    N = b.shape[1]
    ts = min(MATMUL_TOKEN_TILE, S)
    tmm = next(t for t in WGRAD_ROW_TILES if M % t == 0)
    ns = S // ts

    def body(a_ref, b_ref, o_ref, acc):
        s = pl.program_id(1)
        part = _tn(a_ref[...], b_ref[...])

        @pl.when(s == 0)
        def _():
            acc[...] = part

        @pl.when(s > 0)
        def _():
            acc[...] += part

        @pl.when(s == ns - 1)
        def _():
            o_ref[...] = acc[...].astype(BF16)

    return pl.pallas_call(
        body, name=name, grid=(M // tmm, ns),
        in_specs=[pl.BlockSpec((ts, tmm), lambda m, s: (s, m)),
                  pl.BlockSpec((ts, N), lambda m, s: (s, 0))],
        out_specs=pl.BlockSpec((tmm, N), lambda m, s: (m, 0)),
        out_shape=jax.ShapeDtypeStruct((M, N), BF16),
        scratch_shapes=[pltpu.VMEM((tmm, N), F32)],
        compiler_params=_cp("arbitrary", "arbitrary"),
    )(a, b)


def _norm_matmul_nt(x, g, wt, out_dtype, name):
    S, D = x.shape
    N = wt.shape[0]
    tm = min(MATMUL_TOKEN_TILE, S)
    tn = next(t for t in (1024, 768, 256) if N % t == 0)

    def body(x_ref, g_ref, w_ref, o_ref, h_ref, hs):
        @pl.when(pl.program_id(1) == 0)
        def _():
            hb = _norm_fwd(x_ref[...], g_ref[...]).astype(BF16)
            hs[...] = hb
            h_ref[...] = hb

        o_ref[...] = _nt(hs[...], w_ref[...]).astype(out_dtype)

    return pl.pallas_call(
        body, name=name, grid=(S // tm, N // tn),
        in_specs=[pl.BlockSpec((tm, D), lambda i, j: (i, 0)),
                  pl.BlockSpec((1, D), lambda i, j: (0, 0)),
                  pl.BlockSpec((tn, D), lambda i, j: (j, 0))],
        out_specs=[pl.BlockSpec((tm, tn), lambda i, j: (i, j)),
                   pl.BlockSpec((tm, D), lambda i, j: (i, 0))],
        out_shape=[jax.ShapeDtypeStruct((S, N), out_dtype), jax.ShapeDtypeStruct((S, D), BF16)],
        scratch_shapes=[pltpu.VMEM((tm, D), BF16)],
        compiler_params=_cp("arbitrary", "arbitrary"),
    )(x, g, wt)


def _heads_tile(ref):
    Hh, nbk = ref.shape[0], ref.shape[1]
    return jnp.concatenate([jnp.concatenate([ref[h, b] for b in range(nbk)], axis=1) for h in range(Hh)], axis=0)


def _store_heads(ref, val):
    Hh, nbk, dh, T = ref.shape
    for h in range(Hh):
        for b in range(nbk):
            ref[h, b] = val[h * dh:(h + 1) * dh, b * T:(b + 1) * T].astype(ref.dtype)


def _norm_proj_heads(x, g, w_rows, T, name):
    S, D = x.shape
    N = w_rows.shape[0]
    tm, tn = min(MATMUL_TOKEN_TILE, S), 768

    def body(x_ref, g_ref, w_ref, o_ref, hs):
        @pl.when(pl.program_id(1) == 0)
        def _():
            hs[...] = _norm_fwd(x_ref[...], g_ref[...]).astype(BF16)

        _store_heads(o_ref, _nt(w_ref[...], hs[...]))

    return pl.pallas_call(
        body, name=name, grid=(S // tm, N // tn),
        in_specs=[pl.BlockSpec((tm, D), lambda i, j: (i, 0)),
                  pl.BlockSpec((1, D), lambda i, j: (0, 0)),
                  pl.BlockSpec((tn, D), lambda i, j: (j, 0))],
        out_specs=pl.BlockSpec((tn // HEAD_DIM, tm // T, HEAD_DIM, T), lambda i, j: (j, i, 0, 0)),
        out_shape=jax.ShapeDtypeStruct((N // HEAD_DIM, S // T, HEAD_DIM, T), BF16),
        scratch_shapes=[pltpu.VMEM((tm, D), BF16)],
        compiler_params=_cp("arbitrary", "arbitrary"),
    )(x, g, w_rows)


def _heads_matmul(at, b, scale, name):
    Hh, nb, dh, T = at.shape
    S, N = b.shape
    ts = min(MATMUL_TOKEN_TILE, S)
    ns = S // ts

    def body(a_ref, b_ref, o_ref):
        s = pl.program_id(0)
        a = _heads_tile(a_ref)
        part = _nn((a if scale == 1.0 else a * scale).astype(BF16), b_ref[...])

        @pl.when(s == 0)
        def _():
            o_ref[...] = part

        @pl.when(s > 0)
        def _():
            o_ref[...] += part

    return pl.pallas_call(
        body, name=name, grid=(ns,),
        in_specs=[pl.BlockSpec((Hh, ts // T, dh, T), lambda s: (0, s, 0, 0)),
                  pl.BlockSpec((ts, N), lambda s: (s, 0))],
        out_specs=pl.BlockSpec((Hh * dh, N), lambda s: (0, 0)),
        out_shape=jax.ShapeDtypeStruct((Hh * dh, N), F32),
        compiler_params=_cp("arbitrary"),
    )(at, b)


def _proj_bwd(pieces, dgates, w_rows, x, g, dres):
    S, D = x.shape
    tm = min(TOKEN_TILE, S)
    n_p = len(pieces)
    gate_row = w_rows.shape[0] - dgates.shape[1]

    def body(*refs):
        p_refs = refs[:n_p]
        dgt_ref, w_ref, x_ref, g_ref, dres_ref, dx_ref, dg_ref = refs[n_p:]
        i = pl.program_id(0)
        dh = _nn(dgt_ref[...], w_ref[gate_row:, :])
        for p_ref, (arr, row0) in zip(p_refs, pieces):
            rows = arr.shape[0] * arr.shape[2]
            dh += _tn(_heads_tile(p_ref).astype(BF16), w_ref[row0:row0 + rows, :])
        dx, dg = _norm_bwd(x_ref[...], g_ref[...], dh)
        dx_ref[...] = dres_ref[...] + dx

        @pl.when(i == 0)
        def _():
            dg_ref[...] = dg

        @pl.when(i > 0)
        def _():
            dg_ref[...] += dg

    row = pl.BlockSpec((tm, D), lambda i: (i, 0))
    vec = pl.BlockSpec((1, D), lambda i: (0, 0))
    p_specs = [pl.BlockSpec((a.shape[0], tm // a.shape[3], a.shape[2], a.shape[3]), lambda i: (0, i, 0, 0))
               for a, _ in pieces]
    return pl.pallas_call(
        body, name="proj_bwd", grid=(S // tm,),
        in_specs=p_specs + [pl.BlockSpec((tm, dgates.shape[1]), lambda i: (i, 0)),
                            pl.BlockSpec(w_rows.shape, lambda i: (0, 0), pipeline_mode=pl.Buffered(1)),
                            row, vec, row],
        out_specs=[row, vec],
        out_shape=[jax.ShapeDtypeStruct((S, D), F32), jax.ShapeDtypeStruct((1, D), F32)],
        compiler_params=_cp("arbitrary"),
    )(*[a for a, _ in pieces], dgates, w_rows, x, g, dres)


def _merge_fwd(x1, gates, oa_t, ob_t, wat, wbt, w_out):
    S, D = x1.shape
    W = wat.shape[1]
    tm = min(TOKEN_TILE, S)

    def body(x_ref, ga_ref, gb_ref, oa_ref, ob_ref, wa_ref, wb_ref, wo_ref,
             x2_ref, mg_ref, ba_ref, bb_ref):
        ba = _nt(_heads_tile(oa_ref).T.astype(BF16), wa_ref[...])
        bb = _nt(_heads_tile(ob_ref).T.astype(BF16), wb_ref[...])
        merged = jax.nn.sigmoid(ga_ref[...]) * ba + jax.nn.sigmoid(gb_ref[...]) * bb
        mb = merged.astype(BF16)
        mg_ref[...] = mb
        ba_ref[...] = ba.astype(BF16)
        bb_ref[...] = bb.astype(BF16)
        x2_ref[...] = x_ref[...] + _nn(mb, wo_ref[...])

    row = pl.BlockSpec((tm, D), lambda i: (i, 0))
    full = lambda r, c: pl.BlockSpec((r, c), lambda i: (0, 0))
    heads = lambda a: pl.BlockSpec((a.shape[0], tm // a.shape[3], a.shape[2], a.shape[3]), lambda i: (0, i, 0, 0))
    return pl.pallas_call(
        body, name="merge_fwd", grid=(S // tm,),
        in_specs=[row, pl.BlockSpec((tm, D), lambda i: (i, 0)), pl.BlockSpec((tm, D), lambda i: (i, 1)),
                  heads(oa_t), heads(ob_t), full(D, W), full(D, W), full(D, D)],
        out_specs=[row, row, row, row],
        out_shape=[jax.ShapeDtypeStruct((S, D), F32)] + [jax.ShapeDtypeStruct((S, D), BF16)] * 3,
        compiler_params=_cp("arbitrary"),
    )(x1, gates, gates, oa_t, ob_t, wat, wbt, w_out)


def _merge_bwd(dx2, gates, ba, bb, wa, wb, w_out, t_a, t_b):
    S, D = dx2.shape
    W = wa.shape[0]
    tm = min(TOKEN_TILE, S)
    Hh = W // HEAD_DIM

    def body(dx_ref, ga_ref, gb_ref, ba_ref, bb_ref, wa_ref, wb_ref, wo_ref,
             dxb_ref, dba_ref, dbb_ref, dgt_ref, doa_ref, dob_ref):
        dxb = dx_ref[...].astype(BF16)
        dxb_ref[...] = dxb
        dm = _nt(dxb, wo_ref[...])
        sa = jax.nn.sigmoid(ga_ref[...])
        sb = jax.nn.sigmoid(gb_ref[...])
        dba = (dm * sa).astype(BF16)
        dbb = (dm * sb).astype(BF16)
        dba_ref[...] = dba
        dbb_ref[...] = dbb
        dgt_ref[:, :D] = (dm * ba_ref[...].astype(F32) * sa * (1.0 - sa)).astype(BF16)
        dgt_ref[:, D:] = (dm * bb_ref[...].astype(F32) * sb * (1.0 - sb)).astype(BF16)
        _store_heads(doa_ref, _nt(wa_ref[...], dba))
        _store_heads(dob_ref, _nt(wb_ref[...], dbb))

    row = pl.BlockSpec((tm, D), lambda i: (i, 0))
    full = lambda r, c: pl.BlockSpec((r, c), lambda i: (0, 0))
    heads = lambda T: pl.BlockSpec((Hh, tm // T, HEAD_DIM, T), lambda i: (0, i, 0, 0))
    return pl.pallas_call(
        body, name="merge_bwd", grid=(S // tm,),
        in_specs=[row, pl.BlockSpec((tm, D), lambda i: (i, 0)), pl.BlockSpec((tm, D), lambda i: (i, 1)),
                  row, row, full(W, D), full(W, D), full(D, D)],
        out_specs=[row, row, row, pl.BlockSpec((tm, 2 * D), lambda i: (i, 0)), heads(t_a), heads(t_b)],
        out_shape=[jax.ShapeDtypeStruct((S, D), BF16)] * 3 + [jax.ShapeDtypeStruct((S, 2 * D), BF16),
                   jax.ShapeDtypeStruct((Hh, S // t_a, HEAD_DIM, t_a), F32),
                   jax.ShapeDtypeStruct((Hh, S // t_b, HEAD_DIM, t_b), BF16)],
        compiler_params=_cp("arbitrary"),
    )(dx2, gates, gates, ba, bb, wa, wb, w_out)


def _final_loss(x3, gf, target):
    S, D = x3.shape
    tm = min(TOKEN_TILE, S)

    def body(x_ref, g_ref, t_ref, loss_ref, dx_ref, dg_ref):
        i = pl.program_id(0)
        x = x_ref[...]
        g = g_ref[...]
        e = _norm_fwd(x, g) - t_ref[...]
        part = 0.5 * jnp.sum(jnp.mean(e * e, axis=-1, keepdims=True), axis=0, keepdims=True)
        dx, dg = _norm_bwd(x, g, e * (1.0 / D))
        dx_ref[...] = dx

        @pl.when(i == 0)
        def _():
            loss_ref[...] = part
            dg_ref[...] = dg

        @pl.when(i > 0)
        def _():
            loss_ref[...] += part
            dg_ref[...] += dg

    row = pl.BlockSpec((tm, D), lambda i: (i, 0))
    vec = pl.BlockSpec((1, D), lambda i: (0, 0))
    return pl.pallas_call(
        body, name="final_loss", grid=(S // tm,),
        in_specs=[row, vec, row],
        out_specs=[pl.BlockSpec((1, 1), lambda i: (0, 0)), row, vec],
        out_shape=[jax.ShapeDtypeStruct((1, 1), F32), jax.ShapeDtypeStruct((S, D), F32),
                   jax.ShapeDtypeStruct((1, D), F32)],
        compiler_params=_cp("arbitrary"),
    )(x3, gf, target)


SB_FWD_HEAD_GROUP = 8
SB_HEAD_GROUP = 4
LANES = 128


def _tri(T, kind):
    r = lax.broadcasted_iota(jnp.int32, (T, T), 0)
    c = lax.broadcasted_iota(jnp.int32, (T, T), 1)
    return {"after": r > c, "upto": r <= c, "before": r < c}[kind].astype(BF16)


def _lane(v, j):
    return jnp.broadcast_to(v[:, j:j + 1], (v.shape[0], LANES))


def _t_bf16(x):
    return x.astype(F32).T.astype(BF16)


def _wide(v, T):
    return jnp.tile(v, (1, T // LANES))


SB_SLOTS = 3
SB_FWD_SLOTS = 2
COPY_PARTS = 4


class _split_copy:
    def __init__(self, src, dst, sems):
        n = src.shape[0] // COPY_PARTS
        self.parts = [pltpu.make_async_copy(src.at[pl.ds(r * n, n)], dst.at[pl.ds(r * n, n)], sems.at[r])
                      for r in range(COPY_PARTS)]

    def start(self):
        for cp in self.parts:
            cp.start()

    def wait(self):
        for cp in self.parts:
            cp.wait()


def _sb_pair(i, kb):
    return (i * (i + 1)) // 2 + kb


def _sb_fwd(qkv):
    H3, nb, dh, T = qkv.shape
    H = H3 // 3
    HG = SB_FWD_HEAD_GROUP
    assert HG == H, "one head group: a saved tile holds all the heads"
    n_pairs = (nb * (nb + 1)) // 2

    def body(q_ref, k_ref, v_ref, o_ref, saved_ref, stage, sems):
        row = lax.broadcasted_iota(jnp.int32, (T, T), 0)
        col = lax.broadcasted_iota(jnp.int32, (T, T), 1)
        tri = col < row
        after = _tri(T, "after")

        def save(slot, pair):
            return _split_copy(stage.at[slot], saved_ref.at[pair], sems.at[slot])

        def blocks(qs, i, kb, step, carry, diag):
            hs = range(HG)
            slot = step % SB_FWD_SLOTS

            @pl.when(step >= SB_FWD_SLOTS)
            def _():
                save(slot, 0).wait()

            z = [_nn(qs[hh], k_ref[hh, kb]) for hh in hs]
            res, ls, first = [None] * HG, [None] * HG, [None] * HG
            for hh in hs:
                sp = _softplus(z[hh])
                if diag:
                    sp = jnp.where(tri, sp, 0.0)
                ls[hh] = z[hh] - sp
                spb = sp.astype(BF16)
                first[hh] = _lane(spb.astype(F32), 0)
                res[hh] = _nn(spb, after)
            out = []
            for hh in hs:
                c, oacc = carry[2 * hh], carry[2 * hh + 1]
                a = jnp.exp(ls[hh] - (res[hh] + _wide(c, T)))
                if diag:
                    a = jnp.where(tri, a, 0.0)
                ab = a.astype(BF16)
                stage[slot, hh, 0] = ab
                stage[slot, hh, 1] = jnp.exp(ls[hh]).astype(BF16)
                out.extend([c + (first[hh] + _lane(res[hh], 0)), oacc + _nt(v_ref[hh, kb], ab)])
            save(slot, _sb_pair(i, kb)).start()
            return tuple(out)

        def qblock(i, step):
            qs = [_t_bf16(q_ref[hh, i]) for hh in range(HG)]
            carry = blocks(qs, i, i, step, (jnp.zeros((T, LANES), F32), jnp.zeros((dh, T), F32)) * HG, True)

            def kstep(t, carry):
                return blocks(qs, i, i - 1 - t, step + 1 + t, carry, False)

            carry = lax.fori_loop(0, i, kstep, carry)
            for hh in range(HG):
                o_ref[hh, i] = carry[2 * hh + 1]
            return step + 1 + i

        lax.fori_loop(0, nb, qblock, 0)
        for slot in range(min(SB_FWD_SLOTS, n_pairs)):
            save(slot, 0).wait()

    ht = lambda part: pl.BlockSpec((HG, nb, dh, T), lambda h: (part, 0, 0, 0), pipeline_mode=pl.Buffered(1))
    return pl.pallas_call(
        body, name="sb_fwd", grid=(1,),
        in_specs=[ht(0), ht(1), ht(2)],
        out_specs=[ht(0), ANY],
        out_shape=[jax.ShapeDtypeStruct((H, nb, dh, T), F32),
                   jax.ShapeDtypeStruct((n_pairs, H, 2, T, T), BF16)],
        scratch_shapes=[pltpu.VMEM((SB_FWD_SLOTS, HG, 2, T, T), BF16),
                        pltpu.SemaphoreType.DMA((SB_FWD_SLOTS, COPY_PARTS))],
        compiler_params=_cp("arbitrary"),
    )(qkv, qkv, qkv)


def _sb_bwd(qkv, dot, saved):
    H3, nb, dh, T = qkv.shape
    H = H3 // 3
    HG = SB_HEAD_GROUP
    n_pairs = (nb * (nb + 1)) // 2

    def body(qt_ref, k_ref, v_ref, dot_ref, saved_ref, dq_ref, dk_ref, dv_ref, stage, sems):
        head0 = pl.program_id(0) * HG
        row = lax.broadcasted_iota(jnp.int32, (T, T), 0)
        col = lax.broadcasted_iota(jnp.int32, (T, T), 1)
        tri = col < row
        before = _tri(T, "before")
        dk_ref[...] = jnp.zeros_like(dk_ref)
        dv_ref[...] = jnp.zeros_like(dv_ref)

        def fetch(slot, pair):
            return _split_copy(saved_ref.at[pair, pl.ds(head0, HG)], stage.at[slot], sems.at[slot])

        for ahead in range(min(SB_SLOTS - 1, n_pairs)):
            fetch(ahead, ahead).start()

        def blocks(qTs, dos, doTs, i, kb, carry, diag):
            hs = range(HG)
            pair = _sb_pair(i, kb)
            slot = pair % SB_SLOTS
            fetch(slot, pair).wait()
            nxt = pair + (SB_SLOTS - 1)

            @pl.when(nxt < n_pairs)
            def _():
                fetch(nxt % SB_SLOTS, nxt).start()

            kT = [k_ref[hh, kb] for hh in hs]
            da = [_nn(dos[hh], v_ref[hh, kb]) for hh in hs]
            g, gb, resg = [None] * HG, [None] * HG, [None] * HG
            for hh in hs:
                g[hh] = stage[slot, hh, 0].astype(F32) * da[hh]
                gb[hh] = g[hh].astype(BF16)
                resg[hh] = _nn(gb[hh], before)
            out = []
            for hh in hs:
                pre_g, dq = carry[2 * hh], carry[2 * hh + 1]
                dz = g[hh] - (g[hh] + (resg[hh] + _wide(pre_g, T))) * stage[slot, hh, 1].astype(F32)
                if diag:
                    dz = jnp.where(tri, dz, 0.0)
                dzb = dz.astype(BF16)
                dk_ref[hh, kb] += _nn(qTs[hh], dzb)
                dv_ref[hh, kb] += _nn(doTs[hh], stage[slot, hh, 0])
                out.extend([pre_g + (_lane(resg[hh], T - 1) + _lane(gb[hh].astype(F32), T - 1)),
                            dq + _nt(kT[hh], dzb)])
            return tuple(out)

        def qblock(i, _):
            qTs = [qt_ref[hh, i] for hh in range(HG)]
            doTs = [dot_ref[hh, i] for hh in range(HG)]
            dos = [_t_bf16(v) for v in doTs]
            carry = (jnp.zeros((T, LANES), F32), jnp.zeros((dh, T), F32)) * HG

            def kstep(kb, carry):
                return blocks(qTs, dos, doTs, i, kb, carry, False)

            carry = lax.fori_loop(0, i, kstep, carry)
            carry = blocks(qTs, dos, doTs, i, i, carry, True)
            for hh in range(HG):
                dq_ref[hh, i] = carry[2 * hh + 1]
            return 0

        lax.fori_loop(0, nb, qblock, 0)

    G = H // HG
    ht = lambda part: pl.BlockSpec((HG, nb, dh, T), lambda h: (h + part * G, 0, 0, 0),
                                   pipeline_mode=pl.Buffered(1))
    return pl.pallas_call(
        body, name="sb_bwd", grid=(G,),
        in_specs=[ht(0), ht(1), ht(2), ht(0), ANY],
        out_specs=[ht(0), ht(0), ht(0)],
        out_shape=[jax.ShapeDtypeStruct((H, nb, dh, T), F32)] * 3,
        scratch_shapes=[pltpu.VMEM((SB_SLOTS, HG, 2, T, T), BF16), pltpu.SemaphoreType.DMA((SB_SLOTS, COPY_PARTS))],
        compiler_params=_cp("arbitrary"),
    )(qkv, qkv, qkv, dot, saved)


def _swa_probs(zp, zc, bias, sink, first):
    T = zp.shape[0]
    row = lax.broadcasted_iota(jnp.int32, (T, T), 0)
    col = lax.broadcasted_iota(jnp.int32, (T, T), 1)
    lp = jnp.where(jnp.logical_and(col > row, jnp.logical_not(first)), zp + bias[:, :T], NEG_BIG)
    lc = jnp.where(col <= row, zc + bias[:, T:], NEG_BIG)
    m = jnp.maximum(jnp.maximum(jnp.max(lp, axis=1, keepdims=True), jnp.max(lc, axis=1, keepdims=True)), sink)
    pp = jnp.exp(lp - m)
    pc = jnp.exp(lc - m)
    ps = jnp.exp(sink - m)
    inv = 1.0 / (jnp.sum(pp, axis=1, keepdims=True) + jnp.sum(pc, axis=1, keepdims=True) + ps)
    return pp * inv, pc * inv, ps * inv


def _swa_specs(nb, dh, T, grp, Hq, Hkv, clamp):
    blk = (lambda n: jnp.minimum(n, nb - 1)) if clamp else (lambda n: n)
    q = pl.BlockSpec((grp, None, dh, T), lambda h, n: (h, blk(n), 0, 0))
    one = lambda first, back: pl.BlockSpec(
        (None, None, dh, T), lambda h, n: (first + h, jnp.maximum(blk(n) - back, 0) if back else blk(n), 0, 0))
    return q, [one(Hq, 1), one(Hq, 0), one(Hq + Hkv, 1), one(Hq + Hkv, 0)]


def _swa_fwd(qkv, bias, sinks):
    Hq, Hkv, grp = SWA_Q_HEADS, SWA_KV_HEADS, SWA_GROUP
    _, nb, dh, T = qkv.shape

    def body(sink_ref, q_ref, kp_ref, kc_ref, vp_ref, vc_ref, bias_ref, o_ref):
        hk, n = pl.program_id(0), pl.program_id(1)
        kp, kc, vp, vc = kp_ref[...], kc_ref[...], vp_ref[...], vc_ref[...]
        qs = [_t_bf16(q_ref[g]) for g in range(grp)]
        zs = [(_nn(q, kp), _nn(q, kc)) for q in qs]
        for g in range(grp):
            pp, pc, _ = _swa_probs(*zs[g], bias_ref[g], sink_ref[hk * grp + g], n == 0)
            o_ref[g] = _nt(vp, pp.astype(BF16)) + _nt(vc, pc.astype(BF16))

    q_spec, kv_specs = _swa_specs(nb, dh, T, grp, Hq, Hkv, False)
    return pl.pallas_call(
        body, name="swa_fwd", grid=(Hkv, nb),
        in_specs=[pl.BlockSpec(memory_space=pltpu.SMEM), q_spec] + kv_specs
                 + [pl.BlockSpec((grp, T, 2 * T), lambda h, n: (h, 0, 0))],
        out_specs=pl.BlockSpec((grp, None, dh, T), lambda h, n: (h, n, 0, 0)),
        out_shape=jax.ShapeDtypeStruct((Hq, nb, dh, T), F32),
        compiler_params=_cp("arbitrary", "arbitrary"),
    )(sinks, qkv, qkv, qkv, qkv, qkv, bias)


def _swa_bwd(qkv, bias, sinks, dot, ot):
    Hq, Hkv, grp = SWA_Q_HEADS, SWA_KV_HEADS, SWA_GROUP
    _, nb, dh, T = qkv.shape

    def body(sink_ref, qt_ref, kp_ref, kc_ref, vp_ref, vc_ref, bias_ref, dot_ref, ot_ref,
             dq_ref, dk_ref, dv_ref, dbias_ref, dsink_ref, ck, cv):
        hk, n = pl.program_id(0), pl.program_id(1)

        @pl.when(n == 0)
        def _():
            dbias_ref[...] = jnp.zeros_like(dbias_ref)
            dsink_ref[...] = jnp.zeros_like(dsink_ref)
            ck[...] = jnp.zeros_like(ck)
            cv[...] = jnp.zeros_like(cv)

        @pl.when(n < nb)
        def _():
            kp, kc, vp, vc = kp_ref[...], kc_ref[...], vp_ref[...], vc_ref[...]
            kprev = jnp.zeros((dh, T), F32)
            vprev = jnp.zeros((dh, T), F32)
            kcur = jnp.zeros((dh, T), F32)
            vcur = jnp.zeros((dh, T), F32)
            qTs = [qt_ref[g] for g in range(grp)]
            qs = [_t_bf16(v) for v in qTs]
            dos = [dot_ref[g].T for g in range(grp)]
            zs = [(_nn(q, kp), _nn(q, kc)) for q in qs]
            dps = [(_nn(do.astype(BF16), vp), _nn(do.astype(BF16), vc)) for do in dos]
            dls, pbs = [], []
            for g in range(grp):
                pp, pc, ps = _swa_probs(*zs[g], bias_ref[g], sink_ref[hk * grp + g], n == 0)
                delta = jnp.sum(dos[g] * ot_ref[g].T, axis=1, keepdims=True)
                dlp = pp * (dps[g][0] - delta)
                dlc = pc * (dps[g][1] - delta)
                dbias_ref[g, :, :T] += dlp
                dbias_ref[g, :, T:] += dlc
                dsink_ref[g] += -ps * delta
                dls.append((dlp.astype(BF16), dlc.astype(BF16)))
                pbs.append((pp.astype(BF16), pc.astype(BF16)))
            for g in range(grp):
                dlpb, dlcb = dls[g]
                doT = dot_ref[g].astype(BF16)
                dq_ref[g] = _nt(kp, dlpb) + _nt(kc, dlcb)
                kprev += _nn(qTs[g], dlpb)
                kcur += _nn(qTs[g], dlcb)
                vprev += _nn(doT, pbs[g][0])
                vcur += _nn(doT, pbs[g][1])
            dk_ref[...] = ck[...] + kprev
            dv_ref[...] = cv[...] + vprev
            ck[...] = kcur
            cv[...] = vcur

        @pl.when(n == nb)
        def _():
            dk_ref[...] = ck[...]
            dv_ref[...] = cv[...]

    qt_spec, kv_specs = _swa_specs(nb, dh, T, grp, Hq, Hkv, True)
    prev = pl.BlockSpec((None, None, dh, T), lambda h, n: (h, jnp.maximum(n - 1, 0), 0, 0))
    per_group = lambda a, b: pl.BlockSpec((grp, a, b), lambda h, n: (h, 0, 0))
    return pl.pallas_call(
        body, name="swa_bwd", grid=(Hkv, nb + 1),
        in_specs=[pl.BlockSpec(memory_space=pltpu.SMEM), qt_spec] + kv_specs
                 + [per_group(T, 2 * T), qt_spec, qt_spec],
        out_specs=[qt_spec, prev, prev, per_group(T, 2 * T), per_group(T, 1)],
        out_shape=[jax.ShapeDtypeStruct((Hq, nb, dh, T), F32), jax.ShapeDtypeStruct((Hkv, nb, dh, T), F32),
                   jax.ShapeDtypeStruct((Hkv, nb, dh, T), F32), jax.ShapeDtypeStruct((Hq, T, 2 * T), F32),
                   jax.ShapeDtypeStruct((Hq, T, 1), F32)],
        scratch_shapes=[pltpu.VMEM((dh, T), F32), pltpu.VMEM((dh, T), F32)],
        compiler_params=_cp("arbitrary", "arbitrary"),
    )(sinks, qkv, qkv, qkv, qkv, qkv, bias, dot, ot)


def _split3(x):
    h1 = x.astype(BF16)
    r1 = x - h1.astype(F32)
    h2 = r1.astype(BF16)
    h3 = (r1 - h2.astype(F32)).astype(BF16)
    return h1, h2, h3


def _bias_expand(rel_t, onehot):
    Hq, NB = rel_t.shape
    L = onehot.shape[1]

    def body(r_ref, oh_ref, o_ref):
        h1, h2, h3 = _split3(r_ref[...])
        oh = oh_ref[...]
        o_ref[...] = _nn(h1, oh) + _nn(h2, oh) + _nn(h3, oh)

    return pl.pallas_call(
        body, name="bias_expand", grid=(1,),
        in_specs=[pl.BlockSpec((Hq, NB), lambda i: (0, 0)), pl.BlockSpec((NB, L), lambda i: (0, 0))],
        out_specs=pl.BlockSpec((Hq, L), lambda i: (0, 0)),
        out_shape=jax.ShapeDtypeStruct((Hq, L), F32),
        compiler_params=_cp("arbitrary"),
    )(rel_t, onehot)


def _bias_reduce(dbias, onehot):
    Hq, L = dbias.shape
    NB = onehot.shape[0]

    def body(d_ref, oh_ref, o_ref):
        h1, h2, h3 = _split3(d_ref[...])
        oh = oh_ref[...]
        o_ref[...] = _nt(h1, oh) + _nt(h2, oh) + _nt(h3, oh)

    return pl.pallas_call(
        body, name="bias_reduce", grid=(1,),
        in_specs=[pl.BlockSpec((Hq, L), lambda i: (0, 0)), pl.BlockSpec((NB, L), lambda i: (0, 0))],
        out_specs=pl.BlockSpec((Hq, NB), lambda i: (0, 0)),
        out_shape=jax.ShapeDtypeStruct((Hq, NB), F32),
        compiler_params=_cp("arbitrary"),
    )(dbias, onehot)


def _adamw(w, g, m, v, name):
    R, C = w.shape
    tr = 256 if R % 256 == 0 else R
    bc1 = 1.0 - ADAM_B1 ** ADAM_STEP
    bc2 = 1.0 - ADAM_B2 ** ADAM_STEP

    def body(w_ref, g_ref, m_ref, v_ref, d_ref, nm_ref, nv_ref):
        g = g_ref[...]
        m2 = ADAM_B1 * m_ref[...] + (1.0 - ADAM_B1) * g
        v2 = ADAM_B2 * v_ref[...] + (1.0 - ADAM_B2) * (g * g)
        nm_ref[...] = m2
        nv_ref[...] = v2
        d_ref[...] = -ADAM_LR * ((m2 / bc1) / (jnp.sqrt(v2 / bc2) + ADAM_EPS) + ADAM_WD * w_ref[...])

    spec = pl.BlockSpec((tr, C), lambda i: (i, 0))
    return pl.pallas_call(
        body, name=name, grid=(R // tr,),
        in_specs=[spec] * 4, out_specs=[spec] * 3,
        out_shape=[jax.ShapeDtypeStruct((R, C), F32)] * 3,
        compiler_params=_cp("arbitrary"),
    )(w, g, m, v)


def _add_halves(mine, recv, name):
    K, R, C = mine.shape
    tr = 416 if R % 416 == 0 else R

    def body(a_ref, b_ref, o_ref, ob_ref):
        s = a_ref[...].astype(F32) + b_ref[...].astype(F32)
        o_ref[...] = s
        ob_ref[...] = s.astype(BF16)

    spec = pl.BlockSpec((None, tr, C), lambda k, i: (k, i, 0))
    return pl.pallas_call(
        body, name=name, grid=(K, R // tr),
        in_specs=[spec, spec], out_specs=[spec, spec],
        out_shape=[jax.ShapeDtypeStruct((K, R, C), F32), jax.ShapeDtypeStruct((K, R, C), BF16)],
        compiler_params=_cp("arbitrary", "arbitrary"),
    )(mine, recv)


def _add_received(own, recv, name):
    R, C = own.shape
    tr = 416 if R % 416 == 0 else R

    def body(a_ref, r_ref, o_ref):
        o_ref[...] = ((a_ref[...] + r_ref[0].astype(F32)) + r_ref[1].astype(F32)) + r_ref[2].astype(F32)

    return pl.pallas_call(
        body, name=name, grid=(R // tr,),
        in_specs=[pl.BlockSpec((tr, C), lambda i: (i, 0)), pl.BlockSpec((3, tr, C), lambda i: (0, i, 0))],
        out_specs=pl.BlockSpec((tr, C), lambda i: (i, 0)),
        out_shape=jax.ShapeDtypeStruct((R, C), F32),
        compiler_params=_cp("arbitrary"),
    )(own, recv)


def _position():
    x, y, c = lax.axis_index("x"), lax.axis_index("y"), lax.axis_index("c")
    others = [(1 - x, y), (x, 1 - y), (1 - x, 1 - y)]
    return x, y, c, others


def _remote(src, dst, send_sems, recv_sems, k, dev):
    return pltpu.make_async_remote_copy(src_ref=src, dst_ref=dst, send_sem=send_sems.at[k],
                                        recv_sem=recv_sems.at[k], device_id=dev, device_id_type=MESH_ID)


class _gather_exchange:
    def __init__(self, src, out, send_sems, recv_sems):
        x, y, c, others = _position()
        mine, sibling = 2 * x + y, (x, y, 1 - c)
        self.sends, self.arrivals, self.passes, self.from_sibling = [], [], [], []
        for j, (ox, oy) in enumerate(others):
            slot = out.at[2 * ox + oy, c]
            theirs = out.at[2 * ox + oy, 1 - c]
            self.sends.append(_remote(src.at[c], out.at[mine, c], send_sems, recv_sems, j, (ox, oy, c)))
            self.arrivals.append(_remote(slot, slot, send_sems, recv_sems, j, (ox, oy, c)))
            self.passes.append(_remote(slot, slot, send_sems, recv_sems, 3 + j, sibling))
            self.from_sibling.append(_remote(theirs, theirs, send_sems, recv_sems, 3 + j, sibling))

    def start(self):
        for cp in self.sends:
            cp.start()

    def forward(self):
        for arrived, onward in zip(self.arrivals, self.passes):
            arrived.wait_recv()
            onward.start()

    def finish(self):
        for cp in self.from_sibling:
            cp.wait_recv()
        for cp in self.sends + self.passes:
            cp.wait_send()


def _gather_weights(shard):
    R, C = shard.shape
    half = R // 2

    def body(src, out, send_sems, recv_sems):
        ex = _gather_exchange(src, out, send_sems, recv_sems)
        ex.start()
        ex.forward()
        ex.finish()

    return pl.pallas_call(
        body, name="gather_weights",
        in_specs=[ANY], out_specs=ANY,
        out_shape=jax.ShapeDtypeStruct((N_CHIPS, 2, half, C), shard.dtype),
        scratch_shapes=[pltpu.SemaphoreType.DMA((6,)), pltpu.SemaphoreType.DMA((6,))],
    )(shard.reshape(2, half, C)).reshape(N_CHIPS, R, C)


def _swap_halves(grads):
    K, R, C = grads.shape
    half = R // 2

    def body(src, out, send_sems, recv_sems):
        x, y, c, _ = _position()
        theirs = src.at[:, pl.ds(pl.multiple_of((1 - c) * half, 16), half), :]
        cp = _remote(theirs, out, send_sems, recv_sems, 0, (x, y, 1 - c))
        cp.start()
        cp.wait()

    return pl.pallas_call(
        body, name="swap_halves",
        in_specs=[ANY], out_specs=ANY,
        out_shape=jax.ShapeDtypeStruct((K, half, C), grads.dtype),
        scratch_shapes=[pltpu.SemaphoreType.DMA((1,)), pltpu.SemaphoreType.DMA((1,))],
    )(grads)


def _scatter_to_owners(parts):
    K, H, C = parts.shape

    def body(src, out, send_sems, recv_sems):
        x, y, c, others = _position()
        sends = [_remote(src.at[2 * ox + oy], out.at[j], send_sems, recv_sems, j, (ox, oy, c))
                 for j, (ox, oy) in enumerate(others)]
        for cp in sends:
            cp.start()
        for cp in sends:
            cp.wait()

    return pl.pallas_call(
        body, name="scatter_to_owners",
        in_specs=[ANY], out_specs=ANY,
        out_shape=jax.ShapeDtypeStruct((3, H, C), parts.dtype),
        scratch_shapes=[pltpu.SemaphoreType.DMA((3,)), pltpu.SemaphoreType.DMA((3,))],
    )(parts)


def _swap_reduced(half_rows):
    H, C = half_rows.shape

    def body(src, out, send_sems, recv_sems):
        x, y, c, _ = _position()
        cp = _remote(src, out, send_sems, recv_sems, 0, (x, y, 1 - c))
        cp.start()
        cp.wait()

    return pl.pallas_call(
        body, name="swap_reduced",
        in_specs=[ANY], out_specs=ANY,
        out_shape=jax.ShapeDtypeStruct((H, C), half_rows.dtype),
        scratch_shapes=[pltpu.SemaphoreType.DMA((1,)), pltpu.SemaphoreType.DMA((1,))],
    )(half_rows)


def _allreduce_small(block):
    R, C = block.shape
    n_dev = 8

    def body(src, out, slots, send_sems, recv_sems):
        x, y, c, _ = _position()
        me = 4 * x + 2 * y + c
        slots[me] = src[...]
        sends = []
        for r in range(1, n_dev):
            peer = (x ^ (r >> 2), y ^ ((r >> 1) & 1), c ^ (r & 1))
            cp = _remote(src, slots.at[me], send_sems, recv_sems, r - 1, peer)
            cp.start()
            sends.append(cp)
        for r in range(1, n_dev):
            theirs = slots.at[me ^ r]
            _remote(theirs, theirs, send_sems, recv_sems, r - 1, (x, y, c)).wait_recv()
        for cp in sends:
            cp.wait_send()
        acc = slots[0]
        for d in range(1, n_dev):
            acc = acc + slots[d]
        out[...] = acc

    return pl.pallas_call(
        body, name="allreduce_small",
        in_specs=[pl.BlockSpec(memory_space=pltpu.VMEM)], out_specs=pl.BlockSpec(memory_space=pltpu.VMEM),
        out_shape=jax.ShapeDtypeStruct((R, C), F32),
        scratch_shapes=[pltpu.VMEM((n_dev, R, C), F32), pltpu.SemaphoreType.DMA((7,)), pltpu.SemaphoreType.DMA((7,))],
    )(block)


def _rel_bucket(dist):
    max_exact = REL_BUCKETS // 2
    d = jnp.maximum(dist, 1).astype(F32)
    large = max_exact + (jnp.log(d / max_exact) / math.log(REL_MAX_DIST / max_exact)
                         * (REL_BUCKETS - max_exact)).astype(jnp.int32)
    large = jnp.minimum(large, REL_BUCKETS - 1)
    return jnp.where(dist < max_exact, dist, large)


def _bucket_onehot():
    T = SWA_BLOCK
    dist = (jnp.arange(T)[:, None] + T) - jnp.arange(2 * T)[None, :]
    bucket = _rel_bucket(jnp.maximum(dist, 0)).reshape(1, T * 2 * T)
    return (bucket == jnp.arange(REL_BUCKETS)[:, None]).astype(BF16)


_BUF = (("ffn1_w1", "t"), ("ffn1_w3", "t"), ("ffn1_w2", "n"), ("ffn2_w1", "t"), ("ffn2_w3", "t"),
        ("ffn2_w2", "n"), ("w_in", "t"), ("w_out", "n"), ("w_branch_swa", "tw"), ("w_branch_sb", "tw"))


def _to_rows(name_kind, w, D):
    kind = name_kind[1]
    if kind == "n":
        return w
    if kind == "t":
        return w.T
    return w.T.reshape(-1, D)


def _from_rows(name_kind, rows, width):
    kind = name_kind[1]
    if kind == "n":
        return rows
    if kind == "t":
        return rows.T
    return rows.reshape(-1, width).T


def kernel(x, norm_ffn1, ffn1_w1, ffn1_w3, ffn1_w2, norm_mix, w_in, swa_sinks, rel_bias, w_branch_swa, w_branch_sb, w_out, norm_ffn2, ffn2_w1, ffn2_w3, ffn2_w2, norm_final, loss_target, m_norm_ffn1, m_ffn1_w1, m_ffn1_w3, m_ffn1_w2, m_norm_mix, m_w_in, m_swa_sinks, m_rel_bias, m_w_branch_swa, m_w_branch_sb, m_w_out, m_norm_ffn2, m_ffn2_w1, m_ffn2_w3, m_ffn2_w2, m_norm_final, v_norm_ffn1, v_ffn1_w1, v_ffn1_w3, v_ffn1_w2, v_norm_mix, v_w_in, v_swa_sinks, v_rel_bias, v_w_branch_swa, v_w_branch_sb, v_w_out, v_norm_ffn2, v_ffn2_w1, v_ffn2_w3, v_ffn2_w2, v_norm_final):
    names = ["norm_ffn1", "ffn1_w1", "ffn1_w3", "ffn1_w2", "norm_mix", "w_in", "swa_sinks", "rel_bias",
             "w_branch_swa", "w_branch_sb", "w_out", "norm_ffn2", "ffn2_w1", "ffn2_w3", "ffn2_w2", "norm_final"]
    W = dict(zip(names, [norm_ffn1, ffn1_w1, ffn1_w3, ffn1_w2, norm_mix, w_in, swa_sinks, rel_bias,
                         w_branch_swa, w_branch_sb, w_out, norm_ffn2, ffn2_w1, ffn2_w3, ffn2_w2, norm_final]))
    M = dict(zip(names, [m_norm_ffn1, m_ffn1_w1, m_ffn1_w3, m_ffn1_w2, m_norm_mix, m_w_in, m_swa_sinks, m_rel_bias,
                         m_w_branch_swa, m_w_branch_sb, m_w_out, m_norm_ffn2, m_ffn2_w1, m_ffn2_w3, m_ffn2_w2,
                         m_norm_final]))
    V = dict(zip(names, [v_norm_ffn1, v_ffn1_w1, v_ffn1_w3, v_ffn1_w2, v_norm_mix, v_w_in, v_swa_sinks, v_rel_bias,
                         v_w_branch_swa, v_w_branch_sb, v_w_out, v_norm_ffn2, v_ffn2_w1, v_ffn2_w3, v_ffn2_w2,
                         v_norm_final]))
    xs = x[0]
    target = loss_target[0]
    S, D = xs.shape
    QW = SWA_Q_HEADS * HEAD_DIM
    KW = SWA_KV_HEADS * HEAD_DIM
    BW = SB_HEADS * HEAD_DIM
    QKV = QW + 2 * KW + 3 * BW

    pieces = [_to_rows(nk, W[nk[0]][0], D) for nk in _BUF]
    sizes = [p.shape[0] for p in pieces]
    offs = [0]
    for s in sizes:
        offs.append(offs[-1] + s)
    n_first = 3
    first_rows = offs[n_first]
    shard_a = jnp.concatenate(pieces[:n_first], axis=0).astype(BF16)
    shard_b = jnp.concatenate(pieces[n_first:], axis=0).astype(BF16)
    chip = 2 * lax.axis_index("x") + lax.axis_index("y")
    gathered_a = lax.dynamic_update_slice(_gather_weights(shard_a), shard_a[None], (chip, 0, 0))
    f1w1, f1w3, f1w2 = [gathered_a[:, offs[i]:offs[i + 1], :].reshape(N_CHIPS * sizes[i], D) for i in range(n_first)]

    g1, gmix, g3 = W["norm_ffn1"], W["norm_mix"], W["norm_ffn2"]
    gf = W["norm_final"].reshape(1, D)

    x1, h1, a1, b1, gathered_b = _ffn_fwd(xs, g1, f1w1, f1w3, f1w2, "ffn1_fwd", gather=shard_b)
    gathered_b = lax.dynamic_update_slice(gathered_b, shard_b[None], (chip, 0, 0))

    def full(i):
        return gathered_b[:, offs[i] - first_rows:offs[i + 1] - first_rows, :].reshape(N_CHIPS * sizes[i], D)

    f2w1, f2w3, f2w2, w_in_t, w_out_f = [full(i) for i in range(n_first, 8)]
    wa_t = full(8).reshape(D, QW)
    wb_t = full(9).reshape(D, BW)
    o0 = QW + 2 * KW
    rows = jnp.arange(w_in_t.shape[0])
    is_q = (rows < QW) | ((rows >= o0) & (rows < o0 + BW))
    w_in_s = w_in_t * jnp.where(is_q, QK_SCALE, 1.0).astype(BF16)[:, None]
    qkv_a = _norm_proj_heads(x1, gmix, w_in_s[:o0], SWA_BLOCK, "proj_swa")
    qkv_b = _norm_proj_heads(x1, gmix, w_in_s[o0:QKV], SB_BLOCK, "proj_sb")
    gates, h2 = _norm_matmul_nt(x1, gmix, w_in_t[QKV:], F32, "proj_gates")

    onehot = _bucket_onehot()
    bias = _bias_expand(W["rel_bias"].T, onehot).reshape(SWA_Q_HEADS, SWA_BLOCK, 2 * SWA_BLOCK)
    sinks = W["swa_sinks"].reshape(SWA_Q_HEADS)
    oa_t = _swa_fwd(qkv_a, bias, sinks)
    ob_t, saved_sb = _sb_fwd(qkv_b)

    x2, merged, ba, bb = _merge_fwd(x1, gates, oa_t, ob_t, wa_t, wb_t, w_out_f)
    x3, h3, a2, b2 = _ffn_fwd(x2, g3, f2w1, f2w3, f2w2, "ffn2_fwd")
    loss_part, dx3, dgf = _final_loss(x3, gf, target)

    dx2, dg3, dz2, da2, db2, u2 = _ffn_bwd(dx3, x2, g3, a2, b2, f2w1, f2w3, f2w2, "ffn2_bwd")
    grads = {}
    grads["ffn2_w1"] = _tn_matmul(da2, h3, "ffn2_dw1")
    grads["ffn2_w3"] = _tn_matmul(db2, h3, "ffn2_dw3")
    grads["ffn2_w2"] = _tn_matmul(u2, dz2, "ffn2_dw2")

    dx2b, dba, dbb, dgates, doa_t, dob_t = _merge_bwd(dx2, gates, ba, bb, wa_t.T, wb_t.T, w_out_f,
                                                      SWA_BLOCK, SB_BLOCK)
    grads["w_out"] = _tn_matmul(merged, dx2b, "dw_out")
    grads["w_branch_swa"] = _heads_matmul(oa_t, dba, 1.0, "dw_branch_swa").T
    grads["w_branch_sb"] = _heads_matmul(ob_t, dbb, 1.0, "dw_branch_sb").T

    dqb_t, dkb_t, dvb_t = _sb_bwd(qkv_b, dob_t, saved_sb)
    dqa_t, dka_t, dva_t, dbias, dsink_rows = _swa_bwd(qkv_a, bias, sinks, doa_t, oa_t)
    d_rel = _bias_reduce(dbias.reshape(SWA_Q_HEADS, -1), onehot).T
    d_sinks = jnp.sum(dsink_rows, axis=(1, 2))

    dheads = [(dqa_t, QK_SCALE, "q_swa"), (dka_t, 1.0, "k_swa"), (dva_t, 1.0, "v_swa"),
              (dqb_t, QK_SCALE, "q_sb"), (dkb_t, 1.0, "k_sb"), (dvb_t, 1.0, "v_sb")]
    grads["w_in"] = jnp.concatenate([_heads_matmul(a, h2, sc, "dw_in_" + nm).astype(BF16) for a, sc, nm in dheads]
                                    + [_tn_matmul(dgates, h2, "dw_in_gates")], axis=0)
    row0, pieces_in = 0, []
    for a, _, _ in dheads:
        pieces_in.append((a, row0))
        row0 += a.shape[0] * HEAD_DIM
    dx1, dgmix = _proj_bwd(pieces_in, dgates, w_in_s, x1, gmix, dx2)

    dx0, dg1, dz1, da1, db1, u1 = _ffn_bwd(dx1, xs, g1, a1, b1, f1w1, f1w3, f1w2, "ffn1_bwd")
    grads["ffn1_w1"] = _tn_matmul(da1, h1, "ffn1_dw1")
    grads["ffn1_w3"] = _tn_matmul(db1, h1, "ffn1_dw3")
    grads["ffn1_w2"] = _tn_matmul(u1, dz1, "ffn1_dw2")

    gparts = [grads[nk[0]].astype(BF16).reshape(N_CHIPS, sizes[i], D) for i, nk in enumerate(_BUF)]
    gbuf = jnp.concatenate(gparts, axis=1)
    R = gbuf.shape[1]
    half = R // 2
    c = lax.axis_index("c")
    mine = 2 * lax.axis_index("x") + lax.axis_index("y")
    from_sibling = _swap_halves(gbuf)
    my_half = lax.dynamic_slice_in_dim(gbuf, c * half, half, axis=1)
    chip_sum, chip_sum16 = _add_halves(my_half, from_sibling, "add_sibling")
    received = _scatter_to_owners(chip_sum16)
    own = lax.dynamic_index_in_dim(chip_sum, mine, axis=0, keepdims=False)
    my_rows = _add_received(own, received, "add_chips")
    their_rows = _swap_reduced(my_rows)
    reduced = jnp.concatenate([jnp.where(c == 0, my_rows, their_rows), jnp.where(c == 0, their_rows, my_rows)], axis=0)

    small_rows = [dg1, dgmix, dg3, dgf,
                  jnp.pad(d_sinks.reshape(1, -1), ((0, 0), (0, D - SWA_Q_HEADS))),
                  jnp.pad(d_rel.reshape(1, -1), ((0, 0), (0, D - REL_BUCKETS * SWA_Q_HEADS))),
                  jnp.pad(loss_part, ((0, 0), (0, D - 1))), jnp.zeros((1, D), F32)]
    small = _allreduce_small(jnp.concatenate(small_rows, axis=0))
    loss = small[6, 0]

    G = {}
    for i, nk in enumerate(_BUF):
        G[nk[0]] = _from_rows(nk, reduced[offs[i]:offs[i + 1]], W[nk[0]].shape[1])[None]
    G["norm_ffn1"], G["norm_mix"], G["norm_ffn2"] = small[0:1], small[1:2], small[2:3]
    G["norm_final"] = small[3]
    G["swa_sinks"] = small[4:5, :SWA_Q_HEADS]
    G["rel_bias"] = small[5, :REL_BUCKETS * SWA_Q_HEADS].reshape(REL_BUCKETS, SWA_Q_HEADS)

    delta, new_m, new_v = {}, {}, {}
    small_names = ["norm_ffn1", "norm_mix", "norm_ffn2", "norm_final", "swa_sinks", "rel_bias"]

    def pack(d):
        return jnp.concatenate([jnp.pad(d[n].reshape(1, -1), ((0, 0), (0, D - d[n].size))) for n in small_names]
                               + [jnp.zeros((2, D), F32)], axis=0)

    sd, sm, sv = _adamw(pack(W), pack(G), pack(M), pack(V), "adamw_small")
    for r, n in enumerate(small_names):
        for dst, src in ((delta, sd), (new_m, sm), (new_v, sv)):
            dst[n] = src[r, :W[n].size].reshape(W[n].shape)
    for nk in _BUF:
        n = nk[0]
        shp = W[n].shape
        two_d = (shp[1], shp[2])
        d_, m_, v_ = _adamw(W[n].reshape(two_d), G[n].reshape(two_d), M[n].reshape(two_d), V[n].reshape(two_d),
                            "adamw_" + n)
        delta[n], new_m[n], new_v[n] = d_.reshape(shp), m_.reshape(shp), v_.reshape(shp)

    return (loss, dx0[None], *[G[n] for n in names], *[delta[n] for n in names],
            *[new_m[n] for n in names], *[new_v[n] for n in names])
```

```python
import functools
import math

import jax
import jax.numpy as jnp
from jax import lax
from jax.experimental import pallas as pl
from jax.experimental.pallas import tpu as pltpu

F32, BF16 = jnp.float32, jnp.bfloat16
MESH_ID = pl.DeviceIdType.MESH
ANY = pl.BlockSpec(memory_space=pl.ANY)

RMS_EPS = 1e-6
HEAD_DIM = 64
SWA_Q_HEADS, SWA_KV_HEADS, SWA_GROUP = 8, 2, 4
SWA_BLOCK = 128
SB_HEADS = 8
SB_BLOCK = 256
REL_BUCKETS, REL_MAX_DIST = 32, 128
NEG_BIG = -1e30
QK_SCALE = HEAD_DIM ** -0.5
ADAM_LR, ADAM_B1, ADAM_B2, ADAM_EPS, ADAM_WD, ADAM_STEP = 0.001, 0.9, 0.999, 1e-08, 0.01, 10

N_CHIPS = 4
TOKEN_TILE = 512
MATMUL_TOKEN_TILE = 1024
WGRAD_ROW_TILES = (2176, 1408, 1024, 256)
FF_TILE = 1408
FFN_TOKEN_TILE = 512
FF_BWD_TILE = 256
VMEM_LIMIT = 56 * 1024 * 1024


def _cp(*sem):
    return pltpu.CompilerParams(dimension_semantics=sem, vmem_limit_bytes=VMEM_LIMIT)


def _nn(a, b):
    return jnp.dot(a, b, preferred_element_type=F32)


def _nt(a, b):
    return lax.dot_general(a, b, (((1,), (1,)), ((), ())), preferred_element_type=F32)


def _tn(a, b):
    return lax.dot_general(a, b, (((0,), (0,)), ((), ())), preferred_element_type=F32)


def _norm_fwd(x, g):
    return x * lax.rsqrt(jnp.mean(x * x, axis=-1, keepdims=True) + RMS_EPS) * g


def _norm_bwd(x, g, dh):
    r = lax.rsqrt(jnp.mean(x * x, axis=-1, keepdims=True) + RMS_EPS)
    xh = x * r
    dxh = dh * g
    dx = r * (dxh - xh * jnp.mean(dxh * xh, axis=-1, keepdims=True))
    return dx, jnp.sum(dh * xh, axis=0, keepdims=True)


def _softplus(z):
    return jnp.maximum(z, 0.0) + jnp.log(1.0 + jnp.exp(-jnp.abs(z)))


def _ffn_fwd(x, g, w1t, w3t, w2, name, gather=None):
    S, D = x.shape
    F = w2.shape[0]
    tm, tf = min(FFN_TOKEN_TILE, S), FF_TILE
    ni, nj = S // tm, F // tf

    def body(x_ref, g_ref, w1_ref, w3_ref, w2_ref, *rest):
        if gather is None:
            xo_ref, h_ref, a_ref, b_ref, hs, acc = rest
        else:
            shard_ref, xo_ref, h_ref, a_ref, b_ref, gathered_ref, hs, acc, send_sems, recv_sems = rest
        i, j = pl.program_id(0), pl.program_id(1)
        if gather is not None:
            for when, phase in ((jnp.logical_and(i == 0, j == 0), "start"),
                                (jnp.logical_and(i == ni - 1, j == 0), "forward"),
                                (jnp.logical_and(i == ni - 1, j == nj - 1), "finish")):
                @pl.when(when)
                def _():
                    getattr(_gather_exchange(shard_ref, gathered_ref, send_sems, recv_sems), phase)()

        @pl.when(j == 0)
        def _():
            hb = _norm_fwd(x_ref[...], g_ref[...]).astype(BF16)
            hs[...] = hb
            h_ref[...] = hb
            acc[...] = jnp.zeros_like(acc)

        h = hs[...]
        a = _nt(h, w1_ref[...])
        b = _nt(h, w3_ref[...])
        a_ref[...] = a.astype(BF16)
        b_ref[...] = b.astype(BF16)
        u = a * jax.nn.sigmoid(a) * b
        acc[...] += _nn(u.astype(BF16), w2_ref[...])

        @pl.when(j == nj - 1)
        def _():
            xo_ref[...] = x_ref[...] + 0.5 * acc[...]

    in_specs = [pl.BlockSpec((tm, D), lambda i, j: (i, 0)),
                pl.BlockSpec((1, D), lambda i, j: (0, 0)),
                pl.BlockSpec((tf, D), lambda i, j: (j, 0)),
                pl.BlockSpec((tf, D), lambda i, j: (j, 0)),
                pl.BlockSpec((tf, D), lambda i, j: (j, 0))]
    out_specs = [pl.BlockSpec((tm, D), lambda i, j: (i, 0)),
                 pl.BlockSpec((tm, D), lambda i, j: (i, 0)),
                 pl.BlockSpec((tm, tf), lambda i, j: (i, j)),
                 pl.BlockSpec((tm, tf), lambda i, j: (i, j))]
    out_shape = [jax.ShapeDtypeStruct((S, D), F32), jax.ShapeDtypeStruct((S, D), BF16),
                 jax.ShapeDtypeStruct((S, F), BF16), jax.ShapeDtypeStruct((S, F), BF16)]
    scratch = [pltpu.VMEM((tm, D), BF16), pltpu.VMEM((tm, D), F32)]
    operands = [x, g, w1t, w3t, w2]
    if gather is not None:
        R, C = gather.shape
        in_specs.append(ANY)
        out_specs.append(ANY)
        out_shape.append(jax.ShapeDtypeStruct((N_CHIPS, 2, R // 2, C), gather.dtype))
        scratch += [pltpu.SemaphoreType.DMA((6,)), pltpu.SemaphoreType.DMA((6,))]
        operands.append(gather.reshape(2, R // 2, C))
    outs = list(pl.pallas_call(
        body, name=name, grid=(ni, nj), in_specs=in_specs, out_specs=out_specs, out_shape=out_shape,
        scratch_shapes=scratch, compiler_params=_cp("arbitrary", "arbitrary"),
    )(*operands))
    if gather is not None:
        outs[4] = outs[4].reshape(N_CHIPS, R, C)
    return outs


def _ffn_bwd(dxo, x, g, a, b, w1t, w3t, w2, name, scatter=None):
    S, D = x.shape
    F = w2.shape[0]
    tm, tf = min(MATMUL_TOKEN_TILE, S), FF_BWD_TILE
    ni, nj = S // tm, F // tf

    def body(dxo_ref, x_ref, g_ref, a_ref, b_ref, w1_ref, w3_ref, w2_ref, *rest):
        if scatter is None:
            dx_ref, dg_ref, dz_ref, da_ref, db_ref, u_ref, dzs, acc = rest
        else:
            (parts_ref, dx_ref, dg_ref, dz_ref, da_ref, db_ref, u_ref, recv_ref,
             dzs, acc, send_sems, recv_sems) = rest
        i, j = pl.program_id(0), pl.program_id(1)
        if scatter is not None:
            for when, phase in ((jnp.logical_and(i == 0, j == 0), "start"),
                                (jnp.logical_and(i == ni - 1, j == nj - 1), "finish")):
                @pl.when(when)
                def _():
                    getattr(_scatter_exchange(parts_ref, recv_ref, send_sems, recv_sems), phase)()

        @pl.when(j == 0)
        def _():
            dzb = (0.5 * dxo_ref[...]).astype(BF16)
            dzs[...] = dzb
            dz_ref[...] = dzb
            acc[...] = jnp.zeros_like(acc)

        du = _nt(dzs[...], w2_ref[...])
        av = a_ref[...].astype(F32)
        bv = b_ref[...].astype(F32)
        s = jax.nn.sigmoid(av)
        silu = av * s
        db = (du * silu).astype(BF16)
        da = (du * bv * (s * (1.0 + av * (1.0 - s)))).astype(BF16)
        da_ref[...] = da
        db_ref[...] = db
        u_ref[...] = (silu * bv).astype(BF16)
        acc[...] += _nn(da, w1_ref[...]) + _nn(db, w3_ref[...])

        @pl.when(j == nj - 1)
        def _():
            dx, dg = _norm_bwd(x_ref[...], g_ref[...], acc[...])
            dx_ref[...] = dxo_ref[...] + dx

            @pl.when(i == 0)
            def _():
                dg_ref[...] = dg

            @pl.when(i > 0)
            def _():
                dg_ref[...] += dg

    row = pl.BlockSpec((tm, D), lambda i, j: (i, 0))
    wsp = pl.BlockSpec((tf, D), lambda i, j: (j, 0))
    col = pl.BlockSpec((tm, tf), lambda i, j: (i, j))
    vec = pl.BlockSpec((1, D), lambda i, j: (0, 0))
    in_specs = [row, row, vec, col, col, wsp, wsp, wsp]
    out_specs = [row, vec, row, col, col, col]
    out_shape = [jax.ShapeDtypeStruct((S, D), F32), jax.ShapeDtypeStruct((1, D), F32),
                 jax.ShapeDtypeStruct((S, D), BF16), jax.ShapeDtypeStruct((S, F), BF16),
                 jax.ShapeDtypeStruct((S, F), BF16), jax.ShapeDtypeStruct((S, F), BF16)]
    scratch = [pltpu.VMEM((tm, D), BF16), pltpu.VMEM((tm, D), F32)]
    operands = [dxo, x, g, a, b, w1t, w3t, w2]
    if scatter is not None:
        in_specs.append(ANY)
        out_specs.append(ANY)
        out_shape.append(jax.ShapeDtypeStruct((3,) + scatter.shape[1:], scatter.dtype))
        scratch += [pltpu.SemaphoreType.DMA((3,)), pltpu.SemaphoreType.DMA((3,))]
        operands.append(scatter)
    return pl.pallas_call(
        body, name=name, grid=(ni, nj), in_specs=in_specs, out_specs=out_specs, out_shape=out_shape,
        scratch_shapes=scratch, compiler_params=_cp("arbitrary", "arbitrary"),
    )(*operands)


def _tn_matmul(a, b, name):
    S, M = a.shape
    N = b.shape[1]
    ts = min(MATMUL_TOKEN_TILE, S)
    tmm = next(t for t in WGRAD_ROW_TILES if M % t == 0)
    ns = S // ts

    def body(a_ref, b_ref, o_ref, acc):
        s = pl.program_id(1)
        part = _tn(a_ref[...], b_ref[...])

        @pl.when(s == 0)
        def _():
            acc[...] = part

        @pl.when(s > 0)
        def _():
            acc[...] += part

        @pl.when(s == ns - 1)
        def _():
            o_ref[...] = acc[...].astype(BF16)

    return pl.pallas_call(
        body, name=name, grid=(M // tmm, ns),
        in_specs=[pl.BlockSpec((ts, tmm), lambda m, s: (s, m)),
                  pl.BlockSpec((ts, N), lambda m, s: (s, 0))],
        out_specs=pl.BlockSpec((tmm, N), lambda m, s: (m, 0)),
        out_shape=jax.ShapeDtypeStruct((M, N), BF16),
        scratch_shapes=[pltpu.VMEM((tmm, N), F32)],
        compiler_params=_cp("arbitrary", "arbitrary"),
    )(a, b)


def _norm_matmul_nt(x, g, wt, out_dtype, name):
    S, D = x.shape
    N = wt.shape[0]
    tm = min(MATMUL_TOKEN_TILE, S)
    tn = next(t for t in (1024, 768, 256) if N % t == 0)

    def body(x_ref, g_ref, w_ref, o_ref, h_ref, hs):
        @pl.when(pl.program_id(1) == 0)
        def _():
            hb = _norm_fwd(x_ref[...], g_ref[...]).astype(BF16)
            hs[...] = hb
            h_ref[...] = hb

        o_ref[...] = _nt(hs[...], w_ref[...]).astype(out_dtype)

    return pl.pallas_call(
        body, name=name, grid=(S // tm, N // tn),
        in_specs=[pl.BlockSpec((tm, D), lambda i, j: (i, 0)),
                  pl.BlockSpec((1, D), lambda i, j: (0, 0)),
                  pl.BlockSpec((tn, D), lambda i, j: (j, 0))],
        out_specs=[pl.BlockSpec((tm, tn), lambda i, j: (i, j)),
                   pl.BlockSpec((tm, D), lambda i, j: (i, 0))],
        out_shape=[jax.ShapeDtypeStruct((S, N), out_dtype), jax.ShapeDtypeStruct((S, D), BF16)],
        scratch_shapes=[pltpu.VMEM((tm, D), BF16)],
        compiler_params=_cp("arbitrary", "arbitrary"),
    )(x, g, wt)


def _heads_tile(ref):
    Hh, nbk = ref.shape[0], ref.shape[1]
    return jnp.concatenate([jnp.concatenate([ref[h, b] for b in range(nbk)], axis=1) for h in range(Hh)], axis=0)


def _store_heads(ref, val):
    Hh, nbk, dh, T = ref.shape
    for h in range(Hh):
        for b in range(nbk):
            ref[h, b] = val[h * dh:(h + 1) * dh, b * T:(b + 1) * T].astype(ref.dtype)


def _norm_proj_heads(x, g, w_rows, T, name):
    S, D = x.shape
    N = w_rows.shape[0]
    tm, tn = min(MATMUL_TOKEN_TILE, S), 768

    def body(x_ref, g_ref, w_ref, o_ref, hs):
        @pl.when(pl.program_id(1) == 0)
        def _():
            hs[...] = _norm_fwd(x_ref[...], g_ref[...]).astype(BF16)

        _store_heads(o_ref, _nt(w_ref[...], hs[...]))

    return pl.pallas_call(
        body, name=name, grid=(S // tm, N // tn),
        in_specs=[pl.BlockSpec((tm, D), lambda i, j: (i, 0)),
                  pl.BlockSpec((1, D), lambda i, j: (0, 0)),
                  pl.BlockSpec((tn, D), lambda i, j: (j, 0))],
        out_specs=pl.BlockSpec((tn // HEAD_DIM, tm // T, HEAD_DIM, T), lambda i, j: (j, i, 0, 0)),
        out_shape=jax.ShapeDtypeStruct((N // HEAD_DIM, S // T, HEAD_DIM, T), BF16),
        scratch_shapes=[pltpu.VMEM((tm, D), BF16)],
        compiler_params=_cp("arbitrary", "arbitrary"),
    )(x, g, w_rows)


def _heads_matmul(at, b, scale, name):
    Hh, nb, dh, T = at.shape
    S, N = b.shape
    ts = min(MATMUL_TOKEN_TILE, S)
    ns = S // ts

    def body(a_ref, b_ref, o_ref):
        s = pl.program_id(0)
        a = _heads_tile(a_ref)
        part = _nn((a if scale == 1.0 else a * scale).astype(BF16), b_ref[...])

        @pl.when(s == 0)
        def _():
            o_ref[...] = part

        @pl.when(s > 0)
        def _():
            o_ref[...] += part

    return pl.pallas_call(
        body, name=name, grid=(ns,),
        in_specs=[pl.BlockSpec((Hh, ts // T, dh, T), lambda s: (0, s, 0, 0)),
                  pl.BlockSpec((ts, N), lambda s: (s, 0))],
        out_specs=pl.BlockSpec((Hh * dh, N), lambda s: (0, 0)),
        out_shape=jax.ShapeDtypeStruct((Hh * dh, N), F32),
        compiler_params=_cp("arbitrary"),
    )(at, b)


def _proj_bwd(pieces, dgates, w_rows, x, g, dres):
    S, D = x.shape
    tm = min(TOKEN_TILE, S)
    n_p = len(pieces)
    gate_row = w_rows.shape[0] - dgates.shape[1]

    def body(*refs):
        p_refs = refs[:n_p]
        dgt_ref, w_ref, x_ref, g_ref, dres_ref, dx_ref, dg_ref = refs[n_p:]
        i = pl.program_id(0)
        dh = _nn(dgt_ref[...], w_ref[gate_row:, :])
        for p_ref, (arr, row0) in zip(p_refs, pieces):
            rows = arr.shape[0] * arr.shape[2]
            dh += _tn(_heads_tile(p_ref).astype(BF16), w_ref[row0:row0 + rows, :])
        dx, dg = _norm_bwd(x_ref[...], g_ref[...], dh)
        dx_ref[...] = dres_ref[...] + dx

        @pl.when(i == 0)
        def _():
            dg_ref[...] = dg

        @pl.when(i > 0)
        def _():
            dg_ref[...] += dg

    row = pl.BlockSpec((tm, D), lambda i: (i, 0))
    vec = pl.BlockSpec((1, D), lambda i: (0, 0))
    p_specs = [pl.BlockSpec((a.shape[0], tm // a.shape[3], a.shape[2], a.shape[3]), lambda i: (0, i, 0, 0))
               for a, _ in pieces]
    return pl.pallas_call(
        body, name="proj_bwd", grid=(S // tm,),
        in_specs=p_specs + [pl.BlockSpec((tm, dgates.shape[1]), lambda i: (i, 0)),
                            pl.BlockSpec(w_rows.shape, lambda i: (0, 0), pipeline_mode=pl.Buffered(1)),
                            row, vec, row],
        out_specs=[row, vec],
        out_shape=[jax.ShapeDtypeStruct((S, D), F32), jax.ShapeDtypeStruct((1, D), F32)],
        compiler_params=_cp("arbitrary"),
    )(*[a for a, _ in pieces], dgates, w_rows, x, g, dres)


def _merge_fwd(x1, gates, oa_t, ob_t, wat, wbt, w_out):
    S, D = x1.shape
    W = wat.shape[1]
    tm = min(TOKEN_TILE, S)

    def body(x_ref, ga_ref, gb_ref, oa_ref, ob_ref, wa_ref, wb_ref, wo_ref,
             x2_ref, mg_ref, ba_ref, bb_ref):
        ba = _nt(_heads_tile(oa_ref).T.astype(BF16), wa_ref[...])
        bb = _nt(_heads_tile(ob_ref).T.astype(BF16), wb_ref[...])
        merged = jax.nn.sigmoid(ga_ref[...]) * ba + jax.nn.sigmoid(gb_ref[...]) * bb
        mb = merged.astype(BF16)
        mg_ref[...] = mb
        ba_ref[...] = ba.astype(BF16)
        bb_ref[...] = bb.astype(BF16)
        x2_ref[...] = x_ref[...] + _nn(mb, wo_ref[...])

    row = pl.BlockSpec((tm, D), lambda i: (i, 0))
    full = lambda r, c: pl.BlockSpec((r, c), lambda i: (0, 0))
    heads = lambda a: pl.BlockSpec((a.shape[0], tm // a.shape[3], a.shape[2], a.shape[3]), lambda i: (0, i, 0, 0))
    return pl.pallas_call(
        body, name="merge_fwd", grid=(S // tm,),
        in_specs=[row, pl.BlockSpec((tm, D), lambda i: (i, 0)), pl.BlockSpec((tm, D), lambda i: (i, 1)),
                  heads(oa_t), heads(ob_t), full(D, W), full(D, W), full(D, D)],
        out_specs=[row, row, row, row],
        out_shape=[jax.ShapeDtypeStruct((S, D), F32)] + [jax.ShapeDtypeStruct((S, D), BF16)] * 3,
        compiler_params=_cp("arbitrary"),
    )(x1, gates, gates, oa_t, ob_t, wat, wbt, w_out)


def _merge_bwd(dx2, gates, ba, bb, wa, wb, w_out, t_a, t_b):
    S, D = dx2.shape
    W = wa.shape[0]
    tm = min(TOKEN_TILE, S)
    Hh = W // HEAD_DIM

    def body(dx_ref, ga_ref, gb_ref, ba_ref, bb_ref, wa_ref, wb_ref, wo_ref,
             dxb_ref, dba_ref, dbb_ref, dgt_ref, doa_ref, dob_ref):
        dxb = dx_ref[...].astype(BF16)
        dxb_ref[...] = dxb
        dm = _nt(dxb, wo_ref[...])
        sa = jax.nn.sigmoid(ga_ref[...])
        sb = jax.nn.sigmoid(gb_ref[...])
        dba = (dm * sa).astype(BF16)
        dbb = (dm * sb).astype(BF16)
        dba_ref[...] = dba
        dbb_ref[...] = dbb
        dgt_ref[:, :D] = (dm * ba_ref[...].astype(F32) * sa * (1.0 - sa)).astype(BF16)
        dgt_ref[:, D:] = (dm * bb_ref[...].astype(F32) * sb * (1.0 - sb)).astype(BF16)
        _store_heads(doa_ref, _nt(wa_ref[...], dba))
        _store_heads(dob_ref, _nt(wb_ref[...], dbb))

    row = pl.BlockSpec((tm, D), lambda i: (i, 0))
    full = lambda r, c: pl.BlockSpec((r, c), lambda i: (0, 0))
    heads = lambda T: pl.BlockSpec((Hh, tm // T, HEAD_DIM, T), lambda i: (0, i, 0, 0))
    return pl.pallas_call(
        body, name="merge_bwd", grid=(S // tm,),
        in_specs=[row, pl.BlockSpec((tm, D), lambda i: (i, 0)), pl.BlockSpec((tm, D), lambda i: (i, 1)),
                  row, row, full(W, D), full(W, D), full(D, D)],
        out_specs=[row, row, row, pl.BlockSpec((tm, 2 * D), lambda i: (i, 0)), heads(t_a), heads(t_b)],
        out_shape=[jax.ShapeDtypeStruct((S, D), BF16)] * 3 + [jax.ShapeDtypeStruct((S, 2 * D), BF16),
                   jax.ShapeDtypeStruct((Hh, S // t_a, HEAD_DIM, t_a), F32),
                   jax.ShapeDtypeStruct((Hh, S // t_b, HEAD_DIM, t_b), BF16)],
        compiler_params=_cp("arbitrary"),
    )(dx2, gates, gates, ba, bb, wa, wb, w_out)


def _final_loss(x3, gf, target):
    S, D = x3.shape
    tm = min(TOKEN_TILE, S)

    def body(x_ref, g_ref, t_ref, loss_ref, dx_ref, dg_ref):
        i = pl.program_id(0)
        x = x_ref[...]
        g = g_ref[...]
        e = _norm_fwd(x, g) - t_ref[...]
        part = 0.5 * jnp.sum(jnp.mean(e * e, axis=-1, keepdims=True), axis=0, keepdims=True)
        dx, dg = _norm_bwd(x, g, e * (1.0 / D))
        dx_ref[...] = dx

        @pl.when(i == 0)
        def _():
            loss_ref[...] = part
            dg_ref[...] = dg

        @pl.when(i > 0)
        def _():
            loss_ref[...] += part
            dg_ref[...] += dg

    row = pl.BlockSpec((tm, D), lambda i: (i, 0))
    vec = pl.BlockSpec((1, D), lambda i: (0, 0))
    return pl.pallas_call(
        body, name="final_loss", grid=(S // tm,),
        in_specs=[row, vec, row],
        out_specs=[pl.BlockSpec((1, 1), lambda i: (0, 0)), row, vec],
        out_shape=[jax.ShapeDtypeStruct((1, 1), F32), jax.ShapeDtypeStruct((S, D), F32),
                   jax.ShapeDtypeStruct((1, D), F32)],
        compiler_params=_cp("arbitrary"),
    )(x3, gf, target)


SB_FWD_HEAD_GROUP = 8
SB_HEAD_GROUP = 4
LANES = 128


def _tri(T, kind):
    r = lax.broadcasted_iota(jnp.int32, (T, T), 0)
    c = lax.broadcasted_iota(jnp.int32, (T, T), 1)
    return {"after": r > c, "upto": r <= c, "before": r < c}[kind].astype(BF16)


def _lane(v, j):
    return jnp.broadcast_to(v[:, j:j + 1], (v.shape[0], LANES))


def _t_bf16(x):
    return x.astype(F32).T.astype(BF16)


def _wide(v, T):
    return jnp.tile(v, (1, T // LANES))


SB_SLOTS = 3
SB_FWD_SLOTS = 2
COPY_PARTS = 4


class _split_copy:
    def __init__(self, src, dst, sems):
        n = src.shape[0] // COPY_PARTS
        self.parts = [pltpu.make_async_copy(src.at[pl.ds(r * n, n)], dst.at[pl.ds(r * n, n)], sems.at[r])
                      for r in range(COPY_PARTS)]

    def start(self):
        for cp in self.parts:
            cp.start()

    def wait(self):
        for cp in self.parts:
            cp.wait()


def _sb_pair(i, kb):
    return (i * (i + 1)) // 2 + kb


def _sb_fwd(qkv):
    H3, nb, dh, T = qkv.shape
    H = H3 // 3
    HG = SB_FWD_HEAD_GROUP
    assert HG == H, "one head group: a saved tile holds all the heads"
    n_pairs = (nb * (nb + 1)) // 2

    def body(q_ref, k_ref, v_ref, o_ref, saved_ref, stage, sems):
        row = lax.broadcasted_iota(jnp.int32, (T, T), 0)
        col = lax.broadcasted_iota(jnp.int32, (T, T), 1)
        tri = col < row
        after = _tri(T, "after")

        def save(slot, pair):
            return _split_copy(stage.at[slot], saved_ref.at[pair], sems.at[slot])

        def blocks(qs, i, kb, step, carry, diag):
            hs = range(HG)
            slot = step % SB_FWD_SLOTS

            @pl.when(step >= SB_FWD_SLOTS)
            def _():
                save(slot, 0).wait()

            z = [_nn(qs[hh], k_ref[hh, kb]) for hh in hs]
            res, ls, first = [None] * HG, [None] * HG, [None] * HG
            for hh in hs:
                sp = _softplus(z[hh])
                if diag:
                    sp = jnp.where(tri, sp, 0.0)
                ls[hh] = z[hh] - sp
                spb = sp.astype(BF16)
                first[hh] = _lane(spb.astype(F32), 0)
                res[hh] = _nn(spb, after)
            out = []
            for hh in hs:
                c, oacc = carry[2 * hh], carry[2 * hh + 1]
                a = jnp.exp(ls[hh] - (res[hh] + _wide(c, T)))
                if diag:
                    a = jnp.where(tri, a, 0.0)
                ab = a.astype(BF16)
                stage[slot, hh, 0] = ab
                stage[slot, hh, 1] = jnp.exp(ls[hh]).astype(BF16)
                out.extend([c + (first[hh] + _lane(res[hh], 0)), oacc + _nt(v_ref[hh, kb], ab)])
            save(slot, _sb_pair(i, kb)).start()
            return tuple(out)

        def qblock(i, step):
            qs = [_t_bf16(q_ref[hh, i]) for hh in range(HG)]
            carry = blocks(qs, i, i, step, (jnp.zeros((T, LANES), F32), jnp.zeros((dh, T), F32)) * HG, True)

            def kstep(t, carry):
                return blocks(qs, i, i - 1 - t, step + 1 + t, carry, False)

            carry = lax.fori_loop(0, i, kstep, carry)
            for hh in range(HG):
                o_ref[hh, i] = carry[2 * hh + 1]
            return step + 1 + i

        lax.fori_loop(0, nb, qblock, 0)
        for slot in range(min(SB_FWD_SLOTS, n_pairs)):
            save(slot, 0).wait()

    ht = lambda part: pl.BlockSpec((HG, nb, dh, T), lambda h: (part, 0, 0, 0), pipeline_mode=pl.Buffered(1))
    return pl.pallas_call(
        body, name="sb_fwd", grid=(1,),
        in_specs=[ht(0), ht(1), ht(2)],
        out_specs=[ht(0), ANY],
        out_shape=[jax.ShapeDtypeStruct((H, nb, dh, T), F32),
                   jax.ShapeDtypeStruct((n_pairs, H, 2, T, T), BF16)],
        scratch_shapes=[pltpu.VMEM((SB_FWD_SLOTS, HG, 2, T, T), BF16),
                        pltpu.SemaphoreType.DMA((SB_FWD_SLOTS, COPY_PARTS))],
        compiler_params=_cp("arbitrary"),
    )(qkv, qkv, qkv)


def _sb_bwd(qkv, dot, saved):
    H3, nb, dh, T = qkv.shape
    H = H3 // 3
    HG = SB_HEAD_GROUP
    n_pairs = (nb * (nb + 1)) // 2

    def body(qt_ref, k_ref, v_ref, dot_ref, saved_ref, dq_ref, dk_ref, dv_ref, stage, sems):
        head0 = pl.program_id(0) * HG
        row = lax.broadcasted_iota(jnp.int32, (T, T), 0)
        col = lax.broadcasted_iota(jnp.int32, (T, T), 1)
        tri = col < row
        before = _tri(T, "before")
        dk_ref[...] = jnp.zeros_like(dk_ref)
        dv_ref[...] = jnp.zeros_like(dv_ref)

        def fetch(slot, pair):
            return _split_copy(saved_ref.at[pair, pl.ds(head0, HG)], stage.at[slot], sems.at[slot])

        for ahead in range(min(SB_SLOTS - 1, n_pairs)):
            fetch(ahead, ahead).start()

        def blocks(qTs, dos, doTs, i, kb, carry, diag):
            hs = range(HG)
            pair = _sb_pair(i, kb)
            slot = pair % SB_SLOTS
            fetch(slot, pair).wait()
            nxt = pair + (SB_SLOTS - 1)

            @pl.when(nxt < n_pairs)
            def _():
                fetch(nxt % SB_SLOTS, nxt).start()

            kT = [k_ref[hh, kb] for hh in hs]
            da = [_nn(dos[hh], v_ref[hh, kb]) for hh in hs]
            g, gb, resg = [None] * HG, [None] * HG, [None] * HG
            for hh in hs:
                g[hh] = stage[slot, hh, 0].astype(F32) * da[hh]
                gb[hh] = g[hh].astype(BF16)
                resg[hh] = _nn(gb[hh], before)
            out = []
            for hh in hs:
                pre_g, dq = carry[2 * hh], carry[2 * hh + 1]
                dz = g[hh] - (g[hh] + (resg[hh] + _wide(pre_g, T))) * stage[slot, hh, 1].astype(F32)
                if diag:
                    dz = jnp.where(tri, dz, 0.0)
                dzb = dz.astype(BF16)
                dk_ref[hh, kb] += _nn(qTs[hh], dzb)
                dv_ref[hh, kb] += _nn(doTs[hh], stage[slot, hh, 0])
                out.extend([pre_g + (_lane(resg[hh], T - 1) + _lane(gb[hh].astype(F32), T - 1)),
                            dq + _nt(kT[hh], dzb)])
            return tuple(out)

        def qblock(i, _):
            qTs = [qt_ref[hh, i] for hh in range(HG)]
            doTs = [dot_ref[hh, i] for hh in range(HG)]
            dos = [_t_bf16(v) for v in doTs]
            carry = (jnp.zeros((T, LANES), F32), jnp.zeros((dh, T), F32)) * HG

            def kstep(kb, carry):
                return blocks(qTs, dos, doTs, i, kb, carry, False)

            carry = lax.fori_loop(0, i, kstep, carry)
            carry = blocks(qTs, dos, doTs, i, i, carry, True)
            for hh in range(HG):
                dq_ref[hh, i] = carry[2 * hh + 1]
            return 0

        lax.fori_loop(0, nb, qblock, 0)

    G = H // HG
    ht = lambda part: pl.BlockSpec((HG, nb, dh, T), lambda h: (h + part * G, 0, 0, 0),
                                   pipeline_mode=pl.Buffered(1))
    return pl.pallas_call(
        body, name="sb_bwd", grid=(G,),
        in_specs=[ht(0), ht(1), ht(2), ht(0), ANY],
        out_specs=[ht(0), ht(0), ht(0)],
        out_shape=[jax.ShapeDtypeStruct((H, nb, dh, T), F32)] * 3,
        scratch_shapes=[pltpu.VMEM((SB_SLOTS, HG, 2, T, T), BF16), pltpu.SemaphoreType.DMA((SB_SLOTS, COPY_PARTS))],
        compiler_params=_cp("arbitrary"),
    )(qkv, qkv, qkv, dot, saved)


def _swa_probs(zp, zc, bias, sink, first):
    T = zp.shape[0]
    row = lax.broadcasted_iota(jnp.int32, (T, T), 0)
    col = lax.broadcasted_iota(jnp.int32, (T, T), 1)
    lp = jnp.where(jnp.logical_and(col > row, jnp.logical_not(first)), zp + bias[:, :T], NEG_BIG)
    lc = jnp.where(col <= row, zc + bias[:, T:], NEG_BIG)
    m = jnp.maximum(jnp.maximum(jnp.max(lp, axis=1, keepdims=True), jnp.max(lc, axis=1, keepdims=True)), sink)
    pp = jnp.exp(lp - m)
    pc = jnp.exp(lc - m)
    ps = jnp.exp(sink - m)
    inv = 1.0 / (jnp.sum(pp, axis=1, keepdims=True) + jnp.sum(pc, axis=1, keepdims=True) + ps)
    return pp * inv, pc * inv, ps * inv


def _swa_specs(nb, dh, T, grp, Hq, Hkv, clamp):
    blk = (lambda n: jnp.minimum(n, nb - 1)) if clamp else (lambda n: n)
    q = pl.BlockSpec((grp, None, dh, T), lambda h, n: (h, blk(n), 0, 0))
    one = lambda first, back: pl.BlockSpec(
        (None, None, dh, T), lambda h, n: (first + h, jnp.maximum(blk(n) - back, 0) if back else blk(n), 0, 0))
    return q, [one(Hq, 1), one(Hq, 0), one(Hq + Hkv, 1), one(Hq + Hkv, 0)]


def _swa_fwd(qkv, bias, sinks):
    Hq, Hkv, grp = SWA_Q_HEADS, SWA_KV_HEADS, SWA_GROUP
    _, nb, dh, T = qkv.shape

    def body(sink_ref, q_ref, kp_ref, kc_ref, vp_ref, vc_ref, bias_ref, o_ref):
        hk, n = pl.program_id(0), pl.program_id(1)
        kp, kc, vp, vc = kp_ref[...], kc_ref[...], vp_ref[...], vc_ref[...]
        qs = [_t_bf16(q_ref[g]) for g in range(grp)]
        zs = [(_nn(q, kp), _nn(q, kc)) for q in qs]
        for g in range(grp):
            pp, pc, _ = _swa_probs(*zs[g], bias_ref[g], sink_ref[hk * grp + g], n == 0)
            o_ref[g] = _nt(vp, pp.astype(BF16)) + _nt(vc, pc.astype(BF16))

    q_spec, kv_specs = _swa_specs(nb, dh, T, grp, Hq, Hkv, False)
    return pl.pallas_call(
        body, name="swa_fwd", grid=(Hkv, nb),
        in_specs=[pl.BlockSpec(memory_space=pltpu.SMEM), q_spec] + kv_specs
                 + [pl.BlockSpec((grp, T, 2 * T), lambda h, n: (h, 0, 0))],
        out_specs=pl.BlockSpec((grp, None, dh, T), lambda h, n: (h, n, 0, 0)),
        out_shape=jax.ShapeDtypeStruct((Hq, nb, dh, T), F32),
        compiler_params=_cp("arbitrary", "arbitrary"),
    )(sinks, qkv, qkv, qkv, qkv, qkv, bias)


def _swa_bwd(qkv, bias, sinks, dot, ot):
    Hq, Hkv, grp = SWA_Q_HEADS, SWA_KV_HEADS, SWA_GROUP
    _, nb, dh, T = qkv.shape

    def body(sink_ref, qt_ref, kp_ref, kc_ref, vp_ref, vc_ref, bias_ref, dot_ref, ot_ref,
             dq_ref, dk_ref, dv_ref, dbias_ref, dsink_ref, ck, cv):
        hk, n = pl.program_id(0), pl.program_id(1)

        @pl.when(n == 0)
        def _():
            dbias_ref[...] = jnp.zeros_like(dbias_ref)
            dsink_ref[...] = jnp.zeros_like(dsink_ref)
            ck[...] = jnp.zeros_like(ck)
            cv[...] = jnp.zeros_like(cv)

        @pl.when(n < nb)
        def _():
            kp, kc, vp, vc = kp_ref[...], kc_ref[...], vp_ref[...], vc_ref[...]
            kprev = jnp.zeros((dh, T), F32)
            vprev = jnp.zeros((dh, T), F32)
            kcur = jnp.zeros((dh, T), F32)
            vcur = jnp.zeros((dh, T), F32)
            qTs = [qt_ref[g] for g in range(grp)]
            qs = [_t_bf16(v) for v in qTs]
            dos = [dot_ref[g].T for g in range(grp)]
            zs = [(_nn(q, kp), _nn(q, kc)) for q in qs]
            dps = [(_nn(do.astype(BF16), vp), _nn(do.astype(BF16), vc)) for do in dos]
            dls, pbs = [], []
            for g in range(grp):
                pp, pc, ps = _swa_probs(*zs[g], bias_ref[g], sink_ref[hk * grp + g], n == 0)
                delta = jnp.sum(dos[g] * ot_ref[g].T, axis=1, keepdims=True)
                dlp = pp * (dps[g][0] - delta)
                dlc = pc * (dps[g][1] - delta)
                dbias_ref[g, :, :T] += dlp
                dbias_ref[g, :, T:] += dlc
                dsink_ref[g] += -ps * delta
                dls.append((dlp.astype(BF16), dlc.astype(BF16)))
                pbs.append((pp.astype(BF16), pc.astype(BF16)))
            for g in range(grp):
                dlpb, dlcb = dls[g]
                doT = dot_ref[g].astype(BF16)
                dq_ref[g] = _nt(kp, dlpb) + _nt(kc, dlcb)
                kprev += _nn(qTs[g], dlpb)
                kcur += _nn(qTs[g], dlcb)
                vprev += _nn(doT, pbs[g][0])
                vcur += _nn(doT, pbs[g][1])
            dk_ref[...] = ck[...] + kprev
            dv_ref[...] = cv[...] + vprev
            ck[...] = kcur
            cv[...] = vcur

        @pl.when(n == nb)
        def _():
            dk_ref[...] = ck[...]
            dv_ref[...] = cv[...]

    qt_spec, kv_specs = _swa_specs(nb, dh, T, grp, Hq, Hkv, True)
    prev = pl.BlockSpec((None, None, dh, T), lambda h, n: (h, jnp.maximum(n - 1, 0), 0, 0))
    per_group = lambda a, b: pl.BlockSpec((grp, a, b), lambda h, n: (h, 0, 0))
    return pl.pallas_call(
        body, name="swa_bwd", grid=(Hkv, nb + 1),
        in_specs=[pl.BlockSpec(memory_space=pltpu.SMEM), qt_spec] + kv_specs
                 + [per_group(T, 2 * T), qt_spec, qt_spec],
        out_specs=[qt_spec, prev, prev, per_group(T, 2 * T), per_group(T, 1)],
        out_shape=[jax.ShapeDtypeStruct((Hq, nb, dh, T), F32), jax.ShapeDtypeStruct((Hkv, nb, dh, T), F32),
                   jax.ShapeDtypeStruct((Hkv, nb, dh, T), F32), jax.ShapeDtypeStruct((Hq, T, 2 * T), F32),
                   jax.ShapeDtypeStruct((Hq, T, 1), F32)],
        scratch_shapes=[pltpu.VMEM((dh, T), F32), pltpu.VMEM((dh, T), F32)],
        compiler_params=_cp("arbitrary", "arbitrary"),
    )(sinks, qkv, qkv, qkv, qkv, qkv, bias, dot, ot)


def _split3(x):
    h1 = x.astype(BF16)
    r1 = x - h1.astype(F32)
    h2 = r1.astype(BF16)
    h3 = (r1 - h2.astype(F32)).astype(BF16)
    return h1, h2, h3


def _bias_expand(rel_t, onehot):
    Hq, NB = rel_t.shape
    L = onehot.shape[1]

    def body(r_ref, oh_ref, o_ref):
        h1, h2, h3 = _split3(r_ref[...])
        oh = oh_ref[...]
        o_ref[...] = _nn(h1, oh) + _nn(h2, oh) + _nn(h3, oh)

    return pl.pallas_call(
        body, name="bias_expand", grid=(1,),
        in_specs=[pl.BlockSpec((Hq, NB), lambda i: (0, 0)), pl.BlockSpec((NB, L), lambda i: (0, 0))],
        out_specs=pl.BlockSpec((Hq, L), lambda i: (0, 0)),
        out_shape=jax.ShapeDtypeStruct((Hq, L), F32),
        compiler_params=_cp("arbitrary"),
    )(rel_t, onehot)


def _bias_reduce(dbias, onehot):
    Hq, L = dbias.shape
    NB = onehot.shape[0]

    def body(d_ref, oh_ref, o_ref):
        h1, h2, h3 = _split3(d_ref[...])
        oh = oh_ref[...]
        o_ref[...] = _nt(h1, oh) + _nt(h2, oh) + _nt(h3, oh)

    return pl.pallas_call(
        body, name="bias_reduce", grid=(1,),
        in_specs=[pl.BlockSpec((Hq, L), lambda i: (0, 0)), pl.BlockSpec((NB, L), lambda i: (0, 0))],
        out_specs=pl.BlockSpec((Hq, NB), lambda i: (0, 0)),
        out_shape=jax.ShapeDtypeStruct((Hq, NB), F32),
        compiler_params=_cp("arbitrary"),
    )(dbias, onehot)


def _adamw(w, g, m, v, name):
    R, C = w.shape
    tr = 256 if R % 256 == 0 else R
    bc1 = 1.0 - ADAM_B1 ** ADAM_STEP
    bc2 = 1.0 - ADAM_B2 ** ADAM_STEP

    def body(w_ref, g_ref, m_ref, v_ref, d_ref, nm_ref, nv_ref):
        g = g_ref[...]
        m2 = ADAM_B1 * m_ref[...] + (1.0 - ADAM_B1) * g
        v2 = ADAM_B2 * v_ref[...] + (1.0 - ADAM_B2) * (g * g)
        nm_ref[...] = m2
        nv_ref[...] = v2
        d_ref[...] = -ADAM_LR * ((m2 / bc1) / (jnp.sqrt(v2 / bc2) + ADAM_EPS) + ADAM_WD * w_ref[...])

    spec = pl.BlockSpec((tr, C), lambda i: (i, 0))
    return pl.pallas_call(
        body, name=name, grid=(R // tr,),
        in_specs=[spec] * 4, out_specs=[spec] * 3,
        out_shape=[jax.ShapeDtypeStruct((R, C), F32)] * 3,
        compiler_params=_cp("arbitrary"),
    )(w, g, m, v)


def _row_tile(R):
    return max(t for t in range(16, 513, 16) if R % t == 0)


def _add_halves(mine, recv, name):
    K, R, C = mine.shape
    tr = _row_tile(R)

    def body(a_ref, b_ref, o_ref, ob_ref):
        s = a_ref[...].astype(F32) + b_ref[...].astype(F32)
        o_ref[...] = s
        ob_ref[...] = s.astype(BF16)

    spec = pl.BlockSpec((None, tr, C), lambda k, i: (k, i, 0))
    return pl.pallas_call(
        body, name=name, grid=(K, R // tr),
        in_specs=[spec, spec], out_specs=[spec, spec],
        out_shape=[jax.ShapeDtypeStruct((K, R, C), F32), jax.ShapeDtypeStruct((K, R, C), BF16)],
        compiler_params=_cp("arbitrary", "arbitrary"),
    )(mine, recv)


def _add_received(own, recv, name):
    R, C = own.shape
    tr = _row_tile(R)

    def body(a_ref, r_ref, o_ref):
        o_ref[...] = ((a_ref[...] + r_ref[0].astype(F32)) + r_ref[1].astype(F32)) + r_ref[2].astype(F32)

    return pl.pallas_call(
        body, name=name, grid=(R // tr,),
        in_specs=[pl.BlockSpec((tr, C), lambda i: (i, 0)), pl.BlockSpec((3, tr, C), lambda i: (0, i, 0))],
        out_specs=pl.BlockSpec((tr, C), lambda i: (i, 0)),
        out_shape=jax.ShapeDtypeStruct((R, C), F32),
        compiler_params=_cp("arbitrary"),
    )(own, recv)


def _position():
    x, y, c = lax.axis_index("x"), lax.axis_index("y"), lax.axis_index("c")
    others = [(1 - x, y), (x, 1 - y), (1 - x, 1 - y)]
    return x, y, c, others


def _remote(src, dst, send_sems, recv_sems, k, dev):
    return pltpu.make_async_remote_copy(src_ref=src, dst_ref=dst, send_sem=send_sems.at[k],
                                        recv_sem=recv_sems.at[k], device_id=dev, device_id_type=MESH_ID)


class _gather_exchange:
    def __init__(self, src, out, send_sems, recv_sems):
        x, y, c, others = _position()
        mine, sibling = 2 * x + y, (x, y, 1 - c)
        self.sends, self.arrivals, self.passes, self.from_sibling = [], [], [], []
        for j, (ox, oy) in enumerate(others):
            slot = out.at[2 * ox + oy, c]
            theirs = out.at[2 * ox + oy, 1 - c]
            self.sends.append(_remote(src.at[c], out.at[mine, c], send_sems, recv_sems, j, (ox, oy, c)))
            self.arrivals.append(_remote(slot, slot, send_sems, recv_sems, j, (ox, oy, c)))
            self.passes.append(_remote(slot, slot, send_sems, recv_sems, 3 + j, sibling))
            self.from_sibling.append(_remote(theirs, theirs, send_sems, recv_sems, 3 + j, sibling))

    def start(self):
        for cp in self.sends:
            cp.start()

    def forward(self):
        for arrived, onward in zip(self.arrivals, self.passes):
            arrived.wait_recv()
            onward.start()

    def finish(self):
        for cp in self.from_sibling:
            cp.wait_recv()
        for cp in self.sends + self.passes:
            cp.wait_send()


def _gather_weights(shard):
    R, C = shard.shape
    half = R // 2

    def body(src, out, send_sems, recv_sems):
        ex = _gather_exchange(src, out, send_sems, recv_sems)
        ex.start()
        ex.forward()
        ex.finish()

    return pl.pallas_call(
        body, name="gather_weights",
        in_specs=[ANY], out_specs=ANY,
        out_shape=jax.ShapeDtypeStruct((N_CHIPS, 2, half, C), shard.dtype),
        scratch_shapes=[pltpu.SemaphoreType.DMA((6,)), pltpu.SemaphoreType.DMA((6,))],
    )(shard.reshape(2, half, C)).reshape(N_CHIPS, R, C)


def _swap_halves(grads, name):
    K, R, C = grads.shape
    half = R // 2

    def body(src, out, send_sems, recv_sems):
        x, y, c, _ = _position()
        theirs = src.at[:, pl.ds(pl.multiple_of((1 - c) * half, 16), half), :]
        cp = _remote(theirs, out, send_sems, recv_sems, 0, (x, y, 1 - c))
        cp.start()
        cp.wait()

    return pl.pallas_call(
        body, name=name,
        in_specs=[ANY], out_specs=ANY,
        out_shape=jax.ShapeDtypeStruct((K, half, C), grads.dtype),
        scratch_shapes=[pltpu.SemaphoreType.DMA((1,)), pltpu.SemaphoreType.DMA((1,))],
    )(grads)


class _scatter_exchange:
    def __init__(self, src, out, send_sems, recv_sems):
        x, y, c, others = _position()
        self.copies = [_remote(src.at[2 * ox + oy], out.at[j], send_sems, recv_sems, j, (ox, oy, c))
                       for j, (ox, oy) in enumerate(others)]

    def start(self):
        for cp in self.copies:
            cp.start()

    def finish(self):
        for cp in self.copies:
            cp.wait()


def _scatter_to_owners(parts, name):
    K, H, C = parts.shape

    def body(src, out, send_sems, recv_sems):
        ex = _scatter_exchange(src, out, send_sems, recv_sems)
        ex.start()
        ex.finish()

    return pl.pallas_call(
        body, name=name,
        in_specs=[ANY], out_specs=ANY,
        out_shape=jax.ShapeDtypeStruct((3, H, C), parts.dtype),
        scratch_shapes=[pltpu.SemaphoreType.DMA((3,)), pltpu.SemaphoreType.DMA((3,))],
    )(parts)


def _swap_reduced(half_rows, name):
    H, C = half_rows.shape

    def body(src, out, send_sems, recv_sems):
        x, y, c, _ = _position()
        cp = _remote(src, out, send_sems, recv_sems, 0, (x, y, 1 - c))
        cp.start()
        cp.wait()

    return pl.pallas_call(
        body, name=name,
        in_specs=[ANY], out_specs=ANY,
        out_shape=jax.ShapeDtypeStruct((H, C), half_rows.dtype),
        scratch_shapes=[pltpu.SemaphoreType.DMA((1,)), pltpu.SemaphoreType.DMA((1,))],
    )(half_rows)


def _allreduce_small(block):
    R, C = block.shape
    n_dev = 8

    def body(src, out, slots, send_sems, recv_sems):
        x, y, c, _ = _position()
        me = 4 * x + 2 * y + c
        slots[me] = src[...]
        sends = []
        for r in range(1, n_dev):
            peer = (x ^ (r >> 2), y ^ ((r >> 1) & 1), c ^ (r & 1))
            cp = _remote(src, slots.at[me], send_sems, recv_sems, r - 1, peer)
            cp.start()
            sends.append(cp)
        for r in range(1, n_dev):
            theirs = slots.at[me ^ r]
            _remote(theirs, theirs, send_sems, recv_sems, r - 1, (x, y, c)).wait_recv()
        for cp in sends:
            cp.wait_send()
        acc = slots[0]
        for d in range(1, n_dev):
            acc = acc + slots[d]
        out[...] = acc

    return pl.pallas_call(
        body, name="allreduce_small",
        in_specs=[pl.BlockSpec(memory_space=pltpu.VMEM)], out_specs=pl.BlockSpec(memory_space=pltpu.VMEM),
        out_shape=jax.ShapeDtypeStruct((R, C), F32),
        scratch_shapes=[pltpu.VMEM((n_dev, R, C), F32), pltpu.SemaphoreType.DMA((7,)), pltpu.SemaphoreType.DMA((7,))],
    )(block)


def _rel_bucket(dist):
    max_exact = REL_BUCKETS // 2
    d = jnp.maximum(dist, 1).astype(F32)
    large = max_exact + (jnp.log(d / max_exact) / math.log(REL_MAX_DIST / max_exact)
                         * (REL_BUCKETS - max_exact)).astype(jnp.int32)
    large = jnp.minimum(large, REL_BUCKETS - 1)
    return jnp.where(dist < max_exact, dist, large)


def _bucket_onehot():
    T = SWA_BLOCK
    dist = (jnp.arange(T)[:, None] + T) - jnp.arange(2 * T)[None, :]
    bucket = _rel_bucket(jnp.maximum(dist, 0)).reshape(1, T * 2 * T)
    return (bucket == jnp.arange(REL_BUCKETS)[:, None]).astype(BF16)


_BUF = (("ffn1_w1", "t"), ("ffn1_w3", "t"), ("ffn1_w2", "n"), ("ffn2_w1", "t"), ("ffn2_w3", "t"),
        ("ffn2_w2", "n"), ("w_in", "t"), ("w_out", "n"), ("w_branch_swa", "tw"), ("w_branch_sb", "tw"))


def _to_rows(name_kind, w, D):
    kind = name_kind[1]
    if kind == "n":
        return w
    if kind == "t":
        return w.T
    return w.T.reshape(-1, D)


def _from_rows(name_kind, rows, width):
    kind = name_kind[1]
    if kind == "n":
        return rows
    if kind == "t":
        return rows.T
    return rows.reshape(-1, width).T


def kernel(x, norm_ffn1, ffn1_w1, ffn1_w3, ffn1_w2, norm_mix, w_in, swa_sinks, rel_bias, w_branch_swa, w_branch_sb, w_out, norm_ffn2, ffn2_w1, ffn2_w3, ffn2_w2, norm_final, loss_target, m_norm_ffn1, m_ffn1_w1, m_ffn1_w3, m_ffn1_w2, m_norm_mix, m_w_in, m_swa_sinks, m_rel_bias, m_w_branch_swa, m_w_branch_sb, m_w_out, m_norm_ffn2, m_ffn2_w1, m_ffn2_w3, m_ffn2_w2, m_norm_final, v_norm_ffn1, v_ffn1_w1, v_ffn1_w3, v_ffn1_w2, v_norm_mix, v_w_in, v_swa_sinks, v_rel_bias, v_w_branch_swa, v_w_branch_sb, v_w_out, v_norm_ffn2, v_ffn2_w1, v_ffn2_w3, v_ffn2_w2, v_norm_final):
    names = ["norm_ffn1", "ffn1_w1", "ffn1_w3", "ffn1_w2", "norm_mix", "w_in", "swa_sinks", "rel_bias",
             "w_branch_swa", "w_branch_sb", "w_out", "norm_ffn2", "ffn2_w1", "ffn2_w3", "ffn2_w2", "norm_final"]
    W = dict(zip(names, [norm_ffn1, ffn1_w1, ffn1_w3, ffn1_w2, norm_mix, w_in, swa_sinks, rel_bias,
                         w_branch_swa, w_branch_sb, w_out, norm_ffn2, ffn2_w1, ffn2_w3, ffn2_w2, norm_final]))
    M = dict(zip(names, [m_norm_ffn1, m_ffn1_w1, m_ffn1_w3, m_ffn1_w2, m_norm_mix, m_w_in, m_swa_sinks, m_rel_bias,
                         m_w_branch_swa, m_w_branch_sb, m_w_out, m_norm_ffn2, m_ffn2_w1, m_ffn2_w3, m_ffn2_w2,
                         m_norm_final]))
    V = dict(zip(names, [v_norm_ffn1, v_ffn1_w1, v_ffn1_w3, v_ffn1_w2, v_norm_mix, v_w_in, v_swa_sinks, v_rel_bias,
                         v_w_branch_swa, v_w_branch_sb, v_w_out, v_norm_ffn2, v_ffn2_w1, v_ffn2_w3, v_ffn2_w2,
                         v_norm_final]))
    xs = x[0]
    target = loss_target[0]
    S, D = xs.shape
    QW = SWA_Q_HEADS * HEAD_DIM
    KW = SWA_KV_HEADS * HEAD_DIM
    BW = SB_HEADS * HEAD_DIM
    QKV = QW + 2 * KW + 3 * BW

    pieces = [_to_rows(nk, W[nk[0]][0], D) for nk in _BUF]
    sizes = [p.shape[0] for p in pieces]
    offs = [0]
    for s in sizes:
        offs.append(offs[-1] + s)
    n_first = 3
    first_rows = offs[n_first]
    shard_a = jnp.concatenate(pieces[:n_first], axis=0).astype(BF16)
    shard_b = jnp.concatenate(pieces[n_first:], axis=0).astype(BF16)
    chip = 2 * lax.axis_index("x") + lax.axis_index("y")
    gathered_a = lax.dynamic_update_slice(_gather_weights(shard_a), shard_a[None], (chip, 0, 0))
    f1w1, f1w3, f1w2 = [gathered_a[:, offs[i]:offs[i + 1], :].reshape(N_CHIPS * sizes[i], D) for i in range(n_first)]

    g1, gmix, g3 = W["norm_ffn1"], W["norm_mix"], W["norm_ffn2"]
    gf = W["norm_final"].reshape(1, D)

    x1, h1, a1, b1, gathered_b = _ffn_fwd(xs, g1, f1w1, f1w3, f1w2, "ffn1_fwd", gather=shard_b)
    gathered_b = lax.dynamic_update_slice(gathered_b, shard_b[None], (chip, 0, 0))

    def full(i):
        return gathered_b[:, offs[i] - first_rows:offs[i + 1] - first_rows, :].reshape(N_CHIPS * sizes[i], D)

    f2w1, f2w3, f2w2, w_in_t, w_out_f = [full(i) for i in range(n_first, 8)]
    wa_t = full(8).reshape(D, QW)
    wb_t = full(9).reshape(D, BW)
    o0 = QW + 2 * KW
    rows = jnp.arange(w_in_t.shape[0])
    is_q = (rows < QW) | ((rows >= o0) & (rows < o0 + BW))
    w_in_s = w_in_t * jnp.where(is_q, QK_SCALE, 1.0).astype(BF16)[:, None]
    qkv_a = _norm_proj_heads(x1, gmix, w_in_s[:o0], SWA_BLOCK, "proj_swa")
    qkv_b = _norm_proj_heads(x1, gmix, w_in_s[o0:QKV], SB_BLOCK, "proj_sb")
    gates, h2 = _norm_matmul_nt(x1, gmix, w_in_t[QKV:], F32, "proj_gates")

    onehot = _bucket_onehot()
    bias = _bias_expand(W["rel_bias"].T, onehot).reshape(SWA_Q_HEADS, SWA_BLOCK, 2 * SWA_BLOCK)
    sinks = W["swa_sinks"].reshape(SWA_Q_HEADS)
    oa_t = _swa_fwd(qkv_a, bias, sinks)
    ob_t, saved_sb = _sb_fwd(qkv_b)

    x2, merged, ba, bb = _merge_fwd(x1, gates, oa_t, ob_t, wa_t, wb_t, w_out_f)
    x3, h3, a2, b2 = _ffn_fwd(x2, g3, f2w1, f2w3, f2w2, "ffn2_fwd")
    loss_part, dx3, dgf = _final_loss(x3, gf, target)

    dx2, dg3, dz2, da2, db2, u2 = _ffn_bwd(dx3, x2, g3, a2, b2, f2w1, f2w3, f2w2, "ffn2_bwd")
    grads = {}
    grads["ffn2_w1"] = _tn_matmul(da2, h3, "ffn2_dw1")
    grads["ffn2_w3"] = _tn_matmul(db2, h3, "ffn2_dw3")
    grads["ffn2_w2"] = _tn_matmul(u2, dz2, "ffn2_dw2")

    dx2b, dba, dbb, dgates, doa_t, dob_t = _merge_bwd(dx2, gates, ba, bb, wa_t.T, wb_t.T, w_out_f,
                                                      SWA_BLOCK, SB_BLOCK)
    grads["w_out"] = _tn_matmul(merged, dx2b, "dw_out")
    grads["w_branch_swa"] = _heads_matmul(oa_t, dba, 1.0, "dw_branch_swa").T
    grads["w_branch_sb"] = _heads_matmul(ob_t, dbb, 1.0, "dw_branch_sb").T

    dqb_t, dkb_t, dvb_t = _sb_bwd(qkv_b, dob_t, saved_sb)
    dqa_t, dka_t, dva_t, dbias, dsink_rows = _swa_bwd(qkv_a, bias, sinks, doa_t, oa_t)
    d_rel = _bias_reduce(dbias.reshape(SWA_Q_HEADS, -1), onehot).T
    d_sinks = jnp.sum(dsink_rows, axis=(1, 2))

    dheads = [(dqa_t, QK_SCALE, "q_swa"), (dka_t, 1.0, "k_swa"), (dva_t, 1.0, "v_swa"),
              (dqb_t, QK_SCALE, "q_sb"), (dkb_t, 1.0, "k_sb"), (dvb_t, 1.0, "v_sb")]
    grads["w_in"] = jnp.concatenate([_heads_matmul(a, h2, sc, "dw_in_" + nm).astype(BF16) for a, sc, nm in dheads]
                                    + [_tn_matmul(dgates, h2, "dw_in_gates")], axis=0)
    row0, pieces_in = 0, []
    for a, _, _ in dheads:
        pieces_in.append((a, row0))
        row0 += a.shape[0] * HEAD_DIM
    dx1, dgmix = _proj_bwd(pieces_in, dgates, w_in_s, x1, gmix, dx2)

    c = lax.axis_index("c")

    def reduce_start(lo, hi, tag):
        gbuf = jnp.concatenate([grads[_BUF[i][0]].astype(BF16).reshape(N_CHIPS, sizes[i], D) for i in range(lo, hi)],
                               axis=1)
        half = gbuf.shape[1] // 2
        from_sibling = _swap_halves(gbuf, "swap_halves_" + tag)
        my_half = lax.dynamic_slice_in_dim(gbuf, c * half, half, axis=1)
        return _add_halves(my_half, from_sibling, "add_sibling_" + tag)

    def reduce_finish(chip_sum, received, tag):
        own = lax.dynamic_index_in_dim(chip_sum, chip, axis=0, keepdims=False)
        my_rows = _add_received(own, received, "add_chips_" + tag)
        their_rows = _swap_reduced(my_rows, "swap_reduced_" + tag)
        return jnp.concatenate([jnp.where(c == 0, my_rows, their_rows), jnp.where(c == 0, their_rows, my_rows)],
                               axis=0)

    sum_b, sum16_b = reduce_start(n_first, len(_BUF), "late")
    dx0, dg1, dz1, da1, db1, u1, received_b = _ffn_bwd(dx1, xs, g1, a1, b1, f1w1, f1w3, f1w2, "ffn1_bwd",
                                                       scatter=sum16_b)
    grads["ffn1_w1"] = _tn_matmul(da1, h1, "ffn1_dw1")
    grads["ffn1_w3"] = _tn_matmul(db1, h1, "ffn1_dw3")
    grads["ffn1_w2"] = _tn_matmul(u1, dz1, "ffn1_dw2")
    sum_a, sum16_a = reduce_start(0, n_first, "first")
    reduced = jnp.concatenate([reduce_finish(sum_a, _scatter_to_owners(sum16_a, "scatter_to_owners"), "first"),
                               reduce_finish(sum_b, received_b, "late")], axis=0)

    small_rows = [dg1, dgmix, dg3, dgf,
                  jnp.pad(d_sinks.reshape(1, -1), ((0, 0), (0, D - SWA_Q_HEADS))),
                  jnp.pad(d_rel.reshape(1, -1), ((0, 0), (0, D - REL_BUCKETS * SWA_Q_HEADS))),
                  jnp.pad(loss_part, ((0, 0), (0, D - 1))), jnp.zeros((1, D), F32)]
    small = _allreduce_small(jnp.concatenate(small_rows, axis=0))
    loss = small[6, 0]

    G = {}
    for i, nk in enumerate(_BUF):
        G[nk[0]] = _from_rows(nk, reduced[offs[i]:offs[i + 1]], W[nk[0]].shape[1])[None]
    G["norm_ffn1"], G["norm_mix"], G["norm_ffn2"] = small[0:1], small[1:2], small[2:3]
    G["norm_final"] = small[3]
    G["swa_sinks"] = small[4:5, :SWA_Q_HEADS]
    G["rel_bias"] = small[5, :REL_BUCKETS * SWA_Q_HEADS].reshape(REL_BUCKETS, SWA_Q_HEADS)

    delta, new_m, new_v = {}, {}, {}
    small_names = ["norm_ffn1", "norm_mix", "norm_ffn2", "norm_final", "swa_sinks", "rel_bias"]

    def pack(d):
        return jnp.concatenate([jnp.pad(d[n].reshape(1, -1), ((0, 0), (0, D - d[n].size))) for n in small_names]
                               + [jnp.zeros((2, D), F32)], axis=0)

    sd, sm, sv = _adamw(pack(W), pack(G), pack(M), pack(V), "adamw_small")
    for r, n in enumerate(small_names):
        for dst, src in ((delta, sd), (new_m, sm), (new_v, sv)):
            dst[n] = src[r, :W[n].size].reshape(W[n].shape)
    for nk in _BUF:
        n = nk[0]
        shp = W[n].shape
        two_d = (shp[1], shp[2])
        d_, m_, v_ = _adamw(W[n].reshape(two_d), G[n].reshape(two_d), M[n].reshape(two_d), V[n].reshape(two_d),
                            "adamw_" + n)
        delta[n], new_m[n], new_v[n] = d_.reshape(shp), m_.reshape(shp), v_.reshape(shp)

    return (loss, dx0[None], *[G[n] for n in names], *[delta[n] for n in names],
            *[new_m[n] for n in names], *[new_v[n] for n in names])
```

```python
import functools
import math

import jax
import jax.numpy as jnp
from jax import lax
from jax.experimental import pallas as pl
from jax.experimental.pallas import tpu as pltpu

F32, BF16 = jnp.float32, jnp.bfloat16
MESH_ID = pl.DeviceIdType.MESH
ANY = pl.BlockSpec(memory_space=pl.ANY)

RMS_EPS = 1e-6
HEAD_DIM = 64
SWA_Q_HEADS, SWA_KV_HEADS, SWA_GROUP = 8, 2, 4
SWA_BLOCK = 128
SB_HEADS = 8
SB_BLOCK = 256
REL_BUCKETS, REL_MAX_DIST = 32, 128
NEG_BIG = -1e30
QK_SCALE = HEAD_DIM ** -0.5
ADAM_LR, ADAM_B1, ADAM_B2, ADAM_EPS, ADAM_WD, ADAM_STEP = 0.001, 0.9, 0.999, 1e-08, 0.01, 10

N_CHIPS = 4
TOKEN_TILE = 512
MATMUL_TOKEN_TILE = 1024
WGRAD_ROW_TILES = (2176, 1408, 1024, 256)
FF_TILE = 1408
FFN_TOKEN_TILE = 512
FF_BWD_TILE = 256
VMEM_LIMIT = 56 * 1024 * 1024


def _cp(*sem):
    return pltpu.CompilerParams(dimension_semantics=sem, vmem_limit_bytes=VMEM_LIMIT)


def _nn(a, b):
    return jnp.dot(a, b, preferred_element_type=F32)


def _nt(a, b):
    return lax.dot_general(a, b, (((1,), (1,)), ((), ())), preferred_element_type=F32)


def _tn(a, b):
    return lax.dot_general(a, b, (((0,), (0,)), ((), ())), preferred_element_type=F32)


def _norm_fwd(x, g):
    return x * lax.rsqrt(jnp.mean(x * x, axis=-1, keepdims=True) + RMS_EPS) * g


def _norm_bwd(x, g, dh):
    r = lax.rsqrt(jnp.mean(x * x, axis=-1, keepdims=True) + RMS_EPS)
    xh = x * r
    dxh = dh * g
    dx = r * (dxh - xh * jnp.mean(dxh * xh, axis=-1, keepdims=True))
    return dx, jnp.sum(dh * xh, axis=0, keepdims=True)


SOFTPLUS_LINEAR = 20.0


def _softplus(z):
    return jnp.maximum(jnp.log(1.0 + jnp.exp(jnp.minimum(z, SOFTPLUS_LINEAR))), z)


def _ffn_fwd(x, g, w1t, w3t, w2, name, gather=None):
    S, D = x.shape
    F = w2.shape[0]
    tm, tf = min(FFN_TOKEN_TILE, S), FF_TILE
    ni, nj = S // tm, F // tf

    def body(x_ref, g_ref, w1_ref, w3_ref, w2_ref, *rest):
        if gather is None:
            xo_ref, h_ref, a_ref, b_ref, hs, acc = rest
        else:
            shard_ref, xo_ref, h_ref, a_ref, b_ref, gathered_ref, hs, acc, send_sems, recv_sems = rest
        i, j = pl.program_id(0), pl.program_id(1)
        if gather is not None:
            for when, phase in ((jnp.logical_and(i == 0, j == 0), "start"),
                                (jnp.logical_and(i == ni - 1, j == 0), "forward"),
                                (jnp.logical_and(i == ni - 1, j == nj - 1), "finish")):
                @pl.when(when)
                def _():
                    getattr(_gather_exchange(shard_ref, gathered_ref, send_sems, recv_sems), phase)()

        @pl.when(j == 0)
        def _():
            hb = _norm_fwd(x_ref[...], g_ref[...]).astype(BF16)
            hs[...] = hb
            h_ref[...] = hb
            acc[...] = jnp.zeros_like(acc)

        h = hs[...]
        a = _nt(h, w1_ref[...])
        b = _nt(h, w3_ref[...])
        a_ref[...] = a.astype(BF16)
        b_ref[...] = b.astype(BF16)
        u = a * jax.nn.sigmoid(a) * b
        acc[...] += _nn(u.astype(BF16), w2_ref[...])

        @pl.when(j == nj - 1)
        def _():
            xo_ref[...] = x_ref[...] + 0.5 * acc[...]

    in_specs = [pl.BlockSpec((tm, D), lambda i, j: (i, 0)),
                pl.BlockSpec((1, D), lambda i, j: (0, 0)),
                pl.BlockSpec((tf, D), lambda i, j: (j, 0)),
                pl.BlockSpec((tf, D), lambda i, j: (j, 0)),
                pl.BlockSpec((tf, D), lambda i, j: (j, 0))]
    out_specs = [pl.BlockSpec((tm, D), lambda i, j: (i, 0)),
                 pl.BlockSpec((tm, D), lambda i, j: (i, 0)),
                 pl.BlockSpec((tm, tf), lambda i, j: (i, j)),
                 pl.BlockSpec((tm, tf), lambda i, j: (i, j))]
    out_shape = [jax.ShapeDtypeStruct((S, D), F32), jax.ShapeDtypeStruct((S, D), BF16),
                 jax.ShapeDtypeStruct((S, F), BF16), jax.ShapeDtypeStruct((S, F), BF16)]
    scratch = [pltpu.VMEM((tm, D), BF16), pltpu.VMEM((tm, D), F32)]
    operands = [x, g, w1t, w3t, w2]
    if gather is not None:
        R, C = gather.shape
        in_specs.append(ANY)
        out_specs.append(ANY)
        out_shape.append(jax.ShapeDtypeStruct((N_CHIPS, 2, R // 2, C), gather.dtype))
        scratch += [pltpu.SemaphoreType.DMA((6,)), pltpu.SemaphoreType.DMA((6,))]
        operands.append(gather.reshape(2, R // 2, C))
    outs = list(pl.pallas_call(
        body, name=name, grid=(ni, nj), in_specs=in_specs, out_specs=out_specs, out_shape=out_shape,
        scratch_shapes=scratch, compiler_params=_cp("arbitrary", "arbitrary"),
    )(*operands))
    if gather is not None:
        outs[4] = outs[4].reshape(N_CHIPS, R, C)
    return outs


def _ffn_bwd(dxo, x, g, a, b, w1t, w3t, w2, name, scatter=None):
    S, D = x.shape
    F = w2.shape[0]
    tm, tf = min(MATMUL_TOKEN_TILE, S), FF_BWD_TILE
    ni, nj = S // tm, F // tf

    def body(dxo_ref, x_ref, g_ref, a_ref, b_ref, w1_ref, w3_ref, w2_ref, *rest):
        if scatter is None:
            dx_ref, dg_ref, dz_ref, da_ref, db_ref, u_ref, dzs, acc = rest
        else:
            (parts_ref, dx_ref, dg_ref, dz_ref, da_ref, db_ref, u_ref, recv_ref,
             dzs, acc, send_sems, recv_sems) = rest
        i, j = pl.program_id(0), pl.program_id(1)
        if scatter is not None:
            for when, phase in ((jnp.logical_and(i == 0, j == 0), "start"),
                                (jnp.logical_and(i == ni - 1, j == nj - 1), "finish")):
                @pl.when(when)
                def _():
                    getattr(_scatter_exchange(parts_ref, recv_ref, send_sems, recv_sems), phase)()

        @pl.when(j == 0)
        def _():
            dzb = (0.5 * dxo_ref[...]).astype(BF16)
            dzs[...] = dzb
            dz_ref[...] = dzb
            acc[...] = jnp.zeros_like(acc)

        du = _nt(dzs[...], w2_ref[...])
        av = a_ref[...].astype(F32)
        bv = b_ref[...].astype(F32)
        s = jax.nn.sigmoid(av)
        silu = av * s
        db = (du * silu).astype(BF16)
        da = (du * bv * (s * (1.0 + av * (1.0 - s)))).astype(BF16)
        da_ref[...] = da
        db_ref[...] = db
        u_ref[...] = (silu * bv).astype(BF16)
        acc[...] += _nn(da, w1_ref[...]) + _nn(db, w3_ref[...])

        @pl.when(j == nj - 1)
        def _():
            dx, dg = _norm_bwd(x_ref[...], g_ref[...], acc[...])
            dx_ref[...] = dxo_ref[...] + dx

            @pl.when(i == 0)
            def _():
                dg_ref[...] = dg

            @pl.when(i > 0)
            def _():
                dg_ref[...] += dg

    row = pl.BlockSpec((tm, D), lambda i, j: (i, 0))
    wsp = pl.BlockSpec((tf, D), lambda i, j: (j, 0))
    col = pl.BlockSpec((tm, tf), lambda i, j: (i, j))
    vec = pl.BlockSpec((1, D), lambda i, j: (0, 0))
    in_specs = [row, row, vec, col, col, wsp, wsp, wsp]
    out_specs = [row, vec, row, col, col, col]
    out_shape = [jax.ShapeDtypeStruct((S, D), F32), jax.ShapeDtypeStruct((1, D), F32),
                 jax.ShapeDtypeStruct((S, D), BF16), jax.ShapeDtypeStruct((S, F), BF16),
                 jax.ShapeDtypeStruct((S, F), BF16), jax.ShapeDtypeStruct((S, F), BF16)]
    scratch = [pltpu.VMEM((tm, D), BF16), pltpu.VMEM((tm, D), F32)]
    operands = [dxo, x, g, a, b, w1t, w3t, w2]
    if scatter is not None:
        in_specs.append(ANY)
        out_specs.append(ANY)
        out_shape.append(jax.ShapeDtypeStruct((3,) + scatter.shape[1:], scatter.dtype))
        scratch += [pltpu.SemaphoreType.DMA((3,)), pltpu.SemaphoreType.DMA((3,))]
        operands.append(scatter)
    return pl.pallas_call(
        body, name=name, grid=(ni, nj), in_specs=in_specs, out_specs=out_specs, out_shape=out_shape,
        scratch_shapes=scratch, compiler_params=_cp("arbitrary", "arbitrary"),
    )(*operands)


def _tn_matmul(a, b, name):
    S, M = a.shape
    N = b.shape[1]
    ts = min(MATMUL_TOKEN_TILE, S)
    tmm = next(t for t in WGRAD_ROW_TILES if M % t == 0)
    ns = S // ts

    def body(a_ref, b_ref, o_ref, acc):
        s = pl.program_id(1)
        part = _tn(a_ref[...], b_ref[...])

        @pl.when(s == 0)
        def _():
            acc[...] = part

        @pl.when(s > 0)
        def _():
            acc[...] += part

        @pl.when(s == ns - 1)
        def _():
            o_ref[...] = acc[...].astype(BF16)

    return pl.pallas_call(
        body, name=name, grid=(M // tmm, ns),
        in_specs=[pl.BlockSpec((ts, tmm), lambda m, s: (s, m)),
                  pl.BlockSpec((ts, N), lambda m, s: (s, 0))],
        out_specs=pl.BlockSpec((tmm, N), lambda m, s: (m, 0)),
        out_shape=jax.ShapeDtypeStruct((M, N), BF16),
        scratch_shapes=[pltpu.VMEM((tmm, N), F32)],
        compiler_params=_cp("arbitrary", "arbitrary"),
    )(a, b)


def _norm_matmul_nt(x, g, wt, out_dtype, name):
    S, D = x.shape
    N = wt.shape[0]
    tm = min(MATMUL_TOKEN_TILE, S)
    tn = next(t for t in (1024, 768, 256) if N % t == 0)

    def body(x_ref, g_ref, w_ref, o_ref, h_ref, hs):
        @pl.when(pl.program_id(1) == 0)
        def _():
            hb = _norm_fwd(x_ref[...], g_ref[...]).astype(BF16)
            hs[...] = hb
            h_ref[...] = hb

        o_ref[...] = _nt(hs[...], w_ref[...]).astype(out_dtype)

    return pl.pallas_call(
        body, name=name, grid=(S // tm, N // tn),
        in_specs=[pl.BlockSpec((tm, D), lambda i, j: (i, 0)),
                  pl.BlockSpec((1, D), lambda i, j: (0, 0)),
                  pl.BlockSpec((tn, D), lambda i, j: (j, 0))],
        out_specs=[pl.BlockSpec((tm, tn), lambda i, j: (i, j)),
                   pl.BlockSpec((tm, D), lambda i, j: (i, 0))],
        out_shape=[jax.ShapeDtypeStruct((S, N), out_dtype), jax.ShapeDtypeStruct((S, D), BF16)],
        scratch_shapes=[pltpu.VMEM((tm, D), BF16)],
        compiler_params=_cp("arbitrary", "arbitrary"),
    )(x, g, wt)


def _heads_tile(ref):
    Hh, nbk = ref.shape[0], ref.shape[1]
    return jnp.concatenate([jnp.concatenate([ref[h, b] for b in range(nbk)], axis=1) for h in range(Hh)], axis=0)


def _store_heads(ref, val):
    Hh, nbk, dh, T = ref.shape
    for h in range(Hh):
        for b in range(nbk):
            ref[h, b] = val[h * dh:(h + 1) * dh, b * T:(b + 1) * T].astype(ref.dtype)


def _norm_proj_heads(x, g, w_rows, T, name):
    S, D = x.shape
    N = w_rows.shape[0]
    tm, tn = min(MATMUL_TOKEN_TILE, S), 768

    def body(x_ref, g_ref, w_ref, o_ref, hs):
        @pl.when(pl.program_id(1) == 0)
        def _():
            hs[...] = _norm_fwd(x_ref[...], g_ref[...]).astype(BF16)

        _store_heads(o_ref, _nt(w_ref[...], hs[...]))

    return pl.pallas_call(
        body, name=name, grid=(S // tm, N // tn),
        in_specs=[pl.BlockSpec((tm, D), lambda i, j: (i, 0)),
                  pl.BlockSpec((1, D), lambda i, j: (0, 0)),
                  pl.BlockSpec((tn, D), lambda i, j: (j, 0))],
        out_specs=pl.BlockSpec((tn // HEAD_DIM, tm // T, HEAD_DIM, T), lambda i, j: (j, i, 0, 0)),
        out_shape=jax.ShapeDtypeStruct((N // HEAD_DIM, S // T, HEAD_DIM, T), BF16),
        scratch_shapes=[pltpu.VMEM((tm, D), BF16)],
        compiler_params=_cp("arbitrary", "arbitrary"),
    )(x, g, w_rows)


def _heads_matmul(at, b, scale, name):
    Hh, nb, dh, T = at.shape
    S, N = b.shape
    ts = min(MATMUL_TOKEN_TILE, S)
    ns = S // ts

    def body(a_ref, b_ref, o_ref):
        s = pl.program_id(0)
        a = _heads_tile(a_ref)
        part = _nn((a if scale == 1.0 else a * scale).astype(BF16), b_ref[...])

        @pl.when(s == 0)
        def _():
            o_ref[...] = part

        @pl.when(s > 0)
        def _():
            o_ref[...] += part

    return pl.pallas_call(
        body, name=name, grid=(ns,),
        in_specs=[pl.BlockSpec((Hh, ts // T, dh, T), lambda s: (0, s, 0, 0)),
                  pl.BlockSpec((ts, N), lambda s: (s, 0))],
        out_specs=pl.BlockSpec((Hh * dh, N), lambda s: (0, 0)),
        out_shape=jax.ShapeDtypeStruct((Hh * dh, N), F32),
        compiler_params=_cp("arbitrary"),
    )(at, b)


def _proj_bwd(pieces, dgates, w_rows, x, g, dres):
    S, D = x.shape
    tm = min(TOKEN_TILE, S)
    n_p = len(pieces)
    gate_row = w_rows.shape[0] - dgates.shape[1]

    def body(*refs):
        p_refs = refs[:n_p]
        dgt_ref, w_ref, x_ref, g_ref, dres_ref, dx_ref, dg_ref = refs[n_p:]
        i = pl.program_id(0)
        dh = _nn(dgt_ref[...], w_ref[gate_row:, :])
        for p_ref, (arr, row0) in zip(p_refs, pieces):
            rows = arr.shape[0] * arr.shape[2]
            dh += _tn(_heads_tile(p_ref).astype(BF16), w_ref[row0:row0 + rows, :])
        dx, dg = _norm_bwd(x_ref[...], g_ref[...], dh)
        dx_ref[...] = dres_ref[...] + dx

        @pl.when(i == 0)
        def _():
            dg_ref[...] = dg

        @pl.when(i > 0)
        def _():
            dg_ref[...] += dg

    row = pl.BlockSpec((tm, D), lambda i: (i, 0))
    vec = pl.BlockSpec((1, D), lambda i: (0, 0))
    p_specs = [pl.BlockSpec((a.shape[0], tm // a.shape[3], a.shape[2], a.shape[3]), lambda i: (0, i, 0, 0))
               for a, _ in pieces]
    return pl.pallas_call(
        body, name="proj_bwd", grid=(S // tm,),
        in_specs=p_specs + [pl.BlockSpec((tm, dgates.shape[1]), lambda i: (i, 0)),
                            pl.BlockSpec(w_rows.shape, lambda i: (0, 0), pipeline_mode=pl.Buffered(1)),
                            row, vec, row],
        out_specs=[row, vec],
        out_shape=[jax.ShapeDtypeStruct((S, D), F32), jax.ShapeDtypeStruct((1, D), F32)],
        compiler_params=_cp("arbitrary"),
    )(*[a for a, _ in pieces], dgates, w_rows, x, g, dres)


def _merge_fwd(x1, gates, oa_t, ob_t, wat, wbt, w_out):
    S, D = x1.shape
    W = wat.shape[1]
    tm = min(TOKEN_TILE, S)

    def body(x_ref, ga_ref, gb_ref, oa_ref, ob_ref, wa_ref, wb_ref, wo_ref,
             x2_ref, mg_ref, ba_ref, bb_ref):
        ba = _nt(_heads_tile(oa_ref).T.astype(BF16), wa_ref[...])
        bb = _nt(_heads_tile(ob_ref).T.astype(BF16), wb_ref[...])
        merged = jax.nn.sigmoid(ga_ref[...]) * ba + jax.nn.sigmoid(gb_ref[...]) * bb
        mb = merged.astype(BF16)
        mg_ref[...] = mb
        ba_ref[...] = ba.astype(BF16)
        bb_ref[...] = bb.astype(BF16)
        x2_ref[...] = x_ref[...] + _nn(mb, wo_ref[...])

    row = pl.BlockSpec((tm, D), lambda i: (i, 0))
    full = lambda r, c: pl.BlockSpec((r, c), lambda i: (0, 0))
    heads = lambda a: pl.BlockSpec((a.shape[0], tm // a.shape[3], a.shape[2], a.shape[3]), lambda i: (0, i, 0, 0))
    return pl.pallas_call(
        body, name="merge_fwd", grid=(S // tm,),
        in_specs=[row, pl.BlockSpec((tm, D), lambda i: (i, 0)), pl.BlockSpec((tm, D), lambda i: (i, 1)),
                  heads(oa_t), heads(ob_t), full(D, W), full(D, W), full(D, D)],
        out_specs=[row, row, row, row],
        out_shape=[jax.ShapeDtypeStruct((S, D), F32)] + [jax.ShapeDtypeStruct((S, D), BF16)] * 3,
        compiler_params=_cp("arbitrary"),
    )(x1, gates, gates, oa_t, ob_t, wat, wbt, w_out)


def _merge_bwd(dx2, gates, ba, bb, wa, wb, w_out, t_a, t_b):
    S, D = dx2.shape
    W = wa.shape[0]
    tm = min(TOKEN_TILE, S)
    Hh = W // HEAD_DIM

    def body(dx_ref, ga_ref, gb_ref, ba_ref, bb_ref, wa_ref, wb_ref, wo_ref,
             dxb_ref, dba_ref, dbb_ref, dgt_ref, doa_ref, dob_ref):
        dxb = dx_ref[...].astype(BF16)
        dxb_ref[...] = dxb
        dm = _nt(dxb, wo_ref[...])
        sa = jax.nn.sigmoid(ga_ref[...])
        sb = jax.nn.sigmoid(gb_ref[...])
        dba = (dm * sa).astype(BF16)
        dbb = (dm * sb).astype(BF16)
        dba_ref[...] = dba
        dbb_ref[...] = dbb
        dgt_ref[:, :D] = (dm * ba_ref[...].astype(F32) * sa * (1.0 - sa)).astype(BF16)
        dgt_ref[:, D:] = (dm * bb_ref[...].astype(F32) * sb * (1.0 - sb)).astype(BF16)
        _store_heads(doa_ref, _nt(wa_ref[...], dba))
        _store_heads(dob_ref, _nt(wb_ref[...], dbb))

    row = pl.BlockSpec((tm, D), lambda i: (i, 0))
    full = lambda r, c: pl.BlockSpec((r, c), lambda i: (0, 0))
    heads = lambda T: pl.BlockSpec((Hh, tm // T, HEAD_DIM, T), lambda i: (0, i, 0, 0))
    return pl.pallas_call(
        body, name="merge_bwd", grid=(S // tm,),
        in_specs=[row, pl.BlockSpec((tm, D), lambda i: (i, 0)), pl.BlockSpec((tm, D), lambda i: (i, 1)),
                  row, row, full(W, D), full(W, D), full(D, D)],
        out_specs=[row, row, row, pl.BlockSpec((tm, 2 * D), lambda i: (i, 0)), heads(t_a), heads(t_b)],
        out_shape=[jax.ShapeDtypeStruct((S, D), BF16)] * 3 + [jax.ShapeDtypeStruct((S, 2 * D), BF16),
                   jax.ShapeDtypeStruct((Hh, S // t_a, HEAD_DIM, t_a), F32),
                   jax.ShapeDtypeStruct((Hh, S // t_b, HEAD_DIM, t_b), BF16)],
        compiler_params=_cp("arbitrary"),
    )(dx2, gates, gates, ba, bb, wa, wb, w_out)


def _final_loss(x3, gf, target):
    S, D = x3.shape
    tm = min(TOKEN_TILE, S)

    def body(x_ref, g_ref, t_ref, loss_ref, dx_ref, dg_ref):
        i = pl.program_id(0)
        x = x_ref[...]
        g = g_ref[...]
        e = _norm_fwd(x, g) - t_ref[...]
        part = 0.5 * jnp.sum(jnp.mean(e * e, axis=-1, keepdims=True), axis=0, keepdims=True)
        dx, dg = _norm_bwd(x, g, e * (1.0 / D))
        dx_ref[...] = dx

        @pl.when(i == 0)
        def _():
            loss_ref[...] = part
            dg_ref[...] = dg

        @pl.when(i > 0)
        def _():
            loss_ref[...] += part
            dg_ref[...] += dg

    row = pl.BlockSpec((tm, D), lambda i: (i, 0))
    vec = pl.BlockSpec((1, D), lambda i: (0, 0))
    return pl.pallas_call(
        body, name="final_loss", grid=(S // tm,),
        in_specs=[row, vec, row],
        out_specs=[pl.BlockSpec((1, 1), lambda i: (0, 0)), row, vec],
        out_shape=[jax.ShapeDtypeStruct((1, 1), F32), jax.ShapeDtypeStruct((S, D), F32),
                   jax.ShapeDtypeStruct((1, D), F32)],
        compiler_params=_cp("arbitrary"),
    )(x3, gf, target)


SB_FWD_HEAD_GROUP = 8
SB_HEAD_GROUP = 4
LANES = 128


def _tri(T, kind):
    r = lax.broadcasted_iota(jnp.int32, (T, T), 0)
    c = lax.broadcasted_iota(jnp.int32, (T, T), 1)
    return {"after": r > c, "upto": r <= c, "before": r < c}[kind].astype(BF16)


def _lane(v, j):
    return jnp.broadcast_to(v[:, j:j + 1], (v.shape[0], LANES))


def _t_bf16(x):
    return x.astype(F32).T.astype(BF16)


def _wide(v, T):
    return jnp.tile(v, (1, T // LANES))


SB_SLOTS = 3
SB_FWD_SLOTS = 2
COPY_PARTS = 4


class _split_copy:
    def __init__(self, src, dst, sems):
        n = src.shape[0] // COPY_PARTS
        self.parts = [pltpu.make_async_copy(src.at[pl.ds(r * n, n)], dst.at[pl.ds(r * n, n)], sems.at[r])
                      for r in range(COPY_PARTS)]

    def start(self):
        for cp in self.parts:
            cp.start()

    def wait(self):
        for cp in self.parts:
            cp.wait()


def _sb_pair(i, kb):
    return (i * (i + 1)) // 2 + kb


def _sb_fwd(qkv):
    H3, nb, dh, T = qkv.shape
    H = H3 // 3
    HG = SB_FWD_HEAD_GROUP
    assert HG == H, "one head group: a saved tile holds all the heads"
    n_pairs = (nb * (nb + 1)) // 2

    def body(q_ref, k_ref, v_ref, o_ref, saved_ref, stage, sems):
        row = lax.broadcasted_iota(jnp.int32, (T, T), 0)
        col = lax.broadcasted_iota(jnp.int32, (T, T), 1)
        tri = col < row
        after = _tri(T, "after")

        def save(slot, pair):
            return _split_copy(stage.at[slot], saved_ref.at[pair], sems.at[slot])

        def blocks(qs, i, kb, step, carry, diag):
            hs = range(HG)
            slot = step % SB_FWD_SLOTS

            @pl.when(step >= SB_FWD_SLOTS)
            def _():
                save(slot, 0).wait()

            z = [_nn(qs[hh], k_ref[hh, kb]) for hh in hs]
            res, ls, first = [None] * HG, [None] * HG, [None] * HG
            for hh in hs:
                sp = _softplus(z[hh])
                if diag:
                    sp = jnp.where(tri, sp, 0.0)
                ls[hh] = z[hh] - sp
                spb = sp.astype(BF16)
                first[hh] = _lane(spb.astype(F32), 0)
                res[hh] = _nn(spb, after)
            out = []
            for hh in hs:
                c, oacc = carry[2 * hh], carry[2 * hh + 1]
                a = jnp.exp(ls[hh] - (res[hh] + _wide(c, T)))
                if diag:
                    a = jnp.where(tri, a, 0.0)
                ab = a.astype(BF16)
                stage[slot, hh, 0] = ab
                stage[slot, hh, 1] = jnp.exp(ls[hh]).astype(BF16)
                out.extend([c + (first[hh] + _lane(res[hh], 0)), oacc + _nt(v_ref[hh, kb], ab)])
            save(slot, _sb_pair(i, kb)).start()
            return tuple(out)

        def qblock(i, step):
            qs = [_t_bf16(q_ref[hh, i]) for hh in range(HG)]
            carry = blocks(qs, i, i, step, (jnp.zeros((T, LANES), F32), jnp.zeros((dh, T), F32)) * HG, True)

            def kstep(t, carry):
                return blocks(qs, i, i - 1 - t, step + 1 + t, carry, False)

            carry = lax.fori_loop(0, i, kstep, carry)
            for hh in range(HG):
                o_ref[hh, i] = carry[2 * hh + 1]
            return step + 1 + i

        lax.fori_loop(0, nb, qblock, 0)
        for slot in range(min(SB_FWD_SLOTS, n_pairs)):
            save(slot, 0).wait()

    ht = lambda part: pl.BlockSpec((HG, nb, dh, T), lambda h: (part, 0, 0, 0), pipeline_mode=pl.Buffered(1))
    return pl.pallas_call(
        body, name="sb_fwd", grid=(1,),
        in_specs=[ht(0), ht(1), ht(2)],
        out_specs=[ht(0), ANY],
        out_shape=[jax.ShapeDtypeStruct((H, nb, dh, T), F32),
                   jax.ShapeDtypeStruct((n_pairs, H, 2, T, T), BF16)],
        scratch_shapes=[pltpu.VMEM((SB_FWD_SLOTS, HG, 2, T, T), BF16),
                        pltpu.SemaphoreType.DMA((SB_FWD_SLOTS, COPY_PARTS))],
        compiler_params=_cp("arbitrary"),
    )(qkv, qkv, qkv)


def _sb_bwd(qkv, dot, saved):
    H3, nb, dh, T = qkv.shape
    H = H3 // 3
    HG = SB_HEAD_GROUP
    n_pairs = (nb * (nb + 1)) // 2

    def body(qt_ref, k_ref, v_ref, dot_ref, saved_ref, dq_ref, dk_ref, dv_ref, stage, sems):
        head0 = pl.program_id(0) * HG
        row = lax.broadcasted_iota(jnp.int32, (T, T), 0)
        col = lax.broadcasted_iota(jnp.int32, (T, T), 1)
        tri = col < row
        before = _tri(T, "before")
        dk_ref[...] = jnp.zeros_like(dk_ref)
        dv_ref[...] = jnp.zeros_like(dv_ref)

        def fetch(slot, pair):
            return _split_copy(saved_ref.at[pair, pl.ds(head0, HG)], stage.at[slot], sems.at[slot])

        for ahead in range(min(SB_SLOTS - 1, n_pairs)):
            fetch(ahead, ahead).start()

        def blocks(qTs, dos, doTs, i, kb, carry, diag):
            hs = range(HG)
            pair = _sb_pair(i, kb)
            slot = pair % SB_SLOTS
            fetch(slot, pair).wait()
            nxt = pair + (SB_SLOTS - 1)

            @pl.when(nxt < n_pairs)
            def _():
                fetch(nxt % SB_SLOTS, nxt).start()

            kT = [k_ref[hh, kb] for hh in hs]
            da = [_nn(dos[hh], v_ref[hh, kb]) for hh in hs]
            g, gb, resg = [None] * HG, [None] * HG, [None] * HG
            for hh in hs:
                g[hh] = stage[slot, hh, 0].astype(F32) * da[hh]
                gb[hh] = g[hh].astype(BF16)
                resg[hh] = _nn(gb[hh], before)
            out = []
            for hh in hs:
                pre_g, dq = carry[2 * hh], carry[2 * hh + 1]
                dz = g[hh] - (g[hh] + (resg[hh] + _wide(pre_g, T))) * stage[slot, hh, 1].astype(F32)
                if diag:
                    dz = jnp.where(tri, dz, 0.0)
                dzb = dz.astype(BF16)
                dk_ref[hh, kb] += _nn(qTs[hh], dzb)
                dv_ref[hh, kb] += _nn(doTs[hh], stage[slot, hh, 0])
                out.extend([pre_g + (_lane(resg[hh], T - 1) + _lane(gb[hh].astype(F32), T - 1)),
                            dq + _nt(kT[hh], dzb)])
            return tuple(out)

        def qblock(i, _):
            qTs = [qt_ref[hh, i] for hh in range(HG)]
            doTs = [dot_ref[hh, i] for hh in range(HG)]
            dos = [_t_bf16(v) for v in doTs]
            carry = (jnp.zeros((T, LANES), F32), jnp.zeros((dh, T), F32)) * HG

            def kstep(kb, carry):
                return blocks(qTs, dos, doTs, i, kb, carry, False)

            carry = lax.fori_loop(0, i, kstep, carry)
            carry = blocks(qTs, dos, doTs, i, i, carry, True)
            for hh in range(HG):
                dq_ref[hh, i] = carry[2 * hh + 1]
            return 0

        lax.fori_loop(0, nb, qblock, 0)

    G = H // HG
    ht = lambda part: pl.BlockSpec((HG, nb, dh, T), lambda h: (h + part * G, 0, 0, 0),
                                   pipeline_mode=pl.Buffered(1))
    return pl.pallas_call(
        body, name="sb_bwd", grid=(G,),
        in_specs=[ht(0), ht(1), ht(2), ht(0), ANY],
        out_specs=[ht(0), ht(0), ht(0)],
        out_shape=[jax.ShapeDtypeStruct((H, nb, dh, T), F32)] * 3,
        scratch_shapes=[pltpu.VMEM((SB_SLOTS, HG, 2, T, T), BF16), pltpu.SemaphoreType.DMA((SB_SLOTS, COPY_PARTS))],
        compiler_params=_cp("arbitrary"),
    )(qkv, qkv, qkv, dot, saved)


def _swa_probs(zp, zc, bias, sink, first):
    T = zp.shape[0]
    row = lax.broadcasted_iota(jnp.int32, (T, T), 0)
    col = lax.broadcasted_iota(jnp.int32, (T, T), 1)
    lp = jnp.where(jnp.logical_and(col > row, jnp.logical_not(first)), zp + bias[:, :T], NEG_BIG)
    lc = jnp.where(col <= row, zc + bias[:, T:], NEG_BIG)
    m = jnp.maximum(jnp.maximum(jnp.max(lp, axis=1, keepdims=True), jnp.max(lc, axis=1, keepdims=True)), sink)
    pp = jnp.exp(lp - m)
    pc = jnp.exp(lc - m)
    ps = jnp.exp(sink - m)
    inv = 1.0 / (jnp.sum(pp, axis=1, keepdims=True) + jnp.sum(pc, axis=1, keepdims=True) + ps)
    return pp * inv, pc * inv, ps * inv


def _swa_specs(nb, dh, T, grp, Hq, Hkv, clamp):
    blk = (lambda n: jnp.minimum(n, nb - 1)) if clamp else (lambda n: n)
    q = pl.BlockSpec((grp, None, dh, T), lambda h, n: (h, blk(n), 0, 0))
    one = lambda first, back: pl.BlockSpec(
        (None, None, dh, T), lambda h, n: (first + h, jnp.maximum(blk(n) - back, 0) if back else blk(n), 0, 0))
    return q, [one(Hq, 1), one(Hq, 0), one(Hq + Hkv, 1), one(Hq + Hkv, 0)]


def _swa_fwd(qkv, bias, sinks):
    Hq, Hkv, grp = SWA_Q_HEADS, SWA_KV_HEADS, SWA_GROUP
    _, nb, dh, T = qkv.shape

    def body(sink_ref, q_ref, kp_ref, kc_ref, vp_ref, vc_ref, bias_ref, o_ref):
        hk, n = pl.program_id(0), pl.program_id(1)
        kp, kc, vp, vc = kp_ref[...], kc_ref[...], vp_ref[...], vc_ref[...]
        qs = [_t_bf16(q_ref[g]) for g in range(grp)]
        zs = [(_nn(q, kp), _nn(q, kc)) for q in qs]
        for g in range(grp):
            pp, pc, _ = _swa_probs(*zs[g], bias_ref[g], sink_ref[hk * grp + g], n == 0)
            o_ref[g] = _nt(vp, pp.astype(BF16)) + _nt(vc, pc.astype(BF16))

    q_spec, kv_specs = _swa_specs(nb, dh, T, grp, Hq, Hkv, False)
    return pl.pallas_call(
        body, name="swa_fwd", grid=(Hkv, nb),
        in_specs=[pl.BlockSpec(memory_space=pltpu.SMEM), q_spec] + kv_specs
                 + [pl.BlockSpec((grp, T, 2 * T), lambda h, n: (h, 0, 0))],
        out_specs=pl.BlockSpec((grp, None, dh, T), lambda h, n: (h, n, 0, 0)),
        out_shape=jax.ShapeDtypeStruct((Hq, nb, dh, T), F32),
        compiler_params=_cp("arbitrary", "arbitrary"),
    )(sinks, qkv, qkv, qkv, qkv, qkv, bias)


def _swa_bwd(qkv, bias, sinks, dot, ot):
    Hq, Hkv, grp = SWA_Q_HEADS, SWA_KV_HEADS, SWA_GROUP
    _, nb, dh, T = qkv.shape

    def body(sink_ref, qt_ref, kp_ref, kc_ref, vp_ref, vc_ref, bias_ref, dot_ref, ot_ref,
             dq_ref, dk_ref, dv_ref, dbias_ref, dsink_ref, ck, cv):
        hk, n = pl.program_id(0), pl.program_id(1)

        @pl.when(n == 0)
        def _():
            dbias_ref[...] = jnp.zeros_like(dbias_ref)
            dsink_ref[...] = jnp.zeros_like(dsink_ref)
            ck[...] = jnp.zeros_like(ck)
            cv[...] = jnp.zeros_like(cv)

        @pl.when(n < nb)
        def _():
            kp, kc, vp, vc = kp_ref[...], kc_ref[...], vp_ref[...], vc_ref[...]
            kprev = jnp.zeros((dh, T), F32)
            vprev = jnp.zeros((dh, T), F32)
            kcur = jnp.zeros((dh, T), F32)
            vcur = jnp.zeros((dh, T), F32)
            qTs = [qt_ref[g] for g in range(grp)]
            qs = [_t_bf16(v) for v in qTs]
            dos = [dot_ref[g].T for g in range(grp)]
            zs = [(_nn(q, kp), _nn(q, kc)) for q in qs]
            dps = [(_nn(do.astype(BF16), vp), _nn(do.astype(BF16), vc)) for do in dos]
            dls, pbs = [], []
            for g in range(grp):
                pp, pc, ps = _swa_probs(*zs[g], bias_ref[g], sink_ref[hk * grp + g], n == 0)
                delta = jnp.sum(dos[g] * ot_ref[g].T, axis=1, keepdims=True)
                dlp = pp * (dps[g][0] - delta)
                dlc = pc * (dps[g][1] - delta)
                dbias_ref[g, :, :T] += dlp
                dbias_ref[g, :, T:] += dlc
                dsink_ref[g] += -ps * delta
                dls.append((dlp.astype(BF16), dlc.astype(BF16)))
                pbs.append((pp.astype(BF16), pc.astype(BF16)))
            for g in range(grp):
                dlpb, dlcb = dls[g]
                doT = dot_ref[g].astype(BF16)
                dq_ref[g] = _nt(kp, dlpb) + _nt(kc, dlcb)
                kprev += _nn(qTs[g], dlpb)
                kcur += _nn(qTs[g], dlcb)
                vprev += _nn(doT, pbs[g][0])
                vcur += _nn(doT, pbs[g][1])
            dk_ref[...] = ck[...] + kprev
            dv_ref[...] = cv[...] + vprev
            ck[...] = kcur
            cv[...] = vcur

        @pl.when(n == nb)
        def _():
            dk_ref[...] = ck[...]
            dv_ref[...] = cv[...]

    qt_spec, kv_specs = _swa_specs(nb, dh, T, grp, Hq, Hkv, True)
    prev = pl.BlockSpec((None, None, dh, T), lambda h, n: (h, jnp.maximum(n - 1, 0), 0, 0))
    per_group = lambda a, b: pl.BlockSpec((grp, a, b), lambda h, n: (h, 0, 0))
    return pl.pallas_call(
        body, name="swa_bwd", grid=(Hkv, nb + 1),
        in_specs=[pl.BlockSpec(memory_space=pltpu.SMEM), qt_spec] + kv_specs
                 + [per_group(T, 2 * T), qt_spec, qt_spec],
        out_specs=[qt_spec, prev, prev, per_group(T, 2 * T), per_group(T, 1)],
        out_shape=[jax.ShapeDtypeStruct((Hq, nb, dh, T), F32), jax.ShapeDtypeStruct((Hkv, nb, dh, T), F32),
                   jax.ShapeDtypeStruct((Hkv, nb, dh, T), F32), jax.ShapeDtypeStruct((Hq, T, 2 * T), F32),
                   jax.ShapeDtypeStruct((Hq, T, 1), F32)],
        scratch_shapes=[pltpu.VMEM((dh, T), F32), pltpu.VMEM((dh, T), F32)],
        compiler_params=_cp("arbitrary", "arbitrary"),
    )(sinks, qkv, qkv, qkv, qkv, qkv, bias, dot, ot)


def _split3(x):
    h1 = x.astype(BF16)
    r1 = x - h1.astype(F32)
    h2 = r1.astype(BF16)
    h3 = (r1 - h2.astype(F32)).astype(BF16)
    return h1, h2, h3


def _bias_expand(rel_t, onehot):
    Hq, NB = rel_t.shape
    L = onehot.shape[1]

    def body(r_ref, oh_ref, o_ref):
        h1, h2, h3 = _split3(r_ref[...])
        oh = oh_ref[...]
        o_ref[...] = _nn(h1, oh) + _nn(h2, oh) + _nn(h3, oh)

    return pl.pallas_call(
        body, name="bias_expand", grid=(1,),
        in_specs=[pl.BlockSpec((Hq, NB), lambda i: (0, 0)), pl.BlockSpec((NB, L), lambda i: (0, 0))],
        out_specs=pl.BlockSpec((Hq, L), lambda i: (0, 0)),
        out_shape=jax.ShapeDtypeStruct((Hq, L), F32),
        compiler_params=_cp("arbitrary"),
    )(rel_t, onehot)


def _bias_reduce(dbias, onehot):
    Hq, L = dbias.shape
    NB = onehot.shape[0]

    def body(d_ref, oh_ref, o_ref):
        h1, h2, h3 = _split3(d_ref[...])
        oh = oh_ref[...]
        o_ref[...] = _nt(h1, oh) + _nt(h2, oh) + _nt(h3, oh)

    return pl.pallas_call(
        body, name="bias_reduce", grid=(1,),
        in_specs=[pl.BlockSpec((Hq, L), lambda i: (0, 0)), pl.BlockSpec((NB, L), lambda i: (0, 0))],
        out_specs=pl.BlockSpec((Hq, NB), lambda i: (0, 0)),
        out_shape=jax.ShapeDtypeStruct((Hq, NB), F32),
        compiler_params=_cp("arbitrary"),
    )(dbias, onehot)


def _adamw(w, g, m, v, name):
    R, C = w.shape
    tr = 256 if R % 256 == 0 else R
    bc1 = 1.0 - ADAM_B1 ** ADAM_STEP
    bc2 = 1.0 - ADAM_B2 ** ADAM_STEP

    def body(w_ref, g_ref, m_ref, v_ref, d_ref, nm_ref, nv_ref):
        g = g_ref[...]
        m2 = ADAM_B1 * m_ref[...] + (1.0 - ADAM_B1) * g
        v2 = ADAM_B2 * v_ref[...] + (1.0 - ADAM_B2) * (g * g)
        nm_ref[...] = m2
        nv_ref[...] = v2
        d_ref[...] = -ADAM_LR * ((m2 / bc1) / (jnp.sqrt(v2 / bc2) + ADAM_EPS) + ADAM_WD * w_ref[...])

    spec = pl.BlockSpec((tr, C), lambda i: (i, 0))
    return pl.pallas_call(
        body, name=name, grid=(R // tr,),
        in_specs=[spec] * 4, out_specs=[spec] * 3,
        out_shape=[jax.ShapeDtypeStruct((R, C), F32)] * 3,
        compiler_params=_cp("arbitrary"),
    )(w, g, m, v)


def _row_tile(R):
    return max(t for t in range(16, 513, 16) if R % t == 0)


def _add_halves(mine, recv, name):
    K, R, C = mine.shape
    tr = _row_tile(R)

    def body(a_ref, b_ref, o_ref, ob_ref):
        s = a_ref[...].astype(F32) + b_ref[...].astype(F32)
        o_ref[...] = s
        ob_ref[...] = s.astype(BF16)

    spec = pl.BlockSpec((None, tr, C), lambda k, i: (k, i, 0))
    return pl.pallas_call(
        body, name=name, grid=(K, R // tr),
        in_specs=[spec, spec], out_specs=[spec, spec],
        out_shape=[jax.ShapeDtypeStruct((K, R, C), F32), jax.ShapeDtypeStruct((K, R, C), BF16)],
        compiler_params=_cp("arbitrary", "arbitrary"),
    )(mine, recv)


def _add_received(own, recv, name):
    R, C = own.shape
    tr = _row_tile(R)

    def body(a_ref, r_ref, o_ref):
        o_ref[...] = ((a_ref[...] + r_ref[0].astype(F32)) + r_ref[1].astype(F32)) + r_ref[2].astype(F32)

    return pl.pallas_call(
        body, name=name, grid=(R // tr,),
        in_specs=[pl.BlockSpec((tr, C), lambda i: (i, 0)), pl.BlockSpec((3, tr, C), lambda i: (0, i, 0))],
        out_specs=pl.BlockSpec((tr, C), lambda i: (i, 0)),
        out_shape=jax.ShapeDtypeStruct((R, C), F32),
        compiler_params=_cp("arbitrary"),
    )(own, recv)


def _position():
    x, y, c = lax.axis_index("x"), lax.axis_index("y"), lax.axis_index("c")
    others = [(1 - x, y), (x, 1 - y), (1 - x, 1 - y)]
    return x, y, c, others


def _remote(src, dst, send_sems, recv_sems, k, dev):
    return pltpu.make_async_remote_copy(src_ref=src, dst_ref=dst, send_sem=send_sems.at[k],
                                        recv_sem=recv_sems.at[k], device_id=dev, device_id_type=MESH_ID)


class _gather_exchange:
    def __init__(self, src, out, send_sems, recv_sems):
        x, y, c, others = _position()
        mine, sibling = 2 * x + y, (x, y, 1 - c)
        self.sends, self.arrivals, self.passes, self.from_sibling = [], [], [], []
        for j, (ox, oy) in enumerate(others):
            slot = out.at[2 * ox + oy, c]
            theirs = out.at[2 * ox + oy, 1 - c]
            self.sends.append(_remote(src.at[c], out.at[mine, c], send_sems, recv_sems, j, (ox, oy, c)))
            self.arrivals.append(_remote(slot, slot, send_sems, recv_sems, j, (ox, oy, c)))
            self.passes.append(_remote(slot, slot, send_sems, recv_sems, 3 + j, sibling))
            self.from_sibling.append(_remote(theirs, theirs, send_sems, recv_sems, 3 + j, sibling))

    def start(self):
        for cp in self.sends:
            cp.start()

    def forward(self):
        for arrived, onward in zip(self.arrivals, self.passes):
            arrived.wait_recv()
            onward.start()

    def finish(self):
        for cp in self.from_sibling:
            cp.wait_recv()
        for cp in self.sends + self.passes:
            cp.wait_send()


def _gather_weights(shard):
    R, C = shard.shape
    half = R // 2

    def body(src, out, send_sems, recv_sems):
        ex = _gather_exchange(src, out, send_sems, recv_sems)
        ex.start()
        ex.forward()
        ex.finish()

    return pl.pallas_call(
        body, name="gather_weights",
        in_specs=[ANY], out_specs=ANY,
        out_shape=jax.ShapeDtypeStruct((N_CHIPS, 2, half, C), shard.dtype),
        scratch_shapes=[pltpu.SemaphoreType.DMA((6,)), pltpu.SemaphoreType.DMA((6,))],
    )(shard.reshape(2, half, C)).reshape(N_CHIPS, R, C)


def _swap_halves(grads, name):
    K, R, C = grads.shape
    half = R // 2

    def body(src, out, send_sems, recv_sems):
        x, y, c, _ = _position()
        theirs = src.at[:, pl.ds(pl.multiple_of((1 - c) * half, 16), half), :]
        cp = _remote(theirs, out, send_sems, recv_sems, 0, (x, y, 1 - c))
        cp.start()
        cp.wait()

    return pl.pallas_call(
        body, name=name,
        in_specs=[ANY], out_specs=ANY,
        out_shape=jax.ShapeDtypeStruct((K, half, C), grads.dtype),
        scratch_shapes=[pltpu.SemaphoreType.DMA((1,)), pltpu.SemaphoreType.DMA((1,))],
    )(grads)


class _scatter_exchange:
    def __init__(self, src, out, send_sems, recv_sems):
        x, y, c, others = _position()
        self.copies = [_remote(src.at[2 * ox + oy], out.at[j], send_sems, recv_sems, j, (ox, oy, c))
                       for j, (ox, oy) in enumerate(others)]

    def start(self):
        for cp in self.copies:
            cp.start()

    def finish(self):
        for cp in self.copies:
            cp.wait()


def _scatter_to_owners(parts, name):
    K, H, C = parts.shape

    def body(src, out, send_sems, recv_sems):
        ex = _scatter_exchange(src, out, send_sems, recv_sems)
        ex.start()
        ex.finish()

    return pl.pallas_call(
        body, name=name,
        in_specs=[ANY], out_specs=ANY,
        out_shape=jax.ShapeDtypeStruct((3, H, C), parts.dtype),
        scratch_shapes=[pltpu.SemaphoreType.DMA((3,)), pltpu.SemaphoreType.DMA((3,))],
    )(parts)


def _swap_reduced(half_rows, name):
    H, C = half_rows.shape

    def body(src, out, send_sems, recv_sems):
        x, y, c, _ = _position()
        cp = _remote(src, out, send_sems, recv_sems, 0, (x, y, 1 - c))
        cp.start()
        cp.wait()

    return pl.pallas_call(
        body, name=name,
        in_specs=[ANY], out_specs=ANY,
        out_shape=jax.ShapeDtypeStruct((H, C), half_rows.dtype),
        scratch_shapes=[pltpu.SemaphoreType.DMA((1,)), pltpu.SemaphoreType.DMA((1,))],
    )(half_rows)


def _allreduce_small(block):
    R, C = block.shape
    n_dev = 8

    def body(src, out, slots, send_sems, recv_sems):
        x, y, c, _ = _position()
        me = 4 * x + 2 * y + c
        slots[me] = src[...]
        sends = []
        for r in range(1, n_dev):
            peer = (x ^ (r >> 2), y ^ ((r >> 1) & 1), c ^ (r & 1))
            cp = _remote(src, slots.at[me], send_sems, recv_sems, r - 1, peer)
            cp.start()
            sends.append(cp)
        for r in range(1, n_dev):
            theirs = slots.at[me ^ r]
            _remote(theirs, theirs, send_sems, recv_sems, r - 1, (x, y, c)).wait_recv()
        for cp in sends:
            cp.wait_send()
        acc = slots[0]
        for d in range(1, n_dev):
            acc = acc + slots[d]
        out[...] = acc

    return pl.pallas_call(
        body, name="allreduce_small",
        in_specs=[pl.BlockSpec(memory_space=pltpu.VMEM)], out_specs=pl.BlockSpec(memory_space=pltpu.VMEM),
        out_shape=jax.ShapeDtypeStruct((R, C), F32),
        scratch_shapes=[pltpu.VMEM((n_dev, R, C), F32), pltpu.SemaphoreType.DMA((7,)), pltpu.SemaphoreType.DMA((7,))],
    )(block)


def _rel_bucket(dist):
    max_exact = REL_BUCKETS // 2
    d = jnp.maximum(dist, 1).astype(F32)
    large = max_exact + (jnp.log(d / max_exact) / math.log(REL_MAX_DIST / max_exact)
                         * (REL_BUCKETS - max_exact)).astype(jnp.int32)
    large = jnp.minimum(large, REL_BUCKETS - 1)
    return jnp.where(dist < max_exact, dist, large)


def _bucket_onehot():
    T = SWA_BLOCK
    dist = (jnp.arange(T)[:, None] + T) - jnp.arange(2 * T)[None, :]
    bucket = _rel_bucket(jnp.maximum(dist, 0)).reshape(1, T * 2 * T)
    return (bucket == jnp.arange(REL_BUCKETS)[:, None]).astype(BF16)


_BUF = (("ffn1_w1", "t"), ("ffn1_w3", "t"), ("ffn1_w2", "n"), ("ffn2_w1", "t"), ("ffn2_w3", "t"),
        ("ffn2_w2", "n"), ("w_in", "t"), ("w_out", "n"), ("w_branch_swa", "tw"), ("w_branch_sb", "tw"))


def _to_rows(name_kind, w, D):
    kind = name_kind[1]
    if kind == "n":
        return w
    if kind == "t":
        return w.T
    return w.T.reshape(-1, D)


def _from_rows(name_kind, rows, width):
    kind = name_kind[1]
    if kind == "n":
        return rows
    if kind == "t":
        return rows.T
    return rows.reshape(-1, width).T


def kernel(x, norm_ffn1, ffn1_w1, ffn1_w3, ffn1_w2, norm_mix, w_in, swa_sinks, rel_bias, w_branch_swa, w_branch_sb, w_out, norm_ffn2, ffn2_w1, ffn2_w3, ffn2_w2, norm_final, loss_target, m_norm_ffn1, m_ffn1_w1, m_ffn1_w3, m_ffn1_w2, m_norm_mix, m_w_in, m_swa_sinks, m_rel_bias, m_w_branch_swa, m_w_branch_sb, m_w_out, m_norm_ffn2, m_ffn2_w1, m_ffn2_w3, m_ffn2_w2, m_norm_final, v_norm_ffn1, v_ffn1_w1, v_ffn1_w3, v_ffn1_w2, v_norm_mix, v_w_in, v_swa_sinks, v_rel_bias, v_w_branch_swa, v_w_branch_sb, v_w_out, v_norm_ffn2, v_ffn2_w1, v_ffn2_w3, v_ffn2_w2, v_norm_final):
    names = ["norm_ffn1", "ffn1_w1", "ffn1_w3", "ffn1_w2", "norm_mix", "w_in", "swa_sinks", "rel_bias",
             "w_branch_swa", "w_branch_sb", "w_out", "norm_ffn2", "ffn2_w1", "ffn2_w3", "ffn2_w2", "norm_final"]
    W = dict(zip(names, [norm_ffn1, ffn1_w1, ffn1_w3, ffn1_w2, norm_mix, w_in, swa_sinks, rel_bias,
                         w_branch_swa, w_branch_sb, w_out, norm_ffn2, ffn2_w1, ffn2_w3, ffn2_w2, norm_final]))
    M = dict(zip(names, [m_norm_ffn1, m_ffn1_w1, m_ffn1_w3, m_ffn1_w2, m_norm_mix, m_w_in, m_swa_sinks, m_rel_bias,
                         m_w_branch_swa, m_w_branch_sb, m_w_out, m_norm_ffn2, m_ffn2_w1, m_ffn2_w3, m_ffn2_w2,
                         m_norm_final]))
    V = dict(zip(names, [v_norm_ffn1, v_ffn1_w1, v_ffn1_w3, v_ffn1_w2, v_norm_mix, v_w_in, v_swa_sinks, v_rel_bias,
                         v_w_branch_swa, v_w_branch_sb, v_w_out, v_norm_ffn2, v_ffn2_w1, v_ffn2_w3, v_ffn2_w2,
                         v_norm_final]))
    xs = x[0]
    target = loss_target[0]
    S, D = xs.shape
    QW = SWA_Q_HEADS * HEAD_DIM
    KW = SWA_KV_HEADS * HEAD_DIM
    BW = SB_HEADS * HEAD_DIM
    QKV = QW + 2 * KW + 3 * BW

    pieces = [_to_rows(nk, W[nk[0]][0], D) for nk in _BUF]
    sizes = [p.shape[0] for p in pieces]
    offs = [0]
    for s in sizes:
        offs.append(offs[-1] + s)
    n_first = 3
    first_rows = offs[n_first]
    shard_a = jnp.concatenate(pieces[:n_first], axis=0).astype(BF16)
    shard_b = jnp.concatenate(pieces[n_first:], axis=0).astype(BF16)
    chip = 2 * lax.axis_index("x") + lax.axis_index("y")
    gathered_a = lax.dynamic_update_slice(_gather_weights(shard_a), shard_a[None], (chip, 0, 0))
    f1w1, f1w3, f1w2 = [gathered_a[:, offs[i]:offs[i + 1], :].reshape(N_CHIPS * sizes[i], D) for i in range(n_first)]

    g1, gmix, g3 = W["norm_ffn1"], W["norm_mix"], W["norm_ffn2"]
    gf = W["norm_final"].reshape(1, D)

    x1, h1, a1, b1, gathered_b = _ffn_fwd(xs, g1, f1w1, f1w3, f1w2, "ffn1_fwd", gather=shard_b)
    gathered_b = lax.dynamic_update_slice(gathered_b, shard_b[None], (chip, 0, 0))

    def full(i):
        return gathered_b[:, offs[i] - first_rows:offs[i + 1] - first_rows, :].reshape(N_CHIPS * sizes[i], D)

    f2w1, f2w3, f2w2, w_in_t, w_out_f = [full(i) for i in range(n_first, 8)]
    wa_t = full(8).reshape(D, QW)
    wb_t = full(9).reshape(D, BW)
    o0 = QW + 2 * KW
    rows = jnp.arange(w_in_t.shape[0])
    is_q = (rows < QW) | ((rows >= o0) & (rows < o0 + BW))
    w_in_s = w_in_t * jnp.where(is_q, QK_SCALE, 1.0).astype(BF16)[:, None]
    qkv_a = _norm_proj_heads(x1, gmix, w_in_s[:o0], SWA_BLOCK, "proj_swa")
    qkv_b = _norm_proj_heads(x1, gmix, w_in_s[o0:QKV], SB_BLOCK, "proj_sb")
    gates, h2 = _norm_matmul_nt(x1, gmix, w_in_t[QKV:], F32, "proj_gates")

    onehot = _bucket_onehot()
    bias = _bias_expand(W["rel_bias"].T, onehot).reshape(SWA_Q_HEADS, SWA_BLOCK, 2 * SWA_BLOCK)
    sinks = W["swa_sinks"].reshape(SWA_Q_HEADS)
    oa_t = _swa_fwd(qkv_a, bias, sinks)
    ob_t, saved_sb = _sb_fwd(qkv_b)

    x2, merged, ba, bb = _merge_fwd(x1, gates, oa_t, ob_t, wa_t, wb_t, w_out_f)
    x3, h3, a2, b2 = _ffn_fwd(x2, g3, f2w1, f2w3, f2w2, "ffn2_fwd")
    loss_part, dx3, dgf = _final_loss(x3, gf, target)

    dx2, dg3, dz2, da2, db2, u2 = _ffn_bwd(dx3, x2, g3, a2, b2, f2w1, f2w3, f2w2, "ffn2_bwd")
    grads = {}
    grads["ffn2_w1"] = _tn_matmul(da2, h3, "ffn2_dw1")
    grads["ffn2_w3"] = _tn_matmul(db2, h3, "ffn2_dw3")
    grads["ffn2_w2"] = _tn_matmul(u2, dz2, "ffn2_dw2")

    dx2b, dba, dbb, dgates, doa_t, dob_t = _merge_bwd(dx2, gates, ba, bb, wa_t.T, wb_t.T, w_out_f,
                                                      SWA_BLOCK, SB_BLOCK)
    grads["w_out"] = _tn_matmul(merged, dx2b, "dw_out")
    grads["w_branch_swa"] = _heads_matmul(oa_t, dba, 1.0, "dw_branch_swa").T
    grads["w_branch_sb"] = _heads_matmul(ob_t, dbb, 1.0, "dw_branch_sb").T

    dqb_t, dkb_t, dvb_t = _sb_bwd(qkv_b, dob_t, saved_sb)
    dqa_t, dka_t, dva_t, dbias, dsink_rows = _swa_bwd(qkv_a, bias, sinks, doa_t, oa_t)
    d_rel = _bias_reduce(dbias.reshape(SWA_Q_HEADS, -1), onehot).T
    d_sinks = jnp.sum(dsink_rows, axis=(1, 2))

    dheads = [(dqa_t, QK_SCALE, "q_swa"), (dka_t, 1.0, "k_swa"), (dva_t, 1.0, "v_swa"),
              (dqb_t, QK_SCALE, "q_sb"), (dkb_t, 1.0, "k_sb"), (dvb_t, 1.0, "v_sb")]
    grads["w_in"] = jnp.concatenate([_heads_matmul(a, h2, sc, "dw_in_" + nm).astype(BF16) for a, sc, nm in dheads]
                                    + [_tn_matmul(dgates, h2, "dw_in_gates")], axis=0)
    row0, pieces_in = 0, []
    for a, _, _ in dheads:
        pieces_in.append((a, row0))
        row0 += a.shape[0] * HEAD_DIM
    dx1, dgmix = _proj_bwd(pieces_in, dgates, w_in_s, x1, gmix, dx2)

    c = lax.axis_index("c")

    def reduce_start(lo, hi, tag):
        gbuf = jnp.concatenate([grads[_BUF[i][0]].astype(BF16).reshape(N_CHIPS, sizes[i], D) for i in range(lo, hi)],
                               axis=1)
        half = gbuf.shape[1] // 2
        from_sibling = _swap_halves(gbuf, "swap_halves_" + tag)
        my_half = lax.dynamic_slice_in_dim(gbuf, c * half, half, axis=1)
        return _add_halves(my_half, from_sibling, "add_sibling_" + tag)

    def reduce_finish(chip_sum, received, tag):
        own = lax.dynamic_index_in_dim(chip_sum, chip, axis=0, keepdims=False)
        my_rows = _add_received(own, received, "add_chips_" + tag)
        their_rows = _swap_reduced(my_rows, "swap_reduced_" + tag)
        return jnp.concatenate([jnp.where(c == 0, my_rows, their_rows), jnp.where(c == 0, their_rows, my_rows)],
                               axis=0)

    sum_b, sum16_b = reduce_start(n_first, len(_BUF), "late")
    dx0, dg1, dz1, da1, db1, u1, received_b = _ffn_bwd(dx1, xs, g1, a1, b1, f1w1, f1w3, f1w2, "ffn1_bwd",
                                                       scatter=sum16_b)
    grads["ffn1_w1"] = _tn_matmul(da1, h1, "ffn1_dw1")
    grads["ffn1_w3"] = _tn_matmul(db1, h1, "ffn1_dw3")
    grads["ffn1_w2"] = _tn_matmul(u1, dz1, "ffn1_dw2")
    sum_a, sum16_a = reduce_start(0, n_first, "first")
    reduced = jnp.concatenate([reduce_finish(sum_a, _scatter_to_owners(sum16_a, "scatter_to_owners"), "first"),
                               reduce_finish(sum_b, received_b, "late")], axis=0)

    small_rows = [dg1, dgmix, dg3, dgf,
                  jnp.pad(d_sinks.reshape(1, -1), ((0, 0), (0, D - SWA_Q_HEADS))),
                  jnp.pad(d_rel.reshape(1, -1), ((0, 0), (0, D - REL_BUCKETS * SWA_Q_HEADS))),
                  jnp.pad(loss_part, ((0, 0), (0, D - 1))), jnp.zeros((1, D), F32)]
    small = _allreduce_small(jnp.concatenate(small_rows, axis=0))
    loss = small[6, 0]

    G = {}
    for i, nk in enumerate(_BUF):
        G[nk[0]] = _from_rows(nk, reduced[offs[i]:offs[i + 1]], W[nk[0]].shape[1])[None]
    G["norm_ffn1"], G["norm_mix"], G["norm_ffn2"] = small[0:1], small[1:2], small[2:3]
    G["norm_final"] = small[3]
    G["swa_sinks"] = small[4:5, :SWA_Q_HEADS]
    G["rel_bias"] = small[5, :REL_BUCKETS * SWA_Q_HEADS].reshape(REL_BUCKETS, SWA_Q_HEADS)

    delta, new_m, new_v = {}, {}, {}
    small_names = ["norm_ffn1", "norm_mix", "norm_ffn2", "norm_final", "swa_sinks", "rel_bias"]

    def pack(d):
        return jnp.concatenate([jnp.pad(d[n].reshape(1, -1), ((0, 0), (0, D - d[n].size))) for n in small_names]
                               + [jnp.zeros((2, D), F32)], axis=0)

    sd, sm, sv = _adamw(pack(W), pack(G), pack(M), pack(V), "adamw_small")
    for r, n in enumerate(small_names):
        for dst, src in ((delta, sd), (new_m, sm), (new_v, sv)):
            dst[n] = src[r, :W[n].size].reshape(W[n].shape)
    for nk in _BUF:
        n = nk[0]
        shp = W[n].shape
        two_d = (shp[1], shp[2])
        d_, m_, v_ = _adamw(W[n].reshape(two_d), G[n].reshape(two_d), M[n].reshape(two_d), V[n].reshape(two_d),
                            "adamw_" + n)
        delta[n], new_m[n], new_v[n] = d_.reshape(shp), m_.reshape(shp), v_.reshape(shp)

    return (loss, dx0[None], *[G[n] for n in names], *[delta[n] for n in names],
            *[new_m[n] for n in names], *[new_v[n] for n in names])
```

```python
import functools
import math

import jax
import jax.numpy as jnp
from jax import lax
from jax.experimental import pallas as pl
from jax.experimental.pallas import tpu as pltpu

F32, BF16 = jnp.float32, jnp.bfloat16
MESH_ID = pl.DeviceIdType.MESH
ANY = pl.BlockSpec(memory_space=pl.ANY)

RMS_EPS = 1e-6
HEAD_DIM = 64
SWA_Q_HEADS, SWA_KV_HEADS, SWA_GROUP = 8, 2, 4
SWA_BLOCK = 128
SB_HEADS = 8
SB_BLOCK = 256
REL_BUCKETS, REL_MAX_DIST = 32, 128
NEG_BIG = -1e30
QK_SCALE = HEAD_DIM ** -0.5
ADAM_LR, ADAM_B1, ADAM_B2, ADAM_EPS, ADAM_WD, ADAM_STEP = 0.001, 0.9, 0.999, 1e-08, 0.01, 10

N_CHIPS = 4
TOKEN_TILE = 512
MATMUL_TOKEN_TILE = 1024
WGRAD_ROW_TILES = (2176, 1408, 1024, 256)
FF_TILE = 1408
FFN_TOKEN_TILE = 512
FF_BWD_TILE = 256
VMEM_LIMIT = 56 * 1024 * 1024


def _cp(*sem):
    return pltpu.CompilerParams(dimension_semantics=sem, vmem_limit_bytes=VMEM_LIMIT)


def _nn(a, b):
    return jnp.dot(a, b, preferred_element_type=F32)


def _nt(a, b):
    return lax.dot_general(a, b, (((1,), (1,)), ((), ())), preferred_element_type=F32)


def _tn(a, b):
    return lax.dot_general(a, b, (((0,), (0,)), ((), ())), preferred_element_type=F32)


def _norm_fwd(x, g):
    return x * lax.rsqrt(jnp.mean(x * x, axis=-1, keepdims=True) + RMS_EPS) * g


def _norm_bwd(x, g, dh):
    r = lax.rsqrt(jnp.mean(x * x, axis=-1, keepdims=True) + RMS_EPS)
    xh = x * r
    dxh = dh * g
    dx = r * (dxh - xh * jnp.mean(dxh * xh, axis=-1, keepdims=True))
    return dx, jnp.sum(dh * xh, axis=0, keepdims=True)


SOFTPLUS_LINEAR = 20.0


def _softplus(z):
    return jnp.maximum(jnp.log(1.0 + jnp.exp(jnp.minimum(z, SOFTPLUS_LINEAR))), z)


def _ffn_fwd(x, g, w1t, w3t, w2, name, gather=None):
    S, D = x.shape
    F = w2.shape[0]
    tm, tf = min(FFN_TOKEN_TILE, S), FF_TILE
    ni, nj = S // tm, F // tf

    def body(x_ref, g_ref, w1_ref, w3_ref, w2_ref, *rest):
        if gather is None:
            xo_ref, h_ref, a_ref, b_ref, hs, acc = rest
        else:
            shard_ref, xo_ref, h_ref, a_ref, b_ref, gathered_ref, hs, acc, send_sems, recv_sems = rest
        i, j = pl.program_id(0), pl.program_id(1)
        if gather is not None:
            for when, phase in ((jnp.logical_and(i == 0, j == 0), "start"),
                                (jnp.logical_and(i == ni - 1, j == 0), "forward"),
                                (jnp.logical_and(i == ni - 1, j == nj - 1), "finish")):
                @pl.when(when)
                def _():
                    getattr(_gather_exchange(shard_ref, gathered_ref, send_sems, recv_sems), phase)()

        @pl.when(j == 0)
        def _():
            hb = _norm_fwd(x_ref[...], g_ref[...]).astype(BF16)
            hs[...] = hb
            h_ref[...] = hb
            acc[...] = jnp.zeros_like(acc)

        h = hs[...]
        a = _nt(h, w1_ref[...])
        b = _nt(h, w3_ref[...])
        a_ref[...] = a.astype(BF16)
        b_ref[...] = b.astype(BF16)
        u = a * jax.nn.sigmoid(a) * b
        acc[...] += _nn(u.astype(BF16), w2_ref[...])

        @pl.when(j == nj - 1)
        def _():
            xo_ref[...] = x_ref[...] + 0.5 * acc[...]

    in_specs = [pl.BlockSpec((tm, D), lambda i, j: (i, 0)),
                pl.BlockSpec((1, D), lambda i, j: (0, 0)),
                pl.BlockSpec((tf, D), lambda i, j: (j, 0)),
                pl.BlockSpec((tf, D), lambda i, j: (j, 0)),
                pl.BlockSpec((tf, D), lambda i, j: (j, 0))]
    out_specs = [pl.BlockSpec((tm, D), lambda i, j: (i, 0)),
                 pl.BlockSpec((tm, D), lambda i, j: (i, 0)),
                 pl.BlockSpec((tm, tf), lambda i, j: (i, j)),
                 pl.BlockSpec((tm, tf), lambda i, j: (i, j))]
    out_shape = [jax.ShapeDtypeStruct((S, D), F32), jax.ShapeDtypeStruct((S, D), BF16),
                 jax.ShapeDtypeStruct((S, F), BF16), jax.ShapeDtypeStruct((S, F), BF16)]
    scratch = [pltpu.VMEM((tm, D), BF16), pltpu.VMEM((tm, D), F32)]
    operands = [x, g, w1t, w3t, w2]
    if gather is not None:
        R, C = gather.shape
        in_specs.append(ANY)
        out_specs.append(ANY)
        out_shape.append(jax.ShapeDtypeStruct((N_CHIPS, 2, R // 2, C), gather.dtype))
        scratch += [pltpu.SemaphoreType.DMA((6,)), pltpu.SemaphoreType.DMA((6,))]
        operands.append(gather.reshape(2, R // 2, C))
    outs = list(pl.pallas_call(
        body, name=name, grid=(ni, nj), in_specs=in_specs, out_specs=out_specs, out_shape=out_shape,
        scratch_shapes=scratch, compiler_params=_cp("arbitrary", "arbitrary"),
    )(*operands))
    if gather is not None:
        outs[4] = outs[4].reshape(N_CHIPS, R, C)
    return outs


def _ffn_bwd(dxo, x, g, a, b, w1t, w3t, w2, name, scatter=None):
    S, D = x.shape
    F = w2.shape[0]
    tm, tf = min(MATMUL_TOKEN_TILE, S), FF_BWD_TILE
    ni, nj = S // tm, F // tf

    def body(dxo_ref, x_ref, g_ref, a_ref, b_ref, w1_ref, w3_ref, w2_ref, *rest):
        if scatter is None:
            dx_ref, dg_ref, dz_ref, da_ref, db_ref, u_ref, dzs, acc = rest
        else:
            (parts_ref, dx_ref, dg_ref, dz_ref, da_ref, db_ref, u_ref, recv_ref,
             dzs, acc, send_sems, recv_sems) = rest
        i, j = pl.program_id(0), pl.program_id(1)
        if scatter is not None:
            for when, phase in ((jnp.logical_and(i == 0, j == 0), "start"),
                                (jnp.logical_and(i == ni - 1, j == nj - 1), "finish")):
                @pl.when(when)
                def _():
                    getattr(_scatter_exchange(parts_ref, recv_ref, send_sems, recv_sems), phase)()

        @pl.when(j == 0)
        def _():
            dzb = (0.5 * dxo_ref[...]).astype(BF16)
            dzs[...] = dzb
            dz_ref[...] = dzb
            acc[...] = jnp.zeros_like(acc)

        du = _nt(dzs[...], w2_ref[...])
        av = a_ref[...].astype(F32)
        bv = b_ref[...].astype(F32)
        s = jax.nn.sigmoid(av)
        silu = av * s
        db = (du * silu).astype(BF16)
        da = (du * bv * (s * (1.0 + av * (1.0 - s)))).astype(BF16)
        da_ref[...] = da
        db_ref[...] = db
        u_ref[...] = (silu * bv).astype(BF16)
        acc[...] += _nn(da, w1_ref[...]) + _nn(db, w3_ref[...])

        @pl.when(j == nj - 1)
        def _():
            dx, dg = _norm_bwd(x_ref[...], g_ref[...], acc[...])
            dx_ref[...] = dxo_ref[...] + dx

            @pl.when(i == 0)
            def _():
                dg_ref[...] = dg

            @pl.when(i > 0)
            def _():
                dg_ref[...] += dg

    row = pl.BlockSpec((tm, D), lambda i, j: (i, 0))
    wsp = pl.BlockSpec((tf, D), lambda i, j: (j, 0))
    col = pl.BlockSpec((tm, tf), lambda i, j: (i, j))
    vec = pl.BlockSpec((1, D), lambda i, j: (0, 0))
    in_specs = [row, row, vec, col, col, wsp, wsp, wsp]
    out_specs = [row, vec, row, col, col, col]
    out_shape = [jax.ShapeDtypeStruct((S, D), F32), jax.ShapeDtypeStruct((1, D), F32),
                 jax.ShapeDtypeStruct((S, D), BF16), jax.ShapeDtypeStruct((S, F), BF16),
                 jax.ShapeDtypeStruct((S, F), BF16), jax.ShapeDtypeStruct((S, F), BF16)]
    scratch = [pltpu.VMEM((tm, D), BF16), pltpu.VMEM((tm, D), F32)]
    operands = [dxo, x, g, a, b, w1t, w3t, w2]
    if scatter is not None:
        in_specs.append(ANY)
        out_specs.append(ANY)
        out_shape.append(jax.ShapeDtypeStruct((3,) + scatter.shape[1:], scatter.dtype))
        scratch += [pltpu.SemaphoreType.DMA((3,)), pltpu.SemaphoreType.DMA((3,))]
        operands.append(scatter)
    return pl.pallas_call(
        body, name=name, grid=(ni, nj), in_specs=in_specs, out_specs=out_specs, out_shape=out_shape,
        scratch_shapes=scratch, compiler_params=_cp("arbitrary", "arbitrary"),
    )(*operands)


def _tn_matmul(a, b, name):
    S, M = a.shape
    N = b.shape[1]
    ts = min(MATMUL_TOKEN_TILE, S)
    tmm = next(t for t in WGRAD_ROW_TILES if M % t == 0)
    ns = S // ts

    def body(a_ref, b_ref, o_ref, acc):
        s = pl.program_id(1)
        part = _tn(a_ref[...], b_ref[...])

        @pl.when(s == 0)
        def _():
            acc[...] = part

        @pl.when(s > 0)
        def _():
            acc[...] += part

        @pl.when(s == ns - 1)
        def _():
            o_ref[...] = acc[...].astype(BF16)

    return pl.pallas_call(
        body, name=name, grid=(M // tmm, ns),
        in_specs=[pl.BlockSpec((ts, tmm), lambda m, s: (s, m)),
                  pl.BlockSpec((ts, N), lambda m, s: (s, 0))],
        out_specs=pl.BlockSpec((tmm, N), lambda m, s: (m, 0)),
        out_shape=jax.ShapeDtypeStruct((M, N), BF16),
        scratch_shapes=[pltpu.VMEM((tmm, N), F32)],
        compiler_params=_cp("arbitrary", "arbitrary"),
    )(a, b)


def _norm_matmul_nt(x, g, wt, out_dtype, name):
    S, D = x.shape
    N = wt.shape[0]
    tm = min(MATMUL_TOKEN_TILE, S)
    tn = next(t for t in (1024, 768, 256) if N % t == 0)

    def body(x_ref, g_ref, w_ref, o_ref, h_ref, hs):
        @pl.when(pl.program_id(1) == 0)
        def _():
            hb = _norm_fwd(x_ref[...], g_ref[...]).astype(BF16)
            hs[...] = hb
            h_ref[...] = hb

        o_ref[...] = _nt(hs[...], w_ref[...]).astype(out_dtype)

    return pl.pallas_call(
        body, name=name, grid=(S // tm, N // tn),
        in_specs=[pl.BlockSpec((tm, D), lambda i, j: (i, 0)),
                  pl.BlockSpec((1, D), lambda i, j: (0, 0)),
                  pl.BlockSpec((tn, D), lambda i, j: (j, 0))],
        out_specs=[pl.BlockSpec((tm, tn), lambda i, j: (i, j)),
                   pl.BlockSpec((tm, D), lambda i, j: (i, 0))],
        out_shape=[jax.ShapeDtypeStruct((S, N), out_dtype), jax.ShapeDtypeStruct((S, D), BF16)],
        scratch_shapes=[pltpu.VMEM((tm, D), BF16)],
        compiler_params=_cp("arbitrary", "arbitrary"),
    )(x, g, wt)


def _heads_tile(ref):
    Hh, nbk = ref.shape[0], ref.shape[1]
    return jnp.concatenate([jnp.concatenate([ref[h, b] for b in range(nbk)], axis=1) for h in range(Hh)], axis=0)


def _store_heads(ref, val):
    Hh, nbk, dh, T = ref.shape
    for h in range(Hh):
        for b in range(nbk):
            ref[h, b] = val[h * dh:(h + 1) * dh, b * T:(b + 1) * T].astype(ref.dtype)


def _norm_proj_heads(x, g, w_rows, T, name):
    S, D = x.shape
    N = w_rows.shape[0]
    tm, tn = min(MATMUL_TOKEN_TILE, S), 768

    def body(x_ref, g_ref, w_ref, o_ref, hs):
        @pl.when(pl.program_id(1) == 0)
        def _():
            hs[...] = _norm_fwd(x_ref[...], g_ref[...]).astype(BF16)

        _store_heads(o_ref, _nt(w_ref[...], hs[...]))

    return pl.pallas_call(
        body, name=name, grid=(S // tm, N // tn),
        in_specs=[pl.BlockSpec((tm, D), lambda i, j: (i, 0)),
                  pl.BlockSpec((1, D), lambda i, j: (0, 0)),
                  pl.BlockSpec((tn, D), lambda i, j: (j, 0))],
        out_specs=pl.BlockSpec((tn // HEAD_DIM, tm // T, HEAD_DIM, T), lambda i, j: (j, i, 0, 0)),
        out_shape=jax.ShapeDtypeStruct((N // HEAD_DIM, S // T, HEAD_DIM, T), BF16),
        scratch_shapes=[pltpu.VMEM((tm, D), BF16)],
        compiler_params=_cp("arbitrary", "arbitrary"),
    )(x, g, w_rows)


def _heads_matmul(pieces, b, name):
    S, N = b.shape
    ts = min(MATMUL_TOKEN_TILE, S)
    ns = S // ts
    rows = [at.shape[0] * at.shape[2] for at, _ in pieces]

    def body(*refs):
        a_refs, b_ref, o_ref = refs[:-2], refs[-2], refs[-1]
        s = pl.program_id(0)
        row0 = 0
        for a_ref, (_, scale), n in zip(a_refs, pieces, rows):
            a = _heads_tile(a_ref)
            part = _nn((a if scale == 1.0 else a * scale).astype(BF16), b_ref[...])
            out = o_ref.at[row0:row0 + n, :]
            row0 += n

            @pl.when(s == 0)
            def _():
                out[...] = part

            @pl.when(s > 0)
            def _():
                out[...] += part

    return pl.pallas_call(
        body, name=name, grid=(ns,),
        in_specs=[pl.BlockSpec((at.shape[0], ts // at.shape[3], at.shape[2], at.shape[3]), lambda s: (0, s, 0, 0))
                  for at, _ in pieces] + [pl.BlockSpec((ts, N), lambda s: (s, 0))],
        out_specs=pl.BlockSpec((sum(rows), N), lambda s: (0, 0)),
        out_shape=jax.ShapeDtypeStruct((sum(rows), N), F32),
        compiler_params=_cp("arbitrary"),
    )(*[at for at, _ in pieces], b)


def _proj_bwd(pieces, dgates, w_rows, x, g, dres):
    S, D = x.shape
    tm = min(TOKEN_TILE, S)
    n_p = len(pieces)
    gate_row = w_rows.shape[0] - dgates.shape[1]

    def body(*refs):
        p_refs = refs[:n_p]
        dgt_ref, w_ref, x_ref, g_ref, dres_ref, dx_ref, dg_ref = refs[n_p:]
        i = pl.program_id(0)
        dh = _nn(dgt_ref[...], w_ref[gate_row:, :])
        for p_ref, (arr, row0) in zip(p_refs, pieces):
            rows = arr.shape[0] * arr.shape[2]
            dh += _tn(_heads_tile(p_ref).astype(BF16), w_ref[row0:row0 + rows, :])
        dx, dg = _norm_bwd(x_ref[...], g_ref[...], dh)
        dx_ref[...] = dres_ref[...] + dx

        @pl.when(i == 0)
        def _():
            dg_ref[...] = dg

        @pl.when(i > 0)
        def _():
            dg_ref[...] += dg

    row = pl.BlockSpec((tm, D), lambda i: (i, 0))
    vec = pl.BlockSpec((1, D), lambda i: (0, 0))
    p_specs = [pl.BlockSpec((a.shape[0], tm // a.shape[3], a.shape[2], a.shape[3]), lambda i: (0, i, 0, 0))
               for a, _ in pieces]
    return pl.pallas_call(
        body, name="proj_bwd", grid=(S // tm,),
        in_specs=p_specs + [pl.BlockSpec((tm, dgates.shape[1]), lambda i: (i, 0)),
                            pl.BlockSpec(w_rows.shape, lambda i: (0, 0), pipeline_mode=pl.Buffered(1)),
                            row, vec, row],
        out_specs=[row, vec],
        out_shape=[jax.ShapeDtypeStruct((S, D), F32), jax.ShapeDtypeStruct((1, D), F32)],
        compiler_params=_cp("arbitrary"),
    )(*[a for a, _ in pieces], dgates, w_rows, x, g, dres)


def _merge_fwd(x1, gates, oa_t, ob_t, wat, wbt, w_out):
    S, D = x1.shape
    W = wat.shape[1]
    tm = min(TOKEN_TILE, S)

    def body(x_ref, ga_ref, gb_ref, oa_ref, ob_ref, wa_ref, wb_ref, wo_ref,
             x2_ref, mg_ref, ba_ref, bb_ref):
        ba = _nt(_heads_tile(oa_ref).T.astype(BF16), wa_ref[...])
        bb = _nt(_heads_tile(ob_ref).T.astype(BF16), wb_ref[...])
        merged = jax.nn.sigmoid(ga_ref[...]) * ba + jax.nn.sigmoid(gb_ref[...]) * bb
        mb = merged.astype(BF16)
        mg_ref[...] = mb
        ba_ref[...] = ba.astype(BF16)
        bb_ref[...] = bb.astype(BF16)
        x2_ref[...] = x_ref[...] + _nn(mb, wo_ref[...])

    row = pl.BlockSpec((tm, D), lambda i: (i, 0))
    full = lambda r, c: pl.BlockSpec((r, c), lambda i: (0, 0))
    heads = lambda a: pl.BlockSpec((a.shape[0], tm // a.shape[3], a.shape[2], a.shape[3]), lambda i: (0, i, 0, 0))
    return pl.pallas_call(
        body, name="merge_fwd", grid=(S // tm,),
        in_specs=[row, pl.BlockSpec((tm, D), lambda i: (i, 0)), pl.BlockSpec((tm, D), lambda i: (i, 1)),
                  heads(oa_t), heads(ob_t), full(D, W), full(D, W), full(D, D)],
        out_specs=[row, row, row, row],
        out_shape=[jax.ShapeDtypeStruct((S, D), F32)] + [jax.ShapeDtypeStruct((S, D), BF16)] * 3,
        compiler_params=_cp("arbitrary"),
    )(x1, gates, gates, oa_t, ob_t, wat, wbt, w_out)


def _merge_bwd(dx2, gates, ba, bb, wa, wb, w_out, t_a, t_b):
    S, D = dx2.shape
    W = wa.shape[0]
    tm = min(TOKEN_TILE, S)
    Hh = W // HEAD_DIM

    def body(dx_ref, ga_ref, gb_ref, ba_ref, bb_ref, wa_ref, wb_ref, wo_ref,
             dxb_ref, dba_ref, dbb_ref, dgt_ref, doa_ref, dob_ref):
        dxb = dx_ref[...].astype(BF16)
        dxb_ref[...] = dxb
        dm = _nt(dxb, wo_ref[...])
        sa = jax.nn.sigmoid(ga_ref[...])
        sb = jax.nn.sigmoid(gb_ref[...])
        dba = (dm * sa).astype(BF16)
        dbb = (dm * sb).astype(BF16)
        dba_ref[...] = dba
        dbb_ref[...] = dbb
        dgt_ref[:, :D] = (dm * ba_ref[...].astype(F32) * sa * (1.0 - sa)).astype(BF16)
        dgt_ref[:, D:] = (dm * bb_ref[...].astype(F32) * sb * (1.0 - sb)).astype(BF16)
        _store_heads(doa_ref, _nt(wa_ref[...], dba))
        _store_heads(dob_ref, _nt(wb_ref[...], dbb))

    row = pl.BlockSpec((tm, D), lambda i: (i, 0))
    full = lambda r, c: pl.BlockSpec((r, c), lambda i: (0, 0))
    heads = lambda T: pl.BlockSpec((Hh, tm // T, HEAD_DIM, T), lambda i: (0, i, 0, 0))
    return pl.pallas_call(
        body, name="merge_bwd", grid=(S // tm,),
        in_specs=[row, pl.BlockSpec((tm, D), lambda i: (i, 0)), pl.BlockSpec((tm, D), lambda i: (i, 1)),
                  row, row, full(W, D), full(W, D), full(D, D)],
        out_specs=[row, row, row, pl.BlockSpec((tm, 2 * D), lambda i: (i, 0)), heads(t_a), heads(t_b)],
        out_shape=[jax.ShapeDtypeStruct((S, D), BF16)] * 3 + [jax.ShapeDtypeStruct((S, 2 * D), BF16),
                   jax.ShapeDtypeStruct((Hh, S // t_a, HEAD_DIM, t_a), F32),
                   jax.ShapeDtypeStruct((Hh, S // t_b, HEAD_DIM, t_b), BF16)],
        compiler_params=_cp("arbitrary"),
    )(dx2, gates, gates, ba, bb, wa, wb, w_out)


def _final_loss(x3, gf, target):
    S, D = x3.shape
    tm = min(TOKEN_TILE, S)

    def body(x_ref, g_ref, t_ref, loss_ref, dx_ref, dg_ref):
        i = pl.program_id(0)
        x = x_ref[...]
        g = g_ref[...]
        e = _norm_fwd(x, g) - t_ref[...]
        part = 0.5 * jnp.sum(jnp.mean(e * e, axis=-1, keepdims=True), axis=0, keepdims=True)
        dx, dg = _norm_bwd(x, g, e * (1.0 / D))
        dx_ref[...] = dx

        @pl.when(i == 0)
        def _():
            loss_ref[...] = part
            dg_ref[...] = dg

        @pl.when(i > 0)
        def _():
            loss_ref[...] += part
            dg_ref[...] += dg

    row = pl.BlockSpec((tm, D), lambda i: (i, 0))
    vec = pl.BlockSpec((1, D), lambda i: (0, 0))
    return pl.pallas_call(
        body, name="final_loss", grid=(S // tm,),
        in_specs=[row, vec, row],
        out_specs=[pl.BlockSpec((1, 1), lambda i: (0, 0)), row, vec],
        out_shape=[jax.ShapeDtypeStruct((1, 1), F32), jax.ShapeDtypeStruct((S, D), F32),
                   jax.ShapeDtypeStruct((1, D), F32)],
        compiler_params=_cp("arbitrary"),
    )(x3, gf, target)


SB_FWD_HEAD_GROUP = 8
SB_HEAD_GROUP = 4
LANES = 128


def _tri(T, kind):
    r = lax.broadcasted_iota(jnp.int32, (T, T), 0)
    c = lax.broadcasted_iota(jnp.int32, (T, T), 1)
    return {"after": r > c, "upto": r <= c, "before": r < c}[kind].astype(BF16)


def _lane(v, j):
    return jnp.broadcast_to(v[:, j:j + 1], (v.shape[0], LANES))


def _t_bf16(x):
    return x.astype(F32).T.astype(BF16)


def _wide(v, T):
    return jnp.tile(v, (1, T // LANES))


SB_SLOTS = 3
SB_FWD_SLOTS = 2
COPY_PARTS = 4


class _split_copy:
    def __init__(self, src, dst, sems):
        n = src.shape[0] // COPY_PARTS
        self.parts = [pltpu.make_async_copy(src.at[pl.ds(r * n, n)], dst.at[pl.ds(r * n, n)], sems.at[r])
                      for r in range(COPY_PARTS)]

    def start(self):
        for cp in self.parts:
            cp.start()

    def wait(self):
        for cp in self.parts:
            cp.wait()


def _sb_pair(i, kb):
    return (i * (i + 1)) // 2 + kb


def _sb_fwd(qkv):
    H3, nb, dh, T = qkv.shape
    H = H3 // 3
    HG = SB_FWD_HEAD_GROUP
    assert HG == H, "one head group: a saved tile holds all the heads"
    n_pairs = (nb * (nb + 1)) // 2

    def body(q_ref, k_ref, v_ref, o_ref, saved_ref, stage, sems):
        row = lax.broadcasted_iota(jnp.int32, (T, T), 0)
        col = lax.broadcasted_iota(jnp.int32, (T, T), 1)
        tri = col < row
        after = _tri(T, "after")

        def save(slot, pair):
            return _split_copy(stage.at[slot], saved_ref.at[pair], sems.at[slot])

        def blocks(qs, i, kb, step, carry, diag):
            hs = range(HG)
            slot = step % SB_FWD_SLOTS

            @pl.when(step >= SB_FWD_SLOTS)
            def _():
                save(slot, 0).wait()

            z = [_nn(qs[hh], k_ref[hh, kb]) for hh in hs]
            res, ls, first = [None] * HG, [None] * HG, [None] * HG
            for hh in hs:
                sp = _softplus(z[hh])
                if diag:
                    sp = jnp.where(tri, sp, 0.0)
                ls[hh] = z[hh] - sp
                spb = sp.astype(BF16)
                first[hh] = _lane(spb.astype(F32), 0)
                res[hh] = _nn(spb, after)
            out = []
            for hh in hs:
                c, oacc = carry[2 * hh], carry[2 * hh + 1]
                a = jnp.exp(ls[hh] - (res[hh] + _wide(c, T)))
                if diag:
                    a = jnp.where(tri, a, 0.0)
                ab = a.astype(BF16)
                stage[slot, hh, 0] = ab
                stage[slot, hh, 1] = jnp.exp(ls[hh]).astype(BF16)
                out.extend([c + (first[hh] + _lane(res[hh], 0)), oacc + _nt(v_ref[hh, kb], ab)])
            save(slot, _sb_pair(i, kb)).start()
            return tuple(out)

        def qblock(i, step):
            qs = [_t_bf16(q_ref[hh, i]) for hh in range(HG)]
            carry = blocks(qs, i, i, step, (jnp.zeros((T, LANES), F32), jnp.zeros((dh, T), F32)) * HG, True)

            def kstep(t, carry):
                return blocks(qs, i, i - 1 - t, step + 1 + t, carry, False)

            carry = lax.fori_loop(0, i, kstep, carry)
            for hh in range(HG):
                o_ref[hh, i] = carry[2 * hh + 1]
            return step + 1 + i

        lax.fori_loop(0, nb, qblock, 0)
        for slot in range(min(SB_FWD_SLOTS, n_pairs)):
            save(slot, 0).wait()

    ht = lambda part: pl.BlockSpec((HG, nb, dh, T), lambda h: (part, 0, 0, 0), pipeline_mode=pl.Buffered(1))
    return pl.pallas_call(
        body, name="sb_fwd", grid=(1,),
        in_specs=[ht(0), ht(1), ht(2)],
        out_specs=[ht(0), ANY],
        out_shape=[jax.ShapeDtypeStruct((H, nb, dh, T), F32),
                   jax.ShapeDtypeStruct((n_pairs, H, 2, T, T), BF16)],
        scratch_shapes=[pltpu.VMEM((SB_FWD_SLOTS, HG, 2, T, T), BF16),
                        pltpu.SemaphoreType.DMA((SB_FWD_SLOTS, COPY_PARTS))],
        compiler_params=_cp("arbitrary"),
    )(qkv, qkv, qkv)


def _sb_bwd(qkv, dot, saved):
    H3, nb, dh, T = qkv.shape
    H = H3 // 3
    HG = SB_HEAD_GROUP
    n_pairs = (nb * (nb + 1)) // 2

    def body(qt_ref, k_ref, v_ref, dot_ref, saved_ref, dq_ref, dk_ref, dv_ref, stage, sems):
        head0 = pl.program_id(0) * HG
        row = lax.broadcasted_iota(jnp.int32, (T, T), 0)
        col = lax.broadcasted_iota(jnp.int32, (T, T), 1)
        tri = col < row
        before = _tri(T, "before")
        dk_ref[...] = jnp.zeros_like(dk_ref)
        dv_ref[...] = jnp.zeros_like(dv_ref)

        def fetch(slot, pair):
            return _split_copy(saved_ref.at[pair, pl.ds(head0, HG)], stage.at[slot], sems.at[slot])

        for ahead in range(min(SB_SLOTS - 1, n_pairs)):
            fetch(ahead, ahead).start()

        def blocks(qTs, dos, doTs, i, kb, carry, diag):
            hs = range(HG)
            pair = _sb_pair(i, kb)
            slot = pair % SB_SLOTS
            fetch(slot, pair).wait()
            nxt = pair + (SB_SLOTS - 1)

            @pl.when(nxt < n_pairs)
            def _():
                fetch(nxt % SB_SLOTS, nxt).start()

            kT = [k_ref[hh, kb] for hh in hs]
            da = [_nn(dos[hh], v_ref[hh, kb]) for hh in hs]
            g, gb, resg = [None] * HG, [None] * HG, [None] * HG
            for hh in hs:
                g[hh] = stage[slot, hh, 0].astype(F32) * da[hh]
                gb[hh] = g[hh].astype(BF16)
                resg[hh] = _nn(gb[hh], before)
            out = []
            for hh in hs:
                pre_g, dq = carry[2 * hh], carry[2 * hh + 1]
                dz = g[hh] - (g[hh] + (resg[hh] + _wide(pre_g, T))) * stage[slot, hh, 1].astype(F32)
                if diag:
                    dz = jnp.where(tri, dz, 0.0)
                dzb = dz.astype(BF16)
                dk_ref[hh, kb] += _nn(qTs[hh], dzb)
                dv_ref[hh, kb] += _nn(doTs[hh], stage[slot, hh, 0])
                out.extend([pre_g + (_lane(resg[hh], T - 1) + _lane(gb[hh].astype(F32), T - 1)),
                            dq + _nt(kT[hh], dzb)])
            return tuple(out)

        def qblock(i, _):
            qTs = [qt_ref[hh, i] for hh in range(HG)]
            doTs = [dot_ref[hh, i] for hh in range(HG)]
            dos = [_t_bf16(v) for v in doTs]
            carry = (jnp.zeros((T, LANES), F32), jnp.zeros((dh, T), F32)) * HG

            def kstep(kb, carry):
                return blocks(qTs, dos, doTs, i, kb, carry, False)

            carry = lax.fori_loop(0, i, kstep, carry)
            carry = blocks(qTs, dos, doTs, i, i, carry, True)
            for hh in range(HG):
                dq_ref[hh, i] = carry[2 * hh + 1]
            return 0

        lax.fori_loop(0, nb, qblock, 0)

    G = H // HG
    ht = lambda part: pl.BlockSpec((HG, nb, dh, T), lambda h: (h + part * G, 0, 0, 0),
                                   pipeline_mode=pl.Buffered(1))
    return pl.pallas_call(
        body, name="sb_bwd", grid=(G,),
        in_specs=[ht(0), ht(1), ht(2), ht(0), ANY],
        out_specs=[ht(0), ht(0), ht(0)],
        out_shape=[jax.ShapeDtypeStruct((H, nb, dh, T), F32)] * 3,
        scratch_shapes=[pltpu.VMEM((SB_SLOTS, HG, 2, T, T), BF16), pltpu.SemaphoreType.DMA((SB_SLOTS, COPY_PARTS))],
        compiler_params=_cp("arbitrary"),
    )(qkv, qkv, qkv, dot, saved)


def _swa_probs(zp, zc, bias, sink, first):
    T = zp.shape[0]
    row = lax.broadcasted_iota(jnp.int32, (T, T), 0)
    col = lax.broadcasted_iota(jnp.int32, (T, T), 1)
    lp = jnp.where(jnp.logical_and(col > row, jnp.logical_not(first)), zp + bias[:, :T], NEG_BIG)
    lc = jnp.where(col <= row, zc + bias[:, T:], NEG_BIG)
    m = jnp.maximum(jnp.maximum(jnp.max(lp, axis=1, keepdims=True), jnp.max(lc, axis=1, keepdims=True)), sink)
    pp = jnp.exp(lp - m)
    pc = jnp.exp(lc - m)
    ps = jnp.exp(sink - m)
    inv = 1.0 / (jnp.sum(pp, axis=1, keepdims=True) + jnp.sum(pc, axis=1, keepdims=True) + ps)
    return pp * inv, pc * inv, ps * inv


def _swa_specs(nb, dh, T, grp, Hq, Hkv, clamp):
    blk = (lambda n: jnp.minimum(n, nb - 1)) if clamp else (lambda n: n)
    q = pl.BlockSpec((grp, None, dh, T), lambda h, n: (h, blk(n), 0, 0))
    one = lambda first, back: pl.BlockSpec(
        (None, None, dh, T), lambda h, n: (first + h, jnp.maximum(blk(n) - back, 0) if back else blk(n), 0, 0))
    return q, [one(Hq, 1), one(Hq, 0), one(Hq + Hkv, 1), one(Hq + Hkv, 0)]


def _swa_fwd(qkv, bias, sinks):
    Hq, Hkv, grp = SWA_Q_HEADS, SWA_KV_HEADS, SWA_GROUP
    _, nb, dh, T = qkv.shape

    def body(sink_ref, q_ref, kp_ref, kc_ref, vp_ref, vc_ref, bias_ref, o_ref):
        hk, n = pl.program_id(0), pl.program_id(1)
        kp, kc, vp, vc = kp_ref[...], kc_ref[...], vp_ref[...], vc_ref[...]
        qs = [_t_bf16(q_ref[g]) for g in range(grp)]
        zs = [(_nn(q, kp), _nn(q, kc)) for q in qs]
        for g in range(grp):
            pp, pc, _ = _swa_probs(*zs[g], bias_ref[g], sink_ref[hk * grp + g], n == 0)
            o_ref[g] = _nt(vp, pp.astype(BF16)) + _nt(vc, pc.astype(BF16))

    q_spec, kv_specs = _swa_specs(nb, dh, T, grp, Hq, Hkv, False)
    return pl.pallas_call(
        body, name="swa_fwd", grid=(Hkv, nb),
        in_specs=[pl.BlockSpec(memory_space=pltpu.SMEM), q_spec] + kv_specs
                 + [pl.BlockSpec((grp, T, 2 * T), lambda h, n: (h, 0, 0))],
        out_specs=pl.BlockSpec((grp, None, dh, T), lambda h, n: (h, n, 0, 0)),
        out_shape=jax.ShapeDtypeStruct((Hq, nb, dh, T), F32),
        compiler_params=_cp("arbitrary", "arbitrary"),
    )(sinks, qkv, qkv, qkv, qkv, qkv, bias)


def _swa_bwd(qkv, bias, sinks, dot, ot):
    Hq, Hkv, grp = SWA_Q_HEADS, SWA_KV_HEADS, SWA_GROUP
    _, nb, dh, T = qkv.shape

    def body(sink_ref, qt_ref, kp_ref, kc_ref, vp_ref, vc_ref, bias_ref, dot_ref, ot_ref,
             dq_ref, dk_ref, dv_ref, dbias_ref, dsink_ref, ck, cv):
        hk, n = pl.program_id(0), pl.program_id(1)

        @pl.when(n == 0)
        def _():
            dbias_ref[...] = jnp.zeros_like(dbias_ref)
            dsink_ref[...] = jnp.zeros_like(dsink_ref)
            ck[...] = jnp.zeros_like(ck)
            cv[...] = jnp.zeros_like(cv)

        @pl.when(n < nb)
        def _():
            kp, kc, vp, vc = kp_ref[...], kc_ref[...], vp_ref[...], vc_ref[...]
            kprev = jnp.zeros((dh, T), F32)
            vprev = jnp.zeros((dh, T), F32)
            kcur = jnp.zeros((dh, T), F32)
            vcur = jnp.zeros((dh, T), F32)
            qTs = [qt_ref[g] for g in range(grp)]
            qs = [_t_bf16(v) for v in qTs]
            dos = [dot_ref[g].T for g in range(grp)]
            zs = [(_nn(q, kp), _nn(q, kc)) for q in qs]
            dps = [(_nn(do.astype(BF16), vp), _nn(do.astype(BF16), vc)) for do in dos]
            dls, pbs = [], []
            for g in range(grp):
                pp, pc, ps = _swa_probs(*zs[g], bias_ref[g], sink_ref[hk * grp + g], n == 0)
                delta = jnp.sum(dos[g] * ot_ref[g].T, axis=1, keepdims=True)
                dlp = pp * (dps[g][0] - delta)
                dlc = pc * (dps[g][1] - delta)
                dbias_ref[g, :, :T] += dlp
                dbias_ref[g, :, T:] += dlc
                dsink_ref[g] += -ps * delta
                dls.append((dlp.astype(BF16), dlc.astype(BF16)))
                pbs.append((pp.astype(BF16), pc.astype(BF16)))
            for g in range(grp):
                dlpb, dlcb = dls[g]
                doT = dot_ref[g].astype(BF16)
                dq_ref[g] = _nt(kp, dlpb) + _nt(kc, dlcb)
                kprev += _nn(qTs[g], dlpb)
                kcur += _nn(qTs[g], dlcb)
                vprev += _nn(doT, pbs[g][0])
                vcur += _nn(doT, pbs[g][1])
            dk_ref[...] = ck[...] + kprev
            dv_ref[...] = cv[...] + vprev
            ck[...] = kcur
            cv[...] = vcur

        @pl.when(n == nb)
        def _():
            dk_ref[...] = ck[...]
            dv_ref[...] = cv[...]

    qt_spec, kv_specs = _swa_specs(nb, dh, T, grp, Hq, Hkv, True)
    prev = pl.BlockSpec((None, None, dh, T), lambda h, n: (h, jnp.maximum(n - 1, 0), 0, 0))
    per_group = lambda a, b: pl.BlockSpec((grp, a, b), lambda h, n: (h, 0, 0))
    return pl.pallas_call(
        body, name="swa_bwd", grid=(Hkv, nb + 1),
        in_specs=[pl.BlockSpec(memory_space=pltpu.SMEM), qt_spec] + kv_specs
                 + [per_group(T, 2 * T), qt_spec, qt_spec],
        out_specs=[qt_spec, prev, prev, per_group(T, 2 * T), per_group(T, 1)],
        out_shape=[jax.ShapeDtypeStruct((Hq, nb, dh, T), F32), jax.ShapeDtypeStruct((Hkv, nb, dh, T), F32),
                   jax.ShapeDtypeStruct((Hkv, nb, dh, T), F32), jax.ShapeDtypeStruct((Hq, T, 2 * T), F32),
                   jax.ShapeDtypeStruct((Hq, T, 1), F32)],
        scratch_shapes=[pltpu.VMEM((dh, T), F32), pltpu.VMEM((dh, T), F32)],
        compiler_params=_cp("arbitrary", "arbitrary"),
    )(sinks, qkv, qkv, qkv, qkv, qkv, bias, dot, ot)


def _split3(x):
    h1 = x.astype(BF16)
    r1 = x - h1.astype(F32)
    h2 = r1.astype(BF16)
    h3 = (r1 - h2.astype(F32)).astype(BF16)
    return h1, h2, h3


def _bias_expand(rel_t, onehot):
    Hq, NB = rel_t.shape
    L = onehot.shape[1]

    def body(r_ref, oh_ref, o_ref):
        h1, h2, h3 = _split3(r_ref[...])
        oh = oh_ref[...]
        o_ref[...] = _nn(h1, oh) + _nn(h2, oh) + _nn(h3, oh)

    return pl.pallas_call(
        body, name="bias_expand", grid=(1,),
        in_specs=[pl.BlockSpec((Hq, NB), lambda i: (0, 0)), pl.BlockSpec((NB, L), lambda i: (0, 0))],
        out_specs=pl.BlockSpec((Hq, L), lambda i: (0, 0)),
        out_shape=jax.ShapeDtypeStruct((Hq, L), F32),
        compiler_params=_cp("arbitrary"),
    )(rel_t, onehot)


def _bias_reduce(dbias, onehot):
    Hq, L = dbias.shape
    NB = onehot.shape[0]

    def body(d_ref, oh_ref, o_ref):
        h1, h2, h3 = _split3(d_ref[...])
        oh = oh_ref[...]
        o_ref[...] = _nt(h1, oh) + _nt(h2, oh) + _nt(h3, oh)

    return pl.pallas_call(
        body, name="bias_reduce", grid=(1,),
        in_specs=[pl.BlockSpec((Hq, L), lambda i: (0, 0)), pl.BlockSpec((NB, L), lambda i: (0, 0))],
        out_specs=pl.BlockSpec((Hq, NB), lambda i: (0, 0)),
        out_shape=jax.ShapeDtypeStruct((Hq, NB), F32),
        compiler_params=_cp("arbitrary"),
    )(dbias, onehot)


def _adamw(w, g, m, v, name):
    R, C = w.shape
    tr = 256 if R % 256 == 0 else R
    bc1 = 1.0 - ADAM_B1 ** ADAM_STEP
    bc2 = 1.0 - ADAM_B2 ** ADAM_STEP

    def body(w_ref, g_ref, m_ref, v_ref, d_ref, nm_ref, nv_ref):
        g = g_ref[...]
        m2 = ADAM_B1 * m_ref[...] + (1.0 - ADAM_B1) * g
        v2 = ADAM_B2 * v_ref[...] + (1.0 - ADAM_B2) * (g * g)
        nm_ref[...] = m2
        nv_ref[...] = v2
        d_ref[...] = -ADAM_LR * ((m2 / bc1) / (jnp.sqrt(v2 / bc2) + ADAM_EPS) + ADAM_WD * w_ref[...])

    spec = pl.BlockSpec((tr, C), lambda i: (i, 0))
    return pl.pallas_call(
        body, name=name, grid=(R // tr,),
        in_specs=[spec] * 4, out_specs=[spec] * 3,
        out_shape=[jax.ShapeDtypeStruct((R, C), F32)] * 3,
        compiler_params=_cp("arbitrary"),
    )(w, g, m, v)


def _row_tile(R):
    return max(t for t in range(16, 513, 16) if R % t == 0)


def _add_halves(mine, recv, name):
    K, R, C = mine.shape
    tr = _row_tile(R)

    def body(a_ref, b_ref, o_ref, ob_ref):
        s = a_ref[...].astype(F32) + b_ref[...].astype(F32)
        o_ref[...] = s
        ob_ref[...] = s.astype(BF16)

    spec = pl.BlockSpec((None, tr, C), lambda k, i: (k, i, 0))
    return pl.pallas_call(
        body, name=name, grid=(K, R // tr),
        in_specs=[spec, spec], out_specs=[spec, spec],
        out_shape=[jax.ShapeDtypeStruct((K, R, C), F32), jax.ShapeDtypeStruct((K, R, C), BF16)],
        compiler_params=_cp("arbitrary", "arbitrary"),
    )(mine, recv)


def _add_received(own, recv, name):
    R, C = own.shape
    tr = _row_tile(R)

    def body(a_ref, r_ref, o_ref):
        o_ref[...] = ((a_ref[...] + r_ref[0].astype(F32)) + r_ref[1].astype(F32)) + r_ref[2].astype(F32)

    return pl.pallas_call(
        body, name=name, grid=(R // tr,),
        in_specs=[pl.BlockSpec((tr, C), lambda i: (i, 0)), pl.BlockSpec((3, tr, C), lambda i: (0, i, 0))],
        out_specs=pl.BlockSpec((tr, C), lambda i: (i, 0)),
        out_shape=jax.ShapeDtypeStruct((R, C), F32),
        compiler_params=_cp("arbitrary"),
    )(own, recv)


def _position():
    x, y, c = lax.axis_index("x"), lax.axis_index("y"), lax.axis_index("c")
    others = [(1 - x, y), (x, 1 - y), (1 - x, 1 - y)]
    return x, y, c, others


def _remote(src, dst, send_sems, recv_sems, k, dev):
    return pltpu.make_async_remote_copy(src_ref=src, dst_ref=dst, send_sem=send_sems.at[k],
                                        recv_sem=recv_sems.at[k], device_id=dev, device_id_type=MESH_ID)


class _gather_exchange:
    def __init__(self, src, out, send_sems, recv_sems):
        x, y, c, others = _position()
        mine, sibling = 2 * x + y, (x, y, 1 - c)
        self.sends, self.arrivals, self.passes, self.from_sibling = [], [], [], []
        for j, (ox, oy) in enumerate(others):
            slot = out.at[2 * ox + oy, c]
            theirs = out.at[2 * ox + oy, 1 - c]
            self.sends.append(_remote(src.at[c], out.at[mine, c], send_sems, recv_sems, j, (ox, oy, c)))
            self.arrivals.append(_remote(slot, slot, send_sems, recv_sems, j, (ox, oy, c)))
            self.passes.append(_remote(slot, slot, send_sems, recv_sems, 3 + j, sibling))
            self.from_sibling.append(_remote(theirs, theirs, send_sems, recv_sems, 3 + j, sibling))

    def start(self):
        for cp in self.sends:
            cp.start()

    def forward(self):
        for arrived, onward in zip(self.arrivals, self.passes):
            arrived.wait_recv()
            onward.start()

    def finish(self):
        for cp in self.from_sibling:
            cp.wait_recv()
        for cp in self.sends + self.passes:
            cp.wait_send()


def _gather_weights(shard):
    R, C = shard.shape
    half = R // 2

    def body(src, out, send_sems, recv_sems):
        ex = _gather_exchange(src, out, send_sems, recv_sems)
        ex.start()
        ex.forward()
        ex.finish()

    return pl.pallas_call(
        body, name="gather_weights",
        in_specs=[ANY], out_specs=ANY,
        out_shape=jax.ShapeDtypeStruct((N_CHIPS, 2, half, C), shard.dtype),
        scratch_shapes=[pltpu.SemaphoreType.DMA((6,)), pltpu.SemaphoreType.DMA((6,))],
    )(shard.reshape(2, half, C)).reshape(N_CHIPS, R, C)


def _swap_halves(grads, name):
    K, R, C = grads.shape
    half = R // 2

    def body(src, out, send_sems, recv_sems):
        x, y, c, _ = _position()
        theirs = src.at[:, pl.ds(pl.multiple_of((1 - c) * half, 16), half), :]
        cp = _remote(theirs, out, send_sems, recv_sems, 0, (x, y, 1 - c))
        cp.start()
        cp.wait()

    return pl.pallas_call(
        body, name=name,
        in_specs=[ANY], out_specs=ANY,
        out_shape=jax.ShapeDtypeStruct((K, half, C), grads.dtype),
        scratch_shapes=[pltpu.SemaphoreType.DMA((1,)), pltpu.SemaphoreType.DMA((1,))],
    )(grads)


class _scatter_exchange:
    def __init__(self, src, out, send_sems, recv_sems):
        x, y, c, others = _position()
        self.copies = [_remote(src.at[2 * ox + oy], out.at[j], send_sems, recv_sems, j, (ox, oy, c))
                       for j, (ox, oy) in enumerate(others)]

    def start(self):
        for cp in self.copies:
            cp.start()

    def finish(self):
        for cp in self.copies:
            cp.wait()


def _scatter_to_owners(parts, name):
    K, H, C = parts.shape

    def body(src, out, send_sems, recv_sems):
        ex = _scatter_exchange(src, out, send_sems, recv_sems)
        ex.start()
        ex.finish()

    return pl.pallas_call(
        body, name=name,
        in_specs=[ANY], out_specs=ANY,
        out_shape=jax.ShapeDtypeStruct((3, H, C), parts.dtype),
        scratch_shapes=[pltpu.SemaphoreType.DMA((3,)), pltpu.SemaphoreType.DMA((3,))],
    )(parts)


def _swap_reduced(half_rows, name):
    H, C = half_rows.shape

    def body(src, out, send_sems, recv_sems):
        x, y, c, _ = _position()
        cp = _remote(src, out, send_sems, recv_sems, 0, (x, y, 1 - c))
        cp.start()
        cp.wait()

    return pl.pallas_call(
        body, name=name,
        in_specs=[ANY], out_specs=ANY,
        out_shape=jax.ShapeDtypeStruct((H, C), half_rows.dtype),
        scratch_shapes=[pltpu.SemaphoreType.DMA((1,)), pltpu.SemaphoreType.DMA((1,))],
    )(half_rows)


def _allreduce_small(block):
    R, C = block.shape
    n_dev = 8

    def body(src, out, slots, send_sems, recv_sems):
        x, y, c, _ = _position()
        me = 4 * x + 2 * y + c
        slots[me] = src[...]
        sends = []
        for r in range(1, n_dev):
            peer = (x ^ (r >> 2), y ^ ((r >> 1) & 1), c ^ (r & 1))
            cp = _remote(src, slots.at[me], send_sems, recv_sems, r - 1, peer)
            cp.start()
            sends.append(cp)
        for r in range(1, n_dev):
            theirs = slots.at[me ^ r]
            _remote(theirs, theirs, send_sems, recv_sems, r - 1, (x, y, c)).wait_recv()
        for cp in sends:
            cp.wait_send()
        acc = slots[0]
        for d in range(1, n_dev):
            acc = acc + slots[d]
        out[...] = acc

    return pl.pallas_call(
        body, name="allreduce_small",
        in_specs=[pl.BlockSpec(memory_space=pltpu.VMEM)], out_specs=pl.BlockSpec(memory_space=pltpu.VMEM),
        out_shape=jax.ShapeDtypeStruct((R, C), F32),
        scratch_shapes=[pltpu.VMEM((n_dev, R, C), F32), pltpu.SemaphoreType.DMA((7,)), pltpu.SemaphoreType.DMA((7,))],
    )(block)


def _rel_bucket(dist):
    max_exact = REL_BUCKETS // 2
    d = jnp.maximum(dist, 1).astype(F32)
    large = max_exact + (jnp.log(d / max_exact) / math.log(REL_MAX_DIST / max_exact)
                         * (REL_BUCKETS - max_exact)).astype(jnp.int32)
    large = jnp.minimum(large, REL_BUCKETS - 1)
    return jnp.where(dist < max_exact, dist, large)


def _bucket_onehot():
    T = SWA_BLOCK
    dist = (jnp.arange(T)[:, None] + T) - jnp.arange(2 * T)[None, :]
    bucket = _rel_bucket(jnp.maximum(dist, 0)).reshape(1, T * 2 * T)
    return (bucket == jnp.arange(REL_BUCKETS)[:, None]).astype(BF16)


_BUF = (("ffn1_w1", "t"), ("ffn1_w3", "t"), ("ffn1_w2", "n"), ("ffn2_w1", "t"), ("ffn2_w3", "t"),
        ("ffn2_w2", "n"), ("w_in", "t"), ("w_out", "n"), ("w_branch_swa", "tw"), ("w_branch_sb", "tw"))


def _to_rows(name_kind, w, D):
    kind = name_kind[1]
    if kind == "n":
        return w
    if kind == "t":
        return w.T
    return w.T.reshape(-1, D)


def _from_rows(name_kind, rows, width):
    kind = name_kind[1]
    if kind == "n":
        return rows
    if kind == "t":
        return rows.T
    return rows.reshape(-1, width).T


def kernel(x, norm_ffn1, ffn1_w1, ffn1_w3, ffn1_w2, norm_mix, w_in, swa_sinks, rel_bias, w_branch_swa, w_branch_sb, w_out, norm_ffn2, ffn2_w1, ffn2_w3, ffn2_w2, norm_final, loss_target, m_norm_ffn1, m_ffn1_w1, m_ffn1_w3, m_ffn1_w2, m_norm_mix, m_w_in, m_swa_sinks, m_rel_bias, m_w_branch_swa, m_w_branch_sb, m_w_out, m_norm_ffn2, m_ffn2_w1, m_ffn2_w3, m_ffn2_w2, m_norm_final, v_norm_ffn1, v_ffn1_w1, v_ffn1_w3, v_ffn1_w2, v_norm_mix, v_w_in, v_swa_sinks, v_rel_bias, v_w_branch_swa, v_w_branch_sb, v_w_out, v_norm_ffn2, v_ffn2_w1, v_ffn2_w3, v_ffn2_w2, v_norm_final):
    names = ["norm_ffn1", "ffn1_w1", "ffn1_w3", "ffn1_w2", "norm_mix", "w_in", "swa_sinks", "rel_bias",
             "w_branch_swa", "w_branch_sb", "w_out", "norm_ffn2", "ffn2_w1", "ffn2_w3", "ffn2_w2", "norm_final"]
    W = dict(zip(names, [norm_ffn1, ffn1_w1, ffn1_w3, ffn1_w2, norm_mix, w_in, swa_sinks, rel_bias,
                         w_branch_swa, w_branch_sb, w_out, norm_ffn2, ffn2_w1, ffn2_w3, ffn2_w2, norm_final]))
    M = dict(zip(names, [m_norm_ffn1, m_ffn1_w1, m_ffn1_w3, m_ffn1_w2, m_norm_mix, m_w_in, m_swa_sinks, m_rel_bias,
                         m_w_branch_swa, m_w_branch_sb, m_w_out, m_norm_ffn2, m_ffn2_w1, m_ffn2_w3, m_ffn2_w2,
                         m_norm_final]))
    V = dict(zip(names, [v_norm_ffn1, v_ffn1_w1, v_ffn1_w3, v_ffn1_w2, v_norm_mix, v_w_in, v_swa_sinks, v_rel_bias,
                         v_w_branch_swa, v_w_branch_sb, v_w_out, v_norm_ffn2, v_ffn2_w1, v_ffn2_w3, v_ffn2_w2,
                         v_norm_final]))
    xs = x[0]
    target = loss_target[0]
    S, D = xs.shape
    QW = SWA_Q_HEADS * HEAD_DIM
    KW = SWA_KV_HEADS * HEAD_DIM
    BW = SB_HEADS * HEAD_DIM
    QKV = QW + 2 * KW + 3 * BW

    pieces = [_to_rows(nk, W[nk[0]][0], D) for nk in _BUF]
    sizes = [p.shape[0] for p in pieces]
    offs = [0]
    for s in sizes:
        offs.append(offs[-1] + s)
    n_first = 3
    first_rows = offs[n_first]
    shard_a = jnp.concatenate(pieces[:n_first], axis=0).astype(BF16)
    shard_b = jnp.concatenate(pieces[n_first:], axis=0).astype(BF16)
    chip = 2 * lax.axis_index("x") + lax.axis_index("y")
    gathered_a = lax.dynamic_update_slice(_gather_weights(shard_a), shard_a[None], (chip, 0, 0))
    f1w1, f1w3, f1w2 = [gathered_a[:, offs[i]:offs[i + 1], :].reshape(N_CHIPS * sizes[i], D) for i in range(n_first)]

    g1, gmix, g3 = W["norm_ffn1"], W["norm_mix"], W["norm_ffn2"]
    gf = W["norm_final"].reshape(1, D)

    x1, h1, a1, b1, gathered_b = _ffn_fwd(xs, g1, f1w1, f1w3, f1w2, "ffn1_fwd", gather=shard_b)
    gathered_b = lax.dynamic_update_slice(gathered_b, shard_b[None], (chip, 0, 0))

    def full(i):
        return gathered_b[:, offs[i] - first_rows:offs[i + 1] - first_rows, :].reshape(N_CHIPS * sizes[i], D)

    f2w1, f2w3, f2w2, w_in_t, w_out_f = [full(i) for i in range(n_first, 8)]
    wa_t = full(8).reshape(D, QW)
    wb_t = full(9).reshape(D, BW)
    o0 = QW + 2 * KW
    rows = jnp.arange(w_in_t.shape[0])
    is_q = (rows < QW) | ((rows >= o0) & (rows < o0 + BW))
    w_in_s = w_in_t * jnp.where(is_q, QK_SCALE, 1.0).astype(BF16)[:, None]
    qkv_a = _norm_proj_heads(x1, gmix, w_in_s[:o0], SWA_BLOCK, "proj_swa")
    qkv_b = _norm_proj_heads(x1, gmix, w_in_s[o0:QKV], SB_BLOCK, "proj_sb")
    gates, h2 = _norm_matmul_nt(x1, gmix, w_in_t[QKV:], F32, "proj_gates")

    onehot = _bucket_onehot()
    bias = _bias_expand(W["rel_bias"].T, onehot).reshape(SWA_Q_HEADS, SWA_BLOCK, 2 * SWA_BLOCK)
    sinks = W["swa_sinks"].reshape(SWA_Q_HEADS)
    oa_t = _swa_fwd(qkv_a, bias, sinks)
    ob_t, saved_sb = _sb_fwd(qkv_b)

    x2, merged, ba, bb = _merge_fwd(x1, gates, oa_t, ob_t, wa_t, wb_t, w_out_f)
    x3, h3, a2, b2 = _ffn_fwd(x2, g3, f2w1, f2w3, f2w2, "ffn2_fwd")
    loss_part, dx3, dgf = _final_loss(x3, gf, target)

    dx2, dg3, dz2, da2, db2, u2 = _ffn_bwd(dx3, x2, g3, a2, b2, f2w1, f2w3, f2w2, "ffn2_bwd")
    grads = {}
    grads["ffn2_w1"] = _tn_matmul(da2, h3, "ffn2_dw1")
    grads["ffn2_w3"] = _tn_matmul(db2, h3, "ffn2_dw3")
    grads["ffn2_w2"] = _tn_matmul(u2, dz2, "ffn2_dw2")

    dx2b, dba, dbb, dgates, doa_t, dob_t = _merge_bwd(dx2, gates, ba, bb, wa_t.T, wb_t.T, w_out_f,
                                                      SWA_BLOCK, SB_BLOCK)
    grads["w_out"] = _tn_matmul(merged, dx2b, "dw_out")
    grads["w_branch_swa"] = _heads_matmul([(oa_t, 1.0)], dba, "dw_branch_swa").T
    grads["w_branch_sb"] = _heads_matmul([(ob_t, 1.0)], dbb, "dw_branch_sb").T

    dqb_t, dkb_t, dvb_t = _sb_bwd(qkv_b, dob_t, saved_sb)
    dqa_t, dka_t, dva_t, dbias, dsink_rows = _swa_bwd(qkv_a, bias, sinks, doa_t, oa_t)
    d_rel = _bias_reduce(dbias.reshape(SWA_Q_HEADS, -1), onehot).T
    d_sinks = jnp.sum(dsink_rows, axis=(1, 2))

    dheads = [(dqa_t, QK_SCALE), (dka_t, 1.0), (dva_t, 1.0), (dqb_t, QK_SCALE), (dkb_t, 1.0), (dvb_t, 1.0)]
    grads["w_in"] = jnp.concatenate([_heads_matmul(dheads, h2, "dw_in_heads").astype(BF16),
                                     _tn_matmul(dgates, h2, "dw_in_gates")], axis=0)
    row0, pieces_in = 0, []
    for a, _ in dheads:
        pieces_in.append((a, row0))
        row0 += a.shape[0] * HEAD_DIM
    dx1, dgmix = _proj_bwd(pieces_in, dgates, w_in_s, x1, gmix, dx2)

    c = lax.axis_index("c")

    def reduce_start(lo, hi, tag):
        gbuf = jnp.concatenate([grads[_BUF[i][0]].astype(BF16).reshape(N_CHIPS, sizes[i], D) for i in range(lo, hi)],
                               axis=1)
        half = gbuf.shape[1] // 2
        from_sibling = _swap_halves(gbuf, "swap_halves_" + tag)
        my_half = lax.dynamic_slice_in_dim(gbuf, c * half, half, axis=1)
        return _add_halves(my_half, from_sibling, "add_sibling_" + tag)

    def reduce_finish(chip_sum, received, tag):
        own = lax.dynamic_index_in_dim(chip_sum, chip, axis=0, keepdims=False)
        my_rows = _add_received(own, received, "add_chips_" + tag)
        their_rows = _swap_reduced(my_rows, "swap_reduced_" + tag)
        return jnp.concatenate([jnp.where(c == 0, my_rows, their_rows), jnp.where(c == 0, their_rows, my_rows)],
                               axis=0)

    sum_b, sum16_b = reduce_start(n_first, len(_BUF), "late")
    dx0, dg1, dz1, da1, db1, u1, received_b = _ffn_bwd(dx1, xs, g1, a1, b1, f1w1, f1w3, f1w2, "ffn1_bwd",
                                                       scatter=sum16_b)
    grads["ffn1_w1"] = _tn_matmul(da1, h1, "ffn1_dw1")
    grads["ffn1_w3"] = _tn_matmul(db1, h1, "ffn1_dw3")
    grads["ffn1_w2"] = _tn_matmul(u1, dz1, "ffn1_dw2")
    sum_a, sum16_a = reduce_start(0, n_first, "first")
    reduced = jnp.concatenate([reduce_finish(sum_a, _scatter_to_owners(sum16_a, "scatter_to_owners"), "first"),
                               reduce_finish(sum_b, received_b, "late")], axis=0)

    small_rows = [dg1, dgmix, dg3, dgf,
                  jnp.pad(d_sinks.reshape(1, -1), ((0, 0), (0, D - SWA_Q_HEADS))),
                  jnp.pad(d_rel.reshape(1, -1), ((0, 0), (0, D - REL_BUCKETS * SWA_Q_HEADS))),
                  jnp.pad(loss_part, ((0, 0), (0, D - 1))), jnp.zeros((1, D), F32)]
    small = _allreduce_small(jnp.concatenate(small_rows, axis=0))
    loss = small[6, 0]

    G = {}
    for i, nk in enumerate(_BUF):
        G[nk[0]] = _from_rows(nk, reduced[offs[i]:offs[i + 1]], W[nk[0]].shape[1])[None]
    G["norm_ffn1"], G["norm_mix"], G["norm_ffn2"] = small[0:1], small[1:2], small[2:3]
    G["norm_final"] = small[3]
    G["swa_sinks"] = small[4:5, :SWA_Q_HEADS]
    G["rel_bias"] = small[5, :REL_BUCKETS * SWA_Q_HEADS].reshape(REL_BUCKETS, SWA_Q_HEADS)

    delta, new_m, new_v = {}, {}, {}
    small_names = ["norm_ffn1", "norm_mix", "norm_ffn2", "norm_final", "swa_sinks", "rel_bias"]

    def pack(d):
        return jnp.concatenate([jnp.pad(d[n].reshape(1, -1), ((0, 0), (0, D - d[n].size))) for n in small_names]
                               + [jnp.zeros((2, D), F32)], axis=0)

    sd, sm, sv = _adamw(pack(W), pack(G), pack(M), pack(V), "adamw_small")
    for r, n in enumerate(small_names):
        for dst, src in ((delta, sd), (new_m, sm), (new_v, sv)):
            dst[n] = src[r, :W[n].size].reshape(W[n].shape)
    for nk in _BUF:
        n = nk[0]
        shp = W[n].shape
        two_d = (shp[1], shp[2])
        d_, m_, v_ = _adamw(W[n].reshape(two_d), G[n].reshape(two_d), M[n].reshape(two_d), V[n].reshape(two_d),
                            "adamw_" + n)
        delta[n], new_m[n], new_v[n] = d_.reshape(shp), m_.reshape(shp), v_.reshape(shp)

    return (loss, dx0[None], *[G[n] for n in names], *[delta[n] for n in names],
            *[new_m[n] for n in names], *[new_v[n] for n in names])
```

```python
import functools
import math

import jax
import jax.numpy as jnp
from jax import lax
from jax.experimental import pallas as pl
from jax.experimental.pallas import tpu as pltpu

F32, BF16 = jnp.float32, jnp.bfloat16
MESH_ID = pl.DeviceIdType.MESH
ANY = pl.BlockSpec(memory_space=pl.ANY)

RMS_EPS = 1e-6
HEAD_DIM = 64
SWA_Q_HEADS, SWA_KV_HEADS, SWA_GROUP = 8, 2, 4
SWA_BLOCK = 128
SB_HEADS = 8
SB_BLOCK = 256
REL_BUCKETS, REL_MAX_DIST = 32, 128
NEG_BIG = -1e30
QK_SCALE = HEAD_DIM ** -0.5
ADAM_LR, ADAM_B1, ADAM_B2, ADAM_EPS, ADAM_WD, ADAM_STEP = 0.001, 0.9, 0.999, 1e-08, 0.01, 10

N_CHIPS = 4
TOKEN_TILE = 512
MATMUL_TOKEN_TILE = 1024
WGRAD_ROW_TILES = (2176, 1408, 1024, 256)
FF_TILE = 1408
FFN_TOKEN_TILE = 512
FF_BWD_TILE = 256
VMEM_LIMIT = 56 * 1024 * 1024


def _cp(*sem):
    return pltpu.CompilerParams(dimension_semantics=sem, vmem_limit_bytes=VMEM_LIMIT)


def _nn(a, b):
    return jnp.dot(a, b, preferred_element_type=F32)


def _nt(a, b):
    return lax.dot_general(a, b, (((1,), (1,)), ((), ())), preferred_element_type=F32)


def _tn(a, b):
    return lax.dot_general(a, b, (((0,), (0,)), ((), ())), preferred_element_type=F32)


def _norm_fwd(x, g):
    return x * lax.rsqrt(jnp.mean(x * x, axis=-1, keepdims=True) + RMS_EPS) * g


def _norm_bwd(x, g, dh):
    r = lax.rsqrt(jnp.mean(x * x, axis=-1, keepdims=True) + RMS_EPS)
    xh = x * r
    dxh = dh * g
    dx = r * (dxh - xh * jnp.mean(dxh * xh, axis=-1, keepdims=True))
    return dx, jnp.sum(dh * xh, axis=0, keepdims=True)


SOFTPLUS_LINEAR = 20.0


def _softplus(z):
    return jnp.maximum(jnp.log(1.0 + jnp.exp(jnp.minimum(z, SOFTPLUS_LINEAR))), z)


def _ffn_fwd(x, g, w1t, w3t, w2, name, gather=None):
    S, D = x.shape
    F = w2.shape[0]
    tm, tf = min(FFN_TOKEN_TILE, S), FF_TILE
    ni, nj = S // tm, F // tf

    def body(x_ref, g_ref, w1_ref, w3_ref, w2_ref, *rest):
        if gather is None:
            xo_ref, h_ref, a_ref, b_ref, hs, acc = rest
        else:
            shard_ref, xo_ref, h_ref, a_ref, b_ref, gathered_ref, hs, acc, send_sems, recv_sems = rest
        i, j = pl.program_id(0), pl.program_id(1)
        if gather is not None:
            for when, phase in ((jnp.logical_and(i == 0, j == 0), "start"),
                                (jnp.logical_and(i == ni - 1, j == 0), "forward"),
                                (jnp.logical_and(i == ni - 1, j == nj - 1), "finish")):
                @pl.when(when)
                def _():
                    getattr(_gather_exchange(shard_ref, gathered_ref, send_sems, recv_sems), phase)()

        @pl.when(j == 0)
        def _():
            hb = _norm_fwd(x_ref[...], g_ref[...]).astype(BF16)
            hs[...] = hb
            h_ref[...] = hb
            acc[...] = jnp.zeros_like(acc)

        h = hs[...]
        a = _nt(h, w1_ref[...])
        b = _nt(h, w3_ref[...])
        a_ref[...] = a.astype(BF16)
        b_ref[...] = b.astype(BF16)
        u = a * jax.nn.sigmoid(a) * b
        acc[...] += _nn(u.astype(BF16), w2_ref[...])

        @pl.when(j == nj - 1)
        def _():
            xo_ref[...] = x_ref[...] + 0.5 * acc[...]

    in_specs = [pl.BlockSpec((tm, D), lambda i, j: (i, 0)),
                pl.BlockSpec((1, D), lambda i, j: (0, 0)),
                pl.BlockSpec((tf, D), lambda i, j: (j, 0)),
                pl.BlockSpec((tf, D), lambda i, j: (j, 0)),
                pl.BlockSpec((tf, D), lambda i, j: (j, 0))]
    out_specs = [pl.BlockSpec((tm, D), lambda i, j: (i, 0)),
                 pl.BlockSpec((tm, D), lambda i, j: (i, 0)),
                 pl.BlockSpec((tm, tf), lambda i, j: (i, j)),
                 pl.BlockSpec((tm, tf), lambda i, j: (i, j))]
    out_shape = [jax.ShapeDtypeStruct((S, D), F32), jax.ShapeDtypeStruct((S, D), BF16),
                 jax.ShapeDtypeStruct((S, F), BF16), jax.ShapeDtypeStruct((S, F), BF16)]
    scratch = [pltpu.VMEM((tm, D), BF16), pltpu.VMEM((tm, D), F32)]
    operands = [x, g, w1t, w3t, w2]
    if gather is not None:
        R, C = gather.shape
        in_specs.append(ANY)
        out_specs.append(ANY)
        out_shape.append(jax.ShapeDtypeStruct((N_CHIPS, 2, R // 2, C), gather.dtype))
        scratch += [pltpu.SemaphoreType.DMA((6,)), pltpu.SemaphoreType.DMA((6,))]
        operands.append(gather.reshape(2, R // 2, C))
    outs = list(pl.pallas_call(
        body, name=name, grid=(ni, nj), in_specs=in_specs, out_specs=out_specs, out_shape=out_shape,
        scratch_shapes=scratch, compiler_params=_cp("arbitrary", "arbitrary"),
    )(*operands))
    if gather is not None:
        outs[4] = outs[4].reshape(N_CHIPS, R, C)
    return outs


def _ffn_bwd(dxo, x, g, a, b, w1t, w3t, w2, name, scatter=None):
    S, D = x.shape
    F = w2.shape[0]
    tm, tf = min(MATMUL_TOKEN_TILE, S), FF_BWD_TILE
    ni, nj = S // tm, F // tf

    def body(dxo_ref, x_ref, g_ref, a_ref, b_ref, w1_ref, w3_ref, w2_ref, *rest):
        if scatter is None:
            dx_ref, dg_ref, dz_ref, da_ref, db_ref, u_ref, dzs, acc = rest
        else:
            (parts_ref, dx_ref, dg_ref, dz_ref, da_ref, db_ref, u_ref, recv_ref,
             dzs, acc, send_sems, recv_sems) = rest
        i, j = pl.program_id(0), pl.program_id(1)
        if scatter is not None:
            for when, phase in ((jnp.logical_and(i == 0, j == 0), "start"),
                                (jnp.logical_and(i == ni - 1, j == nj - 1), "finish")):
                @pl.when(when)
                def _():
                    getattr(_scatter_exchange(parts_ref, recv_ref, send_sems, recv_sems), phase)()

        @pl.when(j == 0)
        def _():
            dzb = (0.5 * dxo_ref[...]).astype(BF16)
            dzs[...] = dzb
            dz_ref[...] = dzb
            acc[...] = jnp.zeros_like(acc)

        du = _nt(dzs[...], w2_ref[...])
        av = a_ref[...].astype(F32)
        bv = b_ref[...].astype(F32)
        s = jax.nn.sigmoid(av)
        silu = av * s
        db = (du * silu).astype(BF16)
        da = (du * bv * (s * (1.0 + av * (1.0 - s)))).astype(BF16)
        da_ref[...] = da
        db_ref[...] = db
        u_ref[...] = (silu * bv).astype(BF16)
        acc[...] += _nn(da, w1_ref[...]) + _nn(db, w3_ref[...])

        @pl.when(j == nj - 1)
        def _():
            dx, dg = _norm_bwd(x_ref[...], g_ref[...], acc[...])
            dx_ref[...] = dxo_ref[...] + dx

            @pl.when(i == 0)
            def _():
                dg_ref[...] = dg

            @pl.when(i > 0)
            def _():
                dg_ref[...] += dg

    row = pl.BlockSpec((tm, D), lambda i, j: (i, 0))
    wsp = pl.BlockSpec((tf, D), lambda i, j: (j, 0))
    col = pl.BlockSpec((tm, tf), lambda i, j: (i, j))
    vec = pl.BlockSpec((1, D), lambda i, j: (0, 0))
    in_specs = [row, row, vec, col, col, wsp, wsp, wsp]
    out_specs = [row, vec, row, col, col, col]
    out_shape = [jax.ShapeDtypeStruct((S, D), F32), jax.ShapeDtypeStruct((1, D), F32),
                 jax.ShapeDtypeStruct((S, D), BF16), jax.ShapeDtypeStruct((S, F), BF16),
                 jax.ShapeDtypeStruct((S, F), BF16), jax.ShapeDtypeStruct((S, F), BF16)]
    scratch = [pltpu.VMEM((tm, D), BF16), pltpu.VMEM((tm, D), F32)]
    operands = [dxo, x, g, a, b, w1t, w3t, w2]
    if scatter is not None:
        in_specs.append(ANY)
        out_specs.append(ANY)
        out_shape.append(jax.ShapeDtypeStruct((3,) + scatter.shape[1:], scatter.dtype))
        scratch += [pltpu.SemaphoreType.DMA((3,)), pltpu.SemaphoreType.DMA((3,))]
        operands.append(scatter)
    return pl.pallas_call(
        body, name=name, grid=(ni, nj), in_specs=in_specs, out_specs=out_specs, out_shape=out_shape,
        scratch_shapes=scratch, compiler_params=_cp("arbitrary", "arbitrary"),
    )(*operands)


def _tn_matmul(a, b, name):
    S, M = a.shape
    N = b.shape[1]
    ts = min(MATMUL_TOKEN_TILE, S)
    tmm = next(t for t in WGRAD_ROW_TILES if M % t == 0)
    ns = S // ts

    def body(a_ref, b_ref, o_ref, acc):
        s = pl.program_id(1)
        part = _tn(a_ref[...], b_ref[...])

        @pl.when(s == 0)
        def _():
            acc[...] = part

        @pl.when(s > 0)
        def _():
            acc[...] += part

        @pl.when(s == ns - 1)
        def _():
            o_ref[...] = acc[...].astype(BF16)

    return pl.pallas_call(
        body, name=name, grid=(M // tmm, ns),
        in_specs=[pl.BlockSpec((ts, tmm), lambda m, s: (s, m)),
                  pl.BlockSpec((ts, N), lambda m, s: (s, 0))],
        out_specs=pl.BlockSpec((tmm, N), lambda m, s: (m, 0)),
        out_shape=jax.ShapeDtypeStruct((M, N), BF16),
        scratch_shapes=[pltpu.VMEM((tmm, N), F32)],
        compiler_params=_cp("arbitrary", "arbitrary"),
    )(a, b)


def _norm_matmul_nt(x, g, wt, out_dtype, name):
    S, D = x.shape
    N = wt.shape[0]
    tm = min(MATMUL_TOKEN_TILE, S)
    tn = next(t for t in (1024, 768, 256) if N % t == 0)

    def body(x_ref, g_ref, w_ref, o_ref, h_ref, hs):
        @pl.when(pl.program_id(1) == 0)
        def _():
            hb = _norm_fwd(x_ref[...], g_ref[...]).astype(BF16)
            hs[...] = hb
            h_ref[...] = hb

        o_ref[...] = _nt(hs[...], w_ref[...]).astype(out_dtype)

    return pl.pallas_call(
        body, name=name, grid=(S // tm, N // tn),
        in_specs=[pl.BlockSpec((tm, D), lambda i, j: (i, 0)),
                  pl.BlockSpec((1, D), lambda i, j: (0, 0)),
                  pl.BlockSpec((tn, D), lambda i, j: (j, 0))],
        out_specs=[pl.BlockSpec((tm, tn), lambda i, j: (i, j)),
                   pl.BlockSpec((tm, D), lambda i, j: (i, 0))],
        out_shape=[jax.ShapeDtypeStruct((S, N), out_dtype), jax.ShapeDtypeStruct((S, D), BF16)],
        scratch_shapes=[pltpu.VMEM((tm, D), BF16)],
        compiler_params=_cp("arbitrary", "arbitrary"),
    )(x, g, wt)


def _heads_tile(ref):
    Hh, nbk = ref.shape[0], ref.shape[1]
    return jnp.concatenate([jnp.concatenate([ref[h, b] for b in range(nbk)], axis=1) for h in range(Hh)], axis=0)


def _store_heads(ref, val):
    Hh, nbk, dh, T = ref.shape
    for h in range(Hh):
        for b in range(nbk):
            ref[h, b] = val[h * dh:(h + 1) * dh, b * T:(b + 1) * T].astype(ref.dtype)


def _norm_proj_heads(x, g, w_rows, T, name):
    S, D = x.shape
    N = w_rows.shape[0]
    tm, tn = min(MATMUL_TOKEN_TILE, S), 768

    def body(x_ref, g_ref, w_ref, o_ref, hs):
        @pl.when(pl.program_id(1) == 0)
        def _():
            hs[...] = _norm_fwd(x_ref[...], g_ref[...]).astype(BF16)

        _store_heads(o_ref, _nt(w_ref[...], hs[...]))

    return pl.pallas_call(
        body, name=name, grid=(S // tm, N // tn),
        in_specs=[pl.BlockSpec((tm, D), lambda i, j: (i, 0)),
                  pl.BlockSpec((1, D), lambda i, j: (0, 0)),
                  pl.BlockSpec((tn, D), lambda i, j: (j, 0))],
        out_specs=pl.BlockSpec((tn // HEAD_DIM, tm // T, HEAD_DIM, T), lambda i, j: (j, i, 0, 0)),
        out_shape=jax.ShapeDtypeStruct((N // HEAD_DIM, S // T, HEAD_DIM, T), BF16),
        scratch_shapes=[pltpu.VMEM((tm, D), BF16)],
        compiler_params=_cp("arbitrary", "arbitrary"),
    )(x, g, w_rows)


def _heads_matmul(pieces, b, name):
    S, N = b.shape
    ts = min(MATMUL_TOKEN_TILE, S)
    ns = S // ts
    rows = [at.shape[0] * at.shape[2] for at, _ in pieces]

    def body(*refs):
        a_refs, b_ref, o_ref = refs[:-2], refs[-2], refs[-1]
        s = pl.program_id(0)
        row0 = 0
        for a_ref, (_, scale), n in zip(a_refs, pieces, rows):
            a = _heads_tile(a_ref)
            part = _nn((a if scale == 1.0 else a * scale).astype(BF16), b_ref[...])
            out = o_ref.at[row0:row0 + n, :]
            row0 += n

            @pl.when(s == 0)
            def _():
                out[...] = part

            @pl.when(s > 0)
            def _():
                out[...] += part

    return pl.pallas_call(
        body, name=name, grid=(ns,),
        in_specs=[pl.BlockSpec((at.shape[0], ts // at.shape[3], at.shape[2], at.shape[3]), lambda s: (0, s, 0, 0))
                  for at, _ in pieces] + [pl.BlockSpec((ts, N), lambda s: (s, 0))],
        out_specs=pl.BlockSpec((sum(rows), N), lambda s: (0, 0)),
        out_shape=jax.ShapeDtypeStruct((sum(rows), N), F32),
        compiler_params=_cp("arbitrary"),
    )(*[at for at, _ in pieces], b)


def _proj_bwd(pieces, dgates, w_rows, x, g, dres):
    S, D = x.shape
    tm = min(TOKEN_TILE, S)
    n_p = len(pieces)
    gate_row = w_rows.shape[0] - dgates.shape[1]

    def body(*refs):
        p_refs = refs[:n_p]
        dgt_ref, w_ref, x_ref, g_ref, dres_ref, dx_ref, dg_ref = refs[n_p:]
        i = pl.program_id(0)
        dh = _nn(dgt_ref[...], w_ref[gate_row:, :])
        for p_ref, (arr, row0) in zip(p_refs, pieces):
            rows = arr.shape[0] * arr.shape[2]
            dh += _tn(_heads_tile(p_ref).astype(BF16), w_ref[row0:row0 + rows, :])
        dx, dg = _norm_bwd(x_ref[...], g_ref[...], dh)
        dx_ref[...] = dres_ref[...] + dx

        @pl.when(i == 0)
        def _():
            dg_ref[...] = dg

        @pl.when(i > 0)
        def _():
            dg_ref[...] += dg

    row = pl.BlockSpec((tm, D), lambda i: (i, 0))
    vec = pl.BlockSpec((1, D), lambda i: (0, 0))
    p_specs = [pl.BlockSpec((a.shape[0], tm // a.shape[3], a.shape[2], a.shape[3]), lambda i: (0, i, 0, 0))
               for a, _ in pieces]
    return pl.pallas_call(
        body, name="proj_bwd", grid=(S // tm,),
        in_specs=p_specs + [pl.BlockSpec((tm, dgates.shape[1]), lambda i: (i, 0)),
                            pl.BlockSpec(w_rows.shape, lambda i: (0, 0), pipeline_mode=pl.Buffered(1)),
                            row, vec, row],
        out_specs=[row, vec],
        out_shape=[jax.ShapeDtypeStruct((S, D), F32), jax.ShapeDtypeStruct((1, D), F32)],
        compiler_params=_cp("arbitrary"),
    )(*[a for a, _ in pieces], dgates, w_rows, x, g, dres)


def _merge_fwd(x1, gates, oa_t, ob_t, wat, wbt, w_out):
    S, D = x1.shape
    W = wat.shape[1]
    tm = min(TOKEN_TILE, S)

    def body(x_ref, ga_ref, gb_ref, oa_ref, ob_ref, wa_ref, wb_ref, wo_ref,
             x2_ref, mg_ref, ba_ref, bb_ref):
        ba = _nt(_heads_tile(oa_ref).T.astype(BF16), wa_ref[...])
        bb = _nt(_heads_tile(ob_ref).T.astype(BF16), wb_ref[...])
        merged = jax.nn.sigmoid(ga_ref[...]) * ba + jax.nn.sigmoid(gb_ref[...]) * bb
        mb = merged.astype(BF16)
        mg_ref[...] = mb
        ba_ref[...] = ba.astype(BF16)
        bb_ref[...] = bb.astype(BF16)
        x2_ref[...] = x_ref[...] + _nn(mb, wo_ref[...])

    row = pl.BlockSpec((tm, D), lambda i: (i, 0))
    full = lambda r, c: pl.BlockSpec((r, c), lambda i: (0, 0))
    heads = lambda a: pl.BlockSpec((a.shape[0], tm // a.shape[3], a.shape[2], a.shape[3]), lambda i: (0, i, 0, 0))
    return pl.pallas_call(
        body, name="merge_fwd", grid=(S // tm,),
        in_specs=[row, pl.BlockSpec((tm, D), lambda i: (i, 0)), pl.BlockSpec((tm, D), lambda i: (i, 1)),
                  heads(oa_t), heads(ob_t), full(D, W), full(D, W), full(D, D)],
        out_specs=[row, row, row, row],
        out_shape=[jax.ShapeDtypeStruct((S, D), F32)] + [jax.ShapeDtypeStruct((S, D), BF16)] * 3,
        compiler_params=_cp("arbitrary"),
    )(x1, gates, gates, oa_t, ob_t, wat, wbt, w_out)


def _merge_bwd(dx2, gates, ba, bb, wa, wb, w_out, t_a, t_b):
    S, D = dx2.shape
    W = wa.shape[0]
    tm = min(TOKEN_TILE, S)
    Hh = W // HEAD_DIM

    def body(dx_ref, ga_ref, gb_ref, ba_ref, bb_ref, wa_ref, wb_ref, wo_ref,
             dxb_ref, dba_ref, dbb_ref, dgt_ref, doa_ref, dob_ref):
        dxb = dx_ref[...].astype(BF16)
        dxb_ref[...] = dxb
        dm = _nt(dxb, wo_ref[...])
        sa = jax.nn.sigmoid(ga_ref[...])
        sb = jax.nn.sigmoid(gb_ref[...])
        dba = (dm * sa).astype(BF16)
        dbb = (dm * sb).astype(BF16)
        dba_ref[...] = dba
        dbb_ref[...] = dbb
        dgt_ref[:, :D] = (dm * ba_ref[...].astype(F32) * sa * (1.0 - sa)).astype(BF16)
        dgt_ref[:, D:] = (dm * bb_ref[...].astype(F32) * sb * (1.0 - sb)).astype(BF16)
        _store_heads(doa_ref, _nt(wa_ref[...], dba))
        _store_heads(dob_ref, _nt(wb_ref[...], dbb))

    row = pl.BlockSpec((tm, D), lambda i: (i, 0))
    full = lambda r, c: pl.BlockSpec((r, c), lambda i: (0, 0))
    heads = lambda T: pl.BlockSpec((Hh, tm // T, HEAD_DIM, T), lambda i: (0, i, 0, 0))
    return pl.pallas_call(
        body, name="merge_bwd", grid=(S // tm,),
        in_specs=[row, pl.BlockSpec((tm, D), lambda i: (i, 0)), pl.BlockSpec((tm, D), lambda i: (i, 1)),
                  row, row, full(W, D), full(W, D), full(D, D)],
        out_specs=[row, row, row, pl.BlockSpec((tm, 2 * D), lambda i: (i, 0)), heads(t_a), heads(t_b)],
        out_shape=[jax.ShapeDtypeStruct((S, D), BF16)] * 3 + [jax.ShapeDtypeStruct((S, 2 * D), BF16),
                   jax.ShapeDtypeStruct((Hh, S // t_a, HEAD_DIM, t_a), F32),
                   jax.ShapeDtypeStruct((Hh, S // t_b, HEAD_DIM, t_b), BF16)],
        compiler_params=_cp("arbitrary"),
    )(dx2, gates, gates, ba, bb, wa, wb, w_out)


def _final_loss(x3, gf, target):
    S, D = x3.shape
    tm = min(TOKEN_TILE, S)

    def body(x_ref, g_ref, t_ref, loss_ref, dx_ref, dg_ref):
        i = pl.program_id(0)
        x = x_ref[...]
        g = g_ref[...]
        e = _norm_fwd(x, g) - t_ref[...]
        part = 0.5 * jnp.sum(jnp.mean(e * e, axis=-1, keepdims=True), axis=0, keepdims=True)
        dx, dg = _norm_bwd(x, g, e * (1.0 / D))
        dx_ref[...] = dx

        @pl.when(i == 0)
        def _():
            loss_ref[...] = part
            dg_ref[...] = dg

        @pl.when(i > 0)
        def _():
            loss_ref[...] += part
            dg_ref[...] += dg

    row = pl.BlockSpec((tm, D), lambda i: (i, 0))
    vec = pl.BlockSpec((1, D), lambda i: (0, 0))
    return pl.pallas_call(
        body, name="final_loss", grid=(S // tm,),
        in_specs=[row, vec, row],
        out_specs=[pl.BlockSpec((1, 1), lambda i: (0, 0)), row, vec],
        out_shape=[jax.ShapeDtypeStruct((1, 1), F32), jax.ShapeDtypeStruct((S, D), F32),
                   jax.ShapeDtypeStruct((1, D), F32)],
        compiler_params=_cp("arbitrary"),
    )(x3, gf, target)


SB_FWD_HEAD_GROUP = 8
SB_HEAD_GROUP = 4
LANES = 128


def _tri(T, kind):
    r = lax.broadcasted_iota(jnp.int32, (T, T), 0)
    c = lax.broadcasted_iota(jnp.int32, (T, T), 1)
    return {"after": r > c, "upto": r <= c, "before": r < c}[kind].astype(BF16)


def _lane(v, j):
    return jnp.broadcast_to(v[:, j:j + 1], (v.shape[0], LANES))


def _t_bf16(x):
    return x.astype(F32).T.astype(BF16)


def _wide(v, T):
    return jnp.tile(v, (1, T // LANES))


SB_SLOTS = 3
SB_FWD_SLOTS = 2
COPY_PARTS = 4


class _split_copy:
    def __init__(self, src, dst, sems):
        n = src.shape[0] // COPY_PARTS
        self.parts = [pltpu.make_async_copy(src.at[pl.ds(r * n, n)], dst.at[pl.ds(r * n, n)], sems.at[r])
                      for r in range(COPY_PARTS)]

    def start(self):
        for cp in self.parts:
            cp.start()

    def wait(self):
        for cp in self.parts:
            cp.wait()


def _sb_pair(i, kb):
    return (i * (i + 1)) // 2 + kb


def _sb_fwd(qkv):
    H3, nb, dh, T = qkv.shape
    H = H3 // 3
    HG = SB_FWD_HEAD_GROUP
    assert HG == H, "one head group: a saved tile holds all the heads"
    n_pairs = (nb * (nb + 1)) // 2

    def body(q_ref, k_ref, v_ref, o_ref, saved_ref, stage, sems):
        row = lax.broadcasted_iota(jnp.int32, (T, T), 0)
        col = lax.broadcasted_iota(jnp.int32, (T, T), 1)
        tri = col < row
        after = _tri(T, "after")

        def save(slot, pair):
            return _split_copy(stage.at[slot], saved_ref.at[pair], sems.at[slot])

        def blocks(qs, i, kb, step, carry, diag):
            hs = range(HG)
            slot = step % SB_FWD_SLOTS

            @pl.when(step >= SB_FWD_SLOTS)
            def _():
                save(slot, 0).wait()

            z = [_nn(qs[hh], k_ref[hh, kb]) for hh in hs]
            res, ls, first = [None] * HG, [None] * HG, [None] * HG
            for hh in hs:
                sp = _softplus(z[hh])
                if diag:
                    sp = jnp.where(tri, sp, 0.0)
                ls[hh] = z[hh] - sp
                spb = sp.astype(BF16)
                first[hh] = _lane(spb.astype(F32), 0)
                res[hh] = _nn(spb, after)
            out = []
            for hh in hs:
                c, oacc = carry[2 * hh], carry[2 * hh + 1]
                a = jnp.exp(ls[hh] - (res[hh] + _wide(c, T)))
                if diag:
                    a = jnp.where(tri, a, 0.0)
                ab = a.astype(BF16)
                stage[slot, hh, 0] = ab
                stage[slot, hh, 1] = jnp.exp(ls[hh]).astype(BF16)
                out.extend([c + (first[hh] + _lane(res[hh], 0)), oacc + _nt(v_ref[hh, kb], ab)])
            save(slot, _sb_pair(i, kb)).start()
            return tuple(out)

        def qblock(i, step):
            qs = [_t_bf16(q_ref[hh, i]) for hh in range(HG)]
            carry = blocks(qs, i, i, step, (jnp.zeros((T, LANES), F32), jnp.zeros((dh, T), F32)) * HG, True)

            def kstep(t, carry):
                return blocks(qs, i, i - 1 - t, step + 1 + t, carry, False)

            carry = lax.fori_loop(0, i, kstep, carry)
            for hh in range(HG):
                o_ref[hh, i] = carry[2 * hh + 1]
            return step + 1 + i

        lax.fori_loop(0, nb, qblock, 0)
        for slot in range(min(SB_FWD_SLOTS, n_pairs)):
            save(slot, 0).wait()

    ht = lambda part: pl.BlockSpec((HG, nb, dh, T), lambda h: (part, 0, 0, 0), pipeline_mode=pl.Buffered(1))
    return pl.pallas_call(
        body, name="sb_fwd", grid=(1,),
        in_specs=[ht(0), ht(1), ht(2)],
        out_specs=[ht(0), ANY],
        out_shape=[jax.ShapeDtypeStruct((H, nb, dh, T), F32),
                   jax.ShapeDtypeStruct((n_pairs, H, 2, T, T), BF16)],
        scratch_shapes=[pltpu.VMEM((SB_FWD_SLOTS, HG, 2, T, T), BF16),
                        pltpu.SemaphoreType.DMA((SB_FWD_SLOTS, COPY_PARTS))],
        compiler_params=_cp("arbitrary"),
    )(qkv, qkv, qkv)


def _sb_bwd(qkv, dot, saved):
    H3, nb, dh, T = qkv.shape
    H = H3 // 3
    HG = SB_HEAD_GROUP
    n_pairs = (nb * (nb + 1)) // 2

    def body(qt_ref, k_ref, v_ref, dot_ref, saved_ref, dq_ref, dk_ref, dv_ref, stage, sems):
        head0 = pl.program_id(0) * HG
        row = lax.broadcasted_iota(jnp.int32, (T, T), 0)
        col = lax.broadcasted_iota(jnp.int32, (T, T), 1)
        tri = col < row
        before = _tri(T, "before")
        dk_ref[...] = jnp.zeros_like(dk_ref)
        dv_ref[...] = jnp.zeros_like(dv_ref)

        def fetch(slot, pair):
            return _split_copy(saved_ref.at[pair, pl.ds(head0, HG)], stage.at[slot], sems.at[slot])

        for ahead in range(min(SB_SLOTS - 1, n_pairs)):
            fetch(ahead, ahead).start()

        def blocks(qTs, dos, doTs, i, kb, carry, diag):
            hs = range(HG)
            pair = _sb_pair(i, kb)
            slot = pair % SB_SLOTS
            fetch(slot, pair).wait()
            nxt = pair + (SB_SLOTS - 1)

            @pl.when(nxt < n_pairs)
            def _():
                fetch(nxt % SB_SLOTS, nxt).start()

            kT = [k_ref[hh, kb] for hh in hs]
            da = [_nn(dos[hh], v_ref[hh, kb]) for hh in hs]
            g, gb, resg = [None] * HG, [None] * HG, [None] * HG
            for hh in hs:
                g[hh] = stage[slot, hh, 0].astype(F32) * da[hh]
                gb[hh] = g[hh].astype(BF16)
                resg[hh] = _nn(gb[hh], before)
            out = []
            for hh in hs:
                pre_g, dq = carry[2 * hh], carry[2 * hh + 1]
                dz = g[hh] - (g[hh] + (resg[hh] + _wide(pre_g, T))) * stage[slot, hh, 1].astype(F32)
                if diag:
                    dz = jnp.where(tri, dz, 0.0)
                dzb = dz.astype(BF16)
                dk_ref[hh, kb] += _nn(qTs[hh], dzb)
                dv_ref[hh, kb] += _nn(doTs[hh], stage[slot, hh, 0])
                out.extend([pre_g + (_lane(resg[hh], T - 1) + _lane(gb[hh].astype(F32), T - 1)),
                            dq + _nt(kT[hh], dzb)])
            return tuple(out)

        def qblock(i, _):
            qTs = [qt_ref[hh, i] for hh in range(HG)]
            doTs = [dot_ref[hh, i] for hh in range(HG)]
            dos = [_t_bf16(v) for v in doTs]
            carry = (jnp.zeros((T, LANES), F32), jnp.zeros((dh, T), F32)) * HG

            def kstep(kb, carry):
                return blocks(qTs, dos, doTs, i, kb, carry, False)

            carry = lax.fori_loop(0, i, kstep, carry)
            carry = blocks(qTs, dos, doTs, i, i, carry, True)
            for hh in range(HG):
                dq_ref[hh, i] = carry[2 * hh + 1]
            return 0

        lax.fori_loop(0, nb, qblock, 0)

    G = H // HG
    ht = lambda part: pl.BlockSpec((HG, nb, dh, T), lambda h: (h + part * G, 0, 0, 0),
                                   pipeline_mode=pl.Buffered(1))
    return pl.pallas_call(
        body, name="sb_bwd", grid=(G,),
        in_specs=[ht(0), ht(1), ht(2), ht(0), ANY],
        out_specs=[ht(0), ht(0), ht(0)],
        out_shape=[jax.ShapeDtypeStruct((H, nb, dh, T), F32)] * 3,
        scratch_shapes=[pltpu.VMEM((SB_SLOTS, HG, 2, T, T), BF16), pltpu.SemaphoreType.DMA((SB_SLOTS, COPY_PARTS))],
        compiler_params=_cp("arbitrary"),
    )(qkv, qkv, qkv, dot, saved)


def _swa_probs(zp, zc, bias, sink, first):
    T = zp.shape[0]
    key = lax.broadcasted_iota(jnp.int32, (T, T), 0)
    qry = lax.broadcasted_iota(jnp.int32, (T, T), 1)
    lp = jnp.where(jnp.logical_and(key > qry, jnp.logical_not(first)), zp + bias[:T, :], NEG_BIG)
    lc = jnp.where(key <= qry, zc + bias[T:, :], NEG_BIG)
    m = jnp.maximum(jnp.maximum(jnp.max(lp, axis=0, keepdims=True), jnp.max(lc, axis=0, keepdims=True)), sink)
    pp = jnp.exp(lp - m)
    pc = jnp.exp(lc - m)
    ps = jnp.exp(sink - m)
    inv = 1.0 / (jnp.sum(pp, axis=0, keepdims=True) + jnp.sum(pc, axis=0, keepdims=True) + ps)
    return pp * inv, pc * inv, ps * inv


def _swa_specs(nb, dh, T, Hq, Hkv, clamp):
    blk = (lambda n: jnp.minimum(n, nb - 1)) if clamp else (lambda n: n)
    q = pl.BlockSpec((Hq, None, dh, T), lambda n: (0, blk(n), 0, 0))
    kv = lambda first, back: pl.BlockSpec(
        (Hkv, None, dh, T), lambda n: (first // Hkv, jnp.maximum(blk(n) - back, 0) if back else blk(n), 0, 0))
    return q, [kv(Hq, 1), kv(Hq, 0), kv(Hq + Hkv, 1), kv(Hq + Hkv, 0)]


def _swa_fwd(qkv, bias, sinks):
    Hq, Hkv, grp = SWA_Q_HEADS, SWA_KV_HEADS, SWA_GROUP
    _, nb, dh, T = qkv.shape

    def body(sink_ref, q_ref, kp_ref, kc_ref, vp_ref, vc_ref, bias_ref, o_ref):
        n = pl.program_id(0)
        kpn = [_t_bf16(kp_ref[hk]) for hk in range(Hkv)]
        kcn = [_t_bf16(kc_ref[hk]) for hk in range(Hkv)]
        zs = [(_nn(kpn[h // grp], q_ref[h]), _nn(kcn[h // grp], q_ref[h])) for h in range(Hq)]
        for h in range(Hq):
            pp, pc, _ = _swa_probs(*zs[h], bias_ref[h], sink_ref[h], n == 0)
            o_ref[h] = _nn(vp_ref[h // grp], pp.astype(BF16)) + _nn(vc_ref[h // grp], pc.astype(BF16))

    q_spec, kv_specs = _swa_specs(nb, dh, T, Hq, Hkv, False)
    return pl.pallas_call(
        body, name="swa_fwd", grid=(nb,),
        in_specs=[pl.BlockSpec(memory_space=pltpu.SMEM), q_spec] + kv_specs
                 + [pl.BlockSpec((Hq, 2 * T, T), lambda n: (0, 0, 0))],
        out_specs=pl.BlockSpec((Hq, None, dh, T), lambda n: (0, n, 0, 0)),
        out_shape=jax.ShapeDtypeStruct((Hq, nb, dh, T), F32),
        compiler_params=_cp("arbitrary"),
    )(sinks, qkv, qkv, qkv, qkv, qkv, bias)


def _swa_bwd(qkv, bias, sinks, dot, ot):
    Hq, Hkv, grp = SWA_Q_HEADS, SWA_KV_HEADS, SWA_GROUP
    _, nb, dh, T = qkv.shape

    def body(sink_ref, qt_ref, kp_ref, kc_ref, vp_ref, vc_ref, bias_ref, dot_ref, ot_ref,
             dq_ref, dk_ref, dv_ref, dbias_ref, dsink_ref, ck, cv):
        n = pl.program_id(0)

        @pl.when(n == 0)
        def _():
            dbias_ref[...] = jnp.zeros_like(dbias_ref)
            dsink_ref[...] = jnp.zeros_like(dsink_ref)
            ck[...] = jnp.zeros_like(ck)
            cv[...] = jnp.zeros_like(cv)

        @pl.when(n < nb)
        def _():
            kp, kc = [kp_ref[hk] for hk in range(Hkv)], [kc_ref[hk] for hk in range(Hkv)]
            kpn, kcn = [_t_bf16(v) for v in kp], [_t_bf16(v) for v in kc]
            vpn = [_t_bf16(vp_ref[hk]) for hk in range(Hkv)]
            vcn = [_t_bf16(vc_ref[hk]) for hk in range(Hkv)]
            qTs = [qt_ref[h] for h in range(Hq)]
            doTs = [dot_ref[h].astype(BF16) for h in range(Hq)]
            zs = [(_nn(kpn[h // grp], qTs[h]), _nn(kcn[h // grp], qTs[h])) for h in range(Hq)]
            dps = [(_nn(vpn[h // grp], doTs[h]), _nn(vcn[h // grp], doTs[h])) for h in range(Hq)]
            dls, pbs = [], []
            for h in range(Hq):
                pp, pc, ps = _swa_probs(*zs[h], bias_ref[h], sink_ref[h], n == 0)
                delta = jnp.sum(dot_ref[h] * ot_ref[h], axis=0, keepdims=True)
                dlp = pp * (dps[h][0] - delta)
                dlc = pc * (dps[h][1] - delta)
                dbias_ref[h, :T, :] += dlp
                dbias_ref[h, T:, :] += dlc
                dsink_ref[h] += -ps * delta
                dls.append((dlp.astype(BF16), dlc.astype(BF16)))
                pbs.append((pp.astype(BF16), pc.astype(BF16)))
            zero = jnp.zeros((dh, T), F32)
            kprev, kcur, vprev, vcur = [zero] * Hkv, [zero] * Hkv, [zero] * Hkv, [zero] * Hkv
            for h in range(Hq):
                hk = h // grp
                dlpb, dlcb = dls[h]
                dq_ref[h] = _nn(kp[hk], dlpb) + _nn(kc[hk], dlcb)
                kprev[hk] = kprev[hk] + _nt(qTs[h], dlpb)
                kcur[hk] = kcur[hk] + _nt(qTs[h], dlcb)
                vprev[hk] = vprev[hk] + _nt(doTs[h], pbs[h][0])
                vcur[hk] = vcur[hk] + _nt(doTs[h], pbs[h][1])
            for hk in range(Hkv):
                dk_ref[hk] = ck[hk] + kprev[hk]
                dv_ref[hk] = cv[hk] + vprev[hk]
                ck[hk] = kcur[hk]
                cv[hk] = vcur[hk]

        @pl.when(n == nb)
        def _():
            dk_ref[...] = ck[...]
            dv_ref[...] = cv[...]

    qt_spec, kv_specs = _swa_specs(nb, dh, T, Hq, Hkv, True)
    prev = pl.BlockSpec((Hkv, None, dh, T), lambda n: (0, jnp.maximum(n - 1, 0), 0, 0))
    whole = lambda a, b: pl.BlockSpec((Hq, a, b), lambda n: (0, 0, 0))
    return pl.pallas_call(
        body, name="swa_bwd", grid=(nb + 1,),
        in_specs=[pl.BlockSpec(memory_space=pltpu.SMEM), qt_spec] + kv_specs
                 + [whole(2 * T, T), qt_spec, qt_spec],
        out_specs=[qt_spec, prev, prev, whole(2 * T, T), whole(1, T)],
        out_shape=[jax.ShapeDtypeStruct((Hq, nb, dh, T), F32), jax.ShapeDtypeStruct((Hkv, nb, dh, T), F32),
                   jax.ShapeDtypeStruct((Hkv, nb, dh, T), F32), jax.ShapeDtypeStruct((Hq, 2 * T, T), F32),
                   jax.ShapeDtypeStruct((Hq, 1, T), F32)],
        scratch_shapes=[pltpu.VMEM((Hkv, dh, T), F32), pltpu.VMEM((Hkv, dh, T), F32)],
        compiler_params=_cp("arbitrary"),
    )(sinks, qkv, qkv, qkv, qkv, qkv, bias, dot, ot)


def _split3(x):
    h1 = x.astype(BF16)
    r1 = x - h1.astype(F32)
    h2 = r1.astype(BF16)
    h3 = (r1 - h2.astype(F32)).astype(BF16)
    return h1, h2, h3


def _bias_expand(rel_t, onehot):
    Hq, NB = rel_t.shape
    L = onehot.shape[1]

    def body(r_ref, oh_ref, o_ref):
        h1, h2, h3 = _split3(r_ref[...])
        oh = oh_ref[...]
        o_ref[...] = _nn(h1, oh) + _nn(h2, oh) + _nn(h3, oh)

    return pl.pallas_call(
        body, name="bias_expand", grid=(1,),
        in_specs=[pl.BlockSpec((Hq, NB), lambda i: (0, 0)), pl.BlockSpec((NB, L), lambda i: (0, 0))],
        out_specs=pl.BlockSpec((Hq, L), lambda i: (0, 0)),
        out_shape=jax.ShapeDtypeStruct((Hq, L), F32),
        compiler_params=_cp("arbitrary"),
    )(rel_t, onehot)


def _bias_reduce(dbias, onehot):
    Hq, L = dbias.shape
    NB = onehot.shape[0]

    def body(d_ref, oh_ref, o_ref):
        h1, h2, h3 = _split3(d_ref[...])
        oh = oh_ref[...]
        o_ref[...] = _nt(h1, oh) + _nt(h2, oh) + _nt(h3, oh)

    return pl.pallas_call(
        body, name="bias_reduce", grid=(1,),
        in_specs=[pl.BlockSpec((Hq, L), lambda i: (0, 0)), pl.BlockSpec((NB, L), lambda i: (0, 0))],
        out_specs=pl.BlockSpec((Hq, NB), lambda i: (0, 0)),
        out_shape=jax.ShapeDtypeStruct((Hq, NB), F32),
        compiler_params=_cp("arbitrary"),
    )(dbias, onehot)


def _adamw(w, g, m, v, name):
    R, C = w.shape
    tr = 256 if R % 256 == 0 else R
    bc1 = 1.0 - ADAM_B1 ** ADAM_STEP
    bc2 = 1.0 - ADAM_B2 ** ADAM_STEP

    def body(w_ref, g_ref, m_ref, v_ref, d_ref, nm_ref, nv_ref):
        g = g_ref[...]
        m2 = ADAM_B1 * m_ref[...] + (1.0 - ADAM_B1) * g
        v2 = ADAM_B2 * v_ref[...] + (1.0 - ADAM_B2) * (g * g)
        nm_ref[...] = m2
        nv_ref[...] = v2
        d_ref[...] = -ADAM_LR * ((m2 / bc1) / (jnp.sqrt(v2 / bc2) + ADAM_EPS) + ADAM_WD * w_ref[...])

    spec = pl.BlockSpec((tr, C), lambda i: (i, 0))
    return pl.pallas_call(
        body, name=name, grid=(R // tr,),
        in_specs=[spec] * 4, out_specs=[spec] * 3,
        out_shape=[jax.ShapeDtypeStruct((R, C), F32)] * 3,
        compiler_params=_cp("arbitrary"),
    )(w, g, m, v)


def _row_tile(R):
    return max(t for t in range(16, 513, 16) if R % t == 0)


def _add_halves(mine, recv, name):
    K, R, C = mine.shape
    tr = _row_tile(R)

    def body(a_ref, b_ref, o_ref, ob_ref):
        s = a_ref[...].astype(F32) + b_ref[...].astype(F32)
        o_ref[...] = s
        ob_ref[...] = s.astype(BF16)

    spec = pl.BlockSpec((None, tr, C), lambda k, i: (k, i, 0))
    return pl.pallas_call(
        body, name=name, grid=(K, R // tr),
        in_specs=[spec, spec], out_specs=[spec, spec],
        out_shape=[jax.ShapeDtypeStruct((K, R, C), F32), jax.ShapeDtypeStruct((K, R, C), BF16)],
        compiler_params=_cp("arbitrary", "arbitrary"),
    )(mine, recv)


def _add_received(own, recv, name):
    R, C = own.shape
    tr = _row_tile(R)

    def body(a_ref, r_ref, o_ref):
        o_ref[...] = ((a_ref[...] + r_ref[0].astype(F32)) + r_ref[1].astype(F32)) + r_ref[2].astype(F32)

    return pl.pallas_call(
        body, name=name, grid=(R // tr,),
        in_specs=[pl.BlockSpec((tr, C), lambda i: (i, 0)), pl.BlockSpec((3, tr, C), lambda i: (0, i, 0))],
        out_specs=pl.BlockSpec((tr, C), lambda i: (i, 0)),
        out_shape=jax.ShapeDtypeStruct((R, C), F32),
        compiler_params=_cp("arbitrary"),
    )(own, recv)


def _position():
    x, y, c = lax.axis_index("x"), lax.axis_index("y"), lax.axis_index("c")
    others = [(1 - x, y), (x, 1 - y), (1 - x, 1 - y)]
    return x, y, c, others


def _remote(src, dst, send_sems, recv_sems, k, dev):
    return pltpu.make_async_remote_copy(src_ref=src, dst_ref=dst, send_sem=send_sems.at[k],
                                        recv_sem=recv_sems.at[k], device_id=dev, device_id_type=MESH_ID)


class _gather_exchange:
    def __init__(self, src, out, send_sems, recv_sems):
        x, y, c, others = _position()
        mine, sibling = 2 * x + y, (x, y, 1 - c)
        self.sends, self.arrivals, self.passes, self.from_sibling = [], [], [], []
        for j, (ox, oy) in enumerate(others):
            slot = out.at[2 * ox + oy, c]
            theirs = out.at[2 * ox + oy, 1 - c]
            self.sends.append(_remote(src.at[c], out.at[mine, c], send_sems, recv_sems, j, (ox, oy, c)))
            self.arrivals.append(_remote(slot, slot, send_sems, recv_sems, j, (ox, oy, c)))
            self.passes.append(_remote(slot, slot, send_sems, recv_sems, 3 + j, sibling))
            self.from_sibling.append(_remote(theirs, theirs, send_sems, recv_sems, 3 + j, sibling))

    def start(self):
        for cp in self.sends:
            cp.start()

    def forward(self):
        for arrived, onward in zip(self.arrivals, self.passes):
            arrived.wait_recv()
            onward.start()

    def finish(self):
        for cp in self.from_sibling:
            cp.wait_recv()
        for cp in self.sends + self.passes:
            cp.wait_send()


def _gather_weights(shard):
    R, C = shard.shape
    half = R // 2

    def body(src, out, send_sems, recv_sems):
        ex = _gather_exchange(src, out, send_sems, recv_sems)
        ex.start()
        ex.forward()
        ex.finish()

    return pl.pallas_call(
        body, name="gather_weights",
        in_specs=[ANY], out_specs=ANY,
        out_shape=jax.ShapeDtypeStruct((N_CHIPS, 2, half, C), shard.dtype),
        scratch_shapes=[pltpu.SemaphoreType.DMA((6,)), pltpu.SemaphoreType.DMA((6,))],
    )(shard.reshape(2, half, C)).reshape(N_CHIPS, R, C)


def _swap_halves(grads, name):
    K, R, C = grads.shape
    half = R // 2

    def body(src, out, send_sems, recv_sems):
        x, y, c, _ = _position()
        theirs = src.at[:, pl.ds(pl.multiple_of((1 - c) * half, 16), half), :]
        cp = _remote(theirs, out, send_sems, recv_sems, 0, (x, y, 1 - c))
        cp.start()
        cp.wait()

    return pl.pallas_call(
        body, name=name,
        in_specs=[ANY], out_specs=ANY,
        out_shape=jax.ShapeDtypeStruct((K, half, C), grads.dtype),
        scratch_shapes=[pltpu.SemaphoreType.DMA((1,)), pltpu.SemaphoreType.DMA((1,))],
    )(grads)


class _scatter_exchange:
    def __init__(self, src, out, send_sems, recv_sems):
        x, y, c, others = _position()
        self.copies = [_remote(src.at[2 * ox + oy], out.at[j], send_sems, recv_sems, j, (ox, oy, c))
                       for j, (ox, oy) in enumerate(others)]

    def start(self):
        for cp in self.copies:
            cp.start()

    def finish(self):
        for cp in self.copies:
            cp.wait()


def _scatter_to_owners(parts, name):
    K, H, C = parts.shape

    def body(src, out, send_sems, recv_sems):
        ex = _scatter_exchange(src, out, send_sems, recv_sems)
        ex.start()
        ex.finish()

    return pl.pallas_call(
        body, name=name,
        in_specs=[ANY], out_specs=ANY,
        out_shape=jax.ShapeDtypeStruct((3, H, C), parts.dtype),
        scratch_shapes=[pltpu.SemaphoreType.DMA((3,)), pltpu.SemaphoreType.DMA((3,))],
    )(parts)


def _swap_reduced(half_rows, name):
    H, C = half_rows.shape

    def body(src, out, send_sems, recv_sems):
        x, y, c, _ = _position()
        cp = _remote(src, out, send_sems, recv_sems, 0, (x, y, 1 - c))
        cp.start()
        cp.wait()

    return pl.pallas_call(
        body, name=name,
        in_specs=[ANY], out_specs=ANY,
        out_shape=jax.ShapeDtypeStruct((H, C), half_rows.dtype),
        scratch_shapes=[pltpu.SemaphoreType.DMA((1,)), pltpu.SemaphoreType.DMA((1,))],
    )(half_rows)


def _allreduce_small(block):
    R, C = block.shape
    n_dev = 8

    def body(src, out, slots, send_sems, recv_sems):
        x, y, c, _ = _position()
        me = 4 * x + 2 * y + c
        slots[me] = src[...]
        sends = []
        for r in range(1, n_dev):
            peer = (x ^ (r >> 2), y ^ ((r >> 1) & 1), c ^ (r & 1))
            cp = _remote(src, slots.at[me], send_sems, recv_sems, r - 1, peer)
            cp.start()
            sends.append(cp)
        for r in range(1, n_dev):
            theirs = slots.at[me ^ r]
            _remote(theirs, theirs, send_sems, recv_sems, r - 1, (x, y, c)).wait_recv()
        for cp in sends:
            cp.wait_send()
        acc = slots[0]
        for d in range(1, n_dev):
            acc = acc + slots[d]
        out[...] = acc

    return pl.pallas_call(
        body, name="allreduce_small",
        in_specs=[pl.BlockSpec(memory_space=pltpu.VMEM)], out_specs=pl.BlockSpec(memory_space=pltpu.VMEM),
        out_shape=jax.ShapeDtypeStruct((R, C), F32),
        scratch_shapes=[pltpu.VMEM((n_dev, R, C), F32), pltpu.SemaphoreType.DMA((7,)), pltpu.SemaphoreType.DMA((7,))],
    )(block)


def _rel_bucket(dist):
    max_exact = REL_BUCKETS // 2
    d = jnp.maximum(dist, 1).astype(F32)
    large = max_exact + (jnp.log(d / max_exact) / math.log(REL_MAX_DIST / max_exact)
                         * (REL_BUCKETS - max_exact)).astype(jnp.int32)
    large = jnp.minimum(large, REL_BUCKETS - 1)
    return jnp.where(dist < max_exact, dist, large)


def _bucket_onehot():
    T = SWA_BLOCK
    dist = (jnp.arange(T)[None, :] + T) - jnp.arange(2 * T)[:, None]
    bucket = _rel_bucket(jnp.maximum(dist, 0)).reshape(1, T * 2 * T)
    return (bucket == jnp.arange(REL_BUCKETS)[:, None]).astype(BF16)


_BUF = (("ffn1_w1", "t"), ("ffn1_w3", "t"), ("ffn1_w2", "n"), ("ffn2_w1", "t"), ("ffn2_w3", "t"),
        ("ffn2_w2", "n"), ("w_in", "t"), ("w_out", "n"), ("w_branch_swa", "tw"), ("w_branch_sb", "tw"))


def _to_rows(name_kind, w, D):
    kind = name_kind[1]
    if kind == "n":
        return w
    if kind == "t":
        return w.T
    return w.T.reshape(-1, D)


def _from_rows(name_kind, rows, width):
    kind = name_kind[1]
    if kind == "n":
        return rows
    if kind == "t":
        return rows.T
    return rows.reshape(-1, width).T


def kernel(x, norm_ffn1, ffn1_w1, ffn1_w3, ffn1_w2, norm_mix, w_in, swa_sinks, rel_bias, w_branch_swa, w_branch_sb, w_out, norm_ffn2, ffn2_w1, ffn2_w3, ffn2_w2, norm_final, loss_target, m_norm_ffn1, m_ffn1_w1, m_ffn1_w3, m_ffn1_w2, m_norm_mix, m_w_in, m_swa_sinks, m_rel_bias, m_w_branch_swa, m_w_branch_sb, m_w_out, m_norm_ffn2, m_ffn2_w1, m_ffn2_w3, m_ffn2_w2, m_norm_final, v_norm_ffn1, v_ffn1_w1, v_ffn1_w3, v_ffn1_w2, v_norm_mix, v_w_in, v_swa_sinks, v_rel_bias, v_w_branch_swa, v_w_branch_sb, v_w_out, v_norm_ffn2, v_ffn2_w1, v_ffn2_w3, v_ffn2_w2, v_norm_final):
    names = ["norm_ffn1", "ffn1_w1", "ffn1_w3", "ffn1_w2", "norm_mix", "w_in", "swa_sinks", "rel_bias",
             "w_branch_swa", "w_branch_sb", "w_out", "norm_ffn2", "ffn2_w1", "ffn2_w3", "ffn2_w2", "norm_final"]
    W = dict(zip(names, [norm_ffn1, ffn1_w1, ffn1_w3, ffn1_w2, norm_mix, w_in, swa_sinks, rel_bias,
                         w_branch_swa, w_branch_sb, w_out, norm_ffn2, ffn2_w1, ffn2_w3, ffn2_w2, norm_final]))
    M = dict(zip(names, [m_norm_ffn1, m_ffn1_w1, m_ffn1_w3, m_ffn1_w2, m_norm_mix, m_w_in, m_swa_sinks, m_rel_bias,
                         m_w_branch_swa, m_w_branch_sb, m_w_out, m_norm_ffn2, m_ffn2_w1, m_ffn2_w3, m_ffn2_w2,
                         m_norm_final]))
    V = dict(zip(names, [v_norm_ffn1, v_ffn1_w1, v_ffn1_w3, v_ffn1_w2, v_norm_mix, v_w_in, v_swa_sinks, v_rel_bias,
                         v_w_branch_swa, v_w_branch_sb, v_w_out, v_norm_ffn2, v_ffn2_w1, v_ffn2_w3, v_ffn2_w2,
                         v_norm_final]))
    xs = x[0]
    target = loss_target[0]
    S, D = xs.shape
    QW = SWA_Q_HEADS * HEAD_DIM
    KW = SWA_KV_HEADS * HEAD_DIM
    BW = SB_HEADS * HEAD_DIM
    QKV = QW + 2 * KW + 3 * BW

    pieces = [_to_rows(nk, W[nk[0]][0], D) for nk in _BUF]
    sizes = [p.shape[0] for p in pieces]
    offs = [0]
    for s in sizes:
        offs.append(offs[-1] + s)
    n_first = 3
    first_rows = offs[n_first]
    shard_a = jnp.concatenate(pieces[:n_first], axis=0).astype(BF16)
    shard_b = jnp.concatenate(pieces[n_first:], axis=0).astype(BF16)
    chip = 2 * lax.axis_index("x") + lax.axis_index("y")
    gathered_a = lax.dynamic_update_slice(_gather_weights(shard_a), shard_a[None], (chip, 0, 0))
    f1w1, f1w3, f1w2 = [gathered_a[:, offs[i]:offs[i + 1], :].reshape(N_CHIPS * sizes[i], D) for i in range(n_first)]

    g1, gmix, g3 = W["norm_ffn1"], W["norm_mix"], W["norm_ffn2"]
    gf = W["norm_final"].reshape(1, D)

    x1, h1, a1, b1, gathered_b = _ffn_fwd(xs, g1, f1w1, f1w3, f1w2, "ffn1_fwd", gather=shard_b)
    gathered_b = lax.dynamic_update_slice(gathered_b, shard_b[None], (chip, 0, 0))

    def full(i):
        return gathered_b[:, offs[i] - first_rows:offs[i + 1] - first_rows, :].reshape(N_CHIPS * sizes[i], D)

    f2w1, f2w3, f2w2, w_in_t, w_out_f = [full(i) for i in range(n_first, 8)]
    wa_t = full(8).reshape(D, QW)
    wb_t = full(9).reshape(D, BW)
    o0 = QW + 2 * KW
    rows = jnp.arange(w_in_t.shape[0])
    is_q = (rows < QW) | ((rows >= o0) & (rows < o0 + BW))
    w_in_s = w_in_t * jnp.where(is_q, QK_SCALE, 1.0).astype(BF16)[:, None]
    qkv_a = _norm_proj_heads(x1, gmix, w_in_s[:o0], SWA_BLOCK, "proj_swa")
    qkv_b = _norm_proj_heads(x1, gmix, w_in_s[o0:QKV], SB_BLOCK, "proj_sb")
    gates, h2 = _norm_matmul_nt(x1, gmix, w_in_t[QKV:], F32, "proj_gates")

    onehot = _bucket_onehot()
    bias = _bias_expand(W["rel_bias"].T, onehot).reshape(SWA_Q_HEADS, 2 * SWA_BLOCK, SWA_BLOCK)
    sinks = W["swa_sinks"].reshape(SWA_Q_HEADS)
    oa_t = _swa_fwd(qkv_a, bias, sinks)
    ob_t, saved_sb = _sb_fwd(qkv_b)

    x2, merged, ba, bb = _merge_fwd(x1, gates, oa_t, ob_t, wa_t, wb_t, w_out_f)
    x3, h3, a2, b2 = _ffn_fwd(x2, g3, f2w1, f2w3, f2w2, "ffn2_fwd")
    loss_part, dx3, dgf = _final_loss(x3, gf, target)

    dx2, dg3, dz2, da2, db2, u2 = _ffn_bwd(dx3, x2, g3, a2, b2, f2w1, f2w3, f2w2, "ffn2_bwd")
    grads = {}
    grads["ffn2_w1"] = _tn_matmul(da2, h3, "ffn2_dw1")
    grads["ffn2_w3"] = _tn_matmul(db2, h3, "ffn2_dw3")
    grads["ffn2_w2"] = _tn_matmul(u2, dz2, "ffn2_dw2")

    dx2b, dba, dbb, dgates, doa_t, dob_t = _merge_bwd(dx2, gates, ba, bb, wa_t.T, wb_t.T, w_out_f,
                                                      SWA_BLOCK, SB_BLOCK)
    grads["w_out"] = _tn_matmul(merged, dx2b, "dw_out")
    grads["w_branch_swa"] = _heads_matmul([(oa_t, 1.0)], dba, "dw_branch_swa").T
    grads["w_branch_sb"] = _heads_matmul([(ob_t, 1.0)], dbb, "dw_branch_sb").T

    dqb_t, dkb_t, dvb_t = _sb_bwd(qkv_b, dob_t, saved_sb)
    dqa_t, dka_t, dva_t, dbias, dsink_rows = _swa_bwd(qkv_a, bias, sinks, doa_t, oa_t)
    d_rel = _bias_reduce(dbias.reshape(SWA_Q_HEADS, -1), onehot).T
    d_sinks = jnp.sum(dsink_rows, axis=(1, 2))

    dheads = [(dqa_t, QK_SCALE), (dka_t, 1.0), (dva_t, 1.0), (dqb_t, QK_SCALE), (dkb_t, 1.0), (dvb_t, 1.0)]
    grads["w_in"] = jnp.concatenate([_heads_matmul(dheads, h2, "dw_in_heads").astype(BF16),
                                     _tn_matmul(dgates, h2, "dw_in_gates")], axis=0)
    row0, pieces_in = 0, []
    for a, _ in dheads:
        pieces_in.append((a, row0))
        row0 += a.shape[0] * HEAD_DIM
    dx1, dgmix = _proj_bwd(pieces_in, dgates, w_in_s, x1, gmix, dx2)

    c = lax.axis_index("c")

    def reduce_start(lo, hi, tag):
        gbuf = jnp.concatenate([grads[_BUF[i][0]].astype(BF16).reshape(N_CHIPS, sizes[i], D) for i in range(lo, hi)],
                               axis=1)
        half = gbuf.shape[1] // 2
        from_sibling = _swap_halves(gbuf, "swap_halves_" + tag)
        my_half = lax.dynamic_slice_in_dim(gbuf, c * half, half, axis=1)
        return _add_halves(my_half, from_sibling, "add_sibling_" + tag)

    def reduce_finish(chip_sum, received, tag):
        own = lax.dynamic_index_in_dim(chip_sum, chip, axis=0, keepdims=False)
        my_rows = _add_received(own, received, "add_chips_" + tag)
        their_rows = _swap_reduced(my_rows, "swap_reduced_" + tag)
        return jnp.concatenate([jnp.where(c == 0, my_rows, their_rows), jnp.where(c == 0, their_rows, my_rows)],
                               axis=0)

    sum_b, sum16_b = reduce_start(n_first, len(_BUF), "late")
    dx0, dg1, dz1, da1, db1, u1, received_b = _ffn_bwd(dx1, xs, g1, a1, b1, f1w1, f1w3, f1w2, "ffn1_bwd",
                                                       scatter=sum16_b)
    grads["ffn1_w1"] = _tn_matmul(da1, h1, "ffn1_dw1")
    grads["ffn1_w3"] = _tn_matmul(db1, h1, "ffn1_dw3")
    grads["ffn1_w2"] = _tn_matmul(u1, dz1, "ffn1_dw2")
    sum_a, sum16_a = reduce_start(0, n_first, "first")
    reduced = jnp.concatenate([reduce_finish(sum_a, _scatter_to_owners(sum16_a, "scatter_to_owners"), "first"),
                               reduce_finish(sum_b, received_b, "late")], axis=0)

    small_rows = [dg1, dgmix, dg3, dgf,
                  jnp.pad(d_sinks.reshape(1, -1), ((0, 0), (0, D - SWA_Q_HEADS))),
                  jnp.pad(d_rel.reshape(1, -1), ((0, 0), (0, D - REL_BUCKETS * SWA_Q_HEADS))),
                  jnp.pad(loss_part, ((0, 0), (0, D - 1))), jnp.zeros((1, D), F32)]
    small = _allreduce_small(jnp.concatenate(small_rows, axis=0))
    loss = small[6, 0]

    G = {}
    for i, nk in enumerate(_BUF):
        G[nk[0]] = _from_rows(nk, reduced[offs[i]:offs[i + 1]], W[nk[0]].shape[1])[None]
    G["norm_ffn1"], G["norm_mix"], G["norm_ffn2"] = small[0:1], small[1:2], small[2:3]
    G["norm_final"] = small[3]
    G["swa_sinks"] = small[4:5, :SWA_Q_HEADS]
    G["rel_bias"] = small[5, :REL_BUCKETS * SWA_Q_HEADS].reshape(REL_BUCKETS, SWA_Q_HEADS)

    delta, new_m, new_v = {}, {}, {}
    small_names = ["norm_ffn1", "norm_mix", "norm_ffn2", "norm_final", "swa_sinks", "rel_bias"]

    def pack(d):
        return jnp.concatenate([jnp.pad(d[n].reshape(1, -1), ((0, 0), (0, D - d[n].size))) for n in small_names]
                               + [jnp.zeros((2, D), F32)], axis=0)

    sd, sm, sv = _adamw(pack(W), pack(G), pack(M), pack(V), "adamw_small")
    for r, n in enumerate(small_names):
        for dst, src in ((delta, sd), (new_m, sm), (new_v, sv)):
            dst[n] = src[r, :W[n].size].reshape(W[n].shape)
    for nk in _BUF:
        n = nk[0]
        shp = W[n].shape
        two_d = (shp[1], shp[2])
        d_, m_, v_ = _adamw(W[n].reshape(two_d), G[n].reshape(two_d), M[n].reshape(two_d), V[n].reshape(two_d),
                            "adamw_" + n)
        delta[n], new_m[n], new_v[n] = d_.reshape(shp), m_.reshape(shp), v_.reshape(shp)

    return (loss, dx0[None], *[G[n] for n in names], *[delta[n] for n in names],
            *[new_m[n] for n in names], *[new_v[n] for n in names])
```

```python
import functools
import math

import jax
import jax.numpy as jnp
from jax import lax
from jax.experimental import pallas as pl
from jax.experimental.pallas import tpu as pltpu

F32, BF16 = jnp.float32, jnp.bfloat16
MESH_ID = pl.DeviceIdType.MESH
ANY = pl.BlockSpec(memory_space=pl.ANY)

RMS_EPS = 1e-6
HEAD_DIM = 64
SWA_Q_HEADS, SWA_KV_HEADS, SWA_GROUP = 8, 2, 4
SWA_BLOCK = 128
SB_HEADS = 8
SB_BLOCK = 256
REL_BUCKETS, REL_MAX_DIST = 32, 128
NEG_BIG = -1e30
QK_SCALE = HEAD_DIM ** -0.5
ADAM_LR, ADAM_B1, ADAM_B2, ADAM_EPS, ADAM_WD, ADAM_STEP = 0.001, 0.9, 0.999, 1e-08, 0.01, 10

N_CHIPS = 4
TOKEN_TILE = 512
MATMUL_TOKEN_TILE = 1024
WGRAD_ROW_TILES = (2176, 1408, 1024, 256)
FF_TILE = 1408
FFN_TOKEN_TILE = 512
FF_BWD_TILE = 256
VMEM_LIMIT = 56 * 1024 * 1024


def _cp(*sem):
    return pltpu.CompilerParams(dimension_semantics=sem, vmem_limit_bytes=VMEM_LIMIT)


def _nn(a, b):
    return jnp.dot(a, b, preferred_element_type=F32)


def _nt(a, b):
    return lax.dot_general(a, b, (((1,), (1,)), ((), ())), preferred_element_type=F32)


def _tn(a, b):
    return lax.dot_general(a, b, (((0,), (0,)), ((), ())), preferred_element_type=F32)


def _norm_fwd(x, g):
    return x * lax.rsqrt(jnp.mean(x * x, axis=-1, keepdims=True) + RMS_EPS) * g


def _norm_bwd(x, g, dh):
    r = lax.rsqrt(jnp.mean(x * x, axis=-1, keepdims=True) + RMS_EPS)
    xh = x * r
    dxh = dh * g
    dx = r * (dxh - xh * jnp.mean(dxh * xh, axis=-1, keepdims=True))
    return dx, jnp.sum(dh * xh, axis=0, keepdims=True)


SOFTPLUS_LINEAR = 20.0


def _softplus(z):
    return jnp.maximum(jnp.log(1.0 + jnp.exp(jnp.minimum(z, SOFTPLUS_LINEAR))), z)


def _ffn_fwd(x, g, w1t, w3t, w2, name, gather=None):
    S, D = x.shape
    F = w2.shape[0]
    tm, tf = min(FFN_TOKEN_TILE, S), FF_TILE
    ni, nj = S // tm, F // tf

    def body(x_ref, g_ref, w1_ref, w3_ref, w2_ref, *rest):
        if gather is None:
            xo_ref, h_ref, a_ref, b_ref, hs, acc = rest
        else:
            shard_ref, xo_ref, h_ref, a_ref, b_ref, gathered_ref, hs, acc, send_sems, recv_sems = rest
        i, j = pl.program_id(0), pl.program_id(1)
        if gather is not None:
            for when, phase in ((jnp.logical_and(i == 0, j == 0), "start"),
                                (jnp.logical_and(i == ni - 1, j == 0), "forward"),
                                (jnp.logical_and(i == ni - 1, j == nj - 1), "finish")):
                @pl.when(when)
                def _():
                    getattr(_gather_exchange(shard_ref, gathered_ref, send_sems, recv_sems), phase)()

        @pl.when(j == 0)
        def _():
            hb = _norm_fwd(x_ref[...], g_ref[...]).astype(BF16)
            hs[...] = hb
            h_ref[...] = hb
            acc[...] = jnp.zeros_like(acc)

        h = hs[...]
        a = _nt(h, w1_ref[...])
        b = _nt(h, w3_ref[...])
        a_ref[...] = a.astype(BF16)
        b_ref[...] = b.astype(BF16)
        u = a * jax.nn.sigmoid(a) * b
        acc[...] += _nn(u.astype(BF16), w2_ref[...])

        @pl.when(j == nj - 1)
        def _():
            xo_ref[...] = x_ref[...] + 0.5 * acc[...]

    in_specs = [pl.BlockSpec((tm, D), lambda i, j: (i, 0)),
                pl.BlockSpec((1, D), lambda i, j: (0, 0)),
                pl.BlockSpec((tf, D), lambda i, j: (j, 0)),
                pl.BlockSpec((tf, D), lambda i, j: (j, 0)),
                pl.BlockSpec((tf, D), lambda i, j: (j, 0))]
    out_specs = [pl.BlockSpec((tm, D), lambda i, j: (i, 0)),
                 pl.BlockSpec((tm, D), lambda i, j: (i, 0)),
                 pl.BlockSpec((tm, tf), lambda i, j: (i, j)),
                 pl.BlockSpec((tm, tf), lambda i, j: (i, j))]
    out_shape = [jax.ShapeDtypeStruct((S, D), F32), jax.ShapeDtypeStruct((S, D), BF16),
                 jax.ShapeDtypeStruct((S, F), BF16), jax.ShapeDtypeStruct((S, F), BF16)]
    scratch = [pltpu.VMEM((tm, D), BF16), pltpu.VMEM((tm, D), F32)]
    operands = [x, g, w1t, w3t, w2]
    if gather is not None:
        R, C = gather.shape
        in_specs.append(ANY)
        out_specs.append(ANY)
        out_shape.append(jax.ShapeDtypeStruct((N_CHIPS, 2, R // 2, C), gather.dtype))
        scratch += [pltpu.SemaphoreType.DMA((6,)), pltpu.SemaphoreType.DMA((6,))]
        operands.append(gather.reshape(2, R // 2, C))
    outs = list(pl.pallas_call(
        body, name=name, grid=(ni, nj), in_specs=in_specs, out_specs=out_specs, out_shape=out_shape,
        scratch_shapes=scratch, compiler_params=_cp("arbitrary", "arbitrary"),
    )(*operands))
    if gather is not None:
        outs[4] = outs[4].reshape(N_CHIPS, R, C)
    return outs


def _ffn_bwd(dxo, x, g, a, b, w1t, w3t, w2, name, scatter=None):
    S, D = x.shape
    F = w2.shape[0]
    tm, tf = min(MATMUL_TOKEN_TILE, S), FF_BWD_TILE
    ni, nj = S // tm, F // tf

    def body(dxo_ref, x_ref, g_ref, a_ref, b_ref, w1_ref, w3_ref, w2_ref, *rest):
        if scatter is None:
            dx_ref, dg_ref, dz_ref, da_ref, db_ref, u_ref, dzs, acc = rest
        else:
            (parts_ref, dx_ref, dg_ref, dz_ref, da_ref, db_ref, u_ref, recv_ref,
             dzs, acc, send_sems, recv_sems) = rest
        i, j = pl.program_id(0), pl.program_id(1)
        if scatter is not None:
            for when, phase in ((jnp.logical_and(i == 0, j == 0), "start"),
                                (jnp.logical_and(i == ni - 1, j == nj - 1), "finish")):
                @pl.when(when)
                def _():
                    getattr(_scatter_exchange(parts_ref, recv_ref, send_sems, recv_sems), phase)()

        @pl.when(j == 0)
        def _():
            dzb = (0.5 * dxo_ref[...]).astype(BF16)
            dzs[...] = dzb
            dz_ref[...] = dzb
            acc[...] = jnp.zeros_like(acc)

        du = _nt(dzs[...], w2_ref[...])
        av = a_ref[...].astype(F32)
        bv = b_ref[...].astype(F32)
        s = jax.nn.sigmoid(av)
        silu = av * s
        db = (du * silu).astype(BF16)
        da = (du * bv * (s * (1.0 + av * (1.0 - s)))).astype(BF16)
        da_ref[...] = da
        db_ref[...] = db
        u_ref[...] = (silu * bv).astype(BF16)
        acc[...] += _nn(da, w1_ref[...]) + _nn(db, w3_ref[...])

        @pl.when(j == nj - 1)
        def _():
            dx, dg = _norm_bwd(x_ref[...], g_ref[...], acc[...])
            dx_ref[...] = dxo_ref[...] + dx

            @pl.when(i == 0)
            def _():
                dg_ref[...] = dg

            @pl.when(i > 0)
            def _():
                dg_ref[...] += dg

    row = pl.BlockSpec((tm, D), lambda i, j: (i, 0))
    wsp = pl.BlockSpec((tf, D), lambda i, j: (j, 0))
    col = pl.BlockSpec((tm, tf), lambda i, j: (i, j))
    vec = pl.BlockSpec((1, D), lambda i, j: (0, 0))
    in_specs = [row, row, vec, col, col, wsp, wsp, wsp]
    out_specs = [row, vec, row, col, col, col]
    out_shape = [jax.ShapeDtypeStruct((S, D), F32), jax.ShapeDtypeStruct((1, D), F32),
                 jax.ShapeDtypeStruct((S, D), BF16), jax.ShapeDtypeStruct((S, F), BF16),
                 jax.ShapeDtypeStruct((S, F), BF16), jax.ShapeDtypeStruct((S, F), BF16)]
    scratch = [pltpu.VMEM((tm, D), BF16), pltpu.VMEM((tm, D), F32)]
    operands = [dxo, x, g, a, b, w1t, w3t, w2]
    if scatter is not None:
        in_specs.append(ANY)
        out_specs.append(ANY)
        out_shape.append(jax.ShapeDtypeStruct((3,) + scatter.shape[1:], scatter.dtype))
        scratch += [pltpu.SemaphoreType.DMA((3,)), pltpu.SemaphoreType.DMA((3,))]
        operands.append(scatter)
    return pl.pallas_call(
        body, name=name, grid=(ni, nj), in_specs=in_specs, out_specs=out_specs, out_shape=out_shape,
        scratch_shapes=scratch, compiler_params=_cp("arbitrary", "arbitrary"),
    )(*operands)


def _tn_matmul(a, b, name):
    S, M = a.shape
    N = b.shape[1]
    ts = min(MATMUL_TOKEN_TILE, S)
    tmm = next(t for t in WGRAD_ROW_TILES if M % t == 0)
    ns = S // ts

    def body(a_ref, b_ref, o_ref, acc):
        s = pl.program_id(1)
        part = _tn(a_ref[...], b_ref[...])

        @pl.when(s == 0)
        def _():
            acc[...] = part

        @pl.when(s > 0)
        def _():
            acc[...] += part

        @pl.when(s == ns - 1)
        def _():
            o_ref[...] = acc[...].astype(BF16)

    return pl.pallas_call(
        body, name=name, grid=(M // tmm, ns),
        in_specs=[pl.BlockSpec((ts, tmm), lambda m, s: (s, m)),
                  pl.BlockSpec((ts, N), lambda m, s: (s, 0))],
        out_specs=pl.BlockSpec((tmm, N), lambda m, s: (m, 0)),
        out_shape=jax.ShapeDtypeStruct((M, N), BF16),
        scratch_shapes=[pltpu.VMEM((tmm, N), F32)],
        compiler_params=_cp("arbitrary", "arbitrary"),
    )(a, b)


def _norm_matmul_nt(x, g, wt, out_dtype, name):
    S, D = x.shape
    N = wt.shape[0]
    tm = min(MATMUL_TOKEN_TILE, S)
    tn = next(t for t in (1024, 768, 256) if N % t == 0)

    def body(x_ref, g_ref, w_ref, o_ref, h_ref, hs):
        @pl.when(pl.program_id(1) == 0)
        def _():
            hb = _norm_fwd(x_ref[...], g_ref[...]).astype(BF16)
            hs[...] = hb
            h_ref[...] = hb

        o_ref[...] = _nt(hs[...], w_ref[...]).astype(out_dtype)

    return pl.pallas_call(
        body, name=name, grid=(S // tm, N // tn),
        in_specs=[pl.BlockSpec((tm, D), lambda i, j: (i, 0)),
                  pl.BlockSpec((1, D), lambda i, j: (0, 0)),
                  pl.BlockSpec((tn, D), lambda i, j: (j, 0))],
        out_specs=[pl.BlockSpec((tm, tn), lambda i, j: (i, j)),
                   pl.BlockSpec((tm, D), lambda i, j: (i, 0))],
        out_shape=[jax.ShapeDtypeStruct((S, N), out_dtype), jax.ShapeDtypeStruct((S, D), BF16)],
        scratch_shapes=[pltpu.VMEM((tm, D), BF16)],
        compiler_params=_cp("arbitrary", "arbitrary"),
    )(x, g, wt)


def _heads_tile(ref):
    Hh, nbk = ref.shape[0], ref.shape[1]
    return jnp.concatenate([jnp.concatenate([ref[h, b] for b in range(nbk)], axis=1) for h in range(Hh)], axis=0)


def _store_heads(ref, val):
    Hh, nbk, dh, T = ref.shape
    for h in range(Hh):
        for b in range(nbk):
            ref[h, b] = val[h * dh:(h + 1) * dh, b * T:(b + 1) * T].astype(ref.dtype)


def _norm_proj_heads(x, g, w_rows, T, name):
    S, D = x.shape
    N = w_rows.shape[0]
    tm, tn = min(MATMUL_TOKEN_TILE, S), 768

    def body(x_ref, g_ref, w_ref, o_ref, hs):
        @pl.when(pl.program_id(1) == 0)
        def _():
            hs[...] = _norm_fwd(x_ref[...], g_ref[...]).astype(BF16)

        _store_heads(o_ref, _nt(w_ref[...], hs[...]))

    return pl.pallas_call(
        body, name=name, grid=(S // tm, N // tn),
        in_specs=[pl.BlockSpec((tm, D), lambda i, j: (i, 0)),
                  pl.BlockSpec((1, D), lambda i, j: (0, 0)),
                  pl.BlockSpec((tn, D), lambda i, j: (j, 0))],
        out_specs=pl.BlockSpec((tn // HEAD_DIM, tm // T, HEAD_DIM, T), lambda i, j: (j, i, 0, 0)),
        out_shape=jax.ShapeDtypeStruct((N // HEAD_DIM, S // T, HEAD_DIM, T), BF16),
        scratch_shapes=[pltpu.VMEM((tm, D), BF16)],
        compiler_params=_cp("arbitrary", "arbitrary"),
    )(x, g, w_rows)


def _heads_matmul(pieces, b, name):
    S, N = b.shape
    ts = min(MATMUL_TOKEN_TILE, S)
    ns = S // ts
    rows = [at.shape[0] * at.shape[2] for at, _ in pieces]

    def body(*refs):
        a_refs, b_ref, o_ref = refs[:-2], refs[-2], refs[-1]
        s = pl.program_id(0)
        row0 = 0
        for a_ref, (_, scale), n in zip(a_refs, pieces, rows):
            a = _heads_tile(a_ref)
            part = _nn((a if scale == 1.0 else a * scale).astype(BF16), b_ref[...])
            out = o_ref.at[row0:row0 + n, :]
            row0 += n

            @pl.when(s == 0)
            def _():
                out[...] = part

            @pl.when(s > 0)
            def _():
                out[...] += part

    return pl.pallas_call(
        body, name=name, grid=(ns,),
        in_specs=[pl.BlockSpec((at.shape[0], ts // at.shape[3], at.shape[2], at.shape[3]), lambda s: (0, s, 0, 0))
                  for at, _ in pieces] + [pl.BlockSpec((ts, N), lambda s: (s, 0))],
        out_specs=pl.BlockSpec((sum(rows), N), lambda s: (0, 0)),
        out_shape=jax.ShapeDtypeStruct((sum(rows), N), F32),
        compiler_params=_cp("arbitrary"),
    )(*[at for at, _ in pieces], b)


def _proj_bwd(pieces, dgates, w_rows, x, g, dres):
    S, D = x.shape
    tm = min(TOKEN_TILE, S)
    n_p = len(pieces)
    gate_row = w_rows.shape[0] - dgates.shape[1]

    def body(*refs):
        p_refs = refs[:n_p]
        dgt_ref, w_ref, x_ref, g_ref, dres_ref, dx_ref, dg_ref = refs[n_p:]
        i = pl.program_id(0)
        dh = _nn(dgt_ref[...], w_ref[gate_row:, :])
        for p_ref, (arr, row0) in zip(p_refs, pieces):
            rows = arr.shape[0] * arr.shape[2]
            dh += _tn(_heads_tile(p_ref).astype(BF16), w_ref[row0:row0 + rows, :])
        dx, dg = _norm_bwd(x_ref[...], g_ref[...], dh)
        dx_ref[...] = dres_ref[...] + dx

        @pl.when(i == 0)
        def _():
            dg_ref[...] = dg

        @pl.when(i > 0)
        def _():
            dg_ref[...] += dg

    row = pl.BlockSpec((tm, D), lambda i: (i, 0))
    vec = pl.BlockSpec((1, D), lambda i: (0, 0))
    p_specs = [pl.BlockSpec((a.shape[0], tm // a.shape[3], a.shape[2], a.shape[3]), lambda i: (0, i, 0, 0))
               for a, _ in pieces]
    return pl.pallas_call(
        body, name="proj_bwd", grid=(S // tm,),
        in_specs=p_specs + [pl.BlockSpec((tm, dgates.shape[1]), lambda i: (i, 0)),
                            pl.BlockSpec(w_rows.shape, lambda i: (0, 0), pipeline_mode=pl.Buffered(1)),
                            row, vec, row],
        out_specs=[row, vec],
        out_shape=[jax.ShapeDtypeStruct((S, D), F32), jax.ShapeDtypeStruct((1, D), F32)],
        compiler_params=_cp("arbitrary"),
    )(*[a for a, _ in pieces], dgates, w_rows, x, g, dres)


def _merge_fwd(x1, gates, oa_t, ob_t, wat, wbt, w_out):
    S, D = x1.shape
    W = wat.shape[1]
    tm = min(TOKEN_TILE, S)

    def body(x_ref, ga_ref, gb_ref, oa_ref, ob_ref, wa_ref, wb_ref, wo_ref,
             x2_ref, mg_ref, ba_ref, bb_ref):
        ba = _nt(_heads_tile(oa_ref).T.astype(BF16), wa_ref[...])
        bb = _nt(_heads_tile(ob_ref).T.astype(BF16), wb_ref[...])
        merged = jax.nn.sigmoid(ga_ref[...]) * ba + jax.nn.sigmoid(gb_ref[...]) * bb
        mb = merged.astype(BF16)
        mg_ref[...] = mb
        ba_ref[...] = ba.astype(BF16)
        bb_ref[...] = bb.astype(BF16)
        x2_ref[...] = x_ref[...] + _nn(mb, wo_ref[...])

    row = pl.BlockSpec((tm, D), lambda i: (i, 0))
    full = lambda r, c: pl.BlockSpec((r, c), lambda i: (0, 0))
    heads = lambda a: pl.BlockSpec((a.shape[0], tm // a.shape[3], a.shape[2], a.shape[3]), lambda i: (0, i, 0, 0))
    return pl.pallas_call(
        body, name="merge_fwd", grid=(S // tm,),
        in_specs=[row, pl.BlockSpec((tm, D), lambda i: (i, 0)), pl.BlockSpec((tm, D), lambda i: (i, 1)),
                  heads(oa_t), heads(ob_t), full(D, W), full(D, W), full(D, D)],
        out_specs=[row, row, row, row],
        out_shape=[jax.ShapeDtypeStruct((S, D), F32)] + [jax.ShapeDtypeStruct((S, D), BF16)] * 3,
        compiler_params=_cp("arbitrary"),
    )(x1, gates, gates, oa_t, ob_t, wat, wbt, w_out)


def _merge_bwd(dx2, gates, ba, bb, wa, wb, w_out, t_a, t_b):
    S, D = dx2.shape
    W = wa.shape[0]
    tm = min(TOKEN_TILE, S)
    Hh = W // HEAD_DIM

    def body(dx_ref, ga_ref, gb_ref, ba_ref, bb_ref, wa_ref, wb_ref, wo_ref,
             dxb_ref, dba_ref, dbb_ref, dgt_ref, doa_ref, dob_ref):
        dxb = dx_ref[...].astype(BF16)
        dxb_ref[...] = dxb
        dm = _nt(dxb, wo_ref[...])
        sa = jax.nn.sigmoid(ga_ref[...])
        sb = jax.nn.sigmoid(gb_ref[...])
        dba = (dm * sa).astype(BF16)
        dbb = (dm * sb).astype(BF16)
        dba_ref[...] = dba
        dbb_ref[...] = dbb
        dgt_ref[:, :D] = (dm * ba_ref[...].astype(F32) * sa * (1.0 - sa)).astype(BF16)
        dgt_ref[:, D:] = (dm * bb_ref[...].astype(F32) * sb * (1.0 - sb)).astype(BF16)
        _store_heads(doa_ref, _nt(wa_ref[...], dba))
        _store_heads(dob_ref, _nt(wb_ref[...], dbb))

    row = pl.BlockSpec((tm, D), lambda i: (i, 0))
    full = lambda r, c: pl.BlockSpec((r, c), lambda i: (0, 0))
    heads = lambda T: pl.BlockSpec((Hh, tm // T, HEAD_DIM, T), lambda i: (0, i, 0, 0))
    return pl.pallas_call(
        body, name="merge_bwd", grid=(S // tm,),
        in_specs=[row, pl.BlockSpec((tm, D), lambda i: (i, 0)), pl.BlockSpec((tm, D), lambda i: (i, 1)),
                  row, row, full(W, D), full(W, D), full(D, D)],
        out_specs=[row, row, row, pl.BlockSpec((tm, 2 * D), lambda i: (i, 0)), heads(t_a), heads(t_b)],
        out_shape=[jax.ShapeDtypeStruct((S, D), BF16)] * 3 + [jax.ShapeDtypeStruct((S, 2 * D), BF16),
                   jax.ShapeDtypeStruct((Hh, S // t_a, HEAD_DIM, t_a), F32),
                   jax.ShapeDtypeStruct((Hh, S // t_b, HEAD_DIM, t_b), BF16)],
        compiler_params=_cp("arbitrary"),
    )(dx2, gates, gates, ba, bb, wa, wb, w_out)


def _final_loss(x3, gf, target):
    S, D = x3.shape
    tm = min(TOKEN_TILE, S)

    def body(x_ref, g_ref, t_ref, loss_ref, dx_ref, dg_ref):
        i = pl.program_id(0)
        x = x_ref[...]
        g = g_ref[...]
        e = _norm_fwd(x, g) - t_ref[...]
        part = 0.5 * jnp.sum(jnp.mean(e * e, axis=-1, keepdims=True), axis=0, keepdims=True)
        dx, dg = _norm_bwd(x, g, e * (1.0 / D))
        dx_ref[...] = dx

        @pl.when(i == 0)
        def _():
            loss_ref[...] = part
            dg_ref[...] = dg

        @pl.when(i > 0)
        def _():
            loss_ref[...] += part
            dg_ref[...] += dg

    row = pl.BlockSpec((tm, D), lambda i: (i, 0))
    vec = pl.BlockSpec((1, D), lambda i: (0, 0))
    return pl.pallas_call(
        body, name="final_loss", grid=(S // tm,),
        in_specs=[row, vec, row],
        out_specs=[pl.BlockSpec((1, 1), lambda i: (0, 0)), row, vec],
        out_shape=[jax.ShapeDtypeStruct((1, 1), F32), jax.ShapeDtypeStruct((S, D), F32),
                   jax.ShapeDtypeStruct((1, D), F32)],
        compiler_params=_cp("arbitrary"),
    )(x3, gf, target)


SB_FWD_HEAD_GROUP = 8
SB_HEAD_GROUP = 4
LANES = 128


def _tri(T, kind):
    r = lax.broadcasted_iota(jnp.int32, (T, T), 0)
    c = lax.broadcasted_iota(jnp.int32, (T, T), 1)
    return {"after": r > c, "upto": r <= c, "before": r < c}[kind].astype(BF16)


def _lane(v, j):
    return jnp.broadcast_to(v[:, j:j + 1], (v.shape[0], LANES))


def _t_bf16(x):
    return x.astype(F32).T.astype(BF16)


def _wide(v, T):
    return jnp.tile(v, (1, T // LANES))


SB_SLOTS = 4
SB_FWD_SLOTS = 2
COPY_PARTS = 4


class _split_copy:
    def __init__(self, src, dst, sems):
        n = src.shape[0] // COPY_PARTS
        self.parts = [pltpu.make_async_copy(src.at[pl.ds(r * n, n)], dst.at[pl.ds(r * n, n)], sems.at[r])
                      for r in range(COPY_PARTS)]

    def start(self):
        for cp in self.parts:
            cp.start()

    def wait(self):
        for cp in self.parts:
            cp.wait()


def _sb_pair(i, kb):
    return (i * (i + 1)) // 2 + kb


def _sb_fwd(qkv):
    H3, nb, dh, T = qkv.shape
    H = H3 // 3
    HG = SB_FWD_HEAD_GROUP
    assert HG == H, "one head group: a saved tile holds all the heads"
    n_pairs = (nb * (nb + 1)) // 2

    def body(q_ref, k_ref, v_ref, o_ref, saved_ref, stage, sems):
        row = lax.broadcasted_iota(jnp.int32, (T, T), 0)
        col = lax.broadcasted_iota(jnp.int32, (T, T), 1)
        tri = col < row
        after = _tri(T, "after")

        def save(slot, pair):
            return _split_copy(stage.at[slot], saved_ref.at[pair], sems.at[slot])

        def blocks(qs, i, kb, step, carry, diag):
            hs = range(HG)
            slot = step % SB_FWD_SLOTS

            @pl.when(step >= SB_FWD_SLOTS)
            def _():
                save(slot, 0).wait()

            z = [_nn(qs[hh], k_ref[hh, kb]) for hh in hs]
            res, ls, first = [None] * HG, [None] * HG, [None] * HG
            for hh in hs:
                sp = _softplus(z[hh])
                if diag:
                    sp = jnp.where(tri, sp, 0.0)
                ls[hh] = z[hh] - sp
                spb = sp.astype(BF16)
                first[hh] = _lane(spb.astype(F32), 0)
                res[hh] = _nn(spb, after)
            out = []
            for hh in hs:
                c, oacc = carry[2 * hh], carry[2 * hh + 1]
                a = jnp.exp(ls[hh] - (res[hh] + _wide(c, T)))
                if diag:
                    a = jnp.where(tri, a, 0.0)
                ab = a.astype(BF16)
                stage[slot, hh, 0] = ab
                stage[slot, hh, 1] = jnp.exp(ls[hh]).astype(BF16)
                out.extend([c + (first[hh] + _lane(res[hh], 0)), oacc + _nt(v_ref[hh, kb], ab)])
            save(slot, _sb_pair(i, kb)).start()
            return tuple(out)

        def qblock(i, step):
            qs = [_t_bf16(q_ref[hh, i]) for hh in range(HG)]
            carry = blocks(qs, i, i, step, (jnp.zeros((T, LANES), F32), jnp.zeros((dh, T), F32)) * HG, True)

            def kstep(t, carry):
                return blocks(qs, i, i - 1 - t, step + 1 + t, carry, False)

            carry = lax.fori_loop(0, i, kstep, carry)
            for hh in range(HG):
                o_ref[hh, i] = carry[2 * hh + 1]
            return step + 1 + i

        lax.fori_loop(0, nb, qblock, 0)
        for slot in range(min(SB_FWD_SLOTS, n_pairs)):
            save(slot, 0).wait()

    ht = lambda part: pl.BlockSpec((HG, nb, dh, T), lambda h: (part, 0, 0, 0), pipeline_mode=pl.Buffered(1))
    return pl.pallas_call(
        body, name="sb_fwd", grid=(1,),
        in_specs=[ht(0), ht(1), ht(2)],
        out_specs=[ht(0), ANY],
        out_shape=[jax.ShapeDtypeStruct((H, nb, dh, T), F32),
                   jax.ShapeDtypeStruct((n_pairs, H, 2, T, T), BF16)],
        scratch_shapes=[pltpu.VMEM((SB_FWD_SLOTS, HG, 2, T, T), BF16),
                        pltpu.SemaphoreType.DMA((SB_FWD_SLOTS, COPY_PARTS))],
        compiler_params=_cp("arbitrary"),
    )(qkv, qkv, qkv)


def _sb_bwd(qkv, dot, saved):
    H3, nb, dh, T = qkv.shape
    H = H3 // 3
    HG = SB_HEAD_GROUP
    n_pairs = (nb * (nb + 1)) // 2

    def body(qt_ref, k_ref, v_ref, dot_ref, saved_ref, dq_ref, dk_ref, dv_ref, stage, sems):
        head0 = pl.program_id(0) * HG
        row = lax.broadcasted_iota(jnp.int32, (T, T), 0)
        col = lax.broadcasted_iota(jnp.int32, (T, T), 1)
        tri = col < row
        before = _tri(T, "before")
        dk_ref[...] = jnp.zeros_like(dk_ref)
        dv_ref[...] = jnp.zeros_like(dv_ref)

        def fetch(slot, pair):
            return _split_copy(saved_ref.at[pair, pl.ds(head0, HG)], stage.at[slot], sems.at[slot])

        for ahead in range(min(SB_SLOTS - 1, n_pairs)):
            fetch(ahead, ahead).start()

        def blocks(qTs, dos, doTs, i, kb, carry, diag):
            hs = range(HG)
            pair = _sb_pair(i, kb)
            slot = pair % SB_SLOTS
            fetch(slot, pair).wait()
            nxt = pair + (SB_SLOTS - 1)

            @pl.when(nxt < n_pairs)
            def _():
                fetch(nxt % SB_SLOTS, nxt).start()

            kT = [k_ref[hh, kb] for hh in hs]
            da = [_nn(dos[hh], v_ref[hh, kb]) for hh in hs]
            g, gb, resg = [None] * HG, [None] * HG, [None] * HG
            for hh in hs:
                g[hh] = stage[slot, hh, 0].astype(F32) * da[hh]
                gb[hh] = g[hh].astype(BF16)
                resg[hh] = _nn(gb[hh], before)
            out = []
            for hh in hs:
                pre_g, dq = carry[2 * hh], carry[2 * hh + 1]
                dz = g[hh] - (g[hh] + (resg[hh] + _wide(pre_g, T))) * stage[slot, hh, 1].astype(F32)
                if diag:
                    dz = jnp.where(tri, dz, 0.0)
                dzb = dz.astype(BF16)
                dk_ref[hh, kb] += _nn(qTs[hh], dzb)
                dv_ref[hh, kb] += _nn(doTs[hh], stage[slot, hh, 0])
                out.extend([pre_g + (_lane(resg[hh], T - 1) + _lane(gb[hh].astype(F32), T - 1)),
                            dq + _nt(kT[hh], dzb)])
            return tuple(out)

        def qblock(i, _):
            qTs = [qt_ref[hh, i] for hh in range(HG)]
            doTs = [dot_ref[hh, i] for hh in range(HG)]
            dos = [_t_bf16(v) for v in doTs]
            carry = (jnp.zeros((T, LANES), F32), jnp.zeros((dh, T), F32)) * HG

            def kstep(kb, carry):
                return blocks(qTs, dos, doTs, i, kb, carry, False)

            carry = lax.fori_loop(0, i, kstep, carry)
            carry = blocks(qTs, dos, doTs, i, i, carry, True)
            for hh in range(HG):
                dq_ref[hh, i] = carry[2 * hh + 1]
            return 0

        lax.fori_loop(0, nb, qblock, 0)

    G = H // HG
    ht = lambda part: pl.BlockSpec((HG, nb, dh, T), lambda h: (h + part * G, 0, 0, 0),
                                   pipeline_mode=pl.Buffered(1))
    return pl.pallas_call(
        body, name="sb_bwd", grid=(G,),
        in_specs=[ht(0), ht(1), ht(2), ht(0), ANY],
        out_specs=[ht(0), ht(0), ht(0)],
        out_shape=[jax.ShapeDtypeStruct((H, nb, dh, T), F32)] * 3,
        scratch_shapes=[pltpu.VMEM((SB_SLOTS, HG, 2, T, T), BF16), pltpu.SemaphoreType.DMA((SB_SLOTS, COPY_PARTS))],
        compiler_params=_cp("arbitrary"),
    )(qkv, qkv, qkv, dot, saved)


def _swa_probs(zp, zc, bias, sink, first):
    T = zp.shape[0]
    key = lax.broadcasted_iota(jnp.int32, (T, T), 0)
    qry = lax.broadcasted_iota(jnp.int32, (T, T), 1)
    lp = jnp.where(jnp.logical_and(key > qry, jnp.logical_not(first)), zp + bias[:T, :], NEG_BIG)
    lc = jnp.where(key <= qry, zc + bias[T:, :], NEG_BIG)
    m = jnp.maximum(jnp.maximum(jnp.max(lp, axis=0, keepdims=True), jnp.max(lc, axis=0, keepdims=True)), sink)
    pp = jnp.exp(lp - m)
    pc = jnp.exp(lc - m)
    ps = jnp.exp(sink - m)
    inv = 1.0 / (jnp.sum(pp, axis=0, keepdims=True) + jnp.sum(pc, axis=0, keepdims=True) + ps)
    return pp * inv, pc * inv, ps * inv


def _swa_specs(nb, dh, T, Hq, Hkv, clamp):
    blk = (lambda n: jnp.minimum(n, nb - 1)) if clamp else (lambda n: n)
    q = pl.BlockSpec((Hq, None, dh, T), lambda n: (0, blk(n), 0, 0))
    kv = lambda first, back: pl.BlockSpec(
        (Hkv, None, dh, T), lambda n: (first // Hkv, jnp.maximum(blk(n) - back, 0) if back else blk(n), 0, 0))
    return q, [kv(Hq, 1), kv(Hq, 0), kv(Hq + Hkv, 1), kv(Hq + Hkv, 0)]


def _swa_fwd(qkv, bias, sinks):
    Hq, Hkv, grp = SWA_Q_HEADS, SWA_KV_HEADS, SWA_GROUP
    _, nb, dh, T = qkv.shape

    def body(sink_ref, q_ref, kp_ref, kc_ref, vp_ref, vc_ref, bias_ref, o_ref):
        n = pl.program_id(0)
        kpn = [_t_bf16(kp_ref[hk]) for hk in range(Hkv)]
        kcn = [_t_bf16(kc_ref[hk]) for hk in range(Hkv)]
        zs = [(_nn(kpn[h // grp], q_ref[h]), _nn(kcn[h // grp], q_ref[h])) for h in range(Hq)]
        for h in range(Hq):
            pp, pc, _ = _swa_probs(*zs[h], bias_ref[h], sink_ref[h], n == 0)
            o_ref[h] = _nn(vp_ref[h // grp], pp.astype(BF16)) + _nn(vc_ref[h // grp], pc.astype(BF16))

    q_spec, kv_specs = _swa_specs(nb, dh, T, Hq, Hkv, False)
    return pl.pallas_call(
        body, name="swa_fwd", grid=(nb,),
        in_specs=[pl.BlockSpec(memory_space=pltpu.SMEM), q_spec] + kv_specs
                 + [pl.BlockSpec((Hq, 2 * T, T), lambda n: (0, 0, 0))],
        out_specs=pl.BlockSpec((Hq, None, dh, T), lambda n: (0, n, 0, 0)),
        out_shape=jax.ShapeDtypeStruct((Hq, nb, dh, T), F32),
        compiler_params=_cp("arbitrary"),
    )(sinks, qkv, qkv, qkv, qkv, qkv, bias)


def _swa_bwd(qkv, bias, sinks, dot, ot):
    Hq, Hkv, grp = SWA_Q_HEADS, SWA_KV_HEADS, SWA_GROUP
    _, nb, dh, T = qkv.shape

    def body(sink_ref, qt_ref, kp_ref, kc_ref, vp_ref, vc_ref, bias_ref, dot_ref, ot_ref,
             dq_ref, dk_ref, dv_ref, dbias_ref, dsink_ref, ck, cv):
        n = pl.program_id(0)

        @pl.when(n == 0)
        def _():
            dbias_ref[...] = jnp.zeros_like(dbias_ref)
            dsink_ref[...] = jnp.zeros_like(dsink_ref)
            ck[...] = jnp.zeros_like(ck)
            cv[...] = jnp.zeros_like(cv)

        @pl.when(n < nb)
        def _():
            kp, kc = [kp_ref[hk] for hk in range(Hkv)], [kc_ref[hk] for hk in range(Hkv)]
            kpn, kcn = [_t_bf16(v) for v in kp], [_t_bf16(v) for v in kc]
            vpn = [_t_bf16(vp_ref[hk]) for hk in range(Hkv)]
            vcn = [_t_bf16(vc_ref[hk]) for hk in range(Hkv)]
            qTs = [qt_ref[h] for h in range(Hq)]
            doTs = [dot_ref[h].astype(BF16) for h in range(Hq)]
            zs = [(_nn(kpn[h // grp], qTs[h]), _nn(kcn[h // grp], qTs[h])) for h in range(Hq)]
            dps = [(_nn(vpn[h // grp], doTs[h]), _nn(vcn[h // grp], doTs[h])) for h in range(Hq)]
            dls, pbs = [], []
            for h in range(Hq):
                pp, pc, ps = _swa_probs(*zs[h], bias_ref[h], sink_ref[h], n == 0)
                delta = jnp.sum(dot_ref[h] * ot_ref[h], axis=0, keepdims=True)
                dlp = pp * (dps[h][0] - delta)
                dlc = pc * (dps[h][1] - delta)
                dbias_ref[h, :T, :] += dlp
                dbias_ref[h, T:, :] += dlc
                dsink_ref[h] += -ps * delta
                dls.append((dlp.astype(BF16), dlc.astype(BF16)))
                pbs.append((pp.astype(BF16), pc.astype(BF16)))
            zero = jnp.zeros((dh, T), F32)
            kprev, kcur, vprev, vcur = [zero] * Hkv, [zero] * Hkv, [zero] * Hkv, [zero] * Hkv
            for h in range(Hq):
                hk = h // grp
                dlpb, dlcb = dls[h]
                dq_ref[h] = _nn(kp[hk], dlpb) + _nn(kc[hk], dlcb)
                kprev[hk] = kprev[hk] + _nt(qTs[h], dlpb)
                kcur[hk] = kcur[hk] + _nt(qTs[h], dlcb)
                vprev[hk] = vprev[hk] + _nt(doTs[h], pbs[h][0])
                vcur[hk] = vcur[hk] + _nt(doTs[h], pbs[h][1])
            for hk in range(Hkv):
                dk_ref[hk] = ck[hk] + kprev[hk]
                dv_ref[hk] = cv[hk] + vprev[hk]
                ck[hk] = kcur[hk]
                cv[hk] = vcur[hk]

        @pl.when(n == nb)
        def _():
            dk_ref[...] = ck[...]
            dv_ref[...] = cv[...]

    qt_spec, kv_specs = _swa_specs(nb, dh, T, Hq, Hkv, True)
    prev = pl.BlockSpec((Hkv, None, dh, T), lambda n: (0, jnp.maximum(n - 1, 0), 0, 0))
    whole = lambda a, b: pl.BlockSpec((Hq, a, b), lambda n: (0, 0, 0))
    return pl.pallas_call(
        body, name="swa_bwd", grid=(nb + 1,),
        in_specs=[pl.BlockSpec(memory_space=pltpu.SMEM), qt_spec] + kv_specs
                 + [whole(2 * T, T), qt_spec, qt_spec],
        out_specs=[qt_spec, prev, prev, whole(2 * T, T), whole(1, T)],
        out_shape=[jax.ShapeDtypeStruct((Hq, nb, dh, T), F32), jax.ShapeDtypeStruct((Hkv, nb, dh, T), F32),
                   jax.ShapeDtypeStruct((Hkv, nb, dh, T), F32), jax.ShapeDtypeStruct((Hq, 2 * T, T), F32),
                   jax.ShapeDtypeStruct((Hq, 1, T), F32)],
        scratch_shapes=[pltpu.VMEM((Hkv, dh, T), F32), pltpu.VMEM((Hkv, dh, T), F32)],
        compiler_params=_cp("arbitrary"),
    )(sinks, qkv, qkv, qkv, qkv, qkv, bias, dot, ot)


def _split3(x):
    h1 = x.astype(BF16)
    r1 = x - h1.astype(F32)
    h2 = r1.astype(BF16)
    h3 = (r1 - h2.astype(F32)).astype(BF16)
    return h1, h2, h3


def _bias_expand(rel_t, onehot):
    Hq, NB = rel_t.shape
    L = onehot.shape[1]

    def body(r_ref, oh_ref, o_ref):
        h1, h2, h3 = _split3(r_ref[...])
        oh = oh_ref[...]
        o_ref[...] = _nn(h1, oh) + _nn(h2, oh) + _nn(h3, oh)

    return pl.pallas_call(
        body, name="bias_expand", grid=(1,),
        in_specs=[pl.BlockSpec((Hq, NB), lambda i: (0, 0)), pl.BlockSpec((NB, L), lambda i: (0, 0))],
        out_specs=pl.BlockSpec((Hq, L), lambda i: (0, 0)),
        out_shape=jax.ShapeDtypeStruct((Hq, L), F32),
        compiler_params=_cp("arbitrary"),
    )(rel_t, onehot)


def _bias_reduce(dbias, onehot):
    Hq, L = dbias.shape
    NB = onehot.shape[0]

    def body(d_ref, oh_ref, o_ref):
        h1, h2, h3 = _split3(d_ref[...])
        oh = oh_ref[...]
        o_ref[...] = _nt(h1, oh) + _nt(h2, oh) + _nt(h3, oh)

    return pl.pallas_call(
        body, name="bias_reduce", grid=(1,),
        in_specs=[pl.BlockSpec((Hq, L), lambda i: (0, 0)), pl.BlockSpec((NB, L), lambda i: (0, 0))],
        out_specs=pl.BlockSpec((Hq, NB), lambda i: (0, 0)),
        out_shape=jax.ShapeDtypeStruct((Hq, NB), F32),
        compiler_params=_cp("arbitrary"),
    )(dbias, onehot)


def _adamw(w, g, m, v, name):
    R, C = w.shape
    tr = 256 if R % 256 == 0 else R
    bc1 = 1.0 - ADAM_B1 ** ADAM_STEP
    bc2 = 1.0 - ADAM_B2 ** ADAM_STEP

    def body(w_ref, g_ref, m_ref, v_ref, d_ref, nm_ref, nv_ref):
        g = g_ref[...]
        m2 = ADAM_B1 * m_ref[...] + (1.0 - ADAM_B1) * g
        v2 = ADAM_B2 * v_ref[...] + (1.0 - ADAM_B2) * (g * g)
        nm_ref[...] = m2
        nv_ref[...] = v2
        d_ref[...] = -ADAM_LR * ((m2 / bc1) / (jnp.sqrt(v2 / bc2) + ADAM_EPS) + ADAM_WD * w_ref[...])

    spec = pl.BlockSpec((tr, C), lambda i: (i, 0))
    return pl.pallas_call(
        body, name=name, grid=(R // tr,),
        in_specs=[spec] * 4, out_specs=[spec] * 3,
        out_shape=[jax.ShapeDtypeStruct((R, C), F32)] * 3,
        compiler_params=_cp("arbitrary"),
    )(w, g, m, v)


def _row_tile(R):
    return max(t for t in range(16, 513, 16) if R % t == 0)


def _add_halves(mine, recv, name):
    K, R, C = mine.shape
    tr = _row_tile(R)

    def body(a_ref, b_ref, o_ref, ob_ref):
        s = a_ref[...].astype(F32) + b_ref[...].astype(F32)
        o_ref[...] = s
        ob_ref[...] = s.astype(BF16)

    spec = pl.BlockSpec((None, tr, C), lambda k, i: (k, i, 0))
    return pl.pallas_call(
        body, name=name, grid=(K, R // tr),
        in_specs=[spec, spec], out_specs=[spec, spec],
        out_shape=[jax.ShapeDtypeStruct((K, R, C), F32), jax.ShapeDtypeStruct((K, R, C), BF16)],
        compiler_params=_cp("arbitrary", "arbitrary"),
    )(mine, recv)


def _add_received(own, recv, name):
    R, C = own.shape
    tr = _row_tile(R)

    def body(a_ref, r_ref, o_ref):
        o_ref[...] = ((a_ref[...] + r_ref[0].astype(F32)) + r_ref[1].astype(F32)) + r_ref[2].astype(F32)

    return pl.pallas_call(
        body, name=name, grid=(R // tr,),
        in_specs=[pl.BlockSpec((tr, C), lambda i: (i, 0)), pl.BlockSpec((3, tr, C), lambda i: (0, i, 0))],
        out_specs=pl.BlockSpec((tr, C), lambda i: (i, 0)),
        out_shape=jax.ShapeDtypeStruct((R, C), F32),
        compiler_params=_cp("arbitrary"),
    )(own, recv)


def _position():
    x, y, c = lax.axis_index("x"), lax.axis_index("y"), lax.axis_index("c")
    others = [(1 - x, y), (x, 1 - y), (1 - x, 1 - y)]
    return x, y, c, others


def _remote(src, dst, send_sems, recv_sems, k, dev):
    return pltpu.make_async_remote_copy(src_ref=src, dst_ref=dst, send_sem=send_sems.at[k],
                                        recv_sem=recv_sems.at[k], device_id=dev, device_id_type=MESH_ID)


class _gather_exchange:
    def __init__(self, src, out, send_sems, recv_sems):
        x, y, c, others = _position()
        mine, sibling = 2 * x + y, (x, y, 1 - c)
        self.sends, self.arrivals, self.passes, self.from_sibling = [], [], [], []
        for j, (ox, oy) in enumerate(others):
            slot = out.at[2 * ox + oy, c]
            theirs = out.at[2 * ox + oy, 1 - c]
            self.sends.append(_remote(src.at[c], out.at[mine, c], send_sems, recv_sems, j, (ox, oy, c)))
            self.arrivals.append(_remote(slot, slot, send_sems, recv_sems, j, (ox, oy, c)))
            self.passes.append(_remote(slot, slot, send_sems, recv_sems, 3 + j, sibling))
            self.from_sibling.append(_remote(theirs, theirs, send_sems, recv_sems, 3 + j, sibling))

    def start(self):
        for cp in self.sends:
            cp.start()

    def forward(self):
        for arrived, onward in zip(self.arrivals, self.passes):
            arrived.wait_recv()
            onward.start()

    def finish(self):
        for cp in self.from_sibling:
            cp.wait_recv()
        for cp in self.sends + self.passes:
            cp.wait_send()


def _gather_weights(shard):
    R, C = shard.shape
    half = R // 2

    def body(src, out, send_sems, recv_sems):
        ex = _gather_exchange(src, out, send_sems, recv_sems)
        ex.start()
        ex.forward()
        ex.finish()

    return pl.pallas_call(
        body, name="gather_weights",
        in_specs=[ANY], out_specs=ANY,
        out_shape=jax.ShapeDtypeStruct((N_CHIPS, 2, half, C), shard.dtype),
        scratch_shapes=[pltpu.SemaphoreType.DMA((6,)), pltpu.SemaphoreType.DMA((6,))],
    )(shard.reshape(2, half, C)).reshape(N_CHIPS, R, C)


def _swap_halves(grads, name):
    K, R, C = grads.shape
    half = R // 2

    def body(src, out, send_sems, recv_sems):
        x, y, c, _ = _position()
        theirs = src.at[:, pl.ds(pl.multiple_of((1 - c) * half, 16), half), :]
        cp = _remote(theirs, out, send_sems, recv_sems, 0, (x, y, 1 - c))
        cp.start()
        cp.wait()

    return pl.pallas_call(
        body, name=name,
        in_specs=[ANY], out_specs=ANY,
        out_shape=jax.ShapeDtypeStruct((K, half, C), grads.dtype),
        scratch_shapes=[pltpu.SemaphoreType.DMA((1,)), pltpu.SemaphoreType.DMA((1,))],
    )(grads)


class _scatter_exchange:
    def __init__(self, src, out, send_sems, recv_sems):
        x, y, c, others = _position()
        self.copies = [_remote(src.at[2 * ox + oy], out.at[j], send_sems, recv_sems, j, (ox, oy, c))
                       for j, (ox, oy) in enumerate(others)]

    def start(self):
        for cp in self.copies:
            cp.start()

    def finish(self):
        for cp in self.copies:
            cp.wait()


def _scatter_to_owners(parts, name):
    K, H, C = parts.shape

    def body(src, out, send_sems, recv_sems):
        ex = _scatter_exchange(src, out, send_sems, recv_sems)
        ex.start()
        ex.finish()

    return pl.pallas_call(
        body, name=name,
        in_specs=[ANY], out_specs=ANY,
        out_shape=jax.ShapeDtypeStruct((3, H, C), parts.dtype),
        scratch_shapes=[pltpu.SemaphoreType.DMA((3,)), pltpu.SemaphoreType.DMA((3,))],
    )(parts)


def _swap_reduced(half_rows, name):
    H, C = half_rows.shape

    def body(src, out, send_sems, recv_sems):
        x, y, c, _ = _position()
        cp = _remote(src, out, send_sems, recv_sems, 0, (x, y, 1 - c))
        cp.start()
        cp.wait()

    return pl.pallas_call(
        body, name=name,
        in_specs=[ANY], out_specs=ANY,
        out_shape=jax.ShapeDtypeStruct((H, C), half_rows.dtype),
        scratch_shapes=[pltpu.SemaphoreType.DMA((1,)), pltpu.SemaphoreType.DMA((1,))],
    )(half_rows)


def _allreduce_small(block):
    R, C = block.shape
    n_dev = 8

    def body(src, out, slots, send_sems, recv_sems):
        x, y, c, _ = _position()
        me = 4 * x + 2 * y + c
        slots[me] = src[...]
        sends = []
        for r in range(1, n_dev):
            peer = (x ^ (r >> 2), y ^ ((r >> 1) & 1), c ^ (r & 1))
            cp = _remote(src, slots.at[me], send_sems, recv_sems, r - 1, peer)
            cp.start()
            sends.append(cp)
        for r in range(1, n_dev):
            theirs = slots.at[me ^ r]
            _remote(theirs, theirs, send_sems, recv_sems, r - 1, (x, y, c)).wait_recv()
        for cp in sends:
            cp.wait_send()
        acc = slots[0]
        for d in range(1, n_dev):
            acc = acc + slots[d]
        out[...] = acc

    return pl.pallas_call(
        body, name="allreduce_small",
        in_specs=[pl.BlockSpec(memory_space=pltpu.VMEM)], out_specs=pl.BlockSpec(memory_space=pltpu.VMEM),
        out_shape=jax.ShapeDtypeStruct((R, C), F32),
        scratch_shapes=[pltpu.VMEM((n_dev, R, C), F32), pltpu.SemaphoreType.DMA((7,)), pltpu.SemaphoreType.DMA((7,))],
    )(block)


def _rel_bucket(dist):
    max_exact = REL_BUCKETS // 2
    d = jnp.maximum(dist, 1).astype(F32)
    large = max_exact + (jnp.log(d / max_exact) / math.log(REL_MAX_DIST / max_exact)
                         * (REL_BUCKETS - max_exact)).astype(jnp.int32)
    large = jnp.minimum(large, REL_BUCKETS - 1)
    return jnp.where(dist < max_exact, dist, large)


def _bucket_onehot():
    T = SWA_BLOCK
    dist = (jnp.arange(T)[None, :] + T) - jnp.arange(2 * T)[:, None]
    bucket = _rel_bucket(jnp.maximum(dist, 0)).reshape(1, T * 2 * T)
    return (bucket == jnp.arange(REL_BUCKETS)[:, None]).astype(BF16)


_BUF = (("ffn1_w1", "t"), ("ffn1_w3", "t"), ("ffn1_w2", "n"), ("ffn2_w1", "t"), ("ffn2_w3", "t"),
        ("ffn2_w2", "n"), ("w_in", "t"), ("w_out", "n"), ("w_branch_swa", "tw"), ("w_branch_sb", "tw"))


def _to_rows(name_kind, w, D):
    kind = name_kind[1]
    if kind == "n":
        return w
    if kind == "t":
        return w.T
    return w.T.reshape(-1, D)


def _from_rows(name_kind, rows, width):
    kind = name_kind[1]
    if kind == "n":
        return rows
    if kind == "t":
        return rows.T
    return rows.reshape(-1, width).T


def kernel(x, norm_ffn1, ffn1_w1, ffn1_w3, ffn1_w2, norm_mix, w_in, swa_sinks, rel_bias, w_branch_swa, w_branch_sb, w_out, norm_ffn2, ffn2_w1, ffn2_w3, ffn2_w2, norm_final, loss_target, m_norm_ffn1, m_ffn1_w1, m_ffn1_w3, m_ffn1_w2, m_norm_mix, m_w_in, m_swa_sinks, m_rel_bias, m_w_branch_swa, m_w_branch_sb, m_w_out, m_norm_ffn2, m_ffn2_w1, m_ffn2_w3, m_ffn2_w2, m_norm_final, v_norm_ffn1, v_ffn1_w1, v_ffn1_w3, v_ffn1_w2, v_norm_mix, v_w_in, v_swa_sinks, v_rel_bias, v_w_branch_swa, v_w_branch_sb, v_w_out, v_norm_ffn2, v_ffn2_w1, v_ffn2_w3, v_ffn2_w2, v_norm_final):
    names = ["norm_ffn1", "ffn1_w1", "ffn1_w3", "ffn1_w2", "norm_mix", "w_in", "swa_sinks", "rel_bias",
             "w_branch_swa", "w_branch_sb", "w_out", "norm_ffn2", "ffn2_w1", "ffn2_w3", "ffn2_w2", "norm_final"]
    W = dict(zip(names, [norm_ffn1, ffn1_w1, ffn1_w3, ffn1_w2, norm_mix, w_in, swa_sinks, rel_bias,
                         w_branch_swa, w_branch_sb, w_out, norm_ffn2, ffn2_w1, ffn2_w3, ffn2_w2, norm_final]))
    M = dict(zip(names, [m_norm_ffn1, m_ffn1_w1, m_ffn1_w3, m_ffn1_w2, m_norm_mix, m_w_in, m_swa_sinks, m_rel_bias,
                         m_w_branch_swa, m_w_branch_sb, m_w_out, m_norm_ffn2, m_ffn2_w1, m_ffn2_w3, m_ffn2_w2,
                         m_norm_final]))
    V = dict(zip(names, [v_norm_ffn1, v_ffn1_w1, v_ffn1_w3, v_ffn1_w2, v_norm_mix, v_w_in, v_swa_sinks, v_rel_bias,
                         v_w_branch_swa, v_w_branch_sb, v_w_out, v_norm_ffn2, v_ffn2_w1, v_ffn2_w3, v_ffn2_w2,
                         v_norm_final]))
    xs = x[0]
    target = loss_target[0]
    S, D = xs.shape
    QW = SWA_Q_HEADS * HEAD_DIM
    KW = SWA_KV_HEADS * HEAD_DIM
    BW = SB_HEADS * HEAD_DIM
    QKV = QW + 2 * KW + 3 * BW

    pieces = [_to_rows(nk, W[nk[0]][0], D) for nk in _BUF]
    sizes = [p.shape[0] for p in pieces]
    offs = [0]
    for s in sizes:
        offs.append(offs[-1] + s)
    n_first = 3
    first_rows = offs[n_first]
    shard_a = jnp.concatenate(pieces[:n_first], axis=0).astype(BF16)
    shard_b = jnp.concatenate(pieces[n_first:], axis=0).astype(BF16)
    chip = 2 * lax.axis_index("x") + lax.axis_index("y")
    gathered_a = lax.dynamic_update_slice(_gather_weights(shard_a), shard_a[None], (chip, 0, 0))
    f1w1, f1w3, f1w2 = [gathered_a[:, offs[i]:offs[i + 1], :].reshape(N_CHIPS * sizes[i], D) for i in range(n_first)]

    g1, gmix, g3 = W["norm_ffn1"], W["norm_mix"], W["norm_ffn2"]
    gf = W["norm_final"].reshape(1, D)

    x1, h1, a1, b1, gathered_b = _ffn_fwd(xs, g1, f1w1, f1w3, f1w2, "ffn1_fwd", gather=shard_b)
    gathered_b = lax.dynamic_update_slice(gathered_b, shard_b[None], (chip, 0, 0))

    def full(i):
        return gathered_b[:, offs[i] - first_rows:offs[i + 1] - first_rows, :].reshape(N_CHIPS * sizes[i], D)

    f2w1, f2w3, f2w2, w_in_t, w_out_f = [full(i) for i in range(n_first, 8)]
    wa_t = full(8).reshape(D, QW)
    wb_t = full(9).reshape(D, BW)
    o0 = QW + 2 * KW
    rows = jnp.arange(w_in_t.shape[0])
    is_q = (rows < QW) | ((rows >= o0) & (rows < o0 + BW))
    w_in_s = w_in_t * jnp.where(is_q, QK_SCALE, 1.0).astype(BF16)[:, None]
    qkv_a = _norm_proj_heads(x1, gmix, w_in_s[:o0], SWA_BLOCK, "proj_swa")
    qkv_b = _norm_proj_heads(x1, gmix, w_in_s[o0:QKV], SB_BLOCK, "proj_sb")
    gates, h2 = _norm_matmul_nt(x1, gmix, w_in_t[QKV:], F32, "proj_gates")

    onehot = _bucket_onehot()
    bias = _bias_expand(W["rel_bias"].T, onehot).reshape(SWA_Q_HEADS, 2 * SWA_BLOCK, SWA_BLOCK)
    sinks = W["swa_sinks"].reshape(SWA_Q_HEADS)
    oa_t = _swa_fwd(qkv_a, bias, sinks)
    ob_t, saved_sb = _sb_fwd(qkv_b)

    x2, merged, ba, bb = _merge_fwd(x1, gates, oa_t, ob_t, wa_t, wb_t, w_out_f)
    x3, h3, a2, b2 = _ffn_fwd(x2, g3, f2w1, f2w3, f2w2, "ffn2_fwd")
    loss_part, dx3, dgf = _final_loss(x3, gf, target)

    dx2, dg3, dz2, da2, db2, u2 = _ffn_bwd(dx3, x2, g3, a2, b2, f2w1, f2w3, f2w2, "ffn2_bwd")
    grads = {}
    grads["ffn2_w1"] = _tn_matmul(da2, h3, "ffn2_dw1")
    grads["ffn2_w3"] = _tn_matmul(db2, h3, "ffn2_dw3")
    grads["ffn2_w2"] = _tn_matmul(u2, dz2, "ffn2_dw2")

    dx2b, dba, dbb, dgates, doa_t, dob_t = _merge_bwd(dx2, gates, ba, bb, wa_t.T, wb_t.T, w_out_f,
                                                      SWA_BLOCK, SB_BLOCK)
    grads["w_out"] = _tn_matmul(merged, dx2b, "dw_out")
    grads["w_branch_swa"] = _heads_matmul([(oa_t, 1.0)], dba, "dw_branch_swa").T
    grads["w_branch_sb"] = _heads_matmul([(ob_t, 1.0)], dbb, "dw_branch_sb").T

    dqb_t, dkb_t, dvb_t = _sb_bwd(qkv_b, dob_t, saved_sb)
    dqa_t, dka_t, dva_t, dbias, dsink_rows = _swa_bwd(qkv_a, bias, sinks, doa_t, oa_t)
    d_rel = _bias_reduce(dbias.reshape(SWA_Q_HEADS, -1), onehot).T
    d_sinks = jnp.sum(dsink_rows, axis=(1, 2))

    dheads = [(dqa_t, QK_SCALE), (dka_t, 1.0), (dva_t, 1.0), (dqb_t, QK_SCALE), (dkb_t, 1.0), (dvb_t, 1.0)]
    grads["w_in"] = jnp.concatenate([_heads_matmul(dheads, h2, "dw_in_heads").astype(BF16),
                                     _tn_matmul(dgates, h2, "dw_in_gates")], axis=0)
    row0, pieces_in = 0, []
    for a, _ in dheads:
        pieces_in.append((a, row0))
        row0 += a.shape[0] * HEAD_DIM
    dx1, dgmix = _proj_bwd(pieces_in, dgates, w_in_s, x1, gmix, dx2)

    c = lax.axis_index("c")

    def reduce_start(lo, hi, tag):
        gbuf = jnp.concatenate([grads[_BUF[i][0]].astype(BF16).reshape(N_CHIPS, sizes[i], D) for i in range(lo, hi)],
                               axis=1)
        half = gbuf.shape[1] // 2
        from_sibling = _swap_halves(gbuf, "swap_halves_" + tag)
        my_half = lax.dynamic_slice_in_dim(gbuf, c * half, half, axis=1)
        return _add_halves(my_half, from_sibling, "add_sibling_" + tag)

    def reduce_finish(chip_sum, received, tag):
        own = lax.dynamic_index_in_dim(chip_sum, chip, axis=0, keepdims=False)
        my_rows = _add_received(own, received, "add_chips_" + tag)
        their_rows = _swap_reduced(my_rows, "swap_reduced_" + tag)
        return jnp.concatenate([jnp.where(c == 0, my_rows, their_rows), jnp.where(c == 0, their_rows, my_rows)],
                               axis=0)

    sum_b, sum16_b = reduce_start(n_first, len(_BUF), "late")
    dx0, dg1, dz1, da1, db1, u1, received_b = _ffn_bwd(dx1, xs, g1, a1, b1, f1w1, f1w3, f1w2, "ffn1_bwd",
                                                       scatter=sum16_b)
    grads["ffn1_w1"] = _tn_matmul(da1, h1, "ffn1_dw1")
    grads["ffn1_w3"] = _tn_matmul(db1, h1, "ffn1_dw3")
    grads["ffn1_w2"] = _tn_matmul(u1, dz1, "ffn1_dw2")
    sum_a, sum16_a = reduce_start(0, n_first, "first")
    reduced = jnp.concatenate([reduce_finish(sum_a, _scatter_to_owners(sum16_a, "scatter_to_owners"), "first"),
                               reduce_finish(sum_b, received_b, "late")], axis=0)

    small_rows = [dg1, dgmix, dg3, dgf,
                  jnp.pad(d_sinks.reshape(1, -1), ((0, 0), (0, D - SWA_Q_HEADS))),
                  jnp.pad(d_rel.reshape(1, -1), ((0, 0), (0, D - REL_BUCKETS * SWA_Q_HEADS))),
                  jnp.pad(loss_part, ((0, 0), (0, D - 1))), jnp.zeros((1, D), F32)]
    small = _allreduce_small(jnp.concatenate(small_rows, axis=0))
    loss = small[6, 0]

    G = {}
    for i, nk in enumerate(_BUF):
        G[nk[0]] = _from_rows(nk, reduced[offs[i]:offs[i + 1]], W[nk[0]].shape[1])[None]
    G["norm_ffn1"], G["norm_mix"], G["norm_ffn2"] = small[0:1], small[1:2], small[2:3]
    G["norm_final"] = small[3]
    G["swa_sinks"] = small[4:5, :SWA_Q_HEADS]
    G["rel_bias"] = small[5, :REL_BUCKETS * SWA_Q_HEADS].reshape(REL_BUCKETS, SWA_Q_HEADS)

    delta, new_m, new_v = {}, {}, {}
    small_names = ["norm_ffn1", "norm_mix", "norm_ffn2", "norm_final", "swa_sinks", "rel_bias"]

    def pack(d):
        return jnp.concatenate([jnp.pad(d[n].reshape(1, -1), ((0, 0), (0, D - d[n].size))) for n in small_names]
                               + [jnp.zeros((2, D), F32)], axis=0)

    sd, sm, sv = _adamw(pack(W), pack(G), pack(M), pack(V), "adamw_small")
    for r, n in enumerate(small_names):
        for dst, src in ((delta, sd), (new_m, sm), (new_v, sv)):
            dst[n] = src[r, :W[n].size].reshape(W[n].shape)
    for nk in _BUF:
        n = nk[0]
        shp = W[n].shape
        two_d = (shp[1], shp[2])
        d_, m_, v_ = _adamw(W[n].reshape(two_d), G[n].reshape(two_d), M[n].reshape(two_d), V[n].reshape(two_d),
                            "adamw_" + n)
        delta[n], new_m[n], new_v[n] = d_.reshape(shp), m_.reshape(shp), v_.reshape(shp)

    return (loss, dx0[None], *[G[n] for n in names], *[delta[n] for n in names],
            *[new_m[n] for n in names], *[new_v[n] for n in names])
```

```python
import math

import jax
import jax.numpy as jnp
from jax import lax
from jax.experimental import pallas as pl
from jax.experimental.pallas import tpu as pltpu

F32, BF16 = jnp.float32, jnp.bfloat16
MESH_ID = pl.DeviceIdType.MESH
ANY = pl.BlockSpec(memory_space=pl.ANY)

RMS_EPS = 1e-6
HEAD_DIM = 64
SWA_Q_HEADS, SWA_KV_HEADS, SWA_GROUP = 8, 2, 4
SWA_BLOCK = 128
SB_HEADS = 8
SB_BLOCK = 256
REL_BUCKETS, REL_MAX_DIST = 32, 128
NEG_BIG = -1e30
QK_SCALE = HEAD_DIM ** -0.5
ADAM_LR, ADAM_B1, ADAM_B2, ADAM_EPS, ADAM_WD, ADAM_STEP = 0.001, 0.9, 0.999, 1e-08, 0.01, 10

N_CHIPS = 4
TOKEN_TILE = 512
MATMUL_TOKEN_TILE = 1024
WGRAD_ROW_TILES = (2176, 1408, 1024, 256)
FF_TILE = 1408
FFN_TOKEN_TILE = 512
PROJ_HEAD_ROWS = 768
FF_BWD_TILE = 256
VMEM_LIMIT = 56 * 1024 * 1024


def _cp(*sem):
    return pltpu.CompilerParams(dimension_semantics=sem, vmem_limit_bytes=VMEM_LIMIT)


def _nn(a, b):
    return jnp.dot(a, b, preferred_element_type=F32)


def _nt(a, b):
    return lax.dot_general(a, b, (((1,), (1,)), ((), ())), preferred_element_type=F32)


def _tn(a, b):
    return lax.dot_general(a, b, (((0,), (0,)), ((), ())), preferred_element_type=F32)


def _norm_fwd(x, g):
    return x * lax.rsqrt(jnp.mean(x * x, axis=-1, keepdims=True) + RMS_EPS) * g


def _norm_bwd(x, g, dh):
    r = lax.rsqrt(jnp.mean(x * x, axis=-1, keepdims=True) + RMS_EPS)
    xh = x * r
    dxh = dh * g
    dx = r * (dxh - xh * jnp.mean(dxh * xh, axis=-1, keepdims=True))
    return dx, jnp.sum(dh * xh, axis=0, keepdims=True)


SOFTPLUS_LINEAR = 20.0


def _softplus(z):
    return jnp.maximum(jnp.log(1.0 + jnp.exp(jnp.minimum(z, SOFTPLUS_LINEAR))), z)


def _ffn_fwd(x, g, w1t, w3t, w2, name, gather=None):
    S, D = x.shape
    F = w2.shape[0]
    tm, tf = min(FFN_TOKEN_TILE, S), FF_TILE
    ni, nj = S // tm, F // tf

    def body(x_ref, g_ref, w1_ref, w3_ref, w2_ref, *rest):
        if gather is None:
            xo_ref, h_ref, a_ref, b_ref, hs, acc = rest
        else:
            shard_ref, xo_ref, h_ref, a_ref, b_ref, gathered_ref, hs, acc, send_sems, recv_sems = rest
        i, j = pl.program_id(0), pl.program_id(1)
        if gather is not None:
            for when, phase in ((jnp.logical_and(i == 0, j == 0), "start"),
                                (jnp.logical_and(i == ni - 1, j == 0), "forward"),
                                (jnp.logical_and(i == ni - 1, j == nj - 1), "finish")):
                @pl.when(when)
                def _():
                    getattr(_gather_exchange(shard_ref, gathered_ref, send_sems, recv_sems), phase)()

        @pl.when(j == 0)
        def _():
            hb = _norm_fwd(x_ref[...], g_ref[...]).astype(BF16)
            hs[...] = hb
            h_ref[...] = hb
            acc[...] = jnp.zeros_like(acc)

        h = hs[...]
        a = _nt(h, w1_ref[...])
        b = _nt(h, w3_ref[...])
        a_ref[...] = a.astype(BF16)
        b_ref[...] = b.astype(BF16)
        u = a * jax.nn.sigmoid(a) * b
        acc[...] += _nn(u.astype(BF16), w2_ref[...])

        @pl.when(j == nj - 1)
        def _():
            xo_ref[...] = x_ref[...] + 0.5 * acc[...]

    in_specs = [pl.BlockSpec((tm, D), lambda i, j: (i, 0)),
                pl.BlockSpec((1, D), lambda i, j: (0, 0)),
                pl.BlockSpec((tf, D), lambda i, j: (j, 0)),
                pl.BlockSpec((tf, D), lambda i, j: (j, 0)),
                pl.BlockSpec((tf, D), lambda i, j: (j, 0))]
    out_specs = [pl.BlockSpec((tm, D), lambda i, j: (i, 0)),
                 pl.BlockSpec((tm, D), lambda i, j: (i, 0)),
                 pl.BlockSpec((tm, tf), lambda i, j: (i, j)),
                 pl.BlockSpec((tm, tf), lambda i, j: (i, j))]
    out_shape = [jax.ShapeDtypeStruct((S, D), F32), jax.ShapeDtypeStruct((S, D), BF16),
                 jax.ShapeDtypeStruct((S, F), BF16), jax.ShapeDtypeStruct((S, F), BF16)]
    scratch = [pltpu.VMEM((tm, D), BF16), pltpu.VMEM((tm, D), F32)]
    operands = [x, g, w1t, w3t, w2]
    if gather is not None:
        R, C = gather.shape
        in_specs.append(ANY)
        out_specs.append(ANY)
        out_shape.append(jax.ShapeDtypeStruct((N_CHIPS, 2, R // 2, C), gather.dtype))
        scratch += [pltpu.SemaphoreType.DMA((6,)), pltpu.SemaphoreType.DMA((6,))]
        operands.append(gather.reshape(2, R // 2, C))
    outs = list(pl.pallas_call(
        body, name=name, grid=(ni, nj), in_specs=in_specs, out_specs=out_specs, out_shape=out_shape,
        scratch_shapes=scratch, compiler_params=_cp("arbitrary", "arbitrary"),
    )(*operands))
    if gather is not None:
        outs[4] = outs[4].reshape(N_CHIPS, R, C)
    return outs


def _ffn_bwd(dxo, x, g, a, b, w1t, w3t, w2, name, scatter=None):
    S, D = x.shape
    F = w2.shape[0]
    tm, tf = min(MATMUL_TOKEN_TILE, S), FF_BWD_TILE
    ni, nj = S // tm, F // tf

    def body(dxo_ref, x_ref, g_ref, a_ref, b_ref, w1_ref, w3_ref, w2_ref, *rest):
        if scatter is None:
            dx_ref, dg_ref, dz_ref, da_ref, db_ref, u_ref, dzs, acc = rest
        else:
            (parts_ref, dx_ref, dg_ref, dz_ref, da_ref, db_ref, u_ref, recv_ref,
             dzs, acc, send_sems, recv_sems) = rest
        i, j = pl.program_id(0), pl.program_id(1)
        if scatter is not None:
            for when, phase in ((jnp.logical_and(i == 0, j == 0), "start"),
                                (jnp.logical_and(i == ni - 1, j == nj - 1), "finish")):
                @pl.when(when)
                def _():
                    getattr(_scatter_exchange(parts_ref, recv_ref, send_sems, recv_sems), phase)()

        @pl.when(j == 0)
        def _():
            dzb = (0.5 * dxo_ref[...]).astype(BF16)
            dzs[...] = dzb
            dz_ref[...] = dzb
            acc[...] = jnp.zeros_like(acc)

        du = _nt(dzs[...], w2_ref[...])
        av = a_ref[...].astype(F32)
        bv = b_ref[...].astype(F32)
        s = jax.nn.sigmoid(av)
        silu = av * s
        db = (du * silu).astype(BF16)
        da = (du * bv * (s * (1.0 + av * (1.0 - s)))).astype(BF16)
        da_ref[...] = da
        db_ref[...] = db
        u_ref[...] = (silu * bv).astype(BF16)
        acc[...] += _nn(da, w1_ref[...]) + _nn(db, w3_ref[...])

        @pl.when(j == nj - 1)
        def _():
            dx, dg = _norm_bwd(x_ref[...], g_ref[...], acc[...])
            dx_ref[...] = dxo_ref[...] + dx

            @pl.when(i == 0)
            def _():
                dg_ref[...] = dg

            @pl.when(i > 0)
            def _():
                dg_ref[...] += dg

    row = pl.BlockSpec((tm, D), lambda i, j: (i, 0))
    wsp = pl.BlockSpec((tf, D), lambda i, j: (j, 0))
    col = pl.BlockSpec((tm, tf), lambda i, j: (i, j))
    vec = pl.BlockSpec((1, D), lambda i, j: (0, 0))
    in_specs = [row, row, vec, col, col, wsp, wsp, wsp]
    out_specs = [row, vec, row, col, col, col]
    out_shape = [jax.ShapeDtypeStruct((S, D), F32), jax.ShapeDtypeStruct((1, D), F32),
                 jax.ShapeDtypeStruct((S, D), BF16), jax.ShapeDtypeStruct((S, F), BF16),
                 jax.ShapeDtypeStruct((S, F), BF16), jax.ShapeDtypeStruct((S, F), BF16)]
    scratch = [pltpu.VMEM((tm, D), BF16), pltpu.VMEM((tm, D), F32)]
    operands = [dxo, x, g, a, b, w1t, w3t, w2]
    if scatter is not None:
        in_specs.append(ANY)
        out_specs.append(ANY)
        out_shape.append(jax.ShapeDtypeStruct((3,) + scatter.shape[1:], scatter.dtype))
        scratch += [pltpu.SemaphoreType.DMA((3,)), pltpu.SemaphoreType.DMA((3,))]
        operands.append(scatter)
    return pl.pallas_call(
        body, name=name, grid=(ni, nj), in_specs=in_specs, out_specs=out_specs, out_shape=out_shape,
        scratch_shapes=scratch, compiler_params=_cp("arbitrary", "arbitrary"),
    )(*operands)


def _tn_matmul(a, b, name):
    S, M = a.shape
    N = b.shape[1]
    ts = min(MATMUL_TOKEN_TILE, S)
    tmm = next(t for t in WGRAD_ROW_TILES if M % t == 0)
    ns = S // ts

    def body(a_ref, b_ref, o_ref, acc):
        s = pl.program_id(1)
        part = _tn(a_ref[...], b_ref[...])

        @pl.when(s == 0)
        def _():
            acc[...] = part

        @pl.when(s > 0)
        def _():
            acc[...] += part

        @pl.when(s == ns - 1)
        def _():
            o_ref[...] = acc[...].astype(BF16)

    return pl.pallas_call(
        body, name=name, grid=(M // tmm, ns),
        in_specs=[pl.BlockSpec((ts, tmm), lambda m, s: (s, m)),
                  pl.BlockSpec((ts, N), lambda m, s: (s, 0))],
        out_specs=pl.BlockSpec((tmm, N), lambda m, s: (m, 0)),
        out_shape=jax.ShapeDtypeStruct((M, N), BF16),
        scratch_shapes=[pltpu.VMEM((tmm, N), F32)],
        compiler_params=_cp("arbitrary", "arbitrary"),
    )(a, b)


def _norm_matmul_nt(x, g, wt, out_dtype, name):
    S, D = x.shape
    N = wt.shape[0]
    tm = min(MATMUL_TOKEN_TILE, S)
    tn = next(t for t in (1024, 768, 256) if N % t == 0)

    def body(x_ref, g_ref, w_ref, o_ref, h_ref, hs):
        @pl.when(pl.program_id(1) == 0)
        def _():
            hb = _norm_fwd(x_ref[...], g_ref[...]).astype(BF16)
            hs[...] = hb
            h_ref[...] = hb

        o_ref[...] = _nt(hs[...], w_ref[...]).astype(out_dtype)

    return pl.pallas_call(
        body, name=name, grid=(S // tm, N // tn),
        in_specs=[pl.BlockSpec((tm, D), lambda i, j: (i, 0)),
                  pl.BlockSpec((1, D), lambda i, j: (0, 0)),
                  pl.BlockSpec((tn, D), lambda i, j: (j, 0))],
        out_specs=[pl.BlockSpec((tm, tn), lambda i, j: (i, j)),
                   pl.BlockSpec((tm, D), lambda i, j: (i, 0))],
        out_shape=[jax.ShapeDtypeStruct((S, N), out_dtype), jax.ShapeDtypeStruct((S, D), BF16)],
        scratch_shapes=[pltpu.VMEM((tm, D), BF16)],
        compiler_params=_cp("arbitrary", "arbitrary"),
    )(x, g, wt)


def _heads_tile(ref):
    Hh, nbk = ref.shape[0], ref.shape[1]
    return jnp.concatenate([jnp.concatenate([ref[h, b] for b in range(nbk)], axis=1) for h in range(Hh)], axis=0)


def _store_heads(ref, val):
    Hh, nbk, dh, T = ref.shape
    for h in range(Hh):
        for b in range(nbk):
            ref[h, b] = val[h * dh:(h + 1) * dh, b * T:(b + 1) * T].astype(ref.dtype)


def _norm_proj_heads(x, g, w_rows, T, name):
    S, D = x.shape
    N = w_rows.shape[0]
    tm, tn = min(MATMUL_TOKEN_TILE, S), PROJ_HEAD_ROWS

    def body(x_ref, g_ref, w_ref, o_ref, hs):
        @pl.when(pl.program_id(1) == 0)
        def _():
            hs[...] = _norm_fwd(x_ref[...], g_ref[...]).astype(BF16)

        _store_heads(o_ref, _nt(w_ref[...], hs[...]))

    return pl.pallas_call(
        body, name=name, grid=(S // tm, N // tn),
        in_specs=[pl.BlockSpec((tm, D), lambda i, j: (i, 0)),
                  pl.BlockSpec((1, D), lambda i, j: (0, 0)),
                  pl.BlockSpec((tn, D), lambda i, j: (j, 0))],
        out_specs=pl.BlockSpec((tn // HEAD_DIM, tm // T, HEAD_DIM, T), lambda i, j: (j, i, 0, 0)),
        out_shape=jax.ShapeDtypeStruct((N // HEAD_DIM, S // T, HEAD_DIM, T), BF16),
        scratch_shapes=[pltpu.VMEM((tm, D), BF16)],
        compiler_params=_cp("arbitrary", "arbitrary"),
    )(x, g, w_rows)


def _heads_matmul(pieces, b, name):
    S, N = b.shape
    ts = min(MATMUL_TOKEN_TILE, S)
    ns = S // ts
    rows = [at.shape[0] * at.shape[2] for at, _ in pieces]

    def body(*refs):
        a_refs, b_ref, o_ref = refs[:-2], refs[-2], refs[-1]
        s = pl.program_id(0)
        row0 = 0
        for a_ref, (_, scale), n in zip(a_refs, pieces, rows):
            a = _heads_tile(a_ref)
            part = _nn((a if scale == 1.0 else a * scale).astype(BF16), b_ref[...])
            out = o_ref.at[row0:row0 + n, :]
            row0 += n

            @pl.when(s == 0)
            def _():
                out[...] = part

            @pl.when(s > 0)
            def _():
                out[...] += part

    return pl.pallas_call(
        body, name=name, grid=(ns,),
        in_specs=[pl.BlockSpec((at.shape[0], ts // at.shape[3], at.shape[2], at.shape[3]), lambda s: (0, s, 0, 0))
                  for at, _ in pieces] + [pl.BlockSpec((ts, N), lambda s: (s, 0))],
        out_specs=pl.BlockSpec((sum(rows), N), lambda s: (0, 0)),
        out_shape=jax.ShapeDtypeStruct((sum(rows), N), F32),
        compiler_params=_cp("arbitrary"),
    )(*[at for at, _ in pieces], b)


def _proj_bwd(pieces, dgates, w_rows, x, g, dres):
    S, D = x.shape
    tm = min(TOKEN_TILE, S)
    n_p = len(pieces)
    gate_row = w_rows.shape[0] - dgates.shape[1]

    def body(*refs):
        p_refs = refs[:n_p]
        dgt_ref, w_ref, x_ref, g_ref, dres_ref, dx_ref, dg_ref = refs[n_p:]
        i = pl.program_id(0)
        dh = _nn(dgt_ref[...], w_ref[gate_row:, :])
        for p_ref, (arr, row0) in zip(p_refs, pieces):
            rows = arr.shape[0] * arr.shape[2]
            dh += _tn(_heads_tile(p_ref).astype(BF16), w_ref[row0:row0 + rows, :])
        dx, dg = _norm_bwd(x_ref[...], g_ref[...], dh)
        dx_ref[...] = dres_ref[...] + dx

        @pl.when(i == 0)
        def _():
            dg_ref[...] = dg

        @pl.when(i > 0)
        def _():
            dg_ref[...] += dg

    row = pl.BlockSpec((tm, D), lambda i: (i, 0))
    vec = pl.BlockSpec((1, D), lambda i: (0, 0))
    p_specs = [pl.BlockSpec((a.shape[0], tm // a.shape[3], a.shape[2], a.shape[3]), lambda i: (0, i, 0, 0))
               for a, _ in pieces]
    return pl.pallas_call(
        body, name="proj_bwd", grid=(S // tm,),
        in_specs=p_specs + [pl.BlockSpec((tm, dgates.shape[1]), lambda i: (i, 0)),
                            pl.BlockSpec(w_rows.shape, lambda i: (0, 0), pipeline_mode=pl.Buffered(1)),
                            row, vec, row],
        out_specs=[row, vec],
        out_shape=[jax.ShapeDtypeStruct((S, D), F32), jax.ShapeDtypeStruct((1, D), F32)],
        compiler_params=_cp("arbitrary"),
    )(*[a for a, _ in pieces], dgates, w_rows, x, g, dres)


def _merge_fwd(x1, gates, oa_t, ob_t, wat, wbt, w_out):
    S, D = x1.shape
    W = wat.shape[1]
    tm = min(TOKEN_TILE, S)

    def body(x_ref, ga_ref, gb_ref, oa_ref, ob_ref, wa_ref, wb_ref, wo_ref,
             x2_ref, mg_ref, ba_ref, bb_ref):
        ba = _nt(_heads_tile(oa_ref).T.astype(BF16), wa_ref[...])
        bb = _nt(_heads_tile(ob_ref).T.astype(BF16), wb_ref[...])
        merged = jax.nn.sigmoid(ga_ref[...]) * ba + jax.nn.sigmoid(gb_ref[...]) * bb
        mb = merged.astype(BF16)
        mg_ref[...] = mb
        ba_ref[...] = ba.astype(BF16)
        bb_ref[...] = bb.astype(BF16)
        x2_ref[...] = x_ref[...] + _nn(mb, wo_ref[...])

    row = pl.BlockSpec((tm, D), lambda i: (i, 0))
    full = lambda r, c: pl.BlockSpec((r, c), lambda i: (0, 0))
    heads = lambda a: pl.BlockSpec((a.shape[0], tm // a.shape[3], a.shape[2], a.shape[3]), lambda i: (0, i, 0, 0))
    return pl.pallas_call(
        body, name="merge_fwd", grid=(S // tm,),
        in_specs=[row, pl.BlockSpec((tm, D), lambda i: (i, 0)), pl.BlockSpec((tm, D), lambda i: (i, 1)),
                  heads(oa_t), heads(ob_t), full(D, W), full(D, W), full(D, D)],
        out_specs=[row, row, row, row],
        out_shape=[jax.ShapeDtypeStruct((S, D), F32)] + [jax.ShapeDtypeStruct((S, D), BF16)] * 3,
        compiler_params=_cp("arbitrary"),
    )(x1, gates, gates, oa_t, ob_t, wat, wbt, w_out)


def _merge_bwd(dx2, gates, ba, bb, wa, wb, w_out, t_a, t_b):
    S, D = dx2.shape
    W = wa.shape[0]
    tm = min(TOKEN_TILE, S)
    Hh = W // HEAD_DIM

    def body(dx_ref, ga_ref, gb_ref, ba_ref, bb_ref, wa_ref, wb_ref, wo_ref,
             dxb_ref, dba_ref, dbb_ref, dgt_ref, doa_ref, dob_ref):
        dxb = dx_ref[...].astype(BF16)
        dxb_ref[...] = dxb
        dm = _nt(dxb, wo_ref[...])
        sa = jax.nn.sigmoid(ga_ref[...])
        sb = jax.nn.sigmoid(gb_ref[...])
        dba = (dm * sa).astype(BF16)
        dbb = (dm * sb).astype(BF16)
        dba_ref[...] = dba
        dbb_ref[...] = dbb
        dgt_ref[:, :D] = (dm * ba_ref[...].astype(F32) * sa * (1.0 - sa)).astype(BF16)
        dgt_ref[:, D:] = (dm * bb_ref[...].astype(F32) * sb * (1.0 - sb)).astype(BF16)
        _store_heads(doa_ref, _nt(wa_ref[...], dba))
        _store_heads(dob_ref, _nt(wb_ref[...], dbb))

    row = pl.BlockSpec((tm, D), lambda i: (i, 0))
    full = lambda r, c: pl.BlockSpec((r, c), lambda i: (0, 0))
    heads = lambda T: pl.BlockSpec((Hh, tm // T, HEAD_DIM, T), lambda i: (0, i, 0, 0))
    return pl.pallas_call(
        body, name="merge_bwd", grid=(S // tm,),
        in_specs=[row, pl.BlockSpec((tm, D), lambda i: (i, 0)), pl.BlockSpec((tm, D), lambda i: (i, 1)),
                  row, row, full(W, D), full(W, D), full(D, D)],
        out_specs=[row, row, row, pl.BlockSpec((tm, 2 * D), lambda i: (i, 0)), heads(t_a), heads(t_b)],
        out_shape=[jax.ShapeDtypeStruct((S, D), BF16)] * 3 + [jax.ShapeDtypeStruct((S, 2 * D), BF16),
                   jax.ShapeDtypeStruct((Hh, S // t_a, HEAD_DIM, t_a), F32),
                   jax.ShapeDtypeStruct((Hh, S // t_b, HEAD_DIM, t_b), BF16)],
        compiler_params=_cp("arbitrary"),
    )(dx2, gates, gates, ba, bb, wa, wb, w_out)


def _final_loss(x3, gf, target):
    S, D = x3.shape
    tm = min(TOKEN_TILE, S)

    def body(x_ref, g_ref, t_ref, loss_ref, dx_ref, dg_ref):
        i = pl.program_id(0)
        x = x_ref[...]
        g = g_ref[...]
        e = _norm_fwd(x, g) - t_ref[...]
        part = 0.5 * jnp.sum(jnp.mean(e * e, axis=-1, keepdims=True), axis=0, keepdims=True)
        dx, dg = _norm_bwd(x, g, e * (1.0 / D))
        dx_ref[...] = dx

        @pl.when(i == 0)
        def _():
            loss_ref[...] = part
            dg_ref[...] = dg

        @pl.when(i > 0)
        def _():
            loss_ref[...] += part
            dg_ref[...] += dg

    row = pl.BlockSpec((tm, D), lambda i: (i, 0))
    vec = pl.BlockSpec((1, D), lambda i: (0, 0))
    return pl.pallas_call(
        body, name="final_loss", grid=(S // tm,),
        in_specs=[row, vec, row],
        out_specs=[pl.BlockSpec((1, 1), lambda i: (0, 0)), row, vec],
        out_shape=[jax.ShapeDtypeStruct((1, 1), F32), jax.ShapeDtypeStruct((S, D), F32),
                   jax.ShapeDtypeStruct((1, D), F32)],
        compiler_params=_cp("arbitrary"),
    )(x3, gf, target)


SB_FWD_HEAD_GROUP = 8
SB_HEAD_GROUP = 4
LANES = 128


def _tri(T, kind):
    r = lax.broadcasted_iota(jnp.int32, (T, T), 0)
    c = lax.broadcasted_iota(jnp.int32, (T, T), 1)
    return {"after": r > c, "before": r < c}[kind].astype(BF16)


def _lane(v, j):
    return jnp.broadcast_to(v[:, j:j + 1], (v.shape[0], LANES))


def _t_bf16(x):
    return x.astype(F32).T.astype(BF16)


def _wide(v, T):
    return jnp.tile(v, (1, T // LANES))


SB_SLOTS = 3
SB_FWD_SLOTS = 2
COPY_PARTS = 4


class _split_copy:
    def __init__(self, src, dst, sems):
        n = src.shape[0] // COPY_PARTS
        self.parts = [pltpu.make_async_copy(src.at[pl.ds(r * n, n)], dst.at[pl.ds(r * n, n)], sems.at[r])
                      for r in range(COPY_PARTS)]

    def start(self):
        for cp in self.parts:
            cp.start()

    def wait(self):
        for cp in self.parts:
            cp.wait()


def _sb_pair(i, kb):
    return (i * (i + 1)) // 2 + kb


def _sb_fwd(qkv):
    H3, nb, dh, T = qkv.shape
    H = H3 // 3
    HG = SB_FWD_HEAD_GROUP
    assert HG == H, "one head group: a saved tile holds all the heads"
    n_pairs = (nb * (nb + 1)) // 2

    def body(q_ref, k_ref, v_ref, o_ref, saved_ref, stage, sems):
        row = lax.broadcasted_iota(jnp.int32, (T, T), 0)
        col = lax.broadcasted_iota(jnp.int32, (T, T), 1)
        tri = col < row
        after = _tri(T, "after")

        def save(slot, pair):
            return _split_copy(stage.at[slot], saved_ref.at[pair], sems.at[slot])

        def blocks(qs, i, kb, step, carry, diag):
            hs = range(HG)
            slot = step % SB_FWD_SLOTS

            @pl.when(step >= SB_FWD_SLOTS)
            def _():
                save(slot, 0).wait()

            z = [_nn(qs[hh], k_ref[hh, kb]) for hh in hs]
            res, ls, first = [None] * HG, [None] * HG, [None] * HG
            for hh in hs:
                sp = _softplus(z[hh])
                if diag:
                    sp = jnp.where(tri, sp, 0.0)
                ls[hh] = z[hh] - sp
                spb = sp.astype(BF16)
                first[hh] = _lane(spb.astype(F32), 0)
                res[hh] = _nn(spb, after)
            out = []
            for hh in hs:
                c, oacc = carry[2 * hh], carry[2 * hh + 1]
                a = jnp.exp(ls[hh] - (res[hh] + _wide(c, T)))
                if diag:
                    a = jnp.where(tri, a, 0.0)
                ab = a.astype(BF16)
                stage[slot, hh, 0] = ab
                stage[slot, hh, 1] = jnp.exp(ls[hh]).astype(BF16)
                out.extend([c + (first[hh] + _lane(res[hh], 0)), oacc + _nt(v_ref[hh, kb], ab)])
            save(slot, _sb_pair(i, kb)).start()
            return tuple(out)

        def qblock(i, step):
            qs = [_t_bf16(q_ref[hh, i]) for hh in range(HG)]
            carry = blocks(qs, i, i, step, (jnp.zeros((T, LANES), F32), jnp.zeros((dh, T), F32)) * HG, True)

            def kstep(t, carry):
                return blocks(qs, i, i - 1 - t, step + 1 + t, carry, False)

            carry = lax.fori_loop(0, i, kstep, carry)
            for hh in range(HG):
                o_ref[hh, i] = carry[2 * hh + 1]
            return step + 1 + i

        lax.fori_loop(0, nb, qblock, 0)
        for slot in range(min(SB_FWD_SLOTS, n_pairs)):
            save(slot, 0).wait()

    ht = lambda part: pl.BlockSpec((HG, nb, dh, T), lambda h: (part, 0, 0, 0), pipeline_mode=pl.Buffered(1))
    return pl.pallas_call(
        body, name="sb_fwd", grid=(1,),
        in_specs=[ht(0), ht(1), ht(2)],
        out_specs=[ht(0), ANY],
        out_shape=[jax.ShapeDtypeStruct((H, nb, dh, T), F32),
                   jax.ShapeDtypeStruct((n_pairs, H, 2, T, T), BF16)],
        scratch_shapes=[pltpu.VMEM((SB_FWD_SLOTS, HG, 2, T, T), BF16),
                        pltpu.SemaphoreType.DMA((SB_FWD_SLOTS, COPY_PARTS))],
        compiler_params=_cp("arbitrary"),
    )(qkv, qkv, qkv)


def _sb_bwd(qkv, dot, saved):
    H3, nb, dh, T = qkv.shape
    H = H3 // 3
    HG = SB_HEAD_GROUP
    n_pairs = (nb * (nb + 1)) // 2

    def body(qt_ref, k_ref, v_ref, dot_ref, saved_ref, dq_ref, dk_ref, dv_ref, stage, sems):
        head0 = pl.program_id(0) * HG
        row = lax.broadcasted_iota(jnp.int32, (T, T), 0)
        col = lax.broadcasted_iota(jnp.int32, (T, T), 1)
        tri = col < row
        before = _tri(T, "before")
        dk_ref[...] = jnp.zeros_like(dk_ref)
        dv_ref[...] = jnp.zeros_like(dv_ref)

        def fetch(slot, pair):
            return _split_copy(saved_ref.at[pair, pl.ds(head0, HG)], stage.at[slot], sems.at[slot])

        for ahead in range(min(SB_SLOTS - 1, n_pairs)):
            fetch(ahead, ahead).start()

        def blocks(qTs, dos, doTs, i, kb, carry, diag):
            hs = range(HG)
            pair = _sb_pair(i, kb)
            slot = pair % SB_SLOTS
            fetch(slot, pair).wait()
            nxt = pair + (SB_SLOTS - 1)

            @pl.when(nxt < n_pairs)
            def _():
                fetch(nxt % SB_SLOTS, nxt).start()

            kT = [k_ref[hh, kb] for hh in hs]
            da = [_nn(dos[hh], v_ref[hh, kb]) for hh in hs]
            g, gb, resg = [None] * HG, [None] * HG, [None] * HG
            for hh in hs:
                g[hh] = stage[slot, hh, 0].astype(F32) * da[hh]
                gb[hh] = g[hh].astype(BF16)
                resg[hh] = _nn(gb[hh], before)
            out = []
            for hh in hs:
                pre_g, dq = carry[2 * hh], carry[2 * hh + 1]
                dz = g[hh] - (g[hh] + (resg[hh] + _wide(pre_g, T))) * stage[slot, hh, 1].astype(F32)
                if diag:
                    dz = jnp.where(tri, dz, 0.0)
                dzb = dz.astype(BF16)
                dk_ref[hh, kb] += _nn(qTs[hh], dzb)
                dv_ref[hh, kb] += _nn(doTs[hh], stage[slot, hh, 0])
                out.extend([pre_g + (_lane(resg[hh], T - 1) + _lane(gb[hh].astype(F32), T - 1)),
                            dq + _nt(kT[hh], dzb)])
            return tuple(out)

        def qblock(i, _):
            qTs = [qt_ref[hh, i] for hh in range(HG)]
            doTs = [dot_ref[hh, i] for hh in range(HG)]
            dos = [_t_bf16(v) for v in doTs]
            carry = (jnp.zeros((T, LANES), F32), jnp.zeros((dh, T), F32)) * HG

            def kstep(kb, carry):
                return blocks(qTs, dos, doTs, i, kb, carry, False)

            carry = lax.fori_loop(0, i, kstep, carry)
            carry = blocks(qTs, dos, doTs, i, i, carry, True)
            for hh in range(HG):
                dq_ref[hh, i] = carry[2 * hh + 1]
            return 0

        lax.fori_loop(0, nb, qblock, 0)

    G = H // HG
    ht = lambda part: pl.BlockSpec((HG, nb, dh, T), lambda h: (h + part * G, 0, 0, 0),
                                   pipeline_mode=pl.Buffered(1))
    return pl.pallas_call(
        body, name="sb_bwd", grid=(G,),
        in_specs=[ht(0), ht(1), ht(2), ht(0), ANY],
        out_specs=[ht(0), ht(0), ht(0)],
        out_shape=[jax.ShapeDtypeStruct((H, nb, dh, T), F32)] * 3,
        scratch_shapes=[pltpu.VMEM((SB_SLOTS, HG, 2, T, T), BF16), pltpu.SemaphoreType.DMA((SB_SLOTS, COPY_PARTS))],
        compiler_params=_cp("arbitrary"),
    )(qkv, qkv, qkv, dot, saved)


def _swa_probs(zp, zc, bias, sink, first):
    T = zp.shape[0]
    key = lax.broadcasted_iota(jnp.int32, (T, T), 0)
    qry = lax.broadcasted_iota(jnp.int32, (T, T), 1)
    lp = jnp.where(jnp.logical_and(key > qry, jnp.logical_not(first)), zp + bias[:T, :], NEG_BIG)
    lc = jnp.where(key <= qry, zc + bias[T:, :], NEG_BIG)
    m = jnp.maximum(jnp.maximum(jnp.max(lp, axis=0, keepdims=True), jnp.max(lc, axis=0, keepdims=True)), sink)
    pp = jnp.exp(lp - m)
    pc = jnp.exp(lc - m)
    ps = jnp.exp(sink - m)
    inv = 1.0 / (jnp.sum(pp, axis=0, keepdims=True) + jnp.sum(pc, axis=0, keepdims=True) + ps)
    return pp * inv, pc * inv, ps * inv


def _swa_specs(nb, dh, T, Hq, Hkv, clamp):
    blk = (lambda n: jnp.minimum(n, nb - 1)) if clamp else (lambda n: n)
    q = pl.BlockSpec((Hq, None, dh, T), lambda n: (0, blk(n), 0, 0))
    kv = lambda first, back: pl.BlockSpec(
        (Hkv, None, dh, T), lambda n: (first // Hkv, jnp.maximum(blk(n) - back, 0) if back else blk(n), 0, 0))
    return q, [kv(Hq, 1), kv(Hq, 0), kv(Hq + Hkv, 1), kv(Hq + Hkv, 0)]


def _swa_fwd(qkv, bias, sinks):
    Hq, Hkv, grp = SWA_Q_HEADS, SWA_KV_HEADS, SWA_GROUP
    _, nb, dh, T = qkv.shape

    def body(sink_ref, q_ref, kp_ref, kc_ref, vp_ref, vc_ref, bias_ref, o_ref):
        n = pl.program_id(0)
        kpn = [_t_bf16(kp_ref[hk]) for hk in range(Hkv)]
        kcn = [_t_bf16(kc_ref[hk]) for hk in range(Hkv)]
        zs = [(_nn(kpn[h // grp], q_ref[h]), _nn(kcn[h // grp], q_ref[h])) for h in range(Hq)]
        for h in range(Hq):
            pp, pc, _ = _swa_probs(*zs[h], bias_ref[h], sink_ref[h], n == 0)
            o_ref[h] = _nn(vp_ref[h // grp], pp.astype(BF16)) + _nn(vc_ref[h // grp], pc.astype(BF16))

    q_spec, kv_specs = _swa_specs(nb, dh, T, Hq, Hkv, False)
    return pl.pallas_call(
        body, name="swa_fwd", grid=(nb,),
        in_specs=[pl.BlockSpec(memory_space=pltpu.SMEM), q_spec] + kv_specs
                 + [pl.BlockSpec((Hq, 2 * T, T), lambda n: (0, 0, 0))],
        out_specs=pl.BlockSpec((Hq, None, dh, T), lambda n: (0, n, 0, 0)),
        out_shape=jax.ShapeDtypeStruct((Hq, nb, dh, T), F32),
        compiler_params=_cp("arbitrary"),
    )(sinks, qkv, qkv, qkv, qkv, qkv, bias)


def _swa_bwd(qkv, bias, sinks, dot, ot):
    Hq, Hkv, grp = SWA_Q_HEADS, SWA_KV_HEADS, SWA_GROUP
    _, nb, dh, T = qkv.shape

    def body(sink_ref, qt_ref, kp_ref, kc_ref, vp_ref, vc_ref, bias_ref, dot_ref, ot_ref,
             dq_ref, dk_ref, dv_ref, dbias_ref, dsink_ref, ck, cv):
        n = pl.program_id(0)

        @pl.when(n == 0)
        def _():
            dbias_ref[...] = jnp.zeros_like(dbias_ref)
            dsink_ref[...] = jnp.zeros_like(dsink_ref)
            ck[...] = jnp.zeros_like(ck)
            cv[...] = jnp.zeros_like(cv)

        @pl.when(n < nb)
        def _():
            kp, kc = [kp_ref[hk] for hk in range(Hkv)], [kc_ref[hk] for hk in range(Hkv)]
            kpn, kcn = [_t_bf16(v) for v in kp], [_t_bf16(v) for v in kc]
            vpn = [_t_bf16(vp_ref[hk]) for hk in range(Hkv)]
            vcn = [_t_bf16(vc_ref[hk]) for hk in range(Hkv)]
            qTs = [qt_ref[h] for h in range(Hq)]
            doTs = [dot_ref[h].astype(BF16) for h in range(Hq)]
            zs = [(_nn(kpn[h // grp], qTs[h]), _nn(kcn[h // grp], qTs[h])) for h in range(Hq)]
            dps = [(_nn(vpn[h // grp], doTs[h]), _nn(vcn[h // grp], doTs[h])) for h in range(Hq)]
            dls, pbs = [], []
            for h in range(Hq):
                pp, pc, ps = _swa_probs(*zs[h], bias_ref[h], sink_ref[h], n == 0)
                delta = jnp.sum(dot_ref[h] * ot_ref[h], axis=0, keepdims=True)
                dlp = pp * (dps[h][0] - delta)
                dlc = pc * (dps[h][1] - delta)
                dbias_ref[h, :T, :] += dlp
                dbias_ref[h, T:, :] += dlc
                dsink_ref[h] += -ps * delta
                dls.append((dlp.astype(BF16), dlc.astype(BF16)))
                pbs.append((pp.astype(BF16), pc.astype(BF16)))
            zero = jnp.zeros((dh, T), F32)
            kprev, kcur, vprev, vcur = [zero] * Hkv, [zero] * Hkv, [zero] * Hkv, [zero] * Hkv
            for h in range(Hq):
                hk = h // grp
                dlpb, dlcb = dls[h]
                dq_ref[h] = _nn(kp[hk], dlpb) + _nn(kc[hk], dlcb)
                kprev[hk] = kprev[hk] + _nt(qTs[h], dlpb)
                kcur[hk] = kcur[hk] + _nt(qTs[h], dlcb)
                vprev[hk] = vprev[hk] + _nt(doTs[h], pbs[h][0])
                vcur[hk] = vcur[hk] + _nt(doTs[h], pbs[h][1])
            for hk in range(Hkv):
                dk_ref[hk] = ck[hk] + kprev[hk]
                dv_ref[hk] = cv[hk] + vprev[hk]
                ck[hk] = kcur[hk]
                cv[hk] = vcur[hk]

        @pl.when(n == nb)
        def _():
            dk_ref[...] = ck[...]
            dv_ref[...] = cv[...]

    qt_spec, kv_specs = _swa_specs(nb, dh, T, Hq, Hkv, True)
    prev = pl.BlockSpec((Hkv, None, dh, T), lambda n: (0, jnp.maximum(n - 1, 0), 0, 0))
    whole = lambda a, b: pl.BlockSpec((Hq, a, b), lambda n: (0, 0, 0))
    return pl.pallas_call(
        body, name="swa_bwd", grid=(nb + 1,),
        in_specs=[pl.BlockSpec(memory_space=pltpu.SMEM), qt_spec] + kv_specs
                 + [whole(2 * T, T), qt_spec, qt_spec],
        out_specs=[qt_spec, prev, prev, whole(2 * T, T), whole(1, T)],
        out_shape=[jax.ShapeDtypeStruct((Hq, nb, dh, T), F32), jax.ShapeDtypeStruct((Hkv, nb, dh, T), F32),
                   jax.ShapeDtypeStruct((Hkv, nb, dh, T), F32), jax.ShapeDtypeStruct((Hq, 2 * T, T), F32),
                   jax.ShapeDtypeStruct((Hq, 1, T), F32)],
        scratch_shapes=[pltpu.VMEM((Hkv, dh, T), F32), pltpu.VMEM((Hkv, dh, T), F32)],
        compiler_params=_cp("arbitrary"),
    )(sinks, qkv, qkv, qkv, qkv, qkv, bias, dot, ot)


def _split3(x):
    h1 = x.astype(BF16)
    r1 = x - h1.astype(F32)
    h2 = r1.astype(BF16)
    h3 = (r1 - h2.astype(F32)).astype(BF16)
    return h1, h2, h3


def _bias_expand(rel_t, onehot):
    Hq, NB = rel_t.shape
    L = onehot.shape[1]

    def body(r_ref, oh_ref, o_ref):
        h1, h2, h3 = _split3(r_ref[...])
        oh = oh_ref[...]
        o_ref[...] = _nn(h1, oh) + _nn(h2, oh) + _nn(h3, oh)

    return pl.pallas_call(
        body, name="bias_expand", grid=(1,),
        in_specs=[pl.BlockSpec((Hq, NB), lambda i: (0, 0)), pl.BlockSpec((NB, L), lambda i: (0, 0))],
        out_specs=pl.BlockSpec((Hq, L), lambda i: (0, 0)),
        out_shape=jax.ShapeDtypeStruct((Hq, L), F32),
        compiler_params=_cp("arbitrary"),
    )(rel_t, onehot)


def _bias_reduce(dbias, onehot):
    Hq, L = dbias.shape
    NB = onehot.shape[0]

    def body(d_ref, oh_ref, o_ref):
        h1, h2, h3 = _split3(d_ref[...])
        oh = oh_ref[...]
        o_ref[...] = _nt(h1, oh) + _nt(h2, oh) + _nt(h3, oh)

    return pl.pallas_call(
        body, name="bias_reduce", grid=(1,),
        in_specs=[pl.BlockSpec((Hq, L), lambda i: (0, 0)), pl.BlockSpec((NB, L), lambda i: (0, 0))],
        out_specs=pl.BlockSpec((Hq, NB), lambda i: (0, 0)),
        out_shape=jax.ShapeDtypeStruct((Hq, NB), F32),
        compiler_params=_cp("arbitrary"),
    )(dbias, onehot)


def _adamw(w, g, m, v, name):
    R, C = w.shape
    tr = 256 if R % 256 == 0 else R
    bc1 = 1.0 - ADAM_B1 ** ADAM_STEP
    bc2 = 1.0 - ADAM_B2 ** ADAM_STEP

    def body(w_ref, g_ref, m_ref, v_ref, d_ref, nm_ref, nv_ref):
        g = g_ref[...]
        m2 = ADAM_B1 * m_ref[...] + (1.0 - ADAM_B1) * g
        v2 = ADAM_B2 * v_ref[...] + (1.0 - ADAM_B2) * (g * g)
        nm_ref[...] = m2
        nv_ref[...] = v2
        d_ref[...] = -ADAM_LR * ((m2 / bc1) / (jnp.sqrt(v2 / bc2) + ADAM_EPS) + ADAM_WD * w_ref[...])

    spec = pl.BlockSpec((tr, C), lambda i: (i, 0))
    return pl.pallas_call(
        body, name=name, grid=(R // tr,),
        in_specs=[spec] * 4, out_specs=[spec] * 3,
        out_shape=[jax.ShapeDtypeStruct((R, C), F32)] * 3,
        compiler_params=_cp("arbitrary"),
    )(w, g, m, v)


def _row_tile(R):
    return max(t for t in range(16, 513, 16) if R % t == 0)


def _add_halves(mine, recv, name):
    K, R, C = mine.shape
    tr = _row_tile(R)

    def body(a_ref, b_ref, o_ref, ob_ref):
        s = a_ref[...].astype(F32) + b_ref[...].astype(F32)
        o_ref[...] = s
        ob_ref[...] = s.astype(BF16)

    spec = pl.BlockSpec((None, tr, C), lambda k, i: (k, i, 0))
    return pl.pallas_call(
        body, name=name, grid=(K, R // tr),
        in_specs=[spec, spec], out_specs=[spec, spec],
        out_shape=[jax.ShapeDtypeStruct((K, R, C), F32), jax.ShapeDtypeStruct((K, R, C), BF16)],
        compiler_params=_cp("arbitrary", "arbitrary"),
    )(mine, recv)


def _add_received(own, recv, name):
    R, C = own.shape
    tr = _row_tile(R)

    def body(a_ref, r_ref, o_ref):
        o_ref[...] = ((a_ref[...] + r_ref[0].astype(F32)) + r_ref[1].astype(F32)) + r_ref[2].astype(F32)

    return pl.pallas_call(
        body, name=name, grid=(R // tr,),
        in_specs=[pl.BlockSpec((tr, C), lambda i: (i, 0)), pl.BlockSpec((3, tr, C), lambda i: (0, i, 0))],
        out_specs=pl.BlockSpec((tr, C), lambda i: (i, 0)),
        out_shape=jax.ShapeDtypeStruct((R, C), F32),
        compiler_params=_cp("arbitrary"),
    )(own, recv)


def _position():
    x, y, c = lax.axis_index("x"), lax.axis_index("y"), lax.axis_index("c")
    others = [(1 - x, y), (x, 1 - y), (1 - x, 1 - y)]
    return x, y, c, others


def _remote(src, dst, send_sems, recv_sems, k, dev):
    return pltpu.make_async_remote_copy(src_ref=src, dst_ref=dst, send_sem=send_sems.at[k],
                                        recv_sem=recv_sems.at[k], device_id=dev, device_id_type=MESH_ID)


class _gather_exchange:
    def __init__(self, src, out, send_sems, recv_sems):
        x, y, c, others = _position()
        mine, sibling = 2 * x + y, (x, y, 1 - c)
        self.sends, self.arrivals, self.passes, self.from_sibling = [], [], [], []
        for j, (ox, oy) in enumerate(others):
            slot = out.at[2 * ox + oy, c]
            theirs = out.at[2 * ox + oy, 1 - c]
            self.sends.append(_remote(src.at[c], out.at[mine, c], send_sems, recv_sems, j, (ox, oy, c)))
            self.arrivals.append(_remote(slot, slot, send_sems, recv_sems, j, (ox, oy, c)))
            self.passes.append(_remote(slot, slot, send_sems, recv_sems, 3 + j, sibling))
            self.from_sibling.append(_remote(theirs, theirs, send_sems, recv_sems, 3 + j, sibling))

    def start(self):
        for cp in self.sends:
            cp.start()

    def forward(self):
        for arrived, onward in zip(self.arrivals, self.passes):
            arrived.wait_recv()
            onward.start()

    def finish(self):
        for cp in self.from_sibling:
            cp.wait_recv()
        for cp in self.sends + self.passes:
            cp.wait_send()


def _gather_weights(shard):
    R, C = shard.shape
    half = R // 2

    def body(src, out, send_sems, recv_sems):
        ex = _gather_exchange(src, out, send_sems, recv_sems)
        ex.start()
        ex.forward()
        ex.finish()

    return pl.pallas_call(
        body, name="gather_weights",
        in_specs=[ANY], out_specs=ANY,
        out_shape=jax.ShapeDtypeStruct((N_CHIPS, 2, half, C), shard.dtype),
        scratch_shapes=[pltpu.SemaphoreType.DMA((6,)), pltpu.SemaphoreType.DMA((6,))],
    )(shard.reshape(2, half, C)).reshape(N_CHIPS, R, C)


def _swap_halves(grads, name):
    K, R, C = grads.shape
    half = R // 2

    def body(src, out, send_sems, recv_sems):
        x, y, c, _ = _position()
        theirs = src.at[:, pl.ds(pl.multiple_of((1 - c) * half, 16), half), :]
        cp = _remote(theirs, out, send_sems, recv_sems, 0, (x, y, 1 - c))
        cp.start()
        cp.wait()

    return pl.pallas_call(
        body, name=name,
        in_specs=[ANY], out_specs=ANY,
        out_shape=jax.ShapeDtypeStruct((K, half, C), grads.dtype),
        scratch_shapes=[pltpu.SemaphoreType.DMA((1,)), pltpu.SemaphoreType.DMA((1,))],
    )(grads)


class _scatter_exchange:
    def __init__(self, src, out, send_sems, recv_sems):
        x, y, c, others = _position()
        self.copies = [_remote(src.at[2 * ox + oy], out.at[j], send_sems, recv_sems, j, (ox, oy, c))
                       for j, (ox, oy) in enumerate(others)]

    def start(self):
        for cp in self.copies:
            cp.start()

    def finish(self):
        for cp in self.copies:
            cp.wait()


def _scatter_to_owners(parts, name):
    K, H, C = parts.shape

    def body(src, out, send_sems, recv_sems):
        ex = _scatter_exchange(src, out, send_sems, recv_sems)
        ex.start()
        ex.finish()

    return pl.pallas_call(
        body, name=name,
        in_specs=[ANY], out_specs=ANY,
        out_shape=jax.ShapeDtypeStruct((3, H, C), parts.dtype),
        scratch_shapes=[pltpu.SemaphoreType.DMA((3,)), pltpu.SemaphoreType.DMA((3,))],
    )(parts)


def _swap_reduced(half_rows, name):
    H, C = half_rows.shape

    def body(src, out, send_sems, recv_sems):
        x, y, c, _ = _position()
        cp = _remote(src, out, send_sems, recv_sems, 0, (x, y, 1 - c))
        cp.start()
        cp.wait()

    return pl.pallas_call(
        body, name=name,
        in_specs=[ANY], out_specs=ANY,
        out_shape=jax.ShapeDtypeStruct((H, C), half_rows.dtype),
        scratch_shapes=[pltpu.SemaphoreType.DMA((1,)), pltpu.SemaphoreType.DMA((1,))],
    )(half_rows)


def _allreduce_small(block):
    R, C = block.shape
    n_dev = 8

    def body(src, out, slots, send_sems, recv_sems):
        x, y, c, _ = _position()
        me = 4 * x + 2 * y + c
        slots[me] = src[...]
        sends = []
        for r in range(1, n_dev):
            peer = (x ^ (r >> 2), y ^ ((r >> 1) & 1), c ^ (r & 1))
            cp = _remote(src, slots.at[me], send_sems, recv_sems, r - 1, peer)
            cp.start()
            sends.append(cp)
        for r in range(1, n_dev):
            theirs = slots.at[me ^ r]
            _remote(theirs, theirs, send_sems, recv_sems, r - 1, (x, y, c)).wait_recv()
        for cp in sends:
            cp.wait_send()
        acc = slots[0]
        for d in range(1, n_dev):
            acc = acc + slots[d]
        out[...] = acc

    return pl.pallas_call(
        body, name="allreduce_small",
        in_specs=[pl.BlockSpec(memory_space=pltpu.VMEM)], out_specs=pl.BlockSpec(memory_space=pltpu.VMEM),
        out_shape=jax.ShapeDtypeStruct((R, C), F32),
        scratch_shapes=[pltpu.VMEM((n_dev, R, C), F32), pltpu.SemaphoreType.DMA((7,)), pltpu.SemaphoreType.DMA((7,))],
    )(block)


def _rel_bucket(dist):
    max_exact = REL_BUCKETS // 2
    d = jnp.maximum(dist, 1).astype(F32)
    large = max_exact + (jnp.log(d / max_exact) / math.log(REL_MAX_DIST / max_exact)
                         * (REL_BUCKETS - max_exact)).astype(jnp.int32)
    large = jnp.minimum(large, REL_BUCKETS - 1)
    return jnp.where(dist < max_exact, dist, large)


def _bucket_onehot():
    T = SWA_BLOCK
    dist = (jnp.arange(T)[None, :] + T) - jnp.arange(2 * T)[:, None]
    bucket = _rel_bucket(jnp.maximum(dist, 0)).reshape(1, T * 2 * T)
    return (bucket == jnp.arange(REL_BUCKETS)[:, None]).astype(BF16)


_BUF = (("ffn1_w1", "t"), ("ffn1_w3", "t"), ("ffn1_w2", "n"), ("ffn2_w1", "t"), ("ffn2_w3", "t"),
        ("ffn2_w2", "n"), ("w_in", "t"), ("w_out", "n"), ("w_branch_swa", "tw"), ("w_branch_sb", "tw"))


def _to_rows(name_kind, w, D):
    kind = name_kind[1]
    if kind == "n":
        return w
    if kind == "t":
        return w.T
    return w.T.reshape(-1, D)


def _from_rows(name_kind, rows, width):
    kind = name_kind[1]
    if kind == "n":
        return rows
    if kind == "t":
        return rows.T
    return rows.reshape(-1, width).T


def kernel(x, norm_ffn1, ffn1_w1, ffn1_w3, ffn1_w2, norm_mix, w_in, swa_sinks, rel_bias, w_branch_swa, w_branch_sb, w_out, norm_ffn2, ffn2_w1, ffn2_w3, ffn2_w2, norm_final, loss_target, m_norm_ffn1, m_ffn1_w1, m_ffn1_w3, m_ffn1_w2, m_norm_mix, m_w_in, m_swa_sinks, m_rel_bias, m_w_branch_swa, m_w_branch_sb, m_w_out, m_norm_ffn2, m_ffn2_w1, m_ffn2_w3, m_ffn2_w2, m_norm_final, v_norm_ffn1, v_ffn1_w1, v_ffn1_w3, v_ffn1_w2, v_norm_mix, v_w_in, v_swa_sinks, v_rel_bias, v_w_branch_swa, v_w_branch_sb, v_w_out, v_norm_ffn2, v_ffn2_w1, v_ffn2_w3, v_ffn2_w2, v_norm_final):
    names = ["norm_ffn1", "ffn1_w1", "ffn1_w3", "ffn1_w2", "norm_mix", "w_in", "swa_sinks", "rel_bias",
             "w_branch_swa", "w_branch_sb", "w_out", "norm_ffn2", "ffn2_w1", "ffn2_w3", "ffn2_w2", "norm_final"]
    W = dict(zip(names, [norm_ffn1, ffn1_w1, ffn1_w3, ffn1_w2, norm_mix, w_in, swa_sinks, rel_bias,
                         w_branch_swa, w_branch_sb, w_out, norm_ffn2, ffn2_w1, ffn2_w3, ffn2_w2, norm_final]))
    M = dict(zip(names, [m_norm_ffn1, m_ffn1_w1, m_ffn1_w3, m_ffn1_w2, m_norm_mix, m_w_in, m_swa_sinks, m_rel_bias,
                         m_w_branch_swa, m_w_branch_sb, m_w_out, m_norm_ffn2, m_ffn2_w1, m_ffn2_w3, m_ffn2_w2,
                         m_norm_final]))
    V = dict(zip(names, [v_norm_ffn1, v_ffn1_w1, v_ffn1_w3, v_ffn1_w2, v_norm_mix, v_w_in, v_swa_sinks, v_rel_bias,
                         v_w_branch_swa, v_w_branch_sb, v_w_out, v_norm_ffn2, v_ffn2_w1, v_ffn2_w3, v_ffn2_w2,
                         v_norm_final]))
    xs = x[0]
    target = loss_target[0]
    S, D = xs.shape
    QW = SWA_Q_HEADS * HEAD_DIM
    KW = SWA_KV_HEADS * HEAD_DIM
    BW = SB_HEADS * HEAD_DIM
    QKV = QW + 2 * KW + 3 * BW

    pieces = [_to_rows(nk, W[nk[0]][0], D) for nk in _BUF]
    sizes = [p.shape[0] for p in pieces]
    offs = [0]
    for s in sizes:
        offs.append(offs[-1] + s)
    n_first = 3
    first_rows = offs[n_first]
    shard_a = jnp.concatenate(pieces[:n_first], axis=0).astype(BF16)
    shard_b = jnp.concatenate(pieces[n_first:], axis=0).astype(BF16)
    chip = 2 * lax.axis_index("x") + lax.axis_index("y")
    gathered_a = lax.dynamic_update_slice(_gather_weights(shard_a), shard_a[None], (chip, 0, 0))
    f1w1, f1w3, f1w2 = [gathered_a[:, offs[i]:offs[i + 1], :].reshape(N_CHIPS * sizes[i], D) for i in range(n_first)]

    g1, gmix, g3 = W["norm_ffn1"], W["norm_mix"], W["norm_ffn2"]
    gf = W["norm_final"].reshape(1, D)

    x1, h1, a1, b1, gathered_b = _ffn_fwd(xs, g1, f1w1, f1w3, f1w2, "ffn1_fwd", gather=shard_b)
    gathered_b = lax.dynamic_update_slice(gathered_b, shard_b[None], (chip, 0, 0))

    def full(i):
        return gathered_b[:, offs[i] - first_rows:offs[i + 1] - first_rows, :].reshape(N_CHIPS * sizes[i], D)

    f2w1, f2w3, f2w2, w_in_t, w_out_f = [full(i) for i in range(n_first, 8)]
    wa_t = full(8).reshape(D, QW)
    wb_t = full(9).reshape(D, BW)
    o0 = QW + 2 * KW
    rows = jnp.arange(w_in_t.shape[0])
    is_q = (rows < QW) | ((rows >= o0) & (rows < o0 + BW))
    w_in_s = w_in_t * jnp.where(is_q, QK_SCALE, 1.0).astype(BF16)[:, None]
    qkv_a = _norm_proj_heads(x1, gmix, w_in_s[:o0], SWA_BLOCK, "proj_swa")
    qkv_b = _norm_proj_heads(x1, gmix, w_in_s[o0:QKV], SB_BLOCK, "proj_sb")
    gates, h2 = _norm_matmul_nt(x1, gmix, w_in_t[QKV:], F32, "proj_gates")

    onehot = _bucket_onehot()
    bias = _bias_expand(W["rel_bias"].T, onehot).reshape(SWA_Q_HEADS, 2 * SWA_BLOCK, SWA_BLOCK)
    sinks = W["swa_sinks"].reshape(SWA_Q_HEADS)
    oa_t = _swa_fwd(qkv_a, bias, sinks)
    ob_t, saved_sb = _sb_fwd(qkv_b)

    x2, merged, ba, bb = _merge_fwd(x1, gates, oa_t, ob_t, wa_t, wb_t, w_out_f)
    x3, h3, a2, b2 = _ffn_fwd(x2, g3, f2w1, f2w3, f2w2, "ffn2_fwd")
    loss_part, dx3, dgf = _final_loss(x3, gf, target)

    dx2, dg3, dz2, da2, db2, u2 = _ffn_bwd(dx3, x2, g3, a2, b2, f2w1, f2w3, f2w2, "ffn2_bwd")
    grads = {}
    grads["ffn2_w1"] = _tn_matmul(da2, h3, "ffn2_dw1")
    grads["ffn2_w3"] = _tn_matmul(db2, h3, "ffn2_dw3")
    grads["ffn2_w2"] = _tn_matmul(u2, dz2, "ffn2_dw2")

    dx2b, dba, dbb, dgates, doa_t, dob_t = _merge_bwd(dx2, gates, ba, bb, wa_t.T, wb_t.T, w_out_f,
                                                      SWA_BLOCK, SB_BLOCK)
    grads["w_out"] = _tn_matmul(merged, dx2b, "dw_out")
    grads["w_branch_swa"] = _heads_matmul([(oa_t, 1.0)], dba, "dw_branch_swa").T
    grads["w_branch_sb"] = _heads_matmul([(ob_t, 1.0)], dbb, "dw_branch_sb").T

    dqb_t, dkb_t, dvb_t = _sb_bwd(qkv_b, dob_t, saved_sb)
    dqa_t, dka_t, dva_t, dbias, dsink_rows = _swa_bwd(qkv_a, bias, sinks, doa_t, oa_t)
    d_rel = _bias_reduce(dbias.reshape(SWA_Q_HEADS, -1), onehot).T
    d_sinks = jnp.sum(dsink_rows, axis=(1, 2))

    dheads = [(dqa_t, QK_SCALE), (dka_t, 1.0), (dva_t, 1.0), (dqb_t, QK_SCALE), (dkb_t, 1.0), (dvb_t, 1.0)]
    grads["w_in"] = jnp.concatenate([_heads_matmul(dheads, h2, "dw_in_heads").astype(BF16),
                                     _tn_matmul(dgates, h2, "dw_in_gates")], axis=0)
    row0, pieces_in = 0, []
    for a, _ in dheads:
        pieces_in.append((a, row0))
        row0 += a.shape[0] * HEAD_DIM
    dx1, dgmix = _proj_bwd(pieces_in, dgates, w_in_s, x1, gmix, dx2)

    c = lax.axis_index("c")

    def reduce_start(lo, hi, tag):
        gbuf = jnp.concatenate([grads[_BUF[i][0]].astype(BF16).reshape(N_CHIPS, sizes[i], D) for i in range(lo, hi)],
                               axis=1)
        half = gbuf.shape[1] // 2
        from_sibling = _swap_halves(gbuf, "swap_halves_" + tag)
        my_half = lax.dynamic_slice_in_dim(gbuf, c * half, half, axis=1)
        return _add_halves(my_half, from_sibling, "add_sibling_" + tag)

    def reduce_finish(chip_sum, received, tag):
        own = lax.dynamic_index_in_dim(chip_sum, chip, axis=0, keepdims=False)
        my_rows = _add_received(own, received, "add_chips_" + tag)
        their_rows = _swap_reduced(my_rows, "swap_reduced_" + tag)
        return jnp.concatenate([jnp.where(c == 0, my_rows, their_rows), jnp.where(c == 0, their_rows, my_rows)],
                               axis=0)

    sum_b, sum16_b = reduce_start(n_first, len(_BUF), "late")
    dx0, dg1, dz1, da1, db1, u1, received_b = _ffn_bwd(dx1, xs, g1, a1, b1, f1w1, f1w3, f1w2, "ffn1_bwd",
                                                       scatter=sum16_b)
    grads["ffn1_w1"] = _tn_matmul(da1, h1, "ffn1_dw1")
    grads["ffn1_w3"] = _tn_matmul(db1, h1, "ffn1_dw3")
    grads["ffn1_w2"] = _tn_matmul(u1, dz1, "ffn1_dw2")
    sum_a, sum16_a = reduce_start(0, n_first, "first")
    reduced = jnp.concatenate([reduce_finish(sum_a, _scatter_to_owners(sum16_a, "scatter_to_owners"), "first"),
                               reduce_finish(sum_b, received_b, "late")], axis=0)

    small_rows = [dg1, dgmix, dg3, dgf,
                  jnp.pad(d_sinks.reshape(1, -1), ((0, 0), (0, D - SWA_Q_HEADS))),
                  jnp.pad(d_rel.reshape(1, -1), ((0, 0), (0, D - REL_BUCKETS * SWA_Q_HEADS))),
                  jnp.pad(loss_part, ((0, 0), (0, D - 1))), jnp.zeros((1, D), F32)]
    small = _allreduce_small(jnp.concatenate(small_rows, axis=0))
    loss = small[6, 0]

    G, g_rows = {}, {}
    for i, nk in enumerate(_BUF):
        rows = reduced[offs[i]:offs[i + 1]]
        g_rows[nk[0]] = rows.reshape(-1, W[nk[0]].shape[1]) if nk[1] == "tw" else rows
        G[nk[0]] = _from_rows(nk, rows, W[nk[0]].shape[1])[None]
    G["norm_ffn1"], G["norm_mix"], G["norm_ffn2"] = small[0:1], small[1:2], small[2:3]
    G["norm_final"] = small[3]
    G["swa_sinks"] = small[4:5, :SWA_Q_HEADS]
    G["rel_bias"] = small[5, :REL_BUCKETS * SWA_Q_HEADS].reshape(REL_BUCKETS, SWA_Q_HEADS)

    delta, new_m, new_v = {}, {}, {}
    small_names = ["norm_ffn1", "norm_mix", "norm_ffn2", "norm_final", "swa_sinks", "rel_bias"]

    def pack(d):
        return jnp.concatenate([jnp.pad(d[n].reshape(1, -1), ((0, 0), (0, D - d[n].size))) for n in small_names]
                               + [jnp.zeros((2, D), F32)], axis=0)

    sd, sm, sv = _adamw(pack(W), pack(G), pack(M), pack(V), "adamw_small")
    for r, n in enumerate(small_names):
        for dst, src in ((delta, sd), (new_m, sm), (new_v, sv)):
            dst[n] = src[r, :W[n].size].reshape(W[n].shape)
    for n, kind in _BUF:
        turn = (lambda a: a) if kind == "n" else (lambda a: a.T)
        d_, m_, v_ = _adamw(turn(W[n][0]), g_rows[n], turn(M[n][0]), turn(V[n][0]), "adamw_" + n)
        delta[n], new_m[n], new_v[n] = turn(d_)[None], turn(m_)[None], turn(v_)[None]

    return (loss, dx0[None], *[G[n] for n in names], *[delta[n] for n in names],
            *[new_m[n] for n in names], *[new_v[n] for n in names])
```

```python
import math

import jax
import jax.numpy as jnp
from jax import lax
from jax.experimental import pallas as pl
from jax.experimental.pallas import tpu as pltpu

F32, BF16 = jnp.float32, jnp.bfloat16
MESH_ID = pl.DeviceIdType.MESH
ANY = pl.BlockSpec(memory_space=pl.ANY)

RMS_EPS = 1e-6
HEAD_DIM = 64
SWA_Q_HEADS, SWA_KV_HEADS, SWA_GROUP = 8, 2, 4
SWA_BLOCK = 128
SB_HEADS = 8
SB_BLOCK = 256
REL_BUCKETS, REL_MAX_DIST = 32, 128
NEG_BIG = -1e30
QK_SCALE = HEAD_DIM ** -0.5
ADAM_LR, ADAM_B1, ADAM_B2, ADAM_EPS, ADAM_WD, ADAM_STEP = 0.001, 0.9, 0.999, 1e-08, 0.01, 10

N_CHIPS = 4
TOKEN_TILE = 512
MATMUL_TOKEN_TILE = 1024
WGRAD_ROW_TILES = (2176, 1408, 1024, 256)
FF_TILE = 2816
FFN_TOKEN_TILE = 512
PROJ_HEAD_ROWS = 768
FF_BWD_TILE = 256
VMEM_LIMIT = 56 * 1024 * 1024


def _cp(*sem):
    return pltpu.CompilerParams(dimension_semantics=sem, vmem_limit_bytes=VMEM_LIMIT)


def _nn(a, b):
    return jnp.dot(a, b, preferred_element_type=F32)


def _nt(a, b):
    return lax.dot_general(a, b, (((1,), (1,)), ((), ())), preferred_element_type=F32)


def _tn(a, b):
    return lax.dot_general(a, b, (((0,), (0,)), ((), ())), preferred_element_type=F32)


def _norm_fwd(x, g):
    return x * lax.rsqrt(jnp.mean(x * x, axis=-1, keepdims=True) + RMS_EPS) * g


def _norm_bwd(x, g, dh):
    r = lax.rsqrt(jnp.mean(x * x, axis=-1, keepdims=True) + RMS_EPS)
    xh = x * r
    dxh = dh * g
    dx = r * (dxh - xh * jnp.mean(dxh * xh, axis=-1, keepdims=True))
    return dx, jnp.sum(dh * xh, axis=0, keepdims=True)


SOFTPLUS_LINEAR = 20.0


def _softplus(z):
    return jnp.maximum(jnp.log(1.0 + jnp.exp(jnp.minimum(z, SOFTPLUS_LINEAR))), z)


def _ffn_fwd(x, g, w1t, w3t, w2, name, gather=None):
    S, D = x.shape
    F = w2.shape[0]
    tm, tf = min(FFN_TOKEN_TILE, S), FF_TILE
    ni, nj = S // tm, F // tf

    def body(x_ref, g_ref, w1_ref, w3_ref, w2_ref, *rest):
        if gather is None:
            xo_ref, h_ref, a_ref, b_ref, hs, acc = rest
        else:
            shard_ref, xo_ref, h_ref, a_ref, b_ref, gathered_ref, hs, acc, send_sems, recv_sems = rest
        i, j = pl.program_id(0), pl.program_id(1)
        if gather is not None:
            for when, phase in ((jnp.logical_and(i == 0, j == 0), "start"),
                                (jnp.logical_and(i == ni - 1, j == 0), "forward"),
                                (jnp.logical_and(i == ni - 1, j == nj - 1), "finish")):
                @pl.when(when)
                def _():
                    getattr(_gather_exchange(shard_ref, gathered_ref, send_sems, recv_sems), phase)()

        @pl.when(j == 0)
        def _():
            hb = _norm_fwd(x_ref[...], g_ref[...]).astype(BF16)
            hs[...] = hb
            h_ref[...] = hb
            acc[...] = jnp.zeros_like(acc)

        h = hs[...]
        a = _nt(h, w1_ref[...])
        b = _nt(h, w3_ref[...])
        a_ref[...] = a.astype(BF16)
        b_ref[...] = b.astype(BF16)
        u = a * jax.nn.sigmoid(a) * b
        acc[...] += _nn(u.astype(BF16), w2_ref[...])

        @pl.when(j == nj - 1)
        def _():
            xo_ref[...] = x_ref[...] + 0.5 * acc[...]

    in_specs = [pl.BlockSpec((tm, D), lambda i, j: (i, 0)),
                pl.BlockSpec((1, D), lambda i, j: (0, 0)),
                pl.BlockSpec((tf, D), lambda i, j: (j, 0), pipeline_mode=pl.Buffered(1)),
                pl.BlockSpec((tf, D), lambda i, j: (j, 0), pipeline_mode=pl.Buffered(1)),
                pl.BlockSpec((tf, D), lambda i, j: (j, 0), pipeline_mode=pl.Buffered(1))]
    out_specs = [pl.BlockSpec((tm, D), lambda i, j: (i, 0)),
                 pl.BlockSpec((tm, D), lambda i, j: (i, 0)),
                 pl.BlockSpec((tm, tf), lambda i, j: (i, j)),
                 pl.BlockSpec((tm, tf), lambda i, j: (i, j))]
    out_shape = [jax.ShapeDtypeStruct((S, D), F32), jax.ShapeDtypeStruct((S, D), BF16),
                 jax.ShapeDtypeStruct((S, F), BF16), jax.ShapeDtypeStruct((S, F), BF16)]
    scratch = [pltpu.VMEM((tm, D), BF16), pltpu.VMEM((tm, D), F32)]
    operands = [x, g, w1t, w3t, w2]
    if gather is not None:
        R, C = gather.shape
        in_specs.append(ANY)
        out_specs.append(ANY)
        out_shape.append(jax.ShapeDtypeStruct((N_CHIPS, 2, R // 2, C), gather.dtype))
        scratch += [pltpu.SemaphoreType.DMA((6,)), pltpu.SemaphoreType.DMA((6,))]
        operands.append(gather.reshape(2, R // 2, C))
    outs = list(pl.pallas_call(
        body, name=name, grid=(ni, nj), in_specs=in_specs, out_specs=out_specs, out_shape=out_shape,
        scratch_shapes=scratch, compiler_params=_cp("arbitrary", "arbitrary"),
    )(*operands))
    if gather is not None:
        outs[4] = outs[4].reshape(N_CHIPS, R, C)
    return outs


def _ffn_bwd(dxo, x, g, a, b, w1t, w3t, w2, name, scatter=None):
    S, D = x.shape
    F = w2.shape[0]
    tm, tf = min(MATMUL_TOKEN_TILE, S), FF_BWD_TILE
    ni, nj = S // tm, F // tf

    def body(dxo_ref, x_ref, g_ref, a_ref, b_ref, w1_ref, w3_ref, w2_ref, *rest):
        if scatter is None:
            dx_ref, dg_ref, dz_ref, da_ref, db_ref, u_ref, dzs, acc = rest
        else:
            (parts_ref, dx_ref, dg_ref, dz_ref, da_ref, db_ref, u_ref, recv_ref,
             dzs, acc, send_sems, recv_sems) = rest
        i, j = pl.program_id(0), pl.program_id(1)
        if scatter is not None:
            for when, phase in ((jnp.logical_and(i == 0, j == 0), "start"),
                                (jnp.logical_and(i == ni - 1, j == nj - 1), "finish")):
                @pl.when(when)
                def _():
                    getattr(_scatter_exchange(parts_ref, recv_ref, send_sems, recv_sems), phase)()

        @pl.when(j == 0)
        def _():
            dzb = (0.5 * dxo_ref[...]).astype(BF16)
            dzs[...] = dzb
            dz_ref[...] = dzb
            acc[...] = jnp.zeros_like(acc)

        du = _nt(dzs[...], w2_ref[...])
        av = a_ref[...].astype(F32)
        bv = b_ref[...].astype(F32)
        s = jax.nn.sigmoid(av)
        silu = av * s
        db = (du * silu).astype(BF16)
        da = (du * bv * (s * (1.0 + av * (1.0 - s)))).astype(BF16)
        da_ref[...] = da
        db_ref[...] = db
        u_ref[...] = (silu * bv).astype(BF16)
        acc[...] += _nn(da, w1_ref[...]) + _nn(db, w3_ref[...])

        @pl.when(j == nj - 1)
        def _():
            dx, dg = _norm_bwd(x_ref[...], g_ref[...], acc[...])
            dx_ref[...] = dxo_ref[...] + dx

            @pl.when(i == 0)
            def _():
                dg_ref[...] = dg

            @pl.when(i > 0)
            def _():
                dg_ref[...] += dg

    row = pl.BlockSpec((tm, D), lambda i, j: (i, 0))
    wsp = pl.BlockSpec((tf, D), lambda i, j: (j, 0))
    col = pl.BlockSpec((tm, tf), lambda i, j: (i, j))
    vec = pl.BlockSpec((1, D), lambda i, j: (0, 0))
    in_specs = [row, row, vec, col, col, wsp, wsp, wsp]
    out_specs = [row, vec, row, col, col, col]
    out_shape = [jax.ShapeDtypeStruct((S, D), F32), jax.ShapeDtypeStruct((1, D), F32),
                 jax.ShapeDtypeStruct((S, D), BF16), jax.ShapeDtypeStruct((S, F), BF16),
                 jax.ShapeDtypeStruct((S, F), BF16), jax.ShapeDtypeStruct((S, F), BF16)]
    scratch = [pltpu.VMEM((tm, D), BF16), pltpu.VMEM((tm, D), F32)]
    operands = [dxo, x, g, a, b, w1t, w3t, w2]
    if scatter is not None:
        in_specs.append(ANY)
        out_specs.append(ANY)
        out_shape.append(jax.ShapeDtypeStruct((3,) + scatter.shape[1:], scatter.dtype))
        scratch += [pltpu.SemaphoreType.DMA((3,)), pltpu.SemaphoreType.DMA((3,))]
        operands.append(scatter)
    return pl.pallas_call(
        body, name=name, grid=(ni, nj), in_specs=in_specs, out_specs=out_specs, out_shape=out_shape,
        scratch_shapes=scratch, compiler_params=_cp("arbitrary", "arbitrary"),
    )(*operands)


def _tn_matmul(a, b, name):
    S, M = a.shape
    N = b.shape[1]
    ts = min(MATMUL_TOKEN_TILE, S)
    tmm = next(t for t in WGRAD_ROW_TILES if M % t == 0)
    ns = S // ts

    def body(a_ref, b_ref, o_ref, acc):
        s = pl.program_id(1)
        part = _tn(a_ref[...], b_ref[...])

        @pl.when(s == 0)
        def _():
            acc[...] = part

        @pl.when(s > 0)
        def _():
            acc[...] += part

        @pl.when(s == ns - 1)
        def _():
            o_ref[...] = acc[...].astype(BF16)

    return pl.pallas_call(
        body, name=name, grid=(M // tmm, ns),
        in_specs=[pl.BlockSpec((ts, tmm), lambda m, s: (s, m)),
                  pl.BlockSpec((ts, N), lambda m, s: (s, 0))],
        out_specs=pl.BlockSpec((tmm, N), lambda m, s: (m, 0)),
        out_shape=jax.ShapeDtypeStruct((M, N), BF16),
        scratch_shapes=[pltpu.VMEM((tmm, N), F32)],
        compiler_params=_cp("arbitrary", "arbitrary"),
    )(a, b)


def _norm_matmul_nt(x, g, wt, out_dtype, name):
    S, D = x.shape
    N = wt.shape[0]
    tm = min(MATMUL_TOKEN_TILE, S)
    tn = next(t for t in (1024, 768, 256) if N % t == 0)

    def body(x_ref, g_ref, w_ref, o_ref, h_ref, hs):
        @pl.when(pl.program_id(1) == 0)
        def _():
            hb = _norm_fwd(x_ref[...], g_ref[...]).astype(BF16)
            hs[...] = hb
            h_ref[...] = hb

        o_ref[...] = _nt(hs[...], w_ref[...]).astype(out_dtype)

    return pl.pallas_call(
        body, name=name, grid=(S // tm, N // tn),
        in_specs=[pl.BlockSpec((tm, D), lambda i, j: (i, 0)),
                  pl.BlockSpec((1, D), lambda i, j: (0, 0)),
                  pl.BlockSpec((tn, D), lambda i, j: (j, 0))],
        out_specs=[pl.BlockSpec((tm, tn), lambda i, j: (i, j)),
                   pl.BlockSpec((tm, D), lambda i, j: (i, 0))],
        out_shape=[jax.ShapeDtypeStruct((S, N), out_dtype), jax.ShapeDtypeStruct((S, D), BF16)],
        scratch_shapes=[pltpu.VMEM((tm, D), BF16)],
        compiler_params=_cp("arbitrary", "arbitrary"),
    )(x, g, wt)


def _heads_tile(ref):
    Hh, nbk = ref.shape[0], ref.shape[1]
    return jnp.concatenate([jnp.concatenate([ref[h, b] for b in range(nbk)], axis=1) for h in range(Hh)], axis=0)


def _store_heads(ref, val):
    Hh, nbk, dh, T = ref.shape
    for h in range(Hh):
        for b in range(nbk):
            ref[h, b] = val[h * dh:(h + 1) * dh, b * T:(b + 1) * T].astype(ref.dtype)


def _norm_proj_heads(x, g, w_rows, T, name):
    S, D = x.shape
    N = w_rows.shape[0]
    tm, tn = min(MATMUL_TOKEN_TILE, S), PROJ_HEAD_ROWS

    def body(x_ref, g_ref, w_ref, o_ref, hs):
        @pl.when(pl.program_id(1) == 0)
        def _():
            hs[...] = _norm_fwd(x_ref[...], g_ref[...]).astype(BF16)

        _store_heads(o_ref, _nt(w_ref[...], hs[...]))

    return pl.pallas_call(
        body, name=name, grid=(S // tm, N // tn),
        in_specs=[pl.BlockSpec((tm, D), lambda i, j: (i, 0)),
                  pl.BlockSpec((1, D), lambda i, j: (0, 0)),
                  pl.BlockSpec((tn, D), lambda i, j: (j, 0))],
        out_specs=pl.BlockSpec((tn // HEAD_DIM, tm // T, HEAD_DIM, T), lambda i, j: (j, i, 0, 0)),
        out_shape=jax.ShapeDtypeStruct((N // HEAD_DIM, S // T, HEAD_DIM, T), BF16),
        scratch_shapes=[pltpu.VMEM((tm, D), BF16)],
        compiler_params=_cp("arbitrary", "arbitrary"),
    )(x, g, w_rows)


def _heads_matmul(pieces, b, name):
    S, N = b.shape
    ts = min(MATMUL_TOKEN_TILE, S)
    ns = S // ts
    rows = [at.shape[0] * at.shape[2] for at, _ in pieces]

    def body(*refs):
        a_refs, b_ref, o_ref = refs[:-2], refs[-2], refs[-1]
        s = pl.program_id(0)
        row0 = 0
        for a_ref, (_, scale), n in zip(a_refs, pieces, rows):
            a = _heads_tile(a_ref)
            part = _nn((a if scale == 1.0 else a * scale).astype(BF16), b_ref[...])
            out = o_ref.at[row0:row0 + n, :]
            row0 += n

            @pl.when(s == 0)
            def _():
                out[...] = part

            @pl.when(s > 0)
            def _():
                out[...] += part

    return pl.pallas_call(
        body, name=name, grid=(ns,),
        in_specs=[pl.BlockSpec((at.shape[0], ts // at.shape[3], at.shape[2], at.shape[3]), lambda s: (0, s, 0, 0))
                  for at, _ in pieces] + [pl.BlockSpec((ts, N), lambda s: (s, 0))],
        out_specs=pl.BlockSpec((sum(rows), N), lambda s: (0, 0)),
        out_shape=jax.ShapeDtypeStruct((sum(rows), N), F32),
        compiler_params=_cp("arbitrary"),
    )(*[at for at, _ in pieces], b)


def _proj_bwd(pieces, dgates, w_rows, x, g, dres):
    S, D = x.shape
    tm = min(TOKEN_TILE, S)
    n_p = len(pieces)
    gate_row = w_rows.shape[0] - dgates.shape[1]

    def body(*refs):
        p_refs = refs[:n_p]
        dgt_ref, w_ref, x_ref, g_ref, dres_ref, dx_ref, dg_ref = refs[n_p:]
        i = pl.program_id(0)
        dh = _nn(dgt_ref[...], w_ref[gate_row:, :])
        for p_ref, (arr, row0) in zip(p_refs, pieces):
            rows = arr.shape[0] * arr.shape[2]
            dh += _tn(_heads_tile(p_ref).astype(BF16), w_ref[row0:row0 + rows, :])
        dx, dg = _norm_bwd(x_ref[...], g_ref[...], dh)
        dx_ref[...] = dres_ref[...] + dx

        @pl.when(i == 0)
        def _():
            dg_ref[...] = dg

        @pl.when(i > 0)
        def _():
            dg_ref[...] += dg

    row = pl.BlockSpec((tm, D), lambda i: (i, 0))
    vec = pl.BlockSpec((1, D), lambda i: (0, 0))
    p_specs = [pl.BlockSpec((a.shape[0], tm // a.shape[3], a.shape[2], a.shape[3]), lambda i: (0, i, 0, 0))
               for a, _ in pieces]
    return pl.pallas_call(
        body, name="proj_bwd", grid=(S // tm,),
        in_specs=p_specs + [pl.BlockSpec((tm, dgates.shape[1]), lambda i: (i, 0)),
                            pl.BlockSpec(w_rows.shape, lambda i: (0, 0), pipeline_mode=pl.Buffered(1)),
                            row, vec, row],
        out_specs=[row, vec],
        out_shape=[jax.ShapeDtypeStruct((S, D), F32), jax.ShapeDtypeStruct((1, D), F32)],
        compiler_params=_cp("arbitrary"),
    )(*[a for a, _ in pieces], dgates, w_rows, x, g, dres)


def _merge_fwd(x1, gates, oa_t, ob_t, wat, wbt, w_out):
    S, D = x1.shape
    W = wat.shape[1]
    tm = min(TOKEN_TILE, S)

    def body(x_ref, ga_ref, gb_ref, oa_ref, ob_ref, wa_ref, wb_ref, wo_ref,
             x2_ref, mg_ref, ba_ref, bb_ref):
        ba = _nt(_heads_tile(oa_ref).T.astype(BF16), wa_ref[...])
        bb = _nt(_heads_tile(ob_ref).T.astype(BF16), wb_ref[...])
        merged = jax.nn.sigmoid(ga_ref[...]) * ba + jax.nn.sigmoid(gb_ref[...]) * bb
        mb = merged.astype(BF16)
        mg_ref[...] = mb
        ba_ref[...] = ba.astype(BF16)
        bb_ref[...] = bb.astype(BF16)
        x2_ref[...] = x_ref[...] + _nn(mb, wo_ref[...])

    row = pl.BlockSpec((tm, D), lambda i: (i, 0))
    full = lambda r, c: pl.BlockSpec((r, c), lambda i: (0, 0))
    heads = lambda a: pl.BlockSpec((a.shape[0], tm // a.shape[3], a.shape[2], a.shape[3]), lambda i: (0, i, 0, 0))
    return pl.pallas_call(
        body, name="merge_fwd", grid=(S // tm,),
        in_specs=[row, pl.BlockSpec((tm, D), lambda i: (i, 0)), pl.BlockSpec((tm, D), lambda i: (i, 1)),
                  heads(oa_t), heads(ob_t), full(D, W), full(D, W), full(D, D)],
        out_specs=[row, row, row, row],
        out_shape=[jax.ShapeDtypeStruct((S, D), F32)] + [jax.ShapeDtypeStruct((S, D), BF16)] * 3,
        compiler_params=_cp("arbitrary"),
    )(x1, gates, gates, oa_t, ob_t, wat, wbt, w_out)


def _merge_bwd(dx2, gates, ba, bb, wa, wb, w_out, t_a, t_b):
    S, D = dx2.shape
    W = wa.shape[0]
    tm = min(TOKEN_TILE, S)
    Hh = W // HEAD_DIM

    def body(dx_ref, ga_ref, gb_ref, ba_ref, bb_ref, wa_ref, wb_ref, wo_ref,
             dxb_ref, dba_ref, dbb_ref, dgt_ref, doa_ref, dob_ref):
        dxb = dx_ref[...].astype(BF16)
        dxb_ref[...] = dxb
        dm = _nt(dxb, wo_ref[...])
        sa = jax.nn.sigmoid(ga_ref[...])
        sb = jax.nn.sigmoid(gb_ref[...])
        dba = (dm * sa).astype(BF16)
        dbb = (dm * sb).astype(BF16)
        dba_ref[...] = dba
        dbb_ref[...] = dbb
        dgt_ref[:, :D] = (dm * ba_ref[...].astype(F32) * sa * (1.0 - sa)).astype(BF16)
        dgt_ref[:, D:] = (dm * bb_ref[...].astype(F32) * sb * (1.0 - sb)).astype(BF16)
        _store_heads(doa_ref, _nt(wa_ref[...], dba))
        _store_heads(dob_ref, _nt(wb_ref[...], dbb))

    row = pl.BlockSpec((tm, D), lambda i: (i, 0))
    full = lambda r, c: pl.BlockSpec((r, c), lambda i: (0, 0))
    heads = lambda T: pl.BlockSpec((Hh, tm // T, HEAD_DIM, T), lambda i: (0, i, 0, 0))
    return pl.pallas_call(
        body, name="merge_bwd", grid=(S // tm,),
        in_specs=[row, pl.BlockSpec((tm, D), lambda i: (i, 0)), pl.BlockSpec((tm, D), lambda i: (i, 1)),
                  row, row, full(W, D), full(W, D), full(D, D)],
        out_specs=[row, row, row, pl.BlockSpec((tm, 2 * D), lambda i: (i, 0)), heads(t_a), heads(t_b)],
        out_shape=[jax.ShapeDtypeStruct((S, D), BF16)] * 3 + [jax.ShapeDtypeStruct((S, 2 * D), BF16),
                   jax.ShapeDtypeStruct((Hh, S // t_a, HEAD_DIM, t_a), F32),
                   jax.ShapeDtypeStruct((Hh, S // t_b, HEAD_DIM, t_b), BF16)],
        compiler_params=_cp("arbitrary"),
    )(dx2, gates, gates, ba, bb, wa, wb, w_out)


def _final_loss(x3, gf, target):
    S, D = x3.shape
    tm = min(TOKEN_TILE, S)

    def body(x_ref, g_ref, t_ref, loss_ref, dx_ref, dg_ref):
        i = pl.program_id(0)
        x = x_ref[...]
        g = g_ref[...]
        e = _norm_fwd(x, g) - t_ref[...]
        part = 0.5 * jnp.sum(jnp.mean(e * e, axis=-1, keepdims=True), axis=0, keepdims=True)
        dx, dg = _norm_bwd(x, g, e * (1.0 / D))
        dx_ref[...] = dx

        @pl.when(i == 0)
        def _():
            loss_ref[...] = part
            dg_ref[...] = dg

        @pl.when(i > 0)
        def _():
            loss_ref[...] += part
            dg_ref[...] += dg

    row = pl.BlockSpec((tm, D), lambda i: (i, 0))
    vec = pl.BlockSpec((1, D), lambda i: (0, 0))
    return pl.pallas_call(
        body, name="final_loss", grid=(S // tm,),
        in_specs=[row, vec, row],
        out_specs=[pl.BlockSpec((1, 1), lambda i: (0, 0)), row, vec],
        out_shape=[jax.ShapeDtypeStruct((1, 1), F32), jax.ShapeDtypeStruct((S, D), F32),
                   jax.ShapeDtypeStruct((1, D), F32)],
        compiler_params=_cp("arbitrary"),
    )(x3, gf, target)


SB_FWD_HEAD_GROUP = 8
SB_HEAD_GROUP = 4
LANES = 128


def _tri(T, kind):
    r = lax.broadcasted_iota(jnp.int32, (T, T), 0)
    c = lax.broadcasted_iota(jnp.int32, (T, T), 1)
    return {"after": r > c, "before": r < c}[kind].astype(BF16)


def _lane(v, j):
    return jnp.broadcast_to(v[:, j:j + 1], (v.shape[0], LANES))


def _t_bf16(x):
    return x.astype(F32).T.astype(BF16)


def _wide(v, T):
    return jnp.tile(v, (1, T // LANES))


SB_SLOTS = 3
SB_FWD_SLOTS = 2
COPY_PARTS = 4


class _split_copy:
    def __init__(self, src, dst, sems):
        n = src.shape[0] // COPY_PARTS
        self.parts = [pltpu.make_async_copy(src.at[pl.ds(r * n, n)], dst.at[pl.ds(r * n, n)], sems.at[r])
                      for r in range(COPY_PARTS)]

    def start(self):
        for cp in self.parts:
            cp.start()

    def wait(self):
        for cp in self.parts:
            cp.wait()


def _sb_pair(i, kb):
    return (i * (i + 1)) // 2 + kb


def _sb_fwd(qkv):
    H3, nb, dh, T = qkv.shape
    H = H3 // 3
    HG = SB_FWD_HEAD_GROUP
    assert HG == H, "one head group: a saved tile holds all the heads"
    n_pairs = (nb * (nb + 1)) // 2

    def body(q_ref, k_ref, v_ref, o_ref, saved_ref, stage, sems):
        row = lax.broadcasted_iota(jnp.int32, (T, T), 0)
        col = lax.broadcasted_iota(jnp.int32, (T, T), 1)
        tri = col < row
        after = _tri(T, "after")

        def save(slot, pair):
            return _split_copy(stage.at[slot], saved_ref.at[pair], sems.at[slot])

        def blocks(qs, i, kb, step, carry, diag):
            hs = range(HG)
            slot = step % SB_FWD_SLOTS

            @pl.when(step >= SB_FWD_SLOTS)
            def _():
                save(slot, 0).wait()

            z = [_nn(qs[hh], k_ref[hh, kb]) for hh in hs]
            res, ls, first = [None] * HG, [None] * HG, [None] * HG
            for hh in hs:
                sp = _softplus(z[hh])
                if diag:
                    sp = jnp.where(tri, sp, 0.0)
                ls[hh] = z[hh] - sp
                spb = sp.astype(BF16)
                first[hh] = _lane(spb.astype(F32), 0)
                res[hh] = _nn(spb, after)
            out = []
            for hh in hs:
                c, oacc = carry[2 * hh], carry[2 * hh + 1]
                a = jnp.exp(ls[hh] - (res[hh] + _wide(c, T)))
                if diag:
                    a = jnp.where(tri, a, 0.0)
                ab = a.astype(BF16)
                stage[slot, hh, 0] = ab
                stage[slot, hh, 1] = jnp.exp(ls[hh]).astype(BF16)
                out.extend([c + (first[hh] + _lane(res[hh], 0)), oacc + _nt(v_ref[hh, kb], ab)])
            save(slot, _sb_pair(i, kb)).start()
            return tuple(out)

        def qblock(i, step):
            qs = [_t_bf16(q_ref[hh, i]) for hh in range(HG)]
            carry = blocks(qs, i, i, step, (jnp.zeros((T, LANES), F32), jnp.zeros((dh, T), F32)) * HG, True)

            def kstep(t, carry):
                return blocks(qs, i, i - 1 - t, step + 1 + t, carry, False)

            carry = lax.fori_loop(0, i, kstep, carry)
            for hh in range(HG):
                o_ref[hh, i] = carry[2 * hh + 1]
            return step + 1 + i

        lax.fori_loop(0, nb, qblock, 0)
        for slot in range(min(SB_FWD_SLOTS, n_pairs)):
            save(slot, 0).wait()

    ht = lambda part: pl.BlockSpec((HG, nb, dh, T), lambda h: (part, 0, 0, 0), pipeline_mode=pl.Buffered(1))
    return pl.pallas_call(
        body, name="sb_fwd", grid=(1,),
        in_specs=[ht(0), ht(1), ht(2)],
        out_specs=[ht(0), ANY],
        out_shape=[jax.ShapeDtypeStruct((H, nb, dh, T), F32),
                   jax.ShapeDtypeStruct((n_pairs, H, 2, T, T), BF16)],
        scratch_shapes=[pltpu.VMEM((SB_FWD_SLOTS, HG, 2, T, T), BF16),
                        pltpu.SemaphoreType.DMA((SB_FWD_SLOTS, COPY_PARTS))],
        compiler_params=_cp("arbitrary"),
    )(qkv, qkv, qkv)


def _sb_bwd(qkv, dot, saved):
    H3, nb, dh, T = qkv.shape
    H = H3 // 3
    HG = SB_HEAD_GROUP
    n_pairs = (nb * (nb + 1)) // 2

    def body(qt_ref, k_ref, v_ref, dot_ref, saved_ref, dq_ref, dk_ref, dv_ref, stage, sems):
        head0 = pl.program_id(0) * HG
        row = lax.broadcasted_iota(jnp.int32, (T, T), 0)
        col = lax.broadcasted_iota(jnp.int32, (T, T), 1)
        tri = col < row
        before = _tri(T, "before")
        dk_ref[...] = jnp.zeros_like(dk_ref)
        dv_ref[...] = jnp.zeros_like(dv_ref)

        def fetch(slot, pair):
            return _split_copy(saved_ref.at[pair, pl.ds(head0, HG)], stage.at[slot], sems.at[slot])

        for ahead in range(min(SB_SLOTS - 1, n_pairs)):
            fetch(ahead, ahead).start()

        def blocks(qTs, dos, doTs, i, kb, carry, diag):
            hs = range(HG)
            pair = _sb_pair(i, kb)
            slot = pair % SB_SLOTS
            fetch(slot, pair).wait()
            nxt = pair + (SB_SLOTS - 1)

            @pl.when(nxt < n_pairs)
            def _():
                fetch(nxt % SB_SLOTS, nxt).start()

            kT = [k_ref[hh, kb] for hh in hs]
            da = [_nn(dos[hh], v_ref[hh, kb]) for hh in hs]
            g, gb, resg = [None] * HG, [None] * HG, [None] * HG
            for hh in hs:
                g[hh] = stage[slot, hh, 0].astype(F32) * da[hh]
                gb[hh] = g[hh].astype(BF16)
                resg[hh] = _nn(gb[hh], before)
            out = []
            for hh in hs:
                pre_g, dq = carry[2 * hh], carry[2 * hh + 1]
                dz = g[hh] - (g[hh] + (resg[hh] + _wide(pre_g, T))) * stage[slot, hh, 1].astype(F32)
                if diag:
                    dz = jnp.where(tri, dz, 0.0)
                dzb = dz.astype(BF16)
                dk_ref[hh, kb] += _nn(qTs[hh], dzb)
                dv_ref[hh, kb] += _nn(doTs[hh], stage[slot, hh, 0])
                out.extend([pre_g + (_lane(resg[hh], T - 1) + _lane(gb[hh].astype(F32), T - 1)),
                            dq + _nt(kT[hh], dzb)])
            return tuple(out)

        def qblock(i, _):
            qTs = [qt_ref[hh, i] for hh in range(HG)]
            doTs = [dot_ref[hh, i] for hh in range(HG)]
            dos = [_t_bf16(v) for v in doTs]
            carry = (jnp.zeros((T, LANES), F32), jnp.zeros((dh, T), F32)) * HG

            def kstep(kb, carry):
                return blocks(qTs, dos, doTs, i, kb, carry, False)

            carry = lax.fori_loop(0, i, kstep, carry)
            carry = blocks(qTs, dos, doTs, i, i, carry, True)
            for hh in range(HG):
                dq_ref[hh, i] = carry[2 * hh + 1]
            return 0

        lax.fori_loop(0, nb, qblock, 0)

    G = H // HG
    ht = lambda part: pl.BlockSpec((HG, nb, dh, T), lambda h: (h + part * G, 0, 0, 0),
                                   pipeline_mode=pl.Buffered(1))
    return pl.pallas_call(
        body, name="sb_bwd", grid=(G,),
        in_specs=[ht(0), ht(1), ht(2), ht(0), ANY],
        out_specs=[ht(0), ht(0), ht(0)],
        out_shape=[jax.ShapeDtypeStruct((H, nb, dh, T), F32)] * 3,
        scratch_shapes=[pltpu.VMEM((SB_SLOTS, HG, 2, T, T), BF16), pltpu.SemaphoreType.DMA((SB_SLOTS, COPY_PARTS))],
        compiler_params=_cp("arbitrary"),
    )(qkv, qkv, qkv, dot, saved)


def _swa_probs(zp, zc, bias, sink, first):
    T = zp.shape[0]
    key = lax.broadcasted_iota(jnp.int32, (T, T), 0)
    qry = lax.broadcasted_iota(jnp.int32, (T, T), 1)
    lp = jnp.where(jnp.logical_and(key > qry, jnp.logical_not(first)), zp + bias[:T, :], NEG_BIG)
    lc = jnp.where(key <= qry, zc + bias[T:, :], NEG_BIG)
    m = jnp.maximum(jnp.maximum(jnp.max(lp, axis=0, keepdims=True), jnp.max(lc, axis=0, keepdims=True)), sink)
    pp = jnp.exp(lp - m)
    pc = jnp.exp(lc - m)
    ps = jnp.exp(sink - m)
    inv = 1.0 / (jnp.sum(pp, axis=0, keepdims=True) + jnp.sum(pc, axis=0, keepdims=True) + ps)
    return pp * inv, pc * inv, ps * inv


def _swa_specs(nb, dh, T, Hq, Hkv, clamp):
    blk = (lambda n: jnp.minimum(n, nb - 1)) if clamp else (lambda n: n)
    q = pl.BlockSpec((Hq, None, dh, T), lambda n: (0, blk(n), 0, 0))
    kv = lambda first, back: pl.BlockSpec(
        (Hkv, None, dh, T), lambda n: (first // Hkv, jnp.maximum(blk(n) - back, 0) if back else blk(n), 0, 0))
    return q, [kv(Hq, 1), kv(Hq, 0), kv(Hq + Hkv, 1), kv(Hq + Hkv, 0)]


def _swa_fwd(qkv, bias, sinks):
    Hq, Hkv, grp = SWA_Q_HEADS, SWA_KV_HEADS, SWA_GROUP
    _, nb, dh, T = qkv.shape

    def body(sink_ref, q_ref, kp_ref, kc_ref, vp_ref, vc_ref, bias_ref, o_ref):
        n = pl.program_id(0)
        kpn = [_t_bf16(kp_ref[hk]) for hk in range(Hkv)]
        kcn = [_t_bf16(kc_ref[hk]) for hk in range(Hkv)]
        zs = [(_nn(kpn[h // grp], q_ref[h]), _nn(kcn[h // grp], q_ref[h])) for h in range(Hq)]
        for h in range(Hq):
            pp, pc, _ = _swa_probs(*zs[h], bias_ref[h], sink_ref[h], n == 0)
            o_ref[h] = _nn(vp_ref[h // grp], pp.astype(BF16)) + _nn(vc_ref[h // grp], pc.astype(BF16))

    q_spec, kv_specs = _swa_specs(nb, dh, T, Hq, Hkv, False)
    return pl.pallas_call(
        body, name="swa_fwd", grid=(nb,),
        in_specs=[pl.BlockSpec(memory_space=pltpu.SMEM), q_spec] + kv_specs
                 + [pl.BlockSpec((Hq, 2 * T, T), lambda n: (0, 0, 0))],
        out_specs=pl.BlockSpec((Hq, None, dh, T), lambda n: (0, n, 0, 0)),
        out_shape=jax.ShapeDtypeStruct((Hq, nb, dh, T), F32),
        compiler_params=_cp("arbitrary"),
    )(sinks, qkv, qkv, qkv, qkv, qkv, bias)


def _swa_bwd(qkv, bias, sinks, dot, ot):
    Hq, Hkv, grp = SWA_Q_HEADS, SWA_KV_HEADS, SWA_GROUP
    _, nb, dh, T = qkv.shape

    def body(sink_ref, qt_ref, kp_ref, kc_ref, vp_ref, vc_ref, bias_ref, dot_ref, ot_ref,
             dq_ref, dk_ref, dv_ref, dbias_ref, dsink_ref, ck, cv):
        n = pl.program_id(0)

        @pl.when(n == 0)
        def _():
            dbias_ref[...] = jnp.zeros_like(dbias_ref)
            dsink_ref[...] = jnp.zeros_like(dsink_ref)
            ck[...] = jnp.zeros_like(ck)
            cv[...] = jnp.zeros_like(cv)

        @pl.when(n < nb)
        def _():
            kp, kc = [kp_ref[hk] for hk in range(Hkv)], [kc_ref[hk] for hk in range(Hkv)]
            kpn, kcn = [_t_bf16(v) for v in kp], [_t_bf16(v) for v in kc]
            vpn = [_t_bf16(vp_ref[hk]) for hk in range(Hkv)]
            vcn = [_t_bf16(vc_ref[hk]) for hk in range(Hkv)]
            qTs = [qt_ref[h] for h in range(Hq)]
            doTs = [dot_ref[h].astype(BF16) for h in range(Hq)]
            zs = [(_nn(kpn[h // grp], qTs[h]), _nn(kcn[h // grp], qTs[h])) for h in range(Hq)]
            dps = [(_nn(vpn[h // grp], doTs[h]), _nn(vcn[h // grp], doTs[h])) for h in range(Hq)]
            dls, pbs = [], []
            for h in range(Hq):
                pp, pc, ps = _swa_probs(*zs[h], bias_ref[h], sink_ref[h], n == 0)
                delta = jnp.sum(dot_ref[h] * ot_ref[h], axis=0, keepdims=True)
                dlp = pp * (dps[h][0] - delta)
                dlc = pc * (dps[h][1] - delta)
                dbias_ref[h, :T, :] += dlp
                dbias_ref[h, T:, :] += dlc
                dsink_ref[h] += -ps * delta
                dls.append((dlp.astype(BF16), dlc.astype(BF16)))
                pbs.append((pp.astype(BF16), pc.astype(BF16)))
            zero = jnp.zeros((dh, T), F32)
            kprev, kcur, vprev, vcur = [zero] * Hkv, [zero] * Hkv, [zero] * Hkv, [zero] * Hkv
            for h in range(Hq):
                hk = h // grp
                dlpb, dlcb = dls[h]
                dq_ref[h] = _nn(kp[hk], dlpb) + _nn(kc[hk], dlcb)
                kprev[hk] = kprev[hk] + _nt(qTs[h], dlpb)
                kcur[hk] = kcur[hk] + _nt(qTs[h], dlcb)
                vprev[hk] = vprev[hk] + _nt(doTs[h], pbs[h][0])
                vcur[hk] = vcur[hk] + _nt(doTs[h], pbs[h][1])
            for hk in range(Hkv):
                dk_ref[hk] = ck[hk] + kprev[hk]
                dv_ref[hk] = cv[hk] + vprev[hk]
                ck[hk] = kcur[hk]
                cv[hk] = vcur[hk]

        @pl.when(n == nb)
        def _():
            dk_ref[...] = ck[...]
            dv_ref[...] = cv[...]

    qt_spec, kv_specs = _swa_specs(nb, dh, T, Hq, Hkv, True)
    prev = pl.BlockSpec((Hkv, None, dh, T), lambda n: (0, jnp.maximum(n - 1, 0), 0, 0))
    whole = lambda a, b: pl.BlockSpec((Hq, a, b), lambda n: (0, 0, 0))
    return pl.pallas_call(
        body, name="swa_bwd", grid=(nb + 1,),
        in_specs=[pl.BlockSpec(memory_space=pltpu.SMEM), qt_spec] + kv_specs
                 + [whole(2 * T, T), qt_spec, qt_spec],
        out_specs=[qt_spec, prev, prev, whole(2 * T, T), whole(1, T)],
        out_shape=[jax.ShapeDtypeStruct((Hq, nb, dh, T), F32), jax.ShapeDtypeStruct((Hkv, nb, dh, T), F32),
                   jax.ShapeDtypeStruct((Hkv, nb, dh, T), F32), jax.ShapeDtypeStruct((Hq, 2 * T, T), F32),
                   jax.ShapeDtypeStruct((Hq, 1, T), F32)],
        scratch_shapes=[pltpu.VMEM((Hkv, dh, T), F32), pltpu.VMEM((Hkv, dh, T), F32)],
        compiler_params=_cp("arbitrary"),
    )(sinks, qkv, qkv, qkv, qkv, qkv, bias, dot, ot)


def _split3(x):
    h1 = x.astype(BF16)
    r1 = x - h1.astype(F32)
    h2 = r1.astype(BF16)
    h3 = (r1 - h2.astype(F32)).astype(BF16)
    return h1, h2, h3


def _bias_expand(rel_t, onehot):
    Hq, NB = rel_t.shape
    L = onehot.shape[1]

    def body(r_ref, oh_ref, o_ref):
        h1, h2, h3 = _split3(r_ref[...])
        oh = oh_ref[...]
        o_ref[...] = _nn(h1, oh) + _nn(h2, oh) + _nn(h3, oh)

    return pl.pallas_call(
        body, name="bias_expand", grid=(1,),
        in_specs=[pl.BlockSpec((Hq, NB), lambda i: (0, 0)), pl.BlockSpec((NB, L), lambda i: (0, 0))],
        out_specs=pl.BlockSpec((Hq, L), lambda i: (0, 0)),
        out_shape=jax.ShapeDtypeStruct((Hq, L), F32),
        compiler_params=_cp("arbitrary"),
    )(rel_t, onehot)


def _bias_reduce(dbias, onehot):
    Hq, L = dbias.shape
    NB = onehot.shape[0]

    def body(d_ref, oh_ref, o_ref):
        h1, h2, h3 = _split3(d_ref[...])
        oh = oh_ref[...]
        o_ref[...] = _nt(h1, oh) + _nt(h2, oh) + _nt(h3, oh)

    return pl.pallas_call(
        body, name="bias_reduce", grid=(1,),
        in_specs=[pl.BlockSpec((Hq, L), lambda i: (0, 0)), pl.BlockSpec((NB, L), lambda i: (0, 0))],
        out_specs=pl.BlockSpec((Hq, NB), lambda i: (0, 0)),
        out_shape=jax.ShapeDtypeStruct((Hq, NB), F32),
        compiler_params=_cp("arbitrary"),
    )(dbias, onehot)


def _adamw(w, g, m, v, name):
    R, C = w.shape
    tr = 256 if R % 256 == 0 else R
    bc1 = 1.0 - ADAM_B1 ** ADAM_STEP
    bc2 = 1.0 - ADAM_B2 ** ADAM_STEP

    def body(w_ref, g_ref, m_ref, v_ref, d_ref, nm_ref, nv_ref):
        g = g_ref[...]
        m2 = ADAM_B1 * m_ref[...] + (1.0 - ADAM_B1) * g
        v2 = ADAM_B2 * v_ref[...] + (1.0 - ADAM_B2) * (g * g)
        nm_ref[...] = m2
        nv_ref[...] = v2
        d_ref[...] = -ADAM_LR * ((m2 / bc1) / (jnp.sqrt(v2 / bc2) + ADAM_EPS) + ADAM_WD * w_ref[...])

    spec = pl.BlockSpec((tr, C), lambda i: (i, 0))
    return pl.pallas_call(
        body, name=name, grid=(R // tr,),
        in_specs=[spec] * 4, out_specs=[spec] * 3,
        out_shape=[jax.ShapeDtypeStruct((R, C), F32)] * 3,
        compiler_params=_cp("arbitrary"),
    )(w, g, m, v)


def _row_tile(R):
    return max(t for t in range(16, 513, 16) if R % t == 0)


def _add_halves(mine, recv, name):
    K, R, C = mine.shape
    tr = _row_tile(R)

    def body(a_ref, b_ref, o_ref, ob_ref):
        s = a_ref[...].astype(F32) + b_ref[...].astype(F32)
        o_ref[...] = s
        ob_ref[...] = s.astype(BF16)

    spec = pl.BlockSpec((None, tr, C), lambda k, i: (k, i, 0))
    return pl.pallas_call(
        body, name=name, grid=(K, R // tr),
        in_specs=[spec, spec], out_specs=[spec, spec],
        out_shape=[jax.ShapeDtypeStruct((K, R, C), F32), jax.ShapeDtypeStruct((K, R, C), BF16)],
        compiler_params=_cp("arbitrary", "arbitrary"),
    )(mine, recv)


def _add_received(own, recv, name):
    R, C = own.shape
    tr = _row_tile(R)

    def body(a_ref, r_ref, o_ref):
        o_ref[...] = ((a_ref[...] + r_ref[0].astype(F32)) + r_ref[1].astype(F32)) + r_ref[2].astype(F32)

    return pl.pallas_call(
        body, name=name, grid=(R // tr,),
        in_specs=[pl.BlockSpec((tr, C), lambda i: (i, 0)), pl.BlockSpec((3, tr, C), lambda i: (0, i, 0))],
        out_specs=pl.BlockSpec((tr, C), lambda i: (i, 0)),
        out_shape=jax.ShapeDtypeStruct((R, C), F32),
        compiler_params=_cp("arbitrary"),
    )(own, recv)


def _position():
    x, y, c = lax.axis_index("x"), lax.axis_index("y"), lax.axis_index("c")
    others = [(1 - x, y), (x, 1 - y), (1 - x, 1 - y)]
    return x, y, c, others


def _remote(src, dst, send_sems, recv_sems, k, dev):
    return pltpu.make_async_remote_copy(src_ref=src, dst_ref=dst, send_sem=send_sems.at[k],
                                        recv_sem=recv_sems.at[k], device_id=dev, device_id_type=MESH_ID)


class _gather_exchange:
    def __init__(self, src, out, send_sems, recv_sems):
        x, y, c, others = _position()
        mine, sibling = 2 * x + y, (x, y, 1 - c)
        self.sends, self.arrivals, self.passes, self.from_sibling = [], [], [], []
        for j, (ox, oy) in enumerate(others):
            slot = out.at[2 * ox + oy, c]
            theirs = out.at[2 * ox + oy, 1 - c]
            self.sends.append(_remote(src.at[c], out.at[mine, c], send_sems, recv_sems, j, (ox, oy, c)))
            self.arrivals.append(_remote(slot, slot, send_sems, recv_sems, j, (ox, oy, c)))
            self.passes.append(_remote(slot, slot, send_sems, recv_sems, 3 + j, sibling))
            self.from_sibling.append(_remote(theirs, theirs, send_sems, recv_sems, 3 + j, sibling))

    def start(self):
        for cp in self.sends:
            cp.start()

    def forward(self):
        for arrived, onward in zip(self.arrivals, self.passes):
            arrived.wait_recv()
            onward.start()

    def finish(self):
        for cp in self.from_sibling:
            cp.wait_recv()
        for cp in self.sends + self.passes:
            cp.wait_send()


def _gather_weights(shard):
    R, C = shard.shape
    half = R // 2

    def body(src, out, send_sems, recv_sems):
        ex = _gather_exchange(src, out, send_sems, recv_sems)
        ex.start()
        ex.forward()
        ex.finish()

    return pl.pallas_call(
        body, name="gather_weights",
        in_specs=[ANY], out_specs=ANY,
        out_shape=jax.ShapeDtypeStruct((N_CHIPS, 2, half, C), shard.dtype),
        scratch_shapes=[pltpu.SemaphoreType.DMA((6,)), pltpu.SemaphoreType.DMA((6,))],
    )(shard.reshape(2, half, C)).reshape(N_CHIPS, R, C)


def _swap_halves(grads, name):
    K, R, C = grads.shape
    half = R // 2

    def body(src, out, send_sems, recv_sems):
        x, y, c, _ = _position()
        theirs = src.at[:, pl.ds(pl.multiple_of((1 - c) * half, 16), half), :]
        cp = _remote(theirs, out, send_sems, recv_sems, 0, (x, y, 1 - c))
        cp.start()
        cp.wait()

    return pl.pallas_call(
        body, name=name,
        in_specs=[ANY], out_specs=ANY,
        out_shape=jax.ShapeDtypeStruct((K, half, C), grads.dtype),
        scratch_shapes=[pltpu.SemaphoreType.DMA((1,)), pltpu.SemaphoreType.DMA((1,))],
    )(grads)


class _scatter_exchange:
    def __init__(self, src, out, send_sems, recv_sems):
        x, y, c, others = _position()
        self.copies = [_remote(src.at[2 * ox + oy], out.at[j], send_sems, recv_sems, j, (ox, oy, c))
                       for j, (ox, oy) in enumerate(others)]

    def start(self):
        for cp in self.copies:
            cp.start()

    def finish(self):
        for cp in self.copies:
            cp.wait()


def _scatter_to_owners(parts, name):
    K, H, C = parts.shape

    def body(src, out, send_sems, recv_sems):
        ex = _scatter_exchange(src, out, send_sems, recv_sems)
        ex.start()
        ex.finish()

    return pl.pallas_call(
        body, name=name,
        in_specs=[ANY], out_specs=ANY,
        out_shape=jax.ShapeDtypeStruct((3, H, C), parts.dtype),
        scratch_shapes=[pltpu.SemaphoreType.DMA((3,)), pltpu.SemaphoreType.DMA((3,))],
    )(parts)


def _swap_reduced(half_rows, name):
    H, C = half_rows.shape

    def body(src, out, send_sems, recv_sems):
        x, y, c, _ = _position()
        cp = _remote(src, out, send_sems, recv_sems, 0, (x, y, 1 - c))
        cp.start()
        cp.wait()

    return pl.pallas_call(
        body, name=name,
        in_specs=[ANY], out_specs=ANY,
        out_shape=jax.ShapeDtypeStruct((H, C), half_rows.dtype),
        scratch_shapes=[pltpu.SemaphoreType.DMA((1,)), pltpu.SemaphoreType.DMA((1,))],
    )(half_rows)


def _allreduce_small(block):
    R, C = block.shape
    n_dev = 8

    def body(src, out, slots, send_sems, recv_sems):
        x, y, c, _ = _position()
        me = 4 * x + 2 * y + c
        slots[me] = src[...]
        sends = []
        for r in range(1, n_dev):
            peer = (x ^ (r >> 2), y ^ ((r >> 1) & 1), c ^ (r & 1))
            cp = _remote(src, slots.at[me], send_sems, recv_sems, r - 1, peer)
            cp.start()
            sends.append(cp)
        for r in range(1, n_dev):
            theirs = slots.at[me ^ r]
            _remote(theirs, theirs, send_sems, recv_sems, r - 1, (x, y, c)).wait_recv()
        for cp in sends:
            cp.wait_send()
        acc = slots[0]
        for d in range(1, n_dev):
            acc = acc + slots[d]
        out[...] = acc

    return pl.pallas_call(
        body, name="allreduce_small",
        in_specs=[pl.BlockSpec(memory_space=pltpu.VMEM)], out_specs=pl.BlockSpec(memory_space=pltpu.VMEM),
        out_shape=jax.ShapeDtypeStruct((R, C), F32),
        scratch_shapes=[pltpu.VMEM((n_dev, R, C), F32), pltpu.SemaphoreType.DMA((7,)), pltpu.SemaphoreType.DMA((7,))],
    )(block)


def _rel_bucket(dist):
    max_exact = REL_BUCKETS // 2
    d = jnp.maximum(dist, 1).astype(F32)
    large = max_exact + (jnp.log(d / max_exact) / math.log(REL_MAX_DIST / max_exact)
                         * (REL_BUCKETS - max_exact)).astype(jnp.int32)
    large = jnp.minimum(large, REL_BUCKETS - 1)
    return jnp.where(dist < max_exact, dist, large)


def _bucket_onehot():
    T = SWA_BLOCK
    dist = (jnp.arange(T)[None, :] + T) - jnp.arange(2 * T)[:, None]
    bucket = _rel_bucket(jnp.maximum(dist, 0)).reshape(1, T * 2 * T)
    return (bucket == jnp.arange(REL_BUCKETS)[:, None]).astype(BF16)


_BUF = (("ffn1_w1", "t"), ("ffn1_w3", "t"), ("ffn1_w2", "n"), ("ffn2_w1", "t"), ("ffn2_w3", "t"),
        ("ffn2_w2", "n"), ("w_in", "t"), ("w_out", "n"), ("w_branch_swa", "tw"), ("w_branch_sb", "tw"))


def _to_rows(name_kind, w, D):
    kind = name_kind[1]
    if kind == "n":
        return w
    if kind == "t":
        return w.T
    return w.T.reshape(-1, D)


def _from_rows(name_kind, rows, width):
    kind = name_kind[1]
    if kind == "n":
        return rows
    if kind == "t":
        return rows.T
    return rows.reshape(-1, width).T


def kernel(x, norm_ffn1, ffn1_w1, ffn1_w3, ffn1_w2, norm_mix, w_in, swa_sinks, rel_bias, w_branch_swa, w_branch_sb, w_out, norm_ffn2, ffn2_w1, ffn2_w3, ffn2_w2, norm_final, loss_target, m_norm_ffn1, m_ffn1_w1, m_ffn1_w3, m_ffn1_w2, m_norm_mix, m_w_in, m_swa_sinks, m_rel_bias, m_w_branch_swa, m_w_branch_sb, m_w_out, m_norm_ffn2, m_ffn2_w1, m_ffn2_w3, m_ffn2_w2, m_norm_final, v_norm_ffn1, v_ffn1_w1, v_ffn1_w3, v_ffn1_w2, v_norm_mix, v_w_in, v_swa_sinks, v_rel_bias, v_w_branch_swa, v_w_branch_sb, v_w_out, v_norm_ffn2, v_ffn2_w1, v_ffn2_w3, v_ffn2_w2, v_norm_final):
    names = ["norm_ffn1", "ffn1_w1", "ffn1_w3", "ffn1_w2", "norm_mix", "w_in", "swa_sinks", "rel_bias",
             "w_branch_swa", "w_branch_sb", "w_out", "norm_ffn2", "ffn2_w1", "ffn2_w3", "ffn2_w2", "norm_final"]
    W = dict(zip(names, [norm_ffn1, ffn1_w1, ffn1_w3, ffn1_w2, norm_mix, w_in, swa_sinks, rel_bias,
                         w_branch_swa, w_branch_sb, w_out, norm_ffn2, ffn2_w1, ffn2_w3, ffn2_w2, norm_final]))
    M = dict(zip(names, [m_norm_ffn1, m_ffn1_w1, m_ffn1_w3, m_ffn1_w2, m_norm_mix, m_w_in, m_swa_sinks, m_rel_bias,
                         m_w_branch_swa, m_w_branch_sb, m_w_out, m_norm_ffn2, m_ffn2_w1, m_ffn2_w3, m_ffn2_w2,
                         m_norm_final]))
    V = dict(zip(names, [v_norm_ffn1, v_ffn1_w1, v_ffn1_w3, v_ffn1_w2, v_norm_mix, v_w_in, v_swa_sinks, v_rel_bias,
                         v_w_branch_swa, v_w_branch_sb, v_w_out, v_norm_ffn2, v_ffn2_w1, v_ffn2_w3, v_ffn2_w2,
                         v_norm_final]))
    xs = x[0]
    target = loss_target[0]
    S, D = xs.shape
    QW = SWA_Q_HEADS * HEAD_DIM
    KW = SWA_KV_HEADS * HEAD_DIM
    BW = SB_HEADS * HEAD_DIM
    QKV = QW + 2 * KW + 3 * BW

    pieces = [_to_rows(nk, W[nk[0]][0], D) for nk in _BUF]
    sizes = [p.shape[0] for p in pieces]
    offs = [0]
    for s in sizes:
        offs.append(offs[-1] + s)
    n_first = 3
    first_rows = offs[n_first]
    shard_a = jnp.concatenate(pieces[:n_first], axis=0).astype(BF16)
    shard_b = jnp.concatenate(pieces[n_first:], axis=0).astype(BF16)
    chip = 2 * lax.axis_index("x") + lax.axis_index("y")
    gathered_a = lax.dynamic_update_slice(_gather_weights(shard_a), shard_a[None], (chip, 0, 0))
    f1w1, f1w3, f1w2 = [gathered_a[:, offs[i]:offs[i + 1], :].reshape(N_CHIPS * sizes[i], D) for i in range(n_first)]

    g1, gmix, g3 = W["norm_ffn1"], W["norm_mix"], W["norm_ffn2"]
    gf = W["norm_final"].reshape(1, D)

    x1, h1, a1, b1, gathered_b = _ffn_fwd(xs, g1, f1w1, f1w3, f1w2, "ffn1_fwd", gather=shard_b)
    gathered_b = lax.dynamic_update_slice(gathered_b, shard_b[None], (chip, 0, 0))

    def full(i):
        return gathered_b[:, offs[i] - first_rows:offs[i + 1] - first_rows, :].reshape(N_CHIPS * sizes[i], D)

    f2w1, f2w3, f2w2, w_in_t, w_out_f = [full(i) for i in range(n_first, 8)]
    wa_t = full(8).reshape(D, QW)
    wb_t = full(9).reshape(D, BW)
    o0 = QW + 2 * KW
    rows = jnp.arange(w_in_t.shape[0])
    is_q = (rows < QW) | ((rows >= o0) & (rows < o0 + BW))
    w_in_s = w_in_t * jnp.where(is_q, QK_SCALE, 1.0).astype(BF16)[:, None]
    qkv_a = _norm_proj_heads(x1, gmix, w_in_s[:o0], SWA_BLOCK, "proj_swa")
    qkv_b = _norm_proj_heads(x1, gmix, w_in_s[o0:QKV], SB_BLOCK, "proj_sb")
    gates, h2 = _norm_matmul_nt(x1, gmix, w_in_t[QKV:], F32, "proj_gates")

    onehot = _bucket_onehot()
    bias = _bias_expand(W["rel_bias"].T, onehot).reshape(SWA_Q_HEADS, 2 * SWA_BLOCK, SWA_BLOCK)
    sinks = W["swa_sinks"].reshape(SWA_Q_HEADS)
    oa_t = _swa_fwd(qkv_a, bias, sinks)
    ob_t, saved_sb = _sb_fwd(qkv_b)

    x2, merged, ba, bb = _merge_fwd(x1, gates, oa_t, ob_t, wa_t, wb_t, w_out_f)
    x3, h3, a2, b2 = _ffn_fwd(x2, g3, f2w1, f2w3, f2w2, "ffn2_fwd")
    loss_part, dx3, dgf = _final_loss(x3, gf, target)

    dx2, dg3, dz2, da2, db2, u2 = _ffn_bwd(dx3, x2, g3, a2, b2, f2w1, f2w3, f2w2, "ffn2_bwd")
    grads = {}
    grads["ffn2_w1"] = _tn_matmul(da2, h3, "ffn2_dw1")
    grads["ffn2_w3"] = _tn_matmul(db2, h3, "ffn2_dw3")
    grads["ffn2_w2"] = _tn_matmul(u2, dz2, "ffn2_dw2")

    dx2b, dba, dbb, dgates, doa_t, dob_t = _merge_bwd(dx2, gates, ba, bb, wa_t.T, wb_t.T, w_out_f,
                                                      SWA_BLOCK, SB_BLOCK)
    grads["w_out"] = _tn_matmul(merged, dx2b, "dw_out")
    grads["w_branch_swa"] = _heads_matmul([(oa_t, 1.0)], dba, "dw_branch_swa").T
    grads["w_branch_sb"] = _heads_matmul([(ob_t, 1.0)], dbb, "dw_branch_sb").T

    dqb_t, dkb_t, dvb_t = _sb_bwd(qkv_b, dob_t, saved_sb)
    dqa_t, dka_t, dva_t, dbias, dsink_rows = _swa_bwd(qkv_a, bias, sinks, doa_t, oa_t)
    d_rel = _bias_reduce(dbias.reshape(SWA_Q_HEADS, -1), onehot).T
    d_sinks = jnp.sum(dsink_rows, axis=(1, 2))

    dheads = [(dqa_t, QK_SCALE), (dka_t, 1.0), (dva_t, 1.0), (dqb_t, QK_SCALE), (dkb_t, 1.0), (dvb_t, 1.0)]
    grads["w_in"] = jnp.concatenate([_heads_matmul(dheads, h2, "dw_in_heads").astype(BF16),
                                     _tn_matmul(dgates, h2, "dw_in_gates")], axis=0)
    row0, pieces_in = 0, []
    for a, _ in dheads:
        pieces_in.append((a, row0))
        row0 += a.shape[0] * HEAD_DIM
    dx1, dgmix = _proj_bwd(pieces_in, dgates, w_in_s, x1, gmix, dx2)

    c = lax.axis_index("c")

    def reduce_start(lo, hi, tag):
        gbuf = jnp.concatenate([grads[_BUF[i][0]].astype(BF16).reshape(N_CHIPS, sizes[i], D) for i in range(lo, hi)],
                               axis=1)
        half = gbuf.shape[1] // 2
        from_sibling = _swap_halves(gbuf, "swap_halves_" + tag)
        my_half = lax.dynamic_slice_in_dim(gbuf, c * half, half, axis=1)
        return _add_halves(my_half, from_sibling, "add_sibling_" + tag)

    def reduce_finish(chip_sum, received, tag):
        own = lax.dynamic_index_in_dim(chip_sum, chip, axis=0, keepdims=False)
        my_rows = _add_received(own, received, "add_chips_" + tag)
        their_rows = _swap_reduced(my_rows, "swap_reduced_" + tag)
        return jnp.concatenate([jnp.where(c == 0, my_rows, their_rows), jnp.where(c == 0, their_rows, my_rows)],
                               axis=0)

    sum_b, sum16_b = reduce_start(n_first, len(_BUF), "late")
    dx0, dg1, dz1, da1, db1, u1, received_b = _ffn_bwd(dx1, xs, g1, a1, b1, f1w1, f1w3, f1w2, "ffn1_bwd",
                                                       scatter=sum16_b)
    grads["ffn1_w1"] = _tn_matmul(da1, h1, "ffn1_dw1")
    grads["ffn1_w3"] = _tn_matmul(db1, h1, "ffn1_dw3")
    grads["ffn1_w2"] = _tn_matmul(u1, dz1, "ffn1_dw2")
    sum_a, sum16_a = reduce_start(0, n_first, "first")
    reduced = jnp.concatenate([reduce_finish(sum_a, _scatter_to_owners(sum16_a, "scatter_to_owners"), "first"),
                               reduce_finish(sum_b, received_b, "late")], axis=0)

    small_rows = [dg1, dgmix, dg3, dgf,
                  jnp.pad(d_sinks.reshape(1, -1), ((0, 0), (0, D - SWA_Q_HEADS))),
                  jnp.pad(d_rel.reshape(1, -1), ((0, 0), (0, D - REL_BUCKETS * SWA_Q_HEADS))),
                  jnp.pad(loss_part, ((0, 0), (0, D - 1))), jnp.zeros((1, D), F32)]
    small = _allreduce_small(jnp.concatenate(small_rows, axis=0))
    loss = small[6, 0]

    G, g_rows = {}, {}
    for i, nk in enumerate(_BUF):
        rows = reduced[offs[i]:offs[i + 1]]
        g_rows[nk[0]] = rows.reshape(-1, W[nk[0]].shape[1]) if nk[1] == "tw" else rows
        G[nk[0]] = _from_rows(nk, rows, W[nk[0]].shape[1])[None]
    G["norm_ffn1"], G["norm_mix"], G["norm_ffn2"] = small[0:1], small[1:2], small[2:3]
    G["norm_final"] = small[3]
    G["swa_sinks"] = small[4:5, :SWA_Q_HEADS]
    G["rel_bias"] = small[5, :REL_BUCKETS * SWA_Q_HEADS].reshape(REL_BUCKETS, SWA_Q_HEADS)

    delta, new_m, new_v = {}, {}, {}
    small_names = ["norm_ffn1", "norm_mix", "norm_ffn2", "norm_final", "swa_sinks", "rel_bias"]

    def pack(d):
        return jnp.concatenate([jnp.pad(d[n].reshape(1, -1), ((0, 0), (0, D - d[n].size))) for n in small_names]
                               + [jnp.zeros((2, D), F32)], axis=0)

    sd, sm, sv = _adamw(pack(W), pack(G), pack(M), pack(V), "adamw_small")
    for r, n in enumerate(small_names):
        for dst, src in ((delta, sd), (new_m, sm), (new_v, sv)):
            dst[n] = src[r, :W[n].size].reshape(W[n].shape)
    for n, kind in _BUF:
        turn = (lambda a: a) if kind == "n" else (lambda a: a.T)
        d_, m_, v_ = _adamw(turn(W[n][0]), g_rows[n], turn(M[n][0]), turn(V[n][0]), "adamw_" + n)
        delta[n], new_m[n], new_v[n] = turn(d_)[None], turn(m_)[None], turn(v_)[None]

    return (loss, dx0[None], *[G[n] for n in names], *[delta[n] for n in names],
            *[new_m[n] for n in names], *[new_v[n] for n in names])
```

```python
import math

import jax
import jax.numpy as jnp
from jax import lax
from jax.experimental import pallas as pl
from jax.experimental.pallas import tpu as pltpu

F32, BF16 = jnp.float32, jnp.bfloat16
MESH_ID = pl.DeviceIdType.MESH
ANY = pl.BlockSpec(memory_space=pl.ANY)

RMS_EPS = 1e-6
HEAD_DIM = 64
SWA_Q_HEADS, SWA_KV_HEADS, SWA_GROUP = 8, 2, 4
SWA_BLOCK = 128
SB_HEADS = 8
SB_BLOCK = 256
REL_BUCKETS, REL_MAX_DIST = 32, 128
NEG_BIG = -1e30
QK_SCALE = HEAD_DIM ** -0.5
ADAM_LR, ADAM_B1, ADAM_B2, ADAM_EPS, ADAM_WD, ADAM_STEP = 0.001, 0.9, 0.999, 1e-08, 0.01, 10

N_CHIPS = 4
TOKEN_TILE = 512
MATMUL_TOKEN_TILE = 1024
WGRAD_ROW_TILES = (2176, 1408, 1024, 256)
FF_TILE = 2816
FFN_TOKEN_TILE = 512
PROJ_HEAD_ROWS = 768
FF_BWD_TILE = 1408
VMEM_LIMIT = 56 * 1024 * 1024


def _cp(*sem):
    return pltpu.CompilerParams(dimension_semantics=sem, vmem_limit_bytes=VMEM_LIMIT)


def _nn(a, b):
    return jnp.dot(a, b, preferred_element_type=F32)


def _nt(a, b):
    return lax.dot_general(a, b, (((1,), (1,)), ((), ())), preferred_element_type=F32)


def _tn(a, b):
    return lax.dot_general(a, b, (((0,), (0,)), ((), ())), preferred_element_type=F32)


def _norm_fwd(x, g):
    return x * lax.rsqrt(jnp.mean(x * x, axis=-1, keepdims=True) + RMS_EPS) * g


def _norm_bwd(x, g, dh):
    r = lax.rsqrt(jnp.mean(x * x, axis=-1, keepdims=True) + RMS_EPS)
    xh = x * r
    dxh = dh * g
    dx = r * (dxh - xh * jnp.mean(dxh * xh, axis=-1, keepdims=True))
    return dx, jnp.sum(dh * xh, axis=0, keepdims=True)


SOFTPLUS_LINEAR = 20.0


def _softplus(z):
    return jnp.maximum(jnp.log(1.0 + jnp.exp(jnp.minimum(z, SOFTPLUS_LINEAR))), z)


def _ffn_fwd(x, g, w1t, w3t, w2, name, gather=None):
    S, D = x.shape
    F = w2.shape[0]
    tm, tf = min(FFN_TOKEN_TILE, S), FF_TILE
    ni, nj = S // tm, F // tf

    def body(x_ref, g_ref, w1_ref, w3_ref, w2_ref, *rest):
        if gather is None:
            xo_ref, h_ref, a_ref, b_ref, hs, acc = rest
        else:
            shard_ref, xo_ref, h_ref, a_ref, b_ref, gathered_ref, hs, acc, send_sems, recv_sems = rest
        i, j = pl.program_id(0), pl.program_id(1)
        if gather is not None:
            for when, phase in ((jnp.logical_and(i == 0, j == 0), "start"),
                                (jnp.logical_and(i == ni - 1, j == 0), "forward"),
                                (jnp.logical_and(i == ni - 1, j == nj - 1), "finish")):
                @pl.when(when)
                def _():
                    getattr(_gather_exchange(shard_ref, gathered_ref, send_sems, recv_sems), phase)()

        @pl.when(j == 0)
        def _():
            hb = _norm_fwd(x_ref[...], g_ref[...]).astype(BF16)
            hs[...] = hb
            h_ref[...] = hb
            acc[...] = jnp.zeros_like(acc)

        h = hs[...]
        a = _nt(h, w1_ref[...])
        b = _nt(h, w3_ref[...])
        a_ref[...] = a.astype(BF16)
        b_ref[...] = b.astype(BF16)
        u = a * jax.nn.sigmoid(a) * b
        acc[...] += _nn(u.astype(BF16), w2_ref[...])

        @pl.when(j == nj - 1)
        def _():
            xo_ref[...] = x_ref[...] + 0.5 * acc[...]

    in_specs = [pl.BlockSpec((tm, D), lambda i, j: (i, 0)),
                pl.BlockSpec((1, D), lambda i, j: (0, 0)),
                pl.BlockSpec((tf, D), lambda i, j: (j, 0), pipeline_mode=pl.Buffered(1)),
                pl.BlockSpec((tf, D), lambda i, j: (j, 0), pipeline_mode=pl.Buffered(1)),
                pl.BlockSpec((tf, D), lambda i, j: (j, 0), pipeline_mode=pl.Buffered(1))]
    out_specs = [pl.BlockSpec((tm, D), lambda i, j: (i, 0)),
                 pl.BlockSpec((tm, D), lambda i, j: (i, 0)),
                 pl.BlockSpec((tm, tf), lambda i, j: (i, j)),
                 pl.BlockSpec((tm, tf), lambda i, j: (i, j))]
    out_shape = [jax.ShapeDtypeStruct((S, D), F32), jax.ShapeDtypeStruct((S, D), BF16),
                 jax.ShapeDtypeStruct((S, F), BF16), jax.ShapeDtypeStruct((S, F), BF16)]
    scratch = [pltpu.VMEM((tm, D), BF16), pltpu.VMEM((tm, D), F32)]
    operands = [x, g, w1t, w3t, w2]
    if gather is not None:
        R, C = gather.shape
        in_specs.append(ANY)
        out_specs.append(ANY)
        out_shape.append(jax.ShapeDtypeStruct((N_CHIPS, 2, R // 2, C), gather.dtype))
        scratch += [pltpu.SemaphoreType.DMA((6,)), pltpu.SemaphoreType.DMA((6,))]
        operands.append(gather.reshape(2, R // 2, C))
    outs = list(pl.pallas_call(
        body, name=name, grid=(ni, nj), in_specs=in_specs, out_specs=out_specs, out_shape=out_shape,
        scratch_shapes=scratch, compiler_params=_cp("arbitrary", "arbitrary"),
    )(*operands))
    if gather is not None:
        outs[4] = outs[4].reshape(N_CHIPS, R, C)
    return outs


def _ffn_bwd(dxo, x, g, a, b, w1t, w3t, w2, name, scatter=None):
    S, D = x.shape
    F = w2.shape[0]
    tm, tf = min(FFN_TOKEN_TILE, S), FF_BWD_TILE
    ni, nj = S // tm, F // tf

    def body(dxo_ref, x_ref, g_ref, a_ref, b_ref, w1_ref, w3_ref, w2_ref, *rest):
        if scatter is None:
            dx_ref, dg_ref, dz_ref, da_ref, db_ref, u_ref, dzs, acc = rest
        else:
            (parts_ref, dx_ref, dg_ref, dz_ref, da_ref, db_ref, u_ref, recv_ref,
             dzs, acc, send_sems, recv_sems) = rest
        i, j = pl.program_id(0), pl.program_id(1)
        if scatter is not None:
            for when, phase in ((jnp.logical_and(i == 0, j == 0), "start"),
                                (jnp.logical_and(i == ni - 1, j == nj - 1), "finish")):
                @pl.when(when)
                def _():
                    getattr(_scatter_exchange(parts_ref, recv_ref, send_sems, recv_sems), phase)()

        @pl.when(j == 0)
        def _():
            dzb = (0.5 * dxo_ref[...]).astype(BF16)
            dzs[...] = dzb
            dz_ref[...] = dzb
            acc[...] = jnp.zeros_like(acc)

        du = _nt(dzs[...], w2_ref[...])
        av = a_ref[...].astype(F32)
        bv = b_ref[...].astype(F32)
        s = jax.nn.sigmoid(av)
        silu = av * s
        db = (du * silu).astype(BF16)
        da = (du * bv * (s * (1.0 + av * (1.0 - s)))).astype(BF16)
        da_ref[...] = da
        db_ref[...] = db
        u_ref[...] = (silu * bv).astype(BF16)
        acc[...] += _nn(da, w1_ref[...]) + _nn(db, w3_ref[...])

        @pl.when(j == nj - 1)
        def _():
            dx, dg = _norm_bwd(x_ref[...], g_ref[...], acc[...])
            dx_ref[...] = dxo_ref[...] + dx

            @pl.when(i == 0)
            def _():
                dg_ref[...] = dg

            @pl.when(i > 0)
            def _():
                dg_ref[...] += dg

    row = pl.BlockSpec((tm, D), lambda i, j: (i, 0))
    wsp = pl.BlockSpec((tf, D), lambda i, j: (j, 0))
    col = pl.BlockSpec((tm, tf), lambda i, j: (i, j))
    vec = pl.BlockSpec((1, D), lambda i, j: (0, 0))
    col_out = pl.BlockSpec((tm, tf), lambda i, j: (i, j), pipeline_mode=pl.Buffered(1))
    in_specs = [row, row, vec, col, col, wsp, wsp, wsp]
    out_specs = [row, vec, row, col_out, col_out, col_out]
    out_shape = [jax.ShapeDtypeStruct((S, D), F32), jax.ShapeDtypeStruct((1, D), F32),
                 jax.ShapeDtypeStruct((S, D), BF16), jax.ShapeDtypeStruct((S, F), BF16),
                 jax.ShapeDtypeStruct((S, F), BF16), jax.ShapeDtypeStruct((S, F), BF16)]
    scratch = [pltpu.VMEM((tm, D), BF16), pltpu.VMEM((tm, D), F32)]
    operands = [dxo, x, g, a, b, w1t, w3t, w2]
    if scatter is not None:
        in_specs.append(ANY)
        out_specs.append(ANY)
        out_shape.append(jax.ShapeDtypeStruct((3,) + scatter.shape[1:], scatter.dtype))
        scratch += [pltpu.SemaphoreType.DMA((3,)), pltpu.SemaphoreType.DMA((3,))]
        operands.append(scatter)
    return pl.pallas_call(
        body, name=name, grid=(ni, nj), in_specs=in_specs, out_specs=out_specs, out_shape=out_shape,
        scratch_shapes=scratch, compiler_params=_cp("arbitrary", "arbitrary"),
    )(*operands)


def _tn_matmul(a, b, name):
    S, M = a.shape
    N = b.shape[1]
    ts = min(MATMUL_TOKEN_TILE, S)
    tmm = next(t for t in WGRAD_ROW_TILES if M % t == 0)
    ns = S // ts

    def body(a_ref, b_ref, o_ref, acc):
        s = pl.program_id(1)
        part = _tn(a_ref[...], b_ref[...])

        @pl.when(s == 0)
        def _():
            acc[...] = part

        @pl.when(s > 0)
        def _():
            acc[...] += part

        @pl.when(s == ns - 1)
        def _():
            o_ref[...] = acc[...].astype(BF16)

    return pl.pallas_call(
        body, name=name, grid=(M // tmm, ns),
        in_specs=[pl.BlockSpec((ts, tmm), lambda m, s: (s, m)),
                  pl.BlockSpec((ts, N), lambda m, s: (s, 0))],
        out_specs=pl.BlockSpec((tmm, N), lambda m, s: (m, 0)),
        out_shape=jax.ShapeDtypeStruct((M, N), BF16),
        scratch_shapes=[pltpu.VMEM((tmm, N), F32)],
        compiler_params=_cp("arbitrary", "arbitrary"),
    )(a, b)


def _norm_matmul_nt(x, g, wt, out_dtype, name):
    S, D = x.shape
    N = wt.shape[0]
    tm = min(MATMUL_TOKEN_TILE, S)
    tn = next(t for t in (1024, 768, 256) if N % t == 0)

    def body(x_ref, g_ref, w_ref, o_ref, h_ref, hs):
        @pl.when(pl.program_id(1) == 0)
        def _():
            hb = _norm_fwd(x_ref[...], g_ref[...]).astype(BF16)
            hs[...] = hb
            h_ref[...] = hb

        o_ref[...] = _nt(hs[...], w_ref[...]).astype(out_dtype)

    return pl.pallas_call(
        body, name=name, grid=(S // tm, N // tn),
        in_specs=[pl.BlockSpec((tm, D), lambda i, j: (i, 0)),
                  pl.BlockSpec((1, D), lambda i, j: (0, 0)),
                  pl.BlockSpec((tn, D), lambda i, j: (j, 0))],
        out_specs=[pl.BlockSpec((tm, tn), lambda i, j: (i, j)),
                   pl.BlockSpec((tm, D), lambda i, j: (i, 0))],
        out_shape=[jax.ShapeDtypeStruct((S, N), out_dtype), jax.ShapeDtypeStruct((S, D), BF16)],
        scratch_shapes=[pltpu.VMEM((tm, D), BF16)],
        compiler_params=_cp("arbitrary", "arbitrary"),
    )(x, g, wt)


def _heads_tile(ref):
    Hh, nbk = ref.shape[0], ref.shape[1]
    return jnp.concatenate([jnp.concatenate([ref[h, b] for b in range(nbk)], axis=1) for h in range(Hh)], axis=0)


def _store_heads(ref, val):
    Hh, nbk, dh, T = ref.shape
    for h in range(Hh):
        for b in range(nbk):
            ref[h, b] = val[h * dh:(h + 1) * dh, b * T:(b + 1) * T].astype(ref.dtype)


def _norm_proj_heads(x, g, w_rows, T, name):
    S, D = x.shape
    N = w_rows.shape[0]
    tm, tn = min(MATMUL_TOKEN_TILE, S), PROJ_HEAD_ROWS

    def body(x_ref, g_ref, w_ref, o_ref, hs):
        @pl.when(pl.program_id(1) == 0)
        def _():
            hs[...] = _norm_fwd(x_ref[...], g_ref[...]).astype(BF16)

        _store_heads(o_ref, _nt(w_ref[...], hs[...]))

    return pl.pallas_call(
        body, name=name, grid=(S // tm, N // tn),
        in_specs=[pl.BlockSpec((tm, D), lambda i, j: (i, 0)),
                  pl.BlockSpec((1, D), lambda i, j: (0, 0)),
                  pl.BlockSpec((tn, D), lambda i, j: (j, 0))],
        out_specs=pl.BlockSpec((tn // HEAD_DIM, tm // T, HEAD_DIM, T), lambda i, j: (j, i, 0, 0)),
        out_shape=jax.ShapeDtypeStruct((N // HEAD_DIM, S // T, HEAD_DIM, T), BF16),
        scratch_shapes=[pltpu.VMEM((tm, D), BF16)],
        compiler_params=_cp("arbitrary", "arbitrary"),
    )(x, g, w_rows)


def _heads_matmul(pieces, b, name):
    S, N = b.shape
    ts = min(MATMUL_TOKEN_TILE, S)
    ns = S // ts
    rows = [at.shape[0] * at.shape[2] for at, _ in pieces]

    def body(*refs):
        a_refs, b_ref, o_ref = refs[:-2], refs[-2], refs[-1]
        s = pl.program_id(0)
        row0 = 0
        for a_ref, (_, scale), n in zip(a_refs, pieces, rows):
            a = _heads_tile(a_ref)
            part = _nn((a if scale == 1.0 else a * scale).astype(BF16), b_ref[...])
            out = o_ref.at[row0:row0 + n, :]
            row0 += n

            @pl.when(s == 0)
            def _():
                out[...] = part

            @pl.when(s > 0)
            def _():
                out[...] += part

    return pl.pallas_call(
        body, name=name, grid=(ns,),
        in_specs=[pl.BlockSpec((at.shape[0], ts // at.shape[3], at.shape[2], at.shape[3]), lambda s: (0, s, 0, 0))
                  for at, _ in pieces] + [pl.BlockSpec((ts, N), lambda s: (s, 0))],
        out_specs=pl.BlockSpec((sum(rows), N), lambda s: (0, 0)),
        out_shape=jax.ShapeDtypeStruct((sum(rows), N), F32),
        compiler_params=_cp("arbitrary"),
    )(*[at for at, _ in pieces], b)


def _proj_bwd(pieces, dgates, w_rows, x, g, dres):
    S, D = x.shape
    tm = min(TOKEN_TILE, S)
    n_p = len(pieces)
    gate_row = w_rows.shape[0] - dgates.shape[1]

    def body(*refs):
        p_refs = refs[:n_p]
        dgt_ref, w_ref, x_ref, g_ref, dres_ref, dx_ref, dg_ref = refs[n_p:]
        i = pl.program_id(0)
        dh = _nn(dgt_ref[...], w_ref[gate_row:, :])
        for p_ref, (arr, row0) in zip(p_refs, pieces):
            rows = arr.shape[0] * arr.shape[2]
            dh += _tn(_heads_tile(p_ref).astype(BF16), w_ref[row0:row0 + rows, :])
        dx, dg = _norm_bwd(x_ref[...], g_ref[...], dh)
        dx_ref[...] = dres_ref[...] + dx

        @pl.when(i == 0)
        def _():
            dg_ref[...] = dg

        @pl.when(i > 0)
        def _():
            dg_ref[...] += dg

    row = pl.BlockSpec((tm, D), lambda i: (i, 0))
    vec = pl.BlockSpec((1, D), lambda i: (0, 0))
    p_specs = [pl.BlockSpec((a.shape[0], tm // a.shape[3], a.shape[2], a.shape[3]), lambda i: (0, i, 0, 0))
               for a, _ in pieces]
    return pl.pallas_call(
        body, name="proj_bwd", grid=(S // tm,),
        in_specs=p_specs + [pl.BlockSpec((tm, dgates.shape[1]), lambda i: (i, 0)),
                            pl.BlockSpec(w_rows.shape, lambda i: (0, 0), pipeline_mode=pl.Buffered(1)),
                            row, vec, row],
        out_specs=[row, vec],
        out_shape=[jax.ShapeDtypeStruct((S, D), F32), jax.ShapeDtypeStruct((1, D), F32)],
        compiler_params=_cp("arbitrary"),
    )(*[a for a, _ in pieces], dgates, w_rows, x, g, dres)


def _merge_fwd(x1, gates, oa_t, ob_t, wat, wbt, w_out):
    S, D = x1.shape
    W = wat.shape[1]
    tm = min(TOKEN_TILE, S)

    def body(x_ref, ga_ref, gb_ref, oa_ref, ob_ref, wa_ref, wb_ref, wo_ref,
             x2_ref, mg_ref, ba_ref, bb_ref):
        ba = _nt(_heads_tile(oa_ref).T.astype(BF16), wa_ref[...])
        bb = _nt(_heads_tile(ob_ref).T.astype(BF16), wb_ref[...])
        merged = jax.nn.sigmoid(ga_ref[...]) * ba + jax.nn.sigmoid(gb_ref[...]) * bb
        mb = merged.astype(BF16)
        mg_ref[...] = mb
        ba_ref[...] = ba.astype(BF16)
        bb_ref[...] = bb.astype(BF16)
        x2_ref[...] = x_ref[...] + _nn(mb, wo_ref[...])

    row = pl.BlockSpec((tm, D), lambda i: (i, 0))
    full = lambda r, c: pl.BlockSpec((r, c), lambda i: (0, 0))
    heads = lambda a: pl.BlockSpec((a.shape[0], tm // a.shape[3], a.shape[2], a.shape[3]), lambda i: (0, i, 0, 0))
    return pl.pallas_call(
        body, name="merge_fwd", grid=(S // tm,),
        in_specs=[row, pl.BlockSpec((tm, D), lambda i: (i, 0)), pl.BlockSpec((tm, D), lambda i: (i, 1)),
                  heads(oa_t), heads(ob_t), full(D, W), full(D, W), full(D, D)],
        out_specs=[row, row, row, row],
        out_shape=[jax.ShapeDtypeStruct((S, D), F32)] + [jax.ShapeDtypeStruct((S, D), BF16)] * 3,
        compiler_params=_cp("arbitrary"),
    )(x1, gates, gates, oa_t, ob_t, wat, wbt, w_out)


def _merge_bwd(dx2, gates, ba, bb, wa, wb, w_out, t_a, t_b):
    S, D = dx2.shape
    W = wa.shape[0]
    tm = min(TOKEN_TILE, S)
    Hh = W // HEAD_DIM

    def body(dx_ref, ga_ref, gb_ref, ba_ref, bb_ref, wa_ref, wb_ref, wo_ref,
             dxb_ref, dba_ref, dbb_ref, dgt_ref, doa_ref, dob_ref):
        dxb = dx_ref[...].astype(BF16)
        dxb_ref[...] = dxb
        dm = _nt(dxb, wo_ref[...])
        sa = jax.nn.sigmoid(ga_ref[...])
        sb = jax.nn.sigmoid(gb_ref[...])
        dba = (dm * sa).astype(BF16)
        dbb = (dm * sb).astype(BF16)
        dba_ref[...] = dba
        dbb_ref[...] = dbb
        dgt_ref[:, :D] = (dm * ba_ref[...].astype(F32) * sa * (1.0 - sa)).astype(BF16)
        dgt_ref[:, D:] = (dm * bb_ref[...].astype(F32) * sb * (1.0 - sb)).astype(BF16)
        _store_heads(doa_ref, _nt(wa_ref[...], dba))
        _store_heads(dob_ref, _nt(wb_ref[...], dbb))

    row = pl.BlockSpec((tm, D), lambda i: (i, 0))
    full = lambda r, c: pl.BlockSpec((r, c), lambda i: (0, 0))
    heads = lambda T: pl.BlockSpec((Hh, tm // T, HEAD_DIM, T), lambda i: (0, i, 0, 0))
    return pl.pallas_call(
        body, name="merge_bwd", grid=(S // tm,),
        in_specs=[row, pl.BlockSpec((tm, D), lambda i: (i, 0)), pl.BlockSpec((tm, D), lambda i: (i, 1)),
                  row, row, full(W, D), full(W, D), full(D, D)],
        out_specs=[row, row, row, pl.BlockSpec((tm, 2 * D), lambda i: (i, 0)), heads(t_a), heads(t_b)],
        out_shape=[jax.ShapeDtypeStruct((S, D), BF16)] * 3 + [jax.ShapeDtypeStruct((S, 2 * D), BF16),
                   jax.ShapeDtypeStruct((Hh, S // t_a, HEAD_DIM, t_a), F32),
                   jax.ShapeDtypeStruct((Hh, S // t_b, HEAD_DIM, t_b), BF16)],
        compiler_params=_cp("arbitrary"),
    )(dx2, gates, gates, ba, bb, wa, wb, w_out)


def _final_loss(x3, gf, target):
    S, D = x3.shape
    tm = min(TOKEN_TILE, S)

    def body(x_ref, g_ref, t_ref, loss_ref, dx_ref, dg_ref):
        i = pl.program_id(0)
        x = x_ref[...]
        g = g_ref[...]
        e = _norm_fwd(x, g) - t_ref[...]
        part = 0.5 * jnp.sum(jnp.mean(e * e, axis=-1, keepdims=True), axis=0, keepdims=True)
        dx, dg = _norm_bwd(x, g, e * (1.0 / D))
        dx_ref[...] = dx

        @pl.when(i == 0)
        def _():
            loss_ref[...] = part
            dg_ref[...] = dg

        @pl.when(i > 0)
        def _():
            loss_ref[...] += part
            dg_ref[...] += dg

    row = pl.BlockSpec((tm, D), lambda i: (i, 0))
    vec = pl.BlockSpec((1, D), lambda i: (0, 0))
    return pl.pallas_call(
        body, name="final_loss", grid=(S // tm,),
        in_specs=[row, vec, row],
        out_specs=[pl.BlockSpec((1, 1), lambda i: (0, 0)), row, vec],
        out_shape=[jax.ShapeDtypeStruct((1, 1), F32), jax.ShapeDtypeStruct((S, D), F32),
                   jax.ShapeDtypeStruct((1, D), F32)],
        compiler_params=_cp("arbitrary"),
    )(x3, gf, target)


SB_FWD_HEAD_GROUP = 8
SB_HEAD_GROUP = 4
LANES = 128


def _tri(T, kind):
    r = lax.broadcasted_iota(jnp.int32, (T, T), 0)
    c = lax.broadcasted_iota(jnp.int32, (T, T), 1)
    return {"after": r > c, "before": r < c}[kind].astype(BF16)


def _lane(v, j):
    return jnp.broadcast_to(v[:, j:j + 1], (v.shape[0], LANES))


def _t_bf16(x):
    return x.astype(F32).T.astype(BF16)


def _wide(v, T):
    return jnp.tile(v, (1, T // LANES))


SB_SLOTS = 3
SB_FWD_SLOTS = 2
COPY_PARTS = 4


class _split_copy:
    def __init__(self, src, dst, sems):
        n = src.shape[0] // COPY_PARTS
        self.parts = [pltpu.make_async_copy(src.at[pl.ds(r * n, n)], dst.at[pl.ds(r * n, n)], sems.at[r])
                      for r in range(COPY_PARTS)]

    def start(self):
        for cp in self.parts:
            cp.start()

    def wait(self):
        for cp in self.parts:
            cp.wait()


def _sb_pair(i, kb):
    return (i * (i + 1)) // 2 + kb


def _sb_fwd(qkv):
    H3, nb, dh, T = qkv.shape
    H = H3 // 3
    HG = SB_FWD_HEAD_GROUP
    assert HG == H, "one head group: a saved tile holds all the heads"
    n_pairs = (nb * (nb + 1)) // 2

    def body(q_ref, k_ref, v_ref, o_ref, saved_ref, stage, sems):
        row = lax.broadcasted_iota(jnp.int32, (T, T), 0)
        col = lax.broadcasted_iota(jnp.int32, (T, T), 1)
        tri = col < row
        after = _tri(T, "after")

        def save(slot, pair):
            return _split_copy(stage.at[slot], saved_ref.at[pair], sems.at[slot])

        def blocks(qs, i, kb, step, carry, diag):
            hs = range(HG)
            slot = step % SB_FWD_SLOTS

            @pl.when(step >= SB_FWD_SLOTS)
            def _():
                save(slot, 0).wait()

            z = [_nn(qs[hh], k_ref[hh, kb]) for hh in hs]
            res, ls, first = [None] * HG, [None] * HG, [None] * HG
            for hh in hs:
                sp = _softplus(z[hh])
                if diag:
                    sp = jnp.where(tri, sp, 0.0)
                ls[hh] = z[hh] - sp
                spb = sp.astype(BF16)
                first[hh] = _lane(spb.astype(F32), 0)
                res[hh] = _nn(spb, after)
            out = []
            for hh in hs:
                c, oacc = carry[2 * hh], carry[2 * hh + 1]
                a = jnp.exp(ls[hh] - (res[hh] + _wide(c, T)))
                if diag:
                    a = jnp.where(tri, a, 0.0)
                ab = a.astype(BF16)
                stage[slot, hh, 0] = ab
                stage[slot, hh, 1] = jnp.exp(ls[hh]).astype(BF16)
                out.extend([c + (first[hh] + _lane(res[hh], 0)), oacc + _nt(v_ref[hh, kb], ab)])
            save(slot, _sb_pair(i, kb)).start()
            return tuple(out)

        def qblock(i, step):
            qs = [_t_bf16(q_ref[hh, i]) for hh in range(HG)]
            carry = blocks(qs, i, i, step, (jnp.zeros((T, LANES), F32), jnp.zeros((dh, T), F32)) * HG, True)

            def kstep(t, carry):
                return blocks(qs, i, i - 1 - t, step + 1 + t, carry, False)

            carry = lax.fori_loop(0, i, kstep, carry)
            for hh in range(HG):
                o_ref[hh, i] = carry[2 * hh + 1]
            return step + 1 + i

        lax.fori_loop(0, nb, qblock, 0)
        for slot in range(min(SB_FWD_SLOTS, n_pairs)):
            save(slot, 0).wait()

    ht = lambda part: pl.BlockSpec((HG, nb, dh, T), lambda h: (part, 0, 0, 0), pipeline_mode=pl.Buffered(1))
    return pl.pallas_call(
        body, name="sb_fwd", grid=(1,),
        in_specs=[ht(0), ht(1), ht(2)],
        out_specs=[ht(0), ANY],
        out_shape=[jax.ShapeDtypeStruct((H, nb, dh, T), F32),
                   jax.ShapeDtypeStruct((n_pairs, H, 2, T, T), BF16)],
        scratch_shapes=[pltpu.VMEM((SB_FWD_SLOTS, HG, 2, T, T), BF16),
                        pltpu.SemaphoreType.DMA((SB_FWD_SLOTS, COPY_PARTS))],
        compiler_params=_cp("arbitrary"),
    )(qkv, qkv, qkv)


def _sb_bwd(qkv, dot, saved):
    H3, nb, dh, T = qkv.shape
    H = H3 // 3
    HG = SB_HEAD_GROUP
    n_pairs = (nb * (nb + 1)) // 2

    def body(qt_ref, k_ref, v_ref, dot_ref, saved_ref, dq_ref, dk_ref, dv_ref, stage, sems):
        head0 = pl.program_id(0) * HG
        row = lax.broadcasted_iota(jnp.int32, (T, T), 0)
        col = lax.broadcasted_iota(jnp.int32, (T, T), 1)
        tri = col < row
        before = _tri(T, "before")
        dk_ref[...] = jnp.zeros_like(dk_ref)
        dv_ref[...] = jnp.zeros_like(dv_ref)

        def fetch(slot, pair):
            return _split_copy(saved_ref.at[pair, pl.ds(head0, HG)], stage.at[slot], sems.at[slot])

        for ahead in range(min(SB_SLOTS - 1, n_pairs)):
            fetch(ahead, ahead).start()

        def blocks(qTs, dos, doTs, i, kb, carry, diag):
            hs = range(HG)
            pair = _sb_pair(i, kb)
            slot = pair % SB_SLOTS
            fetch(slot, pair).wait()
            nxt = pair + (SB_SLOTS - 1)

            @pl.when(nxt < n_pairs)
            def _():
                fetch(nxt % SB_SLOTS, nxt).start()

            kT = [k_ref[hh, kb] for hh in hs]
            da = [_nn(dos[hh], v_ref[hh, kb]) for hh in hs]
            g, gb, resg = [None] * HG, [None] * HG, [None] * HG
            for hh in hs:
                g[hh] = stage[slot, hh, 0].astype(F32) * da[hh]
                gb[hh] = g[hh].astype(BF16)
                resg[hh] = _nn(gb[hh], before)
            out = []
            for hh in hs:
                pre_g, dq = carry[2 * hh], carry[2 * hh + 1]
                dz = g[hh] - (g[hh] + (resg[hh] + _wide(pre_g, T))) * stage[slot, hh, 1].astype(F32)
                if diag:
                    dz = jnp.where(tri, dz, 0.0)
                dzb = dz.astype(BF16)
                dk_ref[hh, kb] += _nn(qTs[hh], dzb)
                dv_ref[hh, kb] += _nn(doTs[hh], stage[slot, hh, 0])
                out.extend([pre_g + (_lane(resg[hh], T - 1) + _lane(gb[hh].astype(F32), T - 1)),
                            dq + _nt(kT[hh], dzb)])
            return tuple(out)

        def qblock(i, _):
            qTs = [qt_ref[hh, i] for hh in range(HG)]
            doTs = [dot_ref[hh, i] for hh in range(HG)]
            dos = [_t_bf16(v) for v in doTs]
            carry = (jnp.zeros((T, LANES), F32), jnp.zeros((dh, T), F32)) * HG

            def kstep(kb, carry):
                return blocks(qTs, dos, doTs, i, kb, carry, False)

            carry = lax.fori_loop(0, i, kstep, carry)
            carry = blocks(qTs, dos, doTs, i, i, carry, True)
            for hh in range(HG):
                dq_ref[hh, i] = carry[2 * hh + 1]
            return 0

        lax.fori_loop(0, nb, qblock, 0)

    G = H // HG
    ht = lambda part: pl.BlockSpec((HG, nb, dh, T), lambda h: (h + part * G, 0, 0, 0),
                                   pipeline_mode=pl.Buffered(1))
    return pl.pallas_call(
        body, name="sb_bwd", grid=(G,),
        in_specs=[ht(0), ht(1), ht(2), ht(0), ANY],
        out_specs=[ht(0), ht(0), ht(0)],
        out_shape=[jax.ShapeDtypeStruct((H, nb, dh, T), F32)] * 3,
        scratch_shapes=[pltpu.VMEM((SB_SLOTS, HG, 2, T, T), BF16), pltpu.SemaphoreType.DMA((SB_SLOTS, COPY_PARTS))],
        compiler_params=_cp("arbitrary"),
    )(qkv, qkv, qkv, dot, saved)


def _swa_probs(zp, zc, bias, sink, first):
    T = zp.shape[0]
    key = lax.broadcasted_iota(jnp.int32, (T, T), 0)
    qry = lax.broadcasted_iota(jnp.int32, (T, T), 1)
    lp = jnp.where(jnp.logical_and(key > qry, jnp.logical_not(first)), zp + bias[:T, :], NEG_BIG)
    lc = jnp.where(key <= qry, zc + bias[T:, :], NEG_BIG)
    m = jnp.maximum(jnp.maximum(jnp.max(lp, axis=0, keepdims=True), jnp.max(lc, axis=0, keepdims=True)), sink)
    pp = jnp.exp(lp - m)
    pc = jnp.exp(lc - m)
    ps = jnp.exp(sink - m)
    inv = 1.0 / (jnp.sum(pp, axis=0, keepdims=True) + jnp.sum(pc, axis=0, keepdims=True) + ps)
    return pp * inv, pc * inv, ps * inv


def _swa_specs(nb, dh, T, Hq, Hkv, clamp):
    blk = (lambda n: jnp.minimum(n, nb - 1)) if clamp else (lambda n: n)
    q = pl.BlockSpec((Hq, None, dh, T), lambda n: (0, blk(n), 0, 0))
    kv = lambda first, back: pl.BlockSpec(
        (Hkv, None, dh, T), lambda n: (first // Hkv, jnp.maximum(blk(n) - back, 0) if back else blk(n), 0, 0))
    return q, [kv(Hq, 1), kv(Hq, 0), kv(Hq + Hkv, 1), kv(Hq + Hkv, 0)]


def _swa_fwd(qkv, bias, sinks):
    Hq, Hkv, grp = SWA_Q_HEADS, SWA_KV_HEADS, SWA_GROUP
    _, nb, dh, T = qkv.shape

    def body(sink_ref, q_ref, kp_ref, kc_ref, vp_ref, vc_ref, bias_ref, o_ref):
        n = pl.program_id(0)
        kpn = [_t_bf16(kp_ref[hk]) for hk in range(Hkv)]
        kcn = [_t_bf16(kc_ref[hk]) for hk in range(Hkv)]
        zs = [(_nn(kpn[h // grp], q_ref[h]), _nn(kcn[h // grp], q_ref[h])) for h in range(Hq)]
        for h in range(Hq):
            pp, pc, _ = _swa_probs(*zs[h], bias_ref[h], sink_ref[h], n == 0)
            o_ref[h] = _nn(vp_ref[h // grp], pp.astype(BF16)) + _nn(vc_ref[h // grp], pc.astype(BF16))

    q_spec, kv_specs = _swa_specs(nb, dh, T, Hq, Hkv, False)
    return pl.pallas_call(
        body, name="swa_fwd", grid=(nb,),
        in_specs=[pl.BlockSpec(memory_space=pltpu.SMEM), q_spec] + kv_specs
                 + [pl.BlockSpec((Hq, 2 * T, T), lambda n: (0, 0, 0))],
        out_specs=pl.BlockSpec((Hq, None, dh, T), lambda n: (0, n, 0, 0)),
        out_shape=jax.ShapeDtypeStruct((Hq, nb, dh, T), F32),
        compiler_params=_cp("arbitrary"),
    )(sinks, qkv, qkv, qkv, qkv, qkv, bias)


def _swa_bwd(qkv, bias, sinks, dot, ot):
    Hq, Hkv, grp = SWA_Q_HEADS, SWA_KV_HEADS, SWA_GROUP
    _, nb, dh, T = qkv.shape

    def body(sink_ref, qt_ref, kp_ref, kc_ref, vp_ref, vc_ref, bias_ref, dot_ref, ot_ref,
             dq_ref, dk_ref, dv_ref, dbias_ref, dsink_ref, ck, cv):
        n = pl.program_id(0)

        @pl.when(n == 0)
        def _():
            dbias_ref[...] = jnp.zeros_like(dbias_ref)
            dsink_ref[...] = jnp.zeros_like(dsink_ref)
            ck[...] = jnp.zeros_like(ck)
            cv[...] = jnp.zeros_like(cv)

        @pl.when(n < nb)
        def _():
            kp, kc = [kp_ref[hk] for hk in range(Hkv)], [kc_ref[hk] for hk in range(Hkv)]
            kpn, kcn = [_t_bf16(v) for v in kp], [_t_bf16(v) for v in kc]
            vpn = [_t_bf16(vp_ref[hk]) for hk in range(Hkv)]
            vcn = [_t_bf16(vc_ref[hk]) for hk in range(Hkv)]
            qTs = [qt_ref[h] for h in range(Hq)]
            doTs = [dot_ref[h].astype(BF16) for h in range(Hq)]
            zs = [(_nn(kpn[h // grp], qTs[h]), _nn(kcn[h // grp], qTs[h])) for h in range(Hq)]
            dps = [(_nn(vpn[h // grp], doTs[h]), _nn(vcn[h // grp], doTs[h])) for h in range(Hq)]
            dls, pbs = [], []
            for h in range(Hq):
                pp, pc, ps = _swa_probs(*zs[h], bias_ref[h], sink_ref[h], n == 0)
                delta = jnp.sum(dot_ref[h] * ot_ref[h], axis=0, keepdims=True)
                dlp = pp * (dps[h][0] - delta)
                dlc = pc * (dps[h][1] - delta)
                dbias_ref[h, :T, :] += dlp
                dbias_ref[h, T:, :] += dlc
                dsink_ref[h] += -ps * delta
                dls.append((dlp.astype(BF16), dlc.astype(BF16)))
                pbs.append((pp.astype(BF16), pc.astype(BF16)))
            zero = jnp.zeros((dh, T), F32)
            kprev, kcur, vprev, vcur = [zero] * Hkv, [zero] * Hkv, [zero] * Hkv, [zero] * Hkv
            for h in range(Hq):
                hk = h // grp
                dlpb, dlcb = dls[h]
                dq_ref[h] = _nn(kp[hk], dlpb) + _nn(kc[hk], dlcb)
                kprev[hk] = kprev[hk] + _nt(qTs[h], dlpb)
                kcur[hk] = kcur[hk] + _nt(qTs[h], dlcb)
                vprev[hk] = vprev[hk] + _nt(doTs[h], pbs[h][0])
                vcur[hk] = vcur[hk] + _nt(doTs[h], pbs[h][1])
            for hk in range(Hkv):
                dk_ref[hk] = ck[hk] + kprev[hk]
                dv_ref[hk] = cv[hk] + vprev[hk]
                ck[hk] = kcur[hk]
                cv[hk] = vcur[hk]

        @pl.when(n == nb)
        def _():
            dk_ref[...] = ck[...]
            dv_ref[...] = cv[...]

    qt_spec, kv_specs = _swa_specs(nb, dh, T, Hq, Hkv, True)
    prev = pl.BlockSpec((Hkv, None, dh, T), lambda n: (0, jnp.maximum(n - 1, 0), 0, 0))
    whole = lambda a, b: pl.BlockSpec((Hq, a, b), lambda n: (0, 0, 0))
    return pl.pallas_call(
        body, name="swa_bwd", grid=(nb + 1,),
        in_specs=[pl.BlockSpec(memory_space=pltpu.SMEM), qt_spec] + kv_specs
                 + [whole(2 * T, T), qt_spec, qt_spec],
        out_specs=[qt_spec, prev, prev, whole(2 * T, T), whole(1, T)],
        out_shape=[jax.ShapeDtypeStruct((Hq, nb, dh, T), F32), jax.ShapeDtypeStruct((Hkv, nb, dh, T), F32),
                   jax.ShapeDtypeStruct((Hkv, nb, dh, T), F32), jax.ShapeDtypeStruct((Hq, 2 * T, T), F32),
                   jax.ShapeDtypeStruct((Hq, 1, T), F32)],
        scratch_shapes=[pltpu.VMEM((Hkv, dh, T), F32), pltpu.VMEM((Hkv, dh, T), F32)],
        compiler_params=_cp("arbitrary"),
    )(sinks, qkv, qkv, qkv, qkv, qkv, bias, dot, ot)


def _split3(x):
    h1 = x.astype(BF16)
    r1 = x - h1.astype(F32)
    h2 = r1.astype(BF16)
    h3 = (r1 - h2.astype(F32)).astype(BF16)
    return h1, h2, h3


def _bias_expand(rel_t, onehot):
    Hq, NB = rel_t.shape
    L = onehot.shape[1]

    def body(r_ref, oh_ref, o_ref):
        h1, h2, h3 = _split3(r_ref[...])
        oh = oh_ref[...]
        o_ref[...] = _nn(h1, oh) + _nn(h2, oh) + _nn(h3, oh)

    return pl.pallas_call(
        body, name="bias_expand", grid=(1,),
        in_specs=[pl.BlockSpec((Hq, NB), lambda i: (0, 0)), pl.BlockSpec((NB, L), lambda i: (0, 0))],
        out_specs=pl.BlockSpec((Hq, L), lambda i: (0, 0)),
        out_shape=jax.ShapeDtypeStruct((Hq, L), F32),
        compiler_params=_cp("arbitrary"),
    )(rel_t, onehot)


def _bias_reduce(dbias, onehot):
    Hq, L = dbias.shape
    NB = onehot.shape[0]

    def body(d_ref, oh_ref, o_ref):
        h1, h2, h3 = _split3(d_ref[...])
        oh = oh_ref[...]
        o_ref[...] = _nt(h1, oh) + _nt(h2, oh) + _nt(h3, oh)

    return pl.pallas_call(
        body, name="bias_reduce", grid=(1,),
        in_specs=[pl.BlockSpec((Hq, L), lambda i: (0, 0)), pl.BlockSpec((NB, L), lambda i: (0, 0))],
        out_specs=pl.BlockSpec((Hq, NB), lambda i: (0, 0)),
        out_shape=jax.ShapeDtypeStruct((Hq, NB), F32),
        compiler_params=_cp("arbitrary"),
    )(dbias, onehot)


def _adamw(w, g, m, v, name):
    R, C = w.shape
    tr = 256 if R % 256 == 0 else R
    bc1 = 1.0 - ADAM_B1 ** ADAM_STEP
    bc2 = 1.0 - ADAM_B2 ** ADAM_STEP

    def body(w_ref, g_ref, m_ref, v_ref, d_ref, nm_ref, nv_ref):
        g = g_ref[...]
        m2 = ADAM_B1 * m_ref[...] + (1.0 - ADAM_B1) * g
        v2 = ADAM_B2 * v_ref[...] + (1.0 - ADAM_B2) * (g * g)
        nm_ref[...] = m2
        nv_ref[...] = v2
        d_ref[...] = -ADAM_LR * ((m2 / bc1) / (jnp.sqrt(v2 / bc2) + ADAM_EPS) + ADAM_WD * w_ref[...])

    spec = pl.BlockSpec((tr, C), lambda i: (i, 0))
    return pl.pallas_call(
        body, name=name, grid=(R // tr,),
        in_specs=[spec] * 4, out_specs=[spec] * 3,
        out_shape=[jax.ShapeDtypeStruct((R, C), F32)] * 3,
        compiler_params=_cp("arbitrary"),
    )(w, g, m, v)


def _row_tile(R):
    return max(t for t in range(16, 513, 16) if R % t == 0)


def _add_halves(mine, recv, name):
    K, R, C = mine.shape
    tr = _row_tile(R)

    def body(a_ref, b_ref, o_ref, ob_ref):
        s = a_ref[...].astype(F32) + b_ref[...].astype(F32)
        o_ref[...] = s
        ob_ref[...] = s.astype(BF16)

    spec = pl.BlockSpec((None, tr, C), lambda k, i: (k, i, 0))
    return pl.pallas_call(
        body, name=name, grid=(K, R // tr),
        in_specs=[spec, spec], out_specs=[spec, spec],
        out_shape=[jax.ShapeDtypeStruct((K, R, C), F32), jax.ShapeDtypeStruct((K, R, C), BF16)],
        compiler_params=_cp("arbitrary", "arbitrary"),
    )(mine, recv)


def _add_received(own, recv, name):
    R, C = own.shape
    tr = _row_tile(R)

    def body(a_ref, r_ref, o_ref):
        o_ref[...] = ((a_ref[...] + r_ref[0].astype(F32)) + r_ref[1].astype(F32)) + r_ref[2].astype(F32)

    return pl.pallas_call(
        body, name=name, grid=(R // tr,),
        in_specs=[pl.BlockSpec((tr, C), lambda i: (i, 0)), pl.BlockSpec((3, tr, C), lambda i: (0, i, 0))],
        out_specs=pl.BlockSpec((tr, C), lambda i: (i, 0)),
        out_shape=jax.ShapeDtypeStruct((R, C), F32),
        compiler_params=_cp("arbitrary"),
    )(own, recv)


def _position():
    x, y, c = lax.axis_index("x"), lax.axis_index("y"), lax.axis_index("c")
    others = [(1 - x, y), (x, 1 - y), (1 - x, 1 - y)]
    return x, y, c, others


def _remote(src, dst, send_sems, recv_sems, k, dev):
    return pltpu.make_async_remote_copy(src_ref=src, dst_ref=dst, send_sem=send_sems.at[k],
                                        recv_sem=recv_sems.at[k], device_id=dev, device_id_type=MESH_ID)


class _gather_exchange:
    def __init__(self, src, out, send_sems, recv_sems):
        x, y, c, others = _position()
        mine, sibling = 2 * x + y, (x, y, 1 - c)
        self.sends, self.arrivals, self.passes, self.from_sibling = [], [], [], []
        for j, (ox, oy) in enumerate(others):
            slot = out.at[2 * ox + oy, c]
            theirs = out.at[2 * ox + oy, 1 - c]
            self.sends.append(_remote(src.at[c], out.at[mine, c], send_sems, recv_sems, j, (ox, oy, c)))
            self.arrivals.append(_remote(slot, slot, send_sems, recv_sems, j, (ox, oy, c)))
            self.passes.append(_remote(slot, slot, send_sems, recv_sems, 3 + j, sibling))
            self.from_sibling.append(_remote(theirs, theirs, send_sems, recv_sems, 3 + j, sibling))

    def start(self):
        for cp in self.sends:
            cp.start()

    def forward(self):
        for arrived, onward in zip(self.arrivals, self.passes):
            arrived.wait_recv()
            onward.start()

    def finish(self):
        for cp in self.from_sibling:
            cp.wait_recv()
        for cp in self.sends + self.passes:
            cp.wait_send()


def _gather_weights(shard):
    R, C = shard.shape
    half = R // 2

    def body(src, out, send_sems, recv_sems):
        ex = _gather_exchange(src, out, send_sems, recv_sems)
        ex.start()
        ex.forward()
        ex.finish()

    return pl.pallas_call(
        body, name="gather_weights",
        in_specs=[ANY], out_specs=ANY,
        out_shape=jax.ShapeDtypeStruct((N_CHIPS, 2, half, C), shard.dtype),
        scratch_shapes=[pltpu.SemaphoreType.DMA((6,)), pltpu.SemaphoreType.DMA((6,))],
    )(shard.reshape(2, half, C)).reshape(N_CHIPS, R, C)


def _swap_halves(grads, name):
    K, R, C = grads.shape
    half = R // 2

    def body(src, out, send_sems, recv_sems):
        x, y, c, _ = _position()
        theirs = src.at[:, pl.ds(pl.multiple_of((1 - c) * half, 16), half), :]
        cp = _remote(theirs, out, send_sems, recv_sems, 0, (x, y, 1 - c))
        cp.start()
        cp.wait()

    return pl.pallas_call(
        body, name=name,
        in_specs=[ANY], out_specs=ANY,
        out_shape=jax.ShapeDtypeStruct((K, half, C), grads.dtype),
        scratch_shapes=[pltpu.SemaphoreType.DMA((1,)), pltpu.SemaphoreType.DMA((1,))],
    )(grads)


class _scatter_exchange:
    def __init__(self, src, out, send_sems, recv_sems):
        x, y, c, others = _position()
        self.copies = [_remote(src.at[2 * ox + oy], out.at[j], send_sems, recv_sems, j, (ox, oy, c))
                       for j, (ox, oy) in enumerate(others)]

    def start(self):
        for cp in self.copies:
            cp.start()

    def finish(self):
        for cp in self.copies:
            cp.wait()


def _scatter_to_owners(parts, name):
    K, H, C = parts.shape

    def body(src, out, send_sems, recv_sems):
        ex = _scatter_exchange(src, out, send_sems, recv_sems)
        ex.start()
        ex.finish()

    return pl.pallas_call(
        body, name=name,
        in_specs=[ANY], out_specs=ANY,
        out_shape=jax.ShapeDtypeStruct((3, H, C), parts.dtype),
        scratch_shapes=[pltpu.SemaphoreType.DMA((3,)), pltpu.SemaphoreType.DMA((3,))],
    )(parts)


def _swap_reduced(half_rows, name):
    H, C = half_rows.shape

    def body(src, out, send_sems, recv_sems):
        x, y, c, _ = _position()
        cp = _remote(src, out, send_sems, recv_sems, 0, (x, y, 1 - c))
        cp.start()
        cp.wait()

    return pl.pallas_call(
        body, name=name,
        in_specs=[ANY], out_specs=ANY,
        out_shape=jax.ShapeDtypeStruct((H, C), half_rows.dtype),
        scratch_shapes=[pltpu.SemaphoreType.DMA((1,)), pltpu.SemaphoreType.DMA((1,))],
    )(half_rows)


def _allreduce_small(block):
    R, C = block.shape
    n_dev = 8

    def body(src, out, slots, send_sems, recv_sems):
        x, y, c, _ = _position()
        me = 4 * x + 2 * y + c
        slots[me] = src[...]
        sends = []
        for r in range(1, n_dev):
            peer = (x ^ (r >> 2), y ^ ((r >> 1) & 1), c ^ (r & 1))
            cp = _remote(src, slots.at[me], send_sems, recv_sems, r - 1, peer)
            cp.start()
            sends.append(cp)
        for r in range(1, n_dev):
            theirs = slots.at[me ^ r]
            _remote(theirs, theirs, send_sems, recv_sems, r - 1, (x, y, c)).wait_recv()
        for cp in sends:
            cp.wait_send()
        acc = slots[0]
        for d in range(1, n_dev):
            acc = acc + slots[d]
        out[...] = acc

    return pl.pallas_call(
        body, name="allreduce_small",
        in_specs=[pl.BlockSpec(memory_space=pltpu.VMEM)], out_specs=pl.BlockSpec(memory_space=pltpu.VMEM),
        out_shape=jax.ShapeDtypeStruct((R, C), F32),
        scratch_shapes=[pltpu.VMEM((n_dev, R, C), F32), pltpu.SemaphoreType.DMA((7,)), pltpu.SemaphoreType.DMA((7,))],
    )(block)


def _rel_bucket(dist):
    max_exact = REL_BUCKETS // 2
    d = jnp.maximum(dist, 1).astype(F32)
    large = max_exact + (jnp.log(d / max_exact) / math.log(REL_MAX_DIST / max_exact)
                         * (REL_BUCKETS - max_exact)).astype(jnp.int32)
    large = jnp.minimum(large, REL_BUCKETS - 1)
    return jnp.where(dist < max_exact, dist, large)


def _bucket_onehot():
    T = SWA_BLOCK
    dist = (jnp.arange(T)[None, :] + T) - jnp.arange(2 * T)[:, None]
    bucket = _rel_bucket(jnp.maximum(dist, 0)).reshape(1, T * 2 * T)
    return (bucket == jnp.arange(REL_BUCKETS)[:, None]).astype(BF16)


_BUF = (("ffn1_w1", "t"), ("ffn1_w3", "t"), ("ffn1_w2", "n"), ("ffn2_w1", "t"), ("ffn2_w3", "t"),
        ("ffn2_w2", "n"), ("w_in", "t"), ("w_out", "n"), ("w_branch_swa", "tw"), ("w_branch_sb", "tw"))


def _to_rows(name_kind, w, D):
    kind = name_kind[1]
    if kind == "n":
        return w
    if kind == "t":
        return w.T
    return w.T.reshape(-1, D)


def _from_rows(name_kind, rows, width):
    kind = name_kind[1]
    if kind == "n":
        return rows
    if kind == "t":
        return rows.T
    return rows.reshape(-1, width).T


def kernel(x, norm_ffn1, ffn1_w1, ffn1_w3, ffn1_w2, norm_mix, w_in, swa_sinks, rel_bias, w_branch_swa, w_branch_sb, w_out, norm_ffn2, ffn2_w1, ffn2_w3, ffn2_w2, norm_final, loss_target, m_norm_ffn1, m_ffn1_w1, m_ffn1_w3, m_ffn1_w2, m_norm_mix, m_w_in, m_swa_sinks, m_rel_bias, m_w_branch_swa, m_w_branch_sb, m_w_out, m_norm_ffn2, m_ffn2_w1, m_ffn2_w3, m_ffn2_w2, m_norm_final, v_norm_ffn1, v_ffn1_w1, v_ffn1_w3, v_ffn1_w2, v_norm_mix, v_w_in, v_swa_sinks, v_rel_bias, v_w_branch_swa, v_w_branch_sb, v_w_out, v_norm_ffn2, v_ffn2_w1, v_ffn2_w3, v_ffn2_w2, v_norm_final):
    names = ["norm_ffn1", "ffn1_w1", "ffn1_w3", "ffn1_w2", "norm_mix", "w_in", "swa_sinks", "rel_bias",
             "w_branch_swa", "w_branch_sb", "w_out", "norm_ffn2", "ffn2_w1", "ffn2_w3", "ffn2_w2", "norm_final"]
    W = dict(zip(names, [norm_ffn1, ffn1_w1, ffn1_w3, ffn1_w2, norm_mix, w_in, swa_sinks, rel_bias,
                         w_branch_swa, w_branch_sb, w_out, norm_ffn2, ffn2_w1, ffn2_w3, ffn2_w2, norm_final]))
    M = dict(zip(names, [m_norm_ffn1, m_ffn1_w1, m_ffn1_w3, m_ffn1_w2, m_norm_mix, m_w_in, m_swa_sinks, m_rel_bias,
                         m_w_branch_swa, m_w_branch_sb, m_w_out, m_norm_ffn2, m_ffn2_w1, m_ffn2_w3, m_ffn2_w2,
                         m_norm_final]))
    V = dict(zip(names, [v_norm_ffn1, v_ffn1_w1, v_ffn1_w3, v_ffn1_w2, v_norm_mix, v_w_in, v_swa_sinks, v_rel_bias,
                         v_w_branch_swa, v_w_branch_sb, v_w_out, v_norm_ffn2, v_ffn2_w1, v_ffn2_w3, v_ffn2_w2,
                         v_norm_final]))
    xs = x[0]
    target = loss_target[0]
    S, D = xs.shape
    QW = SWA_Q_HEADS * HEAD_DIM
    KW = SWA_KV_HEADS * HEAD_DIM
    BW = SB_HEADS * HEAD_DIM
    QKV = QW + 2 * KW + 3 * BW

    pieces = [_to_rows(nk, W[nk[0]][0], D) for nk in _BUF]
    sizes = [p.shape[0] for p in pieces]
    offs = [0]
    for s in sizes:
        offs.append(offs[-1] + s)
    n_first = 3
    first_rows = offs[n_first]
    shard_a = jnp.concatenate(pieces[:n_first], axis=0).astype(BF16)
    shard_b = jnp.concatenate(pieces[n_first:], axis=0).astype(BF16)
    chip = 2 * lax.axis_index("x") + lax.axis_index("y")
    gathered_a = lax.dynamic_update_slice(_gather_weights(shard_a), shard_a[None], (chip, 0, 0))
    f1w1, f1w3, f1w2 = [gathered_a[:, offs[i]:offs[i + 1], :].reshape(N_CHIPS * sizes[i], D) for i in range(n_first)]

    g1, gmix, g3 = W["norm_ffn1"], W["norm_mix"], W["norm_ffn2"]
    gf = W["norm_final"].reshape(1, D)

    x1, h1, a1, b1, gathered_b = _ffn_fwd(xs, g1, f1w1, f1w3, f1w2, "ffn1_fwd", gather=shard_b)
    gathered_b = lax.dynamic_update_slice(gathered_b, shard_b[None], (chip, 0, 0))

    def full(i):
        return gathered_b[:, offs[i] - first_rows:offs[i + 1] - first_rows, :].reshape(N_CHIPS * sizes[i], D)

    f2w1, f2w3, f2w2, w_in_t, w_out_f = [full(i) for i in range(n_first, 8)]
    wa_t = full(8).reshape(D, QW)
    wb_t = full(9).reshape(D, BW)
    o0 = QW + 2 * KW
    rows = jnp.arange(w_in_t.shape[0])
    is_q = (rows < QW) | ((rows >= o0) & (rows < o0 + BW))
    w_in_s = w_in_t * jnp.where(is_q, QK_SCALE, 1.0).astype(BF16)[:, None]
    qkv_a = _norm_proj_heads(x1, gmix, w_in_s[:o0], SWA_BLOCK, "proj_swa")
    qkv_b = _norm_proj_heads(x1, gmix, w_in_s[o0:QKV], SB_BLOCK, "proj_sb")
    gates, h2 = _norm_matmul_nt(x1, gmix, w_in_t[QKV:], F32, "proj_gates")

    onehot = _bucket_onehot()
    bias = _bias_expand(W["rel_bias"].T, onehot).reshape(SWA_Q_HEADS, 2 * SWA_BLOCK, SWA_BLOCK)
    sinks = W["swa_sinks"].reshape(SWA_Q_HEADS)
    oa_t = _swa_fwd(qkv_a, bias, sinks)
    ob_t, saved_sb = _sb_fwd(qkv_b)

    x2, merged, ba, bb = _merge_fwd(x1, gates, oa_t, ob_t, wa_t, wb_t, w_out_f)
    x3, h3, a2, b2 = _ffn_fwd(x2, g3, f2w1, f2w3, f2w2, "ffn2_fwd")
    loss_part, dx3, dgf = _final_loss(x3, gf, target)

    dx2, dg3, dz2, da2, db2, u2 = _ffn_bwd(dx3, x2, g3, a2, b2, f2w1, f2w3, f2w2, "ffn2_bwd")
    grads = {}
    grads["ffn2_w1"] = _tn_matmul(da2, h3, "ffn2_dw1")
    grads["ffn2_w3"] = _tn_matmul(db2, h3, "ffn2_dw3")
    grads["ffn2_w2"] = _tn_matmul(u2, dz2, "ffn2_dw2")

    dx2b, dba, dbb, dgates, doa_t, dob_t = _merge_bwd(dx2, gates, ba, bb, wa_t.T, wb_t.T, w_out_f,
                                                      SWA_BLOCK, SB_BLOCK)
    grads["w_out"] = _tn_matmul(merged, dx2b, "dw_out")
    grads["w_branch_swa"] = _heads_matmul([(oa_t, 1.0)], dba, "dw_branch_swa").T
    grads["w_branch_sb"] = _heads_matmul([(ob_t, 1.0)], dbb, "dw_branch_sb").T

    dqb_t, dkb_t, dvb_t = _sb_bwd(qkv_b, dob_t, saved_sb)
    dqa_t, dka_t, dva_t, dbias, dsink_rows = _swa_bwd(qkv_a, bias, sinks, doa_t, oa_t)
    d_rel = _bias_reduce(dbias.reshape(SWA_Q_HEADS, -1), onehot).T
    d_sinks = jnp.sum(dsink_rows, axis=(1, 2))

    dheads = [(dqa_t, QK_SCALE), (dka_t, 1.0), (dva_t, 1.0), (dqb_t, QK_SCALE), (dkb_t, 1.0), (dvb_t, 1.0)]
    grads["w_in"] = jnp.concatenate([_heads_matmul(dheads, h2, "dw_in_heads").astype(BF16),
                                     _tn_matmul(dgates, h2, "dw_in_gates")], axis=0)
    row0, pieces_in = 0, []
    for a, _ in dheads:
        pieces_in.append((a, row0))
        row0 += a.shape[0] * HEAD_DIM
    dx1, dgmix = _proj_bwd(pieces_in, dgates, w_in_s, x1, gmix, dx2)

    c = lax.axis_index("c")

    def reduce_start(lo, hi, tag):
        gbuf = jnp.concatenate([grads[_BUF[i][0]].astype(BF16).reshape(N_CHIPS, sizes[i], D) for i in range(lo, hi)],
                               axis=1)
        half = gbuf.shape[1] // 2
        from_sibling = _swap_halves(gbuf, "swap_halves_" + tag)
        my_half = lax.dynamic_slice_in_dim(gbuf, c * half, half, axis=1)
        return _add_halves(my_half, from_sibling, "add_sibling_" + tag)

    def reduce_finish(chip_sum, received, tag):
        own = lax.dynamic_index_in_dim(chip_sum, chip, axis=0, keepdims=False)
        my_rows = _add_received(own, received, "add_chips_" + tag)
        their_rows = _swap_reduced(my_rows, "swap_reduced_" + tag)
        return jnp.concatenate([jnp.where(c == 0, my_rows, their_rows), jnp.where(c == 0, their_rows, my_rows)],
                               axis=0)

    sum_b, sum16_b = reduce_start(n_first, len(_BUF), "late")
    dx0, dg1, dz1, da1, db1, u1, received_b = _ffn_bwd(dx1, xs, g1, a1, b1, f1w1, f1w3, f1w2, "ffn1_bwd",
                                                       scatter=sum16_b)
    grads["ffn1_w1"] = _tn_matmul(da1, h1, "ffn1_dw1")
    grads["ffn1_w3"] = _tn_matmul(db1, h1, "ffn1_dw3")
    grads["ffn1_w2"] = _tn_matmul(u1, dz1, "ffn1_dw2")
    sum_a, sum16_a = reduce_start(0, n_first, "first")
    reduced = jnp.concatenate([reduce_finish(sum_a, _scatter_to_owners(sum16_a, "scatter_to_owners"), "first"),
                               reduce_finish(sum_b, received_b, "late")], axis=0)

    small_rows = [dg1, dgmix, dg3, dgf,
                  jnp.pad(d_sinks.reshape(1, -1), ((0, 0), (0, D - SWA_Q_HEADS))),
                  jnp.pad(d_rel.reshape(1, -1), ((0, 0), (0, D - REL_BUCKETS * SWA_Q_HEADS))),
                  jnp.pad(loss_part, ((0, 0), (0, D - 1))), jnp.zeros((1, D), F32)]
    small = _allreduce_small(jnp.concatenate(small_rows, axis=0))
    loss = small[6, 0]

    G, g_rows = {}, {}
    for i, nk in enumerate(_BUF):
        rows = reduced[offs[i]:offs[i + 1]]
        g_rows[nk[0]] = rows.reshape(-1, W[nk[0]].shape[1]) if nk[1] == "tw" else rows
        G[nk[0]] = _from_rows(nk, rows, W[nk[0]].shape[1])[None]
    G["norm_ffn1"], G["norm_mix"], G["norm_ffn2"] = small[0:1], small[1:2], small[2:3]
    G["norm_final"] = small[3]
    G["swa_sinks"] = small[4:5, :SWA_Q_HEADS]
    G["rel_bias"] = small[5, :REL_BUCKETS * SWA_Q_HEADS].reshape(REL_BUCKETS, SWA_Q_HEADS)

    delta, new_m, new_v = {}, {}, {}
    small_names = ["norm_ffn1", "norm_mix", "norm_ffn2", "norm_final", "swa_sinks", "rel_bias"]

    def pack(d):
        return jnp.concatenate([jnp.pad(d[n].reshape(1, -1), ((0, 0), (0, D - d[n].size))) for n in small_names]
                               + [jnp.zeros((2, D), F32)], axis=0)

    sd, sm, sv = _adamw(pack(W), pack(G), pack(M), pack(V), "adamw_small")
    for r, n in enumerate(small_names):
        for dst, src in ((delta, sd), (new_m, sm), (new_v, sv)):
            dst[n] = src[r, :W[n].size].reshape(W[n].shape)
    for n, kind in _BUF:
        turn = (lambda a: a) if kind == "n" else (lambda a: a.T)
        d_, m_, v_ = _adamw(turn(W[n][0]), g_rows[n], turn(M[n][0]), turn(V[n][0]), "adamw_" + n)
        delta[n], new_m[n], new_v[n] = turn(d_)[None], turn(m_)[None], turn(v_)[None]

    return (loss, dx0[None], *[G[n] for n in names], *[delta[n] for n in names],
            *[new_m[n] for n in names], *[new_v[n] for n in names])
```

```python
import math

import jax
import jax.numpy as jnp
from jax import lax
from jax.experimental import pallas as pl
from jax.experimental.pallas import tpu as pltpu

F32, BF16 = jnp.float32, jnp.bfloat16
MESH_ID = pl.DeviceIdType.MESH
ANY = pl.BlockSpec(memory_space=pl.ANY)

RMS_EPS = 1e-6
HEAD_DIM = 64
SWA_Q_HEADS, SWA_KV_HEADS, SWA_GROUP = 8, 2, 4
SWA_BLOCK = 128
SB_HEADS = 8
SB_BLOCK = 256
REL_BUCKETS, REL_MAX_DIST = 32, 128
NEG_BIG = -1e30
QK_SCALE = HEAD_DIM ** -0.5
ADAM_LR, ADAM_B1, ADAM_B2, ADAM_EPS, ADAM_WD, ADAM_STEP = 0.001, 0.9, 0.999, 1e-08, 0.01, 10

N_CHIPS = 4
TOKEN_TILE = 512
MATMUL_TOKEN_TILE = 1024
WGRAD_ROW_TILES = (2176, 1408, 1024, 256)
FF_TILE = 2816
FFN_TOKEN_TILE = 512
PROJ_HEAD_ROWS = 768
FF_BWD_TILE = 1408
VMEM_LIMIT = 60 * 1024 * 1024


def _cp(*sem):
    return pltpu.CompilerParams(dimension_semantics=sem, vmem_limit_bytes=VMEM_LIMIT)


def _nn(a, b):
    return jnp.dot(a, b, preferred_element_type=F32)


def _nt(a, b):
    return lax.dot_general(a, b, (((1,), (1,)), ((), ())), preferred_element_type=F32)


def _tn(a, b):
    return lax.dot_general(a, b, (((0,), (0,)), ((), ())), preferred_element_type=F32)


def _norm_fwd(x, g):
    return x * lax.rsqrt(jnp.mean(x * x, axis=-1, keepdims=True) + RMS_EPS) * g


def _norm_bwd(x, g, dh):
    r = lax.rsqrt(jnp.mean(x * x, axis=-1, keepdims=True) + RMS_EPS)
    xh = x * r
    dxh = dh * g
    dx = r * (dxh - xh * jnp.mean(dxh * xh, axis=-1, keepdims=True))
    return dx, jnp.sum(dh * xh, axis=0, keepdims=True)


SOFTPLUS_LINEAR = 20.0


def _softplus(z):
    return jnp.maximum(jnp.log(1.0 + jnp.exp(jnp.minimum(z, SOFTPLUS_LINEAR))), z)


def _ffn_fwd(x, g, w1t, w3t, w2, name, gather=None):
    S, D = x.shape
    F = w2.shape[0]
    tm, tf = min(FFN_TOKEN_TILE, S), FF_TILE
    ni, nj = S // tm, F // tf

    def body(x_ref, g_ref, w1_ref, w3_ref, w2_ref, *rest):
        if gather is None:
            xo_ref, h_ref, a_ref, b_ref, hs, acc = rest
        else:
            shard_ref, xo_ref, h_ref, a_ref, b_ref, gathered_ref, hs, acc, send_sems, recv_sems = rest
        i, j = pl.program_id(0), pl.program_id(1)
        if gather is not None:
            for when, phase in ((jnp.logical_and(i == 0, j == 0), "start"),
                                (jnp.logical_and(i == ni - 1, j == 0), "forward"),
                                (jnp.logical_and(i == ni - 1, j == nj - 1), "finish")):
                @pl.when(when)
                def _():
                    getattr(_gather_exchange(shard_ref, gathered_ref, send_sems, recv_sems), phase)()

        @pl.when(j == 0)
        def _():
            hb = _norm_fwd(x_ref[...], g_ref[...]).astype(BF16)
            hs[...] = hb
            h_ref[...] = hb
            acc[...] = jnp.zeros_like(acc)

        h = hs[...]
        a = _nt(h, w1_ref[...])
        b = _nt(h, w3_ref[...])
        a_ref[...] = a.astype(BF16)
        b_ref[...] = b.astype(BF16)
        u = a * jax.nn.sigmoid(a) * b
        acc[...] += _nn(u.astype(BF16), w2_ref[...])

        @pl.when(j == nj - 1)
        def _():
            xo_ref[...] = x_ref[...] + 0.5 * acc[...]

    in_specs = [pl.BlockSpec((tm, D), lambda i, j: (i, 0)),
                pl.BlockSpec((1, D), lambda i, j: (0, 0)),
                pl.BlockSpec((tf, D), lambda i, j: (j, 0), pipeline_mode=pl.Buffered(1)),
                pl.BlockSpec((tf, D), lambda i, j: (j, 0), pipeline_mode=pl.Buffered(1)),
                pl.BlockSpec((tf, D), lambda i, j: (j, 0), pipeline_mode=pl.Buffered(1))]
    out_specs = [pl.BlockSpec((tm, D), lambda i, j: (i, 0)),
                 pl.BlockSpec((tm, D), lambda i, j: (i, 0)),
                 pl.BlockSpec((tm, tf), lambda i, j: (i, j)),
                 pl.BlockSpec((tm, tf), lambda i, j: (i, j))]
    out_shape = [jax.ShapeDtypeStruct((S, D), F32), jax.ShapeDtypeStruct((S, D), BF16),
                 jax.ShapeDtypeStruct((S, F), BF16), jax.ShapeDtypeStruct((S, F), BF16)]
    scratch = [pltpu.VMEM((tm, D), BF16), pltpu.VMEM((tm, D), F32)]
    operands = [x, g, w1t, w3t, w2]
    if gather is not None:
        R, C = gather.shape
        in_specs.append(ANY)
        out_specs.append(ANY)
        out_shape.append(jax.ShapeDtypeStruct((N_CHIPS, 2, R // 2, C), gather.dtype))
        scratch += [pltpu.SemaphoreType.DMA((6,)), pltpu.SemaphoreType.DMA((6,))]
        operands.append(gather.reshape(2, R // 2, C))
    outs = list(pl.pallas_call(
        body, name=name, grid=(ni, nj), in_specs=in_specs, out_specs=out_specs, out_shape=out_shape,
        scratch_shapes=scratch, compiler_params=_cp("arbitrary", "arbitrary"),
    )(*operands))
    if gather is not None:
        outs[4] = outs[4].reshape(N_CHIPS, R, C)
    return outs


def _ffn_bwd(dxo, x, g, a, b, w1t, w3t, w2, name, scatter=None):
    S, D = x.shape
    F = w2.shape[0]
    tm, tf = min(FFN_TOKEN_TILE, S), FF_BWD_TILE
    ni, nj = S // tm, F // tf

    def body(dxo_ref, x_ref, g_ref, a_ref, b_ref, w1_ref, w3_ref, w2_ref, *rest):
        if scatter is None:
            dx_ref, dg_ref, dz_ref, da_ref, db_ref, u_ref, dzs, acc = rest
        else:
            (parts_ref, dx_ref, dg_ref, dz_ref, da_ref, db_ref, u_ref, recv_ref,
             dzs, acc, send_sems, recv_sems) = rest
        i, j = pl.program_id(0), pl.program_id(1)
        if scatter is not None:
            for when, phase in ((jnp.logical_and(i == 0, j == 0), "start"),
                                (jnp.logical_and(i == ni - 1, j == nj - 1), "finish")):
                @pl.when(when)
                def _():
                    getattr(_scatter_exchange(parts_ref, recv_ref, send_sems, recv_sems), phase)()

        @pl.when(j == 0)
        def _():
            dzb = (0.5 * dxo_ref[...]).astype(BF16)
            dzs[...] = dzb
            dz_ref[...] = dzb
            acc[...] = jnp.zeros_like(acc)

        du = _nt(dzs[...], w2_ref[...])
        av = a_ref[...].astype(F32)
        bv = b_ref[...].astype(F32)
        s = jax.nn.sigmoid(av)
        silu = av * s
        db = (du * silu).astype(BF16)
        da = (du * bv * (s * (1.0 + av * (1.0 - s)))).astype(BF16)
        da_ref[...] = da
        db_ref[...] = db
        u_ref[...] = (silu * bv).astype(BF16)
        acc[...] += _nn(da, w1_ref[...]) + _nn(db, w3_ref[...])

        @pl.when(j == nj - 1)
        def _():
            dx, dg = _norm_bwd(x_ref[...], g_ref[...], acc[...])
            dx_ref[...] = dxo_ref[...] + dx

            @pl.when(i == 0)
            def _():
                dg_ref[...] = dg

            @pl.when(i > 0)
            def _():
                dg_ref[...] += dg

    row = pl.BlockSpec((tm, D), lambda i, j: (i, 0))
    wsp = pl.BlockSpec((tf, D), lambda i, j: (j, 0))
    col = pl.BlockSpec((tm, tf), lambda i, j: (i, j))
    vec = pl.BlockSpec((1, D), lambda i, j: (0, 0))
    col_out = pl.BlockSpec((tm, tf), lambda i, j: (i, j))
    in_specs = [row, row, vec, col, col, wsp, wsp, wsp]
    out_specs = [row, vec, row, col_out, col_out, col_out]
    out_shape = [jax.ShapeDtypeStruct((S, D), F32), jax.ShapeDtypeStruct((1, D), F32),
                 jax.ShapeDtypeStruct((S, D), BF16), jax.ShapeDtypeStruct((S, F), BF16),
                 jax.ShapeDtypeStruct((S, F), BF16), jax.ShapeDtypeStruct((S, F), BF16)]
    scratch = [pltpu.VMEM((tm, D), BF16), pltpu.VMEM((tm, D), F32)]
    operands = [dxo, x, g, a, b, w1t, w3t, w2]
    if scatter is not None:
        in_specs.append(ANY)
        out_specs.append(ANY)
        out_shape.append(jax.ShapeDtypeStruct((3,) + scatter.shape[1:], scatter.dtype))
        scratch += [pltpu.SemaphoreType.DMA((3,)), pltpu.SemaphoreType.DMA((3,))]
        operands.append(scatter)
    return pl.pallas_call(
        body, name=name, grid=(ni, nj), in_specs=in_specs, out_specs=out_specs, out_shape=out_shape,
        scratch_shapes=scratch, compiler_params=_cp("arbitrary", "arbitrary"),
    )(*operands)


def _tn_matmul(a, b, name):
    S, M = a.shape
    N = b.shape[1]
    ts = min(MATMUL_TOKEN_TILE, S)
    tmm = next(t for t in WGRAD_ROW_TILES if M % t == 0)
    ns = S // ts

    def body(a_ref, b_ref, o_ref, acc):
        s = pl.program_id(1)
        part = _tn(a_ref[...], b_ref[...])

        @pl.when(s == 0)
        def _():
            acc[...] = part

        @pl.when(s > 0)
        def _():
            acc[...] += part

        @pl.when(s == ns - 1)
        def _():
            o_ref[...] = acc[...].astype(BF16)

    return pl.pallas_call(
        body, name=name, grid=(M // tmm, ns),
        in_specs=[pl.BlockSpec((ts, tmm), lambda m, s: (s, m)),
                  pl.BlockSpec((ts, N), lambda m, s: (s, 0))],
        out_specs=pl.BlockSpec((tmm, N), lambda m, s: (m, 0)),
        out_shape=jax.ShapeDtypeStruct((M, N), BF16),
        scratch_shapes=[pltpu.VMEM((tmm, N), F32)],
        compiler_params=_cp("arbitrary", "arbitrary"),
    )(a, b)


def _norm_matmul_nt(x, g, wt, out_dtype, name):
    S, D = x.shape
    N = wt.shape[0]
    tm = min(MATMUL_TOKEN_TILE, S)
    tn = next(t for t in (1024, 768, 256) if N % t == 0)

    def body(x_ref, g_ref, w_ref, o_ref, h_ref, hs):
        @pl.when(pl.program_id(1) == 0)
        def _():
            hb = _norm_fwd(x_ref[...], g_ref[...]).astype(BF16)
            hs[...] = hb
            h_ref[...] = hb

        o_ref[...] = _nt(hs[...], w_ref[...]).astype(out_dtype)

    return pl.pallas_call(
        body, name=name, grid=(S // tm, N // tn),
        in_specs=[pl.BlockSpec((tm, D), lambda i, j: (i, 0)),
                  pl.BlockSpec((1, D), lambda i, j: (0, 0)),
                  pl.BlockSpec((tn, D), lambda i, j: (j, 0))],
        out_specs=[pl.BlockSpec((tm, tn), lambda i, j: (i, j)),
                   pl.BlockSpec((tm, D), lambda i, j: (i, 0))],
        out_shape=[jax.ShapeDtypeStruct((S, N), out_dtype), jax.ShapeDtypeStruct((S, D), BF16)],
        scratch_shapes=[pltpu.VMEM((tm, D), BF16)],
        compiler_params=_cp("arbitrary", "arbitrary"),
    )(x, g, wt)


def _heads_tile(ref):
    Hh, nbk = ref.shape[0], ref.shape[1]
    return jnp.concatenate([jnp.concatenate([ref[h, b] for b in range(nbk)], axis=1) for h in range(Hh)], axis=0)


def _store_heads(ref, val):
    Hh, nbk, dh, T = ref.shape
    for h in range(Hh):
        for b in range(nbk):
            ref[h, b] = val[h * dh:(h + 1) * dh, b * T:(b + 1) * T].astype(ref.dtype)


def _norm_proj_heads(x, g, w_rows, T, name):
    S, D = x.shape
    N = w_rows.shape[0]
    tm, tn = min(MATMUL_TOKEN_TILE, S), PROJ_HEAD_ROWS

    def body(x_ref, g_ref, w_ref, o_ref, hs):
        @pl.when(pl.program_id(1) == 0)
        def _():
            hs[...] = _norm_fwd(x_ref[...], g_ref[...]).astype(BF16)

        _store_heads(o_ref, _nt(w_ref[...], hs[...]))

    return pl.pallas_call(
        body, name=name, grid=(S // tm, N // tn),
        in_specs=[pl.BlockSpec((tm, D), lambda i, j: (i, 0)),
                  pl.BlockSpec((1, D), lambda i, j: (0, 0)),
                  pl.BlockSpec((tn, D), lambda i, j: (j, 0))],
        out_specs=pl.BlockSpec((tn // HEAD_DIM, tm // T, HEAD_DIM, T), lambda i, j: (j, i, 0, 0)),
        out_shape=jax.ShapeDtypeStruct((N // HEAD_DIM, S // T, HEAD_DIM, T), BF16),
        scratch_shapes=[pltpu.VMEM((tm, D), BF16)],
        compiler_params=_cp("arbitrary", "arbitrary"),
    )(x, g, w_rows)


def _heads_matmul(pieces, b, name):
    S, N = b.shape
    ts = min(MATMUL_TOKEN_TILE, S)
    ns = S // ts
    rows = [at.shape[0] * at.shape[2] for at, _ in pieces]

    def body(*refs):
        a_refs, b_ref, o_ref = refs[:-2], refs[-2], refs[-1]
        s = pl.program_id(0)
        row0 = 0
        for a_ref, (_, scale), n in zip(a_refs, pieces, rows):
            a = _heads_tile(a_ref)
            part = _nn((a if scale == 1.0 else a * scale).astype(BF16), b_ref[...])
            out = o_ref.at[row0:row0 + n, :]
            row0 += n

            @pl.when(s == 0)
            def _():
                out[...] = part

            @pl.when(s > 0)
            def _():
                out[...] += part

    return pl.pallas_call(
        body, name=name, grid=(ns,),
        in_specs=[pl.BlockSpec((at.shape[0], ts // at.shape[3], at.shape[2], at.shape[3]), lambda s: (0, s, 0, 0))
                  for at, _ in pieces] + [pl.BlockSpec((ts, N), lambda s: (s, 0))],
        out_specs=pl.BlockSpec((sum(rows), N), lambda s: (0, 0)),
        out_shape=jax.ShapeDtypeStruct((sum(rows), N), F32),
        compiler_params=_cp("arbitrary"),
    )(*[at for at, _ in pieces], b)


def _proj_bwd(pieces, dgates, w_rows, x, g, dres):
    S, D = x.shape
    tm = min(TOKEN_TILE, S)
    n_p = len(pieces)
    gate_row = w_rows.shape[0] - dgates.shape[1]

    def body(*refs):
        p_refs = refs[:n_p]
        dgt_ref, w_ref, x_ref, g_ref, dres_ref, dx_ref, dg_ref = refs[n_p:]
        i = pl.program_id(0)
        dh = _nn(dgt_ref[...], w_ref[gate_row:, :])
        for p_ref, (arr, row0) in zip(p_refs, pieces):
            rows = arr.shape[0] * arr.shape[2]
            dh += _tn(_heads_tile(p_ref).astype(BF16), w_ref[row0:row0 + rows, :])
        dx, dg = _norm_bwd(x_ref[...], g_ref[...], dh)
        dx_ref[...] = dres_ref[...] + dx

        @pl.when(i == 0)
        def _():
            dg_ref[...] = dg

        @pl.when(i > 0)
        def _():
            dg_ref[...] += dg

    row = pl.BlockSpec((tm, D), lambda i: (i, 0))
    vec = pl.BlockSpec((1, D), lambda i: (0, 0))
    p_specs = [pl.BlockSpec((a.shape[0], tm // a.shape[3], a.shape[2], a.shape[3]), lambda i: (0, i, 0, 0))
               for a, _ in pieces]
    return pl.pallas_call(
        body, name="proj_bwd", grid=(S // tm,),
        in_specs=p_specs + [pl.BlockSpec((tm, dgates.shape[1]), lambda i: (i, 0)),
                            pl.BlockSpec(w_rows.shape, lambda i: (0, 0), pipeline_mode=pl.Buffered(1)),
                            row, vec, row],
        out_specs=[row, vec],
        out_shape=[jax.ShapeDtypeStruct((S, D), F32), jax.ShapeDtypeStruct((1, D), F32)],
        compiler_params=_cp("arbitrary"),
    )(*[a for a, _ in pieces], dgates, w_rows, x, g, dres)


def _merge_fwd(x1, gates, oa_t, ob_t, wat, wbt, w_out):
    S, D = x1.shape
    W = wat.shape[1]
    tm = min(TOKEN_TILE, S)

    def body(x_ref, ga_ref, gb_ref, oa_ref, ob_ref, wa_ref, wb_ref, wo_ref,
             x2_ref, mg_ref, ba_ref, bb_ref):
        ba = _nt(_heads_tile(oa_ref).T.astype(BF16), wa_ref[...])
        bb = _nt(_heads_tile(ob_ref).T.astype(BF16), wb_ref[...])
        merged = jax.nn.sigmoid(ga_ref[...]) * ba + jax.nn.sigmoid(gb_ref[...]) * bb
        mb = merged.astype(BF16)
        mg_ref[...] = mb
        ba_ref[...] = ba.astype(BF16)
        bb_ref[...] = bb.astype(BF16)
        x2_ref[...] = x_ref[...] + _nn(mb, wo_ref[...])

    row = pl.BlockSpec((tm, D), lambda i: (i, 0))
    full = lambda r, c: pl.BlockSpec((r, c), lambda i: (0, 0))
    heads = lambda a: pl.BlockSpec((a.shape[0], tm // a.shape[3], a.shape[2], a.shape[3]), lambda i: (0, i, 0, 0))
    return pl.pallas_call(
        body, name="merge_fwd", grid=(S // tm,),
        in_specs=[row, pl.BlockSpec((tm, D), lambda i: (i, 0)), pl.BlockSpec((tm, D), lambda i: (i, 1)),
                  heads(oa_t), heads(ob_t), full(D, W), full(D, W), full(D, D)],
        out_specs=[row, row, row, row],
        out_shape=[jax.ShapeDtypeStruct((S, D), F32)] + [jax.ShapeDtypeStruct((S, D), BF16)] * 3,
        compiler_params=_cp("arbitrary"),
    )(x1, gates, gates, oa_t, ob_t, wat, wbt, w_out)


def _merge_bwd(dx2, gates, ba, bb, wa, wb, w_out, t_a, t_b):
    S, D = dx2.shape
    W = wa.shape[0]
    tm = min(TOKEN_TILE, S)
    Hh = W // HEAD_DIM

    def body(dx_ref, ga_ref, gb_ref, ba_ref, bb_ref, wa_ref, wb_ref, wo_ref,
             dxb_ref, dba_ref, dbb_ref, dgt_ref, doa_ref, dob_ref):
        dxb = dx_ref[...].astype(BF16)
        dxb_ref[...] = dxb
        dm = _nt(dxb, wo_ref[...])
        sa = jax.nn.sigmoid(ga_ref[...])
        sb = jax.nn.sigmoid(gb_ref[...])
        dba = (dm * sa).astype(BF16)
        dbb = (dm * sb).astype(BF16)
        dba_ref[...] = dba
        dbb_ref[...] = dbb
        dgt_ref[:, :D] = (dm * ba_ref[...].astype(F32) * sa * (1.0 - sa)).astype(BF16)
        dgt_ref[:, D:] = (dm * bb_ref[...].astype(F32) * sb * (1.0 - sb)).astype(BF16)
        _store_heads(doa_ref, _nt(wa_ref[...], dba))
        _store_heads(dob_ref, _nt(wb_ref[...], dbb))

    row = pl.BlockSpec((tm, D), lambda i: (i, 0))
    full = lambda r, c: pl.BlockSpec((r, c), lambda i: (0, 0))
    heads = lambda T: pl.BlockSpec((Hh, tm // T, HEAD_DIM, T), lambda i: (0, i, 0, 0))
    return pl.pallas_call(
        body, name="merge_bwd", grid=(S // tm,),
        in_specs=[row, pl.BlockSpec((tm, D), lambda i: (i, 0)), pl.BlockSpec((tm, D), lambda i: (i, 1)),
                  row, row, full(W, D), full(W, D), full(D, D)],
        out_specs=[row, row, row, pl.BlockSpec((tm, 2 * D), lambda i: (i, 0)), heads(t_a), heads(t_b)],
        out_shape=[jax.ShapeDtypeStruct((S, D), BF16)] * 3 + [jax.ShapeDtypeStruct((S, 2 * D), BF16),
                   jax.ShapeDtypeStruct((Hh, S // t_a, HEAD_DIM, t_a), F32),
                   jax.ShapeDtypeStruct((Hh, S // t_b, HEAD_DIM, t_b), BF16)],
        compiler_params=_cp("arbitrary"),
    )(dx2, gates, gates, ba, bb, wa, wb, w_out)


def _final_loss(x3, gf, target):
    S, D = x3.shape
    tm = min(TOKEN_TILE, S)

    def body(x_ref, g_ref, t_ref, loss_ref, dx_ref, dg_ref):
        i = pl.program_id(0)
        x = x_ref[...]
        g = g_ref[...]
        e = _norm_fwd(x, g) - t_ref[...]
        part = 0.5 * jnp.sum(jnp.mean(e * e, axis=-1, keepdims=True), axis=0, keepdims=True)
        dx, dg = _norm_bwd(x, g, e * (1.0 / D))
        dx_ref[...] = dx

        @pl.when(i == 0)
        def _():
            loss_ref[...] = part
            dg_ref[...] = dg

        @pl.when(i > 0)
        def _():
            loss_ref[...] += part
            dg_ref[...] += dg

    row = pl.BlockSpec((tm, D), lambda i: (i, 0))
    vec = pl.BlockSpec((1, D), lambda i: (0, 0))
    return pl.pallas_call(
        body, name="final_loss", grid=(S // tm,),
        in_specs=[row, vec, row],
        out_specs=[pl.BlockSpec((1, 1), lambda i: (0, 0)), row, vec],
        out_shape=[jax.ShapeDtypeStruct((1, 1), F32), jax.ShapeDtypeStruct((S, D), F32),
                   jax.ShapeDtypeStruct((1, D), F32)],
        compiler_params=_cp("arbitrary"),
    )(x3, gf, target)


SB_FWD_HEAD_GROUP = 8
SB_HEAD_GROUP = 4
LANES = 128


def _tri(T, kind):
    r = lax.broadcasted_iota(jnp.int32, (T, T), 0)
    c = lax.broadcasted_iota(jnp.int32, (T, T), 1)
    return {"after": r > c, "before": r < c}[kind].astype(BF16)


def _lane(v, j):
    return jnp.broadcast_to(v[:, j:j + 1], (v.shape[0], LANES))


def _t_bf16(x):
    return x.astype(F32).T.astype(BF16)


def _wide(v, T):
    return jnp.tile(v, (1, T // LANES))


SB_SLOTS = 3
SB_FWD_SLOTS = 2
COPY_PARTS = 4


class _split_copy:
    def __init__(self, src, dst, sems):
        n = src.shape[0] // COPY_PARTS
        self.parts = [pltpu.make_async_copy(src.at[pl.ds(r * n, n)], dst.at[pl.ds(r * n, n)], sems.at[r])
                      for r in range(COPY_PARTS)]

    def start(self):
        for cp in self.parts:
            cp.start()

    def wait(self):
        for cp in self.parts:
            cp.wait()


def _sb_pair(i, kb):
    return (i * (i + 1)) // 2 + kb


def _sb_fwd(qkv):
    H3, nb, dh, T = qkv.shape
    H = H3 // 3
    HG = SB_FWD_HEAD_GROUP
    assert HG == H, "one head group: a saved tile holds all the heads"
    n_pairs = (nb * (nb + 1)) // 2

    def body(q_ref, k_ref, v_ref, o_ref, saved_ref, stage, sems):
        row = lax.broadcasted_iota(jnp.int32, (T, T), 0)
        col = lax.broadcasted_iota(jnp.int32, (T, T), 1)
        tri = col < row
        after = _tri(T, "after")

        def save(slot, pair):
            return _split_copy(stage.at[slot], saved_ref.at[pair], sems.at[slot])

        def blocks(qs, i, kb, step, carry, diag):
            hs = range(HG)
            slot = step % SB_FWD_SLOTS

            @pl.when(step >= SB_FWD_SLOTS)
            def _():
                save(slot, 0).wait()

            z = [_nn(qs[hh], k_ref[hh, kb]) for hh in hs]
            res, ls, first = [None] * HG, [None] * HG, [None] * HG
            for hh in hs:
                sp = _softplus(z[hh])
                if diag:
                    sp = jnp.where(tri, sp, 0.0)
                ls[hh] = z[hh] - sp
                spb = sp.astype(BF16)
                first[hh] = _lane(spb.astype(F32), 0)
                res[hh] = _nn(spb, after)
            out = []
            for hh in hs:
                c, oacc = carry[2 * hh], carry[2 * hh + 1]
                a = jnp.exp(ls[hh] - (res[hh] + _wide(c, T)))
                if diag:
                    a = jnp.where(tri, a, 0.0)
                ab = a.astype(BF16)
                stage[slot, hh, 0] = ab
                stage[slot, hh, 1] = jnp.exp(ls[hh]).astype(BF16)
                out.extend([c + (first[hh] + _lane(res[hh], 0)), oacc + _nt(v_ref[hh, kb], ab)])
            save(slot, _sb_pair(i, kb)).start()
            return tuple(out)

        def qblock(i, step):
            qs = [_t_bf16(q_ref[hh, i]) for hh in range(HG)]
            carry = blocks(qs, i, i, step, (jnp.zeros((T, LANES), F32), jnp.zeros((dh, T), F32)) * HG, True)

            def kstep(t, carry):
                return blocks(qs, i, i - 1 - t, step + 1 + t, carry, False)

            carry = lax.fori_loop(0, i, kstep, carry)
            for hh in range(HG):
                o_ref[hh, i] = carry[2 * hh + 1]
            return step + 1 + i

        lax.fori_loop(0, nb, qblock, 0)
        for slot in range(min(SB_FWD_SLOTS, n_pairs)):
            save(slot, 0).wait()

    ht = lambda part: pl.BlockSpec((HG, nb, dh, T), lambda h: (part, 0, 0, 0), pipeline_mode=pl.Buffered(1))
    return pl.pallas_call(
        body, name="sb_fwd", grid=(1,),
        in_specs=[ht(0), ht(1), ht(2)],
        out_specs=[ht(0), ANY],
        out_shape=[jax.ShapeDtypeStruct((H, nb, dh, T), F32),
                   jax.ShapeDtypeStruct((n_pairs, H, 2, T, T), BF16)],
        scratch_shapes=[pltpu.VMEM((SB_FWD_SLOTS, HG, 2, T, T), BF16),
                        pltpu.SemaphoreType.DMA((SB_FWD_SLOTS, COPY_PARTS))],
        compiler_params=_cp("arbitrary"),
    )(qkv, qkv, qkv)


def _sb_bwd(qkv, dot, saved):
    H3, nb, dh, T = qkv.shape
    H = H3 // 3
    HG = SB_HEAD_GROUP
    n_pairs = (nb * (nb + 1)) // 2

    def body(qt_ref, k_ref, v_ref, dot_ref, saved_ref, dq_ref, dk_ref, dv_ref, stage, sems):
        head0 = pl.program_id(0) * HG
        row = lax.broadcasted_iota(jnp.int32, (T, T), 0)
        col = lax.broadcasted_iota(jnp.int32, (T, T), 1)
        tri = col < row
        before = _tri(T, "before")
        dk_ref[...] = jnp.zeros_like(dk_ref)
        dv_ref[...] = jnp.zeros_like(dv_ref)

        def fetch(slot, pair):
            return _split_copy(saved_ref.at[pair, pl.ds(head0, HG)], stage.at[slot], sems.at[slot])

        for ahead in range(min(SB_SLOTS - 1, n_pairs)):
            fetch(ahead, ahead).start()

        def blocks(qTs, dos, doTs, i, kb, carry, diag):
            hs = range(HG)
            pair = _sb_pair(i, kb)
            slot = pair % SB_SLOTS
            fetch(slot, pair).wait()
            nxt = pair + (SB_SLOTS - 1)

            @pl.when(nxt < n_pairs)
            def _():
                fetch(nxt % SB_SLOTS, nxt).start()

            kT = [k_ref[hh, kb] for hh in hs]
            da = [_nn(dos[hh], v_ref[hh, kb]) for hh in hs]
            g, gb, resg = [None] * HG, [None] * HG, [None] * HG
            for hh in hs:
                g[hh] = stage[slot, hh, 0].astype(F32) * da[hh]
                gb[hh] = g[hh].astype(BF16)
                resg[hh] = _nn(gb[hh], before)
            out = []
            for hh in hs:
                pre_g, dq = carry[2 * hh], carry[2 * hh + 1]
                dz = g[hh] - (g[hh] + (resg[hh] + _wide(pre_g, T))) * stage[slot, hh, 1].astype(F32)
                if diag:
                    dz = jnp.where(tri, dz, 0.0)
                dzb = dz.astype(BF16)
                dk_ref[hh, kb] += _nn(qTs[hh], dzb)
                dv_ref[hh, kb] += _nn(doTs[hh], stage[slot, hh, 0])
                out.extend([pre_g + (_lane(resg[hh], T - 1) + _lane(gb[hh].astype(F32), T - 1)),
                            dq + _nt(kT[hh], dzb)])
            return tuple(out)

        def qblock(i, _):
            qTs = [qt_ref[hh, i] for hh in range(HG)]
            doTs = [dot_ref[hh, i] for hh in range(HG)]
            dos = [_t_bf16(v) for v in doTs]
            carry = (jnp.zeros((T, LANES), F32), jnp.zeros((dh, T), F32)) * HG

            def kstep(kb, carry):
                return blocks(qTs, dos, doTs, i, kb, carry, False)

            carry = lax.fori_loop(0, i, kstep, carry)
            carry = blocks(qTs, dos, doTs, i, i, carry, True)
            for hh in range(HG):
                dq_ref[hh, i] = carry[2 * hh + 1]
            return 0

        lax.fori_loop(0, nb, qblock, 0)

    G = H // HG
    ht = lambda part: pl.BlockSpec((HG, nb, dh, T), lambda h: (h + part * G, 0, 0, 0),
                                   pipeline_mode=pl.Buffered(1))
    return pl.pallas_call(
        body, name="sb_bwd", grid=(G,),
        in_specs=[ht(0), ht(1), ht(2), ht(0), ANY],
        out_specs=[ht(0), ht(0), ht(0)],
        out_shape=[jax.ShapeDtypeStruct((H, nb, dh, T), F32)] * 3,
        scratch_shapes=[pltpu.VMEM((SB_SLOTS, HG, 2, T, T), BF16), pltpu.SemaphoreType.DMA((SB_SLOTS, COPY_PARTS))],
        compiler_params=_cp("arbitrary"),
    )(qkv, qkv, qkv, dot, saved)


def _swa_probs(zp, zc, bias, sink, first):
    T = zp.shape[0]
    key = lax.broadcasted_iota(jnp.int32, (T, T), 0)
    qry = lax.broadcasted_iota(jnp.int32, (T, T), 1)
    lp = jnp.where(jnp.logical_and(key > qry, jnp.logical_not(first)), zp + bias[:T, :], NEG_BIG)
    lc = jnp.where(key <= qry, zc + bias[T:, :], NEG_BIG)
    m = jnp.maximum(jnp.maximum(jnp.max(lp, axis=0, keepdims=True), jnp.max(lc, axis=0, keepdims=True)), sink)
    pp = jnp.exp(lp - m)
    pc = jnp.exp(lc - m)
    ps = jnp.exp(sink - m)
    inv = 1.0 / (jnp.sum(pp, axis=0, keepdims=True) + jnp.sum(pc, axis=0, keepdims=True) + ps)
    return pp * inv, pc * inv, ps * inv


def _swa_specs(nb, dh, T, Hq, Hkv, clamp):
    blk = (lambda n: jnp.minimum(n, nb - 1)) if clamp else (lambda n: n)
    q = pl.BlockSpec((Hq, None, dh, T), lambda n: (0, blk(n), 0, 0))
    kv = lambda first, back: pl.BlockSpec(
        (Hkv, None, dh, T), lambda n: (first // Hkv, jnp.maximum(blk(n) - back, 0) if back else blk(n), 0, 0))
    return q, [kv(Hq, 1), kv(Hq, 0), kv(Hq + Hkv, 1), kv(Hq + Hkv, 0)]


def _swa_fwd(qkv, bias, sinks):
    Hq, Hkv, grp = SWA_Q_HEADS, SWA_KV_HEADS, SWA_GROUP
    _, nb, dh, T = qkv.shape

    def body(sink_ref, q_ref, kp_ref, kc_ref, vp_ref, vc_ref, bias_ref, o_ref):
        n = pl.program_id(0)
        kpn = [_t_bf16(kp_ref[hk]) for hk in range(Hkv)]
        kcn = [_t_bf16(kc_ref[hk]) for hk in range(Hkv)]
        zs = [(_nn(kpn[h // grp], q_ref[h]), _nn(kcn[h // grp], q_ref[h])) for h in range(Hq)]
        for h in range(Hq):
            pp, pc, _ = _swa_probs(*zs[h], bias_ref[h], sink_ref[h], n == 0)
            o_ref[h] = _nn(vp_ref[h // grp], pp.astype(BF16)) + _nn(vc_ref[h // grp], pc.astype(BF16))

    q_spec, kv_specs = _swa_specs(nb, dh, T, Hq, Hkv, False)
    return pl.pallas_call(
        body, name="swa_fwd", grid=(nb,),
        in_specs=[pl.BlockSpec(memory_space=pltpu.SMEM), q_spec] + kv_specs
                 + [pl.BlockSpec((Hq, 2 * T, T), lambda n: (0, 0, 0))],
        out_specs=pl.BlockSpec((Hq, None, dh, T), lambda n: (0, n, 0, 0)),
        out_shape=jax.ShapeDtypeStruct((Hq, nb, dh, T), F32),
        compiler_params=_cp("arbitrary"),
    )(sinks, qkv, qkv, qkv, qkv, qkv, bias)


def _swa_bwd(qkv, bias, sinks, dot, ot):
    Hq, Hkv, grp = SWA_Q_HEADS, SWA_KV_HEADS, SWA_GROUP
    _, nb, dh, T = qkv.shape

    def body(sink_ref, qt_ref, kp_ref, kc_ref, vp_ref, vc_ref, bias_ref, dot_ref, ot_ref,
             dq_ref, dk_ref, dv_ref, dbias_ref, dsink_ref, ck, cv):
        n = pl.program_id(0)

        @pl.when(n == 0)
        def _():
            dbias_ref[...] = jnp.zeros_like(dbias_ref)
            dsink_ref[...] = jnp.zeros_like(dsink_ref)
            ck[...] = jnp.zeros_like(ck)
            cv[...] = jnp.zeros_like(cv)

        @pl.when(n < nb)
        def _():
            kp, kc = [kp_ref[hk] for hk in range(Hkv)], [kc_ref[hk] for hk in range(Hkv)]
            kpn, kcn = [_t_bf16(v) for v in kp], [_t_bf16(v) for v in kc]
            vpn = [_t_bf16(vp_ref[hk]) for hk in range(Hkv)]
            vcn = [_t_bf16(vc_ref[hk]) for hk in range(Hkv)]
            qTs = [qt_ref[h] for h in range(Hq)]
            doTs = [dot_ref[h].astype(BF16) for h in range(Hq)]
            zs = [(_nn(kpn[h // grp], qTs[h]), _nn(kcn[h // grp], qTs[h])) for h in range(Hq)]
            dps = [(_nn(vpn[h // grp], doTs[h]), _nn(vcn[h // grp], doTs[h])) for h in range(Hq)]
            dls, pbs = [], []
            for h in range(Hq):
                pp, pc, ps = _swa_probs(*zs[h], bias_ref[h], sink_ref[h], n == 0)
                delta = jnp.sum(dot_ref[h] * ot_ref[h], axis=0, keepdims=True)
                dlp = pp * (dps[h][0] - delta)
                dlc = pc * (dps[h][1] - delta)
                dbias_ref[h, :T, :] += dlp
                dbias_ref[h, T:, :] += dlc
                dsink_ref[h] += -ps * delta
                dls.append((dlp.astype(BF16), dlc.astype(BF16)))
                pbs.append((pp.astype(BF16), pc.astype(BF16)))
            zero = jnp.zeros((dh, T), F32)
            kprev, kcur, vprev, vcur = [zero] * Hkv, [zero] * Hkv, [zero] * Hkv, [zero] * Hkv
            for h in range(Hq):
                hk = h // grp
                dlpb, dlcb = dls[h]
                dq_ref[h] = _nn(kp[hk], dlpb) + _nn(kc[hk], dlcb)
                kprev[hk] = kprev[hk] + _nt(qTs[h], dlpb)
                kcur[hk] = kcur[hk] + _nt(qTs[h], dlcb)
                vprev[hk] = vprev[hk] + _nt(doTs[h], pbs[h][0])
                vcur[hk] = vcur[hk] + _nt(doTs[h], pbs[h][1])
            for hk in range(Hkv):
                dk_ref[hk] = ck[hk] + kprev[hk]
                dv_ref[hk] = cv[hk] + vprev[hk]
                ck[hk] = kcur[hk]
                cv[hk] = vcur[hk]

        @pl.when(n == nb)
        def _():
            dk_ref[...] = ck[...]
            dv_ref[...] = cv[...]

    qt_spec, kv_specs = _swa_specs(nb, dh, T, Hq, Hkv, True)
    prev = pl.BlockSpec((Hkv, None, dh, T), lambda n: (0, jnp.maximum(n - 1, 0), 0, 0))
    whole = lambda a, b: pl.BlockSpec((Hq, a, b), lambda n: (0, 0, 0))
    return pl.pallas_call(
        body, name="swa_bwd", grid=(nb + 1,),
        in_specs=[pl.BlockSpec(memory_space=pltpu.SMEM), qt_spec] + kv_specs
                 + [whole(2 * T, T), qt_spec, qt_spec],
        out_specs=[qt_spec, prev, prev, whole(2 * T, T), whole(1, T)],
        out_shape=[jax.ShapeDtypeStruct((Hq, nb, dh, T), F32), jax.ShapeDtypeStruct((Hkv, nb, dh, T), F32),
                   jax.ShapeDtypeStruct((Hkv, nb, dh, T), F32), jax.ShapeDtypeStruct((Hq, 2 * T, T), F32),
                   jax.ShapeDtypeStruct((Hq, 1, T), F32)],
        scratch_shapes=[pltpu.VMEM((Hkv, dh, T), F32), pltpu.VMEM((Hkv, dh, T), F32)],
        compiler_params=_cp("arbitrary"),
    )(sinks, qkv, qkv, qkv, qkv, qkv, bias, dot, ot)


def _split3(x):
    h1 = x.astype(BF16)
    r1 = x - h1.astype(F32)
    h2 = r1.astype(BF16)
    h3 = (r1 - h2.astype(F32)).astype(BF16)
    return h1, h2, h3


def _bias_expand(rel_t, onehot):
    Hq, NB = rel_t.shape
    L = onehot.shape[1]

    def body(r_ref, oh_ref, o_ref):
        h1, h2, h3 = _split3(r_ref[...])
        oh = oh_ref[...]
        o_ref[...] = _nn(h1, oh) + _nn(h2, oh) + _nn(h3, oh)

    return pl.pallas_call(
        body, name="bias_expand", grid=(1,),
        in_specs=[pl.BlockSpec((Hq, NB), lambda i: (0, 0)), pl.BlockSpec((NB, L), lambda i: (0, 0))],
        out_specs=pl.BlockSpec((Hq, L), lambda i: (0, 0)),
        out_shape=jax.ShapeDtypeStruct((Hq, L), F32),
        compiler_params=_cp("arbitrary"),
    )(rel_t, onehot)


def _bias_reduce(dbias, onehot):
    Hq, L = dbias.shape
    NB = onehot.shape[0]

    def body(d_ref, oh_ref, o_ref):
        h1, h2, h3 = _split3(d_ref[...])
        oh = oh_ref[...]
        o_ref[...] = _nt(h1, oh) + _nt(h2, oh) + _nt(h3, oh)

    return pl.pallas_call(
        body, name="bias_reduce", grid=(1,),
        in_specs=[pl.BlockSpec((Hq, L), lambda i: (0, 0)), pl.BlockSpec((NB, L), lambda i: (0, 0))],
        out_specs=pl.BlockSpec((Hq, NB), lambda i: (0, 0)),
        out_shape=jax.ShapeDtypeStruct((Hq, NB), F32),
        compiler_params=_cp("arbitrary"),
    )(dbias, onehot)


def _adamw(w, g, m, v, name):
    R, C = w.shape
    tr = 256 if R % 256 == 0 else R
    bc1 = 1.0 - ADAM_B1 ** ADAM_STEP
    bc2 = 1.0 - ADAM_B2 ** ADAM_STEP

    def body(w_ref, g_ref, m_ref, v_ref, d_ref, nm_ref, nv_ref):
        g = g_ref[...]
        m2 = ADAM_B1 * m_ref[...] + (1.0 - ADAM_B1) * g
        v2 = ADAM_B2 * v_ref[...] + (1.0 - ADAM_B2) * (g * g)
        nm_ref[...] = m2
        nv_ref[...] = v2
        d_ref[...] = -ADAM_LR * ((m2 / bc1) / (jnp.sqrt(v2 / bc2) + ADAM_EPS) + ADAM_WD * w_ref[...])

    spec = pl.BlockSpec((tr, C), lambda i: (i, 0))
    return pl.pallas_call(
        body, name=name, grid=(R // tr,),
        in_specs=[spec] * 4, out_specs=[spec] * 3,
        out_shape=[jax.ShapeDtypeStruct((R, C), F32)] * 3,
        compiler_params=_cp("arbitrary"),
    )(w, g, m, v)


def _row_tile(R):
    return max(t for t in range(16, 513, 16) if R % t == 0)


def _add_halves(mine, recv, name):
    K, R, C = mine.shape
    tr = _row_tile(R)

    def body(a_ref, b_ref, o_ref, ob_ref):
        s = a_ref[...].astype(F32) + b_ref[...].astype(F32)
        o_ref[...] = s
        ob_ref[...] = s.astype(BF16)

    spec = pl.BlockSpec((None, tr, C), lambda k, i: (k, i, 0))
    return pl.pallas_call(
        body, name=name, grid=(K, R // tr),
        in_specs=[spec, spec], out_specs=[spec, spec],
        out_shape=[jax.ShapeDtypeStruct((K, R, C), F32), jax.ShapeDtypeStruct((K, R, C), BF16)],
        compiler_params=_cp("arbitrary", "arbitrary"),
    )(mine, recv)


def _add_received(own, recv, name):
    R, C = own.shape
    tr = _row_tile(R)

    def body(a_ref, r_ref, o_ref):
        o_ref[...] = ((a_ref[...] + r_ref[0].astype(F32)) + r_ref[1].astype(F32)) + r_ref[2].astype(F32)

    return pl.pallas_call(
        body, name=name, grid=(R // tr,),
        in_specs=[pl.BlockSpec((tr, C), lambda i: (i, 0)), pl.BlockSpec((3, tr, C), lambda i: (0, i, 0))],
        out_specs=pl.BlockSpec((tr, C), lambda i: (i, 0)),
        out_shape=jax.ShapeDtypeStruct((R, C), F32),
        compiler_params=_cp("arbitrary"),
    )(own, recv)


def _position():
    x, y, c = lax.axis_index("x"), lax.axis_index("y"), lax.axis_index("c")
    others = [(1 - x, y), (x, 1 - y), (1 - x, 1 - y)]
    return x, y, c, others


def _remote(src, dst, send_sems, recv_sems, k, dev):
    return pltpu.make_async_remote_copy(src_ref=src, dst_ref=dst, send_sem=send_sems.at[k],
                                        recv_sem=recv_sems.at[k], device_id=dev, device_id_type=MESH_ID)


class _gather_exchange:
    def __init__(self, src, out, send_sems, recv_sems):
        x, y, c, others = _position()
        mine, sibling = 2 * x + y, (x, y, 1 - c)
        self.sends, self.arrivals, self.passes, self.from_sibling = [], [], [], []
        for j, (ox, oy) in enumerate(others):
            slot = out.at[2 * ox + oy, c]
            theirs = out.at[2 * ox + oy, 1 - c]
            self.sends.append(_remote(src.at[c], out.at[mine, c], send_sems, recv_sems, j, (ox, oy, c)))
            self.arrivals.append(_remote(slot, slot, send_sems, recv_sems, j, (ox, oy, c)))
            self.passes.append(_remote(slot, slot, send_sems, recv_sems, 3 + j, sibling))
            self.from_sibling.append(_remote(theirs, theirs, send_sems, recv_sems, 3 + j, sibling))

    def start(self):
        for cp in self.sends:
            cp.start()

    def forward(self):
        for arrived, onward in zip(self.arrivals, self.passes):
            arrived.wait_recv()
            onward.start()

    def finish(self):
        for cp in self.from_sibling:
            cp.wait_recv()
        for cp in self.sends + self.passes:
            cp.wait_send()


def _gather_weights(shard):
    R, C = shard.shape
    half = R // 2

    def body(src, out, send_sems, recv_sems):
        ex = _gather_exchange(src, out, send_sems, recv_sems)
        ex.start()
        ex.forward()
        ex.finish()

    return pl.pallas_call(
        body, name="gather_weights",
        in_specs=[ANY], out_specs=ANY,
        out_shape=jax.ShapeDtypeStruct((N_CHIPS, 2, half, C), shard.dtype),
        scratch_shapes=[pltpu.SemaphoreType.DMA((6,)), pltpu.SemaphoreType.DMA((6,))],
    )(shard.reshape(2, half, C)).reshape(N_CHIPS, R, C)


def _swap_halves(grads, name):
    K, R, C = grads.shape
    half = R // 2

    def body(src, out, send_sems, recv_sems):
        x, y, c, _ = _position()
        theirs = src.at[:, pl.ds(pl.multiple_of((1 - c) * half, 16), half), :]
        cp = _remote(theirs, out, send_sems, recv_sems, 0, (x, y, 1 - c))
        cp.start()
        cp.wait()

    return pl.pallas_call(
        body, name=name,
        in_specs=[ANY], out_specs=ANY,
        out_shape=jax.ShapeDtypeStruct((K, half, C), grads.dtype),
        scratch_shapes=[pltpu.SemaphoreType.DMA((1,)), pltpu.SemaphoreType.DMA((1,))],
    )(grads)


class _scatter_exchange:
    def __init__(self, src, out, send_sems, recv_sems):
        x, y, c, others = _position()
        self.copies = [_remote(src.at[2 * ox + oy], out.at[j], send_sems, recv_sems, j, (ox, oy, c))
                       for j, (ox, oy) in enumerate(others)]

    def start(self):
        for cp in self.copies:
            cp.start()

    def finish(self):
        for cp in self.copies:
            cp.wait()


def _scatter_to_owners(parts, name):
    K, H, C = parts.shape

    def body(src, out, send_sems, recv_sems):
        ex = _scatter_exchange(src, out, send_sems, recv_sems)
        ex.start()
        ex.finish()

    return pl.pallas_call(
        body, name=name,
        in_specs=[ANY], out_specs=ANY,
        out_shape=jax.ShapeDtypeStruct((3, H, C), parts.dtype),
        scratch_shapes=[pltpu.SemaphoreType.DMA((3,)), pltpu.SemaphoreType.DMA((3,))],
    )(parts)


def _swap_reduced(half_rows, name):
    H, C = half_rows.shape

    def body(src, out, send_sems, recv_sems):
        x, y, c, _ = _position()
        cp = _remote(src, out, send_sems, recv_sems, 0, (x, y, 1 - c))
        cp.start()
        cp.wait()

    return pl.pallas_call(
        body, name=name,
        in_specs=[ANY], out_specs=ANY,
        out_shape=jax.ShapeDtypeStruct((H, C), half_rows.dtype),
        scratch_shapes=[pltpu.SemaphoreType.DMA((1,)), pltpu.SemaphoreType.DMA((1,))],
    )(half_rows)


def _allreduce_small(block):
    R, C = block.shape
    n_dev = 8

    def body(src, out, slots, send_sems, recv_sems):
        x, y, c, _ = _position()
        me = 4 * x + 2 * y + c
        slots[me] = src[...]
        sends = []
        for r in range(1, n_dev):
            peer = (x ^ (r >> 2), y ^ ((r >> 1) & 1), c ^ (r & 1))
            cp = _remote(src, slots.at[me], send_sems, recv_sems, r - 1, peer)
            cp.start()
            sends.append(cp)
        for r in range(1, n_dev):
            theirs = slots.at[me ^ r]
            _remote(theirs, theirs, send_sems, recv_sems, r - 1, (x, y, c)).wait_recv()
        for cp in sends:
            cp.wait_send()
        acc = slots[0]
        for d in range(1, n_dev):
            acc = acc + slots[d]
        out[...] = acc

    return pl.pallas_call(
        body, name="allreduce_small",
        in_specs=[pl.BlockSpec(memory_space=pltpu.VMEM)], out_specs=pl.BlockSpec(memory_space=pltpu.VMEM),
        out_shape=jax.ShapeDtypeStruct((R, C), F32),
        scratch_shapes=[pltpu.VMEM((n_dev, R, C), F32), pltpu.SemaphoreType.DMA((7,)), pltpu.SemaphoreType.DMA((7,))],
    )(block)


def _rel_bucket(dist):
    max_exact = REL_BUCKETS // 2
    d = jnp.maximum(dist, 1).astype(F32)
    large = max_exact + (jnp.log(d / max_exact) / math.log(REL_MAX_DIST / max_exact)
                         * (REL_BUCKETS - max_exact)).astype(jnp.int32)
    large = jnp.minimum(large, REL_BUCKETS - 1)
    return jnp.where(dist < max_exact, dist, large)


def _bucket_onehot():
    T = SWA_BLOCK
    dist = (jnp.arange(T)[None, :] + T) - jnp.arange(2 * T)[:, None]
    bucket = _rel_bucket(jnp.maximum(dist, 0)).reshape(1, T * 2 * T)
    return (bucket == jnp.arange(REL_BUCKETS)[:, None]).astype(BF16)


_BUF = (("ffn1_w1", "t"), ("ffn1_w3", "t"), ("ffn1_w2", "n"), ("ffn2_w1", "t"), ("ffn2_w3", "t"),
        ("ffn2_w2", "n"), ("w_in", "t"), ("w_out", "n"), ("w_branch_swa", "tw"), ("w_branch_sb", "tw"))


def _to_rows(name_kind, w, D):
    kind = name_kind[1]
    if kind == "n":
        return w
    if kind == "t":
        return w.T
    return w.T.reshape(-1, D)


def _from_rows(name_kind, rows, width):
    kind = name_kind[1]
    if kind == "n":
        return rows
    if kind == "t":
        return rows.T
    return rows.reshape(-1, width).T


def kernel(x, norm_ffn1, ffn1_w1, ffn1_w3, ffn1_w2, norm_mix, w_in, swa_sinks, rel_bias, w_branch_swa, w_branch_sb, w_out, norm_ffn2, ffn2_w1, ffn2_w3, ffn2_w2, norm_final, loss_target, m_norm_ffn1, m_ffn1_w1, m_ffn1_w3, m_ffn1_w2, m_norm_mix, m_w_in, m_swa_sinks, m_rel_bias, m_w_branch_swa, m_w_branch_sb, m_w_out, m_norm_ffn2, m_ffn2_w1, m_ffn2_w3, m_ffn2_w2, m_norm_final, v_norm_ffn1, v_ffn1_w1, v_ffn1_w3, v_ffn1_w2, v_norm_mix, v_w_in, v_swa_sinks, v_rel_bias, v_w_branch_swa, v_w_branch_sb, v_w_out, v_norm_ffn2, v_ffn2_w1, v_ffn2_w3, v_ffn2_w2, v_norm_final):
    names = ["norm_ffn1", "ffn1_w1", "ffn1_w3", "ffn1_w2", "norm_mix", "w_in", "swa_sinks", "rel_bias",
             "w_branch_swa", "w_branch_sb", "w_out", "norm_ffn2", "ffn2_w1", "ffn2_w3", "ffn2_w2", "norm_final"]
    W = dict(zip(names, [norm_ffn1, ffn1_w1, ffn1_w3, ffn1_w2, norm_mix, w_in, swa_sinks, rel_bias,
                         w_branch_swa, w_branch_sb, w_out, norm_ffn2, ffn2_w1, ffn2_w3, ffn2_w2, norm_final]))
    M = dict(zip(names, [m_norm_ffn1, m_ffn1_w1, m_ffn1_w3, m_ffn1_w2, m_norm_mix, m_w_in, m_swa_sinks, m_rel_bias,
                         m_w_branch_swa, m_w_branch_sb, m_w_out, m_norm_ffn2, m_ffn2_w1, m_ffn2_w3, m_ffn2_w2,
                         m_norm_final]))
    V = dict(zip(names, [v_norm_ffn1, v_ffn1_w1, v_ffn1_w3, v_ffn1_w2, v_norm_mix, v_w_in, v_swa_sinks, v_rel_bias,
                         v_w_branch_swa, v_w_branch_sb, v_w_out, v_norm_ffn2, v_ffn2_w1, v_ffn2_w3, v_ffn2_w2,
                         v_norm_final]))
    xs = x[0]
    target = loss_target[0]
    S, D = xs.shape
    QW = SWA_Q_HEADS * HEAD_DIM
    KW = SWA_KV_HEADS * HEAD_DIM
    BW = SB_HEADS * HEAD_DIM
    QKV = QW + 2 * KW + 3 * BW

    pieces = [_to_rows(nk, W[nk[0]][0], D) for nk in _BUF]
    sizes = [p.shape[0] for p in pieces]
    offs = [0]
    for s in sizes:
        offs.append(offs[-1] + s)
    n_first = 3
    first_rows = offs[n_first]
    shard_a = jnp.concatenate(pieces[:n_first], axis=0).astype(BF16)
    shard_b = jnp.concatenate(pieces[n_first:], axis=0).astype(BF16)
    chip = 2 * lax.axis_index("x") + lax.axis_index("y")
    gathered_a = lax.dynamic_update_slice(_gather_weights(shard_a), shard_a[None], (chip, 0, 0))
    f1w1, f1w3, f1w2 = [gathered_a[:, offs[i]:offs[i + 1], :].reshape(N_CHIPS * sizes[i], D) for i in range(n_first)]

    g1, gmix, g3 = W["norm_ffn1"], W["norm_mix"], W["norm_ffn2"]
    gf = W["norm_final"].reshape(1, D)

    x1, h1, a1, b1, gathered_b = _ffn_fwd(xs, g1, f1w1, f1w3, f1w2, "ffn1_fwd", gather=shard_b)
    gathered_b = lax.dynamic_update_slice(gathered_b, shard_b[None], (chip, 0, 0))

    def full(i):
        return gathered_b[:, offs[i] - first_rows:offs[i + 1] - first_rows, :].reshape(N_CHIPS * sizes[i], D)

    f2w1, f2w3, f2w2, w_in_t, w_out_f = [full(i) for i in range(n_first, 8)]
    wa_t = full(8).reshape(D, QW)
    wb_t = full(9).reshape(D, BW)
    o0 = QW + 2 * KW
    rows = jnp.arange(w_in_t.shape[0])
    is_q = (rows < QW) | ((rows >= o0) & (rows < o0 + BW))
    w_in_s = w_in_t * jnp.where(is_q, QK_SCALE, 1.0).astype(BF16)[:, None]
    qkv_a = _norm_proj_heads(x1, gmix, w_in_s[:o0], SWA_BLOCK, "proj_swa")
    qkv_b = _norm_proj_heads(x1, gmix, w_in_s[o0:QKV], SB_BLOCK, "proj_sb")
    gates, h2 = _norm_matmul_nt(x1, gmix, w_in_t[QKV:], F32, "proj_gates")

    onehot = _bucket_onehot()
    bias = _bias_expand(W["rel_bias"].T, onehot).reshape(SWA_Q_HEADS, 2 * SWA_BLOCK, SWA_BLOCK)
    sinks = W["swa_sinks"].reshape(SWA_Q_HEADS)
    oa_t = _swa_fwd(qkv_a, bias, sinks)
    ob_t, saved_sb = _sb_fwd(qkv_b)

    x2, merged, ba, bb = _merge_fwd(x1, gates, oa_t, ob_t, wa_t, wb_t, w_out_f)
    x3, h3, a2, b2 = _ffn_fwd(x2, g3, f2w1, f2w3, f2w2, "ffn2_fwd")
    loss_part, dx3, dgf = _final_loss(x3, gf, target)

    dx2, dg3, dz2, da2, db2, u2 = _ffn_bwd(dx3, x2, g3, a2, b2, f2w1, f2w3, f2w2, "ffn2_bwd")
    grads = {}
    grads["ffn2_w1"] = _tn_matmul(da2, h3, "ffn2_dw1")
    grads["ffn2_w3"] = _tn_matmul(db2, h3, "ffn2_dw3")
    grads["ffn2_w2"] = _tn_matmul(u2, dz2, "ffn2_dw2")

    dx2b, dba, dbb, dgates, doa_t, dob_t = _merge_bwd(dx2, gates, ba, bb, wa_t.T, wb_t.T, w_out_f,
                                                      SWA_BLOCK, SB_BLOCK)
    grads["w_out"] = _tn_matmul(merged, dx2b, "dw_out")
    grads["w_branch_swa"] = _heads_matmul([(oa_t, 1.0)], dba, "dw_branch_swa").T
    grads["w_branch_sb"] = _heads_matmul([(ob_t, 1.0)], dbb, "dw_branch_sb").T

    dqb_t, dkb_t, dvb_t = _sb_bwd(qkv_b, dob_t, saved_sb)
    dqa_t, dka_t, dva_t, dbias, dsink_rows = _swa_bwd(qkv_a, bias, sinks, doa_t, oa_t)
    d_rel = _bias_reduce(dbias.reshape(SWA_Q_HEADS, -1), onehot).T
    d_sinks = jnp.sum(dsink_rows, axis=(1, 2))

    dheads = [(dqa_t, QK_SCALE), (dka_t, 1.0), (dva_t, 1.0), (dqb_t, QK_SCALE), (dkb_t, 1.0), (dvb_t, 1.0)]
    grads["w_in"] = jnp.concatenate([_heads_matmul(dheads, h2, "dw_in_heads").astype(BF16),
                                     _tn_matmul(dgates, h2, "dw_in_gates")], axis=0)
    row0, pieces_in = 0, []
    for a, _ in dheads:
        pieces_in.append((a, row0))
        row0 += a.shape[0] * HEAD_DIM
    dx1, dgmix = _proj_bwd(pieces_in, dgates, w_in_s, x1, gmix, dx2)

    c = lax.axis_index("c")

    def reduce_start(lo, hi, tag):
        gbuf = jnp.concatenate([grads[_BUF[i][0]].astype(BF16).reshape(N_CHIPS, sizes[i], D) for i in range(lo, hi)],
                               axis=1)
        half = gbuf.shape[1] // 2
        from_sibling = _swap_halves(gbuf, "swap_halves_" + tag)
        my_half = lax.dynamic_slice_in_dim(gbuf, c * half, half, axis=1)
        return _add_halves(my_half, from_sibling, "add_sibling_" + tag)

    def reduce_finish(chip_sum, received, tag):
        own = lax.dynamic_index_in_dim(chip_sum, chip, axis=0, keepdims=False)
        my_rows = _add_received(own, received, "add_chips_" + tag)
        their_rows = _swap_reduced(my_rows, "swap_reduced_" + tag)
        return jnp.concatenate([jnp.where(c == 0, my_rows, their_rows), jnp.where(c == 0, their_rows, my_rows)],
                               axis=0)

    sum_b, sum16_b = reduce_start(n_first, len(_BUF), "late")
    dx0, dg1, dz1, da1, db1, u1, received_b = _ffn_bwd(dx1, xs, g1, a1, b1, f1w1, f1w3, f1w2, "ffn1_bwd",
                                                       scatter=sum16_b)
    grads["ffn1_w1"] = _tn_matmul(da1, h1, "ffn1_dw1")
    grads["ffn1_w3"] = _tn_matmul(db1, h1, "ffn1_dw3")
    grads["ffn1_w2"] = _tn_matmul(u1, dz1, "ffn1_dw2")
    sum_a, sum16_a = reduce_start(0, n_first, "first")
    reduced = jnp.concatenate([reduce_finish(sum_a, _scatter_to_owners(sum16_a, "scatter_to_owners"), "first"),
                               reduce_finish(sum_b, received_b, "late")], axis=0)

    small_rows = [dg1, dgmix, dg3, dgf,
                  jnp.pad(d_sinks.reshape(1, -1), ((0, 0), (0, D - SWA_Q_HEADS))),
                  jnp.pad(d_rel.reshape(1, -1), ((0, 0), (0, D - REL_BUCKETS * SWA_Q_HEADS))),
                  jnp.pad(loss_part, ((0, 0), (0, D - 1))), jnp.zeros((1, D), F32)]
    small = _allreduce_small(jnp.concatenate(small_rows, axis=0))
    loss = small[6, 0]

    G, g_rows = {}, {}
    for i, nk in enumerate(_BUF):
        rows = reduced[offs[i]:offs[i + 1]]
        g_rows[nk[0]] = rows.reshape(-1, W[nk[0]].shape[1]) if nk[1] == "tw" else rows
        G[nk[0]] = _from_rows(nk, rows, W[nk[0]].shape[1])[None]
    G["norm_ffn1"], G["norm_mix"], G["norm_ffn2"] = small[0:1], small[1:2], small[2:3]
    G["norm_final"] = small[3]
    G["swa_sinks"] = small[4:5, :SWA_Q_HEADS]
    G["rel_bias"] = small[5, :REL_BUCKETS * SWA_Q_HEADS].reshape(REL_BUCKETS, SWA_Q_HEADS)

    delta, new_m, new_v = {}, {}, {}
    small_names = ["norm_ffn1", "norm_mix", "norm_ffn2", "norm_final", "swa_sinks", "rel_bias"]

    def pack(d):
        return jnp.concatenate([jnp.pad(d[n].reshape(1, -1), ((0, 0), (0, D - d[n].size))) for n in small_names]
                               + [jnp.zeros((2, D), F32)], axis=0)

    sd, sm, sv = _adamw(pack(W), pack(G), pack(M), pack(V), "adamw_small")
    for r, n in enumerate(small_names):
        for dst, src in ((delta, sd), (new_m, sm), (new_v, sv)):
            dst[n] = src[r, :W[n].size].reshape(W[n].shape)
    for n, kind in _BUF:
        turn = (lambda a: a) if kind == "n" else (lambda a: a.T)
        d_, m_, v_ = _adamw(turn(W[n][0]), g_rows[n], turn(M[n][0]), turn(V[n][0]), "adamw_" + n)
        delta[n], new_m[n], new_v[n] = turn(d_)[None], turn(m_)[None], turn(v_)[None]

    return (loss, dx0[None], *[G[n] for n in names], *[delta[n] for n in names],
            *[new_m[n] for n in names], *[new_v[n] for n in names])
```

```python
import math

import jax
import jax.numpy as jnp
from jax import lax
from jax.experimental import pallas as pl
from jax.experimental.pallas import tpu as pltpu

F32, BF16 = jnp.float32, jnp.bfloat16
MESH_ID = pl.DeviceIdType.MESH
ANY = pl.BlockSpec(memory_space=pl.ANY)

RMS_EPS = 1e-6
HEAD_DIM = 64
SWA_Q_HEADS, SWA_KV_HEADS, SWA_GROUP = 8, 2, 4
SWA_BLOCK = 128
SB_HEADS = 8
SB_BLOCK = 256
REL_BUCKETS, REL_MAX_DIST = 32, 128
NEG_BIG = -1e30
QK_SCALE = HEAD_DIM ** -0.5
ADAM_LR, ADAM_B1, ADAM_B2, ADAM_EPS, ADAM_WD, ADAM_STEP = 0.001, 0.9, 0.999, 1e-08, 0.01, 10

N_CHIPS = 4
TOKEN_TILE = 512
MATMUL_TOKEN_TILE = 1024
WGRAD_ROW_TILES = (2176, 1408, 1024, 256)
WGRAD_TOKEN_TILE = 2048
FF_TILE = 2816
FFN_TOKEN_TILE = 512
PROJ_HEAD_ROWS = 768
FF_BWD_TILE = 1408
VMEM_LIMIT = 60 * 1024 * 1024


def _cp(*sem):
    return pltpu.CompilerParams(dimension_semantics=sem, vmem_limit_bytes=VMEM_LIMIT)


def _nn(a, b):
    return jnp.dot(a, b, preferred_element_type=F32)


def _nt(a, b):
    return lax.dot_general(a, b, (((1,), (1,)), ((), ())), preferred_element_type=F32)


def _tn(a, b):
    return lax.dot_general(a, b, (((0,), (0,)), ((), ())), preferred_element_type=F32)


def _norm_fwd(x, g):
    return x * lax.rsqrt(jnp.mean(x * x, axis=-1, keepdims=True) + RMS_EPS) * g


def _norm_bwd(x, g, dh):
    r = lax.rsqrt(jnp.mean(x * x, axis=-1, keepdims=True) + RMS_EPS)
    xh = x * r
    dxh = dh * g
    dx = r * (dxh - xh * jnp.mean(dxh * xh, axis=-1, keepdims=True))
    return dx, jnp.sum(dh * xh, axis=0, keepdims=True)


SOFTPLUS_LINEAR = 20.0


def _softplus(z):
    return jnp.maximum(jnp.log(1.0 + jnp.exp(jnp.minimum(z, SOFTPLUS_LINEAR))), z)


def _ffn_fwd(x, g, w1t, w3t, w2, name, gather=None):
    S, D = x.shape
    F = w2.shape[0]
    tm, tf = min(FFN_TOKEN_TILE, S), FF_TILE
    ni, nj = S // tm, F // tf

    def body(x_ref, g_ref, w1_ref, w3_ref, w2_ref, *rest):
        if gather is None:
            xo_ref, h_ref, a_ref, b_ref, hs, acc = rest
        else:
            shard_ref, xo_ref, h_ref, a_ref, b_ref, gathered_ref, hs, acc, send_sems, recv_sems = rest
        i, j = pl.program_id(0), pl.program_id(1)
        if gather is not None:
            for when, phase in ((jnp.logical_and(i == 0, j == 0), "start"),
                                (jnp.logical_and(i == ni - 1, j == 0), "forward"),
                                (jnp.logical_and(i == ni - 1, j == nj - 1), "finish")):
                @pl.when(when)
                def _():
                    getattr(_gather_exchange(shard_ref, gathered_ref, send_sems, recv_sems), phase)()

        @pl.when(j == 0)
        def _():
            hb = _norm_fwd(x_ref[...], g_ref[...]).astype(BF16)
            hs[...] = hb
            h_ref[...] = hb
            acc[...] = jnp.zeros_like(acc)

        h = hs[...]
        a = _nt(h, w1_ref[...])
        b = _nt(h, w3_ref[...])
        a_ref[...] = a.astype(BF16)
        b_ref[...] = b.astype(BF16)
        u = a * jax.nn.sigmoid(a) * b
        acc[...] += _nn(u.astype(BF16), w2_ref[...])

        @pl.when(j == nj - 1)
        def _():
            xo_ref[...] = x_ref[...] + 0.5 * acc[...]

    in_specs = [pl.BlockSpec((tm, D), lambda i, j: (i, 0)),
                pl.BlockSpec((1, D), lambda i, j: (0, 0)),
                pl.BlockSpec((tf, D), lambda i, j: (j, 0), pipeline_mode=pl.Buffered(1)),
                pl.BlockSpec((tf, D), lambda i, j: (j, 0), pipeline_mode=pl.Buffered(1)),
                pl.BlockSpec((tf, D), lambda i, j: (j, 0), pipeline_mode=pl.Buffered(1))]
    out_specs = [pl.BlockSpec((tm, D), lambda i, j: (i, 0)),
                 pl.BlockSpec((tm, D), lambda i, j: (i, 0)),
                 pl.BlockSpec((tm, tf), lambda i, j: (i, j)),
                 pl.BlockSpec((tm, tf), lambda i, j: (i, j))]
    out_shape = [jax.ShapeDtypeStruct((S, D), F32), jax.ShapeDtypeStruct((S, D), BF16),
                 jax.ShapeDtypeStruct((S, F), BF16), jax.ShapeDtypeStruct((S, F), BF16)]
    scratch = [pltpu.VMEM((tm, D), BF16), pltpu.VMEM((tm, D), F32)]
    operands = [x, g, w1t, w3t, w2]
    if gather is not None:
        R, C = gather.shape
        in_specs.append(ANY)
        out_specs.append(ANY)
        out_shape.append(jax.ShapeDtypeStruct((N_CHIPS, 2, R // 2, C), gather.dtype))
        scratch += [pltpu.SemaphoreType.DMA((6,)), pltpu.SemaphoreType.DMA((6,))]
        operands.append(gather.reshape(2, R // 2, C))
    outs = list(pl.pallas_call(
        body, name=name, grid=(ni, nj), in_specs=in_specs, out_specs=out_specs, out_shape=out_shape,
        scratch_shapes=scratch, compiler_params=_cp("arbitrary", "arbitrary"),
    )(*operands))
    if gather is not None:
        outs[4] = outs[4].reshape(N_CHIPS, R, C)
    return outs


def _ffn_bwd(dxo, x, g, a, b, w1t, w3t, w2, name, scatter=None):
    S, D = x.shape
    F = w2.shape[0]
    tm, tf = min(FFN_TOKEN_TILE, S), FF_BWD_TILE
    ni, nj = S // tm, F // tf

    def body(dxo_ref, x_ref, g_ref, a_ref, b_ref, w1_ref, w3_ref, w2_ref, *rest):
        if scatter is None:
            dx_ref, dg_ref, dz_ref, da_ref, db_ref, u_ref, dzs, acc = rest
        else:
            (parts_ref, dx_ref, dg_ref, dz_ref, da_ref, db_ref, u_ref, recv_ref,
             dzs, acc, send_sems, recv_sems) = rest
        i, j = pl.program_id(0), pl.program_id(1)
        if scatter is not None:
            for when, phase in ((jnp.logical_and(i == 0, j == 0), "start"),
                                (jnp.logical_and(i == ni - 1, j == nj - 1), "finish")):
                @pl.when(when)
                def _():
                    getattr(_scatter_exchange(parts_ref, recv_ref, send_sems, recv_sems), phase)()

        @pl.when(j == 0)
        def _():
            dzb = (0.5 * dxo_ref[...]).astype(BF16)
            dzs[...] = dzb
            dz_ref[...] = dzb
            acc[...] = jnp.zeros_like(acc)

        du = _nt(dzs[...], w2_ref[...])
        av = a_ref[...].astype(F32)
        bv = b_ref[...].astype(F32)
        s = jax.nn.sigmoid(av)
        silu = av * s
        db = (du * silu).astype(BF16)
        da = (du * bv * (s * (1.0 + av * (1.0 - s)))).astype(BF16)
        da_ref[...] = da
        db_ref[...] = db
        u_ref[...] = (silu * bv).astype(BF16)
        acc[...] += _nn(da, w1_ref[...]) + _nn(db, w3_ref[...])

        @pl.when(j == nj - 1)
        def _():
            dx, dg = _norm_bwd(x_ref[...], g_ref[...], acc[...])
            dx_ref[...] = dxo_ref[...] + dx

            @pl.when(i == 0)
            def _():
                dg_ref[...] = dg

            @pl.when(i > 0)
            def _():
                dg_ref[...] += dg

    row = pl.BlockSpec((tm, D), lambda i, j: (i, 0))
    wsp = pl.BlockSpec((tf, D), lambda i, j: (j, 0))
    col = pl.BlockSpec((tm, tf), lambda i, j: (i, j))
    vec = pl.BlockSpec((1, D), lambda i, j: (0, 0))
    col_out = pl.BlockSpec((tm, tf), lambda i, j: (i, j))
    in_specs = [row, row, vec, col, col, wsp, wsp, wsp]
    out_specs = [row, vec, row, col_out, col_out, col_out]
    out_shape = [jax.ShapeDtypeStruct((S, D), F32), jax.ShapeDtypeStruct((1, D), F32),
                 jax.ShapeDtypeStruct((S, D), BF16), jax.ShapeDtypeStruct((S, F), BF16),
                 jax.ShapeDtypeStruct((S, F), BF16), jax.ShapeDtypeStruct((S, F), BF16)]
    scratch = [pltpu.VMEM((tm, D), BF16), pltpu.VMEM((tm, D), F32)]
    operands = [dxo, x, g, a, b, w1t, w3t, w2]
    if scatter is not None:
        in_specs.append(ANY)
        out_specs.append(ANY)
        out_shape.append(jax.ShapeDtypeStruct((3,) + scatter.shape[1:], scatter.dtype))
        scratch += [pltpu.SemaphoreType.DMA((3,)), pltpu.SemaphoreType.DMA((3,))]
        operands.append(scatter)
    return pl.pallas_call(
        body, name=name, grid=(ni, nj), in_specs=in_specs, out_specs=out_specs, out_shape=out_shape,
        scratch_shapes=scratch, compiler_params=_cp("arbitrary", "arbitrary"),
    )(*operands)


def _tn_matmul(a, b, name):
    S, M = a.shape
    N = b.shape[1]
    ts = min(WGRAD_TOKEN_TILE, S)
    tmm = next(t for t in WGRAD_ROW_TILES if M % t == 0)
    ns = S // ts

    def body(a_ref, b_ref, o_ref, acc):
        s = pl.program_id(1)
        part = _tn(a_ref[...], b_ref[...])

        @pl.when(s == 0)
        def _():
            acc[...] = part

        @pl.when(s > 0)
        def _():
            acc[...] += part

        @pl.when(s == ns - 1)
        def _():
            o_ref[...] = acc[...].astype(BF16)

    return pl.pallas_call(
        body, name=name, grid=(M // tmm, ns),
        in_specs=[pl.BlockSpec((ts, tmm), lambda m, s: (s, m)),
                  pl.BlockSpec((ts, N), lambda m, s: (s, 0))],
        out_specs=pl.BlockSpec((tmm, N), lambda m, s: (m, 0)),
        out_shape=jax.ShapeDtypeStruct((M, N), BF16),
        scratch_shapes=[pltpu.VMEM((tmm, N), F32)],
        compiler_params=_cp("arbitrary", "arbitrary"),
    )(a, b)


def _norm_matmul_nt(x, g, wt, out_dtype, name):
    S, D = x.shape
    N = wt.shape[0]
    tm = min(MATMUL_TOKEN_TILE, S)
    tn = next(t for t in (1024, 768, 256) if N % t == 0)

    def body(x_ref, g_ref, w_ref, o_ref, h_ref, hs):
        @pl.when(pl.program_id(1) == 0)
        def _():
            hb = _norm_fwd(x_ref[...], g_ref[...]).astype(BF16)
            hs[...] = hb
            h_ref[...] = hb

        o_ref[...] = _nt(hs[...], w_ref[...]).astype(out_dtype)

    return pl.pallas_call(
        body, name=name, grid=(S // tm, N // tn),
        in_specs=[pl.BlockSpec((tm, D), lambda i, j: (i, 0)),
                  pl.BlockSpec((1, D), lambda i, j: (0, 0)),
                  pl.BlockSpec((tn, D), lambda i, j: (j, 0))],
        out_specs=[pl.BlockSpec((tm, tn), lambda i, j: (i, j)),
                   pl.BlockSpec((tm, D), lambda i, j: (i, 0))],
        out_shape=[jax.ShapeDtypeStruct((S, N), out_dtype), jax.ShapeDtypeStruct((S, D), BF16)],
        scratch_shapes=[pltpu.VMEM((tm, D), BF16)],
        compiler_params=_cp("arbitrary", "arbitrary"),
    )(x, g, wt)


def _heads_tile(ref):
    Hh, nbk = ref.shape[0], ref.shape[1]
    return jnp.concatenate([jnp.concatenate([ref[h, b] for b in range(nbk)], axis=1) for h in range(Hh)], axis=0)


def _store_heads(ref, val):
    Hh, nbk, dh, T = ref.shape
    for h in range(Hh):
        for b in range(nbk):
            ref[h, b] = val[h * dh:(h + 1) * dh, b * T:(b + 1) * T].astype(ref.dtype)


def _norm_proj_heads(x, g, w_rows, T, name):
    S, D = x.shape
    N = w_rows.shape[0]
    tm, tn = min(MATMUL_TOKEN_TILE, S), PROJ_HEAD_ROWS

    def body(x_ref, g_ref, w_ref, o_ref, hs):
        @pl.when(pl.program_id(1) == 0)
        def _():
            hs[...] = _norm_fwd(x_ref[...], g_ref[...]).astype(BF16)

        _store_heads(o_ref, _nt(w_ref[...], hs[...]))

    return pl.pallas_call(
        body, name=name, grid=(S // tm, N // tn),
        in_specs=[pl.BlockSpec((tm, D), lambda i, j: (i, 0)),
                  pl.BlockSpec((1, D), lambda i, j: (0, 0)),
                  pl.BlockSpec((tn, D), lambda i, j: (j, 0))],
        out_specs=pl.BlockSpec((tn // HEAD_DIM, tm // T, HEAD_DIM, T), lambda i, j: (j, i, 0, 0)),
        out_shape=jax.ShapeDtypeStruct((N // HEAD_DIM, S // T, HEAD_DIM, T), BF16),
        scratch_shapes=[pltpu.VMEM((tm, D), BF16)],
        compiler_params=_cp("arbitrary", "arbitrary"),
    )(x, g, w_rows)


def _heads_matmul(pieces, b, name):
    S, N = b.shape
    ts = min(MATMUL_TOKEN_TILE, S)
    ns = S // ts
    rows = [at.shape[0] * at.shape[2] for at, _ in pieces]

    def body(*refs):
        a_refs, b_ref, o_ref = refs[:-2], refs[-2], refs[-1]
        s = pl.program_id(0)
        row0 = 0
        for a_ref, (_, scale), n in zip(a_refs, pieces, rows):
            a = _heads_tile(a_ref)
            part = _nn((a if scale == 1.0 else a * scale).astype(BF16), b_ref[...])
            out = o_ref.at[row0:row0 + n, :]
            row0 += n

            @pl.when(s == 0)
            def _():
                out[...] = part

            @pl.when(s > 0)
            def _():
                out[...] += part

    return pl.pallas_call(
        body, name=name, grid=(ns,),
        in_specs=[pl.BlockSpec((at.shape[0], ts // at.shape[3], at.shape[2], at.shape[3]), lambda s: (0, s, 0, 0))
                  for at, _ in pieces] + [pl.BlockSpec((ts, N), lambda s: (s, 0))],
        out_specs=pl.BlockSpec((sum(rows), N), lambda s: (0, 0)),
        out_shape=jax.ShapeDtypeStruct((sum(rows), N), F32),
        compiler_params=_cp("arbitrary"),
    )(*[at for at, _ in pieces], b)


def _proj_bwd(pieces, dgates, w_rows, x, g, dres):
    S, D = x.shape
    tm = min(TOKEN_TILE, S)
    n_p = len(pieces)
    gate_row = w_rows.shape[0] - dgates.shape[1]

    def body(*refs):
        p_refs = refs[:n_p]
        dgt_ref, w_ref, x_ref, g_ref, dres_ref, dx_ref, dg_ref = refs[n_p:]
        i = pl.program_id(0)
        dh = _nn(dgt_ref[...], w_ref[gate_row:, :])
        for p_ref, (arr, row0) in zip(p_refs, pieces):
            rows = arr.shape[0] * arr.shape[2]
            dh += _tn(_heads_tile(p_ref).astype(BF16), w_ref[row0:row0 + rows, :])
        dx, dg = _norm_bwd(x_ref[...], g_ref[...], dh)
        dx_ref[...] = dres_ref[...] + dx

        @pl.when(i == 0)
        def _():
            dg_ref[...] = dg

        @pl.when(i > 0)
        def _():
            dg_ref[...] += dg

    row = pl.BlockSpec((tm, D), lambda i: (i, 0))
    vec = pl.BlockSpec((1, D), lambda i: (0, 0))
    p_specs = [pl.BlockSpec((a.shape[0], tm // a.shape[3], a.shape[2], a.shape[3]), lambda i: (0, i, 0, 0))
               for a, _ in pieces]
    return pl.pallas_call(
        body, name="proj_bwd", grid=(S // tm,),
        in_specs=p_specs + [pl.BlockSpec((tm, dgates.shape[1]), lambda i: (i, 0)),
                            pl.BlockSpec(w_rows.shape, lambda i: (0, 0), pipeline_mode=pl.Buffered(1)),
                            row, vec, row],
        out_specs=[row, vec],
        out_shape=[jax.ShapeDtypeStruct((S, D), F32), jax.ShapeDtypeStruct((1, D), F32)],
        compiler_params=_cp("arbitrary"),
    )(*[a for a, _ in pieces], dgates, w_rows, x, g, dres)


def _merge_fwd(x1, gates, oa_t, ob_t, wat, wbt, w_out):
    S, D = x1.shape
    W = wat.shape[1]
    tm = min(TOKEN_TILE, S)

    def body(x_ref, ga_ref, gb_ref, oa_ref, ob_ref, wa_ref, wb_ref, wo_ref,
             x2_ref, mg_ref, ba_ref, bb_ref):
        ba = _nt(_heads_tile(oa_ref).T.astype(BF16), wa_ref[...])
        bb = _nt(_heads_tile(ob_ref).T.astype(BF16), wb_ref[...])
        merged = jax.nn.sigmoid(ga_ref[...]) * ba + jax.nn.sigmoid(gb_ref[...]) * bb
        mb = merged.astype(BF16)
        mg_ref[...] = mb
        ba_ref[...] = ba.astype(BF16)
        bb_ref[...] = bb.astype(BF16)
        x2_ref[...] = x_ref[...] + _nn(mb, wo_ref[...])

    row = pl.BlockSpec((tm, D), lambda i: (i, 0))
    full = lambda r, c: pl.BlockSpec((r, c), lambda i: (0, 0))
    heads = lambda a: pl.BlockSpec((a.shape[0], tm // a.shape[3], a.shape[2], a.shape[3]), lambda i: (0, i, 0, 0))
    return pl.pallas_call(
        body, name="merge_fwd", grid=(S // tm,),
        in_specs=[row, pl.BlockSpec((tm, D), lambda i: (i, 0)), pl.BlockSpec((tm, D), lambda i: (i, 1)),
                  heads(oa_t), heads(ob_t), full(D, W), full(D, W), full(D, D)],
        out_specs=[row, row, row, row],
        out_shape=[jax.ShapeDtypeStruct((S, D), F32)] + [jax.ShapeDtypeStruct((S, D), BF16)] * 3,
        compiler_params=_cp("arbitrary"),
    )(x1, gates, gates, oa_t, ob_t, wat, wbt, w_out)


def _merge_bwd(dx2, gates, ba, bb, wa, wb, w_out, t_a, t_b):
    S, D = dx2.shape
    W = wa.shape[0]
    tm = min(TOKEN_TILE, S)
    Hh = W // HEAD_DIM

    def body(dx_ref, ga_ref, gb_ref, ba_ref, bb_ref, wa_ref, wb_ref, wo_ref,
             dxb_ref, dba_ref, dbb_ref, dgt_ref, doa_ref, dob_ref):
        dxb = dx_ref[...].astype(BF16)
        dxb_ref[...] = dxb
        dm = _nt(dxb, wo_ref[...])
        sa = jax.nn.sigmoid(ga_ref[...])
        sb = jax.nn.sigmoid(gb_ref[...])
        dba = (dm * sa).astype(BF16)
        dbb = (dm * sb).astype(BF16)
        dba_ref[...] = dba
        dbb_ref[...] = dbb
        dgt_ref[:, :D] = (dm * ba_ref[...].astype(F32) * sa * (1.0 - sa)).astype(BF16)
        dgt_ref[:, D:] = (dm * bb_ref[...].astype(F32) * sb * (1.0 - sb)).astype(BF16)
        _store_heads(doa_ref, _nt(wa_ref[...], dba))
        _store_heads(dob_ref, _nt(wb_ref[...], dbb))

    row = pl.BlockSpec((tm, D), lambda i: (i, 0))
    full = lambda r, c: pl.BlockSpec((r, c), lambda i: (0, 0))
    heads = lambda T: pl.BlockSpec((Hh, tm // T, HEAD_DIM, T), lambda i: (0, i, 0, 0))
    return pl.pallas_call(
        body, name="merge_bwd", grid=(S // tm,),
        in_specs=[row, pl.BlockSpec((tm, D), lambda i: (i, 0)), pl.BlockSpec((tm, D), lambda i: (i, 1)),
                  row, row, full(W, D), full(W, D), full(D, D)],
        out_specs=[row, row, row, pl.BlockSpec((tm, 2 * D), lambda i: (i, 0)), heads(t_a), heads(t_b)],
        out_shape=[jax.ShapeDtypeStruct((S, D), BF16)] * 3 + [jax.ShapeDtypeStruct((S, 2 * D), BF16),
                   jax.ShapeDtypeStruct((Hh, S // t_a, HEAD_DIM, t_a), F32),
                   jax.ShapeDtypeStruct((Hh, S // t_b, HEAD_DIM, t_b), BF16)],
        compiler_params=_cp("arbitrary"),
    )(dx2, gates, gates, ba, bb, wa, wb, w_out)


def _final_loss(x3, gf, target):
    S, D = x3.shape
    tm = min(TOKEN_TILE, S)

    def body(x_ref, g_ref, t_ref, loss_ref, dx_ref, dg_ref):
        i = pl.program_id(0)
        x = x_ref[...]
        g = g_ref[...]
        e = _norm_fwd(x, g) - t_ref[...]
        part = 0.5 * jnp.sum(jnp.mean(e * e, axis=-1, keepdims=True), axis=0, keepdims=True)
        dx, dg = _norm_bwd(x, g, e * (1.0 / D))
        dx_ref[...] = dx

        @pl.when(i == 0)
        def _():
            loss_ref[...] = part
            dg_ref[...] = dg

        @pl.when(i > 0)
        def _():
            loss_ref[...] += part
            dg_ref[...] += dg

    row = pl.BlockSpec((tm, D), lambda i: (i, 0))
    vec = pl.BlockSpec((1, D), lambda i: (0, 0))
    return pl.pallas_call(
        body, name="final_loss", grid=(S // tm,),
        in_specs=[row, vec, row],
        out_specs=[pl.BlockSpec((1, 1), lambda i: (0, 0)), row, vec],
        out_shape=[jax.ShapeDtypeStruct((1, 1), F32), jax.ShapeDtypeStruct((S, D), F32),
                   jax.ShapeDtypeStruct((1, D), F32)],
        compiler_params=_cp("arbitrary"),
    )(x3, gf, target)


SB_FWD_HEAD_GROUP = 8
SB_HEAD_GROUP = 4
LANES = 128


def _tri(T, kind):
    r = lax.broadcasted_iota(jnp.int32, (T, T), 0)
    c = lax.broadcasted_iota(jnp.int32, (T, T), 1)
    return {"after": r > c, "before": r < c}[kind].astype(BF16)


def _lane(v, j):
    return jnp.broadcast_to(v[:, j:j + 1], (v.shape[0], LANES))


def _t_bf16(x):
    return x.astype(F32).T.astype(BF16)


def _wide(v, T):
    return jnp.tile(v, (1, T // LANES))


SB_SLOTS = 3
SB_FWD_SLOTS = 2
COPY_PARTS = 4


class _split_copy:
    def __init__(self, src, dst, sems):
        n = src.shape[0] // COPY_PARTS
        self.parts = [pltpu.make_async_copy(src.at[pl.ds(r * n, n)], dst.at[pl.ds(r * n, n)], sems.at[r])
                      for r in range(COPY_PARTS)]

    def start(self):
        for cp in self.parts:
            cp.start()

    def wait(self):
        for cp in self.parts:
            cp.wait()


def _sb_pair(i, kb):
    return (i * (i + 1)) // 2 + kb


def _sb_fwd(qkv):
    H3, nb, dh, T = qkv.shape
    H = H3 // 3
    HG = SB_FWD_HEAD_GROUP
    assert HG == H, "one head group: a saved tile holds all the heads"
    n_pairs = (nb * (nb + 1)) // 2

    def body(q_ref, k_ref, v_ref, o_ref, saved_ref, stage, sems):
        row = lax.broadcasted_iota(jnp.int32, (T, T), 0)
        col = lax.broadcasted_iota(jnp.int32, (T, T), 1)
        tri = col < row
        after = _tri(T, "after")

        def save(slot, pair):
            return _split_copy(stage.at[slot], saved_ref.at[pair], sems.at[slot])

        def blocks(qs, i, kb, step, carry, diag):
            hs = range(HG)
            slot = step % SB_FWD_SLOTS

            @pl.when(step >= SB_FWD_SLOTS)
            def _():
                save(slot, 0).wait()

            z = [_nn(qs[hh], k_ref[hh, kb]) for hh in hs]
            res, ls, first = [None] * HG, [None] * HG, [None] * HG
            for hh in hs:
                sp = _softplus(z[hh])
                if diag:
                    sp = jnp.where(tri, sp, 0.0)
                ls[hh] = z[hh] - sp
                spb = sp.astype(BF16)
                first[hh] = _lane(spb.astype(F32), 0)
                res[hh] = _nn(spb, after)
            out = []
            for hh in hs:
                c, oacc = carry[2 * hh], carry[2 * hh + 1]
                a = jnp.exp(ls[hh] - (res[hh] + _wide(c, T)))
                if diag:
                    a = jnp.where(tri, a, 0.0)
                ab = a.astype(BF16)
                stage[slot, hh, 0] = ab
                stage[slot, hh, 1] = jnp.exp(ls[hh]).astype(BF16)
                out.extend([c + (first[hh] + _lane(res[hh], 0)), oacc + _nt(v_ref[hh, kb], ab)])
            save(slot, _sb_pair(i, kb)).start()
            return tuple(out)

        def qblock(i, step):
            qs = [_t_bf16(q_ref[hh, i]) for hh in range(HG)]
            carry = blocks(qs, i, i, step, (jnp.zeros((T, LANES), F32), jnp.zeros((dh, T), F32)) * HG, True)

            def kstep(t, carry):
                return blocks(qs, i, i - 1 - t, step + 1 + t, carry, False)

            carry = lax.fori_loop(0, i, kstep, carry)
            for hh in range(HG):
                o_ref[hh, i] = carry[2 * hh + 1]
            return step + 1 + i

        lax.fori_loop(0, nb, qblock, 0)
        for slot in range(min(SB_FWD_SLOTS, n_pairs)):
            save(slot, 0).wait()

    ht = lambda part: pl.BlockSpec((HG, nb, dh, T), lambda h: (part, 0, 0, 0), pipeline_mode=pl.Buffered(1))
    return pl.pallas_call(
        body, name="sb_fwd", grid=(1,),
        in_specs=[ht(0), ht(1), ht(2)],
        out_specs=[ht(0), ANY],
        out_shape=[jax.ShapeDtypeStruct((H, nb, dh, T), F32),
                   jax.ShapeDtypeStruct((n_pairs, H, 2, T, T), BF16)],
        scratch_shapes=[pltpu.VMEM((SB_FWD_SLOTS, HG, 2, T, T), BF16),
                        pltpu.SemaphoreType.DMA((SB_FWD_SLOTS, COPY_PARTS))],
        compiler_params=_cp("arbitrary"),
    )(qkv, qkv, qkv)


def _sb_bwd(qkv, dot, saved):
    H3, nb, dh, T = qkv.shape
    H = H3 // 3
    HG = SB_HEAD_GROUP
    n_pairs = (nb * (nb + 1)) // 2

    def body(qt_ref, k_ref, v_ref, dot_ref, saved_ref, dq_ref, dk_ref, dv_ref, stage, sems):
        head0 = pl.program_id(0) * HG
        row = lax.broadcasted_iota(jnp.int32, (T, T), 0)
        col = lax.broadcasted_iota(jnp.int32, (T, T), 1)
        tri = col < row
        before = _tri(T, "before")
        dk_ref[...] = jnp.zeros_like(dk_ref)
        dv_ref[...] = jnp.zeros_like(dv_ref)

        def fetch(slot, pair):
            return _split_copy(saved_ref.at[pair, pl.ds(head0, HG)], stage.at[slot], sems.at[slot])

        for ahead in range(min(SB_SLOTS - 1, n_pairs)):
            fetch(ahead, ahead).start()

        def blocks(qTs, dos, doTs, i, kb, carry, diag):
            hs = range(HG)
            pair = _sb_pair(i, kb)
            slot = pair % SB_SLOTS
            fetch(slot, pair).wait()
            nxt = pair + (SB_SLOTS - 1)

            @pl.when(nxt < n_pairs)
            def _():
                fetch(nxt % SB_SLOTS, nxt).start()

            kT = [k_ref[hh, kb] for hh in hs]
            da = [_nn(dos[hh], v_ref[hh, kb]) for hh in hs]
            g, gb, resg = [None] * HG, [None] * HG, [None] * HG
            for hh in hs:
                g[hh] = stage[slot, hh, 0].astype(F32) * da[hh]
                gb[hh] = g[hh].astype(BF16)
                resg[hh] = _nn(gb[hh], before)
            out = []
            for hh in hs:
                pre_g, dq = carry[2 * hh], carry[2 * hh + 1]
                dz = g[hh] - (g[hh] + (resg[hh] + _wide(pre_g, T))) * stage[slot, hh, 1].astype(F32)
                if diag:
                    dz = jnp.where(tri, dz, 0.0)
                dzb = dz.astype(BF16)
                dk_ref[hh, kb] += _nn(qTs[hh], dzb)
                dv_ref[hh, kb] += _nn(doTs[hh], stage[slot, hh, 0])
                out.extend([pre_g + (_lane(resg[hh], T - 1) + _lane(gb[hh].astype(F32), T - 1)),
                            dq + _nt(kT[hh], dzb)])
            return tuple(out)

        def qblock(i, _):
            qTs = [qt_ref[hh, i] for hh in range(HG)]
            doTs = [dot_ref[hh, i] for hh in range(HG)]
            dos = [_t_bf16(v) for v in doTs]
            carry = (jnp.zeros((T, LANES), F32), jnp.zeros((dh, T), F32)) * HG

            def kstep(kb, carry):
                return blocks(qTs, dos, doTs, i, kb, carry, False)

            carry = lax.fori_loop(0, i, kstep, carry)
            carry = blocks(qTs, dos, doTs, i, i, carry, True)
            for hh in range(HG):
                dq_ref[hh, i] = carry[2 * hh + 1]
            return 0

        lax.fori_loop(0, nb, qblock, 0)

    G = H // HG
    ht = lambda part: pl.BlockSpec((HG, nb, dh, T), lambda h: (h + part * G, 0, 0, 0),
                                   pipeline_mode=pl.Buffered(1))
    return pl.pallas_call(
        body, name="sb_bwd", grid=(G,),
        in_specs=[ht(0), ht(1), ht(2), ht(0), ANY],
        out_specs=[ht(0), ht(0), ht(0)],
        out_shape=[jax.ShapeDtypeStruct((H, nb, dh, T), F32)] * 3,
        scratch_shapes=[pltpu.VMEM((SB_SLOTS, HG, 2, T, T), BF16), pltpu.SemaphoreType.DMA((SB_SLOTS, COPY_PARTS))],
        compiler_params=_cp("arbitrary"),
    )(qkv, qkv, qkv, dot, saved)


def _swa_probs(zp, zc, bias, sink, first):
    T = zp.shape[0]
    key = lax.broadcasted_iota(jnp.int32, (T, T), 0)
    qry = lax.broadcasted_iota(jnp.int32, (T, T), 1)
    lp = jnp.where(jnp.logical_and(key > qry, jnp.logical_not(first)), zp + bias[:T, :], NEG_BIG)
    lc = jnp.where(key <= qry, zc + bias[T:, :], NEG_BIG)
    m = jnp.maximum(jnp.maximum(jnp.max(lp, axis=0, keepdims=True), jnp.max(lc, axis=0, keepdims=True)), sink)
    pp = jnp.exp(lp - m)
    pc = jnp.exp(lc - m)
    ps = jnp.exp(sink - m)
    inv = 1.0 / (jnp.sum(pp, axis=0, keepdims=True) + jnp.sum(pc, axis=0, keepdims=True) + ps)
    return pp * inv, pc * inv, ps * inv


def _swa_specs(nb, dh, T, Hq, Hkv, clamp):
    blk = (lambda n: jnp.minimum(n, nb - 1)) if clamp else (lambda n: n)
    q = pl.BlockSpec((Hq, None, dh, T), lambda n: (0, blk(n), 0, 0))
    kv = lambda first, back: pl.BlockSpec(
        (Hkv, None, dh, T), lambda n: (first // Hkv, jnp.maximum(blk(n) - back, 0) if back else blk(n), 0, 0))
    return q, [kv(Hq, 1), kv(Hq, 0), kv(Hq + Hkv, 1), kv(Hq + Hkv, 0)]


def _swa_fwd(qkv, bias, sinks):
    Hq, Hkv, grp = SWA_Q_HEADS, SWA_KV_HEADS, SWA_GROUP
    _, nb, dh, T = qkv.shape

    def body(sink_ref, q_ref, kp_ref, kc_ref, vp_ref, vc_ref, bias_ref, o_ref):
        n = pl.program_id(0)
        kpn = [_t_bf16(kp_ref[hk]) for hk in range(Hkv)]
        kcn = [_t_bf16(kc_ref[hk]) for hk in range(Hkv)]
        zs = [(_nn(kpn[h // grp], q_ref[h]), _nn(kcn[h // grp], q_ref[h])) for h in range(Hq)]
        for h in range(Hq):
            pp, pc, _ = _swa_probs(*zs[h], bias_ref[h], sink_ref[h], n == 0)
            o_ref[h] = _nn(vp_ref[h // grp], pp.astype(BF16)) + _nn(vc_ref[h // grp], pc.astype(BF16))

    q_spec, kv_specs = _swa_specs(nb, dh, T, Hq, Hkv, False)
    return pl.pallas_call(
        body, name="swa_fwd", grid=(nb,),
        in_specs=[pl.BlockSpec(memory_space=pltpu.SMEM), q_spec] + kv_specs
                 + [pl.BlockSpec((Hq, 2 * T, T), lambda n: (0, 0, 0))],
        out_specs=pl.BlockSpec((Hq, None, dh, T), lambda n: (0, n, 0, 0)),
        out_shape=jax.ShapeDtypeStruct((Hq, nb, dh, T), F32),
        compiler_params=_cp("arbitrary"),
    )(sinks, qkv, qkv, qkv, qkv, qkv, bias)


def _swa_bwd(qkv, bias, sinks, dot, ot):
    Hq, Hkv, grp = SWA_Q_HEADS, SWA_KV_HEADS, SWA_GROUP
    _, nb, dh, T = qkv.shape

    def body(sink_ref, qt_ref, kp_ref, kc_ref, vp_ref, vc_ref, bias_ref, dot_ref, ot_ref,
             dq_ref, dk_ref, dv_ref, dbias_ref, dsink_ref, ck, cv):
        n = pl.program_id(0)

        @pl.when(n == 0)
        def _():
            dbias_ref[...] = jnp.zeros_like(dbias_ref)
            dsink_ref[...] = jnp.zeros_like(dsink_ref)
            ck[...] = jnp.zeros_like(ck)
            cv[...] = jnp.zeros_like(cv)

        @pl.when(n < nb)
        def _():
            kp, kc = [kp_ref[hk] for hk in range(Hkv)], [kc_ref[hk] for hk in range(Hkv)]
            kpn, kcn = [_t_bf16(v) for v in kp], [_t_bf16(v) for v in kc]
            vpn = [_t_bf16(vp_ref[hk]) for hk in range(Hkv)]
            vcn = [_t_bf16(vc_ref[hk]) for hk in range(Hkv)]
            qTs = [qt_ref[h] for h in range(Hq)]
            doTs = [dot_ref[h].astype(BF16) for h in range(Hq)]
            zs = [(_nn(kpn[h // grp], qTs[h]), _nn(kcn[h // grp], qTs[h])) for h in range(Hq)]
            dps = [(_nn(vpn[h // grp], doTs[h]), _nn(vcn[h // grp], doTs[h])) for h in range(Hq)]
            dls, pbs = [], []
            for h in range(Hq):
                pp, pc, ps = _swa_probs(*zs[h], bias_ref[h], sink_ref[h], n == 0)
                delta = jnp.sum(dot_ref[h] * ot_ref[h], axis=0, keepdims=True)
                dlp = pp * (dps[h][0] - delta)
                dlc = pc * (dps[h][1] - delta)
                dbias_ref[h, :T, :] += dlp
                dbias_ref[h, T:, :] += dlc
                dsink_ref[h] += -ps * delta
                dls.append((dlp.astype(BF16), dlc.astype(BF16)))
                pbs.append((pp.astype(BF16), pc.astype(BF16)))
            zero = jnp.zeros((dh, T), F32)
            kprev, kcur, vprev, vcur = [zero] * Hkv, [zero] * Hkv, [zero] * Hkv, [zero] * Hkv
            for h in range(Hq):
                hk = h // grp
                dlpb, dlcb = dls[h]
                dq_ref[h] = _nn(kp[hk], dlpb) + _nn(kc[hk], dlcb)
                kprev[hk] = kprev[hk] + _nt(qTs[h], dlpb)
                kcur[hk] = kcur[hk] + _nt(qTs[h], dlcb)
                vprev[hk] = vprev[hk] + _nt(doTs[h], pbs[h][0])
                vcur[hk] = vcur[hk] + _nt(doTs[h], pbs[h][1])
            for hk in range(Hkv):
                dk_ref[hk] = ck[hk] + kprev[hk]
                dv_ref[hk] = cv[hk] + vprev[hk]
                ck[hk] = kcur[hk]
                cv[hk] = vcur[hk]

        @pl.when(n == nb)
        def _():
            dk_ref[...] = ck[...]
            dv_ref[...] = cv[...]

    qt_spec, kv_specs = _swa_specs(nb, dh, T, Hq, Hkv, True)
    prev = pl.BlockSpec((Hkv, None, dh, T), lambda n: (0, jnp.maximum(n - 1, 0), 0, 0))
    whole = lambda a, b: pl.BlockSpec((Hq, a, b), lambda n: (0, 0, 0))
    return pl.pallas_call(
        body, name="swa_bwd", grid=(nb + 1,),
        in_specs=[pl.BlockSpec(memory_space=pltpu.SMEM), qt_spec] + kv_specs
                 + [whole(2 * T, T), qt_spec, qt_spec],
        out_specs=[qt_spec, prev, prev, whole(2 * T, T), whole(1, T)],
        out_shape=[jax.ShapeDtypeStruct((Hq, nb, dh, T), F32), jax.ShapeDtypeStruct((Hkv, nb, dh, T), F32),
                   jax.ShapeDtypeStruct((Hkv, nb, dh, T), F32), jax.ShapeDtypeStruct((Hq, 2 * T, T), F32),
                   jax.ShapeDtypeStruct((Hq, 1, T), F32)],
        scratch_shapes=[pltpu.VMEM((Hkv, dh, T), F32), pltpu.VMEM((Hkv, dh, T), F32)],
        compiler_params=_cp("arbitrary"),
    )(sinks, qkv, qkv, qkv, qkv, qkv, bias, dot, ot)


def _split3(x):
    h1 = x.astype(BF16)
    r1 = x - h1.astype(F32)
    h2 = r1.astype(BF16)
    h3 = (r1 - h2.astype(F32)).astype(BF16)
    return h1, h2, h3


def _bias_expand(rel_t, onehot):
    Hq, NB = rel_t.shape
    L = onehot.shape[1]

    def body(r_ref, oh_ref, o_ref):
        h1, h2, h3 = _split3(r_ref[...])
        oh = oh_ref[...]
        o_ref[...] = _nn(h1, oh) + _nn(h2, oh) + _nn(h3, oh)

    return pl.pallas_call(
        body, name="bias_expand", grid=(1,),
        in_specs=[pl.BlockSpec((Hq, NB), lambda i: (0, 0)), pl.BlockSpec((NB, L), lambda i: (0, 0))],
        out_specs=pl.BlockSpec((Hq, L), lambda i: (0, 0)),
        out_shape=jax.ShapeDtypeStruct((Hq, L), F32),
        compiler_params=_cp("arbitrary"),
    )(rel_t, onehot)


def _bias_reduce(dbias, onehot):
    Hq, L = dbias.shape
    NB = onehot.shape[0]

    def body(d_ref, oh_ref, o_ref):
        h1, h2, h3 = _split3(d_ref[...])
        oh = oh_ref[...]
        o_ref[...] = _nt(h1, oh) + _nt(h2, oh) + _nt(h3, oh)

    return pl.pallas_call(
        body, name="bias_reduce", grid=(1,),
        in_specs=[pl.BlockSpec((Hq, L), lambda i: (0, 0)), pl.BlockSpec((NB, L), lambda i: (0, 0))],
        out_specs=pl.BlockSpec((Hq, NB), lambda i: (0, 0)),
        out_shape=jax.ShapeDtypeStruct((Hq, NB), F32),
        compiler_params=_cp("arbitrary"),
    )(dbias, onehot)


def _adamw(w, g, m, v, name):
    R, C = w.shape
    tr = 256 if R % 256 == 0 else R
    bc1 = 1.0 - ADAM_B1 ** ADAM_STEP
    bc2 = 1.0 - ADAM_B2 ** ADAM_STEP

    def body(w_ref, g_ref, m_ref, v_ref, d_ref, nm_ref, nv_ref):
        g = g_ref[...]
        m2 = ADAM_B1 * m_ref[...] + (1.0 - ADAM_B1) * g
        v2 = ADAM_B2 * v_ref[...] + (1.0 - ADAM_B2) * (g * g)
        nm_ref[...] = m2
        nv_ref[...] = v2
        d_ref[...] = -ADAM_LR * ((m2 / bc1) / (jnp.sqrt(v2 / bc2) + ADAM_EPS) + ADAM_WD * w_ref[...])

    spec = pl.BlockSpec((tr, C), lambda i: (i, 0))
    return pl.pallas_call(
        body, name=name, grid=(R // tr,),
        in_specs=[spec] * 4, out_specs=[spec] * 3,
        out_shape=[jax.ShapeDtypeStruct((R, C), F32)] * 3,
        compiler_params=_cp("arbitrary"),
    )(w, g, m, v)


def _row_tile(R):
    return max(t for t in range(16, 513, 16) if R % t == 0)


def _add_halves(mine, recv, name):
    K, R, C = mine.shape
    tr = _row_tile(R)

    def body(a_ref, b_ref, o_ref, ob_ref):
        s = a_ref[...].astype(F32) + b_ref[...].astype(F32)
        o_ref[...] = s
        ob_ref[...] = s.astype(BF16)

    spec = pl.BlockSpec((None, tr, C), lambda k, i: (k, i, 0))
    return pl.pallas_call(
        body, name=name, grid=(K, R // tr),
        in_specs=[spec, spec], out_specs=[spec, spec],
        out_shape=[jax.ShapeDtypeStruct((K, R, C), F32), jax.ShapeDtypeStruct((K, R, C), BF16)],
        compiler_params=_cp("arbitrary", "arbitrary"),
    )(mine, recv)


def _add_received(own, recv, name):
    R, C = own.shape
    tr = _row_tile(R)

    def body(a_ref, r_ref, o_ref):
        o_ref[...] = ((a_ref[...] + r_ref[0].astype(F32)) + r_ref[1].astype(F32)) + r_ref[2].astype(F32)

    return pl.pallas_call(
        body, name=name, grid=(R // tr,),
        in_specs=[pl.BlockSpec((tr, C), lambda i: (i, 0)), pl.BlockSpec((3, tr, C), lambda i: (0, i, 0))],
        out_specs=pl.BlockSpec((tr, C), lambda i: (i, 0)),
        out_shape=jax.ShapeDtypeStruct((R, C), F32),
        compiler_params=_cp("arbitrary"),
    )(own, recv)


def _position():
    x, y, c = lax.axis_index("x"), lax.axis_index("y"), lax.axis_index("c")
    others = [(1 - x, y), (x, 1 - y), (1 - x, 1 - y)]
    return x, y, c, others


def _remote(src, dst, send_sems, recv_sems, k, dev):
    return pltpu.make_async_remote_copy(src_ref=src, dst_ref=dst, send_sem=send_sems.at[k],
                                        recv_sem=recv_sems.at[k], device_id=dev, device_id_type=MESH_ID)


class _gather_exchange:
    def __init__(self, src, out, send_sems, recv_sems):
        x, y, c, others = _position()
        mine, sibling = 2 * x + y, (x, y, 1 - c)
        self.sends, self.arrivals, self.passes, self.from_sibling = [], [], [], []
        for j, (ox, oy) in enumerate(others):
            slot = out.at[2 * ox + oy, c]
            theirs = out.at[2 * ox + oy, 1 - c]
            self.sends.append(_remote(src.at[c], out.at[mine, c], send_sems, recv_sems, j, (ox, oy, c)))
            self.arrivals.append(_remote(slot, slot, send_sems, recv_sems, j, (ox, oy, c)))
            self.passes.append(_remote(slot, slot, send_sems, recv_sems, 3 + j, sibling))
            self.from_sibling.append(_remote(theirs, theirs, send_sems, recv_sems, 3 + j, sibling))

    def start(self):
        for cp in self.sends:
            cp.start()

    def forward(self):
        for arrived, onward in zip(self.arrivals, self.passes):
            arrived.wait_recv()
            onward.start()

    def finish(self):
        for cp in self.from_sibling:
            cp.wait_recv()
        for cp in self.sends + self.passes:
            cp.wait_send()


def _gather_weights(shard):
    R, C = shard.shape
    half = R // 2

    def body(src, out, send_sems, recv_sems):
        ex = _gather_exchange(src, out, send_sems, recv_sems)
        ex.start()
        ex.forward()
        ex.finish()

    return pl.pallas_call(
        body, name="gather_weights",
        in_specs=[ANY], out_specs=ANY,
        out_shape=jax.ShapeDtypeStruct((N_CHIPS, 2, half, C), shard.dtype),
        scratch_shapes=[pltpu.SemaphoreType.DMA((6,)), pltpu.SemaphoreType.DMA((6,))],
    )(shard.reshape(2, half, C)).reshape(N_CHIPS, R, C)


def _swap_halves(grads, name):
    K, R, C = grads.shape
    half = R // 2

    def body(src, out, send_sems, recv_sems):
        x, y, c, _ = _position()
        theirs = src.at[:, pl.ds(pl.multiple_of((1 - c) * half, 16), half), :]
        cp = _remote(theirs, out, send_sems, recv_sems, 0, (x, y, 1 - c))
        cp.start()
        cp.wait()

    return pl.pallas_call(
        body, name=name,
        in_specs=[ANY], out_specs=ANY,
        out_shape=jax.ShapeDtypeStruct((K, half, C), grads.dtype),
        scratch_shapes=[pltpu.SemaphoreType.DMA((1,)), pltpu.SemaphoreType.DMA((1,))],
    )(grads)


class _scatter_exchange:
    def __init__(self, src, out, send_sems, recv_sems):
        x, y, c, others = _position()
        self.copies = [_remote(src.at[2 * ox + oy], out.at[j], send_sems, recv_sems, j, (ox, oy, c))
                       for j, (ox, oy) in enumerate(others)]

    def start(self):
        for cp in self.copies:
            cp.start()

    def finish(self):
        for cp in self.copies:
            cp.wait()


def _scatter_to_owners(parts, name):
    K, H, C = parts.shape

    def body(src, out, send_sems, recv_sems):
        ex = _scatter_exchange(src, out, send_sems, recv_sems)
        ex.start()
        ex.finish()

    return pl.pallas_call(
        body, name=name,
        in_specs=[ANY], out_specs=ANY,
        out_shape=jax.ShapeDtypeStruct((3, H, C), parts.dtype),
        scratch_shapes=[pltpu.SemaphoreType.DMA((3,)), pltpu.SemaphoreType.DMA((3,))],
    )(parts)


def _swap_reduced(half_rows, name):
    H, C = half_rows.shape

    def body(src, out, send_sems, recv_sems):
        x, y, c, _ = _position()
        cp = _remote(src, out, send_sems, recv_sems, 0, (x, y, 1 - c))
        cp.start()
        cp.wait()

    return pl.pallas_call(
        body, name=name,
        in_specs=[ANY], out_specs=ANY,
        out_shape=jax.ShapeDtypeStruct((H, C), half_rows.dtype),
        scratch_shapes=[pltpu.SemaphoreType.DMA((1,)), pltpu.SemaphoreType.DMA((1,))],
    )(half_rows)


def _allreduce_small(block):
    R, C = block.shape
    n_dev = 8

    def body(src, out, slots, send_sems, recv_sems):
        x, y, c, _ = _position()
        me = 4 * x + 2 * y + c
        slots[me] = src[...]
        sends = []
        for r in range(1, n_dev):
            peer = (x ^ (r >> 2), y ^ ((r >> 1) & 1), c ^ (r & 1))
            cp = _remote(src, slots.at[me], send_sems, recv_sems, r - 1, peer)
            cp.start()
            sends.append(cp)
        for r in range(1, n_dev):
            theirs = slots.at[me ^ r]
            _remote(theirs, theirs, send_sems, recv_sems, r - 1, (x, y, c)).wait_recv()
        for cp in sends:
            cp.wait_send()
        acc = slots[0]
        for d in range(1, n_dev):
            acc = acc + slots[d]
        out[...] = acc

    return pl.pallas_call(
        body, name="allreduce_small",
        in_specs=[pl.BlockSpec(memory_space=pltpu.VMEM)], out_specs=pl.BlockSpec(memory_space=pltpu.VMEM),
        out_shape=jax.ShapeDtypeStruct((R, C), F32),
        scratch_shapes=[pltpu.VMEM((n_dev, R, C), F32), pltpu.SemaphoreType.DMA((7,)), pltpu.SemaphoreType.DMA((7,))],
    )(block)


def _rel_bucket(dist):
    max_exact = REL_BUCKETS // 2
    d = jnp.maximum(dist, 1).astype(F32)
    large = max_exact + (jnp.log(d / max_exact) / math.log(REL_MAX_DIST / max_exact)
                         * (REL_BUCKETS - max_exact)).astype(jnp.int32)
    large = jnp.minimum(large, REL_BUCKETS - 1)
    return jnp.where(dist < max_exact, dist, large)


def _bucket_onehot():
    T = SWA_BLOCK
    dist = (jnp.arange(T)[None, :] + T) - jnp.arange(2 * T)[:, None]
    bucket = _rel_bucket(jnp.maximum(dist, 0)).reshape(1, T * 2 * T)
    return (bucket == jnp.arange(REL_BUCKETS)[:, None]).astype(BF16)


_BUF = (("ffn1_w1", "t"), ("ffn1_w3", "t"), ("ffn1_w2", "n"), ("ffn2_w1", "t"), ("ffn2_w3", "t"),
        ("ffn2_w2", "n"), ("w_in", "t"), ("w_out", "n"), ("w_branch_swa", "tw"), ("w_branch_sb", "tw"))


def _to_rows(name_kind, w, D):
    kind = name_kind[1]
    if kind == "n":
        return w
    if kind == "t":
        return w.T
    return w.T.reshape(-1, D)


def _from_rows(name_kind, rows, width):
    kind = name_kind[1]
    if kind == "n":
        return rows
    if kind == "t":
        return rows.T
    return rows.reshape(-1, width).T


def kernel(x, norm_ffn1, ffn1_w1, ffn1_w3, ffn1_w2, norm_mix, w_in, swa_sinks, rel_bias, w_branch_swa, w_branch_sb, w_out, norm_ffn2, ffn2_w1, ffn2_w3, ffn2_w2, norm_final, loss_target, m_norm_ffn1, m_ffn1_w1, m_ffn1_w3, m_ffn1_w2, m_norm_mix, m_w_in, m_swa_sinks, m_rel_bias, m_w_branch_swa, m_w_branch_sb, m_w_out, m_norm_ffn2, m_ffn2_w1, m_ffn2_w3, m_ffn2_w2, m_norm_final, v_norm_ffn1, v_ffn1_w1, v_ffn1_w3, v_ffn1_w2, v_norm_mix, v_w_in, v_swa_sinks, v_rel_bias, v_w_branch_swa, v_w_branch_sb, v_w_out, v_norm_ffn2, v_ffn2_w1, v_ffn2_w3, v_ffn2_w2, v_norm_final):
    names = ["norm_ffn1", "ffn1_w1", "ffn1_w3", "ffn1_w2", "norm_mix", "w_in", "swa_sinks", "rel_bias",
             "w_branch_swa", "w_branch_sb", "w_out", "norm_ffn2", "ffn2_w1", "ffn2_w3", "ffn2_w2", "norm_final"]
    W = dict(zip(names, [norm_ffn1, ffn1_w1, ffn1_w3, ffn1_w2, norm_mix, w_in, swa_sinks, rel_bias,
                         w_branch_swa, w_branch_sb, w_out, norm_ffn2, ffn2_w1, ffn2_w3, ffn2_w2, norm_final]))
    M = dict(zip(names, [m_norm_ffn1, m_ffn1_w1, m_ffn1_w3, m_ffn1_w2, m_norm_mix, m_w_in, m_swa_sinks, m_rel_bias,
                         m_w_branch_swa, m_w_branch_sb, m_w_out, m_norm_ffn2, m_ffn2_w1, m_ffn2_w3, m_ffn2_w2,
                         m_norm_final]))
    V = dict(zip(names, [v_norm_ffn1, v_ffn1_w1, v_ffn1_w3, v_ffn1_w2, v_norm_mix, v_w_in, v_swa_sinks, v_rel_bias,
                         v_w_branch_swa, v_w_branch_sb, v_w_out, v_norm_ffn2, v_ffn2_w1, v_ffn2_w3, v_ffn2_w2,
                         v_norm_final]))
    xs = x[0]
    target = loss_target[0]
    S, D = xs.shape
    QW = SWA_Q_HEADS * HEAD_DIM
    KW = SWA_KV_HEADS * HEAD_DIM
    BW = SB_HEADS * HEAD_DIM
    QKV = QW + 2 * KW + 3 * BW

    pieces = [_to_rows(nk, W[nk[0]][0], D) for nk in _BUF]
    sizes = [p.shape[0] for p in pieces]
    offs = [0]
    for s in sizes:
        offs.append(offs[-1] + s)
    n_first = 3
    first_rows = offs[n_first]
    shard_a = jnp.concatenate(pieces[:n_first], axis=0).astype(BF16)
    shard_b = jnp.concatenate(pieces[n_first:], axis=0).astype(BF16)
    chip = 2 * lax.axis_index("x") + lax.axis_index("y")
    gathered_a = lax.dynamic_update_slice(_gather_weights(shard_a), shard_a[None], (chip, 0, 0))
    f1w1, f1w3, f1w2 = [gathered_a[:, offs[i]:offs[i + 1], :].reshape(N_CHIPS * sizes[i], D) for i in range(n_first)]

    g1, gmix, g3 = W["norm_ffn1"], W["norm_mix"], W["norm_ffn2"]
    gf = W["norm_final"].reshape(1, D)

    x1, h1, a1, b1, gathered_b = _ffn_fwd(xs, g1, f1w1, f1w3, f1w2, "ffn1_fwd", gather=shard_b)
    gathered_b = lax.dynamic_update_slice(gathered_b, shard_b[None], (chip, 0, 0))

    def full(i):
        return gathered_b[:, offs[i] - first_rows:offs[i + 1] - first_rows, :].reshape(N_CHIPS * sizes[i], D)

    f2w1, f2w3, f2w2, w_in_t, w_out_f = [full(i) for i in range(n_first, 8)]
    wa_t = full(8).reshape(D, QW)
    wb_t = full(9).reshape(D, BW)
    o0 = QW + 2 * KW
    rows = jnp.arange(w_in_t.shape[0])
    is_q = (rows < QW) | ((rows >= o0) & (rows < o0 + BW))
    w_in_s = w_in_t * jnp.where(is_q, QK_SCALE, 1.0).astype(BF16)[:, None]
    qkv_a = _norm_proj_heads(x1, gmix, w_in_s[:o0], SWA_BLOCK, "proj_swa")
    qkv_b = _norm_proj_heads(x1, gmix, w_in_s[o0:QKV], SB_BLOCK, "proj_sb")
    gates, h2 = _norm_matmul_nt(x1, gmix, w_in_t[QKV:], F32, "proj_gates")

    onehot = _bucket_onehot()
    bias = _bias_expand(W["rel_bias"].T, onehot).reshape(SWA_Q_HEADS, 2 * SWA_BLOCK, SWA_BLOCK)
    sinks = W["swa_sinks"].reshape(SWA_Q_HEADS)
    oa_t = _swa_fwd(qkv_a, bias, sinks)
    ob_t, saved_sb = _sb_fwd(qkv_b)

    x2, merged, ba, bb = _merge_fwd(x1, gates, oa_t, ob_t, wa_t, wb_t, w_out_f)
    x3, h3, a2, b2 = _ffn_fwd(x2, g3, f2w1, f2w3, f2w2, "ffn2_fwd")
    loss_part, dx3, dgf = _final_loss(x3, gf, target)

    dx2, dg3, dz2, da2, db2, u2 = _ffn_bwd(dx3, x2, g3, a2, b2, f2w1, f2w3, f2w2, "ffn2_bwd")
    grads = {}
    grads["ffn2_w1"] = _tn_matmul(da2, h3, "ffn2_dw1")
    grads["ffn2_w3"] = _tn_matmul(db2, h3, "ffn2_dw3")
    grads["ffn2_w2"] = _tn_matmul(u2, dz2, "ffn2_dw2")

    dx2b, dba, dbb, dgates, doa_t, dob_t = _merge_bwd(dx2, gates, ba, bb, wa_t.T, wb_t.T, w_out_f,
                                                      SWA_BLOCK, SB_BLOCK)
    grads["w_out"] = _tn_matmul(merged, dx2b, "dw_out")
    grads["w_branch_swa"] = _heads_matmul([(oa_t, 1.0)], dba, "dw_branch_swa").T
    grads["w_branch_sb"] = _heads_matmul([(ob_t, 1.0)], dbb, "dw_branch_sb").T

    dqb_t, dkb_t, dvb_t = _sb_bwd(qkv_b, dob_t, saved_sb)
    dqa_t, dka_t, dva_t, dbias, dsink_rows = _swa_bwd(qkv_a, bias, sinks, doa_t, oa_t)
    d_rel = _bias_reduce(dbias.reshape(SWA_Q_HEADS, -1), onehot).T
    d_sinks = jnp.sum(dsink_rows, axis=(1, 2))

    dheads = [(dqa_t, QK_SCALE), (dka_t, 1.0), (dva_t, 1.0), (dqb_t, QK_SCALE), (dkb_t, 1.0), (dvb_t, 1.0)]
    grads["w_in"] = jnp.concatenate([_heads_matmul(dheads, h2, "dw_in_heads").astype(BF16),
                                     _tn_matmul(dgates, h2, "dw_in_gates")], axis=0)
    row0, pieces_in = 0, []
    for a, _ in dheads:
        pieces_in.append((a, row0))
        row0 += a.shape[0] * HEAD_DIM
    dx1, dgmix = _proj_bwd(pieces_in, dgates, w_in_s, x1, gmix, dx2)

    c = lax.axis_index("c")

    def reduce_start(lo, hi, tag):
        gbuf = jnp.concatenate([grads[_BUF[i][0]].astype(BF16).reshape(N_CHIPS, sizes[i], D) for i in range(lo, hi)],
                               axis=1)
        half = gbuf.shape[1] // 2
        from_sibling = _swap_halves(gbuf, "swap_halves_" + tag)
        my_half = lax.dynamic_slice_in_dim(gbuf, c * half, half, axis=1)
        return _add_halves(my_half, from_sibling, "add_sibling_" + tag)

    def reduce_finish(chip_sum, received, tag):
        own = lax.dynamic_index_in_dim(chip_sum, chip, axis=0, keepdims=False)
        my_rows = _add_received(own, received, "add_chips_" + tag)
        their_rows = _swap_reduced(my_rows, "swap_reduced_" + tag)
        return jnp.concatenate([jnp.where(c == 0, my_rows, their_rows), jnp.where(c == 0, their_rows, my_rows)],
                               axis=0)

    sum_b, sum16_b = reduce_start(n_first, len(_BUF), "late")
    dx0, dg1, dz1, da1, db1, u1, received_b = _ffn_bwd(dx1, xs, g1, a1, b1, f1w1, f1w3, f1w2, "ffn1_bwd",
                                                       scatter=sum16_b)
    grads["ffn1_w1"] = _tn_matmul(da1, h1, "ffn1_dw1")
    grads["ffn1_w3"] = _tn_matmul(db1, h1, "ffn1_dw3")
    grads["ffn1_w2"] = _tn_matmul(u1, dz1, "ffn1_dw2")
    sum_a, sum16_a = reduce_start(0, n_first, "first")
    reduced = jnp.concatenate([reduce_finish(sum_a, _scatter_to_owners(sum16_a, "scatter_to_owners"), "first"),
                               reduce_finish(sum_b, received_b, "late")], axis=0)

    small_rows = [dg1, dgmix, dg3, dgf,
                  jnp.pad(d_sinks.reshape(1, -1), ((0, 0), (0, D - SWA_Q_HEADS))),
                  jnp.pad(d_rel.reshape(1, -1), ((0, 0), (0, D - REL_BUCKETS * SWA_Q_HEADS))),
                  jnp.pad(loss_part, ((0, 0), (0, D - 1))), jnp.zeros((1, D), F32)]
    small = _allreduce_small(jnp.concatenate(small_rows, axis=0))
    loss = small[6, 0]

    G, g_rows = {}, {}
    for i, nk in enumerate(_BUF):
        rows = reduced[offs[i]:offs[i + 1]]
        g_rows[nk[0]] = rows.reshape(-1, W[nk[0]].shape[1]) if nk[1] == "tw" else rows
        G[nk[0]] = _from_rows(nk, rows, W[nk[0]].shape[1])[None]
    G["norm_ffn1"], G["norm_mix"], G["norm_ffn2"] = small[0:1], small[1:2], small[2:3]
    G["norm_final"] = small[3]
    G["swa_sinks"] = small[4:5, :SWA_Q_HEADS]
    G["rel_bias"] = small[5, :REL_BUCKETS * SWA_Q_HEADS].reshape(REL_BUCKETS, SWA_Q_HEADS)

    delta, new_m, new_v = {}, {}, {}
    small_names = ["norm_ffn1", "norm_mix", "norm_ffn2", "norm_final", "swa_sinks", "rel_bias"]

    def pack(d):
        return jnp.concatenate([jnp.pad(d[n].reshape(1, -1), ((0, 0), (0, D - d[n].size))) for n in small_names]
                               + [jnp.zeros((2, D), F32)], axis=0)

    sd, sm, sv = _adamw(pack(W), pack(G), pack(M), pack(V), "adamw_small")
    for r, n in enumerate(small_names):
        for dst, src in ((delta, sd), (new_m, sm), (new_v, sv)):
            dst[n] = src[r, :W[n].size].reshape(W[n].shape)
    for n, kind in _BUF:
        turn = (lambda a: a) if kind == "n" else (lambda a: a.T)
        d_, m_, v_ = _adamw(turn(W[n][0]), g_rows[n], turn(M[n][0]), turn(V[n][0]), "adamw_" + n)
        delta[n], new_m[n], new_v[n] = turn(d_)[None], turn(m_)[None], turn(v_)[None]

    return (loss, dx0[None], *[G[n] for n in names], *[delta[n] for n in names],
            *[new_m[n] for n in names], *[new_v[n] for n in names])
```

```python
import math

import jax
import jax.numpy as jnp
from jax import lax
from jax.experimental import pallas as pl
from jax.experimental.pallas import tpu as pltpu

F32, BF16 = jnp.float32, jnp.bfloat16
MESH_ID = pl.DeviceIdType.MESH
ANY = pl.BlockSpec(memory_space=pl.ANY)

RMS_EPS = 1e-6
HEAD_DIM = 64
SWA_Q_HEADS, SWA_KV_HEADS, SWA_GROUP = 8, 2, 4
SWA_BLOCK = 128
SB_HEADS = 8
SB_BLOCK = 256
REL_BUCKETS, REL_MAX_DIST = 32, 128
NEG_BIG = -1e30
QK_SCALE = HEAD_DIM ** -0.5
ADAM_LR, ADAM_B1, ADAM_B2, ADAM_EPS, ADAM_WD, ADAM_STEP = 0.001, 0.9, 0.999, 1e-08, 0.01, 10

N_CHIPS = 4
TOKEN_TILE = 512
MATMUL_TOKEN_TILE = 1024
WGRAD_ROW_TILES = (2176, 1408, 1024, 256)
WGRAD_TOKEN_TILE = 2048
FF_TILE = 2816
FFN_TOKEN_TILE = 512
PROJ_HEAD_ROWS = 768
FF_BWD_TILE = 1408
VMEM_LIMIT = 60 * 1024 * 1024


def _cp(*sem):
    return pltpu.CompilerParams(dimension_semantics=sem, vmem_limit_bytes=VMEM_LIMIT)


def _nn(a, b):
    return jnp.dot(a, b, preferred_element_type=F32)


def _nt(a, b):
    return lax.dot_general(a, b, (((1,), (1,)), ((), ())), preferred_element_type=F32)


def _tn(a, b):
    return lax.dot_general(a, b, (((0,), (0,)), ((), ())), preferred_element_type=F32)


def _norm_fwd(x, g):
    return x * lax.rsqrt(jnp.mean(x * x, axis=-1, keepdims=True) + RMS_EPS) * g


def _norm_bwd(x, g, dh):
    r = lax.rsqrt(jnp.mean(x * x, axis=-1, keepdims=True) + RMS_EPS)
    xh = x * r
    dxh = dh * g
    dx = r * (dxh - xh * jnp.mean(dxh * xh, axis=-1, keepdims=True))
    return dx, jnp.sum(dh * xh, axis=0, keepdims=True)


SOFTPLUS_LINEAR = 20.0


def _softplus(z):
    return jnp.maximum(jnp.log(1.0 + jnp.exp(jnp.minimum(z, SOFTPLUS_LINEAR))), z)


def _ffn_fwd(x, g, w1t, w3t, w2, name, gather=None):
    S, D = x.shape
    F = w2.shape[0]
    tm, tf = min(FFN_TOKEN_TILE, S), FF_TILE
    ni, nj = S // tm, F // tf

    def body(x_ref, g_ref, w1_ref, w3_ref, w2_ref, *rest):
        if gather is None:
            xo_ref, h_ref, a_ref, b_ref, hs, acc = rest
        else:
            shard_ref, xo_ref, h_ref, a_ref, b_ref, gathered_ref, hs, acc, send_sems, recv_sems = rest
        i, j = pl.program_id(0), pl.program_id(1)
        if gather is not None:
            for when, phase in ((jnp.logical_and(i == 0, j == 0), "start"),
                                (jnp.logical_and(i == ni - 1, j == 0), "forward"),
                                (jnp.logical_and(i == ni - 1, j == nj - 1), "finish")):
                @pl.when(when)
                def _():
                    getattr(_gather_exchange(shard_ref, gathered_ref, send_sems, recv_sems), phase)()

        @pl.when(j == 0)
        def _():
            hb = _norm_fwd(x_ref[...], g_ref[...]).astype(BF16)
            hs[...] = hb
            h_ref[...] = hb
            acc[...] = jnp.zeros_like(acc)

        h = hs[...]
        a = _nt(h, w1_ref[...])
        b = _nt(h, w3_ref[...])
        a_ref[...] = a.astype(BF16)
        b_ref[...] = b.astype(BF16)
        u = a * jax.nn.sigmoid(a) * b
        acc[...] += _nn(u.astype(BF16), w2_ref[...])

        @pl.when(j == nj - 1)
        def _():
            xo_ref[...] = x_ref[...] + 0.5 * acc[...]

    in_specs = [pl.BlockSpec((tm, D), lambda i, j: (i, 0)),
                pl.BlockSpec((1, D), lambda i, j: (0, 0)),
                pl.BlockSpec((tf, D), lambda i, j: (j, 0), pipeline_mode=pl.Buffered(1)),
                pl.BlockSpec((tf, D), lambda i, j: (j, 0), pipeline_mode=pl.Buffered(1)),
                pl.BlockSpec((tf, D), lambda i, j: (j, 0), pipeline_mode=pl.Buffered(1))]
    out_specs = [pl.BlockSpec((tm, D), lambda i, j: (i, 0)),
                 pl.BlockSpec((tm, D), lambda i, j: (i, 0)),
                 pl.BlockSpec((tm, tf), lambda i, j: (i, j)),
                 pl.BlockSpec((tm, tf), lambda i, j: (i, j))]
    out_shape = [jax.ShapeDtypeStruct((S, D), F32), jax.ShapeDtypeStruct((S, D), BF16),
                 jax.ShapeDtypeStruct((S, F), BF16), jax.ShapeDtypeStruct((S, F), BF16)]
    scratch = [pltpu.VMEM((tm, D), BF16), pltpu.VMEM((tm, D), F32)]
    operands = [x, g, w1t, w3t, w2]
    if gather is not None:
        R, C = gather.shape
        in_specs.append(ANY)
        out_specs.append(ANY)
        out_shape.append(jax.ShapeDtypeStruct((N_CHIPS, 2, R // 2, C), gather.dtype))
        scratch += [pltpu.SemaphoreType.DMA((6,)), pltpu.SemaphoreType.DMA((6,))]
        operands.append(gather.reshape(2, R // 2, C))
    outs = list(pl.pallas_call(
        body, name=name, grid=(ni, nj), in_specs=in_specs, out_specs=out_specs, out_shape=out_shape,
        scratch_shapes=scratch, compiler_params=_cp("arbitrary", "arbitrary"),
    )(*operands))
    if gather is not None:
        outs[4] = outs[4].reshape(N_CHIPS, R, C)
    return outs


def _ffn_bwd(dxo, x, g, a, b, w1t, w3t, w2, name, scatter=None):
    S, D = x.shape
    F = w2.shape[0]
    tm, tf = min(FFN_TOKEN_TILE, S), FF_BWD_TILE
    ni, nj = S // tm, F // tf

    def body(dxo_ref, x_ref, g_ref, a_ref, b_ref, w1_ref, w3_ref, w2_ref, *rest):
        if scatter is None:
            dx_ref, dg_ref, dz_ref, da_ref, db_ref, u_ref, dzs, acc = rest
        else:
            (parts_ref, dx_ref, dg_ref, dz_ref, da_ref, db_ref, u_ref, recv_ref,
             dzs, acc, send_sems, recv_sems) = rest
        i, j = pl.program_id(0), pl.program_id(1)
        if scatter is not None:
            for when, phase in ((jnp.logical_and(i == 0, j == 0), "start"),
                                (jnp.logical_and(i == ni - 1, j == nj - 1), "finish")):
                @pl.when(when)
                def _():
                    getattr(_scatter_exchange(parts_ref, recv_ref, send_sems, recv_sems), phase)()

        @pl.when(j == 0)
        def _():
            dzb = (0.5 * dxo_ref[...]).astype(BF16)
            dzs[...] = dzb
            dz_ref[...] = dzb
            acc[...] = jnp.zeros_like(acc)

        du = _nt(dzs[...], w2_ref[...])
        av = a_ref[...].astype(F32)
        bv = b_ref[...].astype(F32)
        s = jax.nn.sigmoid(av)
        silu = av * s
        db = (du * silu).astype(BF16)
        da = (du * bv * (s * (1.0 + av * (1.0 - s)))).astype(BF16)
        da_ref[...] = da
        db_ref[...] = db
        u_ref[...] = (silu * bv).astype(BF16)
        acc[...] += _nn(da, w1_ref[...]) + _nn(db, w3_ref[...])

        @pl.when(j == nj - 1)
        def _():
            dx, dg = _norm_bwd(x_ref[...], g_ref[...], acc[...])
            dx_ref[...] = dxo_ref[...] + dx

            @pl.when(i == 0)
            def _():
                dg_ref[...] = dg

            @pl.when(i > 0)
            def _():
                dg_ref[...] += dg

    row = pl.BlockSpec((tm, D), lambda i, j: (i, 0))
    wsp = pl.BlockSpec((tf, D), lambda i, j: (j, 0))
    col = pl.BlockSpec((tm, tf), lambda i, j: (i, j))
    vec = pl.BlockSpec((1, D), lambda i, j: (0, 0))
    col_out = pl.BlockSpec((tm, tf), lambda i, j: (i, j))
    in_specs = [row, row, vec, col, col, wsp, wsp, wsp]
    out_specs = [row, vec, row, col_out, col_out, col_out]
    out_shape = [jax.ShapeDtypeStruct((S, D), F32), jax.ShapeDtypeStruct((1, D), F32),
                 jax.ShapeDtypeStruct((S, D), BF16), jax.ShapeDtypeStruct((S, F), BF16),
                 jax.ShapeDtypeStruct((S, F), BF16), jax.ShapeDtypeStruct((S, F), BF16)]
    scratch = [pltpu.VMEM((tm, D), BF16), pltpu.VMEM((tm, D), F32)]
    operands = [dxo, x, g, a, b, w1t, w3t, w2]
    if scatter is not None:
        in_specs.append(ANY)
        out_specs.append(ANY)
        out_shape.append(jax.ShapeDtypeStruct((3,) + scatter.shape[1:], scatter.dtype))
        scratch += [pltpu.SemaphoreType.DMA((3,)), pltpu.SemaphoreType.DMA((3,))]
        operands.append(scatter)
    return pl.pallas_call(
        body, name=name, grid=(ni, nj), in_specs=in_specs, out_specs=out_specs, out_shape=out_shape,
        scratch_shapes=scratch, compiler_params=_cp("arbitrary", "arbitrary"),
    )(*operands)


def _tn_matmul(a, b, name):
    S, M = a.shape
    N = b.shape[1]
    ts = min(WGRAD_TOKEN_TILE, S)
    tmm = next(t for t in WGRAD_ROW_TILES if M % t == 0)
    ns = S // ts

    def body(a_ref, b_ref, o_ref, acc):
        s = pl.program_id(1)
        part = _tn(a_ref[...], b_ref[...])

        @pl.when(s == 0)
        def _():
            acc[...] = part

        @pl.when(s > 0)
        def _():
            acc[...] += part

        @pl.when(s == ns - 1)
        def _():
            o_ref[...] = acc[...].astype(BF16)

    return pl.pallas_call(
        body, name=name, grid=(M // tmm, ns),
        in_specs=[pl.BlockSpec((ts, tmm), lambda m, s: (s, m)),
                  pl.BlockSpec((ts, N), lambda m, s: (s, 0))],
        out_specs=pl.BlockSpec((tmm, N), lambda m, s: (m, 0)),
        out_shape=jax.ShapeDtypeStruct((M, N), BF16),
        scratch_shapes=[pltpu.VMEM((tmm, N), F32)],
        compiler_params=_cp("arbitrary", "arbitrary"),
    )(a, b)


def _norm_matmul_nt(x, g, wt, out_dtype, name):
    S, D = x.shape
    N = wt.shape[0]
    tm = min(MATMUL_TOKEN_TILE, S)
    tn = next(t for t in (1024, 768, 256) if N % t == 0)

    def body(x_ref, g_ref, w_ref, o_ref, h_ref, hs):
        @pl.when(pl.program_id(1) == 0)
        def _():
            hb = _norm_fwd(x_ref[...], g_ref[...]).astype(BF16)
            hs[...] = hb
            h_ref[...] = hb

        o_ref[...] = _nt(hs[...], w_ref[...]).astype(out_dtype)

    return pl.pallas_call(
        body, name=name, grid=(S // tm, N // tn),
        in_specs=[pl.BlockSpec((tm, D), lambda i, j: (i, 0)),
                  pl.BlockSpec((1, D), lambda i, j: (0, 0)),
                  pl.BlockSpec((tn, D), lambda i, j: (j, 0))],
        out_specs=[pl.BlockSpec((tm, tn), lambda i, j: (i, j)),
                   pl.BlockSpec((tm, D), lambda i, j: (i, 0))],
        out_shape=[jax.ShapeDtypeStruct((S, N), out_dtype), jax.ShapeDtypeStruct((S, D), BF16)],
        scratch_shapes=[pltpu.VMEM((tm, D), BF16)],
        compiler_params=_cp("arbitrary", "arbitrary"),
    )(x, g, wt)


def _heads_tile(ref):
    Hh, nbk = ref.shape[0], ref.shape[1]
    return jnp.concatenate([jnp.concatenate([ref[h, b] for b in range(nbk)], axis=1) for h in range(Hh)], axis=0)


def _store_heads(ref, val):
    Hh, nbk, dh, T = ref.shape
    for h in range(Hh):
        for b in range(nbk):
            ref[h, b] = val[h * dh:(h + 1) * dh, b * T:(b + 1) * T].astype(ref.dtype)


def _norm_proj_heads(x, g, w_rows, T, name):
    S, D = x.shape
    N = w_rows.shape[0]
    tm, tn = min(MATMUL_TOKEN_TILE, S), PROJ_HEAD_ROWS

    def body(x_ref, g_ref, w_ref, o_ref, hs):
        @pl.when(pl.program_id(1) == 0)
        def _():
            hs[...] = _norm_fwd(x_ref[...], g_ref[...]).astype(BF16)

        _store_heads(o_ref, _nt(w_ref[...], hs[...]))

    return pl.pallas_call(
        body, name=name, grid=(S // tm, N // tn),
        in_specs=[pl.BlockSpec((tm, D), lambda i, j: (i, 0)),
                  pl.BlockSpec((1, D), lambda i, j: (0, 0)),
                  pl.BlockSpec((tn, D), lambda i, j: (j, 0))],
        out_specs=pl.BlockSpec((tn // HEAD_DIM, tm // T, HEAD_DIM, T), lambda i, j: (j, i, 0, 0)),
        out_shape=jax.ShapeDtypeStruct((N // HEAD_DIM, S // T, HEAD_DIM, T), BF16),
        scratch_shapes=[pltpu.VMEM((tm, D), BF16)],
        compiler_params=_cp("arbitrary", "arbitrary"),
    )(x, g, w_rows)


def _heads_matmul(pieces, b, name):
    S, N = b.shape
    ts = min(MATMUL_TOKEN_TILE, S)
    ns = S // ts
    rows = [at.shape[0] * at.shape[2] for at, _ in pieces]

    def body(*refs):
        a_refs, b_ref, o_ref = refs[:-2], refs[-2], refs[-1]
        s = pl.program_id(0)
        row0 = 0
        for a_ref, (_, scale), n in zip(a_refs, pieces, rows):
            a = _heads_tile(a_ref)
            part = _nn((a if scale == 1.0 else a * scale).astype(BF16), b_ref[...])
            out = o_ref.at[row0:row0 + n, :]
            row0 += n

            @pl.when(s == 0)
            def _():
                out[...] = part

            @pl.when(s > 0)
            def _():
                out[...] += part

    return pl.pallas_call(
        body, name=name, grid=(ns,),
        in_specs=[pl.BlockSpec((at.shape[0], ts // at.shape[3], at.shape[2], at.shape[3]), lambda s: (0, s, 0, 0))
                  for at, _ in pieces] + [pl.BlockSpec((ts, N), lambda s: (s, 0))],
        out_specs=pl.BlockSpec((sum(rows), N), lambda s: (0, 0)),
        out_shape=jax.ShapeDtypeStruct((sum(rows), N), F32),
        compiler_params=_cp("arbitrary"),
    )(*[at for at, _ in pieces], b)


def _proj_bwd(pieces, dgates, w_rows, x, g, dres):
    S, D = x.shape
    tm = min(TOKEN_TILE, S)
    n_p = len(pieces)
    gate_row = w_rows.shape[0] - dgates.shape[1]

    def body(*refs):
        p_refs = refs[:n_p]
        dgt_ref, w_ref, x_ref, g_ref, dres_ref, dx_ref, dg_ref = refs[n_p:]
        i = pl.program_id(0)
        dh = _nn(dgt_ref[...], w_ref[gate_row:, :])
        for p_ref, (arr, row0) in zip(p_refs, pieces):
            rows = arr.shape[0] * arr.shape[2]
            dh += _tn(_heads_tile(p_ref).astype(BF16), w_ref[row0:row0 + rows, :])
        dx, dg = _norm_bwd(x_ref[...], g_ref[...], dh)
        dx_ref[...] = dres_ref[...] + dx

        @pl.when(i == 0)
        def _():
            dg_ref[...] = dg

        @pl.when(i > 0)
        def _():
            dg_ref[...] += dg

    row = pl.BlockSpec((tm, D), lambda i: (i, 0))
    vec = pl.BlockSpec((1, D), lambda i: (0, 0))
    p_specs = [pl.BlockSpec((a.shape[0], tm // a.shape[3], a.shape[2], a.shape[3]), lambda i: (0, i, 0, 0))
               for a, _ in pieces]
    return pl.pallas_call(
        body, name="proj_bwd", grid=(S // tm,),
        in_specs=p_specs + [pl.BlockSpec((tm, dgates.shape[1]), lambda i: (i, 0)),
                            pl.BlockSpec(w_rows.shape, lambda i: (0, 0), pipeline_mode=pl.Buffered(1)),
                            row, vec, row],
        out_specs=[row, vec],
        out_shape=[jax.ShapeDtypeStruct((S, D), F32), jax.ShapeDtypeStruct((1, D), F32)],
        compiler_params=_cp("arbitrary"),
    )(*[a for a, _ in pieces], dgates, w_rows, x, g, dres)


def _merge_fwd(x1, gates, oa_t, ob_t, wat, wbt, w_out):
    S, D = x1.shape
    W = wat.shape[1]
    tm = min(TOKEN_TILE, S)

    def body(x_ref, ga_ref, gb_ref, oa_ref, ob_ref, wa_ref, wb_ref, wo_ref,
             x2_ref, mg_ref, ba_ref, bb_ref):
        ba = _nt(_heads_tile(oa_ref).T.astype(BF16), wa_ref[...])
        bb = _nt(_heads_tile(ob_ref).T.astype(BF16), wb_ref[...])
        merged = jax.nn.sigmoid(ga_ref[...]) * ba + jax.nn.sigmoid(gb_ref[...]) * bb
        mb = merged.astype(BF16)
        mg_ref[...] = mb
        ba_ref[...] = ba.astype(BF16)
        bb_ref[...] = bb.astype(BF16)
        x2_ref[...] = x_ref[...] + _nn(mb, wo_ref[...])

    row = pl.BlockSpec((tm, D), lambda i: (i, 0))
    full = lambda r, c: pl.BlockSpec((r, c), lambda i: (0, 0))
    heads = lambda a: pl.BlockSpec((a.shape[0], tm // a.shape[3], a.shape[2], a.shape[3]), lambda i: (0, i, 0, 0))
    return pl.pallas_call(
        body, name="merge_fwd", grid=(S // tm,),
        in_specs=[row, pl.BlockSpec((tm, D), lambda i: (i, 0)), pl.BlockSpec((tm, D), lambda i: (i, 1)),
                  heads(oa_t), heads(ob_t), full(D, W), full(D, W), full(D, D)],
        out_specs=[row, row, row, row],
        out_shape=[jax.ShapeDtypeStruct((S, D), F32)] + [jax.ShapeDtypeStruct((S, D), BF16)] * 3,
        compiler_params=_cp("arbitrary"),
    )(x1, gates, gates, oa_t, ob_t, wat, wbt, w_out)


def _merge_bwd(dx2, gates, ba, bb, wa, wb, w_out, t_a, t_b):
    S, D = dx2.shape
    W = wa.shape[0]
    tm = min(TOKEN_TILE, S)
    Hh = W // HEAD_DIM

    def body(dx_ref, ga_ref, gb_ref, ba_ref, bb_ref, wa_ref, wb_ref, wo_ref,
             dxb_ref, dba_ref, dbb_ref, dgt_ref, doa_ref, dob_ref):
        dxb = dx_ref[...].astype(BF16)
        dxb_ref[...] = dxb
        dm = _nt(dxb, wo_ref[...])
        sa = jax.nn.sigmoid(ga_ref[...])
        sb = jax.nn.sigmoid(gb_ref[...])
        dba = (dm * sa).astype(BF16)
        dbb = (dm * sb).astype(BF16)
        dba_ref[...] = dba
        dbb_ref[...] = dbb
        dgt_ref[:, :D] = (dm * ba_ref[...].astype(F32) * sa * (1.0 - sa)).astype(BF16)
        dgt_ref[:, D:] = (dm * bb_ref[...].astype(F32) * sb * (1.0 - sb)).astype(BF16)
        _store_heads(doa_ref, _nt(wa_ref[...], dba))
        _store_heads(dob_ref, _nt(wb_ref[...], dbb))

    row = pl.BlockSpec((tm, D), lambda i: (i, 0))
    full = lambda r, c: pl.BlockSpec((r, c), lambda i: (0, 0))
    heads = lambda T: pl.BlockSpec((Hh, tm // T, HEAD_DIM, T), lambda i: (0, i, 0, 0))
    return pl.pallas_call(
        body, name="merge_bwd", grid=(S // tm,),
        in_specs=[row, pl.BlockSpec((tm, D), lambda i: (i, 0)), pl.BlockSpec((tm, D), lambda i: (i, 1)),
                  row, row, full(W, D), full(W, D), full(D, D)],
        out_specs=[row, row, row, pl.BlockSpec((tm, 2 * D), lambda i: (i, 0)), heads(t_a), heads(t_b)],
        out_shape=[jax.ShapeDtypeStruct((S, D), BF16)] * 3 + [jax.ShapeDtypeStruct((S, 2 * D), BF16),
                   jax.ShapeDtypeStruct((Hh, S // t_a, HEAD_DIM, t_a), F32),
                   jax.ShapeDtypeStruct((Hh, S // t_b, HEAD_DIM, t_b), BF16)],
        compiler_params=_cp("arbitrary"),
    )(dx2, gates, gates, ba, bb, wa, wb, w_out)


def _final_loss(x3, gf, target):
    S, D = x3.shape
    tm = min(TOKEN_TILE, S)

    def body(x_ref, g_ref, t_ref, loss_ref, dx_ref, dg_ref):
        i = pl.program_id(0)
        x = x_ref[...]
        g = g_ref[...]
        e = _norm_fwd(x, g) - t_ref[...]
        part = 0.5 * jnp.sum(jnp.mean(e * e, axis=-1, keepdims=True), axis=0, keepdims=True)
        dx, dg = _norm_bwd(x, g, e * (1.0 / D))
        dx_ref[...] = dx

        @pl.when(i == 0)
        def _():
            loss_ref[...] = part
            dg_ref[...] = dg

        @pl.when(i > 0)
        def _():
            loss_ref[...] += part
            dg_ref[...] += dg

    row = pl.BlockSpec((tm, D), lambda i: (i, 0))
    vec = pl.BlockSpec((1, D), lambda i: (0, 0))
    return pl.pallas_call(
        body, name="final_loss", grid=(S // tm,),
        in_specs=[row, vec, row],
        out_specs=[pl.BlockSpec((1, 1), lambda i: (0, 0)), row, vec],
        out_shape=[jax.ShapeDtypeStruct((1, 1), F32), jax.ShapeDtypeStruct((S, D), F32),
                   jax.ShapeDtypeStruct((1, D), F32)],
        compiler_params=_cp("arbitrary"),
    )(x3, gf, target)


SB_FWD_HEAD_GROUP = 8
SB_HEAD_GROUP = 4
LANES = 128


def _tri(T, kind):
    r = lax.broadcasted_iota(jnp.int32, (T, T), 0)
    c = lax.broadcasted_iota(jnp.int32, (T, T), 1)
    return {"after": r > c, "before": r < c}[kind].astype(BF16)


def _lane(v, j):
    return jnp.broadcast_to(v[:, j:j + 1], (v.shape[0], LANES))


def _t_bf16(x):
    return x.astype(F32).T.astype(BF16)


def _wide(v, T):
    return jnp.tile(v, (1, T // LANES))


SB_SLOTS = 3
SB_FWD_SLOTS = 2
COPY_PARTS = 4


class _split_copy:
    def __init__(self, src, dst, sems):
        n = src.shape[0] // COPY_PARTS
        self.parts = [pltpu.make_async_copy(src.at[pl.ds(r * n, n)], dst.at[pl.ds(r * n, n)], sems.at[r])
                      for r in range(COPY_PARTS)]

    def start(self):
        for cp in self.parts:
            cp.start()

    def wait(self):
        for cp in self.parts:
            cp.wait()


def _sb_pair(i, kb):
    return (i * (i + 1)) // 2 + kb


def _sb_fwd(qkv):
    H3, nb, dh, T = qkv.shape
    H = H3 // 3
    HG = SB_FWD_HEAD_GROUP
    assert HG == H, "one head group: a saved tile holds all the heads"
    n_pairs = (nb * (nb + 1)) // 2

    def body(q_ref, k_ref, v_ref, o_ref, saved_ref, stage, sems):
        row = lax.broadcasted_iota(jnp.int32, (T, T), 0)
        col = lax.broadcasted_iota(jnp.int32, (T, T), 1)
        tri = col < row
        after = _tri(T, "after")

        def save(slot, pair):
            return _split_copy(stage.at[slot], saved_ref.at[pair], sems.at[slot])

        def blocks(qs, i, kb, step, carry, diag):
            hs = range(HG)
            slot = step % SB_FWD_SLOTS

            @pl.when(step >= SB_FWD_SLOTS)
            def _():
                save(slot, 0).wait()

            halves = [slice(0, T // 2), slice(T // 2, T)]
            z = [[_nn(qs[hh][r], k_ref[hh, kb]) for r in halves] for hh in hs]
            res, ls, first = [[[None, None] for _ in hs] for _ in range(3)]
            for hh in hs:
                for n, r in enumerate(halves):
                    sp = _softplus(z[hh][n])
                    if diag:
                        sp = jnp.where(tri[r], sp, 0.0)
                    ls[hh][n] = z[hh][n] - sp
                    spb = sp.astype(BF16)
                    first[hh][n] = _lane(spb.astype(F32), 0)
                    res[hh][n] = _nn(spb, after)
            out = []
            for hh in hs:
                c, oacc = carry[2 * hh], carry[2 * hh + 1]
                abs_, cs = [], []
                for n, r in enumerate(halves):
                    a = jnp.exp(ls[hh][n] - (res[hh][n] + _wide(c[r], T)))
                    if diag:
                        a = jnp.where(tri[r], a, 0.0)
                    ab = a.astype(BF16)
                    stage[slot, hh, 0, r, :] = ab
                    stage[slot, hh, 1, r, :] = jnp.exp(ls[hh][n]).astype(BF16)
                    abs_.append(ab)
                    cs.append(c[r] + (first[hh][n] + _lane(res[hh][n], 0)))
                out.extend([jnp.concatenate(cs, axis=0),
                            oacc + _nt(v_ref[hh, kb], jnp.concatenate(abs_, axis=0))])
            save(slot, _sb_pair(i, kb)).start()
            return tuple(out)

        def qblock(i, step):
            qs = [_t_bf16(q_ref[hh, i]) for hh in range(HG)]
            carry = blocks(qs, i, i, step, (jnp.zeros((T, LANES), F32), jnp.zeros((dh, T), F32)) * HG, True)

            def kstep(t, carry):
                return blocks(qs, i, i - 1 - t, step + 1 + t, carry, False)

            carry = lax.fori_loop(0, i, kstep, carry)
            for hh in range(HG):
                o_ref[hh, i] = carry[2 * hh + 1]
            return step + 1 + i

        lax.fori_loop(0, nb, qblock, 0)
        for slot in range(min(SB_FWD_SLOTS, n_pairs)):
            save(slot, 0).wait()

    ht = lambda part: pl.BlockSpec((HG, nb, dh, T), lambda h: (part, 0, 0, 0), pipeline_mode=pl.Buffered(1))
    return pl.pallas_call(
        body, name="sb_fwd", grid=(1,),
        in_specs=[ht(0), ht(1), ht(2)],
        out_specs=[ht(0), ANY],
        out_shape=[jax.ShapeDtypeStruct((H, nb, dh, T), F32),
                   jax.ShapeDtypeStruct((n_pairs, H, 2, T, T), BF16)],
        scratch_shapes=[pltpu.VMEM((SB_FWD_SLOTS, HG, 2, T, T), BF16),
                        pltpu.SemaphoreType.DMA((SB_FWD_SLOTS, COPY_PARTS))],
        compiler_params=_cp("arbitrary"),
    )(qkv, qkv, qkv)


def _sb_bwd(qkv, dot, saved):
    H3, nb, dh, T = qkv.shape
    H = H3 // 3
    HG = SB_HEAD_GROUP
    n_pairs = (nb * (nb + 1)) // 2

    def body(qt_ref, k_ref, v_ref, dot_ref, saved_ref, dq_ref, dk_ref, dv_ref, stage, sems):
        head0 = pl.program_id(0) * HG
        row = lax.broadcasted_iota(jnp.int32, (T, T), 0)
        col = lax.broadcasted_iota(jnp.int32, (T, T), 1)
        tri = col < row
        before = _tri(T, "before")
        dk_ref[...] = jnp.zeros_like(dk_ref)
        dv_ref[...] = jnp.zeros_like(dv_ref)

        def fetch(slot, pair):
            return _split_copy(saved_ref.at[pair, pl.ds(head0, HG)], stage.at[slot], sems.at[slot])

        for ahead in range(min(SB_SLOTS - 1, n_pairs)):
            fetch(ahead, ahead).start()

        def blocks(qTs, dos, doTs, i, kb, carry, diag):
            hs = range(HG)
            pair = _sb_pair(i, kb)
            slot = pair % SB_SLOTS
            fetch(slot, pair).wait()
            nxt = pair + (SB_SLOTS - 1)

            @pl.when(nxt < n_pairs)
            def _():
                fetch(nxt % SB_SLOTS, nxt).start()

            kT = [k_ref[hh, kb] for hh in hs]
            da = [_nn(dos[hh], v_ref[hh, kb]) for hh in hs]
            g, gb, resg = [None] * HG, [None] * HG, [None] * HG
            for hh in hs:
                g[hh] = stage[slot, hh, 0].astype(F32) * da[hh]
                gb[hh] = g[hh].astype(BF16)
                resg[hh] = _nn(gb[hh], before)
            out = []
            for hh in hs:
                pre_g, dq = carry[2 * hh], carry[2 * hh + 1]
                dz = g[hh] - (g[hh] + (resg[hh] + _wide(pre_g, T))) * stage[slot, hh, 1].astype(F32)
                if diag:
                    dz = jnp.where(tri, dz, 0.0)
                dzb = dz.astype(BF16)
                dk_ref[hh, kb] += _nn(qTs[hh], dzb)
                dv_ref[hh, kb] += _nn(doTs[hh], stage[slot, hh, 0])
                out.extend([pre_g + (_lane(resg[hh], T - 1) + _lane(gb[hh].astype(F32), T - 1)),
                            dq + _nt(kT[hh], dzb)])
            return tuple(out)

        def qblock(i, _):
            qTs = [qt_ref[hh, i] for hh in range(HG)]
            doTs = [dot_ref[hh, i] for hh in range(HG)]
            dos = [_t_bf16(v) for v in doTs]
            carry = (jnp.zeros((T, LANES), F32), jnp.zeros((dh, T), F32)) * HG

            def kstep(kb, carry):
                return blocks(qTs, dos, doTs, i, kb, carry, False)

            carry = lax.fori_loop(0, i, kstep, carry)
            carry = blocks(qTs, dos, doTs, i, i, carry, True)
            for hh in range(HG):
                dq_ref[hh, i] = carry[2 * hh + 1]
            return 0

        lax.fori_loop(0, nb, qblock, 0)

    G = H // HG
    ht = lambda part: pl.BlockSpec((HG, nb, dh, T), lambda h: (h + part * G, 0, 0, 0),
                                   pipeline_mode=pl.Buffered(1))
    return pl.pallas_call(
        body, name="sb_bwd", grid=(G,),
        in_specs=[ht(0), ht(1), ht(2), ht(0), ANY],
        out_specs=[ht(0), ht(0), ht(0)],
        out_shape=[jax.ShapeDtypeStruct((H, nb, dh, T), F32)] * 3,
        scratch_shapes=[pltpu.VMEM((SB_SLOTS, HG, 2, T, T), BF16), pltpu.SemaphoreType.DMA((SB_SLOTS, COPY_PARTS))],
        compiler_params=_cp("arbitrary"),
    )(qkv, qkv, qkv, dot, saved)


def _swa_probs(zp, zc, bias, sink, first):
    T = zp.shape[0]
    key = lax.broadcasted_iota(jnp.int32, (T, T), 0)
    qry = lax.broadcasted_iota(jnp.int32, (T, T), 1)
    lp = jnp.where(jnp.logical_and(key > qry, jnp.logical_not(first)), zp + bias[:T, :], NEG_BIG)
    lc = jnp.where(key <= qry, zc + bias[T:, :], NEG_BIG)
    m = jnp.maximum(jnp.maximum(jnp.max(lp, axis=0, keepdims=True), jnp.max(lc, axis=0, keepdims=True)), sink)
    pp = jnp.exp(lp - m)
    pc = jnp.exp(lc - m)
    ps = jnp.exp(sink - m)
    inv = 1.0 / (jnp.sum(pp, axis=0, keepdims=True) + jnp.sum(pc, axis=0, keepdims=True) + ps)
    return pp * inv, pc * inv, ps * inv


def _swa_specs(nb, dh, T, Hq, Hkv, clamp):
    blk = (lambda n: jnp.minimum(n, nb - 1)) if clamp else (lambda n: n)
    q = pl.BlockSpec((Hq, None, dh, T), lambda n: (0, blk(n), 0, 0))
    kv = lambda first, back: pl.BlockSpec(
        (Hkv, None, dh, T), lambda n: (first // Hkv, jnp.maximum(blk(n) - back, 0) if back else blk(n), 0, 0))
    return q, [kv(Hq, 1), kv(Hq, 0), kv(Hq + Hkv, 1), kv(Hq + Hkv, 0)]


def _swa_fwd(qkv, bias, sinks):
    Hq, Hkv, grp = SWA_Q_HEADS, SWA_KV_HEADS, SWA_GROUP
    _, nb, dh, T = qkv.shape

    def body(sink_ref, q_ref, kp_ref, kc_ref, vp_ref, vc_ref, bias_ref, o_ref):
        n = pl.program_id(0)
        kpn = [_t_bf16(kp_ref[hk]) for hk in range(Hkv)]
        kcn = [_t_bf16(kc_ref[hk]) for hk in range(Hkv)]
        zs = [(_nn(kpn[h // grp], q_ref[h]), _nn(kcn[h // grp], q_ref[h])) for h in range(Hq)]
        for h in range(Hq):
            pp, pc, _ = _swa_probs(*zs[h], bias_ref[h], sink_ref[h], n == 0)
            o_ref[h] = _nn(vp_ref[h // grp], pp.astype(BF16)) + _nn(vc_ref[h // grp], pc.astype(BF16))

    q_spec, kv_specs = _swa_specs(nb, dh, T, Hq, Hkv, False)
    return pl.pallas_call(
        body, name="swa_fwd", grid=(nb,),
        in_specs=[pl.BlockSpec(memory_space=pltpu.SMEM), q_spec] + kv_specs
                 + [pl.BlockSpec((Hq, 2 * T, T), lambda n: (0, 0, 0))],
        out_specs=pl.BlockSpec((Hq, None, dh, T), lambda n: (0, n, 0, 0)),
        out_shape=jax.ShapeDtypeStruct((Hq, nb, dh, T), F32),
        compiler_params=_cp("arbitrary"),
    )(sinks, qkv, qkv, qkv, qkv, qkv, bias)


def _swa_bwd(qkv, bias, sinks, dot, ot):
    Hq, Hkv, grp = SWA_Q_HEADS, SWA_KV_HEADS, SWA_GROUP
    _, nb, dh, T = qkv.shape

    def body(sink_ref, qt_ref, kp_ref, kc_ref, vp_ref, vc_ref, bias_ref, dot_ref, ot_ref,
             dq_ref, dk_ref, dv_ref, dbias_ref, dsink_ref, ck, cv):
        n = pl.program_id(0)

        @pl.when(n == 0)
        def _():
            dbias_ref[...] = jnp.zeros_like(dbias_ref)
            dsink_ref[...] = jnp.zeros_like(dsink_ref)
            ck[...] = jnp.zeros_like(ck)
            cv[...] = jnp.zeros_like(cv)

        @pl.when(n < nb)
        def _():
            kp, kc = [kp_ref[hk] for hk in range(Hkv)], [kc_ref[hk] for hk in range(Hkv)]
            kpn, kcn = [_t_bf16(v) for v in kp], [_t_bf16(v) for v in kc]
            vpn = [_t_bf16(vp_ref[hk]) for hk in range(Hkv)]
            vcn = [_t_bf16(vc_ref[hk]) for hk in range(Hkv)]
            qTs = [qt_ref[h] for h in range(Hq)]
            doTs = [dot_ref[h].astype(BF16) for h in range(Hq)]
            zs = [(_nn(kpn[h // grp], qTs[h]), _nn(kcn[h // grp], qTs[h])) for h in range(Hq)]
            dps = [(_nn(vpn[h // grp], doTs[h]), _nn(vcn[h // grp], doTs[h])) for h in range(Hq)]
            dls, pbs = [], []
            for h in range(Hq):
                pp, pc, ps = _swa_probs(*zs[h], bias_ref[h], sink_ref[h], n == 0)
                delta = jnp.sum(dot_ref[h] * ot_ref[h], axis=0, keepdims=True)
                dlp = pp * (dps[h][0] - delta)
                dlc = pc * (dps[h][1] - delta)
                dbias_ref[h, :T, :] += dlp
                dbias_ref[h, T:, :] += dlc
                dsink_ref[h] += -ps * delta
                dls.append((dlp.astype(BF16), dlc.astype(BF16)))
                pbs.append((pp.astype(BF16), pc.astype(BF16)))
            zero = jnp.zeros((dh, T), F32)
            kprev, kcur, vprev, vcur = [zero] * Hkv, [zero] * Hkv, [zero] * Hkv, [zero] * Hkv
            for h in range(Hq):
                hk = h // grp
                dlpb, dlcb = dls[h]
                dq_ref[h] = _nn(kp[hk], dlpb) + _nn(kc[hk], dlcb)
                kprev[hk] = kprev[hk] + _nt(qTs[h], dlpb)
                kcur[hk] = kcur[hk] + _nt(qTs[h], dlcb)
                vprev[hk] = vprev[hk] + _nt(doTs[h], pbs[h][0])
                vcur[hk] = vcur[hk] + _nt(doTs[h], pbs[h][1])
            for hk in range(Hkv):
                dk_ref[hk] = ck[hk] + kprev[hk]
                dv_ref[hk] = cv[hk] + vprev[hk]
                ck[hk] = kcur[hk]
                cv[hk] = vcur[hk]

        @pl.when(n == nb)
        def _():
            dk_ref[...] = ck[...]
            dv_ref[...] = cv[...]

    qt_spec, kv_specs = _swa_specs(nb, dh, T, Hq, Hkv, True)
    prev = pl.BlockSpec((Hkv, None, dh, T), lambda n: (0, jnp.maximum(n - 1, 0), 0, 0))
    whole = lambda a, b: pl.BlockSpec((Hq, a, b), lambda n: (0, 0, 0))
    return pl.pallas_call(
        body, name="swa_bwd", grid=(nb + 1,),
        in_specs=[pl.BlockSpec(memory_space=pltpu.SMEM), qt_spec] + kv_specs
                 + [whole(2 * T, T), qt_spec, qt_spec],
        out_specs=[qt_spec, prev, prev, whole(2 * T, T), whole(1, T)],
        out_shape=[jax.ShapeDtypeStruct((Hq, nb, dh, T), F32), jax.ShapeDtypeStruct((Hkv, nb, dh, T), F32),
                   jax.ShapeDtypeStruct((Hkv, nb, dh, T), F32), jax.ShapeDtypeStruct((Hq, 2 * T, T), F32),
                   jax.ShapeDtypeStruct((Hq, 1, T), F32)],
        scratch_shapes=[pltpu.VMEM((Hkv, dh, T), F32), pltpu.VMEM((Hkv, dh, T), F32)],
        compiler_params=_cp("arbitrary"),
    )(sinks, qkv, qkv, qkv, qkv, qkv, bias, dot, ot)


def _split3(x):
    h1 = x.astype(BF16)
    r1 = x - h1.astype(F32)
    h2 = r1.astype(BF16)
    h3 = (r1 - h2.astype(F32)).astype(BF16)
    return h1, h2, h3


def _bias_expand(rel_t, onehot):
    Hq, NB = rel_t.shape
    L = onehot.shape[1]

    def body(r_ref, oh_ref, o_ref):
        h1, h2, h3 = _split3(r_ref[...])
        oh = oh_ref[...]
        o_ref[...] = _nn(h1, oh) + _nn(h2, oh) + _nn(h3, oh)

    return pl.pallas_call(
        body, name="bias_expand", grid=(1,),
        in_specs=[pl.BlockSpec((Hq, NB), lambda i: (0, 0)), pl.BlockSpec((NB, L), lambda i: (0, 0))],
        out_specs=pl.BlockSpec((Hq, L), lambda i: (0, 0)),
        out_shape=jax.ShapeDtypeStruct((Hq, L), F32),
        compiler_params=_cp("arbitrary"),
    )(rel_t, onehot)


def _bias_reduce(dbias, onehot):
    Hq, L = dbias.shape
    NB = onehot.shape[0]

    def body(d_ref, oh_ref, o_ref):
        h1, h2, h3 = _split3(d_ref[...])
        oh = oh_ref[...]
        o_ref[...] = _nt(h1, oh) + _nt(h2, oh) + _nt(h3, oh)

    return pl.pallas_call(
        body, name="bias_reduce", grid=(1,),
        in_specs=[pl.BlockSpec((Hq, L), lambda i: (0, 0)), pl.BlockSpec((NB, L), lambda i: (0, 0))],
        out_specs=pl.BlockSpec((Hq, NB), lambda i: (0, 0)),
        out_shape=jax.ShapeDtypeStruct((Hq, NB), F32),
        compiler_params=_cp("arbitrary"),
    )(dbias, onehot)


def _adamw(w, g, m, v, name):
    R, C = w.shape
    tr = 256 if R % 256 == 0 else R
    bc1 = 1.0 - ADAM_B1 ** ADAM_STEP
    bc2 = 1.0 - ADAM_B2 ** ADAM_STEP

    def body(w_ref, g_ref, m_ref, v_ref, d_ref, nm_ref, nv_ref):
        g = g_ref[...]
        m2 = ADAM_B1 * m_ref[...] + (1.0 - ADAM_B1) * g
        v2 = ADAM_B2 * v_ref[...] + (1.0 - ADAM_B2) * (g * g)
        nm_ref[...] = m2
        nv_ref[...] = v2
        d_ref[...] = -ADAM_LR * ((m2 / bc1) / (jnp.sqrt(v2 / bc2) + ADAM_EPS) + ADAM_WD * w_ref[...])

    spec = pl.BlockSpec((tr, C), lambda i: (i, 0))
    return pl.pallas_call(
        body, name=name, grid=(R // tr,),
        in_specs=[spec] * 4, out_specs=[spec] * 3,
        out_shape=[jax.ShapeDtypeStruct((R, C), F32)] * 3,
        compiler_params=_cp("arbitrary"),
    )(w, g, m, v)


def _row_tile(R):
    return max(t for t in range(16, 513, 16) if R % t == 0)


def _add_halves(mine, recv, name):
    K, R, C = mine.shape
    tr = _row_tile(R)

    def body(a_ref, b_ref, o_ref, ob_ref):
        s = a_ref[...].astype(F32) + b_ref[...].astype(F32)
        o_ref[...] = s
        ob_ref[...] = s.astype(BF16)

    spec = pl.BlockSpec((None, tr, C), lambda k, i: (k, i, 0))
    return pl.pallas_call(
        body, name=name, grid=(K, R // tr),
        in_specs=[spec, spec], out_specs=[spec, spec],
        out_shape=[jax.ShapeDtypeStruct((K, R, C), F32), jax.ShapeDtypeStruct((K, R, C), BF16)],
        compiler_params=_cp("arbitrary", "arbitrary"),
    )(mine, recv)


def _add_received(own, recv, name):
    R, C = own.shape
    tr = _row_tile(R)

    def body(a_ref, r_ref, o_ref):
        o_ref[...] = ((a_ref[...] + r_ref[0].astype(F32)) + r_ref[1].astype(F32)) + r_ref[2].astype(F32)

    return pl.pallas_call(
        body, name=name, grid=(R // tr,),
        in_specs=[pl.BlockSpec((tr, C), lambda i: (i, 0)), pl.BlockSpec((3, tr, C), lambda i: (0, i, 0))],
        out_specs=pl.BlockSpec((tr, C), lambda i: (i, 0)),
        out_shape=jax.ShapeDtypeStruct((R, C), F32),
        compiler_params=_cp("arbitrary"),
    )(own, recv)


def _position():
    x, y, c = lax.axis_index("x"), lax.axis_index("y"), lax.axis_index("c")
    others = [(1 - x, y), (x, 1 - y), (1 - x, 1 - y)]
    return x, y, c, others


def _remote(src, dst, send_sems, recv_sems, k, dev):
    return pltpu.make_async_remote_copy(src_ref=src, dst_ref=dst, send_sem=send_sems.at[k],
                                        recv_sem=recv_sems.at[k], device_id=dev, device_id_type=MESH_ID)


class _gather_exchange:
    def __init__(self, src, out, send_sems, recv_sems):
        x, y, c, others = _position()
        mine, sibling = 2 * x + y, (x, y, 1 - c)
        self.sends, self.arrivals, self.passes, self.from_sibling = [], [], [], []
        for j, (ox, oy) in enumerate(others):
            slot = out.at[2 * ox + oy, c]
            theirs = out.at[2 * ox + oy, 1 - c]
            self.sends.append(_remote(src.at[c], out.at[mine, c], send_sems, recv_sems, j, (ox, oy, c)))
            self.arrivals.append(_remote(slot, slot, send_sems, recv_sems, j, (ox, oy, c)))
            self.passes.append(_remote(slot, slot, send_sems, recv_sems, 3 + j, sibling))
            self.from_sibling.append(_remote(theirs, theirs, send_sems, recv_sems, 3 + j, sibling))

    def start(self):
        for cp in self.sends:
            cp.start()

    def forward(self):
        for arrived, onward in zip(self.arrivals, self.passes):
            arrived.wait_recv()
            onward.start()

    def finish(self):
        for cp in self.from_sibling:
            cp.wait_recv()
        for cp in self.sends + self.passes:
            cp.wait_send()


def _gather_weights(shard):
    R, C = shard.shape
    half = R // 2

    def body(src, out, send_sems, recv_sems):
        ex = _gather_exchange(src, out, send_sems, recv_sems)
        ex.start()
        ex.forward()
        ex.finish()

    return pl.pallas_call(
        body, name="gather_weights",
        in_specs=[ANY], out_specs=ANY,
        out_shape=jax.ShapeDtypeStruct((N_CHIPS, 2, half, C), shard.dtype),
        scratch_shapes=[pltpu.SemaphoreType.DMA((6,)), pltpu.SemaphoreType.DMA((6,))],
    )(shard.reshape(2, half, C)).reshape(N_CHIPS, R, C)


def _swap_halves(grads, name):
    K, R, C = grads.shape
    half = R // 2

    def body(src, out, send_sems, recv_sems):
        x, y, c, _ = _position()
        theirs = src.at[:, pl.ds(pl.multiple_of((1 - c) * half, 16), half), :]
        cp = _remote(theirs, out, send_sems, recv_sems, 0, (x, y, 1 - c))
        cp.start()
        cp.wait()

    return pl.pallas_call(
        body, name=name,
        in_specs=[ANY], out_specs=ANY,
        out_shape=jax.ShapeDtypeStruct((K, half, C), grads.dtype),
        scratch_shapes=[pltpu.SemaphoreType.DMA((1,)), pltpu.SemaphoreType.DMA((1,))],
    )(grads)


class _scatter_exchange:
    def __init__(self, src, out, send_sems, recv_sems):
        x, y, c, others = _position()
        self.copies = [_remote(src.at[2 * ox + oy], out.at[j], send_sems, recv_sems, j, (ox, oy, c))
                       for j, (ox, oy) in enumerate(others)]

    def start(self):
        for cp in self.copies:
            cp.start()

    def finish(self):
        for cp in self.copies:
            cp.wait()


def _scatter_to_owners(parts, name):
    K, H, C = parts.shape

    def body(src, out, send_sems, recv_sems):
        ex = _scatter_exchange(src, out, send_sems, recv_sems)
        ex.start()
        ex.finish()

    return pl.pallas_call(
        body, name=name,
        in_specs=[ANY], out_specs=ANY,
        out_shape=jax.ShapeDtypeStruct((3, H, C), parts.dtype),
        scratch_shapes=[pltpu.SemaphoreType.DMA((3,)), pltpu.SemaphoreType.DMA((3,))],
    )(parts)


def _swap_reduced(half_rows, name):
    H, C = half_rows.shape

    def body(src, out, send_sems, recv_sems):
        x, y, c, _ = _position()
        cp = _remote(src, out, send_sems, recv_sems, 0, (x, y, 1 - c))
        cp.start()
        cp.wait()

    return pl.pallas_call(
        body, name=name,
        in_specs=[ANY], out_specs=ANY,
        out_shape=jax.ShapeDtypeStruct((H, C), half_rows.dtype),
        scratch_shapes=[pltpu.SemaphoreType.DMA((1,)), pltpu.SemaphoreType.DMA((1,))],
    )(half_rows)


def _allreduce_small(block):
    R, C = block.shape
    n_dev = 8

    def body(src, out, slots, send_sems, recv_sems):
        x, y, c, _ = _position()
        me = 4 * x + 2 * y + c
        slots[me] = src[...]
        sends = []
        for r in range(1, n_dev):
            peer = (x ^ (r >> 2), y ^ ((r >> 1) & 1), c ^ (r & 1))
            cp = _remote(src, slots.at[me], send_sems, recv_sems, r - 1, peer)
            cp.start()
            sends.append(cp)
        for r in range(1, n_dev):
            theirs = slots.at[me ^ r]
            _remote(theirs, theirs, send_sems, recv_sems, r - 1, (x, y, c)).wait_recv()
        for cp in sends:
            cp.wait_send()
        acc = slots[0]
        for d in range(1, n_dev):
            acc = acc + slots[d]
        out[...] = acc

    return pl.pallas_call(
        body, name="allreduce_small",
        in_specs=[pl.BlockSpec(memory_space=pltpu.VMEM)], out_specs=pl.BlockSpec(memory_space=pltpu.VMEM),
        out_shape=jax.ShapeDtypeStruct((R, C), F32),
        scratch_shapes=[pltpu.VMEM((n_dev, R, C), F32), pltpu.SemaphoreType.DMA((7,)), pltpu.SemaphoreType.DMA((7,))],
    )(block)


def _rel_bucket(dist):
    max_exact = REL_BUCKETS // 2
    d = jnp.maximum(dist, 1).astype(F32)
    large = max_exact + (jnp.log(d / max_exact) / math.log(REL_MAX_DIST / max_exact)
                         * (REL_BUCKETS - max_exact)).astype(jnp.int32)
    large = jnp.minimum(large, REL_BUCKETS - 1)
    return jnp.where(dist < max_exact, dist, large)


def _bucket_onehot():
    T = SWA_BLOCK
    dist = (jnp.arange(T)[None, :] + T) - jnp.arange(2 * T)[:, None]
    bucket = _rel_bucket(jnp.maximum(dist, 0)).reshape(1, T * 2 * T)
    return (bucket == jnp.arange(REL_BUCKETS)[:, None]).astype(BF16)


_BUF = (("ffn1_w1", "t"), ("ffn1_w3", "t"), ("ffn1_w2", "n"), ("ffn2_w1", "t"), ("ffn2_w3", "t"),
        ("ffn2_w2", "n"), ("w_in", "t"), ("w_out", "n"), ("w_branch_swa", "tw"), ("w_branch_sb", "tw"))


def _to_rows(name_kind, w, D):
    kind = name_kind[1]
    if kind == "n":
        return w
    if kind == "t":
        return w.T
    return w.T.reshape(-1, D)


def _from_rows(name_kind, rows, width):
    kind = name_kind[1]
    if kind == "n":
        return rows
    if kind == "t":
        return rows.T
    return rows.reshape(-1, width).T


def kernel(x, norm_ffn1, ffn1_w1, ffn1_w3, ffn1_w2, norm_mix, w_in, swa_sinks, rel_bias, w_branch_swa, w_branch_sb, w_out, norm_ffn2, ffn2_w1, ffn2_w3, ffn2_w2, norm_final, loss_target, m_norm_ffn1, m_ffn1_w1, m_ffn1_w3, m_ffn1_w2, m_norm_mix, m_w_in, m_swa_sinks, m_rel_bias, m_w_branch_swa, m_w_branch_sb, m_w_out, m_norm_ffn2, m_ffn2_w1, m_ffn2_w3, m_ffn2_w2, m_norm_final, v_norm_ffn1, v_ffn1_w1, v_ffn1_w3, v_ffn1_w2, v_norm_mix, v_w_in, v_swa_sinks, v_rel_bias, v_w_branch_swa, v_w_branch_sb, v_w_out, v_norm_ffn2, v_ffn2_w1, v_ffn2_w3, v_ffn2_w2, v_norm_final):
    names = ["norm_ffn1", "ffn1_w1", "ffn1_w3", "ffn1_w2", "norm_mix", "w_in", "swa_sinks", "rel_bias",
             "w_branch_swa", "w_branch_sb", "w_out", "norm_ffn2", "ffn2_w1", "ffn2_w3", "ffn2_w2", "norm_final"]
    W = dict(zip(names, [norm_ffn1, ffn1_w1, ffn1_w3, ffn1_w2, norm_mix, w_in, swa_sinks, rel_bias,
                         w_branch_swa, w_branch_sb, w_out, norm_ffn2, ffn2_w1, ffn2_w3, ffn2_w2, norm_final]))
    M = dict(zip(names, [m_norm_ffn1, m_ffn1_w1, m_ffn1_w3, m_ffn1_w2, m_norm_mix, m_w_in, m_swa_sinks, m_rel_bias,
                         m_w_branch_swa, m_w_branch_sb, m_w_out, m_norm_ffn2, m_ffn2_w1, m_ffn2_w3, m_ffn2_w2,
                         m_norm_final]))
    V = dict(zip(names, [v_norm_ffn1, v_ffn1_w1, v_ffn1_w3, v_ffn1_w2, v_norm_mix, v_w_in, v_swa_sinks, v_rel_bias,
                         v_w_branch_swa, v_w_branch_sb, v_w_out, v_norm_ffn2, v_ffn2_w1, v_ffn2_w3, v_ffn2_w2,
                         v_norm_final]))
    xs = x[0]
    target = loss_target[0]
    S, D = xs.shape
    QW = SWA_Q_HEADS * HEAD_DIM
    KW = SWA_KV_HEADS * HEAD_DIM
    BW = SB_HEADS * HEAD_DIM
    QKV = QW + 2 * KW + 3 * BW

    pieces = [_to_rows(nk, W[nk[0]][0], D) for nk in _BUF]
    sizes = [p.shape[0] for p in pieces]
    offs = [0]
    for s in sizes:
        offs.append(offs[-1] + s)
    n_first = 3
    first_rows = offs[n_first]
    shard_a = jnp.concatenate(pieces[:n_first], axis=0).astype(BF16)
    shard_b = jnp.concatenate(pieces[n_first:], axis=0).astype(BF16)
    chip = 2 * lax.axis_index("x") + lax.axis_index("y")
    gathered_a = lax.dynamic_update_slice(_gather_weights(shard_a), shard_a[None], (chip, 0, 0))
    f1w1, f1w3, f1w2 = [gathered_a[:, offs[i]:offs[i + 1], :].reshape(N_CHIPS * sizes[i], D) for i in range(n_first)]

    g1, gmix, g3 = W["norm_ffn1"], W["norm_mix"], W["norm_ffn2"]
    gf = W["norm_final"].reshape(1, D)

    x1, h1, a1, b1, gathered_b = _ffn_fwd(xs, g1, f1w1, f1w3, f1w2, "ffn1_fwd", gather=shard_b)
    gathered_b = lax.dynamic_update_slice(gathered_b, shard_b[None], (chip, 0, 0))

    def full(i):
        return gathered_b[:, offs[i] - first_rows:offs[i + 1] - first_rows, :].reshape(N_CHIPS * sizes[i], D)

    f2w1, f2w3, f2w2, w_in_t, w_out_f = [full(i) for i in range(n_first, 8)]
    wa_t = full(8).reshape(D, QW)
    wb_t = full(9).reshape(D, BW)
    o0 = QW + 2 * KW
    rows = jnp.arange(w_in_t.shape[0])
    is_q = (rows < QW) | ((rows >= o0) & (rows < o0 + BW))
    w_in_s = w_in_t * jnp.where(is_q, QK_SCALE, 1.0).astype(BF16)[:, None]
    qkv_a = _norm_proj_heads(x1, gmix, w_in_s[:o0], SWA_BLOCK, "proj_swa")
    qkv_b = _norm_proj_heads(x1, gmix, w_in_s[o0:QKV], SB_BLOCK, "proj_sb")
    gates, h2 = _norm_matmul_nt(x1, gmix, w_in_t[QKV:], F32, "proj_gates")

    onehot = _bucket_onehot()
    bias = _bias_expand(W["rel_bias"].T, onehot).reshape(SWA_Q_HEADS, 2 * SWA_BLOCK, SWA_BLOCK)
    sinks = W["swa_sinks"].reshape(SWA_Q_HEADS)
    oa_t = _swa_fwd(qkv_a, bias, sinks)
    ob_t, saved_sb = _sb_fwd(qkv_b)

    x2, merged, ba, bb = _merge_fwd(x1, gates, oa_t, ob_t, wa_t, wb_t, w_out_f)
    x3, h3, a2, b2 = _ffn_fwd(x2, g3, f2w1, f2w3, f2w2, "ffn2_fwd")
    loss_part, dx3, dgf = _final_loss(x3, gf, target)

    dx2, dg3, dz2, da2, db2, u2 = _ffn_bwd(dx3, x2, g3, a2, b2, f2w1, f2w3, f2w2, "ffn2_bwd")
    grads = {}
    grads["ffn2_w1"] = _tn_matmul(da2, h3, "ffn2_dw1")
    grads["ffn2_w3"] = _tn_matmul(db2, h3, "ffn2_dw3")
    grads["ffn2_w2"] = _tn_matmul(u2, dz2, "ffn2_dw2")

    dx2b, dba, dbb, dgates, doa_t, dob_t = _merge_bwd(dx2, gates, ba, bb, wa_t.T, wb_t.T, w_out_f,
                                                      SWA_BLOCK, SB_BLOCK)
    grads["w_out"] = _tn_matmul(merged, dx2b, "dw_out")
    grads["w_branch_swa"] = _heads_matmul([(oa_t, 1.0)], dba, "dw_branch_swa").T
    grads["w_branch_sb"] = _heads_matmul([(ob_t, 1.0)], dbb, "dw_branch_sb").T

    dqb_t, dkb_t, dvb_t = _sb_bwd(qkv_b, dob_t, saved_sb)
    dqa_t, dka_t, dva_t, dbias, dsink_rows = _swa_bwd(qkv_a, bias, sinks, doa_t, oa_t)
    d_rel = _bias_reduce(dbias.reshape(SWA_Q_HEADS, -1), onehot).T
    d_sinks = jnp.sum(dsink_rows, axis=(1, 2))

    dheads = [(dqa_t, QK_SCALE), (dka_t, 1.0), (dva_t, 1.0), (dqb_t, QK_SCALE), (dkb_t, 1.0), (dvb_t, 1.0)]
    grads["w_in"] = jnp.concatenate([_heads_matmul(dheads, h2, "dw_in_heads").astype(BF16),
                                     _tn_matmul(dgates, h2, "dw_in_gates")], axis=0)
    row0, pieces_in = 0, []
    for a, _ in dheads:
        pieces_in.append((a, row0))
        row0 += a.shape[0] * HEAD_DIM
    dx1, dgmix = _proj_bwd(pieces_in, dgates, w_in_s, x1, gmix, dx2)

    c = lax.axis_index("c")

    def reduce_start(lo, hi, tag):
        gbuf = jnp.concatenate([grads[_BUF[i][0]].astype(BF16).reshape(N_CHIPS, sizes[i], D) for i in range(lo, hi)],
                               axis=1)
        half = gbuf.shape[1] // 2
        from_sibling = _swap_halves(gbuf, "swap_halves_" + tag)
        my_half = lax.dynamic_slice_in_dim(gbuf, c * half, half, axis=1)
        return _add_halves(my_half, from_sibling, "add_sibling_" + tag)

    def reduce_finish(chip_sum, received, tag):
        own = lax.dynamic_index_in_dim(chip_sum, chip, axis=0, keepdims=False)
        my_rows = _add_received(own, received, "add_chips_" + tag)
        their_rows = _swap_reduced(my_rows, "swap_reduced_" + tag)
        return jnp.concatenate([jnp.where(c == 0, my_rows, their_rows), jnp.where(c == 0, their_rows, my_rows)],
                               axis=0)

    sum_b, sum16_b = reduce_start(n_first, len(_BUF), "late")
    dx0, dg1, dz1, da1, db1, u1, received_b = _ffn_bwd(dx1, xs, g1, a1, b1, f1w1, f1w3, f1w2, "ffn1_bwd",
                                                       scatter=sum16_b)
    grads["ffn1_w1"] = _tn_matmul(da1, h1, "ffn1_dw1")
    grads["ffn1_w3"] = _tn_matmul(db1, h1, "ffn1_dw3")
    grads["ffn1_w2"] = _tn_matmul(u1, dz1, "ffn1_dw2")
    sum_a, sum16_a = reduce_start(0, n_first, "first")
    reduced = jnp.concatenate([reduce_finish(sum_a, _scatter_to_owners(sum16_a, "scatter_to_owners"), "first"),
                               reduce_finish(sum_b, received_b, "late")], axis=0)

    small_rows = [dg1, dgmix, dg3, dgf,
                  jnp.pad(d_sinks.reshape(1, -1), ((0, 0), (0, D - SWA_Q_HEADS))),
                  jnp.pad(d_rel.reshape(1, -1), ((0, 0), (0, D - REL_BUCKETS * SWA_Q_HEADS))),
                  jnp.pad(loss_part, ((0, 0), (0, D - 1))), jnp.zeros((1, D), F32)]
    small = _allreduce_small(jnp.concatenate(small_rows, axis=0))
    loss = small[6, 0]

    G, g_rows = {}, {}
    for i, nk in enumerate(_BUF):
        rows = reduced[offs[i]:offs[i + 1]]
        g_rows[nk[0]] = rows.reshape(-1, W[nk[0]].shape[1]) if nk[1] == "tw" else rows
        G[nk[0]] = _from_rows(nk, rows, W[nk[0]].shape[1])[None]
    G["norm_ffn1"], G["norm_mix"], G["norm_ffn2"] = small[0:1], small[1:2], small[2:3]
    G["norm_final"] = small[3]
    G["swa_sinks"] = small[4:5, :SWA_Q_HEADS]
    G["rel_bias"] = small[5, :REL_BUCKETS * SWA_Q_HEADS].reshape(REL_BUCKETS, SWA_Q_HEADS)

    delta, new_m, new_v = {}, {}, {}
    small_names = ["norm_ffn1", "norm_mix", "norm_ffn2", "norm_final", "swa_sinks", "rel_bias"]

    def pack(d):
        return jnp.concatenate([jnp.pad(d[n].reshape(1, -1), ((0, 0), (0, D - d[n].size))) for n in small_names]
                               + [jnp.zeros((2, D), F32)], axis=0)

    sd, sm, sv = _adamw(pack(W), pack(G), pack(M), pack(V), "adamw_small")
    for r, n in enumerate(small_names):
        for dst, src in ((delta, sd), (new_m, sm), (new_v, sv)):
            dst[n] = src[r, :W[n].size].reshape(W[n].shape)
    for n, kind in _BUF:
        turn = (lambda a: a) if kind == "n" else (lambda a: a.T)
        d_, m_, v_ = _adamw(turn(W[n][0]), g_rows[n], turn(M[n][0]), turn(V[n][0]), "adamw_" + n)
        delta[n], new_m[n], new_v[n] = turn(d_)[None], turn(m_)[None], turn(v_)[None]

    return (loss, dx0[None], *[G[n] for n in names], *[delta[n] for n in names],
            *[new_m[n] for n in names], *[new_v[n] for n in names])
```
